```python
import math, functools
import jax, jax.numpy as jnp
from jax import lax
import numpy as np

D_MODEL = 1024
BATCH = 4
SEQ = 4096
DEPTH = 2
DEC_BATCH = 128
DEC_SEQ = 4
PAST_LEN = 8192
PAGE_SIZE = 128

D_MIX = D_MODEL
D_CONV = D_MIX // 4
CONV_WIDTH = 31
H_HGRN = 4
DK_HGRN = 128
D_HGRN = D_MIX // 2
DV_HGRN = D_HGRN // H_HGRN
HGRN_CHUNK = 64
HEAD_DIM = 64
D_ATTN = D_MIX - D_CONV - D_HGRN
H_ATTN = D_ATTN // HEAD_DIM
KV_HEADS = 2
GROUP = H_ATTN // KV_HEADS
WINDOW = 128
ATTN_BLOCK = 128
NUM_BUCKETS = 32
MAX_DISTANCE = 128
D_FF = 4 * D_MODEL
N_MOD = 6
EPS = 1e-6
IN_SIZES = (D_CONV, D_CONV, H_HGRN * DK_HGRN, H_HGRN * DK_HGRN, D_HGRN, D_HGRN,
            D_ATTN, KV_HEADS * HEAD_DIM, KV_HEADS * HEAD_DIM)
IN_WIDTH = sum(IN_SIZES)

kernel_name = "hymba_conv_hgrn2_swa_adaln_step"


def rmsnorm(x, g):
    xf = x.astype(jnp.float32)
    y = xf * lax.rsqrt(jnp.mean(xf * xf, axis=-1, keepdims=True) + EPS)
    return (y * g.astype(jnp.float32)).astype(x.dtype)


def layernorm(x, g, b):
    xf = x.astype(jnp.float32)
    mu = jnp.mean(xf, axis=-1, keepdims=True)
    xc = xf - mu
    y = xc * lax.rsqrt(jnp.mean(xc * xc, axis=-1, keepdims=True) + EPS)
    return (y * g.astype(jnp.float32) + b.astype(jnp.float32)).astype(x.dtype)


def causal_dwconv(a_prev, a, w, b):
    full = jnp.concatenate([a_prev, a], axis=1)
    out = lax.conv_general_dilated(full, w[:, None, :].astype(full.dtype), window_strides=(1,), padding='VALID',
                                   dimension_numbers=('NWC', 'WIO', 'NWC'), feature_group_count=a.shape[-1])
    return out + b, full[:, -(CONV_WIDTH - 1):]


def hgrn2_chunked(q, log_f, k, v, S0):
    B, T = q.shape[:2]
    L = min(HGRN_CHUNK, T)
    n = -(-T // L)
    pad = n * L - T

    def prep(t):
        t = jnp.pad(t, ((0, 0), (0, pad), (0, 0), (0, 0)))
        return t.reshape(B, n, L, t.shape[2], t.shape[3]).transpose(1, 0, 3, 2, 4)

    tri = jnp.tril(jnp.ones((L, L), dtype=bool))

    def step(S, inp):
        qc, lfc, kc, vc = inp
        bcum = jnp.cumsum(lfc, axis=2)
        diff = bcum[:, :, :, None, :] - bcum[:, :, None, :, :]
        decay = jnp.exp(jnp.where(tri[:, :, None], diff, -jnp.inf))
        A = jnp.einsum('bhtd,bhsd,bhtsd->bhts', qc, kc, decay)
        o = jnp.einsum('bhts,bhsv->bhtv', A, vc) + jnp.einsum('bhtd,bhdv->bhtv', qc * jnp.exp(bcum), S)
        bL = bcum[:, :, -1]
        S_new = jnp.exp(bL)[..., None] * S + jnp.einsum('bhsd,bhsv->bhdv', kc * jnp.exp(bL[:, :, None] - bcum), vc)
        return S_new, o

    S_fin, o = lax.scan(step, S0, (prep(q), prep(log_f), prep(k), prep(v)))
    o = o.transpose(1, 0, 3, 2, 4).reshape(B, n * L, q.shape[2], v.shape[3])[:, :T]
    return o, S_fin


def t5_bucket(rel):
    n = jnp.maximum(rel, 0)
    max_exact = NUM_BUCKETS // 2
    nf = jnp.maximum(n, max_exact).astype(jnp.float32)
    large = max_exact + (jnp.log(nf / max_exact) / math.log(MAX_DISTANCE / max_exact)
                         * (NUM_BUCKETS - max_exact)).astype(jnp.int32)
    large = jnp.minimum(large, NUM_BUCKETS - 1)
    return jnp.where(n < max_exact, n, large)


def band_bias(q_pos, k_pos, table):
    rel = q_pos[..., :, None] - k_pos[..., None, :]
    mask = (rel >= 0) & (rel <= WINDOW) & (k_pos[..., None, :] >= 0)
    bias = jnp.moveaxis(table.astype(jnp.float32)[t5_bucket(rel)], -1, -3)
    return jnp.where(mask[..., None, :, :], bias, -jnp.inf)


def sink_attention(q, k, v, bias, sinks):
    qg = q.reshape(q.shape[:-2] + (KV_HEADS, GROUP, HEAD_DIM))
    bias = bias.reshape(bias.shape[:-3] + (KV_HEADS, GROUP) + bias.shape[-2:])
    s = jnp.einsum('...qkgd,...skd->...kgqs', qg, k).astype(jnp.float32) * (HEAD_DIM ** -0.5) + bias
    sink = sinks.astype(jnp.float32).reshape(KV_HEADS, GROUP)[:, :, None, None]
    m = jnp.maximum(jnp.max(s, axis=-1, keepdims=True), sink)
    p = jnp.exp(s - m)
    p = p / (jnp.sum(p, axis=-1, keepdims=True) + jnp.exp(sink - m))
    o = jnp.einsum('...kgqs,...skd->...qkgd', p.astype(v.dtype), v)
    return o.reshape(q.shape[:-2] + (H_ATTN * HEAD_DIM,))


def swa_prompt(q, k, v, sinks, bias):
    B, T = q.shape[:2]
    nb = T // ATTN_BLOCK
    qb = q.reshape(B, nb, ATTN_BLOCK, H_ATTN, HEAD_DIM)
    kb = k.reshape(B, nb, ATTN_BLOCK, KV_HEADS, HEAD_DIM)
    vb = v.reshape(B, nb, ATTN_BLOCK, KV_HEADS, HEAD_DIM)
    kk = jnp.concatenate([jnp.concatenate([jnp.zeros_like(kb[:, :1]), kb[:, :-1]], axis=1), kb], axis=2)
    vv = jnp.concatenate([jnp.concatenate([jnp.zeros_like(vb[:, :1]), vb[:, :-1]], axis=1), vb], axis=2)
    o = sink_attention(qb, kk, vv, bias, sinks).reshape(B, T, D_ATTN)
    keep = min(WINDOW, T)
    return o, k[:, -keep:], v[:, -keep:]


def swa_sample(q, k, v, sinks, bias, k_buf, v_buf):
    kk = jnp.concatenate([k_buf, k], axis=1)
    vv = jnp.concatenate([v_buf, v], axis=1)
    o = sink_attention(q, kk, vv, bias, sinks)
    keep = k_buf.shape[1]
    return o, kk[:, -keep:], vv[:, -keep:]


def layer(x, c, conv_prev, S0, attn_fn, lb, w_ada, b_ada, norm_mix_g, w_in, conv_w, conv_b, conv_ln_g, conv_ln_b,
          hgrn_norm_g, sinks, w_out, norm_mlp_g, w_up, w_down):
    B, T = x.shape[:2]
    mod = jax.nn.silu(c) @ w_ada + b_ada
    sh1, sc1, g1, sh2, sc2, g2 = jnp.split(mod[:, None, :], N_MOD, axis=-1)
    h = rmsnorm(x, norm_mix_g) * (1 + sc1) + sh1
    proj = h @ w_in
    a_val, a_gate, q_h, f_h, i_h, g_h, q_a, k_a, v_a = jnp.split(proj, np.cumsum(IN_SIZES)[:-1].tolist(), axis=-1)
    a = a_val * jax.nn.sigmoid(a_gate)
    a_conv, conv_new = causal_dwconv(conv_prev, a, conv_w, conv_b)
    out_a = jax.nn.silu(layernorm(a_conv, conv_ln_g, conv_ln_b))
    lbh = lb.reshape(H_HGRN, DK_HGRN)
    f = lbh + (1 - lbh) * jax.nn.sigmoid(f_h.reshape(B, T, H_HGRN, DK_HGRN).astype(jnp.float32))
    o_b, S_new = hgrn2_chunked(q_h.reshape(B, T, H_HGRN, DK_HGRN).astype(jnp.float32), jnp.log(f), 1 - f,
                               i_h.reshape(B, T, H_HGRN, DV_HGRN).astype(jnp.float32), S0.astype(jnp.float32))
    out_b = (rmsnorm(o_b, hgrn_norm_g) * jax.nn.silu(g_h.reshape(B, T, H_HGRN, DV_HGRN).astype(jnp.float32)))
    out_b = out_b.reshape(B, T, D_HGRN).astype(x.dtype)
    out_c, k_new, v_new = attn_fn(q_a.reshape(B, T, H_ATTN, HEAD_DIM), k_a.reshape(B, T, KV_HEADS, HEAD_DIM),
                                  v_a.reshape(B, T, KV_HEADS, HEAD_DIM), sinks)
    mix = jnp.concatenate([out_a, out_b, out_c.astype(x.dtype)], axis=-1) @ w_out
    x = x + g1 * mix
    h2 = rmsnorm(x, norm_mlp_g) * (1 + sc2) + sh2
    x = x + g2 * (jnp.square(jax.nn.relu(h2 @ w_up)) @ w_down)
    return x, conv_new, S_new.astype(S0.dtype), k_new, v_new


def setup_inputs(seed: int = 0) -> dict:
    key = jax.random.key(seed)
    ks = jax.random.split(key, 26)
    f32 = jnp.float32

    def nrm(k, shape, scale=1.0):
        return scale * jax.random.normal(k, shape, f32)

    w_buf = min(WINDOW, PAST_LEN)
    return {
        "x_prompt": nrm(ks[0], (BATCH, SEQ, D_MODEL)),
        "x_sample": nrm(ks[1], (DEC_BATCH, DEC_SEQ, D_MODEL)),
        "cache_conv": nrm(ks[2], (DEPTH, DEC_BATCH, CONV_WIDTH - 1, D_CONV), 0.5),
        "state_hgrn": nrm(ks[3], (DEPTH, DEC_BATCH, H_HGRN, DK_HGRN, DV_HGRN), 0.5),
        "cache_swa_k": nrm(ks[4], (DEPTH, DEC_BATCH, w_buf, KV_HEADS, HEAD_DIM)),
        "cache_swa_v": nrm(ks[5], (DEPTH, DEC_BATCH, w_buf, KV_HEADS, HEAD_DIM)),
        "c_prompt": nrm(ks[6], (BATCH, D_MODEL)),
        "c_sample": nrm(ks[7], (DEC_BATCH, D_MODEL)),
        "rel_bias": nrm(ks[8], (NUM_BUCKETS, H_ATTN), 0.5),
        "w_ada": nrm(ks[9], (DEPTH, D_MODEL, N_MOD * D_MODEL), 0.5 * D_MODEL ** -0.5),
        "b_ada": nrm(ks[10], (DEPTH, N_MOD * D_MODEL), 0.02),
        "norm_mix_g": 1 + nrm(ks[11], (DEPTH, D_MODEL), 0.05),
        "w_in": nrm(ks[12], (DEPTH, D_MODEL, IN_WIDTH), D_MODEL ** -0.5),
        "conv_w": nrm(ks[13], (DEPTH, CONV_WIDTH, D_CONV), CONV_WIDTH ** -0.5),
        "conv_b": nrm(ks[14], (DEPTH, D_CONV), 0.02),
        "conv_ln_g": 1 + nrm(ks[15], (DEPTH, D_CONV), 0.05),
        "conv_ln_b": nrm(ks[16], (DEPTH, D_CONV), 0.02),
        "hgrn_lb": nrm(ks[17], (DEPTH, H_HGRN * DK_HGRN), 0.5),
        "hgrn_norm_g": 1 + nrm(ks[18], (DEPTH, DV_HGRN), 0.05),
        "attn_sinks": nrm(ks[19], (DEPTH, H_ATTN), 0.5),
        "w_out": nrm(ks[20], (DEPTH, D_MIX, D_MODEL), D_MIX ** -0.5),
        "norm_mlp_g": 1 + nrm(ks[21], (DEPTH, D_MODEL), 0.05),
        "w_up": nrm(ks[22], (DEPTH, D_MODEL, D_FF), D_MODEL ** -0.5),
        "w_down": nrm(ks[23], (DEPTH, D_FF, D_MODEL), D_FF ** -0.5),
        "final_g": 1 + nrm(ks[24], (D_MODEL,), 0.05),
    }


def reference(x_prompt, x_sample, cache_conv, state_hgrn, cache_swa_k, cache_swa_v, c_prompt, c_sample, rel_bias,
              w_ada, b_ada, norm_mix_g, w_in, conv_w, conv_b, conv_ln_g, conv_ln_b, hgrn_lb, hgrn_norm_g,
              attn_sinks, w_out, norm_mlp_g, w_up, w_down, final_g):
    Bp, Tp = x_prompt.shape[:2]
    Bs, Ts = x_sample.shape[:2]
    w_buf = cache_swa_k.shape[2]
    lbs = jnp.cumsum(jax.nn.softmax(hgrn_lb.astype(jnp.float32), axis=0), axis=0)
    lbs = lbs - lbs[0]
    nb = Tp // ATTN_BLOCK
    qpos_p = jnp.arange(Tp, dtype=jnp.int32).reshape(nb, ATTN_BLOCK)
    kpos_p = (jnp.arange(nb, dtype=jnp.int32) * ATTN_BLOCK)[:, None] - ATTN_BLOCK + jnp.arange(2 * ATTN_BLOCK, dtype=jnp.int32)[None]
    bias_p = band_bias(qpos_p, kpos_p, rel_bias)
    qpos_s = PAST_LEN + jnp.arange(Ts, dtype=jnp.int32)
    kpos_s = PAST_LEN - w_buf + jnp.arange(w_buf + Ts, dtype=jnp.int32)
    bias_s = band_bias(qpos_s, kpos_s, rel_bias)

    xp, xs = x_prompt, x_sample
    conv_p, conv_s, hg_p, hg_s, kp_l, ks_l, vp_l, vs_l = [], [], [], [], [], [], [], []
    for l in range(DEPTH):
        params = (w_ada[l], b_ada[l], norm_mix_g[l], w_in[l], conv_w[l], conv_b[l], conv_ln_g[l], conv_ln_b[l],
                  hgrn_norm_g[l], attn_sinks[l], w_out[l], norm_mlp_g[l], w_up[l], w_down[l])
        xp, cp, sp, kp, vp = layer(xp, c_prompt, jnp.zeros((Bp, CONV_WIDTH - 1, D_CONV), xp.dtype),
                                   jnp.zeros((Bp, H_HGRN, DK_HGRN, DV_HGRN), state_hgrn.dtype),
                                   functools.partial(swa_prompt, bias=bias_p), lbs[l], *params)
        xs, cs, ss, ksn, vsn = layer(xs, c_sample, cache_conv[l], state_hgrn[l],
                                     functools.partial(swa_sample, bias=bias_s, k_buf=cache_swa_k[l], v_buf=cache_swa_v[l]),
                                     lbs[l], *params)
        conv_p.append(cp); conv_s.append(cs); hg_p.append(sp); hg_s.append(ss)
        kp_l.append(kp); ks_l.append(ksn); vp_l.append(vp); vs_l.append(vsn)
    y_prompt = rmsnorm(xp, final_g)
    y_sample = rmsnorm(xs, final_g)
    return (y_prompt, y_sample, jnp.stack(conv_p), jnp.stack(conv_s), jnp.stack(hg_p), jnp.stack(hg_s),
            jnp.stack(kp_l), jnp.stack(ks_l), jnp.stack(vp_l), jnp.stack(vs_l))
```

```python
import functools
import math

import jax
import jax.numpy as jnp
from jax import lax
from jax.experimental import pallas as pl
from jax.experimental.pallas import tpu as pltpu

F32 = jnp.float32
BF16 = jnp.bfloat16

D_MODEL = 1024
D_CONV = 256
CONV_WIDTH = 31
H_HGRN = 4
DK_HGRN = 128
DV_HGRN = 128
D_HGRN = 512
HEAD_DIM = 64
H_ATTN = 4
KV_HEADS = 2
GROUP = H_ATTN // KV_HEADS
D_ATTN = H_ATTN * HEAD_DIM
D_KV = KV_HEADS * HEAD_DIM
WINDOW = 128
ATTN_BLOCK = 128
NUM_BUCKETS = 32
MAX_DISTANCE = 128
D_FF = 4 * D_MODEL
N_MOD = 6
EPS = 1e-6

OFF_AVAL = 0
OFF_AGATE = OFF_AVAL + D_CONV
OFF_Q = OFF_AGATE + D_CONV
OFF_F = OFF_Q + H_HGRN * DK_HGRN
OFF_I = OFF_F + H_HGRN * DK_HGRN
OFF_G = OFF_I + D_HGRN
OFF_QA = OFF_G + D_HGRN
OFF_KA = OFF_QA + D_ATTN
OFF_VA = OFF_KA + D_KV
IN_WIDTH = OFF_VA + D_KV

HGRN_CHUNK = 64
HGRN_SUB = 16
CONV_PAD = 32
MIX_TILE = 256
TOK_TILE = 512
SAMPLE_BLOCK = 8
FF_CHUNK = 1024
VMEM_LIMIT = 56 * 1024 * 1024

NT_DIMS = (((1,), (1,)), ((), ()))
TN_DIMS = (((0,), (0,)), ((), ()))


def _silu(x):
    return x * jax.nn.sigmoid(x)


def _rms_rows(x):
    return x * lax.rsqrt(jnp.mean(x * x, axis=-1, keepdims=True) + EPS)


def _layer_lb(hlb, layer):
    m = jnp.max(hlb, axis=0, keepdims=True)
    e = jnp.exp(hlb - m)
    p = e / jnp.sum(e, axis=0, keepdims=True)
    lb = jnp.zeros_like(m)
    for i in range(1, layer + 1):
        lb = lb + p[i:i + 1, :]
    return lb


def _cumsum_rows_mxu(g, tri):
    g1 = g.astype(BF16)
    r1 = g - g1.astype(F32)
    g2 = r1.astype(BF16)
    g3 = (r1 - g2.astype(F32)).astype(BF16)
    return (jnp.dot(tri, g1, preferred_element_type=F32) + jnp.dot(tri, g2, preferred_element_type=F32)
            + jnp.dot(tri, g3, preferred_element_type=F32))


def _cumsum_rows_small(g):
    row = lax.broadcasted_iota(jnp.int32, g.shape, 0)
    b = jnp.zeros_like(g)
    for u in range(g.shape[0]):
        b = b + jnp.where(row >= u, g[u:u + 1, :], 0.0)
    return b


def _hgrn_chunk(q, k, v, b, st, sub):
    L = q.shape[0]
    nsb = L // sub
    o_blocks = [None] * nsb
    for j in range(nsb):
        lo = j * sub
        r = b[lo + sub - 1:lo + sub, :]
        kk = (k[lo:lo + sub] * jnp.exp(r - b[lo:lo + sub])).astype(BF16)
        ql = (q[lo:] * jnp.exp(b[lo:] - r)).astype(BF16)
        p = lax.dot_general(ql, kk, NT_DIMS, preferred_element_type=F32)
        row = lax.broadcasted_iota(jnp.int32, p.shape, 0)
        col = lax.broadcasted_iota(jnp.int32, p.shape, 1)
        p = jnp.where(row >= col, p, 0.0).astype(BF16)
        c = jnp.dot(p, v[lo:lo + sub].astype(BF16), preferred_element_type=F32)
        for i in range(j, nsb):
            piece = c[(i - j) * sub:(i - j + 1) * sub]
            o_blocks[i] = piece if o_blocks[i] is None else o_blocks[i] + piece
    o = o_blocks[0] if nsb == 1 else jnp.concatenate(o_blocks, axis=0)
    bl = b[L - 1:L, :]
    qt = (q * jnp.exp(b)).astype(BF16)
    o = o + lax.dot_general(qt, st.astype(BF16), NT_DIMS, preferred_element_type=F32)
    kst = (k * jnp.exp(bl - b)).astype(BF16)
    st_new = jnp.exp(bl) * st + lax.dot_general(v.astype(BF16), kst, TN_DIMS, preferred_element_type=F32)
    return o, st_new


def _sink_softmax_parts(scores, sink):
    m = None
    for s in scores:
        ms = jnp.max(s, axis=-1, keepdims=True)
        m = ms if m is None else jnp.maximum(m, ms)
    m = jnp.maximum(m, sink)
    ps = [jnp.exp(s - m) for s in scores]
    den = jnp.exp(sink - m)
    for p in ps:
        den = den + jnp.sum(p, axis=-1, keepdims=True)
    return ps, den


def _bias_kernel(tab_ref, bp_ref, bs_ref, op_ref, os_ref):
    for bref, oref in ((bp_ref, op_ref), (bs_ref, os_ref)):
        bk = bref[...]
        for h in range(H_ATTN):
            acc = jnp.full(bk.shape, -jnp.inf, F32)
            for bkt in range(NUM_BUCKETS):
                acc = jnp.where(bk == bkt, tab_ref[bkt, h], acc)
            oref[h] = acc


def _t5_bucket(rel):
    n = jnp.maximum(rel, 0)
    max_exact = NUM_BUCKETS // 2
    nf = jnp.maximum(n, max_exact).astype(F32)
    large = max_exact + (jnp.log(nf / max_exact) / math.log(MAX_DISTANCE / max_exact)
                         * (NUM_BUCKETS - max_exact)).astype(jnp.int32)
    large = jnp.minimum(large, NUM_BUCKETS - 1)
    return jnp.where(n < max_exact, n, large)


def _bias_tables(rel_bias, dec_seq, w_buf):
    qi = jnp.arange(ATTN_BLOCK, dtype=jnp.int32)[:, None]
    kc = jnp.arange(2 * ATTN_BLOCK, dtype=jnp.int32)[None, :]
    rel_p = qi + ATTN_BLOCK - kc
    bucket_p = jnp.where((rel_p >= 0) & (rel_p <= WINDOW), _t5_bucket(rel_p), -1)
    ts = jnp.arange(8, dtype=jnp.int32)[:, None]
    js = jnp.arange(2 * ATTN_BLOCK, dtype=jnp.int32)[None, :]
    rel_s = w_buf + ts - js
    ok_s = (rel_s >= 0) & (rel_s <= WINDOW) & (ts < dec_seq) & (js < w_buf + dec_seq)
    bucket_s = jnp.where(ok_s, _t5_bucket(rel_s), -1)
    return pl.pallas_call(
        _bias_kernel,
        out_shape=(jax.ShapeDtypeStruct((H_ATTN, ATTN_BLOCK, 2 * ATTN_BLOCK), F32),
                   jax.ShapeDtypeStruct((H_ATTN, 8, 2 * ATTN_BLOCK), F32)),
        in_specs=[pl.BlockSpec(memory_space=pltpu.SMEM),
                  pl.BlockSpec(memory_space=pltpu.VMEM),
                  pl.BlockSpec(memory_space=pltpu.VMEM)],
        out_specs=(pl.BlockSpec(memory_space=pltpu.VMEM), pl.BlockSpec(memory_space=pltpu.VMEM)),
        name="rel_bias_tables",
    )(rel_bias.astype(F32), bucket_p, bucket_s)


def _mod_kernel(c_ref, w_ref, b_ref, o_ref):
    s = _silu(c_ref[...]).astype(BF16)
    o_ref[...] = jnp.dot(s, w_ref[...], preferred_element_type=F32) + b_ref[...]


def _modulation(c_all, w_ada, b_ada):
    depth = w_ada.shape[0]
    n = c_all.shape[0]
    return pl.pallas_call(
        _mod_kernel,
        out_shape=jax.ShapeDtypeStruct((depth, n, N_MOD * D_MODEL), F32),
        grid=(depth, N_MOD),
        in_specs=[pl.BlockSpec((n, D_MODEL), lambda l, j: (0, 0)),
                  pl.BlockSpec((None, D_MODEL, D_MODEL), lambda l, j: (l, 0, j)),
                  pl.BlockSpec((None, 1, D_MODEL), lambda l, j: (l, 0, j))],
        out_specs=pl.BlockSpec((None, n, D_MODEL), lambda l, j: (l, 0, j)),
        compiler_params=pltpu.CompilerParams(dimension_semantics=("arbitrary", "arbitrary"),
                                             vmem_limit_bytes=VMEM_LIMIT),
        name="adaln_modulation",
    )(c_all, w_ada, b_ada.reshape(depth, 1, N_MOD * D_MODEL))


def _mod_spec(mod, chunk, tile, tiles_per_batch):
    if mod.ndim == 3:
        return pl.BlockSpec((None, 1, D_MODEL), lambda i: (i // tiles_per_batch, 0, chunk))
    return pl.BlockSpec((tile, D_MODEL), lambda i: (i, chunk))


def _inproj_kernel(x_ref, sh_ref, sc_ref, g_ref, w_ref, o_ref):
    h = _rms_rows(x_ref[...]) * g_ref[...] * (1.0 + sc_ref[...]) + sh_ref[...]
    o_ref[...] = jnp.dot(h.astype(BF16), w_ref[...], preferred_element_type=F32)


def _inproj(x2, mod, norm_g, w_in, tile, tiles_per_batch):
    n = x2.shape[0]
    return pl.pallas_call(
        _inproj_kernel,
        out_shape=jax.ShapeDtypeStruct((n, IN_WIDTH), F32),
        grid=(n // tile,),
        in_specs=[pl.BlockSpec((tile, D_MODEL), lambda i: (i, 0)),
                  _mod_spec(mod, 0, tile, tiles_per_batch),
                  _mod_spec(mod, 1, tile, tiles_per_batch),
                  pl.BlockSpec((1, D_MODEL), lambda i: (0, 0)),
                  pl.BlockSpec((D_MODEL, IN_WIDTH), lambda i: (0, 0), pipeline_mode=pl.Buffered(1))],
        out_specs=pl.BlockSpec((tile, IN_WIDTH), lambda i: (i, 0)),
        compiler_params=pltpu.CompilerParams(dimension_semantics=("arbitrary",), vmem_limit_bytes=VMEM_LIMIT),
        name="in_projection",
    )(x2, mod, mod, norm_g.reshape(1, D_MODEL), w_in)


def _mlp_kernel(mix_ref, x_ref, g1_ref, sh_ref, sc_ref, g2_ref, ng_ref, wout_ref, wup_ref, wdn_ref, fg_ref, o_ref, *,
                final):
    x1 = x_ref[...] + g1_ref[...] * jnp.dot(mix_ref[...].astype(BF16), wout_ref[...], preferred_element_type=F32)
    h = (_rms_rows(x1) * ng_ref[...] * (1.0 + sc_ref[...]) + sh_ref[...]).astype(BF16)
    acc = None
    for c in range(D_FF // FF_CHUNK):
        u = jnp.dot(h, wup_ref[:, c * FF_CHUNK:(c + 1) * FF_CHUNK], preferred_element_type=F32)
        u = jnp.square(jnp.maximum(u, 0.0)).astype(BF16)
        d = jnp.dot(u, wdn_ref[c * FF_CHUNK:(c + 1) * FF_CHUNK, :], preferred_element_type=F32)
        acc = d if acc is None else acc + d
    x2 = x1 + g2_ref[...] * acc
    if final:
        x2 = _rms_rows(x2) * fg_ref[...]
    o_ref[...] = x2


def _out_mlp(mix2, x2, mod, norm_g, w_out, w_up, w_down, final_g, tile, tiles_per_batch, final):
    n = x2.shape[0]
    const = lambda i: (0, 0)
    return pl.pallas_call(
        functools.partial(_mlp_kernel, final=final),
        out_shape=jax.ShapeDtypeStruct((n, D_MODEL), F32),
        grid=(n // tile,),
        in_specs=[pl.BlockSpec((tile, D_MODEL), lambda i: (i, 0)),
                  pl.BlockSpec((tile, D_MODEL), lambda i: (i, 0)),
                  _mod_spec(mod, 2, tile, tiles_per_batch),
                  _mod_spec(mod, 3, tile, tiles_per_batch),
                  _mod_spec(mod, 4, tile, tiles_per_batch),
                  _mod_spec(mod, 5, tile, tiles_per_batch),
                  pl.BlockSpec((1, D_MODEL), const),
                  pl.BlockSpec((D_MODEL, D_MODEL), const, pipeline_mode=pl.Buffered(1)),
                  pl.BlockSpec((D_MODEL, D_FF), const, pipeline_mode=pl.Buffered(1)),
                  pl.BlockSpec((D_FF, D_MODEL), const, pipeline_mode=pl.Buffered(1)),
                  pl.BlockSpec((1, D_MODEL), const)],
        out_specs=pl.BlockSpec((tile, D_MODEL), lambda i: (i, 0)),
        compiler_params=pltpu.CompilerParams(dimension_semantics=("arbitrary",), vmem_limit_bytes=VMEM_LIMIT),
        name="out_projection_mlp",
    )(mix2, x2, mod, mod, mod, mod, norm_g.reshape(1, D_MODEL), w_out, w_up, w_down, final_g.reshape(1, D_MODEL))


def _conv_ln_swish(acc, lng, lnb):
    mu = jnp.mean(acc, axis=-1, keepdims=True)
    xc = acc - mu
    y = xc * lax.rsqrt(jnp.mean(xc * xc, axis=-1, keepdims=True) + EPS) * lng + lnb
    return _silu(y)


def _hgrn_gates(fh, lb):
    f = lb + (1.0 - lb) * jax.nn.sigmoid(fh)
    return jnp.log(f), 1.0 - f


def _hgrn_out(o, hng, gate):
    return _rms_rows(o) * hng * _silu(gate)


def _prompt_mix_kernel(sinks_ref, proj_ref, convw_ref, convb_ref, lng_ref, lnb_ref, hlb_ref, hng_ref, bias_ref,
                       mix_ref, convo_ref, so_ref, ko_ref, vo_ref,
                       abuf, kbuf, vbuf, st_ref, *, layer, tile):
    t = pl.program_id(1)
    last = pl.num_programs(1) - 1

    @pl.when(t == 0)
    def _():
        abuf[0:CONV_PAD, :] = jnp.zeros((CONV_PAD, D_CONV), F32)
        kbuf[0:ATTN_BLOCK, :] = jnp.zeros((ATTN_BLOCK, D_KV), BF16)
        vbuf[0:ATTN_BLOCK, :] = jnp.zeros((ATTN_BLOCK, D_KV), BF16)
        st_ref[...] = jnp.zeros(st_ref.shape, F32)

    abuf[CONV_PAD:CONV_PAD + tile, :] = (proj_ref[:, OFF_AVAL:OFF_AVAL + D_CONV]
                                         * jax.nn.sigmoid(proj_ref[:, OFF_AGATE:OFF_AGATE + D_CONV]))
    first_row = CONV_PAD - (CONV_WIDTH - 1)
    acc = jnp.broadcast_to(convb_ref[...], (tile, D_CONV))
    for j in range(CONV_WIDTH):
        acc = acc + convw_ref[j:j + 1, :] * abuf[first_row + j:first_row + j + tile, :]
    mix_ref[:, 0:D_CONV] = _conv_ln_swish(acc, lng_ref[...], lnb_ref[...]).astype(BF16)

    @pl.when(t == last)
    def _():
        convo_ref[...] = abuf[CONV_PAD + tile - (CONV_WIDTH - 1):CONV_PAD + tile, :]

    abuf[0:CONV_PAD, :] = abuf[tile:tile + CONV_PAD, :]

    lb = _layer_lb(hlb_ref[...], layer)
    hng = hng_ref[...]
    ri = lax.broadcasted_iota(jnp.int32, (HGRN_CHUNK, HGRN_CHUNK), 0)
    ci = lax.broadcasted_iota(jnp.int32, (HGRN_CHUNK, HGRN_CHUNK), 1)
    tri = jnp.where(ri >= ci, 1.0, 0.0).astype(BF16)

    def chunk_body(c, carry):
        rows = pl.ds(pl.multiple_of(c * HGRN_CHUNK, HGRN_CHUNK), HGRN_CHUNK)
        g, k = _hgrn_gates(proj_ref[rows, OFF_F:OFF_F + D_HGRN], lb)
        b = _cumsum_rows_mxu(g, tri)
        for h in range(H_HGRN):
            cs = slice(h * DK_HGRN, (h + 1) * DK_HGRN)
            q = proj_ref[rows, OFF_Q + h * DK_HGRN:OFF_Q + (h + 1) * DK_HGRN]
            v = proj_ref[rows, OFF_I + h * DV_HGRN:OFF_I + (h + 1) * DV_HGRN]
            o, st_new = _hgrn_chunk(q, k[:, cs], v, b[:, cs], st_ref[h], HGRN_SUB)
            st_ref[h] = st_new
            gate = proj_ref[rows, OFF_G + h * DV_HGRN:OFF_G + (h + 1) * DV_HGRN]
            mix_ref[rows, D_CONV + h * DV_HGRN:D_CONV + (h + 1) * DV_HGRN] = _hgrn_out(o, hng, gate).astype(BF16)
        return carry

    lax.fori_loop(0, tile // HGRN_CHUNK, chunk_body, 0)

    @pl.when(t == last)
    def _():
        for h in range(H_HGRN):
            so_ref[h] = st_ref[h].T

    kbuf[ATTN_BLOCK:ATTN_BLOCK + tile, :] = proj_ref[:, OFF_KA:OFF_KA + D_KV].astype(BF16)
    vbuf[ATTN_BLOCK:ATTN_BLOCK + tile, :] = proj_ref[:, OFF_VA:OFF_VA + D_KV].astype(BF16)
    scale = HEAD_DIM ** -0.5
    for blk in range(tile // ATTN_BLOCK):
        r0 = blk * ATTN_BLOCK
        heads = []
        for h in range(H_ATTN):
            kv = h // GROUP
            q = (proj_ref[r0:r0 + ATTN_BLOCK, OFF_QA + h * HEAD_DIM:OFF_QA + (h + 1) * HEAD_DIM] * scale).astype(BF16)
            kall = kbuf[r0:r0 + 2 * ATTN_BLOCK, kv * HEAD_DIM:(kv + 1) * HEAD_DIM]
            vall = vbuf[r0:r0 + 2 * ATTN_BLOCK, kv * HEAD_DIM:(kv + 1) * HEAD_DIM]
            s = lax.dot_general(q, kall, NT_DIMS, preferred_element_type=F32) + bias_ref[h]
            if blk == 0:
                col = lax.broadcasted_iota(jnp.int32, s.shape, 1)
                s = jnp.where(col + (t * tile - ATTN_BLOCK) >= 0, s, -jnp.inf)
            (p,), den = _sink_softmax_parts([s], sinks_ref[h])
            heads.append(jnp.dot(p.astype(BF16), vall, preferred_element_type=F32) / den)
        mix_ref[r0:r0 + ATTN_BLOCK, D_CONV + D_HGRN:D_MODEL] = jnp.concatenate(heads, axis=1).astype(BF16)

    @pl.when(t == last)
    def _():
        ko_ref[...] = proj_ref[tile - WINDOW:tile, OFF_KA:OFF_KA + D_KV]
        vo_ref[...] = proj_ref[tile - WINDOW:tile, OFF_VA:OFF_VA + D_KV]

    kbuf[0:ATTN_BLOCK, :] = kbuf[tile:tile + ATTN_BLOCK, :]
    vbuf[0:ATTN_BLOCK, :] = vbuf[tile:tile + ATTN_BLOCK, :]


def _prompt_mixers(proj, sinks, conv_w, conv_b, ln_g, ln_b, hgrn_lb, hng, bias_p, layer):
    B, T = proj.shape[:2]
    tile = MIX_TILE
    depth = hgrn_lb.shape[0]
    const2 = lambda b, t: (0, 0)
    return pl.pallas_call(
        functools.partial(_prompt_mix_kernel, layer=layer, tile=tile),
        out_shape=(jax.ShapeDtypeStruct((B, T, D_MODEL), BF16),
                   jax.ShapeDtypeStruct((B, CONV_WIDTH - 1, D_CONV), F32),
                   jax.ShapeDtypeStruct((B, H_HGRN, DK_HGRN, DV_HGRN), F32),
                   jax.ShapeDtypeStruct((B, WINDOW, D_KV), F32),
                   jax.ShapeDtypeStruct((B, WINDOW, D_KV), F32)),
        grid=(B, T // tile),
        in_specs=[pl.BlockSpec(memory_space=pltpu.SMEM),
                  pl.BlockSpec((None, tile, IN_WIDTH), lambda b, t: (b, t, 0)),
                  pl.BlockSpec((CONV_WIDTH, D_CONV), const2),
                  pl.BlockSpec((1, D_CONV), const2),
                  pl.BlockSpec((1, D_CONV), const2),
                  pl.BlockSpec((1, D_CONV), const2),
                  pl.BlockSpec((depth, D_HGRN), const2),
                  pl.BlockSpec((1, DV_HGRN), const2),
                  pl.BlockSpec((H_ATTN, ATTN_BLOCK, 2 * ATTN_BLOCK), lambda b, t: (0, 0, 0))],
        out_specs=(pl.BlockSpec((None, tile, D_MODEL), lambda b, t: (b, t, 0)),
                   pl.BlockSpec((None, CONV_WIDTH - 1, D_CONV), lambda b, t: (b, 0, 0)),
                   pl.BlockSpec((None, H_HGRN, DK_HGRN, DV_HGRN), lambda b, t: (b, 0, 0, 0)),
                   pl.BlockSpec((None, WINDOW, D_KV), lambda b, t: (b, 0, 0)),
                   pl.BlockSpec((None, WINDOW, D_KV), lambda b, t: (b, 0, 0))),
        scratch_shapes=[pltpu.VMEM((CONV_PAD + tile, D_CONV), F32),
                        pltpu.VMEM((ATTN_BLOCK + tile, D_KV), BF16),
                        pltpu.VMEM((ATTN_BLOCK + tile, D_KV), BF16),
                        pltpu.VMEM((H_HGRN, DV_HGRN, DK_HGRN), F32)],
        compiler_params=pltpu.CompilerParams(dimension_semantics=("arbitrary", "arbitrary"),
                                             vmem_limit_bytes=VMEM_LIMIT),
        name="prompt_mixers",
    )(sinks, proj, conv_w, conv_b.reshape(1, D_CONV), ln_g.reshape(1, D_CONV), ln_b.reshape(1, D_CONV),
      hgrn_lb, hng.reshape(1, DV_HGRN), bias_p)


def _sample_mix_kernel(sinks_ref, proj_ref, cconv_ref, state_ref, ck_ref, cv_ref, convw_ref, convb_ref, lng_ref,
                       lnb_ref, hlb_ref, hng_ref, bias_ref,
                       mix_ref, convo_ref, so_ref, ko_ref, vo_ref, full_ref, *, layer, block, seq, w_buf):
    hist = CONV_WIDTH - 1
    lb = _layer_lb(hlb_ref[...], layer)
    hng = hng_ref[...]
    scale = HEAD_DIM ** -0.5

    def body(i, carry):
        p = proj_ref[i]
        full_ref[0:hist, :] = cconv_ref[i]
        full_ref[hist:hist + seq, :] = p[:, OFF_AVAL:OFF_AVAL + D_CONV] * jax.nn.sigmoid(p[:, OFF_AGATE:OFF_AGATE + D_CONV])
        rows = [jnp.sum(convw_ref[...] * full_ref[s:s + CONV_WIDTH, :], axis=0, keepdims=True) for s in range(seq)]
        acc = jnp.concatenate(rows, axis=0) + convb_ref[...]
        convo_ref[i] = full_ref[seq:seq + hist, :]
        parts = [_conv_ln_swish(acc, lng_ref[...], lnb_ref[...])]
        g, k = _hgrn_gates(p[:, OFF_F:OFF_F + D_HGRN], lb)
        b = _cumsum_rows_small(g)
        for h in range(H_HGRN):
            cs = slice(h * DK_HGRN, (h + 1) * DK_HGRN)
            q = p[:, OFF_Q + h * DK_HGRN:OFF_Q + (h + 1) * DK_HGRN]
            v = p[:, OFF_I + h * DV_HGRN:OFF_I + (h + 1) * DV_HGRN]
            o, st_new = _hgrn_chunk(q, k[:, cs], v, b[:, cs], state_ref[i, h].T, seq)
            so_ref[i, h] = st_new.T
            parts.append(_hgrn_out(o, hng, p[:, OFF_G + h * DV_HGRN:OFF_G + (h + 1) * DV_HGRN]))
        knew = p[:, OFF_KA:OFF_KA + D_KV]
        vnew = p[:, OFF_VA:OFF_VA + D_KV]
        kc = ck_ref[i]
        vc = cv_ref[i]
        for h in range(H_ATTN):
            kv = h // GROUP
            hs = slice(kv * HEAD_DIM, (kv + 1) * HEAD_DIM)
            q = (p[:, OFF_QA + h * HEAD_DIM:OFF_QA + (h + 1) * HEAD_DIM] * scale).astype(BF16)
            bias = bias_ref[h]
            s_c = lax.dot_general(q, kc[:, hs].astype(BF16), NT_DIMS, preferred_element_type=F32) + bias[0:seq, 0:w_buf]
            s_n = (lax.dot_general(q, knew[:, hs].astype(BF16), NT_DIMS, preferred_element_type=F32)
                   + bias[0:seq, w_buf:w_buf + seq])
            (p_c, p_n), den = _sink_softmax_parts([s_c, s_n], sinks_ref[h])
            o = (jnp.dot(p_c.astype(BF16), vc[:, hs].astype(BF16), preferred_element_type=F32)
                 + jnp.dot(p_n.astype(BF16), vnew[:, hs].astype(BF16), preferred_element_type=F32))
            parts.append(o / den)
        mix_ref[i] = jnp.concatenate(parts, axis=1)
        ko_ref[i, 0:w_buf - seq, :] = kc[seq:, :]
        ko_ref[i, w_buf - seq:w_buf, :] = knew
        vo_ref[i, 0:w_buf - seq, :] = vc[seq:, :]
        vo_ref[i, w_buf - seq:w_buf, :] = vnew
        return carry

    lax.fori_loop(0, block, body, 0)


def _sample_mixers(proj3, sinks, cache_conv, state, cache_k, cache_v, conv_w, conv_b, ln_g, ln_b, hgrn_lb, hng,
                   bias_s, layer):
    B, seq = proj3.shape[:2]
    w_buf = cache_k.shape[1]
    block = SAMPLE_BLOCK
    depth = hgrn_lb.shape[0]
    hist = CONV_WIDTH - 1
    const2 = lambda i: (0, 0)
    return pl.pallas_call(
        functools.partial(_sample_mix_kernel, layer=layer, block=block, seq=seq, w_buf=w_buf),
        out_shape=(jax.ShapeDtypeStruct((B, seq, D_MODEL), F32),
                   jax.ShapeDtypeStruct((B, hist, D_CONV), F32),
                   jax.ShapeDtypeStruct((B, H_HGRN, DK_HGRN, DV_HGRN), F32),
                   jax.ShapeDtypeStruct((B, w_buf, D_KV), F32),
                   jax.ShapeDtypeStruct((B, w_buf, D_KV), F32)),
        grid=(B // block,),
        in_specs=[pl.BlockSpec(memory_space=pltpu.SMEM),
                  pl.BlockSpec((block, seq, IN_WIDTH), lambda i: (i, 0, 0)),
                  pl.BlockSpec((block, hist, D_CONV), lambda i: (i, 0, 0)),
                  pl.BlockSpec((block, H_HGRN, DK_HGRN, DV_HGRN), lambda i: (i, 0, 0, 0)),
                  pl.BlockSpec((block, w_buf, D_KV), lambda i: (i, 0, 0)),
                  pl.BlockSpec((block, w_buf, D_KV), lambda i: (i, 0, 0)),
                  pl.BlockSpec((CONV_WIDTH, D_CONV), const2),
                  pl.BlockSpec((1, D_CONV), const2),
                  pl.BlockSpec((1, D_CONV), const2),
                  pl.BlockSpec((1, D_CONV), const2),
                  pl.BlockSpec((depth, D_HGRN), const2),
                  pl.BlockSpec((1, DV_HGRN), const2),
                  pl.BlockSpec((H_ATTN, 8, 2 * ATTN_BLOCK), lambda i: (0, 0, 0))],
        out_specs=(pl.BlockSpec((block, seq, D_MODEL), lambda i: (i, 0, 0)),
                   pl.BlockSpec((block, hist, D_CONV), lambda i: (i, 0, 0)),
                   pl.BlockSpec((block, H_HGRN, DK_HGRN, DV_HGRN), lambda i: (i, 0, 0, 0)),
                   pl.BlockSpec((block, w_buf, D_KV), lambda i: (i, 0, 0)),
                   pl.BlockSpec((block, w_buf, D_KV), lambda i: (i, 0, 0))),
        scratch_shapes=[pltpu.VMEM((hist + seq + 6, D_CONV), F32)],
        compiler_params=pltpu.CompilerParams(dimension_semantics=("arbitrary",), vmem_limit_bytes=VMEM_LIMIT),
        name="sample_mixers",
    )(sinks, proj3, cache_conv, state, cache_k, cache_v, conv_w, conv_b.reshape(1, D_CONV), ln_g.reshape(1, D_CONV),
      ln_b.reshape(1, D_CONV), hgrn_lb, hng.reshape(1, DV_HGRN), bias_s)


def kernel(x_prompt, x_sample, cache_conv, state_hgrn, cache_swa_k, cache_swa_v, c_prompt, c_sample, rel_bias, w_ada, b_ada, norm_mix_g, w_in, conv_w, conv_b, conv_ln_g, conv_ln_b, hgrn_lb, hgrn_norm_g, attn_sinks, w_out, norm_mlp_g, w_up, w_down, final_g):
    Bp, Tp = x_prompt.shape[:2]
    Bs, Ts = x_sample.shape[:2]
    depth = w_in.shape[0]
    w_buf = cache_swa_k.shape[2]
    assert Tp % MIX_TILE == 0 and (Bp * Tp) % TOK_TILE == 0 and Tp % TOK_TILE == 0 and Bs % SAMPLE_BLOCK == 0
    assert w_buf == WINDOW and Ts <= 8

    bias_p, bias_s = _bias_tables(rel_bias, Ts, w_buf)
    mod = _modulation(jnp.concatenate([c_prompt, c_sample], axis=0), w_ada.astype(BF16), b_ada)
    w_in_b, w_out_b, w_up_b, w_down_b = (w.astype(BF16) for w in (w_in, w_out, w_up, w_down))
    hlb = hgrn_lb.astype(F32)
    ck = cache_swa_k.reshape(depth, Bs, w_buf, D_KV)
    cv = cache_swa_v.reshape(depth, Bs, w_buf, D_KV)

    xp = x_prompt.reshape(Bp * Tp, D_MODEL)
    xs = x_sample.reshape(Bs * Ts, D_MODEL)
    tile_s = Bs * Ts
    outs = [[] for _ in range(8)]
    for l in range(depth):
        final = l == depth - 1
        mod_p = mod[l, :Bp].reshape(Bp, 1, N_MOD * D_MODEL)
        mod_s = jnp.repeat(mod[l, Bp:], Ts, axis=0)
        proj_p = _inproj(xp, mod_p, norm_mix_g[l], w_in_b[l], TOK_TILE, Tp // TOK_TILE)
        mix_p, cp, sp, kp, vp = _prompt_mixers(proj_p.reshape(Bp, Tp, IN_WIDTH), attn_sinks[l], conv_w[l], conv_b[l],
                                               conv_ln_g[l], conv_ln_b[l], hlb, hgrn_norm_g[l], bias_p, l)
        xp = _out_mlp(mix_p.reshape(Bp * Tp, D_MODEL), xp, mod_p, norm_mlp_g[l], w_out_b[l], w_up_b[l], w_down_b[l],
                      final_g, TOK_TILE, Tp // TOK_TILE, final)
        proj_s = _inproj(xs, mod_s, norm_mix_g[l], w_in_b[l], tile_s, 1)
        mix_s, cs, ss, ksn, vsn = _sample_mixers(proj_s.reshape(Bs, Ts, IN_WIDTH), attn_sinks[l], cache_conv[l],
                                                 state_hgrn[l], ck[l], cv[l], conv_w[l], conv_b[l], conv_ln_g[l],
                                                 conv_ln_b[l], hlb, hgrn_norm_g[l], bias_s, l)
        xs = _out_mlp(mix_s.reshape(Bs * Ts, D_MODEL), xs, mod_s, norm_mlp_g[l], w_out_b[l], w_up_b[l], w_down_b[l],
                      final_g, tile_s, 1, final)
        for lst, val in zip(outs, (cp, cs, sp, ss, kp.reshape(Bp, WINDOW, KV_HEADS, HEAD_DIM),
                                   ksn.reshape(Bs, w_buf, KV_HEADS, HEAD_DIM),
                                   vp.reshape(Bp, WINDOW, KV_HEADS, HEAD_DIM),
                                   vsn.reshape(Bs, w_buf, KV_HEADS, HEAD_DIM))):
            lst.append(val)
    return (xp.reshape(Bp, Tp, D_MODEL), xs.reshape(Bs, Ts, D_MODEL)) + tuple(jnp.stack(o) for o in outs)
```

```python
import functools
import math

import jax
import jax.numpy as jnp
from jax import lax
from jax.experimental import pallas as pl
from jax.experimental.pallas import tpu as pltpu

F32 = jnp.float32
BF16 = jnp.bfloat16

D_MODEL = 1024
D_CONV = 256
CONV_WIDTH = 31
H_HGRN = 4
DK_HGRN = 128
DV_HGRN = 128
D_HGRN = 512
HEAD_DIM = 64
H_ATTN = 4
KV_HEADS = 2
GROUP = H_ATTN // KV_HEADS
D_ATTN = H_ATTN * HEAD_DIM
D_KV = KV_HEADS * HEAD_DIM
WINDOW = 128
ATTN_BLOCK = 128
NUM_BUCKETS = 32
MAX_DISTANCE = 128
D_FF = 4 * D_MODEL
N_MOD = 6
EPS = 1e-6

OFF_AVAL = 0
OFF_AGATE = OFF_AVAL + D_CONV
OFF_Q = OFF_AGATE + D_CONV
OFF_F = OFF_Q + H_HGRN * DK_HGRN
OFF_I = OFF_F + H_HGRN * DK_HGRN
OFF_G = OFF_I + D_HGRN
OFF_QA = OFF_G + D_HGRN
OFF_KA = OFF_QA + D_ATTN
OFF_VA = OFF_KA + D_KV
IN_WIDTH = OFF_VA + D_KV

HGRN_CHUNK = 64
HGRN_SUB = 16
CONV_PAD = 32
MIX_TILE = 256
TOK_TILE = 512
SAMPLE_BLOCK = 8
FF_CHUNK = 1024
VMEM_LIMIT = 56 * 1024 * 1024

NT_DIMS = (((1,), (1,)), ((), ()))
TN_DIMS = (((0,), (0,)), ((), ()))


def _silu(x):
    return x * jax.nn.sigmoid(x)


def _rms_rows(x):
    return x * lax.rsqrt(jnp.mean(x * x, axis=-1, keepdims=True) + EPS)


def _layer_lb(hlb, layer):
    m = jnp.max(hlb, axis=0, keepdims=True)
    e = jnp.exp(hlb - m)
    p = e / jnp.sum(e, axis=0, keepdims=True)
    lb = jnp.zeros_like(m)
    for i in range(1, layer + 1):
        lb = lb + p[i:i + 1, :]
    return lb


def _cumsum_rows_mxu(g, tri):
    g1 = g.astype(BF16)
    r1 = g - g1.astype(F32)
    g2 = r1.astype(BF16)
    g3 = (r1 - g2.astype(F32)).astype(BF16)
    return (jnp.dot(tri, g1, preferred_element_type=F32) + jnp.dot(tri, g2, preferred_element_type=F32)
            + jnp.dot(tri, g3, preferred_element_type=F32))


def _cumsum_rows_small(g):
    row = lax.broadcasted_iota(jnp.int32, g.shape, 0)
    b = jnp.zeros_like(g)
    for u in range(g.shape[0]):
        b = b + jnp.where(row >= u, g[u:u + 1, :], 0.0)
    return b


def _hgrn_chunk(q, k, v, b, st, sub):
    L = q.shape[0]
    nsb = L // sub
    o_blocks = [None] * nsb
    for j in range(nsb):
        lo = j * sub
        r = b[lo + sub - 1:lo + sub, :]
        kk = (k[lo:lo + sub] * jnp.exp(r - b[lo:lo + sub])).astype(BF16)
        ql = (q[lo:] * jnp.exp(b[lo:] - r)).astype(BF16)
        p = lax.dot_general(ql, kk, NT_DIMS, preferred_element_type=F32)
        row = lax.broadcasted_iota(jnp.int32, p.shape, 0)
        col = lax.broadcasted_iota(jnp.int32, p.shape, 1)
        p = jnp.where(row >= col, p, 0.0).astype(BF16)
        c = jnp.dot(p, v[lo:lo + sub].astype(BF16), preferred_element_type=F32)
        for i in range(j, nsb):
            piece = c[(i - j) * sub:(i - j + 1) * sub]
            o_blocks[i] = piece if o_blocks[i] is None else o_blocks[i] + piece
    o = o_blocks[0] if nsb == 1 else jnp.concatenate(o_blocks, axis=0)
    bl = b[L - 1:L, :]
    qt = (q * jnp.exp(b)).astype(BF16)
    o = o + lax.dot_general(qt, st.astype(BF16), NT_DIMS, preferred_element_type=F32)
    kst = (k * jnp.exp(bl - b)).astype(BF16)
    st_new = jnp.exp(bl) * st + lax.dot_general(v.astype(BF16), kst, TN_DIMS, preferred_element_type=F32)
    return o, st_new


def _sink_softmax_parts(scores, sink):
    m = None
    for s in scores:
        ms = jnp.max(s, axis=-1, keepdims=True)
        m = ms if m is None else jnp.maximum(m, ms)
    m = jnp.maximum(m, sink)
    ps = [jnp.exp(s - m) for s in scores]
    den = jnp.exp(sink - m)
    for p in ps:
        den = den + jnp.sum(p, axis=-1, keepdims=True)
    return ps, den


def _bias_kernel(tab_ref, bp_ref, bs_ref, op_ref, os_ref):
    for bref, oref in ((bp_ref, op_ref), (bs_ref, os_ref)):
        bk = bref[...]
        for h in range(H_ATTN):
            acc = jnp.full(bk.shape, -jnp.inf, F32)
            for bkt in range(NUM_BUCKETS):
                acc = jnp.where(bk == bkt, tab_ref[bkt, h], acc)
            oref[h] = acc


def _t5_bucket(rel):
    n = jnp.maximum(rel, 0)
    max_exact = NUM_BUCKETS // 2
    nf = jnp.maximum(n, max_exact).astype(F32)
    large = max_exact + (jnp.log(nf / max_exact) / math.log(MAX_DISTANCE / max_exact)
                         * (NUM_BUCKETS - max_exact)).astype(jnp.int32)
    large = jnp.minimum(large, NUM_BUCKETS - 1)
    return jnp.where(n < max_exact, n, large)


def _bias_tables(rel_bias, dec_seq, w_buf):
    qi = jnp.arange(ATTN_BLOCK, dtype=jnp.int32)[:, None]
    kc = jnp.arange(2 * ATTN_BLOCK, dtype=jnp.int32)[None, :]
    rel_p = qi + ATTN_BLOCK - kc
    bucket_p = jnp.where((rel_p >= 0) & (rel_p <= WINDOW), _t5_bucket(rel_p), -1)
    ts = jnp.arange(8, dtype=jnp.int32)[:, None]
    js = jnp.arange(2 * ATTN_BLOCK, dtype=jnp.int32)[None, :]
    rel_s = w_buf + ts - js
    ok_s = (rel_s >= 0) & (rel_s <= WINDOW) & (ts < dec_seq) & (js < w_buf + dec_seq)
    bucket_s = jnp.where(ok_s, _t5_bucket(rel_s), -1)
    return pl.pallas_call(
        _bias_kernel,
        out_shape=(jax.ShapeDtypeStruct((H_ATTN, ATTN_BLOCK, 2 * ATTN_BLOCK), F32),
                   jax.ShapeDtypeStruct((H_ATTN, 8, 2 * ATTN_BLOCK), F32)),
        in_specs=[pl.BlockSpec(memory_space=pltpu.SMEM),
                  pl.BlockSpec(memory_space=pltpu.VMEM),
                  pl.BlockSpec(memory_space=pltpu.VMEM)],
        out_specs=(pl.BlockSpec(memory_space=pltpu.VMEM), pl.BlockSpec(memory_space=pltpu.VMEM)),
        name="rel_bias_tables",
    )(rel_bias.astype(F32), bucket_p, bucket_s)


def _mod_kernel(c_ref, w_ref, b_ref, o_ref):
    s = _silu(c_ref[...]).astype(BF16)
    o_ref[...] = jnp.dot(s, w_ref[...], preferred_element_type=F32) + b_ref[...]


def _modulation(c_all, w_ada, b_ada):
    depth = w_ada.shape[0]
    n = c_all.shape[0]
    return pl.pallas_call(
        _mod_kernel,
        out_shape=jax.ShapeDtypeStruct((depth, n, N_MOD * D_MODEL), F32),
        grid=(depth, N_MOD),
        in_specs=[pl.BlockSpec((n, D_MODEL), lambda l, j: (0, 0)),
                  pl.BlockSpec((None, D_MODEL, D_MODEL), lambda l, j: (l, 0, j)),
                  pl.BlockSpec((None, 1, D_MODEL), lambda l, j: (l, 0, j))],
        out_specs=pl.BlockSpec((None, n, D_MODEL), lambda l, j: (l, 0, j)),
        compiler_params=pltpu.CompilerParams(dimension_semantics=("arbitrary", "arbitrary"),
                                             vmem_limit_bytes=VMEM_LIMIT),
        name="adaln_modulation",
    )(c_all, w_ada, b_ada.reshape(depth, 1, N_MOD * D_MODEL))


def _mod_spec(mod, chunk, tile, tiles_per_batch):
    if mod.ndim == 3:
        return pl.BlockSpec((None, 1, D_MODEL), lambda i: (i // tiles_per_batch, 0, chunk))
    return pl.BlockSpec((tile, D_MODEL), lambda i: (i, chunk))


def _inproj_kernel(x_ref, sh_ref, sc_ref, g_ref, w_ref, o_ref):
    h = _rms_rows(x_ref[...]) * g_ref[...] * (1.0 + sc_ref[...]) + sh_ref[...]
    o_ref[...] = jnp.dot(h.astype(BF16), w_ref[...], preferred_element_type=F32)


def _inproj(x2, mod, norm_g, w_in, layer, tile, tiles_per_batch):
    n = x2.shape[0]
    return pl.pallas_call(
        _inproj_kernel,
        out_shape=jax.ShapeDtypeStruct((n, IN_WIDTH), F32),
        grid=(n // tile,),
        in_specs=[pl.BlockSpec((tile, D_MODEL), lambda i: (i, 0)),
                  _mod_spec(mod, 0, tile, tiles_per_batch),
                  _mod_spec(mod, 1, tile, tiles_per_batch),
                  pl.BlockSpec((1, D_MODEL), lambda i: (0, 0)),
                  pl.BlockSpec((None, D_MODEL, IN_WIDTH), lambda i: (layer, 0, 0), pipeline_mode=pl.Buffered(1))],
        out_specs=pl.BlockSpec((tile, IN_WIDTH), lambda i: (i, 0)),
        compiler_params=pltpu.CompilerParams(dimension_semantics=("arbitrary",), vmem_limit_bytes=VMEM_LIMIT),
        name="in_projection",
    )(x2, mod, mod, norm_g.reshape(1, D_MODEL), w_in)


def _mlp_kernel(mix_ref, x_ref, g1_ref, sh_ref, sc_ref, g2_ref, ng_ref, wout_ref, wup_ref, wdn_ref, fg_ref, o_ref, *,
                final):
    x1 = x_ref[...] + g1_ref[...] * jnp.dot(mix_ref[...].astype(BF16), wout_ref[...], preferred_element_type=F32)
    h = (_rms_rows(x1) * ng_ref[...] * (1.0 + sc_ref[...]) + sh_ref[...]).astype(BF16)
    acc = None
    for c in range(D_FF // FF_CHUNK):
        u = jnp.dot(h, wup_ref[:, c * FF_CHUNK:(c + 1) * FF_CHUNK], preferred_element_type=F32)
        u = jnp.square(jnp.maximum(u, 0.0)).astype(BF16)
        d = jnp.dot(u, wdn_ref[c * FF_CHUNK:(c + 1) * FF_CHUNK, :], preferred_element_type=F32)
        acc = d if acc is None else acc + d
    x2 = x1 + g2_ref[...] * acc
    if final:
        x2 = _rms_rows(x2) * fg_ref[...]
    o_ref[...] = x2


def _out_mlp(mix2, x2, mod, norm_g, w_out, w_up, w_down, final_g, layer, tile, tiles_per_batch, final):
    n = x2.shape[0]
    const = lambda i: (0, 0)
    of_layer = lambda i: (layer, 0, 0)
    return pl.pallas_call(
        functools.partial(_mlp_kernel, final=final),
        out_shape=jax.ShapeDtypeStruct((n, D_MODEL), F32),
        grid=(n // tile,),
        in_specs=[pl.BlockSpec((tile, D_MODEL), lambda i: (i, 0)),
                  pl.BlockSpec((tile, D_MODEL), lambda i: (i, 0)),
                  _mod_spec(mod, 2, tile, tiles_per_batch),
                  _mod_spec(mod, 3, tile, tiles_per_batch),
                  _mod_spec(mod, 4, tile, tiles_per_batch),
                  _mod_spec(mod, 5, tile, tiles_per_batch),
                  pl.BlockSpec((1, D_MODEL), const),
                  pl.BlockSpec((None, D_MODEL, D_MODEL), of_layer, pipeline_mode=pl.Buffered(1)),
                  pl.BlockSpec((None, D_MODEL, D_FF), of_layer, pipeline_mode=pl.Buffered(1)),
                  pl.BlockSpec((None, D_FF, D_MODEL), of_layer, pipeline_mode=pl.Buffered(1)),
                  pl.BlockSpec((1, D_MODEL), const)],
        out_specs=pl.BlockSpec((tile, D_MODEL), lambda i: (i, 0)),
        compiler_params=pltpu.CompilerParams(dimension_semantics=("arbitrary",), vmem_limit_bytes=VMEM_LIMIT),
        name="out_projection_mlp",
    )(mix2, x2, mod, mod, mod, mod, norm_g.reshape(1, D_MODEL), w_out, w_up, w_down, final_g.reshape(1, D_MODEL))


def _conv_ln_swish(acc, lng, lnb):
    mu = jnp.mean(acc, axis=-1, keepdims=True)
    xc = acc - mu
    y = xc * lax.rsqrt(jnp.mean(xc * xc, axis=-1, keepdims=True) + EPS) * lng + lnb
    return _silu(y)


def _hgrn_gates(fh, lb):
    f = lb + (1.0 - lb) * jax.nn.sigmoid(fh)
    return jnp.log(f), 1.0 - f


def _hgrn_out(o, hng, gate):
    return _rms_rows(o) * hng * _silu(gate)


def _prompt_mix_kernel(sinks_ref, proj_ref, convw_ref, convb_ref, lng_ref, lnb_ref, hlb_ref, hng_ref, bias_ref,
                       mix_ref, convo_ref, so_ref, ko_ref, vo_ref,
                       abuf, kbuf, vbuf, st_ref, *, layer, tile):
    t = pl.program_id(1)
    last = pl.num_programs(1) - 1

    @pl.when(t == 0)
    def _():
        abuf[0:CONV_PAD, :] = jnp.zeros((CONV_PAD, D_CONV), F32)
        kbuf[0:ATTN_BLOCK, :] = jnp.zeros((ATTN_BLOCK, D_KV), BF16)
        vbuf[0:ATTN_BLOCK, :] = jnp.zeros((ATTN_BLOCK, D_KV), BF16)
        st_ref[...] = jnp.zeros(st_ref.shape, F32)

    abuf[CONV_PAD:CONV_PAD + tile, :] = (proj_ref[:, OFF_AVAL:OFF_AVAL + D_CONV]
                                         * jax.nn.sigmoid(proj_ref[:, OFF_AGATE:OFF_AGATE + D_CONV]))
    first_row = CONV_PAD - (CONV_WIDTH - 1)
    acc = jnp.broadcast_to(convb_ref[...], (tile, D_CONV))
    for j in range(CONV_WIDTH):
        acc = acc + convw_ref[j:j + 1, :] * abuf[first_row + j:first_row + j + tile, :]
    mix_ref[:, 0:D_CONV] = _conv_ln_swish(acc, lng_ref[...], lnb_ref[...]).astype(BF16)

    @pl.when(t == last)
    def _():
        convo_ref[...] = abuf[CONV_PAD + tile - (CONV_WIDTH - 1):CONV_PAD + tile, :]

    abuf[0:CONV_PAD, :] = abuf[tile:tile + CONV_PAD, :]

    lb = _layer_lb(hlb_ref[...], layer)
    hng = hng_ref[...]
    ri = lax.broadcasted_iota(jnp.int32, (HGRN_CHUNK, HGRN_CHUNK), 0)
    ci = lax.broadcasted_iota(jnp.int32, (HGRN_CHUNK, HGRN_CHUNK), 1)
    tri = jnp.where(ri >= ci, 1.0, 0.0).astype(BF16)

    def chunk_body(c, carry):
        rows = pl.ds(pl.multiple_of(c * HGRN_CHUNK, HGRN_CHUNK), HGRN_CHUNK)
        g, k = _hgrn_gates(proj_ref[rows, OFF_F:OFF_F + D_HGRN], lb)
        b = _cumsum_rows_mxu(g, tri)
        for h in range(H_HGRN):
            cs = slice(h * DK_HGRN, (h + 1) * DK_HGRN)
            q = proj_ref[rows, OFF_Q + h * DK_HGRN:OFF_Q + (h + 1) * DK_HGRN]
            v = proj_ref[rows, OFF_I + h * DV_HGRN:OFF_I + (h + 1) * DV_HGRN]
            o, st_new = _hgrn_chunk(q, k[:, cs], v, b[:, cs], st_ref[h], HGRN_SUB)
            st_ref[h] = st_new
            gate = proj_ref[rows, OFF_G + h * DV_HGRN:OFF_G + (h + 1) * DV_HGRN]
            mix_ref[rows, D_CONV + h * DV_HGRN:D_CONV + (h + 1) * DV_HGRN] = _hgrn_out(o, hng, gate).astype(BF16)
        return carry

    lax.fori_loop(0, tile // HGRN_CHUNK, chunk_body, 0)

    @pl.when(t == last)
    def _():
        for h in range(H_HGRN):
            so_ref[h] = st_ref[h].T

    kbuf[ATTN_BLOCK:ATTN_BLOCK + tile, :] = proj_ref[:, OFF_KA:OFF_KA + D_KV].astype(BF16)
    vbuf[ATTN_BLOCK:ATTN_BLOCK + tile, :] = proj_ref[:, OFF_VA:OFF_VA + D_KV].astype(BF16)
    scale = HEAD_DIM ** -0.5
    for blk in range(tile // ATTN_BLOCK):
        r0 = blk * ATTN_BLOCK
        heads = []
        for h in range(H_ATTN):
            kv = h // GROUP
            q = (proj_ref[r0:r0 + ATTN_BLOCK, OFF_QA + h * HEAD_DIM:OFF_QA + (h + 1) * HEAD_DIM] * scale).astype(BF16)
            kall = kbuf[r0:r0 + 2 * ATTN_BLOCK, kv * HEAD_DIM:(kv + 1) * HEAD_DIM]
            vall = vbuf[r0:r0 + 2 * ATTN_BLOCK, kv * HEAD_DIM:(kv + 1) * HEAD_DIM]
            s = lax.dot_general(q, kall, NT_DIMS, preferred_element_type=F32) + bias_ref[h]
            if blk == 0:
                col = lax.broadcasted_iota(jnp.int32, s.shape, 1)
                s = jnp.where(col + (t * tile - ATTN_BLOCK) >= 0, s, -jnp.inf)
            (p,), den = _sink_softmax_parts([s], sinks_ref[h])
            heads.append(jnp.dot(p.astype(BF16), vall, preferred_element_type=F32) / den)
        mix_ref[r0:r0 + ATTN_BLOCK, D_CONV + D_HGRN:D_MODEL] = jnp.concatenate(heads, axis=1).astype(BF16)

    @pl.when(t == last)
    def _():
        ko_ref[...] = proj_ref[tile - WINDOW:tile, OFF_KA:OFF_KA + D_KV]
        vo_ref[...] = proj_ref[tile - WINDOW:tile, OFF_VA:OFF_VA + D_KV]

    kbuf[0:ATTN_BLOCK, :] = kbuf[tile:tile + ATTN_BLOCK, :]
    vbuf[0:ATTN_BLOCK, :] = vbuf[tile:tile + ATTN_BLOCK, :]


def _carry_specs(carried):
    return [pl.BlockSpec(memory_space=pl.ANY)] * len(carried)


def _carried(kernel_fn, n_in, n_carried):
    if n_carried == 0:
        return kernel_fn
    return lambda *refs: kernel_fn(*refs[:n_in], *refs[n_in + n_carried:])


def _prompt_mixers(proj, sinks, conv_w, conv_b, ln_g, ln_b, hgrn_lb, hng, bias_p, layer, carried):
    B, T = proj.shape[:2]
    tile = MIX_TILE
    depth = hgrn_lb.shape[0]
    const2 = lambda b, t: (0, 0)
    inputs = (sinks, proj, conv_w, conv_b.reshape(1, D_CONV), ln_g.reshape(1, D_CONV), ln_b.reshape(1, D_CONV),
              hgrn_lb, hng.reshape(1, DV_HGRN), bias_p)
    return pl.pallas_call(
        _carried(functools.partial(_prompt_mix_kernel, layer=layer, tile=tile), len(inputs), len(carried)),
        out_shape=(jax.ShapeDtypeStruct((B, T, D_MODEL), BF16),
                   jax.ShapeDtypeStruct((depth, B, CONV_WIDTH - 1, D_CONV), F32),
                   jax.ShapeDtypeStruct((depth, B, H_HGRN, DK_HGRN, DV_HGRN), F32),
                   jax.ShapeDtypeStruct((depth, B, WINDOW, D_KV), F32),
                   jax.ShapeDtypeStruct((depth, B, WINDOW, D_KV), F32)),
        grid=(B, T // tile),
        in_specs=[pl.BlockSpec(memory_space=pltpu.SMEM),
                  pl.BlockSpec((None, tile, IN_WIDTH), lambda b, t: (b, t, 0)),
                  pl.BlockSpec((CONV_WIDTH, D_CONV), const2),
                  pl.BlockSpec((1, D_CONV), const2),
                  pl.BlockSpec((1, D_CONV), const2),
                  pl.BlockSpec((1, D_CONV), const2),
                  pl.BlockSpec((depth, D_HGRN), const2),
                  pl.BlockSpec((1, DV_HGRN), const2),
                  pl.BlockSpec((H_ATTN, ATTN_BLOCK, 2 * ATTN_BLOCK), lambda b, t: (0, 0, 0))] + _carry_specs(carried),
        out_specs=(pl.BlockSpec((None, tile, D_MODEL), lambda b, t: (b, t, 0)),
                   pl.BlockSpec((None, None, CONV_WIDTH - 1, D_CONV), lambda b, t: (layer, b, 0, 0)),
                   pl.BlockSpec((None, None, H_HGRN, DK_HGRN, DV_HGRN), lambda b, t: (layer, b, 0, 0, 0)),
                   pl.BlockSpec((None, None, WINDOW, D_KV), lambda b, t: (layer, b, 0, 0)),
                   pl.BlockSpec((None, None, WINDOW, D_KV), lambda b, t: (layer, b, 0, 0))),
        input_output_aliases={len(inputs) + i: 1 + i for i in range(len(carried))},
        scratch_shapes=[pltpu.VMEM((CONV_PAD + tile, D_CONV), F32),
                        pltpu.VMEM((ATTN_BLOCK + tile, D_KV), BF16),
                        pltpu.VMEM((ATTN_BLOCK + tile, D_KV), BF16),
                        pltpu.VMEM((H_HGRN, DV_HGRN, DK_HGRN), F32)],
        compiler_params=pltpu.CompilerParams(dimension_semantics=("arbitrary", "arbitrary"),
                                             vmem_limit_bytes=VMEM_LIMIT),
        name="prompt_mixers",
    )(*inputs, *carried)


def _sample_mix_kernel(sinks_ref, proj_ref, cconv_ref, state_ref, ck_ref, cv_ref, convw_ref, convb_ref, lng_ref,
                       lnb_ref, hlb_ref, hng_ref, bias_ref,
                       mix_ref, convo_ref, so_ref, ko_ref, vo_ref, full_ref, *, layer, block, seq, w_buf):
    hist = CONV_WIDTH - 1
    lb = _layer_lb(hlb_ref[...], layer)
    hng = hng_ref[...]
    scale = HEAD_DIM ** -0.5

    def body(i, carry):
        p = proj_ref[i]
        full_ref[0:hist, :] = cconv_ref[i]
        full_ref[hist:hist + seq, :] = p[:, OFF_AVAL:OFF_AVAL + D_CONV] * jax.nn.sigmoid(p[:, OFF_AGATE:OFF_AGATE + D_CONV])
        rows = [jnp.sum(convw_ref[...] * full_ref[s:s + CONV_WIDTH, :], axis=0, keepdims=True) for s in range(seq)]
        acc = jnp.concatenate(rows, axis=0) + convb_ref[...]
        convo_ref[i] = full_ref[seq:seq + hist, :]
        parts = [_conv_ln_swish(acc, lng_ref[...], lnb_ref[...])]
        g, k = _hgrn_gates(p[:, OFF_F:OFF_F + D_HGRN], lb)
        b = _cumsum_rows_small(g)
        for h in range(H_HGRN):
            cs = slice(h * DK_HGRN, (h + 1) * DK_HGRN)
            q = p[:, OFF_Q + h * DK_HGRN:OFF_Q + (h + 1) * DK_HGRN]
            v = p[:, OFF_I + h * DV_HGRN:OFF_I + (h + 1) * DV_HGRN]
            o, st_new = _hgrn_chunk(q, k[:, cs], v, b[:, cs], state_ref[i, h].T, seq)
            so_ref[i, h] = st_new.T
            parts.append(_hgrn_out(o, hng, p[:, OFF_G + h * DV_HGRN:OFF_G + (h + 1) * DV_HGRN]))
        knew = p[:, OFF_KA:OFF_KA + D_KV]
        vnew = p[:, OFF_VA:OFF_VA + D_KV]
        kc = ck_ref[i]
        vc = cv_ref[i]
        for h in range(H_ATTN):
            kv = h // GROUP
            hs = slice(kv * HEAD_DIM, (kv + 1) * HEAD_DIM)
            q = (p[:, OFF_QA + h * HEAD_DIM:OFF_QA + (h + 1) * HEAD_DIM] * scale).astype(BF16)
            bias = bias_ref[h]
            s_c = lax.dot_general(q, kc[:, hs].astype(BF16), NT_DIMS, preferred_element_type=F32) + bias[0:seq, 0:w_buf]
            s_n = (lax.dot_general(q, knew[:, hs].astype(BF16), NT_DIMS, preferred_element_type=F32)
                   + bias[0:seq, w_buf:w_buf + seq])
            (p_c, p_n), den = _sink_softmax_parts([s_c, s_n], sinks_ref[h])
            o = (jnp.dot(p_c.astype(BF16), vc[:, hs].astype(BF16), preferred_element_type=F32)
                 + jnp.dot(p_n.astype(BF16), vnew[:, hs].astype(BF16), preferred_element_type=F32))
            parts.append(o / den)
        mix_ref[i] = jnp.concatenate(parts, axis=1)
        ko_ref[i, 0:w_buf - seq, :] = kc[seq:, :]
        ko_ref[i, w_buf - seq:w_buf, :] = knew
        vo_ref[i, 0:w_buf - seq, :] = vc[seq:, :]
        vo_ref[i, w_buf - seq:w_buf, :] = vnew
        return carry

    lax.fori_loop(0, block, body, 0)


def _sample_mixers(proj3, sinks, cache_conv, state, cache_k, cache_v, conv_w, conv_b, ln_g, ln_b, hgrn_lb, hng,
                   bias_s, layer, carried):
    B, seq = proj3.shape[:2]
    w_buf = cache_k.shape[2]
    block = SAMPLE_BLOCK
    depth = hgrn_lb.shape[0]
    hist = CONV_WIDTH - 1
    const2 = lambda i: (0, 0)
    cache_specs = [pl.BlockSpec((None, block, hist, D_CONV), lambda i: (layer, i, 0, 0)),
                   pl.BlockSpec((None, block, H_HGRN, DK_HGRN, DV_HGRN), lambda i: (layer, i, 0, 0, 0)),
                   pl.BlockSpec((None, block, w_buf, D_KV), lambda i: (layer, i, 0, 0)),
                   pl.BlockSpec((None, block, w_buf, D_KV), lambda i: (layer, i, 0, 0))]
    inputs = (sinks, proj3, cache_conv, state, cache_k, cache_v, conv_w, conv_b.reshape(1, D_CONV),
              ln_g.reshape(1, D_CONV), ln_b.reshape(1, D_CONV), hgrn_lb, hng.reshape(1, DV_HGRN), bias_s)
    return pl.pallas_call(
        _carried(functools.partial(_sample_mix_kernel, layer=layer, block=block, seq=seq, w_buf=w_buf),
                 len(inputs), len(carried)),
        out_shape=(jax.ShapeDtypeStruct((B, seq, D_MODEL), F32),
                   jax.ShapeDtypeStruct((depth, B, hist, D_CONV), F32),
                   jax.ShapeDtypeStruct((depth, B, H_HGRN, DK_HGRN, DV_HGRN), F32),
                   jax.ShapeDtypeStruct((depth, B, w_buf, D_KV), F32),
                   jax.ShapeDtypeStruct((depth, B, w_buf, D_KV), F32)),
        grid=(B // block,),
        in_specs=[pl.BlockSpec(memory_space=pltpu.SMEM),
                  pl.BlockSpec((block, seq, IN_WIDTH), lambda i: (i, 0, 0))] + cache_specs + [
                  pl.BlockSpec((CONV_WIDTH, D_CONV), const2),
                  pl.BlockSpec((1, D_CONV), const2),
                  pl.BlockSpec((1, D_CONV), const2),
                  pl.BlockSpec((1, D_CONV), const2),
                  pl.BlockSpec((depth, D_HGRN), const2),
                  pl.BlockSpec((1, DV_HGRN), const2),
                  pl.BlockSpec((H_ATTN, 8, 2 * ATTN_BLOCK), lambda i: (0, 0, 0))] + _carry_specs(carried),
        out_specs=tuple([pl.BlockSpec((block, seq, D_MODEL), lambda i: (i, 0, 0))] + cache_specs),
        input_output_aliases={len(inputs) + i: 1 + i for i in range(len(carried))},
        scratch_shapes=[pltpu.VMEM((hist + seq + 6, D_CONV), F32)],
        compiler_params=pltpu.CompilerParams(dimension_semantics=("arbitrary",), vmem_limit_bytes=VMEM_LIMIT),
        name="sample_mixers",
    )(*inputs, *carried)


def kernel(x_prompt, x_sample, cache_conv, state_hgrn, cache_swa_k, cache_swa_v, c_prompt, c_sample, rel_bias, w_ada, b_ada, norm_mix_g, w_in, conv_w, conv_b, conv_ln_g, conv_ln_b, hgrn_lb, hgrn_norm_g, attn_sinks, w_out, norm_mlp_g, w_up, w_down, final_g):
    Bp, Tp = x_prompt.shape[:2]
    Bs, Ts = x_sample.shape[:2]
    depth = w_in.shape[0]
    w_buf = cache_swa_k.shape[2]
    assert Tp % MIX_TILE == 0 and (Bp * Tp) % TOK_TILE == 0 and Tp % TOK_TILE == 0 and Bs % SAMPLE_BLOCK == 0
    assert w_buf == WINDOW and Ts <= 8

    bias_p, bias_s = _bias_tables(rel_bias, Ts, w_buf)
    mod = _modulation(jnp.concatenate([c_prompt, c_sample], axis=0), w_ada.astype(BF16), b_ada)
    w_in_b, w_out_b, w_up_b, w_down_b = (w.astype(BF16) for w in (w_in, w_out, w_up, w_down))
    hlb = hgrn_lb.astype(F32)
    ck = cache_swa_k.reshape(depth, Bs, w_buf, D_KV)
    cv = cache_swa_v.reshape(depth, Bs, w_buf, D_KV)

    xp = x_prompt.reshape(Bp * Tp, D_MODEL)
    xs = x_sample.reshape(Bs * Ts, D_MODEL)
    tile_s = Bs * Ts
    caches_p = ()
    caches_s = ()
    for l in range(depth):
        final = l == depth - 1
        mod_p = mod[l, :Bp].reshape(Bp, 1, N_MOD * D_MODEL)
        mod_s = jnp.repeat(mod[l, Bp:], Ts, axis=0)
        proj_p = _inproj(xp, mod_p, norm_mix_g[l], w_in_b, l, TOK_TILE, Tp // TOK_TILE)
        mix_p, *caches_p = _prompt_mixers(proj_p.reshape(Bp, Tp, IN_WIDTH), attn_sinks[l], conv_w[l], conv_b[l],
                                          conv_ln_g[l], conv_ln_b[l], hlb, hgrn_norm_g[l], bias_p, l, caches_p)
        xp = _out_mlp(mix_p.reshape(Bp * Tp, D_MODEL), xp, mod_p, norm_mlp_g[l], w_out_b, w_up_b, w_down_b,
                      final_g, l, TOK_TILE, Tp // TOK_TILE, final)
        proj_s = _inproj(xs, mod_s, norm_mix_g[l], w_in_b, l, tile_s, 1)
        mix_s, *caches_s = _sample_mixers(proj_s.reshape(Bs, Ts, IN_WIDTH), attn_sinks[l], cache_conv, state_hgrn,
                                          ck, cv, conv_w[l], conv_b[l], conv_ln_g[l], conv_ln_b[l], hlb,
                                          hgrn_norm_g[l], bias_s, l, caches_s)
        xs = _out_mlp(mix_s.reshape(Bs * Ts, D_MODEL), xs, mod_s, norm_mlp_g[l], w_out_b, w_up_b, w_down_b,
                      final_g, l, tile_s, 1, final)
    cp, sp, kp, vp = caches_p
    cs, ss, ksn, vsn = caches_s
    return (xp.reshape(Bp, Tp, D_MODEL), xs.reshape(Bs, Ts, D_MODEL), cp, cs, sp, ss,
            kp.reshape(depth, Bp, WINDOW, KV_HEADS, HEAD_DIM), ksn.reshape(depth, Bs, w_buf, KV_HEADS, HEAD_DIM),
            vp.reshape(depth, Bp, WINDOW, KV_HEADS, HEAD_DIM), vsn.reshape(depth, Bs, w_buf, KV_HEADS, HEAD_DIM))
```

```python
import functools
import math

import jax
import jax.numpy as jnp
from jax import lax
from jax.experimental import pallas as pl
from jax.experimental.pallas import tpu as pltpu

F32 = jnp.float32
BF16 = jnp.bfloat16

D_MODEL = 1024
D_CONV = 256
CONV_WIDTH = 31
H_HGRN = 4
DK_HGRN = 128
DV_HGRN = 128
D_HGRN = 512
HEAD_DIM = 64
H_ATTN = 4
KV_HEADS = 2
GROUP = H_ATTN // KV_HEADS
D_ATTN = H_ATTN * HEAD_DIM
D_KV = KV_HEADS * HEAD_DIM
WINDOW = 128
ATTN_BLOCK = 128
NUM_BUCKETS = 32
MAX_DISTANCE = 128
D_FF = 4 * D_MODEL
N_MOD = 6
EPS = 1e-6

OFF_AVAL = 0
OFF_AGATE = OFF_AVAL + D_CONV
OFF_Q = OFF_AGATE + D_CONV
OFF_F = OFF_Q + H_HGRN * DK_HGRN
OFF_I = OFF_F + H_HGRN * DK_HGRN
OFF_G = OFF_I + D_HGRN
OFF_QA = OFF_G + D_HGRN
OFF_KA = OFF_QA + D_ATTN
OFF_VA = OFF_KA + D_KV
IN_WIDTH = OFF_VA + D_KV

HGRN_CHUNK = 64
HGRN_KEYBLOCK = 32
HGRN_SPAN = 4
SUBLANES = 8
CONV_PAD = 32
MIX_TILE = 256
TOK_TILE = 512
SAMPLE_BLOCK = 8
FF_CHUNK = 1024
VMEM_LIMIT = 56 * 1024 * 1024

NT_DIMS = (((1,), (1,)), ((), ()))
TN_DIMS = (((0,), (0,)), ((), ()))


def _silu(x):
    return x * jax.nn.sigmoid(x)


def _rms_rows(x):
    return x * lax.rsqrt(jnp.mean(x * x, axis=-1, keepdims=True) + EPS)


def _layer_lb(hlb, layer):
    m = jnp.max(hlb, axis=0, keepdims=True)
    e = jnp.exp(hlb - m)
    p = e / jnp.sum(e, axis=0, keepdims=True)
    lb = jnp.zeros_like(m)
    for i in range(1, layer + 1):
        lb = lb + p[i:i + 1, :]
    return lb


def _cumsum_rows_mxu(g, tri):
    g1 = g.astype(BF16)
    r1 = g - g1.astype(F32)
    g2 = r1.astype(BF16)
    g3 = (r1 - g2.astype(F32)).astype(BF16)
    return (jnp.dot(tri, g1, preferred_element_type=F32) + jnp.dot(tri, g2, preferred_element_type=F32)
            + jnp.dot(tri, g3, preferred_element_type=F32))


def _cumsum_rows_small(g):
    row = lax.broadcasted_iota(jnp.int32, g.shape, 0)
    b = jnp.zeros_like(g)
    for u in range(g.shape[0]):
        b = b + jnp.where(row >= u, g[u:u + 1, :], 0.0)
    return b


def _hgrn_chunk(q, k, v, b, st, sub):
    L = q.shape[0]
    nsb = L // sub
    o_blocks = [None] * nsb
    for j in range(nsb):
        lo = j * sub
        r = b[lo + sub - 1:lo + sub, :]
        kk = (k[lo:lo + sub] * jnp.exp(r - b[lo:lo + sub])).astype(BF16)
        ql = (q[lo:] * jnp.exp(b[lo:] - r)).astype(BF16)
        p = lax.dot_general(ql, kk, NT_DIMS, preferred_element_type=F32)
        row = lax.broadcasted_iota(jnp.int32, p.shape, 0)
        col = lax.broadcasted_iota(jnp.int32, p.shape, 1)
        p = jnp.where(row >= col, p, 0.0).astype(BF16)
        c = jnp.dot(p, v[lo:lo + sub].astype(BF16), preferred_element_type=F32)
        for i in range(j, nsb):
            piece = c[(i - j) * sub:(i - j + 1) * sub]
            o_blocks[i] = piece if o_blocks[i] is None else o_blocks[i] + piece
    o = o_blocks[0] if nsb == 1 else jnp.concatenate(o_blocks, axis=0)
    bl = b[L - 1:L, :]
    qt = (q * jnp.exp(b)).astype(BF16)
    o = o + lax.dot_general(qt, st.astype(BF16), NT_DIMS, preferred_element_type=F32)
    kst = (k * jnp.exp(bl - b)).astype(BF16)
    st_new = jnp.exp(bl) * st + lax.dot_general(v.astype(BF16), kst, TN_DIMS, preferred_element_type=F32)
    return o, st_new


def _hgrn_span(proj_ref, mix_ref, st_ref, row0, lb, hng, tri):
    L, KB = HGRN_CHUNK, HGRN_KEYBLOCK
    span = HGRN_SPAN * L
    g, k = _hgrn_gates(proj_ref[pl.ds(row0, span), OFF_F:OFF_F + D_HGRN], lb)
    b = _cumsum_rows_mxu(g, tri)
    units = [(c, h) for c in range(HGRN_SPAN) for h in range(H_HGRN)]

    ops = {}
    for c, h in units:
        rows = pl.ds(row0 + c * L, L)
        cs = slice(h * DK_HGRN, (h + 1) * DK_HGRN)
        q = proj_ref[rows, OFF_Q + h * DK_HGRN:OFF_Q + (h + 1) * DK_HGRN]
        v = proj_ref[rows, OFF_I + h * DV_HGRN:OFF_I + (h + 1) * DV_HGRN].astype(BF16)
        kk = k[c * L:(c + 1) * L, cs]
        bb = b[c * L:(c + 1) * L, cs]
        qp, kp = [], []
        for lo in range(0, L, KB):
            r = bb[lo + KB // 2 - 1:lo + KB // 2, :]
            kp.append((kk[lo:lo + KB] * jnp.exp(r - bb[lo:lo + KB])).astype(BF16))
            qp.append((q[lo:] * jnp.exp(bb[lo:] - r)).astype(BF16))
        bl = bb[L - 1:L, :]
        ops[c, h] = dict(qp=qp, kp=kp, v=v, qt=(q * jnp.exp(bb)).astype(BF16),
                         kst=(kk * jnp.exp(bl - bb)).astype(BF16), e=jnp.exp(bl))

    for u in units:
        o = ops[u]
        o["p"] = [lax.dot_general(qp, kp, NT_DIMS, preferred_element_type=F32) for qp, kp in zip(o["qp"], o["kp"])]
        o["m"] = lax.dot_general(o["v"], o["kst"], TN_DIMS, preferred_element_type=F32)

    for u in units:
        pm = []
        for p in ops[u]["p"]:
            row = lax.broadcasted_iota(jnp.int32, p.shape, 0)
            col = lax.broadcasted_iota(jnp.int32, p.shape, 1)
            pm.append(jnp.where(row >= col, p, 0.0).astype(BF16))
        ops[u]["p"] = pm

    for u in units:
        o = ops[u]
        blocks = [None] * (L // KB)
        for j, p in enumerate(o["p"]):
            cj = jnp.dot(p, o["v"][j * KB:(j + 1) * KB], preferred_element_type=F32)
            for i in range(j, L // KB):
                piece = cj[(i - j) * KB:(i - j + 1) * KB]
                blocks[i] = piece if blocks[i] is None else blocks[i] + piece
        o["o"] = jnp.concatenate(blocks, axis=0)

    for h in range(H_HGRN):
        st = st_ref[h]
        for c in range(HGRN_SPAN):
            o = ops[c, h]
            out = o["o"] + lax.dot_general(o["qt"], st.astype(BF16), NT_DIMS, preferred_element_type=F32)
            st = o["e"] * st + o["m"]
            rows = pl.ds(row0 + c * L, L)
            gate = proj_ref[rows, OFF_G + h * DV_HGRN:OFF_G + (h + 1) * DV_HGRN]
            mix_ref[rows, D_CONV + h * DV_HGRN:D_CONV + (h + 1) * DV_HGRN] = _hgrn_out(out, hng, gate).astype(BF16)
        st_ref[h] = st


def _sink_softmax_parts(scores, sink):
    m = None
    for s in scores:
        ms = jnp.max(s, axis=-1, keepdims=True)
        m = ms if m is None else jnp.maximum(m, ms)
    m = jnp.maximum(m, sink)
    ps = [jnp.exp(s - m) for s in scores]
    den = jnp.exp(sink - m)
    for p in ps:
        den = den + jnp.sum(p, axis=-1, keepdims=True)
    return ps, den


def _bias_kernel(tab_ref, bp_ref, bs_ref, op_ref, os_ref):
    for bref, oref in ((bp_ref, op_ref), (bs_ref, os_ref)):
        bk = bref[...]
        for h in range(H_ATTN):
            acc = jnp.full(bk.shape, -jnp.inf, F32)
            for bkt in range(NUM_BUCKETS):
                acc = jnp.where(bk == bkt, tab_ref[bkt, h], acc)
            oref[h] = acc


def _t5_bucket(rel):
    n = jnp.maximum(rel, 0)
    max_exact = NUM_BUCKETS // 2
    nf = jnp.maximum(n, max_exact).astype(F32)
    large = max_exact + (jnp.log(nf / max_exact) / math.log(MAX_DISTANCE / max_exact)
                         * (NUM_BUCKETS - max_exact)).astype(jnp.int32)
    large = jnp.minimum(large, NUM_BUCKETS - 1)
    return jnp.where(n < max_exact, n, large)


def _bias_tables(rel_bias, dec_seq, w_buf):
    qi = jnp.arange(ATTN_BLOCK, dtype=jnp.int32)[:, None]
    kc = jnp.arange(2 * ATTN_BLOCK, dtype=jnp.int32)[None, :]
    rel_p = qi + ATTN_BLOCK - kc
    bucket_p = jnp.where((rel_p >= 0) & (rel_p <= WINDOW), _t5_bucket(rel_p), -1)
    ts = jnp.arange(8, dtype=jnp.int32)[:, None]
    js = jnp.arange(2 * ATTN_BLOCK, dtype=jnp.int32)[None, :]
    rel_s = w_buf + ts - js
    ok_s = (rel_s >= 0) & (rel_s <= WINDOW) & (ts < dec_seq) & (js < w_buf + dec_seq)
    bucket_s = jnp.where(ok_s, _t5_bucket(rel_s), -1)
    return pl.pallas_call(
        _bias_kernel,
        out_shape=(jax.ShapeDtypeStruct((H_ATTN, ATTN_BLOCK, 2 * ATTN_BLOCK), F32),
                   jax.ShapeDtypeStruct((H_ATTN, 8, 2 * ATTN_BLOCK), F32)),
        in_specs=[pl.BlockSpec(memory_space=pltpu.SMEM),
                  pl.BlockSpec(memory_space=pltpu.VMEM),
                  pl.BlockSpec(memory_space=pltpu.VMEM)],
        out_specs=(pl.BlockSpec(memory_space=pltpu.VMEM), pl.BlockSpec(memory_space=pltpu.VMEM)),
        name="rel_bias_tables",
    )(rel_bias.astype(F32), bucket_p, bucket_s)


def _mod_kernel(c_ref, w_ref, b_ref, o_ref):
    s = _silu(c_ref[...]).astype(BF16)
    o_ref[...] = jnp.dot(s, w_ref[...], preferred_element_type=F32) + b_ref[...]


def _modulation(c_all, w_ada, b_ada):
    depth = w_ada.shape[0]
    n = c_all.shape[0]
    return pl.pallas_call(
        _mod_kernel,
        out_shape=jax.ShapeDtypeStruct((depth, n, N_MOD * D_MODEL), F32),
        grid=(depth, N_MOD),
        in_specs=[pl.BlockSpec((n, D_MODEL), lambda l, j: (0, 0)),
                  pl.BlockSpec((None, D_MODEL, D_MODEL), lambda l, j: (l, 0, j)),
                  pl.BlockSpec((None, 1, D_MODEL), lambda l, j: (l, 0, j))],
        out_specs=pl.BlockSpec((None, n, D_MODEL), lambda l, j: (l, 0, j)),
        compiler_params=pltpu.CompilerParams(dimension_semantics=("arbitrary", "arbitrary"),
                                             vmem_limit_bytes=VMEM_LIMIT),
        name="adaln_modulation",
    )(c_all, w_ada, b_ada.reshape(depth, 1, N_MOD * D_MODEL))


def _mod_spec(mod, chunk, tile, tiles_per_batch):
    if mod.ndim == 3:
        return pl.BlockSpec((None, 1, D_MODEL), lambda i: (i // tiles_per_batch, 0, chunk))
    return pl.BlockSpec((tile, D_MODEL), lambda i: (i, chunk))


def _inproj_kernel(x_ref, sh_ref, sc_ref, g_ref, w_ref, o_ref):
    h = _rms_rows(x_ref[...]) * g_ref[...] * (1.0 + sc_ref[...]) + sh_ref[...]
    o_ref[...] = jnp.dot(h.astype(BF16), w_ref[...], preferred_element_type=F32)


def _inproj(x2, mod, norm_g, w_in, layer, tile, tiles_per_batch):
    n = x2.shape[0]
    return pl.pallas_call(
        _inproj_kernel,
        out_shape=jax.ShapeDtypeStruct((n, IN_WIDTH), F32),
        grid=(n // tile,),
        in_specs=[pl.BlockSpec((tile, D_MODEL), lambda i: (i, 0)),
                  _mod_spec(mod, 0, tile, tiles_per_batch),
                  _mod_spec(mod, 1, tile, tiles_per_batch),
                  pl.BlockSpec((1, D_MODEL), lambda i: (0, 0)),
                  pl.BlockSpec((None, D_MODEL, IN_WIDTH), lambda i: (layer, 0, 0), pipeline_mode=pl.Buffered(1))],
        out_specs=pl.BlockSpec((tile, IN_WIDTH), lambda i: (i, 0)),
        compiler_params=pltpu.CompilerParams(dimension_semantics=("arbitrary",), vmem_limit_bytes=VMEM_LIMIT),
        name="in_projection",
    )(x2, mod, mod, norm_g.reshape(1, D_MODEL), w_in)


def _mlp_kernel(mix_ref, x_ref, g1_ref, sh_ref, sc_ref, g2_ref, ng_ref, wout_ref, wup_ref, wdn_ref, fg_ref, o_ref, *,
                final):
    x1 = x_ref[...] + g1_ref[...] * jnp.dot(mix_ref[...].astype(BF16), wout_ref[...], preferred_element_type=F32)
    h = (_rms_rows(x1) * ng_ref[...] * (1.0 + sc_ref[...]) + sh_ref[...]).astype(BF16)
    acc = None
    for c in range(D_FF // FF_CHUNK):
        u = jnp.dot(h, wup_ref[:, c * FF_CHUNK:(c + 1) * FF_CHUNK], preferred_element_type=F32)
        u = jnp.square(jnp.maximum(u, 0.0)).astype(BF16)
        d = jnp.dot(u, wdn_ref[c * FF_CHUNK:(c + 1) * FF_CHUNK, :], preferred_element_type=F32)
        acc = d if acc is None else acc + d
    x2 = x1 + g2_ref[...] * acc
    if final:
        x2 = _rms_rows(x2) * fg_ref[...]
    o_ref[...] = x2


def _out_mlp(mix2, x2, mod, norm_g, w_out, w_up, w_down, final_g, layer, tile, tiles_per_batch, final):
    n = x2.shape[0]
    const = lambda i: (0, 0)
    of_layer = lambda i: (layer, 0, 0)
    return pl.pallas_call(
        functools.partial(_mlp_kernel, final=final),
        out_shape=jax.ShapeDtypeStruct((n, D_MODEL), F32),
        grid=(n // tile,),
        in_specs=[pl.BlockSpec((tile, D_MODEL), lambda i: (i, 0)),
                  pl.BlockSpec((tile, D_MODEL), lambda i: (i, 0)),
                  _mod_spec(mod, 2, tile, tiles_per_batch),
                  _mod_spec(mod, 3, tile, tiles_per_batch),
                  _mod_spec(mod, 4, tile, tiles_per_batch),
                  _mod_spec(mod, 5, tile, tiles_per_batch),
                  pl.BlockSpec((1, D_MODEL), const),
                  pl.BlockSpec((None, D_MODEL, D_MODEL), of_layer, pipeline_mode=pl.Buffered(1)),
                  pl.BlockSpec((None, D_MODEL, D_FF), of_layer, pipeline_mode=pl.Buffered(1)),
                  pl.BlockSpec((None, D_FF, D_MODEL), of_layer, pipeline_mode=pl.Buffered(1)),
                  pl.BlockSpec((1, D_MODEL), const)],
        out_specs=pl.BlockSpec((tile, D_MODEL), lambda i: (i, 0)),
        compiler_params=pltpu.CompilerParams(dimension_semantics=("arbitrary",), vmem_limit_bytes=VMEM_LIMIT),
        name="out_projection_mlp",
    )(mix2, x2, mod, mod, mod, mod, norm_g.reshape(1, D_MODEL), w_out, w_up, w_down, final_g.reshape(1, D_MODEL))


def _conv_ln_swish(acc, lng, lnb):
    mu = jnp.mean(acc, axis=-1, keepdims=True)
    xc = acc - mu
    y = xc * lax.rsqrt(jnp.mean(xc * xc, axis=-1, keepdims=True) + EPS) * lng + lnb
    return _silu(y)


def _hgrn_gates(fh, lb):
    f = lb + (1.0 - lb) * jax.nn.sigmoid(fh)
    return jnp.log(f), 1.0 - f


def _hgrn_out(o, hng, gate):
    return _rms_rows(o) * hng * _silu(gate)


def _prompt_mix_kernel(sinks_ref, proj_ref, convw_ref, convb_ref, lng_ref, lnb_ref, hlb_ref, hng_ref, bias_ref,
                       mix_ref, convo_ref, so_ref, ko_ref, vo_ref,
                       abuf, kbuf, vbuf, st_ref, *, layer, tile):
    t = pl.program_id(1)
    last = pl.num_programs(1) - 1

    @pl.when(t == 0)
    def _():
        abuf[0:CONV_PAD, :] = jnp.zeros((CONV_PAD, D_CONV), F32)
        abuf[CONV_PAD + tile:CONV_PAD + tile + SUBLANES, :] = jnp.zeros((SUBLANES, D_CONV), F32)
        kbuf[0:ATTN_BLOCK, :] = jnp.zeros((ATTN_BLOCK, D_KV), BF16)
        vbuf[0:ATTN_BLOCK, :] = jnp.zeros((ATTN_BLOCK, D_KV), BF16)
        st_ref[...] = jnp.zeros(st_ref.shape, F32)

    abuf[CONV_PAD:CONV_PAD + tile, :] = (proj_ref[:, OFF_AVAL:OFF_AVAL + D_CONV]
                                         * jax.nn.sigmoid(proj_ref[:, OFF_AGATE:OFF_AGATE + D_CONV]))
    first_row = CONV_PAD - (CONV_WIDTH - 1)
    acc = jnp.broadcast_to(convb_ref[...], (tile, D_CONV))
    for r in range(SUBLANES):
        z = None
        for off in range(r, first_row + CONV_WIDTH, SUBLANES):
            j = off - first_row
            if j < 0:
                continue
            term = convw_ref[j:j + 1, :] * abuf[off - r:off - r + tile + SUBLANES, :]
            z = term if z is None else z + term
        acc = acc + z[r:r + tile]
    mix_ref[:, 0:D_CONV] = _conv_ln_swish(acc, lng_ref[...], lnb_ref[...]).astype(BF16)

    @pl.when(t == last)
    def _():
        convo_ref[...] = abuf[CONV_PAD + tile - (CONV_WIDTH - 1):CONV_PAD + tile, :]

    abuf[0:CONV_PAD, :] = abuf[tile:tile + CONV_PAD, :]

    lb = _layer_lb(hlb_ref[...], layer)
    hng = hng_ref[...]
    span = HGRN_SPAN * HGRN_CHUNK
    ri = lax.broadcasted_iota(jnp.int32, (span, span), 0)
    ci = lax.broadcasted_iota(jnp.int32, (span, span), 1)
    tri = jnp.where((ri >= ci) & (ri // HGRN_CHUNK == ci // HGRN_CHUNK), 1.0, 0.0).astype(BF16)

    def span_body(i, carry):
        _hgrn_span(proj_ref, mix_ref, st_ref, pl.multiple_of(i * span, span), lb, hng, tri)
        return carry

    lax.fori_loop(0, tile // span, span_body, 0)

    @pl.when(t == last)
    def _():
        for h in range(H_HGRN):
            so_ref[h] = st_ref[h].T

    kbuf[ATTN_BLOCK:ATTN_BLOCK + tile, :] = proj_ref[:, OFF_KA:OFF_KA + D_KV].astype(BF16)
    vbuf[ATTN_BLOCK:ATTN_BLOCK + tile, :] = proj_ref[:, OFF_VA:OFF_VA + D_KV].astype(BF16)
    scale = HEAD_DIM ** -0.5
    for blk in range(tile // ATTN_BLOCK):
        r0 = blk * ATTN_BLOCK
        heads = []
        for h in range(H_ATTN):
            kv = h // GROUP
            q = (proj_ref[r0:r0 + ATTN_BLOCK, OFF_QA + h * HEAD_DIM:OFF_QA + (h + 1) * HEAD_DIM] * scale).astype(BF16)
            kall = kbuf[r0:r0 + 2 * ATTN_BLOCK, kv * HEAD_DIM:(kv + 1) * HEAD_DIM]
            vall = vbuf[r0:r0 + 2 * ATTN_BLOCK, kv * HEAD_DIM:(kv + 1) * HEAD_DIM]
            s = lax.dot_general(q, kall, NT_DIMS, preferred_element_type=F32) + bias_ref[h]
            if blk == 0:
                col = lax.broadcasted_iota(jnp.int32, s.shape, 1)
                s = jnp.where(col + (t * tile - ATTN_BLOCK) >= 0, s, -jnp.inf)
            (p,), den = _sink_softmax_parts([s], sinks_ref[h])
            heads.append(jnp.dot(p.astype(BF16), vall, preferred_element_type=F32) / den)
        mix_ref[r0:r0 + ATTN_BLOCK, D_CONV + D_HGRN:D_MODEL] = jnp.concatenate(heads, axis=1).astype(BF16)

    @pl.when(t == last)
    def _():
        ko_ref[...] = proj_ref[tile - WINDOW:tile, OFF_KA:OFF_KA + D_KV]
        vo_ref[...] = proj_ref[tile - WINDOW:tile, OFF_VA:OFF_VA + D_KV]

    kbuf[0:ATTN_BLOCK, :] = kbuf[tile:tile + ATTN_BLOCK, :]
    vbuf[0:ATTN_BLOCK, :] = vbuf[tile:tile + ATTN_BLOCK, :]


def _carry_specs(carried):
    return [pl.BlockSpec(memory_space=pl.ANY)] * len(carried)


def _carried(kernel_fn, n_in, n_carried):
    if n_carried == 0:
        return kernel_fn
    return lambda *refs: kernel_fn(*refs[:n_in], *refs[n_in + n_carried:])


def _prompt_mixers(proj, sinks, conv_w, conv_b, ln_g, ln_b, hgrn_lb, hng, bias_p, layer, carried):
    B, T = proj.shape[:2]
    tile = MIX_TILE
    depth = hgrn_lb.shape[0]
    const2 = lambda b, t: (0, 0)
    inputs = (sinks, proj, conv_w, conv_b.reshape(1, D_CONV), ln_g.reshape(1, D_CONV), ln_b.reshape(1, D_CONV),
              hgrn_lb, hng.reshape(1, DV_HGRN), bias_p)
    return pl.pallas_call(
        _carried(functools.partial(_prompt_mix_kernel, layer=layer, tile=tile), len(inputs), len(carried)),
        out_shape=(jax.ShapeDtypeStruct((B, T, D_MODEL), BF16),
                   jax.ShapeDtypeStruct((depth, B, CONV_WIDTH - 1, D_CONV), F32),
                   jax.ShapeDtypeStruct((depth, B, H_HGRN, DK_HGRN, DV_HGRN), F32),
                   jax.ShapeDtypeStruct((depth, B, WINDOW, D_KV), F32),
                   jax.ShapeDtypeStruct((depth, B, WINDOW, D_KV), F32)),
        grid=(B, T // tile),
        in_specs=[pl.BlockSpec(memory_space=pltpu.SMEM),
                  pl.BlockSpec((None, tile, IN_WIDTH), lambda b, t: (b, t, 0)),
                  pl.BlockSpec((CONV_WIDTH, D_CONV), const2),
                  pl.BlockSpec((1, D_CONV), const2),
                  pl.BlockSpec((1, D_CONV), const2),
                  pl.BlockSpec((1, D_CONV), const2),
                  pl.BlockSpec((depth, D_HGRN), const2),
                  pl.BlockSpec((1, DV_HGRN), const2),
                  pl.BlockSpec((H_ATTN, ATTN_BLOCK, 2 * ATTN_BLOCK), lambda b, t: (0, 0, 0))] + _carry_specs(carried),
        out_specs=(pl.BlockSpec((None, tile, D_MODEL), lambda b, t: (b, t, 0)),
                   pl.BlockSpec((None, None, CONV_WIDTH - 1, D_CONV), lambda b, t: (layer, b, 0, 0)),
                   pl.BlockSpec((None, None, H_HGRN, DK_HGRN, DV_HGRN), lambda b, t: (layer, b, 0, 0, 0)),
                   pl.BlockSpec((None, None, WINDOW, D_KV), lambda b, t: (layer, b, 0, 0)),
                   pl.BlockSpec((None, None, WINDOW, D_KV), lambda b, t: (layer, b, 0, 0))),
        input_output_aliases={len(inputs) + i: 1 + i for i in range(len(carried))},
        scratch_shapes=[pltpu.VMEM((CONV_PAD + tile + SUBLANES, D_CONV), F32),
                        pltpu.VMEM((ATTN_BLOCK + tile, D_KV), BF16),
                        pltpu.VMEM((ATTN_BLOCK + tile, D_KV), BF16),
                        pltpu.VMEM((H_HGRN, DV_HGRN, DK_HGRN), F32)],
        compiler_params=pltpu.CompilerParams(dimension_semantics=("arbitrary", "arbitrary"),
                                             vmem_limit_bytes=VMEM_LIMIT),
        name="prompt_mixers",
    )(*inputs, *carried)


def _sample_mix_kernel(sinks_ref, proj_ref, cconv_ref, state_ref, ck_ref, cv_ref, convw_ref, convb_ref, lng_ref,
                       lnb_ref, hlb_ref, hng_ref, bias_ref,
                       mix_ref, convo_ref, so_ref, ko_ref, vo_ref, full_ref, *, layer, block, seq, w_buf):
    hist = CONV_WIDTH - 1
    lb = _layer_lb(hlb_ref[...], layer)
    hng = hng_ref[...]
    scale = HEAD_DIM ** -0.5

    def body(i, carry):
        p = proj_ref[i]
        full_ref[0:hist, :] = cconv_ref[i]
        full_ref[hist:hist + seq, :] = p[:, OFF_AVAL:OFF_AVAL + D_CONV] * jax.nn.sigmoid(p[:, OFF_AGATE:OFF_AGATE + D_CONV])
        rows = [jnp.sum(convw_ref[...] * full_ref[s:s + CONV_WIDTH, :], axis=0, keepdims=True) for s in range(seq)]
        acc = jnp.concatenate(rows, axis=0) + convb_ref[...]
        convo_ref[i] = full_ref[seq:seq + hist, :]
        parts = [_conv_ln_swish(acc, lng_ref[...], lnb_ref[...])]
        g, k = _hgrn_gates(p[:, OFF_F:OFF_F + D_HGRN], lb)
        b = _cumsum_rows_small(g)
        for h in range(H_HGRN):
            cs = slice(h * DK_HGRN, (h + 1) * DK_HGRN)
            q = p[:, OFF_Q + h * DK_HGRN:OFF_Q + (h + 1) * DK_HGRN]
            v = p[:, OFF_I + h * DV_HGRN:OFF_I + (h + 1) * DV_HGRN]
            o, st_new = _hgrn_chunk(q, k[:, cs], v, b[:, cs], state_ref[i, h].T, seq)
            so_ref[i, h] = st_new.T
            parts.append(_hgrn_out(o, hng, p[:, OFF_G + h * DV_HGRN:OFF_G + (h + 1) * DV_HGRN]))
        knew = p[:, OFF_KA:OFF_KA + D_KV]
        vnew = p[:, OFF_VA:OFF_VA + D_KV]
        kc = ck_ref[i]
        vc = cv_ref[i]
        for h in range(H_ATTN):
            kv = h // GROUP
            hs = slice(kv * HEAD_DIM, (kv + 1) * HEAD_DIM)
            q = (p[:, OFF_QA + h * HEAD_DIM:OFF_QA + (h + 1) * HEAD_DIM] * scale).astype(BF16)
            bias = bias_ref[h]
            s_c = lax.dot_general(q, kc[:, hs].astype(BF16), NT_DIMS, preferred_element_type=F32) + bias[0:seq, 0:w_buf]
            s_n = (lax.dot_general(q, knew[:, hs].astype(BF16), NT_DIMS, preferred_element_type=F32)
                   + bias[0:seq, w_buf:w_buf + seq])
            (p_c, p_n), den = _sink_softmax_parts([s_c, s_n], sinks_ref[h])
            o = (jnp.dot(p_c.astype(BF16), vc[:, hs].astype(BF16), preferred_element_type=F32)
                 + jnp.dot(p_n.astype(BF16), vnew[:, hs].astype(BF16), preferred_element_type=F32))
            parts.append(o / den)
        mix_ref[i] = jnp.concatenate(parts, axis=1)
        ko_ref[i, 0:w_buf - seq, :] = kc[seq:, :]
        ko_ref[i, w_buf - seq:w_buf, :] = knew
        vo_ref[i, 0:w_buf - seq, :] = vc[seq:, :]
        vo_ref[i, w_buf - seq:w_buf, :] = vnew
        return carry

    lax.fori_loop(0, block, body, 0)


def _sample_mixers(proj3, sinks, cache_conv, state, cache_k, cache_v, conv_w, conv_b, ln_g, ln_b, hgrn_lb, hng,
                   bias_s, layer, carried):
    B, seq = proj3.shape[:2]
    w_buf = cache_k.shape[2]
    block = SAMPLE_BLOCK
    depth = hgrn_lb.shape[0]
    hist = CONV_WIDTH - 1
    const2 = lambda i: (0, 0)
    cache_specs = [pl.BlockSpec((None, block, hist, D_CONV), lambda i: (layer, i, 0, 0)),
                   pl.BlockSpec((None, block, H_HGRN, DK_HGRN, DV_HGRN), lambda i: (layer, i, 0, 0, 0)),
                   pl.BlockSpec((None, block, w_buf, D_KV), lambda i: (layer, i, 0, 0)),
                   pl.BlockSpec((None, block, w_buf, D_KV), lambda i: (layer, i, 0, 0))]
    inputs = (sinks, proj3, cache_conv, state, cache_k, cache_v, conv_w, conv_b.reshape(1, D_CONV),
              ln_g.reshape(1, D_CONV), ln_b.reshape(1, D_CONV), hgrn_lb, hng.reshape(1, DV_HGRN), bias_s)
    return pl.pallas_call(
        _carried(functools.partial(_sample_mix_kernel, layer=layer, block=block, seq=seq, w_buf=w_buf),
                 len(inputs), len(carried)),
        out_shape=(jax.ShapeDtypeStruct((B, seq, D_MODEL), F32),
                   jax.ShapeDtypeStruct((depth, B, hist, D_CONV), F32),
                   jax.ShapeDtypeStruct((depth, B, H_HGRN, DK_HGRN, DV_HGRN), F32),
                   jax.ShapeDtypeStruct((depth, B, w_buf, D_KV), F32),
                   jax.ShapeDtypeStruct((depth, B, w_buf, D_KV), F32)),
        grid=(B // block,),
        in_specs=[pl.BlockSpec(memory_space=pltpu.SMEM),
                  pl.BlockSpec((block, seq, IN_WIDTH), lambda i: (i, 0, 0))] + cache_specs + [
                  pl.BlockSpec((CONV_WIDTH, D_CONV), const2),
                  pl.BlockSpec((1, D_CONV), const2),
                  pl.BlockSpec((1, D_CONV), const2),
                  pl.BlockSpec((1, D_CONV), const2),
                  pl.BlockSpec((depth, D_HGRN), const2),
                  pl.BlockSpec((1, DV_HGRN), const2),
                  pl.BlockSpec((H_ATTN, 8, 2 * ATTN_BLOCK), lambda i: (0, 0, 0))] + _carry_specs(carried),
        out_specs=tuple([pl.BlockSpec((block, seq, D_MODEL), lambda i: (i, 0, 0))] + cache_specs),
        input_output_aliases={len(inputs) + i: 1 + i for i in range(len(carried))},
        scratch_shapes=[pltpu.VMEM((hist + seq + 6, D_CONV), F32)],
        compiler_params=pltpu.CompilerParams(dimension_semantics=("arbitrary",), vmem_limit_bytes=VMEM_LIMIT),
        name="sample_mixers",
    )(*inputs, *carried)


def kernel(x_prompt, x_sample, cache_conv, state_hgrn, cache_swa_k, cache_swa_v, c_prompt, c_sample, rel_bias, w_ada, b_ada, norm_mix_g, w_in, conv_w, conv_b, conv_ln_g, conv_ln_b, hgrn_lb, hgrn_norm_g, attn_sinks, w_out, norm_mlp_g, w_up, w_down, final_g):
    Bp, Tp = x_prompt.shape[:2]
    Bs, Ts = x_sample.shape[:2]
    depth = w_in.shape[0]
    w_buf = cache_swa_k.shape[2]
    assert Tp % MIX_TILE == 0 and (Bp * Tp) % TOK_TILE == 0 and Tp % TOK_TILE == 0 and Bs % SAMPLE_BLOCK == 0
    assert w_buf == WINDOW and Ts <= 8

    bias_p, bias_s = _bias_tables(rel_bias, Ts, w_buf)
    mod = _modulation(jnp.concatenate([c_prompt, c_sample], axis=0), w_ada.astype(BF16), b_ada)
    w_in_b, w_out_b, w_up_b, w_down_b = (w.astype(BF16) for w in (w_in, w_out, w_up, w_down))
    hlb = hgrn_lb.astype(F32)
    ck = cache_swa_k.reshape(depth, Bs, w_buf, D_KV)
    cv = cache_swa_v.reshape(depth, Bs, w_buf, D_KV)

    xp = x_prompt.reshape(Bp * Tp, D_MODEL)
    xs = x_sample.reshape(Bs * Ts, D_MODEL)
    tile_s = Bs * Ts
    caches_p = ()
    caches_s = ()
    for l in range(depth):
        final = l == depth - 1
        mod_p = mod[l, :Bp].reshape(Bp, 1, N_MOD * D_MODEL)
        mod_s = jnp.repeat(mod[l, Bp:], Ts, axis=0)
        proj_p = _inproj(xp, mod_p, norm_mix_g[l], w_in_b, l, TOK_TILE, Tp // TOK_TILE)
        mix_p, *caches_p = _prompt_mixers(proj_p.reshape(Bp, Tp, IN_WIDTH), attn_sinks[l], conv_w[l], conv_b[l],
                                          conv_ln_g[l], conv_ln_b[l], hlb, hgrn_norm_g[l], bias_p, l, caches_p)
        xp = _out_mlp(mix_p.reshape(Bp * Tp, D_MODEL), xp, mod_p, norm_mlp_g[l], w_out_b, w_up_b, w_down_b,
                      final_g, l, TOK_TILE, Tp // TOK_TILE, final)
        proj_s = _inproj(xs, mod_s, norm_mix_g[l], w_in_b, l, tile_s, 1)
        mix_s, *caches_s = _sample_mixers(proj_s.reshape(Bs, Ts, IN_WIDTH), attn_sinks[l], cache_conv, state_hgrn,
                                          ck, cv, conv_w[l], conv_b[l], conv_ln_g[l], conv_ln_b[l], hlb,
                                          hgrn_norm_g[l], bias_s, l, caches_s)
        xs = _out_mlp(mix_s.reshape(Bs * Ts, D_MODEL), xs, mod_s, norm_mlp_g[l], w_out_b, w_up_b, w_down_b,
                      final_g, l, tile_s, 1, final)
    cp, sp, kp, vp = caches_p
    cs, ss, ksn, vsn = caches_s
    return (xp.reshape(Bp, Tp, D_MODEL), xs.reshape(Bs, Ts, D_MODEL), cp, cs, sp, ss,
            kp.reshape(depth, Bp, WINDOW, KV_HEADS, HEAD_DIM), ksn.reshape(depth, Bs, w_buf, KV_HEADS, HEAD_DIM),
            vp.reshape(depth, Bp, WINDOW, KV_HEADS, HEAD_DIM), vsn.reshape(depth, Bs, w_buf, KV_HEADS, HEAD_DIM))
```

```python
import functools
import math

import jax
import jax.numpy as jnp
from jax import lax
from jax.experimental import pallas as pl
from jax.experimental.pallas import tpu as pltpu

F32 = jnp.float32
BF16 = jnp.bfloat16

D_MODEL = 1024
D_CONV = 256
CONV_WIDTH = 31
H_HGRN = 4
DK_HGRN = 128
DV_HGRN = 128
D_HGRN = 512
HEAD_DIM = 64
H_ATTN = 4
KV_HEADS = 2
GROUP = H_ATTN // KV_HEADS
D_ATTN = H_ATTN * HEAD_DIM
D_KV = KV_HEADS * HEAD_DIM
WINDOW = 128
ATTN_BLOCK = 128
NUM_BUCKETS = 32
MAX_DISTANCE = 128
D_FF = 4 * D_MODEL
N_MOD = 6
EPS = 1e-6

OFF_AVAL = 0
OFF_AGATE = OFF_AVAL + D_CONV
OFF_Q = OFF_AGATE + D_CONV
OFF_F = OFF_Q + H_HGRN * DK_HGRN
OFF_I = OFF_F + H_HGRN * DK_HGRN
OFF_G = OFF_I + D_HGRN
OFF_QA = OFF_G + D_HGRN
OFF_KA = OFF_QA + D_ATTN
OFF_VA = OFF_KA + D_KV
IN_WIDTH = OFF_VA + D_KV

HGRN_CHUNK = 64
HGRN_KEYBLOCK = 32
HGRN_SPAN = 4
SUBLANES = 8
CONV_PAD = 32
MIX_TILE = 256
TOK_TILE = 512
SAMPLE_BLOCK = 8
FF_CHUNK = 1024
VMEM_LIMIT = 56 * 1024 * 1024

NT_DIMS = (((1,), (1,)), ((), ()))
TN_DIMS = (((0,), (0,)), ((), ()))


def _silu(x):
    return x * jax.nn.sigmoid(x)


def _rms_rows(x):
    return x * lax.rsqrt(jnp.mean(x * x, axis=-1, keepdims=True) + EPS)


def _layer_lb(hlb, layer):
    m = jnp.max(hlb, axis=0, keepdims=True)
    e = jnp.exp(hlb - m)
    p = e / jnp.sum(e, axis=0, keepdims=True)
    lb = jnp.zeros_like(m)
    for i in range(1, layer + 1):
        lb = lb + p[i:i + 1, :]
    return lb


def _split3_bf16(x):
    hi = x.astype(BF16)
    r = x - hi.astype(F32)
    mid = r.astype(BF16)
    return hi, mid, (r - mid.astype(F32)).astype(BF16)


def _cumsum_rows_mxu(g, tri):
    return sum(jnp.dot(tri, part, preferred_element_type=F32) for part in _split3_bf16(g))


def _cumsum_rows_small(g):
    row = lax.broadcasted_iota(jnp.int32, g.shape, 0)
    b = jnp.zeros_like(g)
    for u in range(g.shape[0]):
        b = b + jnp.where(row >= u, g[u:u + 1, :], 0.0)
    return b


def _hgrn_span(proj_ref, mix_ref, st_ref, row0, lb, hng, tri):
    L, KB = HGRN_CHUNK, HGRN_KEYBLOCK
    span = HGRN_SPAN * L
    g, k = _hgrn_gates(proj_ref[pl.ds(row0, span), OFF_F:OFF_F + D_HGRN], lb)
    b = _cumsum_rows_mxu(g, tri)
    units = [(c, h) for c in range(HGRN_SPAN) for h in range(H_HGRN)]

    ops = {}
    for c, h in units:
        rows = pl.ds(row0 + c * L, L)
        cs = slice(h * DK_HGRN, (h + 1) * DK_HGRN)
        q = proj_ref[rows, OFF_Q + h * DK_HGRN:OFF_Q + (h + 1) * DK_HGRN]
        v = proj_ref[rows, OFF_I + h * DV_HGRN:OFF_I + (h + 1) * DV_HGRN].astype(BF16)
        kk = k[c * L:(c + 1) * L, cs]
        bb = b[c * L:(c + 1) * L, cs]
        qp, kp = [], []
        for lo in range(0, L, KB):
            r = bb[lo + KB // 2 - 1:lo + KB // 2, :]
            kp.append((kk[lo:lo + KB] * jnp.exp(r - bb[lo:lo + KB])).astype(BF16))
            qp.append((q[lo:] * jnp.exp(bb[lo:] - r)).astype(BF16))
        bl = bb[L - 1:L, :]
        ops[c, h] = dict(qp=qp, kp=kp, v=v, qt=(q * jnp.exp(bb)).astype(BF16),
                         kst=(kk * jnp.exp(bl - bb)).astype(BF16), e=jnp.exp(bl))

    for u in units:
        o = ops[u]
        o["p"] = [lax.dot_general(qp, kp, NT_DIMS, preferred_element_type=F32) for qp, kp in zip(o["qp"], o["kp"])]
        o["m"] = lax.dot_general(o["v"], o["kst"], TN_DIMS, preferred_element_type=F32)

    for u in units:
        pm = []
        for p in ops[u]["p"]:
            row = lax.broadcasted_iota(jnp.int32, p.shape, 0)
            col = lax.broadcasted_iota(jnp.int32, p.shape, 1)
            pm.append(jnp.where(row >= col, p, 0.0).astype(BF16))
        ops[u]["p"] = pm

    for u in units:
        o = ops[u]
        blocks = [None] * (L // KB)
        for j, p in enumerate(o["p"]):
            cj = jnp.dot(p, o["v"][j * KB:(j + 1) * KB], preferred_element_type=F32)
            for i in range(j, L // KB):
                piece = cj[(i - j) * KB:(i - j + 1) * KB]
                blocks[i] = piece if blocks[i] is None else blocks[i] + piece
        o["o"] = jnp.concatenate(blocks, axis=0)

    for h in range(H_HGRN):
        st = st_ref[h]
        for c in range(HGRN_SPAN):
            o = ops[c, h]
            out = o["o"] + lax.dot_general(o["qt"], st.astype(BF16), NT_DIMS, preferred_element_type=F32)
            st = o["e"] * st + o["m"]
            rows = pl.ds(row0 + c * L, L)
            gate = proj_ref[rows, OFF_G + h * DV_HGRN:OFF_G + (h + 1) * DV_HGRN]
            mix_ref[rows, D_CONV + h * DV_HGRN:D_CONV + (h + 1) * DV_HGRN] = _hgrn_out(out, hng, gate).astype(BF16)
        st_ref[h] = st


def _sink_softmax(s, sink):
    m = jnp.maximum(jnp.max(s, axis=-1, keepdims=True), sink)
    p = jnp.exp(s - m)
    return p, jnp.sum(p, axis=-1, keepdims=True) + jnp.exp(sink - m)


def _bias_kernel(tab_ref, bp_ref, bs_ref, op_ref, os_ref, *, seq):
    bk = bp_ref[...]
    for h in range(H_ATTN):
        acc = jnp.full(bk.shape, -jnp.inf, F32)
        for bkt in range(NUM_BUCKETS):
            acc = jnp.where(bk == bkt, tab_ref[bkt, h], acc)
        op_ref[h] = acc
    bk = bs_ref[...]
    row = lax.broadcasted_iota(jnp.int32, bk.shape, 0)
    for kv in range(KV_HEADS):
        acc = jnp.full(bk.shape, -jnp.inf, F32)
        for bkt in range(NUM_BUCKETS):
            val = jnp.full(bk.shape, tab_ref[bkt, kv * GROUP], F32)
            for gi in range(1, GROUP):
                val = jnp.where(row >= gi * seq, tab_ref[bkt, kv * GROUP + gi], val)
            acc = jnp.where(bk == bkt, val, acc)
        os_ref[kv] = acc


def _t5_bucket(rel):
    n = jnp.maximum(rel, 0)
    max_exact = NUM_BUCKETS // 2
    nf = jnp.maximum(n, max_exact).astype(F32)
    large = max_exact + (jnp.log(nf / max_exact) / math.log(MAX_DISTANCE / max_exact)
                         * (NUM_BUCKETS - max_exact)).astype(jnp.int32)
    large = jnp.minimum(large, NUM_BUCKETS - 1)
    return jnp.where(n < max_exact, n, large)


def _bias_tables(rel_bias, dec_seq, w_buf):
    qi = jnp.arange(ATTN_BLOCK, dtype=jnp.int32)[:, None]
    kc = jnp.arange(2 * ATTN_BLOCK, dtype=jnp.int32)[None, :]
    rel_p = qi + ATTN_BLOCK - kc
    bucket_p = jnp.where((rel_p >= 0) & (rel_p <= WINDOW), _t5_bucket(rel_p), -1)
    ts = (jnp.arange(GROUP * dec_seq, dtype=jnp.int32) % dec_seq)[:, None]
    js = jnp.arange(2 * ATTN_BLOCK, dtype=jnp.int32)[None, :]
    rel_s = w_buf + ts - js
    ok_s = (rel_s >= 0) & (rel_s <= WINDOW) & (js < w_buf + dec_seq)
    bucket_s = jnp.where(ok_s, _t5_bucket(rel_s), -1)
    return pl.pallas_call(
        functools.partial(_bias_kernel, seq=dec_seq),
        out_shape=(jax.ShapeDtypeStruct((H_ATTN, ATTN_BLOCK, 2 * ATTN_BLOCK), F32),
                   jax.ShapeDtypeStruct((KV_HEADS, GROUP * dec_seq, 2 * ATTN_BLOCK), F32)),
        in_specs=[pl.BlockSpec(memory_space=pltpu.SMEM),
                  pl.BlockSpec(memory_space=pltpu.VMEM),
                  pl.BlockSpec(memory_space=pltpu.VMEM)],
        out_specs=(pl.BlockSpec(memory_space=pltpu.VMEM), pl.BlockSpec(memory_space=pltpu.VMEM)),
        name="rel_bias_tables",
    )(rel_bias.astype(F32), bucket_p, bucket_s)


def _mod_kernel(c_ref, w_ref, b_ref, o_ref):
    s = _silu(c_ref[...]).astype(BF16)
    o_ref[...] = jnp.dot(s, w_ref[...], preferred_element_type=F32) + b_ref[...]


def _modulation(c_all, w_ada, b_ada):
    depth = w_ada.shape[0]
    n = c_all.shape[0]
    return pl.pallas_call(
        _mod_kernel,
        out_shape=jax.ShapeDtypeStruct((depth, n, N_MOD * D_MODEL), F32),
        grid=(depth, N_MOD),
        in_specs=[pl.BlockSpec((n, D_MODEL), lambda l, j: (0, 0)),
                  pl.BlockSpec((None, D_MODEL, D_MODEL), lambda l, j: (l, 0, j)),
                  pl.BlockSpec((None, 1, D_MODEL), lambda l, j: (l, 0, j))],
        out_specs=pl.BlockSpec((None, n, D_MODEL), lambda l, j: (l, 0, j)),
        compiler_params=pltpu.CompilerParams(dimension_semantics=("arbitrary", "arbitrary"),
                                             vmem_limit_bytes=VMEM_LIMIT),
        name="adaln_modulation",
    )(c_all, w_ada, b_ada.reshape(depth, 1, N_MOD * D_MODEL))


def _mod_spec(mod, chunk, tile, tiles_per_batch):
    if mod.ndim == 3:
        return pl.BlockSpec((None, 1, D_MODEL), lambda i: (i // tiles_per_batch, 0, chunk))
    return pl.BlockSpec((tile, D_MODEL), lambda i: (i, chunk))


def _inproj_kernel(x_ref, sh_ref, sc_ref, g_ref, w_ref, o_ref):
    h = _rms_rows(x_ref[...]) * g_ref[...] * (1.0 + sc_ref[...]) + sh_ref[...]
    o_ref[...] = jnp.dot(h.astype(BF16), w_ref[...], preferred_element_type=F32)


def _inproj(x2, mod, norm_g, w_in, layer, tile, tiles_per_batch):
    n = x2.shape[0]
    return pl.pallas_call(
        _inproj_kernel,
        out_shape=jax.ShapeDtypeStruct((n, IN_WIDTH), F32),
        grid=(n // tile,),
        in_specs=[pl.BlockSpec((tile, D_MODEL), lambda i: (i, 0)),
                  _mod_spec(mod, 0, tile, tiles_per_batch),
                  _mod_spec(mod, 1, tile, tiles_per_batch),
                  pl.BlockSpec((1, D_MODEL), lambda i: (0, 0)),
                  pl.BlockSpec((None, D_MODEL, IN_WIDTH), lambda i: (layer, 0, 0), pipeline_mode=pl.Buffered(1))],
        out_specs=pl.BlockSpec((tile, IN_WIDTH), lambda i: (i, 0)),
        compiler_params=pltpu.CompilerParams(dimension_semantics=("arbitrary",), vmem_limit_bytes=VMEM_LIMIT),
        name="in_projection",
    )(x2, mod, mod, norm_g.reshape(1, D_MODEL), w_in)


def _mlp_kernel(mix_ref, x_ref, g1_ref, sh_ref, sc_ref, g2_ref, ng_ref, wout_ref, wup_ref, wdn_ref, fg_ref, o_ref, *,
                final):
    x1 = x_ref[...] + g1_ref[...] * jnp.dot(mix_ref[...].astype(BF16), wout_ref[...], preferred_element_type=F32)
    h = (_rms_rows(x1) * ng_ref[...] * (1.0 + sc_ref[...]) + sh_ref[...]).astype(BF16)
    acc = None
    for c in range(D_FF // FF_CHUNK):
        u = jnp.dot(h, wup_ref[:, c * FF_CHUNK:(c + 1) * FF_CHUNK], preferred_element_type=F32)
        u = jnp.square(jnp.maximum(u, 0.0)).astype(BF16)
        d = jnp.dot(u, wdn_ref[c * FF_CHUNK:(c + 1) * FF_CHUNK, :], preferred_element_type=F32)
        acc = d if acc is None else acc + d
    x2 = x1 + g2_ref[...] * acc
    if final:
        x2 = _rms_rows(x2) * fg_ref[...]
    o_ref[...] = x2


def _out_mlp(mix2, x2, mod, norm_g, w_out, w_up, w_down, final_g, layer, tile, tiles_per_batch, final):
    n = x2.shape[0]
    const = lambda i: (0, 0)
    of_layer = lambda i: (layer, 0, 0)
    return pl.pallas_call(
        functools.partial(_mlp_kernel, final=final),
        out_shape=jax.ShapeDtypeStruct((n, D_MODEL), F32),
        grid=(n // tile,),
        in_specs=[pl.BlockSpec((tile, D_MODEL), lambda i: (i, 0)),
                  pl.BlockSpec((tile, D_MODEL), lambda i: (i, 0)),
                  _mod_spec(mod, 2, tile, tiles_per_batch),
                  _mod_spec(mod, 3, tile, tiles_per_batch),
                  _mod_spec(mod, 4, tile, tiles_per_batch),
                  _mod_spec(mod, 5, tile, tiles_per_batch),
                  pl.BlockSpec((1, D_MODEL), const),
                  pl.BlockSpec((None, D_MODEL, D_MODEL), of_layer, pipeline_mode=pl.Buffered(1)),
                  pl.BlockSpec((None, D_MODEL, D_FF), of_layer, pipeline_mode=pl.Buffered(1)),
                  pl.BlockSpec((None, D_FF, D_MODEL), of_layer, pipeline_mode=pl.Buffered(1)),
                  pl.BlockSpec((1, D_MODEL), const)],
        out_specs=pl.BlockSpec((tile, D_MODEL), lambda i: (i, 0)),
        compiler_params=pltpu.CompilerParams(dimension_semantics=("arbitrary",), vmem_limit_bytes=VMEM_LIMIT),
        name="out_projection_mlp",
    )(mix2, x2, mod, mod, mod, mod, norm_g.reshape(1, D_MODEL), w_out, w_up, w_down, final_g.reshape(1, D_MODEL))


def _conv_ln_swish(acc, lng, lnb):
    mu = jnp.mean(acc, axis=-1, keepdims=True)
    xc = acc - mu
    y = xc * lax.rsqrt(jnp.mean(xc * xc, axis=-1, keepdims=True) + EPS) * lng + lnb
    return _silu(y)


def _hgrn_gates(fh, lb):
    f = lb + (1.0 - lb) * jax.nn.sigmoid(fh)
    return jnp.log(f), 1.0 - f


def _hgrn_out(o, hng, gate):
    return _rms_rows(o) * hng * _silu(gate)


def _prompt_mix_kernel(sinks_ref, proj_ref, convw_ref, convb_ref, lng_ref, lnb_ref, hlb_ref, hng_ref, bias_ref,
                       mix_ref, convo_ref, so_ref, ko_ref, vo_ref,
                       abuf, kbuf, vbuf, st_ref, *, layer, tile):
    t = pl.program_id(1)
    last = pl.num_programs(1) - 1

    @pl.when(t == 0)
    def _():
        abuf[0:CONV_PAD, :] = jnp.zeros((CONV_PAD, D_CONV), F32)
        abuf[CONV_PAD + tile:CONV_PAD + tile + SUBLANES, :] = jnp.zeros((SUBLANES, D_CONV), F32)
        kbuf[0:ATTN_BLOCK, :] = jnp.zeros((ATTN_BLOCK, D_KV), BF16)
        vbuf[0:ATTN_BLOCK, :] = jnp.zeros((ATTN_BLOCK, D_KV), BF16)
        st_ref[...] = jnp.zeros(st_ref.shape, F32)

    abuf[CONV_PAD:CONV_PAD + tile, :] = (proj_ref[:, OFF_AVAL:OFF_AVAL + D_CONV]
                                         * jax.nn.sigmoid(proj_ref[:, OFF_AGATE:OFF_AGATE + D_CONV]))
    first_row = CONV_PAD - (CONV_WIDTH - 1)
    acc = jnp.broadcast_to(convb_ref[...], (tile, D_CONV))
    for r in range(SUBLANES):
        z = None
        for off in range(r, first_row + CONV_WIDTH, SUBLANES):
            j = off - first_row
            if j < 0:
                continue
            term = convw_ref[j:j + 1, :] * abuf[off - r:off - r + tile + SUBLANES, :]
            z = term if z is None else z + term
        acc = acc + z[r:r + tile]
    mix_ref[:, 0:D_CONV] = _conv_ln_swish(acc, lng_ref[...], lnb_ref[...]).astype(BF16)

    @pl.when(t == last)
    def _():
        convo_ref[...] = abuf[CONV_PAD + tile - (CONV_WIDTH - 1):CONV_PAD + tile, :]

    abuf[0:CONV_PAD, :] = abuf[tile:tile + CONV_PAD, :]

    lb = _layer_lb(hlb_ref[...], layer)
    hng = hng_ref[...]
    span = HGRN_SPAN * HGRN_CHUNK
    ri = lax.broadcasted_iota(jnp.int32, (span, span), 0)
    ci = lax.broadcasted_iota(jnp.int32, (span, span), 1)
    tri = jnp.where((ri >= ci) & (ri // HGRN_CHUNK == ci // HGRN_CHUNK), 1.0, 0.0).astype(BF16)

    def span_body(i, carry):
        _hgrn_span(proj_ref, mix_ref, st_ref, pl.multiple_of(i * span, span), lb, hng, tri)
        return carry

    lax.fori_loop(0, tile // span, span_body, 0)

    @pl.when(t == last)
    def _():
        for h in range(H_HGRN):
            so_ref[h] = st_ref[h].T

    kbuf[ATTN_BLOCK:ATTN_BLOCK + tile, :] = proj_ref[:, OFF_KA:OFF_KA + D_KV].astype(BF16)
    vbuf[ATTN_BLOCK:ATTN_BLOCK + tile, :] = proj_ref[:, OFF_VA:OFF_VA + D_KV].astype(BF16)
    scale = HEAD_DIM ** -0.5
    for blk in range(tile // ATTN_BLOCK):
        r0 = blk * ATTN_BLOCK
        heads = []
        for h in range(H_ATTN):
            kv = h // GROUP
            q = (proj_ref[r0:r0 + ATTN_BLOCK, OFF_QA + h * HEAD_DIM:OFF_QA + (h + 1) * HEAD_DIM] * scale).astype(BF16)
            kall = kbuf[r0:r0 + 2 * ATTN_BLOCK, kv * HEAD_DIM:(kv + 1) * HEAD_DIM]
            vall = vbuf[r0:r0 + 2 * ATTN_BLOCK, kv * HEAD_DIM:(kv + 1) * HEAD_DIM]
            s = lax.dot_general(q, kall, NT_DIMS, preferred_element_type=F32) + bias_ref[h]
            if blk == 0:
                col = lax.broadcasted_iota(jnp.int32, s.shape, 1)
                s = jnp.where(col + (t * tile - ATTN_BLOCK) >= 0, s, -jnp.inf)
            p, den = _sink_softmax(s, sinks_ref[h])
            heads.append(jnp.dot(p.astype(BF16), vall, preferred_element_type=F32) / den)
        mix_ref[r0:r0 + ATTN_BLOCK, D_CONV + D_HGRN:D_MODEL] = jnp.concatenate(heads, axis=1).astype(BF16)

    @pl.when(t == last)
    def _():
        ko_ref[...] = proj_ref[tile - WINDOW:tile, OFF_KA:OFF_KA + D_KV]
        vo_ref[...] = proj_ref[tile - WINDOW:tile, OFF_VA:OFF_VA + D_KV]

    kbuf[0:ATTN_BLOCK, :] = kbuf[tile:tile + ATTN_BLOCK, :]
    vbuf[0:ATTN_BLOCK, :] = vbuf[tile:tile + ATTN_BLOCK, :]


def _carry_specs(carried):
    return [pl.BlockSpec(memory_space=pl.ANY)] * len(carried)


def _carried(kernel_fn, n_in, n_carried):
    if n_carried == 0:
        return kernel_fn
    return lambda *refs: kernel_fn(*refs[:n_in], *refs[n_in + n_carried:])


def _prompt_mixers(proj, sinks, conv_w, conv_b, ln_g, ln_b, hgrn_lb, hng, bias_p, layer, carried):
    B, T = proj.shape[:2]
    tile = MIX_TILE
    depth = hgrn_lb.shape[0]
    const2 = lambda b, t: (0, 0)
    inputs = (sinks, proj, conv_w, conv_b.reshape(1, D_CONV), ln_g.reshape(1, D_CONV), ln_b.reshape(1, D_CONV),
              hgrn_lb, hng.reshape(1, DV_HGRN), bias_p)
    return pl.pallas_call(
        _carried(functools.partial(_prompt_mix_kernel, layer=layer, tile=tile), len(inputs), len(carried)),
        out_shape=(jax.ShapeDtypeStruct((B, T, D_MODEL), BF16),
                   jax.ShapeDtypeStruct((depth, B, CONV_WIDTH - 1, D_CONV), F32),
                   jax.ShapeDtypeStruct((depth, B, H_HGRN, DK_HGRN, DV_HGRN), F32),
                   jax.ShapeDtypeStruct((depth, B, WINDOW, D_KV), F32),
                   jax.ShapeDtypeStruct((depth, B, WINDOW, D_KV), F32)),
        grid=(B, T // tile),
        in_specs=[pl.BlockSpec(memory_space=pltpu.SMEM),
                  pl.BlockSpec((None, tile, IN_WIDTH), lambda b, t: (b, t, 0)),
                  pl.BlockSpec((CONV_WIDTH, D_CONV), const2),
                  pl.BlockSpec((1, D_CONV), const2),
                  pl.BlockSpec((1, D_CONV), const2),
                  pl.BlockSpec((1, D_CONV), const2),
                  pl.BlockSpec((depth, D_HGRN), const2),
                  pl.BlockSpec((1, DV_HGRN), const2),
                  pl.BlockSpec((H_ATTN, ATTN_BLOCK, 2 * ATTN_BLOCK), lambda b, t: (0, 0, 0))] + _carry_specs(carried),
        out_specs=(pl.BlockSpec((None, tile, D_MODEL), lambda b, t: (b, t, 0)),
                   pl.BlockSpec((None, None, CONV_WIDTH - 1, D_CONV), lambda b, t: (layer, b, 0, 0)),
                   pl.BlockSpec((None, None, H_HGRN, DK_HGRN, DV_HGRN), lambda b, t: (layer, b, 0, 0, 0)),
                   pl.BlockSpec((None, None, WINDOW, D_KV), lambda b, t: (layer, b, 0, 0)),
                   pl.BlockSpec((None, None, WINDOW, D_KV), lambda b, t: (layer, b, 0, 0))),
        input_output_aliases={len(inputs) + i: 1 + i for i in range(len(carried))},
        scratch_shapes=[pltpu.VMEM((CONV_PAD + tile + SUBLANES, D_CONV), F32),
                        pltpu.VMEM((ATTN_BLOCK + tile, D_KV), BF16),
                        pltpu.VMEM((ATTN_BLOCK + tile, D_KV), BF16),
                        pltpu.VMEM((H_HGRN, DV_HGRN, DK_HGRN), F32)],
        compiler_params=pltpu.CompilerParams(dimension_semantics=("arbitrary", "arbitrary"),
                                             vmem_limit_bytes=VMEM_LIMIT),
        name="prompt_mixers",
    )(*inputs, *carried)


def _sample_mix_kernel(sinks_ref, proj_ref, cconv_ref, state_ref, ck_ref, cv_ref, convw_ref, convb_ref, lng_ref,
                       lnb_ref, hlb_ref, hng_ref, bias_ref,
                       mix_ref, convo_ref, so_ref, ko_ref, vo_ref, full_ref, kall_ref, vall_ref, *,
                       layer, block, seq, w_buf):
    hist = CONV_WIDTH - 1
    n_keys = w_buf + seq
    pad_keys = kall_ref.shape[1]

    @pl.when(pl.program_id(0) == 0)
    def _():
        for ref in (kall_ref, vall_ref):
            ref[:, w_buf:pad_keys, :] = jnp.zeros((block, pad_keys - w_buf, D_KV), F32)

    lb = _layer_lb(hlb_ref[...], layer)
    hng = hng_ref[...]
    scale = HEAD_DIM ** -0.5
    elems = range(block)
    row8 = lax.broadcasted_iota(jnp.int32, (SUBLANES, DV_HGRN), 0)
    ones_rows = jnp.where((row8 >= seq) & (row8 < seq + 3), 1.0, 0.0)
    zrow = jnp.zeros((1, DK_HGRN), BF16)
    prow = lax.broadcasted_iota(jnp.int32, (seq, seq), 0)
    pcol = lax.broadcasted_iota(jnp.int32, (seq, seq), 1)
    grow = lax.broadcasted_iota(jnp.int32, (GROUP * seq, 1), 0)

    proj = [proj_ref[e * seq:(e + 1) * seq, :] for e in elems]

    out_a = []
    for e in elems:
        p = proj[e]
        full_ref[e, 0:hist, :] = cconv_ref[e]
        full_ref[e, hist:hist + seq, :] = (p[:, OFF_AVAL:OFF_AVAL + D_CONV]
                                           * jax.nn.sigmoid(p[:, OFF_AGATE:OFF_AGATE + D_CONV]))
        rows = [jnp.sum(convw_ref[...] * full_ref[e, s:s + CONV_WIDTH, :], axis=0, keepdims=True) for s in range(seq)]
        convo_ref[e] = full_ref[e, seq:seq + hist, :]
        out_a.append(_conv_ln_swish(jnp.concatenate(rows, axis=0) + convb_ref[...], lng_ref[...], lnb_ref[...]))

    units = [(e, h) for e in elems for h in range(H_HGRN)]
    ops = {}
    for e in elems:
        p = proj[e]
        g, k = _hgrn_gates(p[:, OFF_F:OFF_F + D_HGRN], lb)
        b = _cumsum_rows_small(g)
        for h in range(H_HGRN):
            cs = slice(h * DK_HGRN, (h + 1) * DK_HGRN)
            q = p[:, OFF_Q + h * DK_HGRN:OFF_Q + (h + 1) * DK_HGRN]
            v = p[:, OFF_I + h * DV_HGRN:OFF_I + (h + 1) * DV_HGRN]
            bb = b[:, cs]
            bl = bb[seq - 1:seq, :]
            kst = (k[:, cs] * jnp.exp(bl - bb)).astype(BF16)
            x = jnp.concatenate([kst.astype(F32), *(part.astype(F32) for part in _split3_bf16(jnp.exp(bl))),
                                 zrow.astype(F32)], axis=0).astype(BF16)
            vpad = jnp.concatenate([v, jnp.zeros((SUBLANES - seq, DV_HGRN), F32)], axis=0)
            ops[e, h] = dict(qp=(q * jnp.exp(bb - bl)).astype(BF16), kst=kst, v=v.astype(BF16),
                             qt=(q * jnp.exp(bb)).astype(BF16), x=x,
                             r=jnp.concatenate([vpad, ones_rows], axis=1).astype(BF16))
    for u in units:
        o = ops[u]
        st = state_ref[u[0], u[1]]
        o["p"] = lax.dot_general(o["qp"], o["kst"], NT_DIMS, preferred_element_type=F32)
        o["inter"] = jnp.dot(o["qt"], st.astype(BF16), preferred_element_type=F32)
        me = lax.dot_general(o["x"], o["r"], TN_DIMS, preferred_element_type=F32)
        so_ref[u[0], u[1]] = me[:, DV_HGRN:] * st + me[:, :DV_HGRN]
    out_b = {}
    for u in units:
        o = ops[u]
        pm = jnp.where(prow >= pcol, o["p"], 0.0).astype(BF16)
        out = o["inter"] + jnp.dot(pm, o["v"], preferred_element_type=F32)
        gate = proj[u[0]][:, OFF_G + u[1] * DV_HGRN:OFF_G + (u[1] + 1) * DV_HGRN]
        out_b[u] = _hgrn_out(out, hng, gate)

    for e in elems:
        p = proj[e]
        for ref, cache, new, out in ((kall_ref, ck_ref, p[:, OFF_KA:OFF_KA + D_KV], ko_ref),
                                     (vall_ref, cv_ref, p[:, OFF_VA:OFF_VA + D_KV], vo_ref)):
            ref[e, 0:w_buf, :] = cache[e]
            ref[e, w_buf:n_keys, :] = new
            out[e] = ref[e, seq:n_keys, :]
    scores = {}
    for e in elems:
        p = proj[e]
        for kv in range(KV_HEADS):
            hs = slice(kv * HEAD_DIM, (kv + 1) * HEAD_DIM)
            q2 = jnp.concatenate([p[:, OFF_QA + h * HEAD_DIM:OFF_QA + (h + 1) * HEAD_DIM]
                                  for h in range(kv * GROUP, (kv + 1) * GROUP)], axis=0)
            kall = kall_ref[e][:, hs].astype(BF16)
            scores[e, kv] = (lax.dot_general((q2 * scale).astype(BF16), kall, NT_DIMS, preferred_element_type=F32)
                             + bias_ref[kv])
    out_c = {}
    for e in elems:
        for kv in range(KV_HEADS):
            hs = slice(kv * HEAD_DIM, (kv + 1) * HEAD_DIM)
            sink = jnp.zeros((GROUP * seq, 1), F32)
            for gi in range(GROUP):
                sink = jnp.where(grow >= gi * seq, sinks_ref[kv * GROUP + gi], sink)
            pr, den = _sink_softmax(scores[e, kv], sink)
            o2 = jnp.dot(pr.astype(BF16), vall_ref[e][:, hs].astype(BF16), preferred_element_type=F32) / den
            for gi in range(GROUP):
                out_c[e, kv * GROUP + gi] = o2[gi * seq:(gi + 1) * seq]

    for e in elems:
        parts = [out_a[e]] + [out_b[e, h] for h in range(H_HGRN)] + [out_c[e, h] for h in range(H_ATTN)]
        mix_ref[e * seq:(e + 1) * seq, :] = jnp.concatenate(parts, axis=1)


def _sample_mixers(proj2, sinks, cache_conv, state, cache_k, cache_v, conv_w, conv_b, ln_g, ln_b, hgrn_lb, hng,
                   bias_s, layer, carried):
    B = state.shape[1]
    seq = proj2.shape[0] // B
    w_buf = cache_k.shape[2]
    block = SAMPLE_BLOCK
    depth = hgrn_lb.shape[0]
    hist = CONV_WIDTH - 1
    const2 = lambda i: (0, 0)
    cache_specs = [pl.BlockSpec((None, block, hist, D_CONV), lambda i: (layer, i, 0, 0)),
                   pl.BlockSpec((None, block, H_HGRN, DK_HGRN, DV_HGRN), lambda i: (layer, i, 0, 0, 0)),
                   pl.BlockSpec((None, block, w_buf, D_KV), lambda i: (layer, i, 0, 0)),
                   pl.BlockSpec((None, block, w_buf, D_KV), lambda i: (layer, i, 0, 0))]
    inputs = (sinks, proj2, cache_conv, state, cache_k, cache_v, conv_w, conv_b.reshape(1, D_CONV),
              ln_g.reshape(1, D_CONV), ln_b.reshape(1, D_CONV), hgrn_lb, hng.reshape(1, DV_HGRN), bias_s)
    return pl.pallas_call(
        _carried(functools.partial(_sample_mix_kernel, layer=layer, block=block, seq=seq, w_buf=w_buf),
                 len(inputs), len(carried)),
        out_shape=(jax.ShapeDtypeStruct((B * seq, D_MODEL), F32),
                   jax.ShapeDtypeStruct((depth, B, hist, D_CONV), F32),
                   jax.ShapeDtypeStruct((depth, B, H_HGRN, DK_HGRN, DV_HGRN), F32),
                   jax.ShapeDtypeStruct((depth, B, w_buf, D_KV), F32),
                   jax.ShapeDtypeStruct((depth, B, w_buf, D_KV), F32)),
        grid=(B // block,),
        in_specs=[pl.BlockSpec(memory_space=pltpu.SMEM),
                  pl.BlockSpec((block * seq, IN_WIDTH), lambda i: (i, 0))] + cache_specs + [
                  pl.BlockSpec((CONV_WIDTH, D_CONV), const2),
                  pl.BlockSpec((1, D_CONV), const2),
                  pl.BlockSpec((1, D_CONV), const2),
                  pl.BlockSpec((1, D_CONV), const2),
                  pl.BlockSpec((depth, D_HGRN), const2),
                  pl.BlockSpec((1, DV_HGRN), const2),
                  pl.BlockSpec((KV_HEADS, GROUP * seq, 2 * ATTN_BLOCK), lambda i: (0, 0, 0))] + _carry_specs(carried),
        out_specs=tuple([pl.BlockSpec((block * seq, D_MODEL), lambda i: (i, 0))] + cache_specs),
        input_output_aliases={len(inputs) + i: 1 + i for i in range(len(carried))},
        scratch_shapes=[pltpu.VMEM((block, hist + seq + 6, D_CONV), F32),
                        pltpu.VMEM((block, 2 * ATTN_BLOCK, D_KV), F32),
                        pltpu.VMEM((block, 2 * ATTN_BLOCK, D_KV), F32)],
        compiler_params=pltpu.CompilerParams(dimension_semantics=("arbitrary",), vmem_limit_bytes=VMEM_LIMIT),
        name="sample_mixers",
    )(*inputs, *carried)


def kernel(x_prompt, x_sample, cache_conv, state_hgrn, cache_swa_k, cache_swa_v, c_prompt, c_sample, rel_bias, w_ada, b_ada, norm_mix_g, w_in, conv_w, conv_b, conv_ln_g, conv_ln_b, hgrn_lb, hgrn_norm_g, attn_sinks, w_out, norm_mlp_g, w_up, w_down, final_g):
    Bp, Tp = x_prompt.shape[:2]
    Bs, Ts = x_sample.shape[:2]
    depth = w_in.shape[0]
    w_buf = cache_swa_k.shape[2]
    assert Tp % MIX_TILE == 0 and (Bp * Tp) % TOK_TILE == 0 and Tp % TOK_TILE == 0 and Bs % SAMPLE_BLOCK == 0
    assert w_buf == WINDOW and GROUP * Ts == SUBLANES

    bias_p, bias_s = _bias_tables(rel_bias, Ts, w_buf)
    mod = _modulation(jnp.concatenate([c_prompt, c_sample], axis=0), w_ada.astype(BF16), b_ada)
    w_in_b, w_out_b, w_up_b, w_down_b = (w.astype(BF16) for w in (w_in, w_out, w_up, w_down))
    hlb = hgrn_lb.astype(F32)
    ck = cache_swa_k.reshape(depth, Bs, w_buf, D_KV)
    cv = cache_swa_v.reshape(depth, Bs, w_buf, D_KV)

    xp = x_prompt.reshape(Bp * Tp, D_MODEL)
    xs = x_sample.reshape(Bs * Ts, D_MODEL)
    tile_s = Bs * Ts
    caches_p = ()
    caches_s = ()
    for l in range(depth):
        final = l == depth - 1
        mod_p = mod[l, :Bp].reshape(Bp, 1, N_MOD * D_MODEL)
        mod_s = jnp.repeat(mod[l, Bp:], Ts, axis=0)
        proj_p = _inproj(xp, mod_p, norm_mix_g[l], w_in_b, l, TOK_TILE, Tp // TOK_TILE)
        mix_p, *caches_p = _prompt_mixers(proj_p.reshape(Bp, Tp, IN_WIDTH), attn_sinks[l], conv_w[l], conv_b[l],
                                          conv_ln_g[l], conv_ln_b[l], hlb, hgrn_norm_g[l], bias_p, l, caches_p)
        xp = _out_mlp(mix_p.reshape(Bp * Tp, D_MODEL), xp, mod_p, norm_mlp_g[l], w_out_b, w_up_b, w_down_b,
                      final_g, l, TOK_TILE, Tp // TOK_TILE, final)
        proj_s = _inproj(xs, mod_s, norm_mix_g[l], w_in_b, l, tile_s, 1)
        mix_s, *caches_s = _sample_mixers(proj_s, attn_sinks[l], cache_conv, state_hgrn, ck, cv, conv_w[l],
                                          conv_b[l], conv_ln_g[l], conv_ln_b[l], hlb, hgrn_norm_g[l], bias_s, l,
                                          caches_s)
        xs = _out_mlp(mix_s, xs, mod_s, norm_mlp_g[l], w_out_b, w_up_b, w_down_b, final_g, l, tile_s, 1, final)
    cp, sp, kp, vp = caches_p
    cs, ss, ksn, vsn = caches_s
    return (xp.reshape(Bp, Tp, D_MODEL), xs.reshape(Bs, Ts, D_MODEL), cp, cs, sp, ss,
            kp.reshape(depth, Bp, WINDOW, KV_HEADS, HEAD_DIM), ksn.reshape(depth, Bs, w_buf, KV_HEADS, HEAD_DIM),
            vp.reshape(depth, Bp, WINDOW, KV_HEADS, HEAD_DIM), vsn.reshape(depth, Bs, w_buf, KV_HEADS, HEAD_DIM))
```

```python
import functools
import math

import jax
import jax.numpy as jnp
from jax import lax
from jax.experimental import pallas as pl
from jax.experimental.pallas import tpu as pltpu

F32 = jnp.float32
BF16 = jnp.bfloat16

D_MODEL = 1024
D_CONV = 256
CONV_WIDTH = 31
H_HGRN = 4
DK_HGRN = 128
DV_HGRN = 128
D_HGRN = 512
HEAD_DIM = 64
H_ATTN = 4
KV_HEADS = 2
GROUP = H_ATTN // KV_HEADS
D_ATTN = H_ATTN * HEAD_DIM
D_KV = KV_HEADS * HEAD_DIM
WINDOW = 128
ATTN_BLOCK = 128
NUM_BUCKETS = 32
MAX_DISTANCE = 128
D_FF = 4 * D_MODEL
N_MOD = 6
EPS = 1e-6

OFF_AVAL = 0
OFF_AGATE = OFF_AVAL + D_CONV
OFF_Q = OFF_AGATE + D_CONV
OFF_F = OFF_Q + H_HGRN * DK_HGRN
OFF_I = OFF_F + H_HGRN * DK_HGRN
OFF_G = OFF_I + D_HGRN
OFF_QA = OFF_G + D_HGRN
OFF_KA = OFF_QA + D_ATTN
OFF_VA = OFF_KA + D_KV
IN_WIDTH = OFF_VA + D_KV

HGRN_CHUNK = 64
HGRN_KEYBLOCK = 32
HGRN_SPAN = 4
SUBLANES = 8
CONV_PAD = 32
MIX_TILE = 256
TOK_TILE = 512
SAMPLE_BLOCK = 8
FF_CHUNK = 1024
VMEM_LIMIT = 56 * 1024 * 1024

NT_DIMS = (((1,), (1,)), ((), ()))
TN_DIMS = (((0,), (0,)), ((), ()))


def _silu(x):
    return x * jax.nn.sigmoid(x)


def _rms_rows(x):
    return x * lax.rsqrt(jnp.mean(x * x, axis=-1, keepdims=True) + EPS)


def _layer_lb(hlb, layer):
    m = jnp.max(hlb, axis=0, keepdims=True)
    e = jnp.exp(hlb - m)
    p = e / jnp.sum(e, axis=0, keepdims=True)
    lb = jnp.zeros_like(m)
    for i in range(1, layer + 1):
        lb = lb + p[i:i + 1, :]
    return lb


def _split3_bf16(x):
    hi = x.astype(BF16)
    r = x - hi.astype(F32)
    mid = r.astype(BF16)
    return hi, mid, (r - mid.astype(F32)).astype(BF16)


def _cumsum_rows_mxu(g, tri):
    return sum(jnp.dot(tri, part, preferred_element_type=F32) for part in _split3_bf16(g))


def _cumsum_rows_small(g):
    row = lax.broadcasted_iota(jnp.int32, g.shape, 0)
    b = jnp.zeros_like(g)
    for u in range(g.shape[0]):
        b = b + jnp.where(row >= u, g[u:u + 1, :], 0.0)
    return b


def _hgrn_span(proj_ref, mix_ref, st_ref, row0, lb, hng, tri):
    L, KB = HGRN_CHUNK, HGRN_KEYBLOCK
    span = HGRN_SPAN * L
    g, k = _hgrn_gates(proj_ref[pl.ds(row0, span), OFF_F:OFF_F + D_HGRN], lb)
    b = _cumsum_rows_mxu(g, tri)
    units = [(c, h) for c in range(HGRN_SPAN) for h in range(H_HGRN)]

    ops = {}
    for c, h in units:
        rows = pl.ds(row0 + c * L, L)
        cs = slice(h * DK_HGRN, (h + 1) * DK_HGRN)
        q = proj_ref[rows, OFF_Q + h * DK_HGRN:OFF_Q + (h + 1) * DK_HGRN]
        v = proj_ref[rows, OFF_I + h * DV_HGRN:OFF_I + (h + 1) * DV_HGRN].astype(BF16)
        kk = k[c * L:(c + 1) * L, cs]
        bb = b[c * L:(c + 1) * L, cs]
        qp, kp = [], []
        for lo in range(0, L, KB):
            r = bb[lo + KB // 2 - 1:lo + KB // 2, :]
            kp.append((kk[lo:lo + KB] * jnp.exp(r - bb[lo:lo + KB])).astype(BF16))
            qp.append((q[lo:] * jnp.exp(bb[lo:] - r)).astype(BF16))
        bl = bb[L - 1:L, :]
        ops[c, h] = dict(qp=qp, kp=kp, v=v, qt=(q * jnp.exp(bb)).astype(BF16),
                         kst=(kk * jnp.exp(bl - bb)).astype(BF16), e=jnp.exp(bl))

    for u in units:
        o = ops[u]
        o["p"] = [lax.dot_general(qp, kp, NT_DIMS, preferred_element_type=F32) for qp, kp in zip(o["qp"], o["kp"])]
        o["m"] = lax.dot_general(o["v"], o["kst"], TN_DIMS, preferred_element_type=F32)

    for u in units:
        pm = []
        for p in ops[u]["p"]:
            row = lax.broadcasted_iota(jnp.int32, p.shape, 0)
            col = lax.broadcasted_iota(jnp.int32, p.shape, 1)
            pm.append(jnp.where(row >= col, p, 0.0).astype(BF16))
        ops[u]["p"] = pm

    for u in units:
        o = ops[u]
        blocks = [None] * (L // KB)
        for j, p in enumerate(o["p"]):
            cj = jnp.dot(p, o["v"][j * KB:(j + 1) * KB], preferred_element_type=F32)
            for i in range(j, L // KB):
                piece = cj[(i - j) * KB:(i - j + 1) * KB]
                blocks[i] = piece if blocks[i] is None else blocks[i] + piece
        o["o"] = jnp.concatenate(blocks, axis=0)

    for h in range(H_HGRN):
        st = st_ref[h]
        for c in range(HGRN_SPAN):
            o = ops[c, h]
            out = o["o"] + lax.dot_general(o["qt"], st.astype(BF16), NT_DIMS, preferred_element_type=F32)
            st = o["e"] * st + o["m"]
            rows = pl.ds(row0 + c * L, L)
            gate = proj_ref[rows, OFF_G + h * DV_HGRN:OFF_G + (h + 1) * DV_HGRN]
            mix_ref[rows, D_CONV + h * DV_HGRN:D_CONV + (h + 1) * DV_HGRN] = _hgrn_out(out, hng, gate).astype(BF16)
        st_ref[h] = st


def _sink_softmax(s, sink):
    m = jnp.maximum(jnp.max(s, axis=-1, keepdims=True), sink)
    p = jnp.exp(s - m)
    return p, jnp.sum(p, axis=-1, keepdims=True) + jnp.exp(sink - m)


def _bias_kernel(tab_ref, bp_ref, bs_ref, op_ref, os_ref, *, seq):
    bk = bp_ref[...]
    for h in range(H_ATTN):
        acc = jnp.full(bk.shape, -jnp.inf, F32)
        for bkt in range(NUM_BUCKETS):
            acc = jnp.where(bk == bkt, tab_ref[bkt, h], acc)
        op_ref[h] = acc
    bk = bs_ref[...]
    row = lax.broadcasted_iota(jnp.int32, bk.shape, 0)
    for kv in range(KV_HEADS):
        acc = jnp.full(bk.shape, -jnp.inf, F32)
        for bkt in range(NUM_BUCKETS):
            val = jnp.full(bk.shape, tab_ref[bkt, kv * GROUP], F32)
            for gi in range(1, GROUP):
                val = jnp.where(row >= gi * seq, tab_ref[bkt, kv * GROUP + gi], val)
            acc = jnp.where(bk == bkt, val, acc)
        os_ref[kv] = acc


def _t5_bucket(rel):
    n = jnp.maximum(rel, 0)
    max_exact = NUM_BUCKETS // 2
    nf = jnp.maximum(n, max_exact).astype(F32)
    large = max_exact + (jnp.log(nf / max_exact) / math.log(MAX_DISTANCE / max_exact)
                         * (NUM_BUCKETS - max_exact)).astype(jnp.int32)
    large = jnp.minimum(large, NUM_BUCKETS - 1)
    return jnp.where(n < max_exact, n, large)


def _bias_tables(rel_bias, dec_seq, w_buf):
    qi = jnp.arange(ATTN_BLOCK, dtype=jnp.int32)[:, None]
    kc = jnp.arange(2 * ATTN_BLOCK, dtype=jnp.int32)[None, :]
    rel_p = qi + ATTN_BLOCK - kc
    bucket_p = jnp.where((rel_p >= 0) & (rel_p <= WINDOW), _t5_bucket(rel_p), -1)
    ts = (jnp.arange(GROUP * dec_seq, dtype=jnp.int32) % dec_seq)[:, None]
    js = jnp.arange(2 * ATTN_BLOCK, dtype=jnp.int32)[None, :]
    rel_s = w_buf + ts - js
    ok_s = (rel_s >= 0) & (rel_s <= WINDOW) & (js < w_buf + dec_seq)
    bucket_s = jnp.where(ok_s, _t5_bucket(rel_s), -1)
    return pl.pallas_call(
        functools.partial(_bias_kernel, seq=dec_seq),
        out_shape=(jax.ShapeDtypeStruct((H_ATTN, ATTN_BLOCK, 2 * ATTN_BLOCK), F32),
                   jax.ShapeDtypeStruct((KV_HEADS, GROUP * dec_seq, 2 * ATTN_BLOCK), F32)),
        in_specs=[pl.BlockSpec(memory_space=pltpu.SMEM),
                  pl.BlockSpec(memory_space=pltpu.VMEM),
                  pl.BlockSpec(memory_space=pltpu.VMEM)],
        out_specs=(pl.BlockSpec(memory_space=pltpu.VMEM), pl.BlockSpec(memory_space=pltpu.VMEM)),
        name="rel_bias_tables",
    )(rel_bias.astype(F32), bucket_p, bucket_s)


def _mod_kernel(c_ref, w_ref, b_ref, o_ref):
    s = _silu(c_ref[...]).astype(BF16)
    o_ref[...] = jnp.dot(s, w_ref[...], preferred_element_type=F32) + b_ref[...]


def _modulation(c_all, w_ada, b_ada):
    depth = w_ada.shape[0]
    n = c_all.shape[0]
    return pl.pallas_call(
        _mod_kernel,
        out_shape=jax.ShapeDtypeStruct((depth, n, N_MOD * D_MODEL), F32),
        grid=(depth, N_MOD),
        in_specs=[pl.BlockSpec((n, D_MODEL), lambda l, j: (0, 0)),
                  pl.BlockSpec((None, D_MODEL, D_MODEL), lambda l, j: (l, 0, j)),
                  pl.BlockSpec((None, 1, D_MODEL), lambda l, j: (l, 0, j))],
        out_specs=pl.BlockSpec((None, n, D_MODEL), lambda l, j: (l, 0, j)),
        compiler_params=pltpu.CompilerParams(dimension_semantics=("arbitrary", "arbitrary"),
                                             vmem_limit_bytes=VMEM_LIMIT),
        name="adaln_modulation",
    )(c_all, w_ada, b_ada.reshape(depth, 1, N_MOD * D_MODEL))


def _mod_spec(mod, chunk, tile, tiles_per_batch):
    if mod.ndim == 3:
        return pl.BlockSpec((None, 1, D_MODEL), lambda i: (i // tiles_per_batch, 0, chunk))
    return pl.BlockSpec((tile, D_MODEL), lambda i: (i, chunk))


def _inproj_kernel(x_ref, sh_ref, sc_ref, g_ref, w_ref, o_ref):
    h = _rms_rows(x_ref[...]) * g_ref[...] * (1.0 + sc_ref[...]) + sh_ref[...]
    o_ref[...] = jnp.dot(h.astype(BF16), w_ref[...], preferred_element_type=F32)


def _inproj(x2, mod, norm_g, w_in, layer, tile, tiles_per_batch):
    n = x2.shape[0]
    return pl.pallas_call(
        _inproj_kernel,
        out_shape=jax.ShapeDtypeStruct((n, IN_WIDTH), F32),
        grid=(n // tile,),
        in_specs=[pl.BlockSpec((tile, D_MODEL), lambda i: (i, 0)),
                  _mod_spec(mod, 0, tile, tiles_per_batch),
                  _mod_spec(mod, 1, tile, tiles_per_batch),
                  pl.BlockSpec((1, D_MODEL), lambda i: (0, 0)),
                  pl.BlockSpec((None, D_MODEL, IN_WIDTH), lambda i: (layer, 0, 0), pipeline_mode=pl.Buffered(1))],
        out_specs=pl.BlockSpec((tile, IN_WIDTH), lambda i: (i, 0)),
        compiler_params=pltpu.CompilerParams(dimension_semantics=("arbitrary",), vmem_limit_bytes=VMEM_LIMIT),
        name="in_projection",
    )(x2, mod, mod, norm_g.reshape(1, D_MODEL), w_in)


def _mlp_kernel(mix_ref, x_ref, g1_ref, sh_ref, sc_ref, g2_ref, ng_ref, wout_ref, wup_ref, wdn_ref, fg_ref, o_ref, *,
                final):
    x1 = x_ref[...] + g1_ref[...] * jnp.dot(mix_ref[...].astype(BF16), wout_ref[...], preferred_element_type=F32)
    h = (_rms_rows(x1) * ng_ref[...] * (1.0 + sc_ref[...]) + sh_ref[...]).astype(BF16)
    acc = None
    for c in range(D_FF // FF_CHUNK):
        u = jnp.dot(h, wup_ref[:, c * FF_CHUNK:(c + 1) * FF_CHUNK], preferred_element_type=F32)
        u = jnp.square(jnp.maximum(u, 0.0)).astype(BF16)
        d = jnp.dot(u, wdn_ref[c * FF_CHUNK:(c + 1) * FF_CHUNK, :], preferred_element_type=F32)
        acc = d if acc is None else acc + d
    x2 = x1 + g2_ref[...] * acc
    if final:
        x2 = _rms_rows(x2) * fg_ref[...]
    o_ref[...] = x2


def _out_mlp(mix2, x2, mod, norm_g, w_out, w_up, w_down, final_g, layer, tile, tiles_per_batch, final):
    n = x2.shape[0]
    const = lambda i: (0, 0)
    of_layer = lambda i: (layer, 0, 0)
    return pl.pallas_call(
        functools.partial(_mlp_kernel, final=final),
        out_shape=jax.ShapeDtypeStruct((n, D_MODEL), F32),
        grid=(n // tile,),
        in_specs=[pl.BlockSpec((tile, D_MODEL), lambda i: (i, 0)),
                  pl.BlockSpec((tile, D_MODEL), lambda i: (i, 0)),
                  _mod_spec(mod, 2, tile, tiles_per_batch),
                  _mod_spec(mod, 3, tile, tiles_per_batch),
                  _mod_spec(mod, 4, tile, tiles_per_batch),
                  _mod_spec(mod, 5, tile, tiles_per_batch),
                  pl.BlockSpec((1, D_MODEL), const),
                  pl.BlockSpec((None, D_MODEL, D_MODEL), of_layer, pipeline_mode=pl.Buffered(1)),
                  pl.BlockSpec((None, D_MODEL, D_FF), of_layer, pipeline_mode=pl.Buffered(1)),
                  pl.BlockSpec((None, D_FF, D_MODEL), of_layer, pipeline_mode=pl.Buffered(1)),
                  pl.BlockSpec((1, D_MODEL), const)],
        out_specs=pl.BlockSpec((tile, D_MODEL), lambda i: (i, 0)),
        compiler_params=pltpu.CompilerParams(dimension_semantics=("arbitrary",), vmem_limit_bytes=VMEM_LIMIT),
        name="out_projection_mlp",
    )(mix2, x2, mod, mod, mod, mod, norm_g.reshape(1, D_MODEL), w_out, w_up, w_down, final_g.reshape(1, D_MODEL))


def _conv_ln_swish(acc, lng, lnb):
    mu = jnp.mean(acc, axis=-1, keepdims=True)
    xc = acc - mu
    y = xc * lax.rsqrt(jnp.mean(xc * xc, axis=-1, keepdims=True) + EPS) * lng + lnb
    return _silu(y)


def _hgrn_gates(fh, lb):
    f = lb + (1.0 - lb) * jax.nn.sigmoid(fh)
    return jnp.log(f), 1.0 - f


def _hgrn_out(o, hng, gate):
    return _rms_rows(o) * hng * _silu(gate)


def _prompt_mix_kernel(sinks_ref, x_ref, sh_ref, sc_ref, ng_ref, win_ref, convw_ref, convb_ref, lng_ref, lnb_ref,
                       hlb_ref, hng_ref, bias_ref,
                       mix_ref, convo_ref, so_ref, ko_ref, vo_ref,
                       proj_ref, abuf, kbuf, vbuf, st_ref, *, layer, tile):
    t = pl.program_id(1)
    last = pl.num_programs(1) - 1

    @pl.when(t == 0)
    def _():
        abuf[0:CONV_PAD, :] = jnp.zeros((CONV_PAD, D_CONV), F32)
        abuf[CONV_PAD + tile:CONV_PAD + tile + SUBLANES, :] = jnp.zeros((SUBLANES, D_CONV), F32)
        kbuf[0:ATTN_BLOCK, :] = jnp.zeros((ATTN_BLOCK, D_KV), BF16)
        vbuf[0:ATTN_BLOCK, :] = jnp.zeros((ATTN_BLOCK, D_KV), BF16)
        st_ref[...] = jnp.zeros(st_ref.shape, F32)

    h_in = (_rms_rows(x_ref[...]) * ng_ref[...] * (1.0 + sc_ref[...]) + sh_ref[...]).astype(BF16)
    for lo, hi in ((OFF_AVAL, OFF_Q), (OFF_F, OFF_I), (OFF_Q, OFF_F), (OFF_I, OFF_G), (OFF_G, OFF_QA),
                   (OFF_QA, IN_WIDTH)):
        proj_ref[:, lo:hi] = jnp.dot(h_in, win_ref[:, lo:hi], preferred_element_type=F32)

    abuf[CONV_PAD:CONV_PAD + tile, :] = (proj_ref[:, OFF_AVAL:OFF_AVAL + D_CONV]
                                         * jax.nn.sigmoid(proj_ref[:, OFF_AGATE:OFF_AGATE + D_CONV]))
    first_row = CONV_PAD - (CONV_WIDTH - 1)
    acc = jnp.broadcast_to(convb_ref[...], (tile, D_CONV))
    for r in range(SUBLANES):
        z = None
        for off in range(r, first_row + CONV_WIDTH, SUBLANES):
            j = off - first_row
            if j < 0:
                continue
            term = convw_ref[j:j + 1, :] * abuf[off - r:off - r + tile + SUBLANES, :]
            z = term if z is None else z + term
        acc = acc + z[r:r + tile]
    mix_ref[:, 0:D_CONV] = _conv_ln_swish(acc, lng_ref[...], lnb_ref[...]).astype(BF16)

    lb = _layer_lb(hlb_ref[...], layer)
    hng = hng_ref[...]
    span = HGRN_SPAN * HGRN_CHUNK
    ri = lax.broadcasted_iota(jnp.int32, (span, span), 0)
    ci = lax.broadcasted_iota(jnp.int32, (span, span), 1)
    tri = jnp.where((ri >= ci) & (ri // HGRN_CHUNK == ci // HGRN_CHUNK), 1.0, 0.0).astype(BF16)

    for i in range(tile // span):
        _hgrn_span(proj_ref, mix_ref, st_ref, i * span, lb, hng, tri)

    kbuf[ATTN_BLOCK:ATTN_BLOCK + tile, :] = proj_ref[:, OFF_KA:OFF_KA + D_KV].astype(BF16)
    vbuf[ATTN_BLOCK:ATTN_BLOCK + tile, :] = proj_ref[:, OFF_VA:OFF_VA + D_KV].astype(BF16)
    scale = HEAD_DIM ** -0.5
    for blk in range(tile // ATTN_BLOCK):
        r0 = blk * ATTN_BLOCK
        heads = []
        for h in range(H_ATTN):
            kv = h // GROUP
            q = (proj_ref[r0:r0 + ATTN_BLOCK, OFF_QA + h * HEAD_DIM:OFF_QA + (h + 1) * HEAD_DIM] * scale).astype(BF16)
            kall = kbuf[r0:r0 + 2 * ATTN_BLOCK, kv * HEAD_DIM:(kv + 1) * HEAD_DIM]
            vall = vbuf[r0:r0 + 2 * ATTN_BLOCK, kv * HEAD_DIM:(kv + 1) * HEAD_DIM]
            s = lax.dot_general(q, kall, NT_DIMS, preferred_element_type=F32) + bias_ref[h]
            if blk == 0:
                col = lax.broadcasted_iota(jnp.int32, s.shape, 1)
                s = jnp.where(col + (t * tile - ATTN_BLOCK) >= 0, s, -jnp.inf)
            p, den = _sink_softmax(s, sinks_ref[h])
            heads.append(jnp.dot(p.astype(BF16), vall, preferred_element_type=F32) / den)
        mix_ref[r0:r0 + ATTN_BLOCK, D_CONV + D_HGRN:D_MODEL] = jnp.concatenate(heads, axis=1).astype(BF16)

    @pl.when(t == last)
    def _():
        convo_ref[...] = abuf[CONV_PAD + tile - (CONV_WIDTH - 1):CONV_PAD + tile, :]
        for h in range(H_HGRN):
            so_ref[h] = st_ref[h].T
        ko_ref[...] = proj_ref[tile - WINDOW:tile, OFF_KA:OFF_KA + D_KV]
        vo_ref[...] = proj_ref[tile - WINDOW:tile, OFF_VA:OFF_VA + D_KV]

    abuf[0:CONV_PAD, :] = abuf[tile:tile + CONV_PAD, :]
    kbuf[0:ATTN_BLOCK, :] = kbuf[tile:tile + ATTN_BLOCK, :]
    vbuf[0:ATTN_BLOCK, :] = vbuf[tile:tile + ATTN_BLOCK, :]


def _carry_specs(carried):
    return [pl.BlockSpec(memory_space=pl.ANY)] * len(carried)


def _carried(kernel_fn, n_in, n_carried):
    if n_carried == 0:
        return kernel_fn
    return lambda *refs: kernel_fn(*refs[:n_in], *refs[n_in + n_carried:])


def _prompt_mixers(x, mod, norm_g, w_in, sinks, conv_w, conv_b, ln_g, ln_b, hgrn_lb, hng, bias_p, layer, carried):
    B, T = x.shape[:2]
    tile = MIX_TILE
    depth = hgrn_lb.shape[0]
    const2 = lambda b, t: (0, 0)
    inputs = (sinks, x, mod, mod, norm_g.reshape(1, D_MODEL), w_in, conv_w, conv_b.reshape(1, D_CONV),
              ln_g.reshape(1, D_CONV), ln_b.reshape(1, D_CONV), hgrn_lb, hng.reshape(1, DV_HGRN), bias_p)
    return pl.pallas_call(
        _carried(functools.partial(_prompt_mix_kernel, layer=layer, tile=tile), len(inputs), len(carried)),
        out_shape=(jax.ShapeDtypeStruct((B, T, D_MODEL), BF16),
                   jax.ShapeDtypeStruct((depth, B, CONV_WIDTH - 1, D_CONV), F32),
                   jax.ShapeDtypeStruct((depth, B, H_HGRN, DK_HGRN, DV_HGRN), F32),
                   jax.ShapeDtypeStruct((depth, B, WINDOW, D_KV), F32),
                   jax.ShapeDtypeStruct((depth, B, WINDOW, D_KV), F32)),
        grid=(B, T // tile),
        in_specs=[pl.BlockSpec(memory_space=pltpu.SMEM),
                  pl.BlockSpec((None, tile, D_MODEL), lambda b, t: (b, t, 0)),
                  pl.BlockSpec((None, 1, D_MODEL), lambda b, t: (b, 0, 0)),
                  pl.BlockSpec((None, 1, D_MODEL), lambda b, t: (b, 0, 1)),
                  pl.BlockSpec((1, D_MODEL), const2),
                  pl.BlockSpec((None, D_MODEL, IN_WIDTH), lambda b, t: (layer, 0, 0), pipeline_mode=pl.Buffered(1)),
                  pl.BlockSpec((CONV_WIDTH, D_CONV), const2),
                  pl.BlockSpec((1, D_CONV), const2),
                  pl.BlockSpec((1, D_CONV), const2),
                  pl.BlockSpec((1, D_CONV), const2),
                  pl.BlockSpec((depth, D_HGRN), const2),
                  pl.BlockSpec((1, DV_HGRN), const2),
                  pl.BlockSpec((H_ATTN, ATTN_BLOCK, 2 * ATTN_BLOCK), lambda b, t: (0, 0, 0))] + _carry_specs(carried),
        out_specs=(pl.BlockSpec((None, tile, D_MODEL), lambda b, t: (b, t, 0)),
                   pl.BlockSpec((None, None, CONV_WIDTH - 1, D_CONV), lambda b, t: (layer, b, 0, 0)),
                   pl.BlockSpec((None, None, H_HGRN, DK_HGRN, DV_HGRN), lambda b, t: (layer, b, 0, 0, 0)),
                   pl.BlockSpec((None, None, WINDOW, D_KV), lambda b, t: (layer, b, 0, 0)),
                   pl.BlockSpec((None, None, WINDOW, D_KV), lambda b, t: (layer, b, 0, 0))),
        input_output_aliases={len(inputs) + i: 1 + i for i in range(len(carried))},
        scratch_shapes=[pltpu.VMEM((tile, IN_WIDTH), F32),
                        pltpu.VMEM((CONV_PAD + tile + SUBLANES, D_CONV), F32),
                        pltpu.VMEM((ATTN_BLOCK + tile, D_KV), BF16),
                        pltpu.VMEM((ATTN_BLOCK + tile, D_KV), BF16),
                        pltpu.VMEM((H_HGRN, DV_HGRN, DK_HGRN), F32)],
        compiler_params=pltpu.CompilerParams(dimension_semantics=("arbitrary", "arbitrary"),
                                             vmem_limit_bytes=VMEM_LIMIT),
        name="prompt_mixers",
    )(*inputs, *carried)


def _sample_mix_kernel(sinks_ref, proj_ref, cconv_ref, state_ref, ck_ref, cv_ref, convw_ref, convb_ref, lng_ref,
                       lnb_ref, hlb_ref, hng_ref, bias_ref,
                       mix_ref, convo_ref, so_ref, ko_ref, vo_ref, full_ref, kall_ref, vall_ref, *,
                       layer, block, seq, w_buf):
    hist = CONV_WIDTH - 1
    n_keys = w_buf + seq
    pad_keys = kall_ref.shape[1]

    @pl.when(pl.program_id(0) == 0)
    def _():
        for ref in (kall_ref, vall_ref):
            ref[:, w_buf:pad_keys, :] = jnp.zeros((block, pad_keys - w_buf, D_KV), F32)

    lb = _layer_lb(hlb_ref[...], layer)
    hng = hng_ref[...]
    scale = HEAD_DIM ** -0.5
    elems = range(block)
    row8 = lax.broadcasted_iota(jnp.int32, (SUBLANES, DV_HGRN), 0)
    ones_rows = jnp.where((row8 >= seq) & (row8 < seq + 3), 1.0, 0.0)
    zrow = jnp.zeros((1, DK_HGRN), BF16)
    prow = lax.broadcasted_iota(jnp.int32, (seq, seq), 0)
    pcol = lax.broadcasted_iota(jnp.int32, (seq, seq), 1)
    grow = lax.broadcasted_iota(jnp.int32, (GROUP * seq, 1), 0)

    proj = [proj_ref[e * seq:(e + 1) * seq, :] for e in elems]

    out_a = []
    for e in elems:
        p = proj[e]
        full_ref[e, 0:hist, :] = cconv_ref[e]
        full_ref[e, hist:hist + seq, :] = (p[:, OFF_AVAL:OFF_AVAL + D_CONV]
                                           * jax.nn.sigmoid(p[:, OFF_AGATE:OFF_AGATE + D_CONV]))
        rows = [jnp.sum(convw_ref[...] * full_ref[e, s:s + CONV_WIDTH, :], axis=0, keepdims=True) for s in range(seq)]
        convo_ref[e] = full_ref[e, seq:seq + hist, :]
        out_a.append(_conv_ln_swish(jnp.concatenate(rows, axis=0) + convb_ref[...], lng_ref[...], lnb_ref[...]))

    units = [(e, h) for e in elems for h in range(H_HGRN)]
    ops = {}
    for e in elems:
        p = proj[e]
        g, k = _hgrn_gates(p[:, OFF_F:OFF_F + D_HGRN], lb)
        b = _cumsum_rows_small(g)
        for h in range(H_HGRN):
            cs = slice(h * DK_HGRN, (h + 1) * DK_HGRN)
            q = p[:, OFF_Q + h * DK_HGRN:OFF_Q + (h + 1) * DK_HGRN]
            v = p[:, OFF_I + h * DV_HGRN:OFF_I + (h + 1) * DV_HGRN]
            bb = b[:, cs]
            bl = bb[seq - 1:seq, :]
            kst = (k[:, cs] * jnp.exp(bl - bb)).astype(BF16)
            x = jnp.concatenate([kst.astype(F32), *(part.astype(F32) for part in _split3_bf16(jnp.exp(bl))),
                                 zrow.astype(F32)], axis=0).astype(BF16)
            vpad = jnp.concatenate([v, jnp.zeros((SUBLANES - seq, DV_HGRN), F32)], axis=0)
            ops[e, h] = dict(qp=(q * jnp.exp(bb - bl)).astype(BF16), kst=kst, v=v.astype(BF16),
                             qt=(q * jnp.exp(bb)).astype(BF16), x=x,
                             r=jnp.concatenate([vpad, ones_rows], axis=1).astype(BF16))
    for u in units:
        o = ops[u]
        st = state_ref[u[0], u[1]]
        o["p"] = lax.dot_general(o["qp"], o["kst"], NT_DIMS, preferred_element_type=F32)
        o["inter"] = jnp.dot(o["qt"], st.astype(BF16), preferred_element_type=F32)
        me = lax.dot_general(o["x"], o["r"], TN_DIMS, preferred_element_type=F32)
        so_ref[u[0], u[1]] = me[:, DV_HGRN:] * st + me[:, :DV_HGRN]
    out_b = {}
    for u in units:
        o = ops[u]
        pm = jnp.where(prow >= pcol, o["p"], 0.0).astype(BF16)
        out = o["inter"] + jnp.dot(pm, o["v"], preferred_element_type=F32)
        gate = proj[u[0]][:, OFF_G + u[1] * DV_HGRN:OFF_G + (u[1] + 1) * DV_HGRN]
        out_b[u] = _hgrn_out(out, hng, gate)

    for e in elems:
        p = proj[e]
        for ref, cache, new, out in ((kall_ref, ck_ref, p[:, OFF_KA:OFF_KA + D_KV], ko_ref),
                                     (vall_ref, cv_ref, p[:, OFF_VA:OFF_VA + D_KV], vo_ref)):
            ref[e, 0:w_buf, :] = cache[e]
            ref[e, w_buf:n_keys, :] = new
            out[e] = ref[e, seq:n_keys, :]
    scores = {}
    for e in elems:
        p = proj[e]
        for kv in range(KV_HEADS):
            hs = slice(kv * HEAD_DIM, (kv + 1) * HEAD_DIM)
            q2 = jnp.concatenate([p[:, OFF_QA + h * HEAD_DIM:OFF_QA + (h + 1) * HEAD_DIM]
                                  for h in range(kv * GROUP, (kv + 1) * GROUP)], axis=0)
            kall = kall_ref[e][:, hs].astype(BF16)
            scores[e, kv] = (lax.dot_general((q2 * scale).astype(BF16), kall, NT_DIMS, preferred_element_type=F32)
                             + bias_ref[kv])
    out_c = {}
    for e in elems:
        for kv in range(KV_HEADS):
            hs = slice(kv * HEAD_DIM, (kv + 1) * HEAD_DIM)
            sink = jnp.zeros((GROUP * seq, 1), F32)
            for gi in range(GROUP):
                sink = jnp.where(grow >= gi * seq, sinks_ref[kv * GROUP + gi], sink)
            pr, den = _sink_softmax(scores[e, kv], sink)
            o2 = jnp.dot(pr.astype(BF16), vall_ref[e][:, hs].astype(BF16), preferred_element_type=F32) / den
            for gi in range(GROUP):
                out_c[e, kv * GROUP + gi] = o2[gi * seq:(gi + 1) * seq]

    for e in elems:
        parts = [out_a[e]] + [out_b[e, h] for h in range(H_HGRN)] + [out_c[e, h] for h in range(H_ATTN)]
        mix_ref[e * seq:(e + 1) * seq, :] = jnp.concatenate(parts, axis=1)


def _sample_mixers(proj2, sinks, cache_conv, state, cache_k, cache_v, conv_w, conv_b, ln_g, ln_b, hgrn_lb, hng,
                   bias_s, layer, carried):
    B = state.shape[1]
    seq = proj2.shape[0] // B
    w_buf = cache_k.shape[2]
    block = SAMPLE_BLOCK
    depth = hgrn_lb.shape[0]
    hist = CONV_WIDTH - 1
    const2 = lambda i: (0, 0)
    cache_specs = [pl.BlockSpec((None, block, hist, D_CONV), lambda i: (layer, i, 0, 0)),
                   pl.BlockSpec((None, block, H_HGRN, DK_HGRN, DV_HGRN), lambda i: (layer, i, 0, 0, 0)),
                   pl.BlockSpec((None, block, w_buf, D_KV), lambda i: (layer, i, 0, 0)),
                   pl.BlockSpec((None, block, w_buf, D_KV), lambda i: (layer, i, 0, 0))]
    inputs = (sinks, proj2, cache_conv, state, cache_k, cache_v, conv_w, conv_b.reshape(1, D_CONV),
              ln_g.reshape(1, D_CONV), ln_b.reshape(1, D_CONV), hgrn_lb, hng.reshape(1, DV_HGRN), bias_s)
    return pl.pallas_call(
        _carried(functools.partial(_sample_mix_kernel, layer=layer, block=block, seq=seq, w_buf=w_buf),
                 len(inputs), len(carried)),
        out_shape=(jax.ShapeDtypeStruct((B * seq, D_MODEL), F32),
                   jax.ShapeDtypeStruct((depth, B, hist, D_CONV), F32),
                   jax.ShapeDtypeStruct((depth, B, H_HGRN, DK_HGRN, DV_HGRN), F32),
                   jax.ShapeDtypeStruct((depth, B, w_buf, D_KV), F32),
                   jax.ShapeDtypeStruct((depth, B, w_buf, D_KV), F32)),
        grid=(B // block,),
        in_specs=[pl.BlockSpec(memory_space=pltpu.SMEM),
                  pl.BlockSpec((block * seq, IN_WIDTH), lambda i: (i, 0))] + cache_specs + [
                  pl.BlockSpec((CONV_WIDTH, D_CONV), const2),
                  pl.BlockSpec((1, D_CONV), const2),
                  pl.BlockSpec((1, D_CONV), const2),
                  pl.BlockSpec((1, D_CONV), const2),
                  pl.BlockSpec((depth, D_HGRN), const2),
                  pl.BlockSpec((1, DV_HGRN), const2),
                  pl.BlockSpec((KV_HEADS, GROUP * seq, 2 * ATTN_BLOCK), lambda i: (0, 0, 0))] + _carry_specs(carried),
        out_specs=tuple([pl.BlockSpec((block * seq, D_MODEL), lambda i: (i, 0))] + cache_specs),
        input_output_aliases={len(inputs) + i: 1 + i for i in range(len(carried))},
        scratch_shapes=[pltpu.VMEM((block, hist + seq + 6, D_CONV), F32),
                        pltpu.VMEM((block, 2 * ATTN_BLOCK, D_KV), F32),
                        pltpu.VMEM((block, 2 * ATTN_BLOCK, D_KV), F32)],
        compiler_params=pltpu.CompilerParams(dimension_semantics=("arbitrary",), vmem_limit_bytes=VMEM_LIMIT),
        name="sample_mixers",
    )(*inputs, *carried)


def kernel(x_prompt, x_sample, cache_conv, state_hgrn, cache_swa_k, cache_swa_v, c_prompt, c_sample, rel_bias, w_ada, b_ada, norm_mix_g, w_in, conv_w, conv_b, conv_ln_g, conv_ln_b, hgrn_lb, hgrn_norm_g, attn_sinks, w_out, norm_mlp_g, w_up, w_down, final_g):
    Bp, Tp = x_prompt.shape[:2]
    Bs, Ts = x_sample.shape[:2]
    depth = w_in.shape[0]
    w_buf = cache_swa_k.shape[2]
    assert Tp % MIX_TILE == 0 and (Bp * Tp) % TOK_TILE == 0 and Tp % TOK_TILE == 0 and Bs % SAMPLE_BLOCK == 0
    assert w_buf == WINDOW and GROUP * Ts == SUBLANES

    bias_p, bias_s = _bias_tables(rel_bias, Ts, w_buf)
    mod = _modulation(jnp.concatenate([c_prompt, c_sample], axis=0), w_ada.astype(BF16), b_ada)
    w_in_b, w_out_b, w_up_b, w_down_b = (w.astype(BF16) for w in (w_in, w_out, w_up, w_down))
    hlb = hgrn_lb.astype(F32)
    ck = cache_swa_k.reshape(depth, Bs, w_buf, D_KV)
    cv = cache_swa_v.reshape(depth, Bs, w_buf, D_KV)

    xp = x_prompt.reshape(Bp * Tp, D_MODEL)
    xs = x_sample.reshape(Bs * Ts, D_MODEL)
    tile_s = Bs * Ts
    caches_p = ()
    caches_s = ()
    for l in range(depth):
        final = l == depth - 1
        mod_p = mod[l, :Bp].reshape(Bp, 1, N_MOD * D_MODEL)
        mod_s = jnp.repeat(mod[l, Bp:], Ts, axis=0)
        mix_p, *caches_p = _prompt_mixers(xp.reshape(Bp, Tp, D_MODEL), mod_p, norm_mix_g[l], w_in_b, attn_sinks[l],
                                          conv_w[l], conv_b[l], conv_ln_g[l], conv_ln_b[l], hlb, hgrn_norm_g[l],
                                          bias_p, l, caches_p)
        xp = _out_mlp(mix_p.reshape(Bp * Tp, D_MODEL), xp, mod_p, norm_mlp_g[l], w_out_b, w_up_b, w_down_b,
                      final_g, l, TOK_TILE, Tp // TOK_TILE, final)
        proj_s = _inproj(xs, mod_s, norm_mix_g[l], w_in_b, l, tile_s, 1)
        mix_s, *caches_s = _sample_mixers(proj_s, attn_sinks[l], cache_conv, state_hgrn, ck, cv, conv_w[l],
                                          conv_b[l], conv_ln_g[l], conv_ln_b[l], hlb, hgrn_norm_g[l], bias_s, l,
                                          caches_s)
        xs = _out_mlp(mix_s, xs, mod_s, norm_mlp_g[l], w_out_b, w_up_b, w_down_b, final_g, l, tile_s, 1, final)
    cp, sp, kp, vp = caches_p
    cs, ss, ksn, vsn = caches_s
    return (xp.reshape(Bp, Tp, D_MODEL), xs.reshape(Bs, Ts, D_MODEL), cp, cs, sp, ss,
            kp.reshape(depth, Bp, WINDOW, KV_HEADS, HEAD_DIM), ksn.reshape(depth, Bs, w_buf, KV_HEADS, HEAD_DIM),
            vp.reshape(depth, Bp, WINDOW, KV_HEADS, HEAD_DIM), vsn.reshape(depth, Bs, w_buf, KV_HEADS, HEAD_DIM))
```

```python
import functools
import math

import jax
import jax.numpy as jnp
from jax import lax
from jax.experimental import pallas as pl
from jax.experimental.pallas import tpu as pltpu

F32 = jnp.float32
BF16 = jnp.bfloat16

D_MODEL = 1024
D_CONV = 256
CONV_WIDTH = 31
H_HGRN = 4
DK_HGRN = 128
DV_HGRN = 128
D_HGRN = 512
HEAD_DIM = 64
H_ATTN = 4
KV_HEADS = 2
GROUP = H_ATTN // KV_HEADS
D_ATTN = H_ATTN * HEAD_DIM
D_KV = KV_HEADS * HEAD_DIM
WINDOW = 128
ATTN_BLOCK = 128
NUM_BUCKETS = 32
MAX_DISTANCE = 128
D_FF = 4 * D_MODEL
N_MOD = 6
EPS = 1e-6

OFF_AVAL = 0
OFF_AGATE = OFF_AVAL + D_CONV
OFF_Q = OFF_AGATE + D_CONV
OFF_F = OFF_Q + H_HGRN * DK_HGRN
OFF_I = OFF_F + H_HGRN * DK_HGRN
OFF_G = OFF_I + D_HGRN
OFF_QA = OFF_G + D_HGRN
OFF_KA = OFF_QA + D_ATTN
OFF_VA = OFF_KA + D_KV
IN_WIDTH = OFF_VA + D_KV

HGRN_CHUNK = 64
HGRN_KEYBLOCK = 32
HGRN_SPAN = 4
SUBLANES = 8
CONV_PAD = 32
MIX_TILE = 256
TOK_TILE = 512
SAMPLE_BLOCK = 8
FF_CHUNK = 1024
VMEM_LIMIT = 56 * 1024 * 1024

NT_DIMS = (((1,), (1,)), ((), ()))
TN_DIMS = (((0,), (0,)), ((), ()))


def _silu(x):
    return x * jax.nn.sigmoid(x)


def _rms_rows(x):
    return x * lax.rsqrt(jnp.mean(x * x, axis=-1, keepdims=True) + EPS)


def _layer_lb(hlb, layer):
    m = jnp.max(hlb, axis=0, keepdims=True)
    e = jnp.exp(hlb - m)
    p = e / jnp.sum(e, axis=0, keepdims=True)
    lb = jnp.zeros_like(m)
    for i in range(1, layer + 1):
        lb = lb + p[i:i + 1, :]
    return lb


def _split3_bf16(x):
    hi = x.astype(BF16)
    r = x - hi.astype(F32)
    mid = r.astype(BF16)
    return hi, mid, (r - mid.astype(F32)).astype(BF16)


def _select_rows_mxu(sel, x):
    return sum(jnp.dot(sel, part, preferred_element_type=F32) for part in _split3_bf16(x))


def _cumsum_rows_small(g):
    row = lax.broadcasted_iota(jnp.int32, g.shape, 0)
    b = jnp.zeros_like(g)
    for u in range(g.shape[0]):
        b = b + jnp.where(row >= u, g[u:u + 1, :], 0.0)
    return b


def _hgrn_span(proj_ref, mix_ref, st_ref, row0, lb, hng, tri):
    L, KB = HGRN_CHUNK, HGRN_KEYBLOCK
    span = HGRN_SPAN * L
    g, k = _hgrn_gates(proj_ref[pl.ds(row0, span), OFF_F:OFF_F + D_HGRN], lb)
    b = _select_rows_mxu(tri, g)
    units = [(c, h) for c in range(HGRN_SPAN) for h in range(H_HGRN)]

    ops = {}
    for c, h in units:
        rows = pl.ds(row0 + c * L, L)
        cs = slice(h * DK_HGRN, (h + 1) * DK_HGRN)
        q = proj_ref[rows, OFF_Q + h * DK_HGRN:OFF_Q + (h + 1) * DK_HGRN]
        v = proj_ref[rows, OFF_I + h * DV_HGRN:OFF_I + (h + 1) * DV_HGRN].astype(BF16)
        kk = k[c * L:(c + 1) * L, cs]
        bb = b[c * L:(c + 1) * L, cs]
        qp, kp = [], []
        for lo in range(0, L, KB):
            r = bb[lo + KB // 2 - 1:lo + KB // 2, :]
            kp.append((kk[lo:lo + KB] * jnp.exp(r - bb[lo:lo + KB])).astype(BF16))
            qp.append((q[lo:] * jnp.exp(bb[lo:] - r)).astype(BF16))
        bl = bb[L - 1:L, :]
        ops[c, h] = dict(qp=qp, kp=kp, v=v, qt=(q * jnp.exp(bb)).astype(BF16),
                         kst=(kk * jnp.exp(bl - bb)).astype(BF16), e=jnp.exp(bl))

    for u in units:
        o = ops[u]
        o["p"] = [lax.dot_general(qp, kp, NT_DIMS, preferred_element_type=F32) for qp, kp in zip(o["qp"], o["kp"])]
        o["m"] = lax.dot_general(o["v"], o["kst"], TN_DIMS, preferred_element_type=F32)

    for u in units:
        pm = []
        for p in ops[u]["p"]:
            row = lax.broadcasted_iota(jnp.int32, p.shape, 0)
            col = lax.broadcasted_iota(jnp.int32, p.shape, 1)
            pm.append(jnp.where(row >= col, p, 0.0).astype(BF16))
        ops[u]["p"] = pm

    for u in units:
        o = ops[u]
        blocks = [None] * (L // KB)
        for j, p in enumerate(o["p"]):
            cj = jnp.dot(p, o["v"][j * KB:(j + 1) * KB], preferred_element_type=F32)
            for i in range(j, L // KB):
                piece = cj[(i - j) * KB:(i - j + 1) * KB]
                blocks[i] = piece if blocks[i] is None else blocks[i] + piece
        o["o"] = jnp.concatenate(blocks, axis=0)

    for h in range(H_HGRN):
        st = st_ref[h]
        for c in range(HGRN_SPAN):
            o = ops[c, h]
            out = o["o"] + lax.dot_general(o["qt"], st.astype(BF16), NT_DIMS, preferred_element_type=F32)
            st = o["e"] * st + o["m"]
            rows = pl.ds(row0 + c * L, L)
            gate = proj_ref[rows, OFF_G + h * DV_HGRN:OFF_G + (h + 1) * DV_HGRN]
            mix_ref[rows, D_CONV + h * DV_HGRN:D_CONV + (h + 1) * DV_HGRN] = _hgrn_out(out, hng, gate).astype(BF16)
        st_ref[h] = st


def _sink_softmax(s, sink):
    m = jnp.maximum(jnp.max(s, axis=-1, keepdims=True), sink)
    p = jnp.exp(s - m)
    return p, jnp.sum(p, axis=-1, keepdims=True) + jnp.exp(sink - m)


def _bias_kernel(tab_ref, bp_ref, bs_ref, op_ref, os_ref, *, seq):
    bk = bp_ref[...]
    for h in range(H_ATTN):
        acc = jnp.full(bk.shape, -jnp.inf, F32)
        for bkt in range(NUM_BUCKETS):
            acc = jnp.where(bk == bkt, tab_ref[bkt, h], acc)
        op_ref[h] = acc
    bk = bs_ref[...]
    row = lax.broadcasted_iota(jnp.int32, bk.shape, 0)
    for kv in range(KV_HEADS):
        acc = jnp.full(bk.shape, -jnp.inf, F32)
        for bkt in range(NUM_BUCKETS):
            val = jnp.full(bk.shape, tab_ref[bkt, kv * GROUP], F32)
            for gi in range(1, GROUP):
                val = jnp.where(row >= gi * seq, tab_ref[bkt, kv * GROUP + gi], val)
            acc = jnp.where(bk == bkt, val, acc)
        os_ref[kv] = acc


def _t5_bucket(rel):
    n = jnp.maximum(rel, 0)
    max_exact = NUM_BUCKETS // 2
    nf = jnp.maximum(n, max_exact).astype(F32)
    large = max_exact + (jnp.log(nf / max_exact) / math.log(MAX_DISTANCE / max_exact)
                         * (NUM_BUCKETS - max_exact)).astype(jnp.int32)
    large = jnp.minimum(large, NUM_BUCKETS - 1)
    return jnp.where(n < max_exact, n, large)


def _bias_tables(rel_bias, dec_seq, w_buf):
    qi = jnp.arange(ATTN_BLOCK, dtype=jnp.int32)[:, None]
    kc = jnp.arange(2 * ATTN_BLOCK, dtype=jnp.int32)[None, :]
    rel_p = qi + ATTN_BLOCK - kc
    bucket_p = jnp.where((rel_p >= 0) & (rel_p <= WINDOW), _t5_bucket(rel_p), -1)
    ts = (jnp.arange(GROUP * dec_seq, dtype=jnp.int32) % dec_seq)[:, None]
    js = jnp.arange(2 * ATTN_BLOCK, dtype=jnp.int32)[None, :]
    rel_s = w_buf + ts - js
    ok_s = (rel_s >= 0) & (rel_s <= WINDOW) & (js < w_buf + dec_seq)
    bucket_s = jnp.where(ok_s, _t5_bucket(rel_s), -1)
    return pl.pallas_call(
        functools.partial(_bias_kernel, seq=dec_seq),
        out_shape=(jax.ShapeDtypeStruct((H_ATTN, ATTN_BLOCK, 2 * ATTN_BLOCK), F32),
                   jax.ShapeDtypeStruct((KV_HEADS, GROUP * dec_seq, 2 * ATTN_BLOCK), F32)),
        in_specs=[pl.BlockSpec(memory_space=pltpu.SMEM),
                  pl.BlockSpec(memory_space=pltpu.VMEM),
                  pl.BlockSpec(memory_space=pltpu.VMEM)],
        out_specs=(pl.BlockSpec(memory_space=pltpu.VMEM), pl.BlockSpec(memory_space=pltpu.VMEM)),
        name="rel_bias_tables",
    )(rel_bias.astype(F32), bucket_p, bucket_s)


def _mod_kernel(c_ref, w_ref, b_ref, o_ref):
    s = _silu(c_ref[...]).astype(BF16)
    o_ref[...] = jnp.dot(s, w_ref[...].astype(BF16), preferred_element_type=F32) + b_ref[...]


def _modulation(c_all, w_ada, b_ada):
    depth = w_ada.shape[0]
    n = c_all.shape[0]
    return pl.pallas_call(
        _mod_kernel,
        out_shape=jax.ShapeDtypeStruct((depth, n, N_MOD * D_MODEL), F32),
        grid=(depth, N_MOD),
        in_specs=[pl.BlockSpec((n, D_MODEL), lambda l, j: (0, 0)),
                  pl.BlockSpec((None, D_MODEL, D_MODEL), lambda l, j: (l, 0, j)),
                  pl.BlockSpec((None, 1, D_MODEL), lambda l, j: (l, 0, j))],
        out_specs=pl.BlockSpec((None, n, D_MODEL), lambda l, j: (l, 0, j)),
        compiler_params=pltpu.CompilerParams(dimension_semantics=("arbitrary", "arbitrary"),
                                             vmem_limit_bytes=VMEM_LIMIT),
        name="adaln_modulation",
    )(c_all, w_ada, b_ada.reshape(depth, 1, N_MOD * D_MODEL))


def _mod_spec(mod, chunk, tiles_per_batch):
    if mod.ndim == 3:
        return pl.BlockSpec((None, 1, D_MODEL), lambda i: (i // tiles_per_batch, 0, chunk))
    return pl.BlockSpec((mod.shape[0], D_MODEL), lambda i: (0, chunk))


def _mod_rows(m, n_tokens):
    if m.shape[0] == 1:
        return m
    reps = n_tokens // m.shape[0]
    tok = lax.broadcasted_iota(jnp.int32, (n_tokens, m.shape[0]), 0)
    bat = lax.broadcasted_iota(jnp.int32, (n_tokens, m.shape[0]), 1)
    sel = jnp.where((tok >= bat * reps) & (tok < (bat + 1) * reps), 1.0, 0.0).astype(BF16)
    return _select_rows_mxu(sel, m)


def _inproj_kernel(x_ref, sh_ref, sc_ref, g_ref, w_ref, o_ref):
    n = x_ref.shape[0]
    h = _rms_rows(x_ref[...]) * g_ref[...] * (1.0 + _mod_rows(sc_ref[...], n)) + _mod_rows(sh_ref[...], n)
    o_ref[...] = jnp.dot(h.astype(BF16), w_ref[...], preferred_element_type=F32)


def _inproj(x2, mod, norm_g, w_in, layer, tile, tiles_per_batch):
    n = x2.shape[0]
    return pl.pallas_call(
        _inproj_kernel,
        out_shape=jax.ShapeDtypeStruct((n, IN_WIDTH), F32),
        grid=(n // tile,),
        in_specs=[pl.BlockSpec((tile, D_MODEL), lambda i: (i, 0)),
                  _mod_spec(mod, 0, tiles_per_batch),
                  _mod_spec(mod, 1, tiles_per_batch),
                  pl.BlockSpec((1, D_MODEL), lambda i: (0, 0)),
                  pl.BlockSpec((None, D_MODEL, IN_WIDTH), lambda i: (layer, 0, 0), pipeline_mode=pl.Buffered(1))],
        out_specs=pl.BlockSpec((tile, IN_WIDTH), lambda i: (i, 0)),
        compiler_params=pltpu.CompilerParams(dimension_semantics=("arbitrary",), vmem_limit_bytes=VMEM_LIMIT),
        name="in_projection",
    )(x2, mod, mod, norm_g.reshape(1, D_MODEL), w_in)


def _mlp_kernel(mix_ref, x_ref, g1_ref, sh_ref, sc_ref, g2_ref, ng_ref, wout_ref, wup_ref, wdn_ref, fg_ref, o_ref, *,
                final):
    n = x_ref.shape[0]
    x1 = x_ref[...] + _mod_rows(g1_ref[...], n) * jnp.dot(mix_ref[...].astype(BF16), wout_ref[...],
                                                          preferred_element_type=F32)
    h = (_rms_rows(x1) * ng_ref[...] * (1.0 + _mod_rows(sc_ref[...], n)) + _mod_rows(sh_ref[...], n)).astype(BF16)
    acc = None
    for c in range(D_FF // FF_CHUNK):
        u = jnp.dot(h, wup_ref[:, c * FF_CHUNK:(c + 1) * FF_CHUNK], preferred_element_type=F32)
        u = jnp.square(jnp.maximum(u, 0.0)).astype(BF16)
        d = jnp.dot(u, wdn_ref[c * FF_CHUNK:(c + 1) * FF_CHUNK, :], preferred_element_type=F32)
        acc = d if acc is None else acc + d
    x2 = x1 + _mod_rows(g2_ref[...], n) * acc
    if final:
        x2 = _rms_rows(x2) * fg_ref[...]
    o_ref[...] = x2


def _out_mlp(mix2, x2, mod, norm_g, w_out, w_up, w_down, final_g, layer, tile, tiles_per_batch, final):
    n = x2.shape[0]
    const = lambda i: (0, 0)
    of_layer = lambda i: (layer, 0, 0)
    return pl.pallas_call(
        functools.partial(_mlp_kernel, final=final),
        out_shape=jax.ShapeDtypeStruct((n, D_MODEL), F32),
        grid=(n // tile,),
        in_specs=[pl.BlockSpec((tile, D_MODEL), lambda i: (i, 0)),
                  pl.BlockSpec((tile, D_MODEL), lambda i: (i, 0)),
                  _mod_spec(mod, 2, tiles_per_batch),
                  _mod_spec(mod, 3, tiles_per_batch),
                  _mod_spec(mod, 4, tiles_per_batch),
                  _mod_spec(mod, 5, tiles_per_batch),
                  pl.BlockSpec((1, D_MODEL), const),
                  pl.BlockSpec((None, D_MODEL, D_MODEL), of_layer, pipeline_mode=pl.Buffered(1)),
                  pl.BlockSpec((None, D_MODEL, D_FF), of_layer, pipeline_mode=pl.Buffered(1)),
                  pl.BlockSpec((None, D_FF, D_MODEL), of_layer, pipeline_mode=pl.Buffered(1)),
                  pl.BlockSpec((1, D_MODEL), const)],
        out_specs=pl.BlockSpec((tile, D_MODEL), lambda i: (i, 0)),
        compiler_params=pltpu.CompilerParams(dimension_semantics=("arbitrary",), vmem_limit_bytes=VMEM_LIMIT),
        name="out_projection_mlp",
    )(mix2, x2, mod, mod, mod, mod, norm_g.reshape(1, D_MODEL), w_out, w_up, w_down, final_g.reshape(1, D_MODEL))


def _conv_ln_swish(acc, lng, lnb):
    mu = jnp.mean(acc, axis=-1, keepdims=True)
    xc = acc - mu
    y = xc * lax.rsqrt(jnp.mean(xc * xc, axis=-1, keepdims=True) + EPS) * lng + lnb
    return _silu(y)


def _hgrn_gates(fh, lb):
    f = lb + (1.0 - lb) * jax.nn.sigmoid(fh)
    return jnp.log(f), 1.0 - f


def _hgrn_out(o, hng, gate):
    return _rms_rows(o) * hng * _silu(gate)


def _prompt_mix_kernel(sinks_ref, x_ref, sh_ref, sc_ref, ng_ref, win_ref, convw_ref, convb_ref, lng_ref, lnb_ref,
                       hlb_ref, hng_ref, bias_ref,
                       mix_ref, convo_ref, so_ref, ko_ref, vo_ref,
                       proj_ref, abuf, kbuf, vbuf, st_ref, *, layer, tile):
    t = pl.program_id(1)
    last = pl.num_programs(1) - 1

    @pl.when(t == 0)
    def _():
        abuf[0:CONV_PAD, :] = jnp.zeros((CONV_PAD, D_CONV), F32)
        abuf[CONV_PAD + tile:CONV_PAD + tile + SUBLANES, :] = jnp.zeros((SUBLANES, D_CONV), F32)
        kbuf[0:ATTN_BLOCK, :] = jnp.zeros((ATTN_BLOCK, D_KV), BF16)
        vbuf[0:ATTN_BLOCK, :] = jnp.zeros((ATTN_BLOCK, D_KV), BF16)
        st_ref[...] = jnp.zeros(st_ref.shape, F32)

    h_in = (_rms_rows(x_ref[...]) * ng_ref[...] * (1.0 + sc_ref[...]) + sh_ref[...]).astype(BF16)
    for lo, hi in ((OFF_AVAL, OFF_Q), (OFF_F, OFF_I), (OFF_Q, OFF_F), (OFF_I, OFF_G), (OFF_G, OFF_QA),
                   (OFF_QA, IN_WIDTH)):
        proj_ref[:, lo:hi] = jnp.dot(h_in, win_ref[:, lo:hi], preferred_element_type=F32)

    abuf[CONV_PAD:CONV_PAD + tile, :] = (proj_ref[:, OFF_AVAL:OFF_AVAL + D_CONV]
                                         * jax.nn.sigmoid(proj_ref[:, OFF_AGATE:OFF_AGATE + D_CONV]))
    first_row = CONV_PAD - (CONV_WIDTH - 1)
    acc = jnp.broadcast_to(convb_ref[...], (tile, D_CONV))
    for r in range(SUBLANES):
        z = None
        for off in range(r, first_row + CONV_WIDTH, SUBLANES):
            j = off - first_row
            if j < 0:
                continue
            term = convw_ref[j:j + 1, :] * abuf[off - r:off - r + tile + SUBLANES, :]
            z = term if z is None else z + term
        acc = acc + z[r:r + tile]
    mix_ref[:, 0:D_CONV] = _conv_ln_swish(acc, lng_ref[...], lnb_ref[...]).astype(BF16)

    lb = _layer_lb(hlb_ref[...], layer)
    hng = hng_ref[...]
    span = HGRN_SPAN * HGRN_CHUNK
    ri = lax.broadcasted_iota(jnp.int32, (span, span), 0)
    ci = lax.broadcasted_iota(jnp.int32, (span, span), 1)
    tri = jnp.where((ri >= ci) & (ri // HGRN_CHUNK == ci // HGRN_CHUNK), 1.0, 0.0).astype(BF16)
    for i in range(tile // span):
        _hgrn_span(proj_ref, mix_ref, st_ref, i * span, lb, hng, tri)

    kbuf[ATTN_BLOCK:ATTN_BLOCK + tile, :] = proj_ref[:, OFF_KA:OFF_KA + D_KV].astype(BF16)
    vbuf[ATTN_BLOCK:ATTN_BLOCK + tile, :] = proj_ref[:, OFF_VA:OFF_VA + D_KV].astype(BF16)
    scale = HEAD_DIM ** -0.5
    for blk in range(tile // ATTN_BLOCK):
        r0 = blk * ATTN_BLOCK
        heads = []
        for h in range(H_ATTN):
            kv = h // GROUP
            q = (proj_ref[r0:r0 + ATTN_BLOCK, OFF_QA + h * HEAD_DIM:OFF_QA + (h + 1) * HEAD_DIM] * scale).astype(BF16)
            kall = kbuf[r0:r0 + 2 * ATTN_BLOCK, kv * HEAD_DIM:(kv + 1) * HEAD_DIM]
            vall = vbuf[r0:r0 + 2 * ATTN_BLOCK, kv * HEAD_DIM:(kv + 1) * HEAD_DIM]
            s = lax.dot_general(q, kall, NT_DIMS, preferred_element_type=F32) + bias_ref[h]
            if blk == 0:
                col = lax.broadcasted_iota(jnp.int32, s.shape, 1)
                s = jnp.where(col + (t * tile - ATTN_BLOCK) >= 0, s, -jnp.inf)
            p, den = _sink_softmax(s, sinks_ref[h])
            heads.append(jnp.dot(p.astype(BF16), vall, preferred_element_type=F32) / den)
        mix_ref[r0:r0 + ATTN_BLOCK, D_CONV + D_HGRN:D_MODEL] = jnp.concatenate(heads, axis=1).astype(BF16)

    @pl.when(t == last)
    def _():
        convo_ref[...] = abuf[CONV_PAD + tile - (CONV_WIDTH - 1):CONV_PAD + tile, :]
        for h in range(H_HGRN):
            so_ref[h] = st_ref[h].T
        ko_ref[...] = proj_ref[tile - WINDOW:tile, OFF_KA:OFF_KA + D_KV]
        vo_ref[...] = proj_ref[tile - WINDOW:tile, OFF_VA:OFF_VA + D_KV]

    abuf[0:CONV_PAD, :] = abuf[tile:tile + CONV_PAD, :]
    kbuf[0:ATTN_BLOCK, :] = kbuf[tile:tile + ATTN_BLOCK, :]
    vbuf[0:ATTN_BLOCK, :] = vbuf[tile:tile + ATTN_BLOCK, :]


def _carry_specs(carried):
    return [pl.BlockSpec(memory_space=pl.ANY)] * len(carried)


def _carried(kernel_fn, n_in, n_carried):
    if n_carried == 0:
        return kernel_fn
    return lambda *refs: kernel_fn(*refs[:n_in], *refs[n_in + n_carried:])


def _prompt_mixers(x, mod, norm_g, w_in, sinks, conv_w, conv_b, ln_g, ln_b, hgrn_lb, hng, bias_p, layer, carried):
    B, T = x.shape[:2]
    tile = MIX_TILE
    depth = hgrn_lb.shape[0]
    const2 = lambda b, t: (0, 0)
    inputs = (sinks, x, mod, mod, norm_g.reshape(1, D_MODEL), w_in, conv_w, conv_b.reshape(1, D_CONV),
              ln_g.reshape(1, D_CONV), ln_b.reshape(1, D_CONV), hgrn_lb, hng.reshape(1, DV_HGRN), bias_p)
    return pl.pallas_call(
        _carried(functools.partial(_prompt_mix_kernel, layer=layer, tile=tile), len(inputs), len(carried)),
        out_shape=(jax.ShapeDtypeStruct((B, T, D_MODEL), BF16),
                   jax.ShapeDtypeStruct((depth, B, CONV_WIDTH - 1, D_CONV), F32),
                   jax.ShapeDtypeStruct((depth, B, H_HGRN, DK_HGRN, DV_HGRN), F32),
                   jax.ShapeDtypeStruct((depth, B, WINDOW, D_KV), F32),
                   jax.ShapeDtypeStruct((depth, B, WINDOW, D_KV), F32)),
        grid=(B, T // tile),
        in_specs=[pl.BlockSpec(memory_space=pltpu.SMEM),
                  pl.BlockSpec((None, tile, D_MODEL), lambda b, t: (b, t, 0)),
                  pl.BlockSpec((None, 1, D_MODEL), lambda b, t: (b, 0, 0)),
                  pl.BlockSpec((None, 1, D_MODEL), lambda b, t: (b, 0, 1)),
                  pl.BlockSpec((1, D_MODEL), const2),
                  pl.BlockSpec((None, D_MODEL, IN_WIDTH), lambda b, t: (layer, 0, 0), pipeline_mode=pl.Buffered(1)),
                  pl.BlockSpec((CONV_WIDTH, D_CONV), const2),
                  pl.BlockSpec((1, D_CONV), const2),
                  pl.BlockSpec((1, D_CONV), const2),
                  pl.BlockSpec((1, D_CONV), const2),
                  pl.BlockSpec((depth, D_HGRN), const2),
                  pl.BlockSpec((1, DV_HGRN), const2),
                  pl.BlockSpec((H_ATTN, ATTN_BLOCK, 2 * ATTN_BLOCK), lambda b, t: (0, 0, 0))] + _carry_specs(carried),
        out_specs=(pl.BlockSpec((None, tile, D_MODEL), lambda b, t: (b, t, 0)),
                   pl.BlockSpec((None, None, CONV_WIDTH - 1, D_CONV), lambda b, t: (layer, b, 0, 0)),
                   pl.BlockSpec((None, None, H_HGRN, DK_HGRN, DV_HGRN), lambda b, t: (layer, b, 0, 0, 0)),
                   pl.BlockSpec((None, None, WINDOW, D_KV), lambda b, t: (layer, b, 0, 0)),
                   pl.BlockSpec((None, None, WINDOW, D_KV), lambda b, t: (layer, b, 0, 0))),
        input_output_aliases={len(inputs) + i: 1 + i for i in range(len(carried))},
        scratch_shapes=[pltpu.VMEM((tile, IN_WIDTH), F32),
                        pltpu.VMEM((CONV_PAD + tile + SUBLANES, D_CONV), F32),
                        pltpu.VMEM((ATTN_BLOCK + tile, D_KV), BF16),
                        pltpu.VMEM((ATTN_BLOCK + tile, D_KV), BF16),
                        pltpu.VMEM((H_HGRN, DV_HGRN, DK_HGRN), F32)],
        compiler_params=pltpu.CompilerParams(dimension_semantics=("arbitrary", "arbitrary"),
                                             vmem_limit_bytes=VMEM_LIMIT),
        name="prompt_mixers",
    )(*inputs, *carried)


def _sample_mix_kernel(sinks_ref, proj_ref, cconv_ref, state_ref, ck_ref, cv_ref, convw_ref, convb_ref, lng_ref,
                       lnb_ref, hlb_ref, hng_ref, bias_ref,
                       mix_ref, convo_ref, so_ref, ko_ref, vo_ref, full_ref, kall_ref, vall_ref, *,
                       layer, block, seq, w_buf):
    hist = CONV_WIDTH - 1
    n_keys = w_buf + seq
    pad_keys = kall_ref.shape[1]

    @pl.when(pl.program_id(0) == 0)
    def _():
        for ref in (kall_ref, vall_ref):
            ref[:, w_buf:pad_keys, :] = jnp.zeros((block, pad_keys - w_buf, D_KV), F32)

    lb = _layer_lb(hlb_ref[...], layer)
    hng = hng_ref[...]
    scale = HEAD_DIM ** -0.5
    elems = range(block)
    row8 = lax.broadcasted_iota(jnp.int32, (SUBLANES, DV_HGRN), 0)
    ones_rows = jnp.where((row8 >= seq) & (row8 < seq + 3), 1.0, 0.0)
    zrow = jnp.zeros((1, DK_HGRN), BF16)
    prow = lax.broadcasted_iota(jnp.int32, (seq, seq), 0)
    pcol = lax.broadcasted_iota(jnp.int32, (seq, seq), 1)
    grow = lax.broadcasted_iota(jnp.int32, (GROUP * seq, 1), 0)

    proj = [proj_ref[e * seq:(e + 1) * seq, :] for e in elems]

    out_a = []
    for e in elems:
        p = proj[e]
        full_ref[e, 0:hist, :] = cconv_ref[e]
        full_ref[e, hist:hist + seq, :] = (p[:, OFF_AVAL:OFF_AVAL + D_CONV]
                                           * jax.nn.sigmoid(p[:, OFF_AGATE:OFF_AGATE + D_CONV]))
        rows = [jnp.sum(convw_ref[...] * full_ref[e, s:s + CONV_WIDTH, :], axis=0, keepdims=True) for s in range(seq)]
        convo_ref[e] = full_ref[e, seq:seq + hist, :]
        out_a.append(_conv_ln_swish(jnp.concatenate(rows, axis=0) + convb_ref[...], lng_ref[...], lnb_ref[...]))

    units = [(e, h) for e in elems for h in range(H_HGRN)]
    ops = {}
    for e in elems:
        p = proj[e]
        g, k = _hgrn_gates(p[:, OFF_F:OFF_F + D_HGRN], lb)
        b = _cumsum_rows_small(g)
        for h in range(H_HGRN):
            cs = slice(h * DK_HGRN, (h + 1) * DK_HGRN)
            q = p[:, OFF_Q + h * DK_HGRN:OFF_Q + (h + 1) * DK_HGRN]
            v = p[:, OFF_I + h * DV_HGRN:OFF_I + (h + 1) * DV_HGRN]
            bb = b[:, cs]
            bl = bb[seq - 1:seq, :]
            kst = (k[:, cs] * jnp.exp(bl - bb)).astype(BF16)
            x = jnp.concatenate([kst.astype(F32), *(part.astype(F32) for part in _split3_bf16(jnp.exp(bl))),
                                 zrow.astype(F32)], axis=0).astype(BF16)
            vpad = jnp.concatenate([v, jnp.zeros((SUBLANES - seq, DV_HGRN), F32)], axis=0)
            ops[e, h] = dict(qp=(q * jnp.exp(bb - bl)).astype(BF16), kst=kst, v=v.astype(BF16),
                             qt=(q * jnp.exp(bb)).astype(BF16), x=x,
                             r=jnp.concatenate([vpad, ones_rows], axis=1).astype(BF16))
    for u in units:
        o = ops[u]
        st = state_ref[u[0], u[1]]
        o["p"] = lax.dot_general(o["qp"], o["kst"], NT_DIMS, preferred_element_type=F32)
        o["inter"] = jnp.dot(o["qt"], st.astype(BF16), preferred_element_type=F32)
        me = lax.dot_general(o["x"], o["r"], TN_DIMS, preferred_element_type=F32)
        so_ref[u[0], u[1]] = me[:, DV_HGRN:] * st + me[:, :DV_HGRN]
    out_b = {}
    for u in units:
        o = ops[u]
        pm = jnp.where(prow >= pcol, o["p"], 0.0).astype(BF16)
        out = o["inter"] + jnp.dot(pm, o["v"], preferred_element_type=F32)
        gate = proj[u[0]][:, OFF_G + u[1] * DV_HGRN:OFF_G + (u[1] + 1) * DV_HGRN]
        out_b[u] = _hgrn_out(out, hng, gate)

    for e in elems:
        p = proj[e]
        for ref, cache, new, out in ((kall_ref, ck_ref, p[:, OFF_KA:OFF_KA + D_KV], ko_ref),
                                     (vall_ref, cv_ref, p[:, OFF_VA:OFF_VA + D_KV], vo_ref)):
            ref[e, 0:w_buf, :] = cache[e]
            ref[e, w_buf:n_keys, :] = new
            out[e] = ref[e, seq:n_keys, :]
    scores = {}
    for e in elems:
        p = proj[e]
        for kv in range(KV_HEADS):
            hs = slice(kv * HEAD_DIM, (kv + 1) * HEAD_DIM)
            q2 = jnp.concatenate([p[:, OFF_QA + h * HEAD_DIM:OFF_QA + (h + 1) * HEAD_DIM]
                                  for h in range(kv * GROUP, (kv + 1) * GROUP)], axis=0)
            kall = kall_ref[e][:, hs].astype(BF16)
            scores[e, kv] = (lax.dot_general((q2 * scale).astype(BF16), kall, NT_DIMS, preferred_element_type=F32)
                             + bias_ref[kv])
    out_c = {}
    for e in elems:
        for kv in range(KV_HEADS):
            hs = slice(kv * HEAD_DIM, (kv + 1) * HEAD_DIM)
            sink = jnp.zeros((GROUP * seq, 1), F32)
            for gi in range(GROUP):
                sink = jnp.where(grow >= gi * seq, sinks_ref[kv * GROUP + gi], sink)
            pr, den = _sink_softmax(scores[e, kv], sink)
            o2 = jnp.dot(pr.astype(BF16), vall_ref[e][:, hs].astype(BF16), preferred_element_type=F32) / den
            for gi in range(GROUP):
                out_c[e, kv * GROUP + gi] = o2[gi * seq:(gi + 1) * seq]

    for e in elems:
        parts = [out_a[e]] + [out_b[e, h] for h in range(H_HGRN)] + [out_c[e, h] for h in range(H_ATTN)]
        mix_ref[e * seq:(e + 1) * seq, :] = jnp.concatenate(parts, axis=1)


def _sample_mixers(proj2, sinks, cache_conv, state, cache_k, cache_v, conv_w, conv_b, ln_g, ln_b, hgrn_lb, hng,
                   bias_s, layer, carried):
    B = state.shape[1]
    seq = proj2.shape[0] // B
    w_buf = cache_k.shape[2]
    block = SAMPLE_BLOCK
    depth = hgrn_lb.shape[0]
    hist = CONV_WIDTH - 1
    const2 = lambda i: (0, 0)
    cache_specs = [pl.BlockSpec((None, block, hist, D_CONV), lambda i: (layer, i, 0, 0)),
                   pl.BlockSpec((None, block, H_HGRN, DK_HGRN, DV_HGRN), lambda i: (layer, i, 0, 0, 0)),
                   pl.BlockSpec((None, block, w_buf, D_KV), lambda i: (layer, i, 0, 0)),
                   pl.BlockSpec((None, block, w_buf, D_KV), lambda i: (layer, i, 0, 0))]
    inputs = (sinks, proj2, cache_conv, state, cache_k, cache_v, conv_w, conv_b.reshape(1, D_CONV),
              ln_g.reshape(1, D_CONV), ln_b.reshape(1, D_CONV), hgrn_lb, hng.reshape(1, DV_HGRN), bias_s)
    return pl.pallas_call(
        _carried(functools.partial(_sample_mix_kernel, layer=layer, block=block, seq=seq, w_buf=w_buf),
                 len(inputs), len(carried)),
        out_shape=(jax.ShapeDtypeStruct((B * seq, D_MODEL), F32),
                   jax.ShapeDtypeStruct((depth, B, hist, D_CONV), F32),
                   jax.ShapeDtypeStruct((depth, B, H_HGRN, DK_HGRN, DV_HGRN), F32),
                   jax.ShapeDtypeStruct((depth, B, w_buf, D_KV), F32),
                   jax.ShapeDtypeStruct((depth, B, w_buf, D_KV), F32)),
        grid=(B // block,),
        in_specs=[pl.BlockSpec(memory_space=pltpu.SMEM),
                  pl.BlockSpec((block * seq, IN_WIDTH), lambda i: (i, 0))] + cache_specs + [
                  pl.BlockSpec((CONV_WIDTH, D_CONV), const2),
                  pl.BlockSpec((1, D_CONV), const2),
                  pl.BlockSpec((1, D_CONV), const2),
                  pl.BlockSpec((1, D_CONV), const2),
                  pl.BlockSpec((depth, D_HGRN), const2),
                  pl.BlockSpec((1, DV_HGRN), const2),
                  pl.BlockSpec((KV_HEADS, GROUP * seq, 2 * ATTN_BLOCK), lambda i: (0, 0, 0))] + _carry_specs(carried),
        out_specs=tuple([pl.BlockSpec((block * seq, D_MODEL), lambda i: (i, 0))] + cache_specs),
        input_output_aliases={len(inputs) + i: 1 + i for i in range(len(carried))},
        scratch_shapes=[pltpu.VMEM((block, hist + seq + 6, D_CONV), F32),
                        pltpu.VMEM((block, 2 * ATTN_BLOCK, D_KV), F32),
                        pltpu.VMEM((block, 2 * ATTN_BLOCK, D_KV), F32)],
        compiler_params=pltpu.CompilerParams(dimension_semantics=("arbitrary",), vmem_limit_bytes=VMEM_LIMIT),
        name="sample_mixers",
    )(*inputs, *carried)


def kernel(x_prompt, x_sample, cache_conv, state_hgrn, cache_swa_k, cache_swa_v, c_prompt, c_sample, rel_bias, w_ada, b_ada, norm_mix_g, w_in, conv_w, conv_b, conv_ln_g, conv_ln_b, hgrn_lb, hgrn_norm_g, attn_sinks, w_out, norm_mlp_g, w_up, w_down, final_g):
    Bp, Tp = x_prompt.shape[:2]
    Bs, Ts = x_sample.shape[:2]
    depth = w_in.shape[0]
    w_buf = cache_swa_k.shape[2]
    assert Tp % MIX_TILE == 0 and (Bp * Tp) % TOK_TILE == 0 and Tp % TOK_TILE == 0 and Bs % SAMPLE_BLOCK == 0
    assert w_buf == WINDOW and GROUP * Ts == SUBLANES

    bias_p, bias_s = _bias_tables(rel_bias, Ts, w_buf)
    mod = _modulation(jnp.concatenate([c_prompt, c_sample], axis=0), w_ada, b_ada)
    w_in_b, w_out_b, w_up_b, w_down_b = (w.astype(BF16) for w in (w_in, w_out, w_up, w_down))
    hlb = hgrn_lb.astype(F32)
    ck = cache_swa_k.reshape(depth, Bs, w_buf, D_KV)
    cv = cache_swa_v.reshape(depth, Bs, w_buf, D_KV)

    xp = x_prompt.reshape(Bp * Tp, D_MODEL)
    xs = x_sample.reshape(Bs * Ts, D_MODEL)
    tile_s = Bs * Ts
    caches_p = ()
    caches_s = ()
    for l in range(depth):
        final = l == depth - 1
        mod_p = mod[l, :Bp].reshape(Bp, 1, N_MOD * D_MODEL)
        mod_s = mod[l, Bp:]
        mix_p, *caches_p = _prompt_mixers(xp.reshape(Bp, Tp, D_MODEL), mod_p, norm_mix_g[l], w_in_b, attn_sinks[l],
                                          conv_w[l], conv_b[l], conv_ln_g[l], conv_ln_b[l], hlb, hgrn_norm_g[l],
                                          bias_p, l, caches_p)
        xp = _out_mlp(mix_p.reshape(Bp * Tp, D_MODEL), xp, mod_p, norm_mlp_g[l], w_out_b, w_up_b, w_down_b,
                      final_g, l, TOK_TILE, Tp // TOK_TILE, final)
        proj_s = _inproj(xs, mod_s, norm_mix_g[l], w_in_b, l, tile_s, 1)
        mix_s, *caches_s = _sample_mixers(proj_s, attn_sinks[l], cache_conv, state_hgrn, ck, cv, conv_w[l],
                                          conv_b[l], conv_ln_g[l], conv_ln_b[l], hlb, hgrn_norm_g[l], bias_s, l,
                                          caches_s)
        xs = _out_mlp(mix_s, xs, mod_s, norm_mlp_g[l], w_out_b, w_up_b, w_down_b, final_g, l, tile_s, 1, final)
    cp, sp, kp, vp = caches_p
    cs, ss, ksn, vsn = caches_s
    return (xp.reshape(Bp, Tp, D_MODEL), xs.reshape(Bs, Ts, D_MODEL), cp, cs, sp, ss,
            kp.reshape(depth, Bp, WINDOW, KV_HEADS, HEAD_DIM), ksn.reshape(depth, Bs, w_buf, KV_HEADS, HEAD_DIM),
            vp.reshape(depth, Bp, WINDOW, KV_HEADS, HEAD_DIM), vsn.reshape(depth, Bs, w_buf, KV_HEADS, HEAD_DIM))
```

```python
import functools
import math

import jax
import jax.numpy as jnp
from jax import lax
from jax.experimental import pallas as pl
from jax.experimental.pallas import tpu as pltpu

F32 = jnp.float32
BF16 = jnp.bfloat16

D_MODEL = 1024
D_CONV = 256
CONV_WIDTH = 31
H_HGRN = 4
DK_HGRN = 128
DV_HGRN = 128
D_HGRN = 512
HEAD_DIM = 64
H_ATTN = 4
KV_HEADS = 2
GROUP = H_ATTN // KV_HEADS
D_ATTN = H_ATTN * HEAD_DIM
D_KV = KV_HEADS * HEAD_DIM
WINDOW = 128
ATTN_BLOCK = 128
NUM_BUCKETS = 32
MAX_DISTANCE = 128
D_FF = 4 * D_MODEL
N_MOD = 6
EPS = 1e-6

OFF_AVAL = 0
OFF_AGATE = OFF_AVAL + D_CONV
OFF_Q = OFF_AGATE + D_CONV
OFF_F = OFF_Q + H_HGRN * DK_HGRN
OFF_I = OFF_F + H_HGRN * DK_HGRN
OFF_G = OFF_I + D_HGRN
OFF_QA = OFF_G + D_HGRN
OFF_KA = OFF_QA + D_ATTN
OFF_VA = OFF_KA + D_KV
IN_WIDTH = OFF_VA + D_KV

HGRN_CHUNK = 64
HGRN_KEYBLOCK = 32
HGRN_SPAN = 4
SUBLANES = 8
CONV_PAD = 32
MIX_TILE = 256
TOK_TILE = 512
SAMPLE_BLOCK = 8
FF_CHUNK = 1024
VMEM_LIMIT = 56 * 1024 * 1024

NT_DIMS = (((1,), (1,)), ((), ()))
TN_DIMS = (((0,), (0,)), ((), ()))


def _silu(x):
    return x * jax.nn.sigmoid(x)


def _rms_rows(x):
    return x * lax.rsqrt(jnp.mean(x * x, axis=-1, keepdims=True) + EPS)


def _layer_lb(hlb, layer):
    m = jnp.max(hlb, axis=0, keepdims=True)
    e = jnp.exp(hlb - m)
    p = e / jnp.sum(e, axis=0, keepdims=True)
    lb = jnp.zeros_like(m)
    for i in range(1, layer + 1):
        lb = lb + p[i:i + 1, :]
    return lb


def _split3_bf16(x):
    hi = x.astype(BF16)
    r = x - hi.astype(F32)
    mid = r.astype(BF16)
    return hi, mid, (r - mid.astype(F32)).astype(BF16)


def _select_rows_mxu(sel, x):
    return sum(jnp.dot(sel, part, preferred_element_type=F32) for part in _split3_bf16(x))


def _cumsum_rows_small(g):
    row = lax.broadcasted_iota(jnp.int32, g.shape, 0)
    b = jnp.zeros_like(g)
    for u in range(g.shape[0]):
        b = b + jnp.where(row >= u, g[u:u + 1, :], 0.0)
    return b


def _hgrn_span(proj_ref, mix_ref, st_ref, row0, lb, hng, tri, tick):
    L, KB = HGRN_CHUNK, HGRN_KEYBLOCK
    span = HGRN_SPAN * L
    g, k = _hgrn_gates(proj_ref[pl.ds(row0, span), OFF_F:OFF_F + D_HGRN], lb)
    b = _select_rows_mxu(tri, g)
    units = [(c, h) for c in range(HGRN_SPAN) for h in range(H_HGRN)]

    ops = {}
    for c, h in units:
        rows = pl.ds(row0 + c * L, L)
        cs = slice(h * DK_HGRN, (h + 1) * DK_HGRN)
        q = proj_ref[rows, OFF_Q + h * DK_HGRN:OFF_Q + (h + 1) * DK_HGRN]
        v = proj_ref[rows, OFF_I + h * DV_HGRN:OFF_I + (h + 1) * DV_HGRN].astype(BF16)
        kk = k[c * L:(c + 1) * L, cs]
        bb = b[c * L:(c + 1) * L, cs]
        qp, kp = [], []
        for lo in range(0, L, KB):
            r = bb[lo + KB // 2 - 1:lo + KB // 2, :]
            kp.append((kk[lo:lo + KB] * jnp.exp(r - bb[lo:lo + KB])).astype(BF16))
            qp.append((q[lo:] * jnp.exp(bb[lo:] - r)).astype(BF16))
        bl = bb[L - 1:L, :]
        ops[c, h] = dict(qp=qp, kp=kp, v=v, qt=(q * jnp.exp(bb)).astype(BF16),
                         kst=(kk * jnp.exp(bl - bb)).astype(BF16), e=jnp.exp(bl))
    tick()

    for u in units:
        o = ops[u]
        o["p"] = [lax.dot_general(qp, kp, NT_DIMS, preferred_element_type=F32) for qp, kp in zip(o["qp"], o["kp"])]
        o["m"] = lax.dot_general(o["v"], o["kst"], TN_DIMS, preferred_element_type=F32)
    tick()

    for u in units:
        pm = []
        for p in ops[u]["p"]:
            row = lax.broadcasted_iota(jnp.int32, p.shape, 0)
            col = lax.broadcasted_iota(jnp.int32, p.shape, 1)
            pm.append(jnp.where(row >= col, p, 0.0).astype(BF16))
        ops[u]["p"] = pm
    tick()

    for u in units:
        o = ops[u]
        blocks = [None] * (L // KB)
        for j, p in enumerate(o["p"]):
            cj = jnp.dot(p, o["v"][j * KB:(j + 1) * KB], preferred_element_type=F32)
            for i in range(j, L // KB):
                piece = cj[(i - j) * KB:(i - j + 1) * KB]
                blocks[i] = piece if blocks[i] is None else blocks[i] + piece
        o["o"] = jnp.concatenate(blocks, axis=0)
    tick()

    for h in range(H_HGRN):
        st = st_ref[h]
        for c in range(HGRN_SPAN):
            o = ops[c, h]
            out = o["o"] + lax.dot_general(o["qt"], st.astype(BF16), NT_DIMS, preferred_element_type=F32)
            st = o["e"] * st + o["m"]
            rows = pl.ds(row0 + c * L, L)
            gate = proj_ref[rows, OFF_G + h * DV_HGRN:OFF_G + (h + 1) * DV_HGRN]
            mix_ref[rows, D_CONV + h * DV_HGRN:D_CONV + (h + 1) * DV_HGRN] = _hgrn_out(out, hng, gate).astype(BF16)
        st_ref[h] = st


def _sink_softmax(s, sink):
    m = jnp.maximum(jnp.max(s, axis=-1, keepdims=True), sink)
    p = jnp.exp(s - m)
    return p, jnp.sum(p, axis=-1, keepdims=True) + jnp.exp(sink - m)


def _bias_kernel(tab_ref, bp_ref, bs_ref, op_ref, os_ref, *, seq):
    bk = bp_ref[...]
    for h in range(H_ATTN):
        acc = jnp.full(bk.shape, -jnp.inf, F32)
        for bkt in range(NUM_BUCKETS):
            acc = jnp.where(bk == bkt, tab_ref[bkt, h], acc)
        op_ref[h] = acc
    bk = bs_ref[...]
    row = lax.broadcasted_iota(jnp.int32, bk.shape, 0)
    for kv in range(KV_HEADS):
        acc = jnp.full(bk.shape, -jnp.inf, F32)
        for bkt in range(NUM_BUCKETS):
            val = jnp.full(bk.shape, tab_ref[bkt, kv * GROUP], F32)
            for gi in range(1, GROUP):
                val = jnp.where(row >= gi * seq, tab_ref[bkt, kv * GROUP + gi], val)
            acc = jnp.where(bk == bkt, val, acc)
        os_ref[kv] = acc


def _t5_bucket(rel):
    n = jnp.maximum(rel, 0)
    max_exact = NUM_BUCKETS // 2
    nf = jnp.maximum(n, max_exact).astype(F32)
    large = max_exact + (jnp.log(nf / max_exact) / math.log(MAX_DISTANCE / max_exact)
                         * (NUM_BUCKETS - max_exact)).astype(jnp.int32)
    large = jnp.minimum(large, NUM_BUCKETS - 1)
    return jnp.where(n < max_exact, n, large)


def _bias_tables(rel_bias, dec_seq, w_buf):
    qi = jnp.arange(ATTN_BLOCK, dtype=jnp.int32)[:, None]
    kc = jnp.arange(2 * ATTN_BLOCK, dtype=jnp.int32)[None, :]
    rel_p = qi + ATTN_BLOCK - kc
    bucket_p = jnp.where((rel_p >= 0) & (rel_p <= WINDOW), _t5_bucket(rel_p), -1)
    ts = (jnp.arange(GROUP * dec_seq, dtype=jnp.int32) % dec_seq)[:, None]
    js = jnp.arange(2 * ATTN_BLOCK, dtype=jnp.int32)[None, :]
    rel_s = w_buf + ts - js
    ok_s = (rel_s >= 0) & (rel_s <= WINDOW) & (js < w_buf + dec_seq)
    bucket_s = jnp.where(ok_s, _t5_bucket(rel_s), -1)
    return pl.pallas_call(
        functools.partial(_bias_kernel, seq=dec_seq),
        out_shape=(jax.ShapeDtypeStruct((H_ATTN, ATTN_BLOCK, 2 * ATTN_BLOCK), F32),
                   jax.ShapeDtypeStruct((KV_HEADS, GROUP * dec_seq, 2 * ATTN_BLOCK), F32)),
        in_specs=[pl.BlockSpec(memory_space=pltpu.SMEM),
                  pl.BlockSpec(memory_space=pltpu.VMEM),
                  pl.BlockSpec(memory_space=pltpu.VMEM)],
        out_specs=(pl.BlockSpec(memory_space=pltpu.VMEM), pl.BlockSpec(memory_space=pltpu.VMEM)),
        name="rel_bias_tables",
    )(rel_bias.astype(F32), bucket_p, bucket_s)


def _mod_kernel(c_ref, w_ref, b_ref, o_ref):
    s = _silu(c_ref[...]).astype(BF16)
    o_ref[...] = jnp.dot(s, w_ref[...].astype(BF16), preferred_element_type=F32) + b_ref[...]


def _modulation(c_all, w_ada, b_ada):
    depth = w_ada.shape[0]
    n = c_all.shape[0]
    return pl.pallas_call(
        _mod_kernel,
        out_shape=jax.ShapeDtypeStruct((depth, n, N_MOD * D_MODEL), F32),
        grid=(depth, N_MOD),
        in_specs=[pl.BlockSpec((n, D_MODEL), lambda l, j: (0, 0)),
                  pl.BlockSpec((None, D_MODEL, D_MODEL), lambda l, j: (l, 0, j)),
                  pl.BlockSpec((None, 1, D_MODEL), lambda l, j: (l, 0, j))],
        out_specs=pl.BlockSpec((None, n, D_MODEL), lambda l, j: (l, 0, j)),
        compiler_params=pltpu.CompilerParams(dimension_semantics=("arbitrary", "arbitrary"),
                                             vmem_limit_bytes=VMEM_LIMIT),
        name="adaln_modulation",
    )(c_all, w_ada, b_ada.reshape(depth, 1, N_MOD * D_MODEL))


def _mod_spec(mod, chunk, tiles_per_batch):
    if mod.ndim == 3:
        return pl.BlockSpec((None, 1, D_MODEL), lambda i: (i // tiles_per_batch, 0, chunk))
    return pl.BlockSpec((mod.shape[0], D_MODEL), lambda i: (0, chunk))


def _mod_rows(m, n_tokens):
    if m.shape[0] == 1:
        return m
    reps = n_tokens // m.shape[0]
    tok = lax.broadcasted_iota(jnp.int32, (n_tokens, m.shape[0]), 0)
    bat = lax.broadcasted_iota(jnp.int32, (n_tokens, m.shape[0]), 1)
    sel = jnp.where((tok >= bat * reps) & (tok < (bat + 1) * reps), 1.0, 0.0).astype(BF16)
    return _select_rows_mxu(sel, m)


def _inproj_kernel(x_ref, sh_ref, sc_ref, g_ref, w_ref, o_ref):
    n = x_ref.shape[0]
    h = _rms_rows(x_ref[...]) * g_ref[...] * (1.0 + _mod_rows(sc_ref[...], n)) + _mod_rows(sh_ref[...], n)
    o_ref[...] = jnp.dot(h.astype(BF16), w_ref[...], preferred_element_type=F32)


def _inproj(x2, mod, norm_g, w_in, layer, tile, tiles_per_batch):
    n = x2.shape[0]
    return pl.pallas_call(
        _inproj_kernel,
        out_shape=jax.ShapeDtypeStruct((n, IN_WIDTH), F32),
        grid=(n // tile,),
        in_specs=[pl.BlockSpec((tile, D_MODEL), lambda i: (i, 0)),
                  _mod_spec(mod, 0, tiles_per_batch),
                  _mod_spec(mod, 1, tiles_per_batch),
                  pl.BlockSpec((1, D_MODEL), lambda i: (0, 0)),
                  pl.BlockSpec((None, D_MODEL, IN_WIDTH), lambda i: (layer, 0, 0), pipeline_mode=pl.Buffered(1))],
        out_specs=pl.BlockSpec((tile, IN_WIDTH), lambda i: (i, 0)),
        compiler_params=pltpu.CompilerParams(dimension_semantics=("arbitrary",), vmem_limit_bytes=VMEM_LIMIT),
        name="in_projection",
    )(x2, mod, mod, norm_g.reshape(1, D_MODEL), w_in)


def _mlp_kernel(mix_ref, x_ref, g1_ref, sh_ref, sc_ref, g2_ref, ng_ref, wout_ref, wup_ref, wdn_ref, fg_ref, o_ref, *,
                final):
    n = x_ref.shape[0]
    x1 = x_ref[...] + _mod_rows(g1_ref[...], n) * jnp.dot(mix_ref[...].astype(BF16), wout_ref[...],
                                                          preferred_element_type=F32)
    h = (_rms_rows(x1) * ng_ref[...] * (1.0 + _mod_rows(sc_ref[...], n)) + _mod_rows(sh_ref[...], n)).astype(BF16)
    acc = None
    for c in range(D_FF // FF_CHUNK):
        u = jnp.dot(h, wup_ref[:, c * FF_CHUNK:(c + 1) * FF_CHUNK], preferred_element_type=F32)
        u = jnp.square(jnp.maximum(u, 0.0)).astype(BF16)
        d = jnp.dot(u, wdn_ref[c * FF_CHUNK:(c + 1) * FF_CHUNK, :], preferred_element_type=F32)
        acc = d if acc is None else acc + d
    x2 = x1 + _mod_rows(g2_ref[...], n) * acc
    if final:
        x2 = _rms_rows(x2) * fg_ref[...]
    o_ref[...] = x2


def _out_mlp(mix2, x2, mod, norm_g, w_out, w_up, w_down, final_g, layer, tile, tiles_per_batch, final):
    n = x2.shape[0]
    const = lambda i: (0, 0)
    of_layer = lambda i: (layer, 0, 0)
    return pl.pallas_call(
        functools.partial(_mlp_kernel, final=final),
        out_shape=jax.ShapeDtypeStruct((n, D_MODEL), F32),
        grid=(n // tile,),
        in_specs=[pl.BlockSpec((tile, D_MODEL), lambda i: (i, 0)),
                  pl.BlockSpec((tile, D_MODEL), lambda i: (i, 0)),
                  _mod_spec(mod, 2, tiles_per_batch),
                  _mod_spec(mod, 3, tiles_per_batch),
                  _mod_spec(mod, 4, tiles_per_batch),
                  _mod_spec(mod, 5, tiles_per_batch),
                  pl.BlockSpec((1, D_MODEL), const),
                  pl.BlockSpec((None, D_MODEL, D_MODEL), of_layer, pipeline_mode=pl.Buffered(1)),
                  pl.BlockSpec((None, D_MODEL, D_FF), of_layer, pipeline_mode=pl.Buffered(1)),
                  pl.BlockSpec((None, D_FF, D_MODEL), of_layer, pipeline_mode=pl.Buffered(1)),
                  pl.BlockSpec((1, D_MODEL), const)],
        out_specs=pl.BlockSpec((tile, D_MODEL), lambda i: (i, 0)),
        compiler_params=pltpu.CompilerParams(dimension_semantics=("arbitrary",), vmem_limit_bytes=VMEM_LIMIT),
        name="out_projection_mlp",
    )(mix2, x2, mod, mod, mod, mod, norm_g.reshape(1, D_MODEL), w_out, w_up, w_down, final_g.reshape(1, D_MODEL))


def _conv_ln_swish(acc, lng, lnb):
    mu = jnp.mean(acc, axis=-1, keepdims=True)
    xc = acc - mu
    y = xc * lax.rsqrt(jnp.mean(xc * xc, axis=-1, keepdims=True) + EPS) * lng + lnb
    return _silu(y)


def _hgrn_gates(fh, lb):
    f = lb + (1.0 - lb) * jax.nn.sigmoid(fh)
    return jnp.log(f), 1.0 - f


def _hgrn_out(o, hng, gate):
    return _rms_rows(o) * hng * _silu(gate)


def _ticker(pieces):
    it = iter(pieces)

    def tick():
        piece = next(it, None)
        if piece is not None:
            piece()

    def flush():
        for piece in it:
            piece()

    tick.flush = flush
    return tick


def _prompt_mix_kernel(sinks_ref, x_ref, sh_ref, sc_ref, ng_ref, win_ref, convw_ref, convb_ref, lng_ref, lnb_ref,
                       hlb_ref, hng_ref, bias_ref,
                       mix_ref, convo_ref, so_ref, ko_ref, vo_ref,
                       proj_ref, abuf, kbuf, vbuf, st_ref, *, layer, tile):
    t = pl.program_id(1)
    last = pl.num_programs(1) - 1

    @pl.when(t == 0)
    def _():
        abuf[0:CONV_PAD, :] = jnp.zeros((CONV_PAD, D_CONV), F32)
        abuf[CONV_PAD + tile:CONV_PAD + tile + SUBLANES, :] = jnp.zeros((SUBLANES, D_CONV), F32)
        kbuf[0:ATTN_BLOCK, :] = jnp.zeros((ATTN_BLOCK, D_KV), BF16)
        vbuf[0:ATTN_BLOCK, :] = jnp.zeros((ATTN_BLOCK, D_KV), BF16)
        st_ref[...] = jnp.zeros(st_ref.shape, F32)

    h_in = (_rms_rows(x_ref[...]) * ng_ref[...] * (1.0 + sc_ref[...]) + sh_ref[...]).astype(BF16)
    for lo, hi in ((OFF_AVAL, OFF_Q), (OFF_F, OFF_I), (OFF_Q, OFF_F), (OFF_I, OFF_G), (OFF_G, OFF_QA),
                   (OFF_QA, IN_WIDTH)):
        proj_ref[:, lo:hi] = jnp.dot(h_in, win_ref[:, lo:hi], preferred_element_type=F32)

    kbuf[ATTN_BLOCK:ATTN_BLOCK + tile, :] = proj_ref[:, OFF_KA:OFF_KA + D_KV].astype(BF16)
    vbuf[ATTN_BLOCK:ATTN_BLOCK + tile, :] = proj_ref[:, OFF_VA:OFF_VA + D_KV].astype(BF16)
    scale = HEAD_DIM ** -0.5
    attn = {}

    def attn_scores(blk):
        def run():
            r0 = blk * ATTN_BLOCK
            for h in range(H_ATTN):
                kv = h // GROUP
                q = (proj_ref[r0:r0 + ATTN_BLOCK, OFF_QA + h * HEAD_DIM:OFF_QA + (h + 1) * HEAD_DIM]
                     * scale).astype(BF16)
                kall = kbuf[r0:r0 + 2 * ATTN_BLOCK, kv * HEAD_DIM:(kv + 1) * HEAD_DIM]
                attn[blk, h] = lax.dot_general(q, kall, NT_DIMS, preferred_element_type=F32)
        return run

    def attn_softmax(blk):
        def run():
            for h in range(H_ATTN):
                s = attn[blk, h] + bias_ref[h]
                if blk == 0:
                    col = lax.broadcasted_iota(jnp.int32, s.shape, 1)
                    s = jnp.where(col + (t * tile - ATTN_BLOCK) >= 0, s, -jnp.inf)
                p, den = _sink_softmax(s, sinks_ref[h])
                attn[blk, h] = (p.astype(BF16), den)
        return run

    def attn_values(blk):
        def run():
            r0 = blk * ATTN_BLOCK
            heads = []
            for h in range(H_ATTN):
                kv = h // GROUP
                p, den = attn[blk, h]
                vall = vbuf[r0:r0 + 2 * ATTN_BLOCK, kv * HEAD_DIM:(kv + 1) * HEAD_DIM]
                heads.append(jnp.dot(p, vall, preferred_element_type=F32) / den)
            mix_ref[r0:r0 + ATTN_BLOCK, D_CONV + D_HGRN:D_MODEL] = jnp.concatenate(heads, axis=1).astype(BF16)
        return run

    tick = _ticker([stage(blk) for blk in range(tile // ATTN_BLOCK)
                    for stage in (attn_scores, attn_softmax, attn_values)])

    abuf[CONV_PAD:CONV_PAD + tile, :] = (proj_ref[:, OFF_AVAL:OFF_AVAL + D_CONV]
                                         * jax.nn.sigmoid(proj_ref[:, OFF_AGATE:OFF_AGATE + D_CONV]))
    first_row = CONV_PAD - (CONV_WIDTH - 1)
    acc = jnp.broadcast_to(convb_ref[...], (tile, D_CONV))
    for r in range(SUBLANES):
        z = None
        for off in range(r, first_row + CONV_WIDTH, SUBLANES):
            j = off - first_row
            if j < 0:
                continue
            term = convw_ref[j:j + 1, :] * abuf[off - r:off - r + tile + SUBLANES, :]
            z = term if z is None else z + term
        acc = acc + (z[0:tile] if r == 0 else pltpu.roll(z, tile + SUBLANES - r, 0)[0:tile])
    mix_ref[:, 0:D_CONV] = _conv_ln_swish(acc, lng_ref[...], lnb_ref[...]).astype(BF16)
    tick()

    lb = _layer_lb(hlb_ref[...], layer)
    hng = hng_ref[...]
    span = HGRN_SPAN * HGRN_CHUNK
    ri = lax.broadcasted_iota(jnp.int32, (span, span), 0)
    ci = lax.broadcasted_iota(jnp.int32, (span, span), 1)
    tri = jnp.where((ri >= ci) & (ri // HGRN_CHUNK == ci // HGRN_CHUNK), 1.0, 0.0).astype(BF16)
    for i in range(tile // span):
        _hgrn_span(proj_ref, mix_ref, st_ref, i * span, lb, hng, tri, tick)
    tick.flush()

    @pl.when(t == last)
    def _():
        convo_ref[...] = abuf[CONV_PAD + tile - (CONV_WIDTH - 1):CONV_PAD + tile, :]
        for h in range(H_HGRN):
            so_ref[h] = st_ref[h].T
        ko_ref[...] = proj_ref[tile - WINDOW:tile, OFF_KA:OFF_KA + D_KV]
        vo_ref[...] = proj_ref[tile - WINDOW:tile, OFF_VA:OFF_VA + D_KV]

    abuf[0:CONV_PAD, :] = abuf[tile:tile + CONV_PAD, :]
    kbuf[0:ATTN_BLOCK, :] = kbuf[tile:tile + ATTN_BLOCK, :]
    vbuf[0:ATTN_BLOCK, :] = vbuf[tile:tile + ATTN_BLOCK, :]


def _carry_specs(carried):
    return [pl.BlockSpec(memory_space=pl.ANY)] * len(carried)


def _carried(kernel_fn, n_in, n_carried):
    if n_carried == 0:
        return kernel_fn
    return lambda *refs: kernel_fn(*refs[:n_in], *refs[n_in + n_carried:])


def _prompt_mixers(x, mod, norm_g, w_in, sinks, conv_w, conv_b, ln_g, ln_b, hgrn_lb, hng, bias_p, layer, carried):
    B, T = x.shape[:2]
    tile = MIX_TILE
    depth = hgrn_lb.shape[0]
    const2 = lambda b, t: (0, 0)
    inputs = (sinks, x, mod, mod, norm_g.reshape(1, D_MODEL), w_in, conv_w, conv_b.reshape(1, D_CONV),
              ln_g.reshape(1, D_CONV), ln_b.reshape(1, D_CONV), hgrn_lb, hng.reshape(1, DV_HGRN), bias_p)
    return pl.pallas_call(
        _carried(functools.partial(_prompt_mix_kernel, layer=layer, tile=tile), len(inputs), len(carried)),
        out_shape=(jax.ShapeDtypeStruct((B, T, D_MODEL), BF16),
                   jax.ShapeDtypeStruct((depth, B, CONV_WIDTH - 1, D_CONV), F32),
                   jax.ShapeDtypeStruct((depth, B, H_HGRN, DK_HGRN, DV_HGRN), F32),
                   jax.ShapeDtypeStruct((depth, B, WINDOW, D_KV), F32),
                   jax.ShapeDtypeStruct((depth, B, WINDOW, D_KV), F32)),
        grid=(B, T // tile),
        in_specs=[pl.BlockSpec(memory_space=pltpu.SMEM),
                  pl.BlockSpec((None, tile, D_MODEL), lambda b, t: (b, t, 0)),
                  pl.BlockSpec((None, 1, D_MODEL), lambda b, t: (b, 0, 0)),
                  pl.BlockSpec((None, 1, D_MODEL), lambda b, t: (b, 0, 1)),
                  pl.BlockSpec((1, D_MODEL), const2),
                  pl.BlockSpec((None, D_MODEL, IN_WIDTH), lambda b, t: (layer, 0, 0), pipeline_mode=pl.Buffered(1)),
                  pl.BlockSpec((CONV_WIDTH, D_CONV), const2),
                  pl.BlockSpec((1, D_CONV), const2),
                  pl.BlockSpec((1, D_CONV), const2),
                  pl.BlockSpec((1, D_CONV), const2),
                  pl.BlockSpec((depth, D_HGRN), const2),
                  pl.BlockSpec((1, DV_HGRN), const2),
                  pl.BlockSpec((H_ATTN, ATTN_BLOCK, 2 * ATTN_BLOCK), lambda b, t: (0, 0, 0))] + _carry_specs(carried),
        out_specs=(pl.BlockSpec((None, tile, D_MODEL), lambda b, t: (b, t, 0)),
                   pl.BlockSpec((None, None, CONV_WIDTH - 1, D_CONV), lambda b, t: (layer, b, 0, 0)),
                   pl.BlockSpec((None, None, H_HGRN, DK_HGRN, DV_HGRN), lambda b, t: (layer, b, 0, 0, 0)),
                   pl.BlockSpec((None, None, WINDOW, D_KV), lambda b, t: (layer, b, 0, 0)),
                   pl.BlockSpec((None, None, WINDOW, D_KV), lambda b, t: (layer, b, 0, 0))),
        input_output_aliases={len(inputs) + i: 1 + i for i in range(len(carried))},
        scratch_shapes=[pltpu.VMEM((tile, IN_WIDTH), F32),
                        pltpu.VMEM((CONV_PAD + tile + SUBLANES, D_CONV), F32),
                        pltpu.VMEM((ATTN_BLOCK + tile, D_KV), BF16),
                        pltpu.VMEM((ATTN_BLOCK + tile, D_KV), BF16),
                        pltpu.VMEM((H_HGRN, DV_HGRN, DK_HGRN), F32)],
        compiler_params=pltpu.CompilerParams(dimension_semantics=("arbitrary", "arbitrary"),
                                             vmem_limit_bytes=VMEM_LIMIT),
        name="prompt_mixers",
    )(*inputs, *carried)


def _sample_mix_kernel(sinks_ref, proj_ref, cconv_ref, state_ref, ck_ref, cv_ref, convw_ref, convb_ref, lng_ref,
                       lnb_ref, hlb_ref, hng_ref, bias_ref,
                       mix_ref, convo_ref, so_ref, ko_ref, vo_ref, full_ref, kall_ref, vall_ref, *,
                       layer, block, seq, w_buf):
    hist = CONV_WIDTH - 1
    n_keys = w_buf + seq
    pad_keys = kall_ref.shape[1]

    @pl.when(pl.program_id(0) == 0)
    def _():
        for ref in (kall_ref, vall_ref):
            ref[:, w_buf:pad_keys, :] = jnp.zeros((block, pad_keys - w_buf, D_KV), F32)

    lb = _layer_lb(hlb_ref[...], layer)
    hng = hng_ref[...]
    scale = HEAD_DIM ** -0.5
    elems = range(block)
    row8 = lax.broadcasted_iota(jnp.int32, (SUBLANES, DV_HGRN), 0)
    ones_rows = jnp.where((row8 >= seq) & (row8 < seq + 3), 1.0, 0.0)
    zrow = jnp.zeros((1, DK_HGRN), BF16)
    prow = lax.broadcasted_iota(jnp.int32, (seq, seq), 0)
    pcol = lax.broadcasted_iota(jnp.int32, (seq, seq), 1)
    grow = lax.broadcasted_iota(jnp.int32, (GROUP * seq, 1), 0)

    proj = [proj_ref[e * seq:(e + 1) * seq, :] for e in elems]

    out_a = []
    for e in elems:
        p = proj[e]
        full_ref[e, 0:hist, :] = cconv_ref[e]
        full_ref[e, hist:hist + seq, :] = (p[:, OFF_AVAL:OFF_AVAL + D_CONV]
                                           * jax.nn.sigmoid(p[:, OFF_AGATE:OFF_AGATE + D_CONV]))
        rows = [jnp.sum(convw_ref[...] * full_ref[e, s:s + CONV_WIDTH, :], axis=0, keepdims=True) for s in range(seq)]
        convo_ref[e] = full_ref[e, seq:seq + hist, :]
        out_a.append(_conv_ln_swish(jnp.concatenate(rows, axis=0) + convb_ref[...], lng_ref[...], lnb_ref[...]))

    units = [(e, h) for e in elems for h in range(H_HGRN)]
    ops = {}
    for e in elems:
        p = proj[e]
        g, k = _hgrn_gates(p[:, OFF_F:OFF_F + D_HGRN], lb)
        b = _cumsum_rows_small(g)
        for h in range(H_HGRN):
            cs = slice(h * DK_HGRN, (h + 1) * DK_HGRN)
            q = p[:, OFF_Q + h * DK_HGRN:OFF_Q + (h + 1) * DK_HGRN]
            v = p[:, OFF_I + h * DV_HGRN:OFF_I + (h + 1) * DV_HGRN]
            bb = b[:, cs]
            bl = bb[seq - 1:seq, :]
            kst = (k[:, cs] * jnp.exp(bl - bb)).astype(BF16)
            x = jnp.concatenate([kst.astype(F32), *(part.astype(F32) for part in _split3_bf16(jnp.exp(bl))),
                                 zrow.astype(F32)], axis=0).astype(BF16)
            vpad = jnp.concatenate([v, jnp.zeros((SUBLANES - seq, DV_HGRN), F32)], axis=0)
            ops[e, h] = dict(qp=(q * jnp.exp(bb - bl)).astype(BF16), kst=kst, v=v.astype(BF16),
                             qt=(q * jnp.exp(bb)).astype(BF16), x=x,
                             r=jnp.concatenate([vpad, ones_rows], axis=1).astype(BF16))
    for u in units:
        o = ops[u]
        st = state_ref[u[0], u[1]]
        o["p"] = lax.dot_general(o["qp"], o["kst"], NT_DIMS, preferred_element_type=F32)
        o["inter"] = jnp.dot(o["qt"], st.astype(BF16), preferred_element_type=F32)
        me = lax.dot_general(o["x"], o["r"], TN_DIMS, preferred_element_type=F32)
        so_ref[u[0], u[1]] = me[:, DV_HGRN:] * st + me[:, :DV_HGRN]
    out_b = {}
    for u in units:
        o = ops[u]
        pm = jnp.where(prow >= pcol, o["p"], 0.0).astype(BF16)
        out = o["inter"] + jnp.dot(pm, o["v"], preferred_element_type=F32)
        gate = proj[u[0]][:, OFF_G + u[1] * DV_HGRN:OFF_G + (u[1] + 1) * DV_HGRN]
        out_b[u] = _hgrn_out(out, hng, gate)

    for e in elems:
        p = proj[e]
        for ref, cache, new, out in ((kall_ref, ck_ref, p[:, OFF_KA:OFF_KA + D_KV], ko_ref),
                                     (vall_ref, cv_ref, p[:, OFF_VA:OFF_VA + D_KV], vo_ref)):
            ref[e, 0:w_buf, :] = cache[e]
            ref[e, w_buf:n_keys, :] = new
            out[e] = ref[e, seq:n_keys, :]
    scores = {}
    for e in elems:
        p = proj[e]
        for kv in range(KV_HEADS):
            hs = slice(kv * HEAD_DIM, (kv + 1) * HEAD_DIM)
            q2 = jnp.concatenate([p[:, OFF_QA + h * HEAD_DIM:OFF_QA + (h + 1) * HEAD_DIM]
                                  for h in range(kv * GROUP, (kv + 1) * GROUP)], axis=0)
            kall = kall_ref[e][:, hs].astype(BF16)
            scores[e, kv] = (lax.dot_general((q2 * scale).astype(BF16), kall, NT_DIMS, preferred_element_type=F32)
                             + bias_ref[kv])
    out_c = {}
    for e in elems:
        for kv in range(KV_HEADS):
            hs = slice(kv * HEAD_DIM, (kv + 1) * HEAD_DIM)
            sink = jnp.zeros((GROUP * seq, 1), F32)
            for gi in range(GROUP):
                sink = jnp.where(grow >= gi * seq, sinks_ref[kv * GROUP + gi], sink)
            pr, den = _sink_softmax(scores[e, kv], sink)
            o2 = jnp.dot(pr.astype(BF16), vall_ref[e][:, hs].astype(BF16), preferred_element_type=F32) / den
            for gi in range(GROUP):
                out_c[e, kv * GROUP + gi] = o2[gi * seq:(gi + 1) * seq]

    for e in elems:
        parts = [out_a[e]] + [out_b[e, h] for h in range(H_HGRN)] + [out_c[e, h] for h in range(H_ATTN)]
        mix_ref[e * seq:(e + 1) * seq, :] = jnp.concatenate(parts, axis=1)


def _sample_mixers(proj2, sinks, cache_conv, state, cache_k, cache_v, conv_w, conv_b, ln_g, ln_b, hgrn_lb, hng,
                   bias_s, layer, carried):
    B = state.shape[1]
    seq = proj2.shape[0] // B
    w_buf = cache_k.shape[2]
    block = SAMPLE_BLOCK
    depth = hgrn_lb.shape[0]
    hist = CONV_WIDTH - 1
    const2 = lambda i: (0, 0)
    cache_specs = [pl.BlockSpec((None, block, hist, D_CONV), lambda i: (layer, i, 0, 0)),
                   pl.BlockSpec((None, block, H_HGRN, DK_HGRN, DV_HGRN), lambda i: (layer, i, 0, 0, 0)),
                   pl.BlockSpec((None, block, w_buf, D_KV), lambda i: (layer, i, 0, 0)),
                   pl.BlockSpec((None, block, w_buf, D_KV), lambda i: (layer, i, 0, 0))]
    inputs = (sinks, proj2, cache_conv, state, cache_k, cache_v, conv_w, conv_b.reshape(1, D_CONV),
              ln_g.reshape(1, D_CONV), ln_b.reshape(1, D_CONV), hgrn_lb, hng.reshape(1, DV_HGRN), bias_s)
    return pl.pallas_call(
        _carried(functools.partial(_sample_mix_kernel, layer=layer, block=block, seq=seq, w_buf=w_buf),
                 len(inputs), len(carried)),
        out_shape=(jax.ShapeDtypeStruct((B * seq, D_MODEL), F32),
                   jax.ShapeDtypeStruct((depth, B, hist, D_CONV), F32),
                   jax.ShapeDtypeStruct((depth, B, H_HGRN, DK_HGRN, DV_HGRN), F32),
                   jax.ShapeDtypeStruct((depth, B, w_buf, D_KV), F32),
                   jax.ShapeDtypeStruct((depth, B, w_buf, D_KV), F32)),
        grid=(B // block,),
        in_specs=[pl.BlockSpec(memory_space=pltpu.SMEM),
                  pl.BlockSpec((block * seq, IN_WIDTH), lambda i: (i, 0))] + cache_specs + [
                  pl.BlockSpec((CONV_WIDTH, D_CONV), const2),
                  pl.BlockSpec((1, D_CONV), const2),
                  pl.BlockSpec((1, D_CONV), const2),
                  pl.BlockSpec((1, D_CONV), const2),
                  pl.BlockSpec((depth, D_HGRN), const2),
                  pl.BlockSpec((1, DV_HGRN), const2),
                  pl.BlockSpec((KV_HEADS, GROUP * seq, 2 * ATTN_BLOCK), lambda i: (0, 0, 0))] + _carry_specs(carried),
        out_specs=tuple([pl.BlockSpec((block * seq, D_MODEL), lambda i: (i, 0))] + cache_specs),
        input_output_aliases={len(inputs) + i: 1 + i for i in range(len(carried))},
        scratch_shapes=[pltpu.VMEM((block, hist + seq + 6, D_CONV), F32),
                        pltpu.VMEM((block, 2 * ATTN_BLOCK, D_KV), F32),
                        pltpu.VMEM((block, 2 * ATTN_BLOCK, D_KV), F32)],
        compiler_params=pltpu.CompilerParams(dimension_semantics=("arbitrary",), vmem_limit_bytes=VMEM_LIMIT),
        name="sample_mixers",
    )(*inputs, *carried)


def kernel(x_prompt, x_sample, cache_conv, state_hgrn, cache_swa_k, cache_swa_v, c_prompt, c_sample, rel_bias, w_ada, b_ada, norm_mix_g, w_in, conv_w, conv_b, conv_ln_g, conv_ln_b, hgrn_lb, hgrn_norm_g, attn_sinks, w_out, norm_mlp_g, w_up, w_down, final_g):
    Bp, Tp = x_prompt.shape[:2]
    Bs, Ts = x_sample.shape[:2]
    depth = w_in.shape[0]
    w_buf = cache_swa_k.shape[2]
    assert Tp % MIX_TILE == 0 and (Bp * Tp) % TOK_TILE == 0 and Tp % TOK_TILE == 0 and Bs % SAMPLE_BLOCK == 0
    assert w_buf == WINDOW and GROUP * Ts == SUBLANES

    bias_p, bias_s = _bias_tables(rel_bias, Ts, w_buf)
    mod = _modulation(jnp.concatenate([c_prompt, c_sample], axis=0), w_ada, b_ada)
    w_in_b, w_out_b, w_up_b, w_down_b = (w.astype(BF16) for w in (w_in, w_out, w_up, w_down))
    hlb = hgrn_lb.astype(F32)
    ck = cache_swa_k.reshape(depth, Bs, w_buf, D_KV)
    cv = cache_swa_v.reshape(depth, Bs, w_buf, D_KV)

    xp = x_prompt.reshape(Bp * Tp, D_MODEL)
    xs = x_sample.reshape(Bs * Ts, D_MODEL)
    tile_s = Bs * Ts
    caches_p = ()
    caches_s = ()
    for l in range(depth):
        final = l == depth - 1
        mod_p = mod[l, :Bp].reshape(Bp, 1, N_MOD * D_MODEL)
        mod_s = mod[l, Bp:]
        mix_p, *caches_p = _prompt_mixers(xp.reshape(Bp, Tp, D_MODEL), mod_p, norm_mix_g[l], w_in_b, attn_sinks[l],
                                          conv_w[l], conv_b[l], conv_ln_g[l], conv_ln_b[l], hlb, hgrn_norm_g[l],
                                          bias_p, l, caches_p)
        xp = _out_mlp(mix_p.reshape(Bp * Tp, D_MODEL), xp, mod_p, norm_mlp_g[l], w_out_b, w_up_b, w_down_b,
                      final_g, l, TOK_TILE, Tp // TOK_TILE, final)
        proj_s = _inproj(xs, mod_s, norm_mix_g[l], w_in_b, l, tile_s, 1)
        mix_s, *caches_s = _sample_mixers(proj_s, attn_sinks[l], cache_conv, state_hgrn, ck, cv, conv_w[l],
                                          conv_b[l], conv_ln_g[l], conv_ln_b[l], hlb, hgrn_norm_g[l], bias_s, l,
                                          caches_s)
        xs = _out_mlp(mix_s, xs, mod_s, norm_mlp_g[l], w_out_b, w_up_b, w_down_b, final_g, l, tile_s, 1, final)
    cp, sp, kp, vp = caches_p
    cs, ss, ksn, vsn = caches_s
    return (xp.reshape(Bp, Tp, D_MODEL), xs.reshape(Bs, Ts, D_MODEL), cp, cs, sp, ss,
            kp.reshape(depth, Bp, WINDOW, KV_HEADS, HEAD_DIM), ksn.reshape(depth, Bs, w_buf, KV_HEADS, HEAD_DIM),
            vp.reshape(depth, Bp, WINDOW, KV_HEADS, HEAD_DIM), vsn.reshape(depth, Bs, w_buf, KV_HEADS, HEAD_DIM))
```

```python
import functools
import math

import jax
import jax.numpy as jnp
from jax import lax
from jax.experimental import pallas as pl
from jax.experimental.pallas import tpu as pltpu

F32 = jnp.float32
BF16 = jnp.bfloat16

D_MODEL = 1024
D_CONV = 256
CONV_WIDTH = 31
H_HGRN = 4
DK_HGRN = 128
DV_HGRN = 128
D_HGRN = 512
HEAD_DIM = 64
H_ATTN = 4
KV_HEADS = 2
GROUP = H_ATTN // KV_HEADS
D_ATTN = H_ATTN * HEAD_DIM
D_KV = KV_HEADS * HEAD_DIM
WINDOW = 128
ATTN_BLOCK = 128
NUM_BUCKETS = 32
MAX_DISTANCE = 128
D_FF = 4 * D_MODEL
N_MOD = 6
EPS = 1e-6

OFF_AVAL = 0
OFF_AGATE = OFF_AVAL + D_CONV
OFF_Q = OFF_AGATE + D_CONV
OFF_F = OFF_Q + H_HGRN * DK_HGRN
OFF_I = OFF_F + H_HGRN * DK_HGRN
OFF_G = OFF_I + D_HGRN
OFF_QA = OFF_G + D_HGRN
OFF_KA = OFF_QA + D_ATTN
OFF_VA = OFF_KA + D_KV
IN_WIDTH = OFF_VA + D_KV

HGRN_CHUNK = 64
HGRN_KEYBLOCK = 32
HGRN_SPAN = 4
SUBLANES = 8
CONV_PAD = 32
MIX_TILE = 512
TOK_TILE = 512
SAMPLE_BLOCK = 8
FF_CHUNK = 1024
VMEM_LIMIT = 56 * 1024 * 1024

NT_DIMS = (((1,), (1,)), ((), ()))
TN_DIMS = (((0,), (0,)), ((), ()))


def _silu(x):
    return x * jax.nn.sigmoid(x)


def _rms_rows(x):
    return x * lax.rsqrt(jnp.mean(x * x, axis=-1, keepdims=True) + EPS)


def _layer_lb(hlb, layer):
    m = jnp.max(hlb, axis=0, keepdims=True)
    e = jnp.exp(hlb - m)
    p = e / jnp.sum(e, axis=0, keepdims=True)
    lb = jnp.zeros_like(m)
    for i in range(1, layer + 1):
        lb = lb + p[i:i + 1, :]
    return lb


def _split3_bf16(x):
    hi = x.astype(BF16)
    r = x - hi.astype(F32)
    mid = r.astype(BF16)
    return hi, mid, (r - mid.astype(F32)).astype(BF16)


def _select_rows_mxu(sel, x):
    return sum(jnp.dot(sel, part, preferred_element_type=F32) for part in _split3_bf16(x))


def _cumsum_rows_small(g):
    row = lax.broadcasted_iota(jnp.int32, g.shape, 0)
    b = jnp.zeros_like(g)
    for u in range(g.shape[0]):
        b = b + jnp.where(row >= u, g[u:u + 1, :], 0.0)
    return b


def _hgrn_span(proj_ref, mix_ref, st_ref, row0, lb, hng, tri, tick):
    L, KB = HGRN_CHUNK, HGRN_KEYBLOCK
    span = HGRN_SPAN * L
    g, k = _hgrn_gates(proj_ref[pl.ds(row0, span), OFF_F:OFF_F + D_HGRN], lb)
    b = _select_rows_mxu(tri, g)
    units = [(c, h) for c in range(HGRN_SPAN) for h in range(H_HGRN)]

    ops = {}
    for c, h in units:
        rows = pl.ds(row0 + c * L, L)
        cs = slice(h * DK_HGRN, (h + 1) * DK_HGRN)
        q = proj_ref[rows, OFF_Q + h * DK_HGRN:OFF_Q + (h + 1) * DK_HGRN]
        v = proj_ref[rows, OFF_I + h * DV_HGRN:OFF_I + (h + 1) * DV_HGRN].astype(BF16)
        kk = k[c * L:(c + 1) * L, cs]
        bb = b[c * L:(c + 1) * L, cs]
        qp, kp = [], []
        for lo in range(0, L, KB):
            r = bb[lo + KB // 2 - 1:lo + KB // 2, :]
            kp.append((kk[lo:lo + KB] * jnp.exp(r - bb[lo:lo + KB])).astype(BF16))
            qp.append((q[lo:] * jnp.exp(bb[lo:] - r)).astype(BF16))
        bl = bb[L - 1:L, :]
        ops[c, h] = dict(qp=qp, kp=kp, v=v, qt=(q * jnp.exp(bb)).astype(BF16),
                         kst=(kk * jnp.exp(bl - bb)).astype(BF16), e=jnp.exp(bl))
    tick()

    for u in units:
        o = ops[u]
        o["p"] = [lax.dot_general(qp, kp, NT_DIMS, preferred_element_type=F32) for qp, kp in zip(o["qp"], o["kp"])]
        o["m"] = lax.dot_general(o["v"], o["kst"], TN_DIMS, preferred_element_type=F32)
    tick()

    for u in units:
        pm = []
        for p in ops[u]["p"]:
            row = lax.broadcasted_iota(jnp.int32, p.shape, 0)
            col = lax.broadcasted_iota(jnp.int32, p.shape, 1)
            pm.append(jnp.where(row >= col, p, 0.0).astype(BF16))
        ops[u]["p"] = pm
    tick()

    for u in units:
        o = ops[u]
        blocks = [None] * (L // KB)
        for j, p in enumerate(o["p"]):
            cj = jnp.dot(p, o["v"][j * KB:(j + 1) * KB], preferred_element_type=F32)
            for i in range(j, L // KB):
                piece = cj[(i - j) * KB:(i - j + 1) * KB]
                blocks[i] = piece if blocks[i] is None else blocks[i] + piece
        o["o"] = jnp.concatenate(blocks, axis=0)
    tick()

    for h in range(H_HGRN):
        st = st_ref[h]
        for c in range(HGRN_SPAN):
            o = ops[c, h]
            out = o["o"] + lax.dot_general(o["qt"], st.astype(BF16), NT_DIMS, preferred_element_type=F32)
            st = o["e"] * st + o["m"]
            rows = pl.ds(row0 + c * L, L)
            gate = proj_ref[rows, OFF_G + h * DV_HGRN:OFF_G + (h + 1) * DV_HGRN]
            mix_ref[rows, D_CONV + h * DV_HGRN:D_CONV + (h + 1) * DV_HGRN] = _hgrn_out(out, hng, gate).astype(BF16)
        st_ref[h] = st


def _sink_softmax(s, sink):
    m = jnp.maximum(jnp.max(s, axis=-1, keepdims=True), sink)
    p = jnp.exp(s - m)
    return p, jnp.sum(p, axis=-1, keepdims=True) + jnp.exp(sink - m)


def _bias_kernel(tab_ref, bp_ref, bs_ref, op_ref, os_ref, *, seq):
    bk = bp_ref[...]
    for h in range(H_ATTN):
        acc = jnp.full(bk.shape, -jnp.inf, F32)
        for bkt in range(NUM_BUCKETS):
            acc = jnp.where(bk == bkt, tab_ref[bkt, h], acc)
        op_ref[h] = acc
    bk = bs_ref[...]
    row = lax.broadcasted_iota(jnp.int32, bk.shape, 0)
    for kv in range(KV_HEADS):
        acc = jnp.full(bk.shape, -jnp.inf, F32)
        for bkt in range(NUM_BUCKETS):
            val = jnp.full(bk.shape, tab_ref[bkt, kv * GROUP], F32)
            for gi in range(1, GROUP):
                val = jnp.where(row >= gi * seq, tab_ref[bkt, kv * GROUP + gi], val)
            acc = jnp.where(bk == bkt, val, acc)
        os_ref[kv] = acc


def _t5_bucket(rel):
    n = jnp.maximum(rel, 0)
    max_exact = NUM_BUCKETS // 2
    nf = jnp.maximum(n, max_exact).astype(F32)
    large = max_exact + (jnp.log(nf / max_exact) / math.log(MAX_DISTANCE / max_exact)
                         * (NUM_BUCKETS - max_exact)).astype(jnp.int32)
    large = jnp.minimum(large, NUM_BUCKETS - 1)
    return jnp.where(n < max_exact, n, large)


def _bias_tables(rel_bias, dec_seq, w_buf):
    qi = jnp.arange(ATTN_BLOCK, dtype=jnp.int32)[:, None]
    kc = jnp.arange(2 * ATTN_BLOCK, dtype=jnp.int32)[None, :]
    rel_p = qi + ATTN_BLOCK - kc
    bucket_p = jnp.where((rel_p >= 0) & (rel_p <= WINDOW), _t5_bucket(rel_p), -1)
    ts = (jnp.arange(GROUP * dec_seq, dtype=jnp.int32) % dec_seq)[:, None]
    js = jnp.arange(2 * ATTN_BLOCK, dtype=jnp.int32)[None, :]
    rel_s = w_buf + ts - js
    ok_s = (rel_s >= 0) & (rel_s <= WINDOW) & (js < w_buf + dec_seq)
    bucket_s = jnp.where(ok_s, _t5_bucket(rel_s), -1)
    return pl.pallas_call(
        functools.partial(_bias_kernel, seq=dec_seq),
        out_shape=(jax.ShapeDtypeStruct((H_ATTN, ATTN_BLOCK, 2 * ATTN_BLOCK), F32),
                   jax.ShapeDtypeStruct((KV_HEADS, GROUP * dec_seq, 2 * ATTN_BLOCK), F32)),
        in_specs=[pl.BlockSpec(memory_space=pltpu.SMEM),
                  pl.BlockSpec(memory_space=pltpu.VMEM),
                  pl.BlockSpec(memory_space=pltpu.VMEM)],
        out_specs=(pl.BlockSpec(memory_space=pltpu.VMEM), pl.BlockSpec(memory_space=pltpu.VMEM)),
        name="rel_bias_tables",
    )(rel_bias.astype(F32), bucket_p, bucket_s)


def _mod_kernel(c_ref, w_ref, b_ref, o_ref):
    s = _silu(c_ref[...]).astype(BF16)
    o_ref[...] = jnp.dot(s, w_ref[...].astype(BF16), preferred_element_type=F32) + b_ref[...]


def _modulation(c_all, w_ada, b_ada):
    depth = w_ada.shape[0]
    n = c_all.shape[0]
    return pl.pallas_call(
        _mod_kernel,
        out_shape=jax.ShapeDtypeStruct((depth, n, N_MOD * D_MODEL), F32),
        grid=(depth, N_MOD),
        in_specs=[pl.BlockSpec((n, D_MODEL), lambda l, j: (0, 0)),
                  pl.BlockSpec((None, D_MODEL, D_MODEL), lambda l, j: (l, 0, j)),
                  pl.BlockSpec((None, 1, D_MODEL), lambda l, j: (l, 0, j))],
        out_specs=pl.BlockSpec((None, n, D_MODEL), lambda l, j: (l, 0, j)),
        compiler_params=pltpu.CompilerParams(dimension_semantics=("arbitrary", "arbitrary"),
                                             vmem_limit_bytes=VMEM_LIMIT),
        name="adaln_modulation",
    )(c_all, w_ada, b_ada.reshape(depth, 1, N_MOD * D_MODEL))


def _mod_spec(mod, chunk, tiles_per_batch):
    if mod.ndim == 3:
        return pl.BlockSpec((None, 1, D_MODEL), lambda i: (i // tiles_per_batch, 0, chunk))
    return pl.BlockSpec((mod.shape[0], D_MODEL), lambda i: (0, chunk))


def _mod_rows(m, n_tokens):
    if m.shape[0] == 1:
        return m
    reps = n_tokens // m.shape[0]
    tok = lax.broadcasted_iota(jnp.int32, (n_tokens, m.shape[0]), 0)
    bat = lax.broadcasted_iota(jnp.int32, (n_tokens, m.shape[0]), 1)
    sel = jnp.where((tok >= bat * reps) & (tok < (bat + 1) * reps), 1.0, 0.0).astype(BF16)
    return _select_rows_mxu(sel, m)


def _inproj_kernel(x_ref, sh_ref, sc_ref, g_ref, w_ref, o_ref):
    n = x_ref.shape[0]
    h = _rms_rows(x_ref[...]) * g_ref[...] * (1.0 + _mod_rows(sc_ref[...], n)) + _mod_rows(sh_ref[...], n)
    o_ref[...] = jnp.dot(h.astype(BF16), w_ref[...], preferred_element_type=F32)


def _inproj(x2, mod, norm_g, w_in, layer, tile, tiles_per_batch):
    n = x2.shape[0]
    return pl.pallas_call(
        _inproj_kernel,
        out_shape=jax.ShapeDtypeStruct((n, IN_WIDTH), F32),
        grid=(n // tile,),
        in_specs=[pl.BlockSpec((tile, D_MODEL), lambda i: (i, 0)),
                  _mod_spec(mod, 0, tiles_per_batch),
                  _mod_spec(mod, 1, tiles_per_batch),
                  pl.BlockSpec((1, D_MODEL), lambda i: (0, 0)),
                  pl.BlockSpec((None, D_MODEL, IN_WIDTH), lambda i: (layer, 0, 0), pipeline_mode=pl.Buffered(1))],
        out_specs=pl.BlockSpec((tile, IN_WIDTH), lambda i: (i, 0)),
        compiler_params=pltpu.CompilerParams(dimension_semantics=("arbitrary",), vmem_limit_bytes=VMEM_LIMIT),
        name="in_projection",
    )(x2, mod, mod, norm_g.reshape(1, D_MODEL), w_in)


def _mlp_kernel(mix_ref, x_ref, g1_ref, sh_ref, sc_ref, g2_ref, ng_ref, wout_ref, wup_ref, wdn_ref, fg_ref, o_ref, *,
                final):
    n = x_ref.shape[0]
    x1 = x_ref[...] + _mod_rows(g1_ref[...], n) * jnp.dot(mix_ref[...].astype(BF16), wout_ref[...],
                                                          preferred_element_type=F32)
    h = (_rms_rows(x1) * ng_ref[...] * (1.0 + _mod_rows(sc_ref[...], n)) + _mod_rows(sh_ref[...], n)).astype(BF16)
    acc = None
    for c in range(D_FF // FF_CHUNK):
        u = jnp.dot(h, wup_ref[:, c * FF_CHUNK:(c + 1) * FF_CHUNK], preferred_element_type=F32)
        u = jnp.square(jnp.maximum(u, 0.0)).astype(BF16)
        d = jnp.dot(u, wdn_ref[c * FF_CHUNK:(c + 1) * FF_CHUNK, :], preferred_element_type=F32)
        acc = d if acc is None else acc + d
    x2 = x1 + _mod_rows(g2_ref[...], n) * acc
    if final:
        x2 = _rms_rows(x2) * fg_ref[...]
    o_ref[...] = x2


def _out_mlp(mix2, x2, mod, norm_g, w_out, w_up, w_down, final_g, layer, tile, tiles_per_batch, final):
    n = x2.shape[0]
    const = lambda i: (0, 0)
    of_layer = lambda i: (layer, 0, 0)
    return pl.pallas_call(
        functools.partial(_mlp_kernel, final=final),
        out_shape=jax.ShapeDtypeStruct((n, D_MODEL), F32),
        grid=(n // tile,),
        in_specs=[pl.BlockSpec((tile, D_MODEL), lambda i: (i, 0)),
                  pl.BlockSpec((tile, D_MODEL), lambda i: (i, 0)),
                  _mod_spec(mod, 2, tiles_per_batch),
                  _mod_spec(mod, 3, tiles_per_batch),
                  _mod_spec(mod, 4, tiles_per_batch),
                  _mod_spec(mod, 5, tiles_per_batch),
                  pl.BlockSpec((1, D_MODEL), const),
                  pl.BlockSpec((None, D_MODEL, D_MODEL), of_layer, pipeline_mode=pl.Buffered(1)),
                  pl.BlockSpec((None, D_MODEL, D_FF), of_layer, pipeline_mode=pl.Buffered(1)),
                  pl.BlockSpec((None, D_FF, D_MODEL), of_layer, pipeline_mode=pl.Buffered(1)),
                  pl.BlockSpec((1, D_MODEL), const)],
        out_specs=pl.BlockSpec((tile, D_MODEL), lambda i: (i, 0)),
        compiler_params=pltpu.CompilerParams(dimension_semantics=("arbitrary",), vmem_limit_bytes=VMEM_LIMIT),
        name="out_projection_mlp",
    )(mix2, x2, mod, mod, mod, mod, norm_g.reshape(1, D_MODEL), w_out, w_up, w_down, final_g.reshape(1, D_MODEL))


def _conv_ln_swish(acc, lng, lnb):
    mu = jnp.mean(acc, axis=-1, keepdims=True)
    xc = acc - mu
    y = xc * lax.rsqrt(jnp.mean(xc * xc, axis=-1, keepdims=True) + EPS) * lng + lnb
    return _silu(y)


def _hgrn_gates(fh, lb):
    f = lb + (1.0 - lb) * jax.nn.sigmoid(fh)
    return jnp.log(f), 1.0 - f


def _hgrn_out(o, hng, gate):
    return _rms_rows(o) * hng * _silu(gate)


def _ticker(pieces):
    it = iter(pieces)

    def tick():
        piece = next(it, None)
        if piece is not None:
            piece()

    def flush():
        for piece in it:
            piece()

    tick.flush = flush
    return tick


def _prompt_mix_kernel(sinks_ref, x_ref, sh_ref, sc_ref, ng_ref, win_ref, convw_ref, convb_ref, lng_ref, lnb_ref,
                       hlb_ref, hng_ref, bias_ref,
                       mix_ref, convo_ref, so_ref, ko_ref, vo_ref,
                       proj_ref, abuf, kbuf, vbuf, st_ref, *, layer, tile):
    t = pl.program_id(1)
    last = pl.num_programs(1) - 1

    @pl.when(t == 0)
    def _():
        abuf[0:CONV_PAD, :] = jnp.zeros((CONV_PAD, D_CONV), F32)
        abuf[CONV_PAD + tile:CONV_PAD + tile + SUBLANES, :] = jnp.zeros((SUBLANES, D_CONV), F32)
        kbuf[0:ATTN_BLOCK, :] = jnp.zeros((ATTN_BLOCK, D_KV), BF16)
        vbuf[0:ATTN_BLOCK, :] = jnp.zeros((ATTN_BLOCK, D_KV), BF16)
        st_ref[...] = jnp.zeros(st_ref.shape, F32)

    h_in = (_rms_rows(x_ref[...]) * ng_ref[...] * (1.0 + sc_ref[...]) + sh_ref[...]).astype(BF16)
    for lo, hi in ((OFF_AVAL, OFF_Q), (OFF_F, OFF_I), (OFF_Q, OFF_F), (OFF_I, OFF_G), (OFF_G, OFF_QA),
                   (OFF_QA, IN_WIDTH)):
        proj_ref[:, lo:hi] = jnp.dot(h_in, win_ref[:, lo:hi], preferred_element_type=F32)

    kbuf[ATTN_BLOCK:ATTN_BLOCK + tile, :] = proj_ref[:, OFF_KA:OFF_KA + D_KV].astype(BF16)
    vbuf[ATTN_BLOCK:ATTN_BLOCK + tile, :] = proj_ref[:, OFF_VA:OFF_VA + D_KV].astype(BF16)
    scale = HEAD_DIM ** -0.5
    attn = {}

    def attn_scores(blk):
        def run():
            r0 = blk * ATTN_BLOCK
            for h in range(H_ATTN):
                kv = h // GROUP
                q = (proj_ref[r0:r0 + ATTN_BLOCK, OFF_QA + h * HEAD_DIM:OFF_QA + (h + 1) * HEAD_DIM]
                     * scale).astype(BF16)
                kall = kbuf[r0:r0 + 2 * ATTN_BLOCK, kv * HEAD_DIM:(kv + 1) * HEAD_DIM]
                attn[blk, h] = lax.dot_general(q, kall, NT_DIMS, preferred_element_type=F32)
        return run

    def attn_softmax(blk):
        def run():
            for h in range(H_ATTN):
                s = attn[blk, h] + bias_ref[h]
                if blk == 0:
                    col = lax.broadcasted_iota(jnp.int32, s.shape, 1)
                    s = jnp.where(col + (t * tile - ATTN_BLOCK) >= 0, s, -jnp.inf)
                p, den = _sink_softmax(s, sinks_ref[h])
                attn[blk, h] = (p.astype(BF16), den)
        return run

    def attn_values(blk):
        def run():
            r0 = blk * ATTN_BLOCK
            heads = []
            for h in range(H_ATTN):
                kv = h // GROUP
                p, den = attn[blk, h]
                vall = vbuf[r0:r0 + 2 * ATTN_BLOCK, kv * HEAD_DIM:(kv + 1) * HEAD_DIM]
                heads.append(jnp.dot(p, vall, preferred_element_type=F32) / den)
            mix_ref[r0:r0 + ATTN_BLOCK, D_CONV + D_HGRN:D_MODEL] = jnp.concatenate(heads, axis=1).astype(BF16)
        return run

    tick = _ticker([stage(blk) for blk in range(tile // ATTN_BLOCK)
                    for stage in (attn_scores, attn_softmax, attn_values)])

    abuf[CONV_PAD:CONV_PAD + tile, :] = (proj_ref[:, OFF_AVAL:OFF_AVAL + D_CONV]
                                         * jax.nn.sigmoid(proj_ref[:, OFF_AGATE:OFF_AGATE + D_CONV]))
    first_row = CONV_PAD - (CONV_WIDTH - 1)
    acc = jnp.broadcast_to(convb_ref[...], (tile, D_CONV))
    for r in range(SUBLANES):
        z = None
        for off in range(r, first_row + CONV_WIDTH, SUBLANES):
            j = off - first_row
            if j < 0:
                continue
            term = convw_ref[j:j + 1, :] * abuf[off - r:off - r + tile + SUBLANES, :]
            z = term if z is None else z + term
        acc = acc + (z[0:tile] if r == 0 else pltpu.roll(z, tile + SUBLANES - r, 0)[0:tile])
    mix_ref[:, 0:D_CONV] = _conv_ln_swish(acc, lng_ref[...], lnb_ref[...]).astype(BF16)
    tick()

    lb = _layer_lb(hlb_ref[...], layer)
    hng = hng_ref[...]
    span = HGRN_SPAN * HGRN_CHUNK
    ri = lax.broadcasted_iota(jnp.int32, (span, span), 0)
    ci = lax.broadcasted_iota(jnp.int32, (span, span), 1)
    tri = jnp.where((ri >= ci) & (ri // HGRN_CHUNK == ci // HGRN_CHUNK), 1.0, 0.0).astype(BF16)
    for i in range(tile // span):
        _hgrn_span(proj_ref, mix_ref, st_ref, i * span, lb, hng, tri, tick)
    tick.flush()

    @pl.when(t == last)
    def _():
        convo_ref[...] = abuf[CONV_PAD + tile - (CONV_WIDTH - 1):CONV_PAD + tile, :]
        for h in range(H_HGRN):
            so_ref[h] = st_ref[h].T
        ko_ref[...] = proj_ref[tile - WINDOW:tile, OFF_KA:OFF_KA + D_KV]
        vo_ref[...] = proj_ref[tile - WINDOW:tile, OFF_VA:OFF_VA + D_KV]

    abuf[0:CONV_PAD, :] = abuf[tile:tile + CONV_PAD, :]
    kbuf[0:ATTN_BLOCK, :] = kbuf[tile:tile + ATTN_BLOCK, :]
    vbuf[0:ATTN_BLOCK, :] = vbuf[tile:tile + ATTN_BLOCK, :]


def _carry_specs(carried):
    return [pl.BlockSpec(memory_space=pl.ANY)] * len(carried)


def _carried(kernel_fn, n_in, n_carried):
    if n_carried == 0:
        return kernel_fn
    return lambda *refs: kernel_fn(*refs[:n_in], *refs[n_in + n_carried:])


def _prompt_mixers(x, mod, norm_g, w_in, sinks, conv_w, conv_b, ln_g, ln_b, hgrn_lb, hng, bias_p, layer, carried):
    B, T = x.shape[:2]
    tile = MIX_TILE
    depth = hgrn_lb.shape[0]
    const2 = lambda b, t: (0, 0)
    inputs = (sinks, x, mod, mod, norm_g.reshape(1, D_MODEL), w_in, conv_w, conv_b.reshape(1, D_CONV),
              ln_g.reshape(1, D_CONV), ln_b.reshape(1, D_CONV), hgrn_lb, hng.reshape(1, DV_HGRN), bias_p)
    return pl.pallas_call(
        _carried(functools.partial(_prompt_mix_kernel, layer=layer, tile=tile), len(inputs), len(carried)),
        out_shape=(jax.ShapeDtypeStruct((B, T, D_MODEL), BF16),
                   jax.ShapeDtypeStruct((depth, B, CONV_WIDTH - 1, D_CONV), F32),
                   jax.ShapeDtypeStruct((depth, B, H_HGRN, DK_HGRN, DV_HGRN), F32),
                   jax.ShapeDtypeStruct((depth, B, WINDOW, D_KV), F32),
                   jax.ShapeDtypeStruct((depth, B, WINDOW, D_KV), F32)),
        grid=(B, T // tile),
        in_specs=[pl.BlockSpec(memory_space=pltpu.SMEM),
                  pl.BlockSpec((None, tile, D_MODEL), lambda b, t: (b, t, 0)),
                  pl.BlockSpec((None, 1, D_MODEL), lambda b, t: (b, 0, 0)),
                  pl.BlockSpec((None, 1, D_MODEL), lambda b, t: (b, 0, 1)),
                  pl.BlockSpec((1, D_MODEL), const2),
                  pl.BlockSpec((None, D_MODEL, IN_WIDTH), lambda b, t: (layer, 0, 0), pipeline_mode=pl.Buffered(1)),
                  pl.BlockSpec((CONV_WIDTH, D_CONV), const2),
                  pl.BlockSpec((1, D_CONV), const2),
                  pl.BlockSpec((1, D_CONV), const2),
                  pl.BlockSpec((1, D_CONV), const2),
                  pl.BlockSpec((depth, D_HGRN), const2),
                  pl.BlockSpec((1, DV_HGRN), const2),
                  pl.BlockSpec((H_ATTN, ATTN_BLOCK, 2 * ATTN_BLOCK), lambda b, t: (0, 0, 0))] + _carry_specs(carried),
        out_specs=(pl.BlockSpec((None, tile, D_MODEL), lambda b, t: (b, t, 0)),
                   pl.BlockSpec((None, None, CONV_WIDTH - 1, D_CONV), lambda b, t: (layer, b, 0, 0)),
                   pl.BlockSpec((None, None, H_HGRN, DK_HGRN, DV_HGRN), lambda b, t: (layer, b, 0, 0, 0)),
                   pl.BlockSpec((None, None, WINDOW, D_KV), lambda b, t: (layer, b, 0, 0)),
                   pl.BlockSpec((None, None, WINDOW, D_KV), lambda b, t: (layer, b, 0, 0))),
        input_output_aliases={len(inputs) + i: 1 + i for i in range(len(carried))},
        scratch_shapes=[pltpu.VMEM((tile, IN_WIDTH), F32),
                        pltpu.VMEM((CONV_PAD + tile + SUBLANES, D_CONV), F32),
                        pltpu.VMEM((ATTN_BLOCK + tile, D_KV), BF16),
                        pltpu.VMEM((ATTN_BLOCK + tile, D_KV), BF16),
                        pltpu.VMEM((H_HGRN, DV_HGRN, DK_HGRN), F32)],
        compiler_params=pltpu.CompilerParams(dimension_semantics=("arbitrary", "arbitrary"),
                                             vmem_limit_bytes=VMEM_LIMIT),
        name="prompt_mixers",
    )(*inputs, *carried)


def _sample_mix_kernel(sinks_ref, proj_ref, cconv_ref, state_ref, ck_ref, cv_ref, convw_ref, convb_ref, lng_ref,
                       lnb_ref, hlb_ref, hng_ref, bias_ref,
                       mix_ref, convo_ref, so_ref, ko_ref, vo_ref, full_ref, kall_ref, vall_ref, *,
                       layer, block, seq, w_buf):
    hist = CONV_WIDTH - 1
    n_keys = w_buf + seq
    pad_keys = kall_ref.shape[1]

    @pl.when(pl.program_id(0) == 0)
    def _():
        for ref in (kall_ref, vall_ref):
            ref[:, w_buf:pad_keys, :] = jnp.zeros((block, pad_keys - w_buf, D_KV), F32)

    lb = _layer_lb(hlb_ref[...], layer)
    hng = hng_ref[...]
    scale = HEAD_DIM ** -0.5
    elems = range(block)
    row8 = lax.broadcasted_iota(jnp.int32, (SUBLANES, DV_HGRN), 0)
    ones_rows = jnp.where((row8 >= seq) & (row8 < seq + 3), 1.0, 0.0)
    zrow = jnp.zeros((1, DK_HGRN), BF16)
    prow = lax.broadcasted_iota(jnp.int32, (seq, seq), 0)
    pcol = lax.broadcasted_iota(jnp.int32, (seq, seq), 1)
    grow = lax.broadcasted_iota(jnp.int32, (GROUP * seq, 1), 0)

    proj = [proj_ref[e * seq:(e + 1) * seq, :] for e in elems]

    out_a = []
    for e in elems:
        p = proj[e]
        full_ref[e, 0:hist, :] = cconv_ref[e]
        full_ref[e, hist:hist + seq, :] = (p[:, OFF_AVAL:OFF_AVAL + D_CONV]
                                           * jax.nn.sigmoid(p[:, OFF_AGATE:OFF_AGATE + D_CONV]))
        rows = [jnp.sum(convw_ref[...] * full_ref[e, s:s + CONV_WIDTH, :], axis=0, keepdims=True) for s in range(seq)]
        convo_ref[e] = full_ref[e, seq:seq + hist, :]
        out_a.append(_conv_ln_swish(jnp.concatenate(rows, axis=0) + convb_ref[...], lng_ref[...], lnb_ref[...]))

    units = [(e, h) for e in elems for h in range(H_HGRN)]
    ops = {}
    for e in elems:
        p = proj[e]
        g, k = _hgrn_gates(p[:, OFF_F:OFF_F + D_HGRN], lb)
        b = _cumsum_rows_small(g)
        for h in range(H_HGRN):
            cs = slice(h * DK_HGRN, (h + 1) * DK_HGRN)
            q = p[:, OFF_Q + h * DK_HGRN:OFF_Q + (h + 1) * DK_HGRN]
            v = p[:, OFF_I + h * DV_HGRN:OFF_I + (h + 1) * DV_HGRN]
            bb = b[:, cs]
            bl = bb[seq - 1:seq, :]
            kst = (k[:, cs] * jnp.exp(bl - bb)).astype(BF16)
            x = jnp.concatenate([kst.astype(F32), *(part.astype(F32) for part in _split3_bf16(jnp.exp(bl))),
                                 zrow.astype(F32)], axis=0).astype(BF16)
            vpad = jnp.concatenate([v, jnp.zeros((SUBLANES - seq, DV_HGRN), F32)], axis=0)
            ops[e, h] = dict(qp=(q * jnp.exp(bb - bl)).astype(BF16), kst=kst, v=v.astype(BF16),
                             qt=(q * jnp.exp(bb)).astype(BF16), x=x,
                             r=jnp.concatenate([vpad, ones_rows], axis=1).astype(BF16))
    for u in units:
        o = ops[u]
        st = state_ref[u[0], u[1]]
        o["p"] = lax.dot_general(o["qp"], o["kst"], NT_DIMS, preferred_element_type=F32)
        o["inter"] = jnp.dot(o["qt"], st.astype(BF16), preferred_element_type=F32)
        me = lax.dot_general(o["x"], o["r"], TN_DIMS, preferred_element_type=F32)
        so_ref[u[0], u[1]] = me[:, DV_HGRN:] * st + me[:, :DV_HGRN]
    out_b = {}
    for u in units:
        o = ops[u]
        pm = jnp.where(prow >= pcol, o["p"], 0.0).astype(BF16)
        out = o["inter"] + jnp.dot(pm, o["v"], preferred_element_type=F32)
        gate = proj[u[0]][:, OFF_G + u[1] * DV_HGRN:OFF_G + (u[1] + 1) * DV_HGRN]
        out_b[u] = _hgrn_out(out, hng, gate)

    for e in elems:
        p = proj[e]
        for ref, cache, new, out in ((kall_ref, ck_ref, p[:, OFF_KA:OFF_KA + D_KV], ko_ref),
                                     (vall_ref, cv_ref, p[:, OFF_VA:OFF_VA + D_KV], vo_ref)):
            ref[e, 0:w_buf, :] = cache[e]
            ref[e, w_buf:n_keys, :] = new
            out[e] = ref[e, seq:n_keys, :]
    scores = {}
    for e in elems:
        p = proj[e]
        for kv in range(KV_HEADS):
            hs = slice(kv * HEAD_DIM, (kv + 1) * HEAD_DIM)
            q2 = jnp.concatenate([p[:, OFF_QA + h * HEAD_DIM:OFF_QA + (h + 1) * HEAD_DIM]
                                  for h in range(kv * GROUP, (kv + 1) * GROUP)], axis=0)
            kall = kall_ref[e][:, hs].astype(BF16)
            scores[e, kv] = (lax.dot_general((q2 * scale).astype(BF16), kall, NT_DIMS, preferred_element_type=F32)
                             + bias_ref[kv])
    out_c = {}
    for e in elems:
        for kv in range(KV_HEADS):
            hs = slice(kv * HEAD_DIM, (kv + 1) * HEAD_DIM)
            sink = jnp.zeros((GROUP * seq, 1), F32)
            for gi in range(GROUP):
                sink = jnp.where(grow >= gi * seq, sinks_ref[kv * GROUP + gi], sink)
            pr, den = _sink_softmax(scores[e, kv], sink)
            o2 = jnp.dot(pr.astype(BF16), vall_ref[e][:, hs].astype(BF16), preferred_element_type=F32) / den
            for gi in range(GROUP):
                out_c[e, kv * GROUP + gi] = o2[gi * seq:(gi + 1) * seq]

    for e in elems:
        parts = [out_a[e]] + [out_b[e, h] for h in range(H_HGRN)] + [out_c[e, h] for h in range(H_ATTN)]
        mix_ref[e * seq:(e + 1) * seq, :] = jnp.concatenate(parts, axis=1)


def _sample_mixers(proj2, sinks, cache_conv, state, cache_k, cache_v, conv_w, conv_b, ln_g, ln_b, hgrn_lb, hng,
                   bias_s, layer, carried):
    B = state.shape[1]
    seq = proj2.shape[0] // B
    w_buf = cache_k.shape[2]
    block = SAMPLE_BLOCK
    depth = hgrn_lb.shape[0]
    hist = CONV_WIDTH - 1
    const2 = lambda i: (0, 0)
    cache_specs = [pl.BlockSpec((None, block, hist, D_CONV), lambda i: (layer, i, 0, 0)),
                   pl.BlockSpec((None, block, H_HGRN, DK_HGRN, DV_HGRN), lambda i: (layer, i, 0, 0, 0)),
                   pl.BlockSpec((None, block, w_buf, D_KV), lambda i: (layer, i, 0, 0)),
                   pl.BlockSpec((None, block, w_buf, D_KV), lambda i: (layer, i, 0, 0))]
    inputs = (sinks, proj2, cache_conv, state, cache_k, cache_v, conv_w, conv_b.reshape(1, D_CONV),
              ln_g.reshape(1, D_CONV), ln_b.reshape(1, D_CONV), hgrn_lb, hng.reshape(1, DV_HGRN), bias_s)
    return pl.pallas_call(
        _carried(functools.partial(_sample_mix_kernel, layer=layer, block=block, seq=seq, w_buf=w_buf),
                 len(inputs), len(carried)),
        out_shape=(jax.ShapeDtypeStruct((B * seq, D_MODEL), F32),
                   jax.ShapeDtypeStruct((depth, B, hist, D_CONV), F32),
                   jax.ShapeDtypeStruct((depth, B, H_HGRN, DK_HGRN, DV_HGRN), F32),
                   jax.ShapeDtypeStruct((depth, B, w_buf, D_KV), F32),
                   jax.ShapeDtypeStruct((depth, B, w_buf, D_KV), F32)),
        grid=(B // block,),
        in_specs=[pl.BlockSpec(memory_space=pltpu.SMEM),
                  pl.BlockSpec((block * seq, IN_WIDTH), lambda i: (i, 0))] + cache_specs + [
                  pl.BlockSpec((CONV_WIDTH, D_CONV), const2),
                  pl.BlockSpec((1, D_CONV), const2),
                  pl.BlockSpec((1, D_CONV), const2),
                  pl.BlockSpec((1, D_CONV), const2),
                  pl.BlockSpec((depth, D_HGRN), const2),
                  pl.BlockSpec((1, DV_HGRN), const2),
                  pl.BlockSpec((KV_HEADS, GROUP * seq, 2 * ATTN_BLOCK), lambda i: (0, 0, 0))] + _carry_specs(carried),
        out_specs=tuple([pl.BlockSpec((block * seq, D_MODEL), lambda i: (i, 0))] + cache_specs),
        input_output_aliases={len(inputs) + i: 1 + i for i in range(len(carried))},
        scratch_shapes=[pltpu.VMEM((block, hist + seq + 6, D_CONV), F32),
                        pltpu.VMEM((block, 2 * ATTN_BLOCK, D_KV), F32),
                        pltpu.VMEM((block, 2 * ATTN_BLOCK, D_KV), F32)],
        compiler_params=pltpu.CompilerParams(dimension_semantics=("arbitrary",), vmem_limit_bytes=VMEM_LIMIT),
        name="sample_mixers",
    )(*inputs, *carried)


def kernel(x_prompt, x_sample, cache_conv, state_hgrn, cache_swa_k, cache_swa_v, c_prompt, c_sample, rel_bias, w_ada, b_ada, norm_mix_g, w_in, conv_w, conv_b, conv_ln_g, conv_ln_b, hgrn_lb, hgrn_norm_g, attn_sinks, w_out, norm_mlp_g, w_up, w_down, final_g):
    Bp, Tp = x_prompt.shape[:2]
    Bs, Ts = x_sample.shape[:2]
    depth = w_in.shape[0]
    w_buf = cache_swa_k.shape[2]
    assert Tp % MIX_TILE == 0 and (Bp * Tp) % TOK_TILE == 0 and Tp % TOK_TILE == 0 and Bs % SAMPLE_BLOCK == 0
    assert w_buf == WINDOW and GROUP * Ts == SUBLANES

    bias_p, bias_s = _bias_tables(rel_bias, Ts, w_buf)
    mod = _modulation(jnp.concatenate([c_prompt, c_sample], axis=0), w_ada, b_ada)
    w_in_b, w_out_b, w_up_b, w_down_b = (w.astype(BF16) for w in (w_in, w_out, w_up, w_down))
    hlb = hgrn_lb.astype(F32)
    ck = cache_swa_k.reshape(depth, Bs, w_buf, D_KV)
    cv = cache_swa_v.reshape(depth, Bs, w_buf, D_KV)

    xp = x_prompt.reshape(Bp * Tp, D_MODEL)
    xs = x_sample.reshape(Bs * Ts, D_MODEL)
    tile_s = Bs * Ts
    caches_p = ()
    caches_s = ()
    for l in range(depth):
        final = l == depth - 1
        mod_p = mod[l, :Bp].reshape(Bp, 1, N_MOD * D_MODEL)
        mod_s = mod[l, Bp:]
        mix_p, *caches_p = _prompt_mixers(xp.reshape(Bp, Tp, D_MODEL), mod_p, norm_mix_g[l], w_in_b, attn_sinks[l],
                                          conv_w[l], conv_b[l], conv_ln_g[l], conv_ln_b[l], hlb, hgrn_norm_g[l],
                                          bias_p, l, caches_p)
        xp = _out_mlp(mix_p.reshape(Bp * Tp, D_MODEL), xp, mod_p, norm_mlp_g[l], w_out_b, w_up_b, w_down_b,
                      final_g, l, TOK_TILE, Tp // TOK_TILE, final)
        proj_s = _inproj(xs, mod_s, norm_mix_g[l], w_in_b, l, tile_s, 1)
        mix_s, *caches_s = _sample_mixers(proj_s, attn_sinks[l], cache_conv, state_hgrn, ck, cv, conv_w[l],
                                          conv_b[l], conv_ln_g[l], conv_ln_b[l], hlb, hgrn_norm_g[l], bias_s, l,
                                          caches_s)
        xs = _out_mlp(mix_s, xs, mod_s, norm_mlp_g[l], w_out_b, w_up_b, w_down_b, final_g, l, tile_s, 1, final)
    cp, sp, kp, vp = caches_p
    cs, ss, ksn, vsn = caches_s
    return (xp.reshape(Bp, Tp, D_MODEL), xs.reshape(Bs, Ts, D_MODEL), cp, cs, sp, ss,
            kp.reshape(depth, Bp, WINDOW, KV_HEADS, HEAD_DIM), ksn.reshape(depth, Bs, w_buf, KV_HEADS, HEAD_DIM),
            vp.reshape(depth, Bp, WINDOW, KV_HEADS, HEAD_DIM), vsn.reshape(depth, Bs, w_buf, KV_HEADS, HEAD_DIM))
```

```python
import functools
import math

import jax
import jax.numpy as jnp
from jax import lax
from jax.experimental import pallas as pl
from jax.experimental.pallas import tpu as pltpu

F32 = jnp.float32
BF16 = jnp.bfloat16

D_MODEL = 1024
D_CONV = 256
CONV_WIDTH = 31
H_HGRN = 4
DK_HGRN = 128
DV_HGRN = 128
D_HGRN = 512
HEAD_DIM = 64
H_ATTN = 4
KV_HEADS = 2
GROUP = H_ATTN // KV_HEADS
D_ATTN = H_ATTN * HEAD_DIM
D_KV = KV_HEADS * HEAD_DIM
WINDOW = 128
ATTN_BLOCK = 128
NUM_BUCKETS = 32
MAX_DISTANCE = 128
D_FF = 4 * D_MODEL
N_MOD = 6
EPS = 1e-6

OFF_AVAL = 0
OFF_AGATE = OFF_AVAL + D_CONV
OFF_Q = OFF_AGATE + D_CONV
OFF_F = OFF_Q + H_HGRN * DK_HGRN
OFF_I = OFF_F + H_HGRN * DK_HGRN
OFF_G = OFF_I + D_HGRN
OFF_QA = OFF_G + D_HGRN
OFF_KA = OFF_QA + D_ATTN
OFF_VA = OFF_KA + D_KV
IN_WIDTH = OFF_VA + D_KV

HGRN_CHUNK = 64
HGRN_KEYBLOCK = 32
HGRN_SPAN = 4
SUBLANES = 8
CONV_PAD = 32
MIX_TILE = 1024
TOK_TILE = 512
SAMPLE_BLOCK = 8
FF_CHUNK = 1024
VMEM_LIMIT = 56 * 1024 * 1024

NT_DIMS = (((1,), (1,)), ((), ()))
TN_DIMS = (((0,), (0,)), ((), ()))


def _silu(x):
    return x * jax.nn.sigmoid(x)


def _rms_rows(x):
    return x * lax.rsqrt(jnp.mean(x * x, axis=-1, keepdims=True) + EPS)


def _layer_lb(hlb, layer):
    m = jnp.max(hlb, axis=0, keepdims=True)
    e = jnp.exp(hlb - m)
    p = e / jnp.sum(e, axis=0, keepdims=True)
    lb = jnp.zeros_like(m)
    for i in range(1, layer + 1):
        lb = lb + p[i:i + 1, :]
    return lb


def _split3_bf16(x):
    hi = x.astype(BF16)
    r = x - hi.astype(F32)
    mid = r.astype(BF16)
    return hi, mid, (r - mid.astype(F32)).astype(BF16)


def _select_rows_mxu(sel, x):
    return sum(jnp.dot(sel, part, preferred_element_type=F32) for part in _split3_bf16(x))


def _cumsum_rows_small(g):
    row = lax.broadcasted_iota(jnp.int32, g.shape, 0)
    b = jnp.zeros_like(g)
    for u in range(g.shape[0]):
        b = b + jnp.where(row >= u, g[u:u + 1, :], 0.0)
    return b


def _hgrn_span(proj_ref, mix_ref, st_ref, row0, lb, hng, tri, tick):
    L, KB = HGRN_CHUNK, HGRN_KEYBLOCK
    span = HGRN_SPAN * L
    g, k = _hgrn_gates(proj_ref[pl.ds(row0, span), OFF_F:OFF_F + D_HGRN], lb)
    b = _select_rows_mxu(tri, g)
    units = [(c, h) for c in range(HGRN_SPAN) for h in range(H_HGRN)]

    ops = {}
    for c, h in units:
        rows = pl.ds(row0 + c * L, L)
        cs = slice(h * DK_HGRN, (h + 1) * DK_HGRN)
        q = proj_ref[rows, OFF_Q + h * DK_HGRN:OFF_Q + (h + 1) * DK_HGRN]
        v = proj_ref[rows, OFF_I + h * DV_HGRN:OFF_I + (h + 1) * DV_HGRN].astype(BF16)
        kk = k[c * L:(c + 1) * L, cs]
        bb = b[c * L:(c + 1) * L, cs]
        qp, kp = [], []
        for lo in range(0, L, KB):
            r = bb[lo + KB // 2 - 1:lo + KB // 2, :]
            kp.append((kk[lo:lo + KB] * jnp.exp(r - bb[lo:lo + KB])).astype(BF16))
            qp.append((q[lo:] * jnp.exp(bb[lo:] - r)).astype(BF16))
        bl = bb[L - 1:L, :]
        ops[c, h] = dict(qp=qp, kp=kp, v=v, qt=(q * jnp.exp(bb)).astype(BF16),
                         kst=(kk * jnp.exp(bl - bb)).astype(BF16), e=jnp.exp(bl))
    tick()

    for u in units:
        o = ops[u]
        o["p"] = [lax.dot_general(qp, kp, NT_DIMS, preferred_element_type=F32) for qp, kp in zip(o["qp"], o["kp"])]
        o["m"] = lax.dot_general(o["v"], o["kst"], TN_DIMS, preferred_element_type=F32)
    tick()

    for u in units:
        pm = []
        for p in ops[u]["p"]:
            row = lax.broadcasted_iota(jnp.int32, p.shape, 0)
            col = lax.broadcasted_iota(jnp.int32, p.shape, 1)
            pm.append(jnp.where(row >= col, p, 0.0).astype(BF16))
        ops[u]["p"] = pm
    tick()

    for u in units:
        o = ops[u]
        blocks = [None] * (L // KB)
        for j, p in enumerate(o["p"]):
            cj = jnp.dot(p, o["v"][j * KB:(j + 1) * KB], preferred_element_type=F32)
            for i in range(j, L // KB):
                piece = cj[(i - j) * KB:(i - j + 1) * KB]
                blocks[i] = piece if blocks[i] is None else blocks[i] + piece
        o["o"] = jnp.concatenate(blocks, axis=0)
    tick()

    for h in range(H_HGRN):
        st = st_ref[h]
        for c in range(HGRN_SPAN):
            o = ops[c, h]
            out = o["o"] + lax.dot_general(o["qt"], st.astype(BF16), NT_DIMS, preferred_element_type=F32)
            st = o["e"] * st + o["m"]
            rows = pl.ds(row0 + c * L, L)
            gate = proj_ref[rows, OFF_G + h * DV_HGRN:OFF_G + (h + 1) * DV_HGRN]
            mix_ref[rows, D_CONV + h * DV_HGRN:D_CONV + (h + 1) * DV_HGRN] = _hgrn_out(out, hng, gate).astype(BF16)
        st_ref[h] = st


def _sink_softmax(s, sink):
    m = jnp.maximum(jnp.max(s, axis=-1, keepdims=True), sink)
    p = jnp.exp(s - m)
    return p, jnp.sum(p, axis=-1, keepdims=True) + jnp.exp(sink - m)


def _bias_kernel(tab_ref, bp_ref, bs_ref, op_ref, os_ref, *, seq):
    bk = bp_ref[...]
    for h in range(H_ATTN):
        acc = jnp.full(bk.shape, -jnp.inf, F32)
        for bkt in range(NUM_BUCKETS):
            acc = jnp.where(bk == bkt, tab_ref[bkt, h], acc)
        op_ref[h] = acc
    bk = bs_ref[...]
    row = lax.broadcasted_iota(jnp.int32, bk.shape, 0)
    for kv in range(KV_HEADS):
        acc = jnp.full(bk.shape, -jnp.inf, F32)
        for bkt in range(NUM_BUCKETS):
            val = jnp.full(bk.shape, tab_ref[bkt, kv * GROUP], F32)
            for gi in range(1, GROUP):
                val = jnp.where(row >= gi * seq, tab_ref[bkt, kv * GROUP + gi], val)
            acc = jnp.where(bk == bkt, val, acc)
        os_ref[kv] = acc


def _t5_bucket(rel):
    n = jnp.maximum(rel, 0)
    max_exact = NUM_BUCKETS // 2
    nf = jnp.maximum(n, max_exact).astype(F32)
    large = max_exact + (jnp.log(nf / max_exact) / math.log(MAX_DISTANCE / max_exact)
                         * (NUM_BUCKETS - max_exact)).astype(jnp.int32)
    large = jnp.minimum(large, NUM_BUCKETS - 1)
    return jnp.where(n < max_exact, n, large)


def _bias_tables(rel_bias, dec_seq, w_buf):
    qi = jnp.arange(ATTN_BLOCK, dtype=jnp.int32)[:, None]
    kc = jnp.arange(2 * ATTN_BLOCK, dtype=jnp.int32)[None, :]
    rel_p = qi + ATTN_BLOCK - kc
    bucket_p = jnp.where((rel_p >= 0) & (rel_p <= WINDOW), _t5_bucket(rel_p), -1)
    ts = (jnp.arange(GROUP * dec_seq, dtype=jnp.int32) % dec_seq)[:, None]
    js = jnp.arange(2 * ATTN_BLOCK, dtype=jnp.int32)[None, :]
    rel_s = w_buf + ts - js
    ok_s = (rel_s >= 0) & (rel_s <= WINDOW) & (js < w_buf + dec_seq)
    bucket_s = jnp.where(ok_s, _t5_bucket(rel_s), -1)
    return pl.pallas_call(
        functools.partial(_bias_kernel, seq=dec_seq),
        out_shape=(jax.ShapeDtypeStruct((H_ATTN, ATTN_BLOCK, 2 * ATTN_BLOCK), F32),
                   jax.ShapeDtypeStruct((KV_HEADS, GROUP * dec_seq, 2 * ATTN_BLOCK), F32)),
        in_specs=[pl.BlockSpec(memory_space=pltpu.SMEM),
                  pl.BlockSpec(memory_space=pltpu.VMEM),
                  pl.BlockSpec(memory_space=pltpu.VMEM)],
        out_specs=(pl.BlockSpec(memory_space=pltpu.VMEM), pl.BlockSpec(memory_space=pltpu.VMEM)),
        name="rel_bias_tables",
    )(rel_bias.astype(F32), bucket_p, bucket_s)


def _mod_kernel(c_ref, w_ref, b_ref, o_ref):
    s = _silu(c_ref[...]).astype(BF16)
    o_ref[...] = jnp.dot(s, w_ref[...].astype(BF16), preferred_element_type=F32) + b_ref[...]


def _modulation(c_all, w_ada, b_ada):
    depth = w_ada.shape[0]
    n = c_all.shape[0]
    return pl.pallas_call(
        _mod_kernel,
        out_shape=jax.ShapeDtypeStruct((depth, n, N_MOD * D_MODEL), F32),
        grid=(depth, N_MOD),
        in_specs=[pl.BlockSpec((n, D_MODEL), lambda l, j: (0, 0)),
                  pl.BlockSpec((None, D_MODEL, D_MODEL), lambda l, j: (l, 0, j)),
                  pl.BlockSpec((None, 1, D_MODEL), lambda l, j: (l, 0, j))],
        out_specs=pl.BlockSpec((None, n, D_MODEL), lambda l, j: (l, 0, j)),
        compiler_params=pltpu.CompilerParams(dimension_semantics=("arbitrary", "arbitrary"),
                                             vmem_limit_bytes=VMEM_LIMIT),
        name="adaln_modulation",
    )(c_all, w_ada, b_ada.reshape(depth, 1, N_MOD * D_MODEL))


def _mod_spec(mod, chunk, tiles_per_batch):
    if mod.ndim == 3:
        return pl.BlockSpec((None, 1, D_MODEL), lambda i: (i // tiles_per_batch, 0, chunk))
    return pl.BlockSpec((mod.shape[0], D_MODEL), lambda i: (0, chunk))


def _mod_rows(m, n_tokens):
    if m.shape[0] == 1:
        return m
    reps = n_tokens // m.shape[0]
    tok = lax.broadcasted_iota(jnp.int32, (n_tokens, m.shape[0]), 0)
    bat = lax.broadcasted_iota(jnp.int32, (n_tokens, m.shape[0]), 1)
    sel = jnp.where((tok >= bat * reps) & (tok < (bat + 1) * reps), 1.0, 0.0).astype(BF16)
    return _select_rows_mxu(sel, m)


def _inproj_kernel(x_ref, sh_ref, sc_ref, g_ref, w_ref, o_ref):
    n = x_ref.shape[0]
    h = _rms_rows(x_ref[...]) * g_ref[...] * (1.0 + _mod_rows(sc_ref[...], n)) + _mod_rows(sh_ref[...], n)
    o_ref[...] = jnp.dot(h.astype(BF16), w_ref[...], preferred_element_type=F32)


def _inproj(x2, mod, norm_g, w_in, layer, tile, tiles_per_batch):
    n = x2.shape[0]
    return pl.pallas_call(
        _inproj_kernel,
        out_shape=jax.ShapeDtypeStruct((n, IN_WIDTH), F32),
        grid=(n // tile,),
        in_specs=[pl.BlockSpec((tile, D_MODEL), lambda i: (i, 0)),
                  _mod_spec(mod, 0, tiles_per_batch),
                  _mod_spec(mod, 1, tiles_per_batch),
                  pl.BlockSpec((1, D_MODEL), lambda i: (0, 0)),
                  pl.BlockSpec((None, D_MODEL, IN_WIDTH), lambda i: (layer, 0, 0), pipeline_mode=pl.Buffered(1))],
        out_specs=pl.BlockSpec((tile, IN_WIDTH), lambda i: (i, 0)),
        compiler_params=pltpu.CompilerParams(dimension_semantics=("arbitrary",), vmem_limit_bytes=VMEM_LIMIT),
        name="in_projection",
    )(x2, mod, mod, norm_g.reshape(1, D_MODEL), w_in)


def _mlp_kernel(mix_ref, x_ref, g1_ref, sh_ref, sc_ref, g2_ref, ng_ref, wout_ref, wup_ref, wdn_ref, fg_ref, o_ref, *,
                final):
    n = x_ref.shape[0]
    x1 = x_ref[...] + _mod_rows(g1_ref[...], n) * jnp.dot(mix_ref[...].astype(BF16), wout_ref[...],
                                                          preferred_element_type=F32)
    h = (_rms_rows(x1) * ng_ref[...] * (1.0 + _mod_rows(sc_ref[...], n)) + _mod_rows(sh_ref[...], n)).astype(BF16)
    acc = None
    for c in range(D_FF // FF_CHUNK):
        u = jnp.dot(h, wup_ref[:, c * FF_CHUNK:(c + 1) * FF_CHUNK], preferred_element_type=F32)
        u = jnp.square(jnp.maximum(u, 0.0)).astype(BF16)
        d = jnp.dot(u, wdn_ref[c * FF_CHUNK:(c + 1) * FF_CHUNK, :], preferred_element_type=F32)
        acc = d if acc is None else acc + d
    x2 = x1 + _mod_rows(g2_ref[...], n) * acc
    if final:
        x2 = _rms_rows(x2) * fg_ref[...]
    o_ref[...] = x2


def _out_mlp(mix2, x2, mod, norm_g, w_out, w_up, w_down, final_g, layer, tile, tiles_per_batch, final):
    n = x2.shape[0]
    const = lambda i: (0, 0)
    of_layer = lambda i: (layer, 0, 0)
    return pl.pallas_call(
        functools.partial(_mlp_kernel, final=final),
        out_shape=jax.ShapeDtypeStruct((n, D_MODEL), F32),
        grid=(n // tile,),
        in_specs=[pl.BlockSpec((tile, D_MODEL), lambda i: (i, 0)),
                  pl.BlockSpec((tile, D_MODEL), lambda i: (i, 0)),
                  _mod_spec(mod, 2, tiles_per_batch),
                  _mod_spec(mod, 3, tiles_per_batch),
                  _mod_spec(mod, 4, tiles_per_batch),
                  _mod_spec(mod, 5, tiles_per_batch),
                  pl.BlockSpec((1, D_MODEL), const),
                  pl.BlockSpec((None, D_MODEL, D_MODEL), of_layer, pipeline_mode=pl.Buffered(1)),
                  pl.BlockSpec((None, D_MODEL, D_FF), of_layer, pipeline_mode=pl.Buffered(1)),
                  pl.BlockSpec((None, D_FF, D_MODEL), of_layer, pipeline_mode=pl.Buffered(1)),
                  pl.BlockSpec((1, D_MODEL), const)],
        out_specs=pl.BlockSpec((tile, D_MODEL), lambda i: (i, 0)),
        compiler_params=pltpu.CompilerParams(dimension_semantics=("arbitrary",), vmem_limit_bytes=VMEM_LIMIT),
        name="out_projection_mlp",
    )(mix2, x2, mod, mod, mod, mod, norm_g.reshape(1, D_MODEL), w_out, w_up, w_down, final_g.reshape(1, D_MODEL))


def _conv_ln_swish(acc, lng, lnb):
    mu = jnp.mean(acc, axis=-1, keepdims=True)
    xc = acc - mu
    y = xc * lax.rsqrt(jnp.mean(xc * xc, axis=-1, keepdims=True) + EPS) * lng + lnb
    return _silu(y)


def _hgrn_gates(fh, lb):
    f = lb + (1.0 - lb) * jax.nn.sigmoid(fh)
    return jnp.log(f), 1.0 - f


def _hgrn_out(o, hng, gate):
    return _rms_rows(o) * hng * _silu(gate)


def _ticker(pieces):
    it = iter(pieces)

    def tick():
        piece = next(it, None)
        if piece is not None:
            piece()

    def flush():
        for piece in it:
            piece()

    tick.flush = flush
    return tick


def _prompt_mix_kernel(sinks_ref, x_ref, sh_ref, sc_ref, ng_ref, win_ref, convw_ref, convb_ref, lng_ref, lnb_ref,
                       hlb_ref, hng_ref, bias_ref,
                       mix_ref, convo_ref, so_ref, ko_ref, vo_ref,
                       proj_ref, abuf, kbuf, vbuf, st_ref, *, layer, tile):
    t = pl.program_id(1)
    last = pl.num_programs(1) - 1

    @pl.when(t == 0)
    def _():
        abuf[0:CONV_PAD, :] = jnp.zeros((CONV_PAD, D_CONV), F32)
        abuf[CONV_PAD + tile:CONV_PAD + tile + SUBLANES, :] = jnp.zeros((SUBLANES, D_CONV), F32)
        kbuf[0:ATTN_BLOCK, :] = jnp.zeros((ATTN_BLOCK, D_KV), BF16)
        vbuf[0:ATTN_BLOCK, :] = jnp.zeros((ATTN_BLOCK, D_KV), BF16)
        st_ref[...] = jnp.zeros(st_ref.shape, F32)

    h_in = (_rms_rows(x_ref[...]) * ng_ref[...] * (1.0 + sc_ref[...]) + sh_ref[...]).astype(BF16)
    for lo, hi in ((OFF_AVAL, OFF_Q), (OFF_F, OFF_I), (OFF_Q, OFF_F), (OFF_I, OFF_G), (OFF_G, OFF_QA),
                   (OFF_QA, IN_WIDTH)):
        proj_ref[:, lo:hi] = jnp.dot(h_in, win_ref[:, lo:hi], preferred_element_type=F32)

    kbuf[ATTN_BLOCK:ATTN_BLOCK + tile, :] = proj_ref[:, OFF_KA:OFF_KA + D_KV].astype(BF16)
    vbuf[ATTN_BLOCK:ATTN_BLOCK + tile, :] = proj_ref[:, OFF_VA:OFF_VA + D_KV].astype(BF16)
    scale = HEAD_DIM ** -0.5
    attn = {}

    def attn_scores(blk):
        def run():
            r0 = blk * ATTN_BLOCK
            for h in range(H_ATTN):
                kv = h // GROUP
                q = (proj_ref[r0:r0 + ATTN_BLOCK, OFF_QA + h * HEAD_DIM:OFF_QA + (h + 1) * HEAD_DIM]
                     * scale).astype(BF16)
                kall = kbuf[r0:r0 + 2 * ATTN_BLOCK, kv * HEAD_DIM:(kv + 1) * HEAD_DIM]
                attn[blk, h] = lax.dot_general(q, kall, NT_DIMS, preferred_element_type=F32)
        return run

    def attn_softmax(blk):
        def run():
            for h in range(H_ATTN):
                s = attn[blk, h] + bias_ref[h]
                if blk == 0:
                    col = lax.broadcasted_iota(jnp.int32, s.shape, 1)
                    s = jnp.where(col + (t * tile - ATTN_BLOCK) >= 0, s, -jnp.inf)
                p, den = _sink_softmax(s, sinks_ref[h])
                attn[blk, h] = (p.astype(BF16), den)
        return run

    def attn_values(blk):
        def run():
            r0 = blk * ATTN_BLOCK
            heads = []
            for h in range(H_ATTN):
                kv = h // GROUP
                p, den = attn[blk, h]
                vall = vbuf[r0:r0 + 2 * ATTN_BLOCK, kv * HEAD_DIM:(kv + 1) * HEAD_DIM]
                heads.append(jnp.dot(p, vall, preferred_element_type=F32) / den)
            mix_ref[r0:r0 + ATTN_BLOCK, D_CONV + D_HGRN:D_MODEL] = jnp.concatenate(heads, axis=1).astype(BF16)
        return run

    tick = _ticker([stage(blk) for blk in range(tile // ATTN_BLOCK)
                    for stage in (attn_scores, attn_softmax, attn_values)])

    abuf[CONV_PAD:CONV_PAD + tile, :] = (proj_ref[:, OFF_AVAL:OFF_AVAL + D_CONV]
                                         * jax.nn.sigmoid(proj_ref[:, OFF_AGATE:OFF_AGATE + D_CONV]))
    first_row = CONV_PAD - (CONV_WIDTH - 1)
    acc = jnp.broadcast_to(convb_ref[...], (tile, D_CONV))
    for r in range(SUBLANES):
        z = None
        for off in range(r, first_row + CONV_WIDTH, SUBLANES):
            j = off - first_row
            if j < 0:
                continue
            term = convw_ref[j:j + 1, :] * abuf[off - r:off - r + tile + SUBLANES, :]
            z = term if z is None else z + term
        acc = acc + (z[0:tile] if r == 0 else pltpu.roll(z, tile + SUBLANES - r, 0)[0:tile])
    mix_ref[:, 0:D_CONV] = _conv_ln_swish(acc, lng_ref[...], lnb_ref[...]).astype(BF16)
    tick()

    lb = _layer_lb(hlb_ref[...], layer)
    hng = hng_ref[...]
    span = HGRN_SPAN * HGRN_CHUNK
    ri = lax.broadcasted_iota(jnp.int32, (span, span), 0)
    ci = lax.broadcasted_iota(jnp.int32, (span, span), 1)
    tri = jnp.where((ri >= ci) & (ri // HGRN_CHUNK == ci // HGRN_CHUNK), 1.0, 0.0).astype(BF16)
    for i in range(tile // span):
        _hgrn_span(proj_ref, mix_ref, st_ref, i * span, lb, hng, tri, tick)
    tick.flush()

    @pl.when(t == last)
    def _():
        convo_ref[...] = abuf[CONV_PAD + tile - (CONV_WIDTH - 1):CONV_PAD + tile, :]
        for h in range(H_HGRN):
            so_ref[h] = st_ref[h].T
        ko_ref[...] = proj_ref[tile - WINDOW:tile, OFF_KA:OFF_KA + D_KV]
        vo_ref[...] = proj_ref[tile - WINDOW:tile, OFF_VA:OFF_VA + D_KV]

    abuf[0:CONV_PAD, :] = abuf[tile:tile + CONV_PAD, :]
    kbuf[0:ATTN_BLOCK, :] = kbuf[tile:tile + ATTN_BLOCK, :]
    vbuf[0:ATTN_BLOCK, :] = vbuf[tile:tile + ATTN_BLOCK, :]


def _carry_specs(carried):
    return [pl.BlockSpec(memory_space=pl.ANY)] * len(carried)


def _carried(kernel_fn, n_in, n_carried):
    if n_carried == 0:
        return kernel_fn
    return lambda *refs: kernel_fn(*refs[:n_in], *refs[n_in + n_carried:])


def _prompt_mixers(x, mod, norm_g, w_in, sinks, conv_w, conv_b, ln_g, ln_b, hgrn_lb, hng, bias_p, layer, carried):
    B, T = x.shape[:2]
    tile = MIX_TILE
    depth = hgrn_lb.shape[0]
    const2 = lambda b, t: (0, 0)
    inputs = (sinks, x, mod, mod, norm_g.reshape(1, D_MODEL), w_in, conv_w, conv_b.reshape(1, D_CONV),
              ln_g.reshape(1, D_CONV), ln_b.reshape(1, D_CONV), hgrn_lb, hng.reshape(1, DV_HGRN), bias_p)
    return pl.pallas_call(
        _carried(functools.partial(_prompt_mix_kernel, layer=layer, tile=tile), len(inputs), len(carried)),
        out_shape=(jax.ShapeDtypeStruct((B, T, D_MODEL), BF16),
                   jax.ShapeDtypeStruct((depth, B, CONV_WIDTH - 1, D_CONV), F32),
                   jax.ShapeDtypeStruct((depth, B, H_HGRN, DK_HGRN, DV_HGRN), F32),
                   jax.ShapeDtypeStruct((depth, B, WINDOW, D_KV), F32),
                   jax.ShapeDtypeStruct((depth, B, WINDOW, D_KV), F32)),
        grid=(B, T // tile),
        in_specs=[pl.BlockSpec(memory_space=pltpu.SMEM),
                  pl.BlockSpec((None, tile, D_MODEL), lambda b, t: (b, t, 0)),
                  pl.BlockSpec((None, 1, D_MODEL), lambda b, t: (b, 0, 0)),
                  pl.BlockSpec((None, 1, D_MODEL), lambda b, t: (b, 0, 1)),
                  pl.BlockSpec((1, D_MODEL), const2),
                  pl.BlockSpec((None, D_MODEL, IN_WIDTH), lambda b, t: (layer, 0, 0), pipeline_mode=pl.Buffered(1)),
                  pl.BlockSpec((CONV_WIDTH, D_CONV), const2),
                  pl.BlockSpec((1, D_CONV), const2),
                  pl.BlockSpec((1, D_CONV), const2),
                  pl.BlockSpec((1, D_CONV), const2),
                  pl.BlockSpec((depth, D_HGRN), const2),
                  pl.BlockSpec((1, DV_HGRN), const2),
                  pl.BlockSpec((H_ATTN, ATTN_BLOCK, 2 * ATTN_BLOCK), lambda b, t: (0, 0, 0))] + _carry_specs(carried),
        out_specs=(pl.BlockSpec((None, tile, D_MODEL), lambda b, t: (b, t, 0)),
                   pl.BlockSpec((None, None, CONV_WIDTH - 1, D_CONV), lambda b, t: (layer, b, 0, 0)),
                   pl.BlockSpec((None, None, H_HGRN, DK_HGRN, DV_HGRN), lambda b, t: (layer, b, 0, 0, 0)),
                   pl.BlockSpec((None, None, WINDOW, D_KV), lambda b, t: (layer, b, 0, 0)),
                   pl.BlockSpec((None, None, WINDOW, D_KV), lambda b, t: (layer, b, 0, 0))),
        input_output_aliases={len(inputs) + i: 1 + i for i in range(len(carried))},
        scratch_shapes=[pltpu.VMEM((tile, IN_WIDTH), F32),
                        pltpu.VMEM((CONV_PAD + tile + SUBLANES, D_CONV), F32),
                        pltpu.VMEM((ATTN_BLOCK + tile, D_KV), BF16),
                        pltpu.VMEM((ATTN_BLOCK + tile, D_KV), BF16),
                        pltpu.VMEM((H_HGRN, DV_HGRN, DK_HGRN), F32)],
        compiler_params=pltpu.CompilerParams(dimension_semantics=("arbitrary", "arbitrary"),
                                             vmem_limit_bytes=VMEM_LIMIT),
        name="prompt_mixers",
    )(*inputs, *carried)


def _sample_mix_kernel(sinks_ref, proj_ref, cconv_ref, state_ref, ck_ref, cv_ref, convw_ref, convb_ref, lng_ref,
                       lnb_ref, hlb_ref, hng_ref, bias_ref,
                       mix_ref, convo_ref, so_ref, ko_ref, vo_ref, full_ref, kall_ref, vall_ref, *,
                       layer, block, seq, w_buf):
    hist = CONV_WIDTH - 1
    n_keys = w_buf + seq
    pad_keys = kall_ref.shape[1]

    @pl.when(pl.program_id(0) == 0)
    def _():
        for ref in (kall_ref, vall_ref):
            ref[:, w_buf:pad_keys, :] = jnp.zeros((block, pad_keys - w_buf, D_KV), F32)

    lb = _layer_lb(hlb_ref[...], layer)
    hng = hng_ref[...]
    scale = HEAD_DIM ** -0.5
    elems = range(block)
    row8 = lax.broadcasted_iota(jnp.int32, (SUBLANES, DV_HGRN), 0)
    ones_rows = jnp.where((row8 >= seq) & (row8 < seq + 3), 1.0, 0.0)
    zrow = jnp.zeros((1, DK_HGRN), BF16)
    prow = lax.broadcasted_iota(jnp.int32, (seq, seq), 0)
    pcol = lax.broadcasted_iota(jnp.int32, (seq, seq), 1)
    grow = lax.broadcasted_iota(jnp.int32, (GROUP * seq, 1), 0)

    proj = [proj_ref[e * seq:(e + 1) * seq, :] for e in elems]

    out_a = []
    for e in elems:
        p = proj[e]
        full_ref[e, 0:hist, :] = cconv_ref[e]
        full_ref[e, hist:hist + seq, :] = (p[:, OFF_AVAL:OFF_AVAL + D_CONV]
                                           * jax.nn.sigmoid(p[:, OFF_AGATE:OFF_AGATE + D_CONV]))
        rows = [jnp.sum(convw_ref[...] * full_ref[e, s:s + CONV_WIDTH, :], axis=0, keepdims=True) for s in range(seq)]
        convo_ref[e] = full_ref[e, seq:seq + hist, :]
        out_a.append(_conv_ln_swish(jnp.concatenate(rows, axis=0) + convb_ref[...], lng_ref[...], lnb_ref[...]))

    units = [(e, h) for e in elems for h in range(H_HGRN)]
    ops = {}
    for e in elems:
        p = proj[e]
        g, k = _hgrn_gates(p[:, OFF_F:OFF_F + D_HGRN], lb)
        b = _cumsum_rows_small(g)
        for h in range(H_HGRN):
            cs = slice(h * DK_HGRN, (h + 1) * DK_HGRN)
            q = p[:, OFF_Q + h * DK_HGRN:OFF_Q + (h + 1) * DK_HGRN]
            v = p[:, OFF_I + h * DV_HGRN:OFF_I + (h + 1) * DV_HGRN]
            bb = b[:, cs]
            bl = bb[seq - 1:seq, :]
            kst = (k[:, cs] * jnp.exp(bl - bb)).astype(BF16)
            x = jnp.concatenate([kst.astype(F32), *(part.astype(F32) for part in _split3_bf16(jnp.exp(bl))),
                                 zrow.astype(F32)], axis=0).astype(BF16)
            vpad = jnp.concatenate([v, jnp.zeros((SUBLANES - seq, DV_HGRN), F32)], axis=0)
            ops[e, h] = dict(qp=(q * jnp.exp(bb - bl)).astype(BF16), kst=kst, v=v.astype(BF16),
                             qt=(q * jnp.exp(bb)).astype(BF16), x=x,
                             r=jnp.concatenate([vpad, ones_rows], axis=1).astype(BF16))
    for u in units:
        o = ops[u]
        st = state_ref[u[0], u[1]]
        o["p"] = lax.dot_general(o["qp"], o["kst"], NT_DIMS, preferred_element_type=F32)
        o["inter"] = jnp.dot(o["qt"], st.astype(BF16), preferred_element_type=F32)
        me = lax.dot_general(o["x"], o["r"], TN_DIMS, preferred_element_type=F32)
        so_ref[u[0], u[1]] = me[:, DV_HGRN:] * st + me[:, :DV_HGRN]
    out_b = {}
    for u in units:
        o = ops[u]
        pm = jnp.where(prow >= pcol, o["p"], 0.0).astype(BF16)
        out = o["inter"] + jnp.dot(pm, o["v"], preferred_element_type=F32)
        gate = proj[u[0]][:, OFF_G + u[1] * DV_HGRN:OFF_G + (u[1] + 1) * DV_HGRN]
        out_b[u] = _hgrn_out(out, hng, gate)

    for e in elems:
        p = proj[e]
        for ref, cache, new, out in ((kall_ref, ck_ref, p[:, OFF_KA:OFF_KA + D_KV], ko_ref),
                                     (vall_ref, cv_ref, p[:, OFF_VA:OFF_VA + D_KV], vo_ref)):
            ref[e, 0:w_buf, :] = cache[e]
            ref[e, w_buf:n_keys, :] = new
            out[e] = ref[e, seq:n_keys, :]
    scores = {}
    for e in elems:
        p = proj[e]
        for kv in range(KV_HEADS):
            hs = slice(kv * HEAD_DIM, (kv + 1) * HEAD_DIM)
            q2 = jnp.concatenate([p[:, OFF_QA + h * HEAD_DIM:OFF_QA + (h + 1) * HEAD_DIM]
                                  for h in range(kv * GROUP, (kv + 1) * GROUP)], axis=0)
            kall = kall_ref[e][:, hs].astype(BF16)
            scores[e, kv] = (lax.dot_general((q2 * scale).astype(BF16), kall, NT_DIMS, preferred_element_type=F32)
                             + bias_ref[kv])
    out_c = {}
    for e in elems:
        for kv in range(KV_HEADS):
            hs = slice(kv * HEAD_DIM, (kv + 1) * HEAD_DIM)
            sink = jnp.zeros((GROUP * seq, 1), F32)
            for gi in range(GROUP):
                sink = jnp.where(grow >= gi * seq, sinks_ref[kv * GROUP + gi], sink)
            pr, den = _sink_softmax(scores[e, kv], sink)
            o2 = jnp.dot(pr.astype(BF16), vall_ref[e][:, hs].astype(BF16), preferred_element_type=F32) / den
            for gi in range(GROUP):
                out_c[e, kv * GROUP + gi] = o2[gi * seq:(gi + 1) * seq]

    for e in elems:
        parts = [out_a[e]] + [out_b[e, h] for h in range(H_HGRN)] + [out_c[e, h] for h in range(H_ATTN)]
        mix_ref[e * seq:(e + 1) * seq, :] = jnp.concatenate(parts, axis=1)


def _sample_mixers(proj2, sinks, cache_conv, state, cache_k, cache_v, conv_w, conv_b, ln_g, ln_b, hgrn_lb, hng,
                   bias_s, layer, carried):
    B = state.shape[1]
    seq = proj2.shape[0] // B
    w_buf = cache_k.shape[2]
    block = SAMPLE_BLOCK
    depth = hgrn_lb.shape[0]
    hist = CONV_WIDTH - 1
    const2 = lambda i: (0, 0)
    cache_specs = [pl.BlockSpec((None, block, hist, D_CONV), lambda i: (layer, i, 0, 0)),
                   pl.BlockSpec((None, block, H_HGRN, DK_HGRN, DV_HGRN), lambda i: (layer, i, 0, 0, 0)),
                   pl.BlockSpec((None, block, w_buf, D_KV), lambda i: (layer, i, 0, 0)),
                   pl.BlockSpec((None, block, w_buf, D_KV), lambda i: (layer, i, 0, 0))]
    inputs = (sinks, proj2, cache_conv, state, cache_k, cache_v, conv_w, conv_b.reshape(1, D_CONV),
              ln_g.reshape(1, D_CONV), ln_b.reshape(1, D_CONV), hgrn_lb, hng.reshape(1, DV_HGRN), bias_s)
    return pl.pallas_call(
        _carried(functools.partial(_sample_mix_kernel, layer=layer, block=block, seq=seq, w_buf=w_buf),
                 len(inputs), len(carried)),
        out_shape=(jax.ShapeDtypeStruct((B * seq, D_MODEL), F32),
                   jax.ShapeDtypeStruct((depth, B, hist, D_CONV), F32),
                   jax.ShapeDtypeStruct((depth, B, H_HGRN, DK_HGRN, DV_HGRN), F32),
                   jax.ShapeDtypeStruct((depth, B, w_buf, D_KV), F32),
                   jax.ShapeDtypeStruct((depth, B, w_buf, D_KV), F32)),
        grid=(B // block,),
        in_specs=[pl.BlockSpec(memory_space=pltpu.SMEM),
                  pl.BlockSpec((block * seq, IN_WIDTH), lambda i: (i, 0))] + cache_specs + [
                  pl.BlockSpec((CONV_WIDTH, D_CONV), const2),
                  pl.BlockSpec((1, D_CONV), const2),
                  pl.BlockSpec((1, D_CONV), const2),
                  pl.BlockSpec((1, D_CONV), const2),
                  pl.BlockSpec((depth, D_HGRN), const2),
                  pl.BlockSpec((1, DV_HGRN), const2),
                  pl.BlockSpec((KV_HEADS, GROUP * seq, 2 * ATTN_BLOCK), lambda i: (0, 0, 0))] + _carry_specs(carried),
        out_specs=tuple([pl.BlockSpec((block * seq, D_MODEL), lambda i: (i, 0))] + cache_specs),
        input_output_aliases={len(inputs) + i: 1 + i for i in range(len(carried))},
        scratch_shapes=[pltpu.VMEM((block, hist + seq + 6, D_CONV), F32),
                        pltpu.VMEM((block, 2 * ATTN_BLOCK, D_KV), F32),
                        pltpu.VMEM((block, 2 * ATTN_BLOCK, D_KV), F32)],
        compiler_params=pltpu.CompilerParams(dimension_semantics=("arbitrary",), vmem_limit_bytes=VMEM_LIMIT),
        name="sample_mixers",
    )(*inputs, *carried)


def kernel(x_prompt, x_sample, cache_conv, state_hgrn, cache_swa_k, cache_swa_v, c_prompt, c_sample, rel_bias, w_ada, b_ada, norm_mix_g, w_in, conv_w, conv_b, conv_ln_g, conv_ln_b, hgrn_lb, hgrn_norm_g, attn_sinks, w_out, norm_mlp_g, w_up, w_down, final_g):
    Bp, Tp = x_prompt.shape[:2]
    Bs, Ts = x_sample.shape[:2]
    depth = w_in.shape[0]
    w_buf = cache_swa_k.shape[2]
    assert Tp % MIX_TILE == 0 and (Bp * Tp) % TOK_TILE == 0 and Tp % TOK_TILE == 0 and Bs % SAMPLE_BLOCK == 0
    assert w_buf == WINDOW and GROUP * Ts == SUBLANES

    bias_p, bias_s = _bias_tables(rel_bias, Ts, w_buf)
    mod = _modulation(jnp.concatenate([c_prompt, c_sample], axis=0), w_ada, b_ada)
    w_in_b, w_out_b, w_up_b, w_down_b = (w.astype(BF16) for w in (w_in, w_out, w_up, w_down))
    hlb = hgrn_lb.astype(F32)
    ck = cache_swa_k.reshape(depth, Bs, w_buf, D_KV)
    cv = cache_swa_v.reshape(depth, Bs, w_buf, D_KV)

    xp = x_prompt.reshape(Bp * Tp, D_MODEL)
    xs = x_sample.reshape(Bs * Ts, D_MODEL)
    tile_s = Bs * Ts
    caches_p = ()
    caches_s = ()
    for l in range(depth):
        final = l == depth - 1
        mod_p = mod[l, :Bp].reshape(Bp, 1, N_MOD * D_MODEL)
        mod_s = mod[l, Bp:]
        mix_p, *caches_p = _prompt_mixers(xp.reshape(Bp, Tp, D_MODEL), mod_p, norm_mix_g[l], w_in_b, attn_sinks[l],
                                          conv_w[l], conv_b[l], conv_ln_g[l], conv_ln_b[l], hlb, hgrn_norm_g[l],
                                          bias_p, l, caches_p)
        xp = _out_mlp(mix_p.reshape(Bp * Tp, D_MODEL), xp, mod_p, norm_mlp_g[l], w_out_b, w_up_b, w_down_b,
                      final_g, l, TOK_TILE, Tp // TOK_TILE, final)
        proj_s = _inproj(xs, mod_s, norm_mix_g[l], w_in_b, l, tile_s, 1)
        mix_s, *caches_s = _sample_mixers(proj_s, attn_sinks[l], cache_conv, state_hgrn, ck, cv, conv_w[l],
                                          conv_b[l], conv_ln_g[l], conv_ln_b[l], hlb, hgrn_norm_g[l], bias_s, l,
                                          caches_s)
        xs = _out_mlp(mix_s, xs, mod_s, norm_mlp_g[l], w_out_b, w_up_b, w_down_b, final_g, l, tile_s, 1, final)
    cp, sp, kp, vp = caches_p
    cs, ss, ksn, vsn = caches_s
    return (xp.reshape(Bp, Tp, D_MODEL), xs.reshape(Bs, Ts, D_MODEL), cp, cs, sp, ss,
            kp.reshape(depth, Bp, WINDOW, KV_HEADS, HEAD_DIM), ksn.reshape(depth, Bs, w_buf, KV_HEADS, HEAD_DIM),
            vp.reshape(depth, Bp, WINDOW, KV_HEADS, HEAD_DIM), vsn.reshape(depth, Bs, w_buf, KV_HEADS, HEAD_DIM))
```

```python
import functools
import math

import jax
import jax.numpy as jnp
from jax import lax
from jax.experimental import pallas as pl
from jax.experimental.pallas import tpu as pltpu

F32 = jnp.float32
BF16 = jnp.bfloat16

D_MODEL = 1024
D_CONV = 256
CONV_WIDTH = 31
H_HGRN = 4
DK_HGRN = 128
DV_HGRN = 128
D_HGRN = 512
HEAD_DIM = 64
H_ATTN = 4
KV_HEADS = 2
GROUP = H_ATTN // KV_HEADS
D_ATTN = H_ATTN * HEAD_DIM
D_KV = KV_HEADS * HEAD_DIM
WINDOW = 128
ATTN_BLOCK = 128
NUM_BUCKETS = 32
MAX_DISTANCE = 128
D_FF = 4 * D_MODEL
N_MOD = 6
EPS = 1e-6

OFF_AVAL = 0
OFF_AGATE = OFF_AVAL + D_CONV
OFF_Q = OFF_AGATE + D_CONV
OFF_F = OFF_Q + H_HGRN * DK_HGRN
OFF_I = OFF_F + H_HGRN * DK_HGRN
OFF_G = OFF_I + D_HGRN
OFF_QA = OFF_G + D_HGRN
OFF_KA = OFF_QA + D_ATTN
OFF_VA = OFF_KA + D_KV
IN_WIDTH = OFF_VA + D_KV

HGRN_CHUNK = 64
HGRN_KEYBLOCK = 32
HGRN_SPAN = 4
SUBLANES = 8
CONV_PAD = 32
MIX_TILE = 512
TOK_TILE = 512
SAMPLE_BLOCK = 8
FF_CHUNK = 1024
VMEM_LIMIT = 56 * 1024 * 1024

NT_DIMS = (((1,), (1,)), ((), ()))
TN_DIMS = (((0,), (0,)), ((), ()))


def _silu(x):
    return x * jax.nn.sigmoid(x)


def _rms_rows(x):
    return x * lax.rsqrt(jnp.mean(x * x, axis=-1, keepdims=True) + EPS)


def _layer_lb(hlb, layer):
    m = jnp.max(hlb, axis=0, keepdims=True)
    e = jnp.exp(hlb - m)
    p = e / jnp.sum(e, axis=0, keepdims=True)
    lb = jnp.zeros_like(m)
    for i in range(1, layer + 1):
        lb = lb + p[i:i + 1, :]
    return lb


def _split3_bf16(x):
    hi = x.astype(BF16)
    r = x - hi.astype(F32)
    mid = r.astype(BF16)
    return hi, mid, (r - mid.astype(F32)).astype(BF16)


def _select_rows_mxu(sel, x):
    return sum(jnp.dot(sel, part, preferred_element_type=F32) for part in _split3_bf16(x))


def _cumsum_rows_small(g):
    row = lax.broadcasted_iota(jnp.int32, g.shape, 0)
    b = jnp.zeros_like(g)
    for u in range(g.shape[0]):
        b = b + jnp.where(row >= u, g[u:u + 1, :], 0.0)
    return b


def _hgrn_span(proj_ref, mix_ref, st_ref, row0, lb, hng, tri, tick):
    L, KB = HGRN_CHUNK, HGRN_KEYBLOCK
    span = HGRN_SPAN * L
    g, k = _hgrn_gates(proj_ref[pl.ds(row0, span), OFF_F:OFF_F + D_HGRN], lb)
    b = _select_rows_mxu(tri, g)
    units = [(c, h) for c in range(HGRN_SPAN) for h in range(H_HGRN)]

    ops = {}
    for c, h in units:
        rows = pl.ds(row0 + c * L, L)
        cs = slice(h * DK_HGRN, (h + 1) * DK_HGRN)
        q = proj_ref[rows, OFF_Q + h * DK_HGRN:OFF_Q + (h + 1) * DK_HGRN]
        v = proj_ref[rows, OFF_I + h * DV_HGRN:OFF_I + (h + 1) * DV_HGRN].astype(BF16)
        kk = k[c * L:(c + 1) * L, cs]
        bb = b[c * L:(c + 1) * L, cs]
        qp, kp = [], []
        for lo in range(0, L, KB):
            r = bb[lo + KB // 2 - 1:lo + KB // 2, :]
            kp.append((kk[lo:lo + KB] * jnp.exp(r - bb[lo:lo + KB])).astype(BF16))
            qp.append((q[lo:] * jnp.exp(bb[lo:] - r)).astype(BF16))
        bl = bb[L - 1:L, :]
        ops[c, h] = dict(qp=qp, kp=kp, v=v, qt=(q * jnp.exp(bb)).astype(BF16),
                         kst=(kk * jnp.exp(bl - bb)).astype(BF16), e=jnp.exp(bl))
    tick()

    for u in units:
        o = ops[u]
        o["p"] = [lax.dot_general(qp, kp, NT_DIMS, preferred_element_type=F32) for qp, kp in zip(o["qp"], o["kp"])]
        o["m"] = lax.dot_general(o["v"], o["kst"], TN_DIMS, preferred_element_type=F32)
    tick()

    for u in units:
        pm = []
        for p in ops[u]["p"]:
            row = lax.broadcasted_iota(jnp.int32, p.shape, 0)
            col = lax.broadcasted_iota(jnp.int32, p.shape, 1)
            pm.append(jnp.where(row >= col, p, 0.0).astype(BF16))
        ops[u]["p"] = pm
    tick()

    for u in units:
        o = ops[u]
        blocks = [None] * (L // KB)
        for j, p in enumerate(o["p"]):
            cj = jnp.dot(p, o["v"][j * KB:(j + 1) * KB], preferred_element_type=F32)
            for i in range(j, L // KB):
                piece = cj[(i - j) * KB:(i - j + 1) * KB]
                blocks[i] = piece if blocks[i] is None else blocks[i] + piece
        o["o"] = jnp.concatenate(blocks, axis=0)
    tick()

    for h in range(H_HGRN):
        st = st_ref[h]
        for c in range(HGRN_SPAN):
            o = ops[c, h]
            out = o["o"] + lax.dot_general(o["qt"], st.astype(BF16), NT_DIMS, preferred_element_type=F32)
            st = o["e"] * st + o["m"]
            rows = pl.ds(row0 + c * L, L)
            gate = proj_ref[rows, OFF_G + h * DV_HGRN:OFF_G + (h + 1) * DV_HGRN]
            mix_ref[rows, D_CONV + h * DV_HGRN:D_CONV + (h + 1) * DV_HGRN] = _hgrn_out(out, hng, gate).astype(BF16)
        st_ref[h] = st


def _sink_softmax(s, sink):
    m = jnp.maximum(jnp.max(s, axis=-1, keepdims=True), sink)
    p = jnp.exp(s - m)
    return p, jnp.sum(p, axis=-1, keepdims=True) + jnp.exp(sink - m)


def _bias_kernel(tab_ref, bp_ref, bs_ref, op_ref, os_ref, *, seq):
    bk = bp_ref[...]
    for h in range(H_ATTN):
        acc = jnp.full(bk.shape, -jnp.inf, F32)
        for bkt in range(NUM_BUCKETS):
            acc = jnp.where(bk == bkt, tab_ref[bkt, h], acc)
        op_ref[h] = acc
    bk = bs_ref[...]
    row = lax.broadcasted_iota(jnp.int32, bk.shape, 0)
    for kv in range(KV_HEADS):
        acc = jnp.full(bk.shape, -jnp.inf, F32)
        for bkt in range(NUM_BUCKETS):
            val = jnp.full(bk.shape, tab_ref[bkt, kv * GROUP], F32)
            for gi in range(1, GROUP):
                val = jnp.where(row >= gi * seq, tab_ref[bkt, kv * GROUP + gi], val)
            acc = jnp.where(bk == bkt, val, acc)
        os_ref[kv] = acc


def _t5_bucket(rel):
    n = jnp.maximum(rel, 0)
    max_exact = NUM_BUCKETS // 2
    nf = jnp.maximum(n, max_exact).astype(F32)
    large = max_exact + (jnp.log(nf / max_exact) / math.log(MAX_DISTANCE / max_exact)
                         * (NUM_BUCKETS - max_exact)).astype(jnp.int32)
    large = jnp.minimum(large, NUM_BUCKETS - 1)
    return jnp.where(n < max_exact, n, large)


def _bias_tables(rel_bias, dec_seq, w_buf):
    qi = jnp.arange(ATTN_BLOCK, dtype=jnp.int32)[:, None]
    kc = jnp.arange(2 * ATTN_BLOCK, dtype=jnp.int32)[None, :]
    rel_p = qi + ATTN_BLOCK - kc
    bucket_p = jnp.where((rel_p >= 0) & (rel_p <= WINDOW), _t5_bucket(rel_p), -1)
    ts = (jnp.arange(GROUP * dec_seq, dtype=jnp.int32) % dec_seq)[:, None]
    js = jnp.arange(2 * ATTN_BLOCK, dtype=jnp.int32)[None, :]
    rel_s = w_buf + ts - js
    ok_s = (rel_s >= 0) & (rel_s <= WINDOW) & (js < w_buf + dec_seq)
    bucket_s = jnp.where(ok_s, _t5_bucket(rel_s), -1)
    return pl.pallas_call(
        functools.partial(_bias_kernel, seq=dec_seq),
        out_shape=(jax.ShapeDtypeStruct((H_ATTN, ATTN_BLOCK, 2 * ATTN_BLOCK), F32),
                   jax.ShapeDtypeStruct((KV_HEADS, GROUP * dec_seq, 2 * ATTN_BLOCK), F32)),
        in_specs=[pl.BlockSpec(memory_space=pltpu.SMEM),
                  pl.BlockSpec(memory_space=pltpu.VMEM),
                  pl.BlockSpec(memory_space=pltpu.VMEM)],
        out_specs=(pl.BlockSpec(memory_space=pltpu.VMEM), pl.BlockSpec(memory_space=pltpu.VMEM)),
        name="rel_bias_tables",
    )(rel_bias.astype(F32), bucket_p, bucket_s)


def _mod_kernel(c_ref, w_ref, b_ref, o_ref):
    s = _silu(c_ref[...]).astype(BF16)
    o_ref[...] = jnp.dot(s, w_ref[...].astype(BF16), preferred_element_type=F32) + b_ref[...]


def _modulation(c_all, w_ada, b_ada):
    depth = w_ada.shape[0]
    n = c_all.shape[0]
    return pl.pallas_call(
        _mod_kernel,
        out_shape=jax.ShapeDtypeStruct((depth, n, N_MOD * D_MODEL), F32),
        grid=(depth, N_MOD),
        in_specs=[pl.BlockSpec((n, D_MODEL), lambda l, j: (0, 0)),
                  pl.BlockSpec((None, D_MODEL, D_MODEL), lambda l, j: (l, 0, j)),
                  pl.BlockSpec((None, 1, D_MODEL), lambda l, j: (l, 0, j))],
        out_specs=pl.BlockSpec((None, n, D_MODEL), lambda l, j: (l, 0, j)),
        compiler_params=pltpu.CompilerParams(dimension_semantics=("arbitrary", "arbitrary"),
                                             vmem_limit_bytes=VMEM_LIMIT),
        name="adaln_modulation",
    )(c_all, w_ada, b_ada.reshape(depth, 1, N_MOD * D_MODEL))


def _mod_spec(mod, chunk, tiles_per_batch):
    if mod.ndim == 3:
        return pl.BlockSpec((None, 1, D_MODEL), lambda i: (i // tiles_per_batch, 0, chunk))
    return pl.BlockSpec((mod.shape[0], D_MODEL), lambda i: (0, chunk))


def _mod_rows(m, n_tokens):
    if m.shape[0] == 1:
        return m
    reps = n_tokens // m.shape[0]
    tok = lax.broadcasted_iota(jnp.int32, (n_tokens, m.shape[0]), 0)
    bat = lax.broadcasted_iota(jnp.int32, (n_tokens, m.shape[0]), 1)
    sel = jnp.where((tok >= bat * reps) & (tok < (bat + 1) * reps), 1.0, 0.0).astype(BF16)
    return _select_rows_mxu(sel, m)


def _inproj_kernel(x_ref, sh_ref, sc_ref, g_ref, w_ref, o_ref):
    n = x_ref.shape[0]
    h = _rms_rows(x_ref[...]) * g_ref[...] * (1.0 + _mod_rows(sc_ref[...], n)) + _mod_rows(sh_ref[...], n)
    o_ref[...] = jnp.dot(h.astype(BF16), w_ref[...], preferred_element_type=F32)


def _inproj(x2, mod, norm_g, w_in, layer, tile, tiles_per_batch):
    n = x2.shape[0]
    return pl.pallas_call(
        _inproj_kernel,
        out_shape=jax.ShapeDtypeStruct((n, IN_WIDTH), F32),
        grid=(n // tile,),
        in_specs=[pl.BlockSpec((tile, D_MODEL), lambda i: (i, 0)),
                  _mod_spec(mod, 0, tiles_per_batch),
                  _mod_spec(mod, 1, tiles_per_batch),
                  pl.BlockSpec((1, D_MODEL), lambda i: (0, 0)),
                  pl.BlockSpec((None, D_MODEL, IN_WIDTH), lambda i: (layer, 0, 0), pipeline_mode=pl.Buffered(1))],
        out_specs=pl.BlockSpec((tile, IN_WIDTH), lambda i: (i, 0)),
        compiler_params=pltpu.CompilerParams(dimension_semantics=("arbitrary",), vmem_limit_bytes=VMEM_LIMIT),
        name="in_projection",
    )(x2, mod, mod, norm_g.reshape(1, D_MODEL), w_in)


def _mlp_kernel(mix_ref, x_ref, g1_ref, sh_ref, sc_ref, g2_ref, ng_ref, wout_ref, wup_ref, wdn_ref, fg_ref, o_ref, *,
                final):
    n = x_ref.shape[0]
    x1 = x_ref[...] + _mod_rows(g1_ref[...], n) * jnp.dot(mix_ref[...].astype(BF16), wout_ref[...],
                                                          preferred_element_type=F32)
    h = (_rms_rows(x1) * ng_ref[...] * (1.0 + _mod_rows(sc_ref[...], n)) + _mod_rows(sh_ref[...], n)).astype(BF16)
    acc = None
    for c in range(D_FF // FF_CHUNK):
        u = jnp.dot(h, wup_ref[:, c * FF_CHUNK:(c + 1) * FF_CHUNK], preferred_element_type=F32)
        u = jnp.square(jnp.maximum(u, 0.0)).astype(BF16)
        d = jnp.dot(u, wdn_ref[c * FF_CHUNK:(c + 1) * FF_CHUNK, :], preferred_element_type=F32)
        acc = d if acc is None else acc + d
    x2 = x1 + _mod_rows(g2_ref[...], n) * acc
    if final:
        x2 = _rms_rows(x2) * fg_ref[...]
    o_ref[...] = x2


def _out_mlp(mix2, x2, mod, norm_g, w_out, w_up, w_down, final_g, layer, tile, tiles_per_batch, final):
    n = x2.shape[0]
    const = lambda i: (0, 0)
    of_layer = lambda i: (layer, 0, 0)
    return pl.pallas_call(
        functools.partial(_mlp_kernel, final=final),
        out_shape=jax.ShapeDtypeStruct((n, D_MODEL), F32),
        grid=(n // tile,),
        in_specs=[pl.BlockSpec((tile, D_MODEL), lambda i: (i, 0)),
                  pl.BlockSpec((tile, D_MODEL), lambda i: (i, 0)),
                  _mod_spec(mod, 2, tiles_per_batch),
                  _mod_spec(mod, 3, tiles_per_batch),
                  _mod_spec(mod, 4, tiles_per_batch),
                  _mod_spec(mod, 5, tiles_per_batch),
                  pl.BlockSpec((1, D_MODEL), const),
                  pl.BlockSpec((None, D_MODEL, D_MODEL), of_layer, pipeline_mode=pl.Buffered(1)),
                  pl.BlockSpec((None, D_MODEL, D_FF), of_layer, pipeline_mode=pl.Buffered(1)),
                  pl.BlockSpec((None, D_FF, D_MODEL), of_layer, pipeline_mode=pl.Buffered(1)),
                  pl.BlockSpec((1, D_MODEL), const)],
        out_specs=pl.BlockSpec((tile, D_MODEL), lambda i: (i, 0)),
        compiler_params=pltpu.CompilerParams(dimension_semantics=("arbitrary",), vmem_limit_bytes=VMEM_LIMIT),
        name="out_projection_mlp",
    )(mix2, x2, mod, mod, mod, mod, norm_g.reshape(1, D_MODEL), w_out, w_up, w_down, final_g.reshape(1, D_MODEL))


def _conv_ln_swish(acc, lng, lnb):
    mu = jnp.mean(acc, axis=-1, keepdims=True)
    xc = acc - mu
    y = xc * lax.rsqrt(jnp.mean(xc * xc, axis=-1, keepdims=True) + EPS) * lng + lnb
    return _silu(y)


def _hgrn_gates(fh, lb):
    f = lb + (1.0 - lb) * jax.nn.sigmoid(fh)
    return jnp.log(f), 1.0 - f


def _hgrn_out(o, hng, gate):
    return _rms_rows(o) * hng * _silu(gate)


def _ticker(pieces):
    it = iter(pieces)

    def tick():
        piece = next(it, None)
        if piece is not None:
            piece()

    def flush():
        for piece in it:
            piece()

    tick.flush = flush
    return tick


def _prompt_mix_kernel(sinks_ref, x_ref, sh_ref, sc_ref, ng_ref, win_ref, convw_ref, convb_ref, lng_ref, lnb_ref,
                       hlb_ref, hng_ref, bias_ref,
                       mix_ref, convo_ref, so_ref, ko_ref, vo_ref,
                       proj_ref, abuf, kbuf, vbuf, st_ref, *, layer, tile):
    t = pl.program_id(1)
    last = pl.num_programs(1) - 1

    @pl.when(t == 0)
    def _():
        abuf[0:CONV_PAD, :] = jnp.zeros((CONV_PAD, D_CONV), F32)
        abuf[CONV_PAD + tile:CONV_PAD + tile + SUBLANES, :] = jnp.zeros((SUBLANES, D_CONV), F32)
        kbuf[0:ATTN_BLOCK, :] = jnp.zeros((ATTN_BLOCK, D_KV), BF16)
        vbuf[0:ATTN_BLOCK, :] = jnp.zeros((ATTN_BLOCK, D_KV), BF16)
        st_ref[...] = jnp.zeros(st_ref.shape, F32)

    h_in = (_rms_rows(x_ref[...]) * ng_ref[...] * (1.0 + sc_ref[...]) + sh_ref[...]).astype(BF16)
    for lo, hi in ((OFF_AVAL, OFF_Q), (OFF_F, OFF_I), (OFF_Q, OFF_F), (OFF_I, OFF_G), (OFF_G, OFF_QA),
                   (OFF_QA, IN_WIDTH)):
        proj_ref[:, lo:hi] = jnp.dot(h_in, win_ref[:, lo:hi], preferred_element_type=F32)

    kbuf[ATTN_BLOCK:ATTN_BLOCK + tile, :] = proj_ref[:, OFF_KA:OFF_KA + D_KV].astype(BF16)
    vbuf[ATTN_BLOCK:ATTN_BLOCK + tile, :] = proj_ref[:, OFF_VA:OFF_VA + D_KV].astype(BF16)
    scale = HEAD_DIM ** -0.5
    attn = {}

    def attn_scores(blk):
        def run():
            r0 = blk * ATTN_BLOCK
            for h in range(H_ATTN):
                kv = h // GROUP
                q = (proj_ref[r0:r0 + ATTN_BLOCK, OFF_QA + h * HEAD_DIM:OFF_QA + (h + 1) * HEAD_DIM]
                     * scale).astype(BF16)
                kall = kbuf[r0:r0 + 2 * ATTN_BLOCK, kv * HEAD_DIM:(kv + 1) * HEAD_DIM]
                attn[blk, h] = lax.dot_general(q, kall, NT_DIMS, preferred_element_type=F32)
        return run

    def attn_softmax(blk):
        def run():
            for h in range(H_ATTN):
                s = attn[blk, h] + bias_ref[h]
                if blk == 0:
                    col = lax.broadcasted_iota(jnp.int32, s.shape, 1)
                    s = jnp.where(col + (t * tile - ATTN_BLOCK) >= 0, s, -jnp.inf)
                p, den = _sink_softmax(s, sinks_ref[h])
                attn[blk, h] = (p.astype(BF16), den)
        return run

    def attn_values(blk):
        def run():
            r0 = blk * ATTN_BLOCK
            heads = []
            for h in range(H_ATTN):
                kv = h // GROUP
                p, den = attn[blk, h]
                vall = vbuf[r0:r0 + 2 * ATTN_BLOCK, kv * HEAD_DIM:(kv + 1) * HEAD_DIM]
                heads.append(jnp.dot(p, vall, preferred_element_type=F32) / den)
            mix_ref[r0:r0 + ATTN_BLOCK, D_CONV + D_HGRN:D_MODEL] = jnp.concatenate(heads, axis=1).astype(BF16)
        return run

    tick = _ticker([stage(blk) for blk in range(tile // ATTN_BLOCK)
                    for stage in (attn_scores, attn_softmax, attn_values)])

    abuf[CONV_PAD:CONV_PAD + tile, :] = (proj_ref[:, OFF_AVAL:OFF_AVAL + D_CONV]
                                         * jax.nn.sigmoid(proj_ref[:, OFF_AGATE:OFF_AGATE + D_CONV]))
    first_row = CONV_PAD - (CONV_WIDTH - 1)
    acc = jnp.broadcast_to(convb_ref[...], (tile, D_CONV))
    for r in range(SUBLANES):
        z = None
        for off in range(r, first_row + CONV_WIDTH, SUBLANES):
            j = off - first_row
            if j < 0:
                continue
            term = convw_ref[j:j + 1, :] * abuf[off - r:off - r + tile + SUBLANES, :]
            z = term if z is None else z + term
        acc = acc + (z[0:tile] if r == 0 else pltpu.roll(z, tile + SUBLANES - r, 0)[0:tile])
    mix_ref[:, 0:D_CONV] = _conv_ln_swish(acc, lng_ref[...], lnb_ref[...]).astype(BF16)
    tick()

    lb = _layer_lb(hlb_ref[...], layer)
    hng = hng_ref[...]
    span = HGRN_SPAN * HGRN_CHUNK
    ri = lax.broadcasted_iota(jnp.int32, (span, span), 0)
    ci = lax.broadcasted_iota(jnp.int32, (span, span), 1)
    tri = jnp.where((ri >= ci) & (ri // HGRN_CHUNK == ci // HGRN_CHUNK), 1.0, 0.0).astype(BF16)
    for i in range(tile // span):
        _hgrn_span(proj_ref, mix_ref, st_ref, i * span, lb, hng, tri, tick)
    tick.flush()

    @pl.when(t == last)
    def _():
        convo_ref[...] = abuf[CONV_PAD + tile - (CONV_WIDTH - 1):CONV_PAD + tile, :]
        for h in range(H_HGRN):
            so_ref[h] = st_ref[h].T
        ko_ref[...] = proj_ref[tile - WINDOW:tile, OFF_KA:OFF_KA + D_KV]
        vo_ref[...] = proj_ref[tile - WINDOW:tile, OFF_VA:OFF_VA + D_KV]

    abuf[0:CONV_PAD, :] = abuf[tile:tile + CONV_PAD, :]
    kbuf[0:ATTN_BLOCK, :] = kbuf[tile:tile + ATTN_BLOCK, :]
    vbuf[0:ATTN_BLOCK, :] = vbuf[tile:tile + ATTN_BLOCK, :]


def _carry_specs(carried):
    return [pl.BlockSpec(memory_space=pl.ANY)] * len(carried)


def _carried(kernel_fn, n_in, n_carried):
    if n_carried == 0:
        return kernel_fn
    return lambda *refs: kernel_fn(*refs[:n_in], *refs[n_in + n_carried:])


def _prompt_mixers(x, mod, norm_g, w_in, sinks, conv_w, conv_b, ln_g, ln_b, hgrn_lb, hng, bias_p, layer, carried):
    B, T = x.shape[:2]
    tile = MIX_TILE
    depth = hgrn_lb.shape[0]
    const2 = lambda b, t: (0, 0)
    inputs = (sinks, x, mod, mod, norm_g.reshape(1, D_MODEL), w_in, conv_w, conv_b.reshape(1, D_CONV),
              ln_g.reshape(1, D_CONV), ln_b.reshape(1, D_CONV), hgrn_lb, hng.reshape(1, DV_HGRN), bias_p)
    return pl.pallas_call(
        _carried(functools.partial(_prompt_mix_kernel, layer=layer, tile=tile), len(inputs), len(carried)),
        out_shape=(jax.ShapeDtypeStruct((B, T, D_MODEL), BF16),
                   jax.ShapeDtypeStruct((depth, B, CONV_WIDTH - 1, D_CONV), F32),
                   jax.ShapeDtypeStruct((depth, B, H_HGRN, DK_HGRN, DV_HGRN), F32),
                   jax.ShapeDtypeStruct((depth, B, WINDOW, D_KV), F32),
                   jax.ShapeDtypeStruct((depth, B, WINDOW, D_KV), F32)),
        grid=(B, T // tile),
        in_specs=[pl.BlockSpec(memory_space=pltpu.SMEM),
                  pl.BlockSpec((None, tile, D_MODEL), lambda b, t: (b, t, 0)),
                  pl.BlockSpec((None, 1, D_MODEL), lambda b, t: (b, 0, 0)),
                  pl.BlockSpec((None, 1, D_MODEL), lambda b, t: (b, 0, 1)),
                  pl.BlockSpec((1, D_MODEL), const2),
                  pl.BlockSpec((None, D_MODEL, IN_WIDTH), lambda b, t: (layer, 0, 0), pipeline_mode=pl.Buffered(1)),
                  pl.BlockSpec((CONV_WIDTH, D_CONV), const2),
                  pl.BlockSpec((1, D_CONV), const2),
                  pl.BlockSpec((1, D_CONV), const2),
                  pl.BlockSpec((1, D_CONV), const2),
                  pl.BlockSpec((depth, D_HGRN), const2),
                  pl.BlockSpec((1, DV_HGRN), const2),
                  pl.BlockSpec((H_ATTN, ATTN_BLOCK, 2 * ATTN_BLOCK), lambda b, t: (0, 0, 0))] + _carry_specs(carried),
        out_specs=(pl.BlockSpec((None, tile, D_MODEL), lambda b, t: (b, t, 0)),
                   pl.BlockSpec((None, None, CONV_WIDTH - 1, D_CONV), lambda b, t: (layer, b, 0, 0)),
                   pl.BlockSpec((None, None, H_HGRN, DK_HGRN, DV_HGRN), lambda b, t: (layer, b, 0, 0, 0)),
                   pl.BlockSpec((None, None, WINDOW, D_KV), lambda b, t: (layer, b, 0, 0)),
                   pl.BlockSpec((None, None, WINDOW, D_KV), lambda b, t: (layer, b, 0, 0))),
        input_output_aliases={len(inputs) + i: 1 + i for i in range(len(carried))},
        scratch_shapes=[pltpu.VMEM((tile, IN_WIDTH), F32),
                        pltpu.VMEM((CONV_PAD + tile + SUBLANES, D_CONV), F32),
                        pltpu.VMEM((ATTN_BLOCK + tile, D_KV), BF16),
                        pltpu.VMEM((ATTN_BLOCK + tile, D_KV), BF16),
                        pltpu.VMEM((H_HGRN, DV_HGRN, DK_HGRN), F32)],
        compiler_params=pltpu.CompilerParams(dimension_semantics=("arbitrary", "arbitrary"),
                                             vmem_limit_bytes=VMEM_LIMIT),
        name="prompt_mixers",
    )(*inputs, *carried)


def _sample_mix_kernel(sinks_ref, proj_ref, cconv_ref, state_ref, ck_ref, cv_ref, convw_ref, convb_ref, lng_ref,
                       lnb_ref, hlb_ref, hng_ref, bias_ref,
                       mix_ref, convo_ref, so_ref, ko_ref, vo_ref, full_ref, kpad_ref, vpad_ref, *,
                       layer, block, seq, w_buf):
    hist = CONV_WIDTH - 1

    @pl.when(pl.program_id(0) == 0)
    def _():
        for ref in (kpad_ref, vpad_ref):
            ref[:, 0:w_buf - seq, :] = jnp.zeros((block, w_buf - seq, D_KV), F32)

    lb = _layer_lb(hlb_ref[...], layer)
    hng = hng_ref[...]
    scale = HEAD_DIM ** -0.5
    elems = range(block)
    row8 = lax.broadcasted_iota(jnp.int32, (SUBLANES, DV_HGRN), 0)
    ones_rows = jnp.where((row8 >= seq) & (row8 < seq + 3), 1.0, 0.0)
    zrow = jnp.zeros((1, DK_HGRN), BF16)
    prow = lax.broadcasted_iota(jnp.int32, (seq, seq), 0)
    pcol = lax.broadcasted_iota(jnp.int32, (seq, seq), 1)
    grow = lax.broadcasted_iota(jnp.int32, (GROUP * seq, 1), 0)

    proj = [proj_ref[e * seq:(e + 1) * seq, :] for e in elems]

    out_a = []
    for e in elems:
        p = proj[e]
        full_ref[e, 0:hist, :] = cconv_ref[e]
        full_ref[e, hist:hist + seq, :] = (p[:, OFF_AVAL:OFF_AVAL + D_CONV]
                                           * jax.nn.sigmoid(p[:, OFF_AGATE:OFF_AGATE + D_CONV]))
        rows = [jnp.sum(convw_ref[...] * full_ref[e, s:s + CONV_WIDTH, :], axis=0, keepdims=True) for s in range(seq)]
        convo_ref[e] = full_ref[e, seq:seq + hist, :]
        out_a.append(_conv_ln_swish(jnp.concatenate(rows, axis=0) + convb_ref[...], lng_ref[...], lnb_ref[...]))

    units = [(e, h) for e in elems for h in range(H_HGRN)]
    ops = {}
    for e in elems:
        p = proj[e]
        g, k = _hgrn_gates(p[:, OFF_F:OFF_F + D_HGRN], lb)
        b = _cumsum_rows_small(g)
        for h in range(H_HGRN):
            cs = slice(h * DK_HGRN, (h + 1) * DK_HGRN)
            q = p[:, OFF_Q + h * DK_HGRN:OFF_Q + (h + 1) * DK_HGRN]
            v = p[:, OFF_I + h * DV_HGRN:OFF_I + (h + 1) * DV_HGRN]
            bb = b[:, cs]
            bl = bb[seq - 1:seq, :]
            kst = (k[:, cs] * jnp.exp(bl - bb)).astype(BF16)
            x = jnp.concatenate([kst.astype(F32), *(part.astype(F32) for part in _split3_bf16(jnp.exp(bl))),
                                 zrow.astype(F32)], axis=0).astype(BF16)
            vpad = jnp.concatenate([v, jnp.zeros((SUBLANES - seq, DV_HGRN), F32)], axis=0)
            ops[e, h] = dict(qp=(q * jnp.exp(bb - bl)).astype(BF16), kst=kst, v=v.astype(BF16),
                             qt=(q * jnp.exp(bb)).astype(BF16), x=x,
                             r=jnp.concatenate([vpad, ones_rows], axis=1).astype(BF16))
    for u in units:
        o = ops[u]
        st = state_ref[u[0], u[1]]
        o["p"] = lax.dot_general(o["qp"], o["kst"], NT_DIMS, preferred_element_type=F32)
        o["inter"] = jnp.dot(o["qt"], st.astype(BF16), preferred_element_type=F32)
        me = lax.dot_general(o["x"], o["r"], TN_DIMS, preferred_element_type=F32)
        so_ref[u[0], u[1]] = me[:, DV_HGRN:] * st + me[:, :DV_HGRN]
    out_b = {}
    for u in units:
        o = ops[u]
        pm = jnp.where(prow >= pcol, o["p"], 0.0).astype(BF16)
        out = o["inter"] + jnp.dot(pm, o["v"], preferred_element_type=F32)
        gate = proj[u[0]][:, OFF_G + u[1] * DV_HGRN:OFF_G + (u[1] + 1) * DV_HGRN]
        out_b[u] = _hgrn_out(out, hng, gate)

    scores = {}
    for e in elems:
        p = proj[e]
        for kv in range(KV_HEADS):
            hs = slice(kv * HEAD_DIM, (kv + 1) * HEAD_DIM)
            q2 = jnp.concatenate([p[:, OFF_QA + h * HEAD_DIM:OFF_QA + (h + 1) * HEAD_DIM]
                                  for h in range(kv * GROUP, (kv + 1) * GROUP)], axis=0)
            q2 = (q2 * scale).astype(BF16)
            bias = bias_ref[kv]
            s_c = jnp.dot(q2, ck_ref[e, hs, :].astype(BF16), preferred_element_type=F32) + bias[:, 0:w_buf]
            s_n = (lax.dot_general(q2, p[:, OFF_KA + kv * HEAD_DIM:OFF_KA + (kv + 1) * HEAD_DIM].astype(BF16), NT_DIMS,
                                   preferred_element_type=F32) + bias[:, w_buf:w_buf + seq])
            scores[e, kv] = (s_c, s_n)
    out_c = {}
    for e in elems:
        p = proj[e]
        for kv in range(KV_HEADS):
            hs = slice(kv * HEAD_DIM, (kv + 1) * HEAD_DIM)
            sink = jnp.zeros((GROUP * seq, 1), F32)
            for gi in range(GROUP):
                sink = jnp.where(grow >= gi * seq, sinks_ref[kv * GROUP + gi], sink)
            s_c, s_n = scores[e, kv]
            m = jnp.maximum(jnp.maximum(jnp.max(s_c, axis=-1, keepdims=True), jnp.max(s_n, axis=-1, keepdims=True)),
                            sink)
            p_c = jnp.exp(s_c - m)
            p_n = jnp.exp(s_n - m)
            den = jnp.sum(p_c, axis=-1, keepdims=True) + jnp.sum(p_n, axis=-1, keepdims=True) + jnp.exp(sink - m)
            vnew = p[:, OFF_VA + kv * HEAD_DIM:OFF_VA + (kv + 1) * HEAD_DIM].astype(BF16)
            o2 = (lax.dot_general(p_c.astype(BF16), cv_ref[e, hs, :].astype(BF16), NT_DIMS,
                                  preferred_element_type=F32)
                  + jnp.dot(p_n.astype(BF16), vnew, preferred_element_type=F32)) / den
            for gi in range(GROUP):
                out_c[e, kv * GROUP + gi] = o2[gi * seq:(gi + 1) * seq]
    lane = lax.broadcasted_iota(jnp.int32, (D_KV, w_buf), 1)
    for e in elems:
        p = proj[e]
        for pad, cache, new, out in ((kpad_ref, ck_ref, p[:, OFF_KA:OFF_KA + D_KV], ko_ref),
                                     (vpad_ref, cv_ref, p[:, OFF_VA:OFF_VA + D_KV], vo_ref)):
            pad[e, w_buf - seq:w_buf, :] = new
            out[e] = jnp.where(lane >= w_buf - seq, pad[e].T, pltpu.roll(cache[e], w_buf - seq, 1))

    for e in elems:
        parts = [out_a[e]] + [out_b[e, h] for h in range(H_HGRN)] + [out_c[e, h] for h in range(H_ATTN)]
        mix_ref[e * seq:(e + 1) * seq, :] = jnp.concatenate(parts, axis=1)


def _sample_mixers(proj2, sinks, cache_conv, state, cache_k, cache_v, conv_w, conv_b, ln_g, ln_b, hgrn_lb, hng,
                   bias_s, layer, carried):
    B = state.shape[1]
    seq = proj2.shape[0] // B
    w_buf = cache_k.shape[3]
    block = SAMPLE_BLOCK
    depth = hgrn_lb.shape[0]
    hist = CONV_WIDTH - 1
    const2 = lambda i: (0, 0)
    cache_specs = [pl.BlockSpec((None, block, hist, D_CONV), lambda i: (layer, i, 0, 0)),
                   pl.BlockSpec((None, block, H_HGRN, DK_HGRN, DV_HGRN), lambda i: (layer, i, 0, 0, 0)),
                   pl.BlockSpec((None, block, D_KV, w_buf), lambda i: (layer, i, 0, 0)),
                   pl.BlockSpec((None, block, D_KV, w_buf), lambda i: (layer, i, 0, 0))]
    inputs = (sinks, proj2, cache_conv, state, cache_k, cache_v, conv_w, conv_b.reshape(1, D_CONV),
              ln_g.reshape(1, D_CONV), ln_b.reshape(1, D_CONV), hgrn_lb, hng.reshape(1, DV_HGRN), bias_s)
    return pl.pallas_call(
        _carried(functools.partial(_sample_mix_kernel, layer=layer, block=block, seq=seq, w_buf=w_buf),
                 len(inputs), len(carried)),
        out_shape=(jax.ShapeDtypeStruct((B * seq, D_MODEL), F32),
                   jax.ShapeDtypeStruct((depth, B, hist, D_CONV), F32),
                   jax.ShapeDtypeStruct((depth, B, H_HGRN, DK_HGRN, DV_HGRN), F32),
                   jax.ShapeDtypeStruct((depth, B, D_KV, w_buf), F32),
                   jax.ShapeDtypeStruct((depth, B, D_KV, w_buf), F32)),
        grid=(B // block,),
        in_specs=[pl.BlockSpec(memory_space=pltpu.SMEM),
                  pl.BlockSpec((block * seq, IN_WIDTH), lambda i: (i, 0))] + cache_specs + [
                  pl.BlockSpec((CONV_WIDTH, D_CONV), const2),
                  pl.BlockSpec((1, D_CONV), const2),
                  pl.BlockSpec((1, D_CONV), const2),
                  pl.BlockSpec((1, D_CONV), const2),
                  pl.BlockSpec((depth, D_HGRN), const2),
                  pl.BlockSpec((1, DV_HGRN), const2),
                  pl.BlockSpec((KV_HEADS, GROUP * seq, 2 * ATTN_BLOCK), lambda i: (0, 0, 0))] + _carry_specs(carried),
        out_specs=tuple([pl.BlockSpec((block * seq, D_MODEL), lambda i: (i, 0))] + cache_specs),
        input_output_aliases={len(inputs) + i: 1 + i for i in range(len(carried))},
        scratch_shapes=[pltpu.VMEM((block, hist + seq + 6, D_CONV), F32),
                        pltpu.VMEM((block, w_buf, D_KV), F32),
                        pltpu.VMEM((block, w_buf, D_KV), F32)],
        compiler_params=pltpu.CompilerParams(dimension_semantics=("arbitrary",), vmem_limit_bytes=VMEM_LIMIT),
        name="sample_mixers",
    )(*inputs, *carried)


def kernel(x_prompt, x_sample, cache_conv, state_hgrn, cache_swa_k, cache_swa_v, c_prompt, c_sample, rel_bias, w_ada, b_ada, norm_mix_g, w_in, conv_w, conv_b, conv_ln_g, conv_ln_b, hgrn_lb, hgrn_norm_g, attn_sinks, w_out, norm_mlp_g, w_up, w_down, final_g):
    Bp, Tp = x_prompt.shape[:2]
    Bs, Ts = x_sample.shape[:2]
    depth = w_in.shape[0]
    w_buf = cache_swa_k.shape[2]
    assert Tp % MIX_TILE == 0 and (Bp * Tp) % TOK_TILE == 0 and Tp % TOK_TILE == 0 and Bs % SAMPLE_BLOCK == 0
    assert w_buf == WINDOW and GROUP * Ts == SUBLANES

    bias_p, bias_s = _bias_tables(rel_bias, Ts, w_buf)
    mod = _modulation(jnp.concatenate([c_prompt, c_sample], axis=0), w_ada, b_ada)
    w_in_b, w_out_b, w_up_b, w_down_b = (w.astype(BF16) for w in (w_in, w_out, w_up, w_down))
    hlb = hgrn_lb.astype(F32)
    ck = jnp.swapaxes(cache_swa_k.reshape(depth, Bs, w_buf, D_KV), 2, 3)
    cv = jnp.swapaxes(cache_swa_v.reshape(depth, Bs, w_buf, D_KV), 2, 3)

    xp = x_prompt.reshape(Bp * Tp, D_MODEL)
    xs = x_sample.reshape(Bs * Ts, D_MODEL)
    tile_s = Bs * Ts
    caches_p = ()
    caches_s = ()
    for l in range(depth):
        final = l == depth - 1
        mod_p = mod[l, :Bp].reshape(Bp, 1, N_MOD * D_MODEL)
        mod_s = mod[l, Bp:]
        mix_p, *caches_p = _prompt_mixers(xp.reshape(Bp, Tp, D_MODEL), mod_p, norm_mix_g[l], w_in_b, attn_sinks[l],
                                          conv_w[l], conv_b[l], conv_ln_g[l], conv_ln_b[l], hlb, hgrn_norm_g[l],
                                          bias_p, l, caches_p)
        xp = _out_mlp(mix_p.reshape(Bp * Tp, D_MODEL), xp, mod_p, norm_mlp_g[l], w_out_b, w_up_b, w_down_b,
                      final_g, l, TOK_TILE, Tp // TOK_TILE, final)
        proj_s = _inproj(xs, mod_s, norm_mix_g[l], w_in_b, l, tile_s, 1)
        mix_s, *caches_s = _sample_mixers(proj_s, attn_sinks[l], cache_conv, state_hgrn, ck, cv, conv_w[l],
                                          conv_b[l], conv_ln_g[l], conv_ln_b[l], hlb, hgrn_norm_g[l], bias_s, l,
                                          caches_s)
        xs = _out_mlp(mix_s, xs, mod_s, norm_mlp_g[l], w_out_b, w_up_b, w_down_b, final_g, l, tile_s, 1, final)
    cp, sp, kp, vp = caches_p
    cs, ss, ksn, vsn = caches_s
    ksn, vsn = jnp.swapaxes(ksn, 2, 3), jnp.swapaxes(vsn, 2, 3)
    return (xp.reshape(Bp, Tp, D_MODEL), xs.reshape(Bs, Ts, D_MODEL), cp, cs, sp, ss,
            kp.reshape(depth, Bp, WINDOW, KV_HEADS, HEAD_DIM), ksn.reshape(depth, Bs, w_buf, KV_HEADS, HEAD_DIM),
            vp.reshape(depth, Bp, WINDOW, KV_HEADS, HEAD_DIM), vsn.reshape(depth, Bs, w_buf, KV_HEADS, HEAD_DIM))
```

```python
import functools
import math

import jax
import jax.numpy as jnp
from jax import lax
from jax.experimental import pallas as pl
from jax.experimental.pallas import tpu as pltpu

F32 = jnp.float32
BF16 = jnp.bfloat16

D_MODEL = 1024
D_CONV = 256
CONV_WIDTH = 31
H_HGRN = 4
DK_HGRN = 128
DV_HGRN = 128
D_HGRN = 512
HEAD_DIM = 64
H_ATTN = 4
KV_HEADS = 2
GROUP = H_ATTN // KV_HEADS
D_ATTN = H_ATTN * HEAD_DIM
D_KV = KV_HEADS * HEAD_DIM
WINDOW = 128
ATTN_BLOCK = 128
NUM_BUCKETS = 32
MAX_DISTANCE = 128
D_FF = 4 * D_MODEL
N_MOD = 6
EPS = 1e-6

OFF_AVAL = 0
OFF_AGATE = OFF_AVAL + D_CONV
OFF_Q = OFF_AGATE + D_CONV
OFF_F = OFF_Q + H_HGRN * DK_HGRN
OFF_I = OFF_F + H_HGRN * DK_HGRN
OFF_G = OFF_I + D_HGRN
OFF_QA = OFF_G + D_HGRN
OFF_KA = OFF_QA + D_ATTN
OFF_VA = OFF_KA + D_KV
IN_WIDTH = OFF_VA + D_KV

HGRN_CHUNK = 64
HGRN_KEYBLOCK = 32
HGRN_SPAN = 4
SUBLANES = 8
CONV_PAD = 32
MIX_TILE = 512
TOK_TILE = 512
SAMPLE_BLOCK = 8
FF_CHUNK = 1024
VMEM_LIMIT = 56 * 1024 * 1024

NT_DIMS = (((1,), (1,)), ((), ()))
TN_DIMS = (((0,), (0,)), ((), ()))


def _silu(x):
    return x * jax.nn.sigmoid(x)


def _rms_rows(x):
    return x * lax.rsqrt(jnp.mean(x * x, axis=-1, keepdims=True) + EPS)


def _layer_lb(hlb, layer):
    m = jnp.max(hlb, axis=0, keepdims=True)
    e = jnp.exp(hlb - m)
    p = e / jnp.sum(e, axis=0, keepdims=True)
    lb = jnp.zeros_like(m)
    for i in range(1, layer + 1):
        lb = lb + p[i:i + 1, :]
    return lb


def _split3_bf16(x):
    hi = x.astype(BF16)
    r = x - hi.astype(F32)
    mid = r.astype(BF16)
    return hi, mid, (r - mid.astype(F32)).astype(BF16)


def _select_rows_mxu(sel, x):
    return sum(jnp.dot(sel, part, preferred_element_type=F32) for part in _split3_bf16(x))


def _cumsum_rows_small(g):
    row = lax.broadcasted_iota(jnp.int32, g.shape, 0)
    b = jnp.zeros_like(g)
    for u in range(g.shape[0]):
        b = b + jnp.where(row >= u, g[u:u + 1, :], 0.0)
    return b


def _hgrn_span(proj_ref, mix_ref, st_ref, row0, lb, hng, tri, tick):
    L, KB = HGRN_CHUNK, HGRN_KEYBLOCK
    span = HGRN_SPAN * L
    g, k = _hgrn_gates(proj_ref[pl.ds(row0, span), OFF_F:OFF_F + D_HGRN], lb)
    b = _select_rows_mxu(tri, g)
    units = [(c, h) for c in range(HGRN_SPAN) for h in range(H_HGRN)]

    ops = {}
    for c, h in units:
        rows = pl.ds(row0 + c * L, L)
        cs = slice(h * DK_HGRN, (h + 1) * DK_HGRN)
        q = proj_ref[rows, OFF_Q + h * DK_HGRN:OFF_Q + (h + 1) * DK_HGRN]
        v = proj_ref[rows, OFF_I + h * DV_HGRN:OFF_I + (h + 1) * DV_HGRN].astype(BF16)
        kk = k[c * L:(c + 1) * L, cs]
        bb = b[c * L:(c + 1) * L, cs]
        qp, kp = [], []
        for lo in range(0, L, KB):
            r = bb[lo + KB // 2 - 1:lo + KB // 2, :]
            kp.append((kk[lo:lo + KB] * jnp.exp(r - bb[lo:lo + KB])).astype(BF16))
            qp.append((q[lo:] * jnp.exp(bb[lo:] - r)).astype(BF16))
        bl = bb[L - 1:L, :]
        ops[c, h] = dict(qp=qp, kp=kp, v=v, qt=(q * jnp.exp(bb)).astype(BF16),
                         kst=(kk * jnp.exp(bl - bb)).astype(BF16), e=jnp.exp(bl))
    tick()

    for u in units:
        o = ops[u]
        o["p"] = [lax.dot_general(qp, kp, NT_DIMS, preferred_element_type=F32) for qp, kp in zip(o["qp"], o["kp"])]
        o["m"] = lax.dot_general(o["v"], o["kst"], TN_DIMS, preferred_element_type=F32)
    tick()

    for u in units:
        pm = []
        for p in ops[u]["p"]:
            row = lax.broadcasted_iota(jnp.int32, p.shape, 0)
            col = lax.broadcasted_iota(jnp.int32, p.shape, 1)
            pm.append(jnp.where(row >= col, p, 0.0).astype(BF16))
        ops[u]["p"] = pm
    tick()

    for u in units:
        o = ops[u]
        blocks = [None] * (L // KB)
        for j, p in enumerate(o["p"]):
            cj = jnp.dot(p, o["v"][j * KB:(j + 1) * KB], preferred_element_type=F32)
            for i in range(j, L // KB):
                piece = cj[(i - j) * KB:(i - j + 1) * KB]
                blocks[i] = piece if blocks[i] is None else blocks[i] + piece
        o["o"] = jnp.concatenate(blocks, axis=0)
    tick()

    for h in range(H_HGRN):
        st = st_ref[h]
        for c in range(HGRN_SPAN):
            o = ops[c, h]
            out = o["o"] + lax.dot_general(o["qt"], st.astype(BF16), NT_DIMS, preferred_element_type=F32)
            st = o["e"] * st + o["m"]
            rows = pl.ds(row0 + c * L, L)
            gate = proj_ref[rows, OFF_G + h * DV_HGRN:OFF_G + (h + 1) * DV_HGRN]
            mix_ref[rows, D_CONV + h * DV_HGRN:D_CONV + (h + 1) * DV_HGRN] = _hgrn_out(out, hng, gate).astype(BF16)
        st_ref[h] = st


def _sink_softmax(s, sink):
    m = jnp.maximum(jnp.max(s, axis=-1, keepdims=True), sink)
    p = jnp.exp(s - m)
    return p, jnp.sum(p, axis=-1, keepdims=True) + jnp.exp(sink - m)


def _bias_kernel(tab_ref, bp_ref, bs_ref, op_ref, os_ref, *, seq):
    bk = bp_ref[...]
    for h in range(H_ATTN):
        acc = jnp.full(bk.shape, -jnp.inf, F32)
        for bkt in range(NUM_BUCKETS):
            acc = jnp.where(bk == bkt, tab_ref[bkt, h], acc)
        op_ref[h] = acc
    bk = bs_ref[...]
    row = lax.broadcasted_iota(jnp.int32, bk.shape, 0)
    for kv in range(KV_HEADS):
        acc = jnp.full(bk.shape, -jnp.inf, F32)
        for bkt in range(NUM_BUCKETS):
            val = jnp.full(bk.shape, tab_ref[bkt, kv * GROUP], F32)
            for gi in range(1, GROUP):
                val = jnp.where(row >= gi * seq, tab_ref[bkt, kv * GROUP + gi], val)
            acc = jnp.where(bk == bkt, val, acc)
        os_ref[kv] = acc


def _t5_bucket(rel):
    n = jnp.maximum(rel, 0)
    max_exact = NUM_BUCKETS // 2
    nf = jnp.maximum(n, max_exact).astype(F32)
    large = max_exact + (jnp.log(nf / max_exact) / math.log(MAX_DISTANCE / max_exact)
                         * (NUM_BUCKETS - max_exact)).astype(jnp.int32)
    large = jnp.minimum(large, NUM_BUCKETS - 1)
    return jnp.where(n < max_exact, n, large)


def _bias_tables(rel_bias, dec_seq, w_buf):
    qi = jnp.arange(ATTN_BLOCK, dtype=jnp.int32)[:, None]
    kc = jnp.arange(2 * ATTN_BLOCK, dtype=jnp.int32)[None, :]
    rel_p = qi + ATTN_BLOCK - kc
    bucket_p = jnp.where((rel_p >= 0) & (rel_p <= WINDOW), _t5_bucket(rel_p), -1)
    ts = (jnp.arange(GROUP * dec_seq, dtype=jnp.int32) % dec_seq)[:, None]
    js = jnp.arange(2 * ATTN_BLOCK, dtype=jnp.int32)[None, :]
    rel_s = w_buf + ts - js
    ok_s = (rel_s >= 0) & (rel_s <= WINDOW) & (js < w_buf + dec_seq)
    bucket_s = jnp.where(ok_s, _t5_bucket(rel_s), -1)
    return pl.pallas_call(
        functools.partial(_bias_kernel, seq=dec_seq),
        out_shape=(jax.ShapeDtypeStruct((H_ATTN, ATTN_BLOCK, 2 * ATTN_BLOCK), F32),
                   jax.ShapeDtypeStruct((KV_HEADS, GROUP * dec_seq, 2 * ATTN_BLOCK), F32)),
        in_specs=[pl.BlockSpec(memory_space=pltpu.SMEM),
                  pl.BlockSpec(memory_space=pltpu.VMEM),
                  pl.BlockSpec(memory_space=pltpu.VMEM)],
        out_specs=(pl.BlockSpec(memory_space=pltpu.VMEM), pl.BlockSpec(memory_space=pltpu.VMEM)),
        name="rel_bias_tables",
    )(rel_bias.astype(F32), bucket_p, bucket_s)


def _mod_kernel(c_ref, w_ref, b_ref, o_ref):
    s = _silu(c_ref[...]).astype(BF16)
    o_ref[...] = jnp.dot(s, w_ref[...].astype(BF16), preferred_element_type=F32) + b_ref[...]


def _modulation(c_all, w_ada, b_ada):
    depth = w_ada.shape[0]
    n = c_all.shape[0]
    return pl.pallas_call(
        _mod_kernel,
        out_shape=jax.ShapeDtypeStruct((depth, n, N_MOD * D_MODEL), F32),
        grid=(depth, N_MOD),
        in_specs=[pl.BlockSpec((n, D_MODEL), lambda l, j: (0, 0)),
                  pl.BlockSpec((None, D_MODEL, D_MODEL), lambda l, j: (l, 0, j)),
                  pl.BlockSpec((None, 1, D_MODEL), lambda l, j: (l, 0, j))],
        out_specs=pl.BlockSpec((None, n, D_MODEL), lambda l, j: (l, 0, j)),
        compiler_params=pltpu.CompilerParams(dimension_semantics=("arbitrary", "arbitrary"),
                                             vmem_limit_bytes=VMEM_LIMIT),
        name="adaln_modulation",
    )(c_all, w_ada, b_ada.reshape(depth, 1, N_MOD * D_MODEL))


def _mod_spec(mod, chunk, tiles_per_batch):
    if mod.ndim == 3:
        return pl.BlockSpec((None, 1, D_MODEL), lambda i: (i // tiles_per_batch, 0, chunk))
    return pl.BlockSpec((mod.shape[0], D_MODEL), lambda i: (0, chunk))


def _mod_rows(m, n_tokens):
    if m.shape[0] == 1:
        return m
    reps = n_tokens // m.shape[0]
    tok = lax.broadcasted_iota(jnp.int32, (n_tokens, m.shape[0]), 0)
    bat = lax.broadcasted_iota(jnp.int32, (n_tokens, m.shape[0]), 1)
    sel = jnp.where((tok >= bat * reps) & (tok < (bat + 1) * reps), 1.0, 0.0).astype(BF16)
    return _select_rows_mxu(sel, m)


def _inproj_kernel(x_ref, sh_ref, sc_ref, g_ref, w_ref, o_ref):
    n = x_ref.shape[0]
    h = _rms_rows(x_ref[...]) * g_ref[...] * (1.0 + _mod_rows(sc_ref[...], n)) + _mod_rows(sh_ref[...], n)
    o_ref[...] = jnp.dot(h.astype(BF16), w_ref[...], preferred_element_type=F32)


def _inproj(x2, mod, norm_g, w_in, layer, tile, tiles_per_batch):
    n = x2.shape[0]
    return pl.pallas_call(
        _inproj_kernel,
        out_shape=jax.ShapeDtypeStruct((n, IN_WIDTH), F32),
        grid=(n // tile,),
        in_specs=[pl.BlockSpec((tile, D_MODEL), lambda i: (i, 0)),
                  _mod_spec(mod, 0, tiles_per_batch),
                  _mod_spec(mod, 1, tiles_per_batch),
                  pl.BlockSpec((1, D_MODEL), lambda i: (0, 0)),
                  pl.BlockSpec((None, D_MODEL, IN_WIDTH), lambda i: (layer, 0, 0), pipeline_mode=pl.Buffered(1))],
        out_specs=pl.BlockSpec((tile, IN_WIDTH), lambda i: (i, 0)),
        compiler_params=pltpu.CompilerParams(dimension_semantics=("arbitrary",), vmem_limit_bytes=VMEM_LIMIT),
        name="in_projection",
    )(x2, mod, mod, norm_g.reshape(1, D_MODEL), w_in)


def _mlp_kernel(mix_ref, x_ref, g1_ref, sh_ref, sc_ref, g2_ref, ng_ref, wout_ref, wup_ref, wdn_ref, fg_ref, o_ref, *,
                final):
    n = x_ref.shape[0]
    x1 = x_ref[...] + _mod_rows(g1_ref[...], n) * jnp.dot(mix_ref[...].astype(BF16), wout_ref[...],
                                                          preferred_element_type=F32)
    h = (_rms_rows(x1) * ng_ref[...] * (1.0 + _mod_rows(sc_ref[...], n)) + _mod_rows(sh_ref[...], n)).astype(BF16)
    acc = None
    for c in range(D_FF // FF_CHUNK):
        u = jnp.dot(h, wup_ref[:, c * FF_CHUNK:(c + 1) * FF_CHUNK], preferred_element_type=F32)
        u = jnp.square(jnp.maximum(u, 0.0)).astype(BF16)
        d = jnp.dot(u, wdn_ref[c * FF_CHUNK:(c + 1) * FF_CHUNK, :], preferred_element_type=F32)
        acc = d if acc is None else acc + d
    x2 = x1 + _mod_rows(g2_ref[...], n) * acc
    if final:
        x2 = _rms_rows(x2) * fg_ref[...]
    o_ref[...] = x2


def _out_mlp(mix2, x2, mod, norm_g, w_out, w_up, w_down, final_g, layer, tile, tiles_per_batch, final):
    n = x2.shape[0]
    const = lambda i: (0, 0)
    of_layer = lambda i: (layer, 0, 0)
    return pl.pallas_call(
        functools.partial(_mlp_kernel, final=final),
        out_shape=jax.ShapeDtypeStruct((n, D_MODEL), F32),
        grid=(n // tile,),
        in_specs=[pl.BlockSpec((tile, D_MODEL), lambda i: (i, 0)),
                  pl.BlockSpec((tile, D_MODEL), lambda i: (i, 0)),
                  _mod_spec(mod, 2, tiles_per_batch),
                  _mod_spec(mod, 3, tiles_per_batch),
                  _mod_spec(mod, 4, tiles_per_batch),
                  _mod_spec(mod, 5, tiles_per_batch),
                  pl.BlockSpec((1, D_MODEL), const),
                  pl.BlockSpec((None, D_MODEL, D_MODEL), of_layer, pipeline_mode=pl.Buffered(1)),
                  pl.BlockSpec((None, D_MODEL, D_FF), of_layer, pipeline_mode=pl.Buffered(1)),
                  pl.BlockSpec((None, D_FF, D_MODEL), of_layer, pipeline_mode=pl.Buffered(1)),
                  pl.BlockSpec((1, D_MODEL), const)],
        out_specs=pl.BlockSpec((tile, D_MODEL), lambda i: (i, 0)),
        compiler_params=pltpu.CompilerParams(dimension_semantics=("arbitrary",), vmem_limit_bytes=VMEM_LIMIT),
        name="out_projection_mlp",
    )(mix2, x2, mod, mod, mod, mod, norm_g.reshape(1, D_MODEL), w_out, w_up, w_down, final_g.reshape(1, D_MODEL))


def _conv_ln_swish(acc, lng, lnb):
    mu = jnp.mean(acc, axis=-1, keepdims=True)
    xc = acc - mu
    y = xc * lax.rsqrt(jnp.mean(xc * xc, axis=-1, keepdims=True) + EPS) * lng + lnb
    return _silu(y)


def _hgrn_gates(fh, lb):
    f = lb + (1.0 - lb) * jax.nn.sigmoid(fh)
    return jnp.log(f), 1.0 - f


def _hgrn_out(o, hng, gate):
    return _rms_rows(o) * hng * _silu(gate)


def _ticker(pieces):
    it = iter(pieces)

    def tick():
        piece = next(it, None)
        if piece is not None:
            piece()

    def flush():
        for piece in it:
            piece()

    tick.flush = flush
    return tick


def _prompt_mix_kernel(sinks_ref, x_ref, sh_ref, sc_ref, ng_ref, win_ref, convw_ref, convb_ref, lng_ref, lnb_ref,
                       hlb_ref, hng_ref, bias_ref,
                       mix_ref, convo_ref, so_ref, ko_ref, vo_ref,
                       proj_ref, abuf, kbuf, vbuf, st_ref, *, layer, tile):
    t = pl.program_id(1)
    last = pl.num_programs(1) - 1

    @pl.when(t == 0)
    def _():
        abuf[0:CONV_PAD, :] = jnp.zeros((CONV_PAD, D_CONV), F32)
        abuf[CONV_PAD + tile:CONV_PAD + tile + SUBLANES, :] = jnp.zeros((SUBLANES, D_CONV), F32)
        kbuf[0:ATTN_BLOCK, :] = jnp.zeros((ATTN_BLOCK, D_KV), BF16)
        vbuf[0:ATTN_BLOCK, :] = jnp.zeros((ATTN_BLOCK, D_KV), BF16)
        st_ref[...] = jnp.zeros(st_ref.shape, F32)

    h_in = (_rms_rows(x_ref[...]) * ng_ref[...] * (1.0 + sc_ref[...]) + sh_ref[...]).astype(BF16)
    for lo, hi in ((OFF_AVAL, OFF_Q), (OFF_F, OFF_I), (OFF_Q, OFF_F), (OFF_I, OFF_G), (OFF_G, OFF_QA),
                   (OFF_QA, IN_WIDTH)):
        proj_ref[:, lo:hi] = jnp.dot(h_in, win_ref[:, lo:hi], preferred_element_type=F32)

    kbuf[ATTN_BLOCK:ATTN_BLOCK + tile, :] = proj_ref[:, OFF_KA:OFF_KA + D_KV].astype(BF16)
    vbuf[ATTN_BLOCK:ATTN_BLOCK + tile, :] = proj_ref[:, OFF_VA:OFF_VA + D_KV].astype(BF16)
    scale = HEAD_DIM ** -0.5
    attn = {}

    def attn_scores(blk):
        def run():
            r0 = blk * ATTN_BLOCK
            for h in range(H_ATTN):
                kv = h // GROUP
                q = (proj_ref[r0:r0 + ATTN_BLOCK, OFF_QA + h * HEAD_DIM:OFF_QA + (h + 1) * HEAD_DIM]
                     * scale).astype(BF16)
                kall = kbuf[r0:r0 + 2 * ATTN_BLOCK, kv * HEAD_DIM:(kv + 1) * HEAD_DIM]
                attn[blk, h] = lax.dot_general(q, kall, NT_DIMS, preferred_element_type=F32)
        return run

    def attn_softmax(blk):
        def run():
            for h in range(H_ATTN):
                s = attn[blk, h] + bias_ref[h]
                if blk == 0:
                    col = lax.broadcasted_iota(jnp.int32, s.shape, 1)
                    s = jnp.where(col + (t * tile - ATTN_BLOCK) >= 0, s, -jnp.inf)
                p, den = _sink_softmax(s, sinks_ref[h])
                attn[blk, h] = (p.astype(BF16), den)
        return run

    def attn_values(blk):
        def run():
            r0 = blk * ATTN_BLOCK
            heads = []
            for h in range(H_ATTN):
                kv = h // GROUP
                p, den = attn[blk, h]
                vall = vbuf[r0:r0 + 2 * ATTN_BLOCK, kv * HEAD_DIM:(kv + 1) * HEAD_DIM]
                heads.append(jnp.dot(p, vall, preferred_element_type=F32) / den)
            mix_ref[r0:r0 + ATTN_BLOCK, D_CONV + D_HGRN:D_MODEL] = jnp.concatenate(heads, axis=1).astype(BF16)
        return run

    tick = _ticker([stage(blk) for blk in range(tile // ATTN_BLOCK)
                    for stage in (attn_scores, attn_softmax, attn_values)])

    abuf[CONV_PAD:CONV_PAD + tile, :] = (proj_ref[:, OFF_AVAL:OFF_AVAL + D_CONV]
                                         * jax.nn.sigmoid(proj_ref[:, OFF_AGATE:OFF_AGATE + D_CONV]))
    first_row = CONV_PAD - (CONV_WIDTH - 1)
    acc = jnp.broadcast_to(convb_ref[...], (tile, D_CONV))
    for r in range(SUBLANES):
        z = None
        for off in range(r, first_row + CONV_WIDTH, SUBLANES):
            j = off - first_row
            if j < 0:
                continue
            term = convw_ref[j:j + 1, :] * abuf[off - r:off - r + tile + SUBLANES, :]
            z = term if z is None else z + term
        acc = acc + (z[0:tile] if r == 0 else pltpu.roll(z, tile + SUBLANES - r, 0)[0:tile])
    mix_ref[:, 0:D_CONV] = _conv_ln_swish(acc, lng_ref[...], lnb_ref[...]).astype(BF16)
    tick()

    lb = _layer_lb(hlb_ref[...], layer)
    hng = hng_ref[...]
    span = HGRN_SPAN * HGRN_CHUNK
    ri = lax.broadcasted_iota(jnp.int32, (span, span), 0)
    ci = lax.broadcasted_iota(jnp.int32, (span, span), 1)
    tri = jnp.where((ri >= ci) & (ri // HGRN_CHUNK == ci // HGRN_CHUNK), 1.0, 0.0).astype(BF16)
    for i in range(tile // span):
        _hgrn_span(proj_ref, mix_ref, st_ref, i * span, lb, hng, tri, tick)
    tick.flush()

    @pl.when(t == last)
    def _():
        convo_ref[...] = abuf[CONV_PAD + tile - (CONV_WIDTH - 1):CONV_PAD + tile, :]
        for h in range(H_HGRN):
            so_ref[h] = st_ref[h].T
        ko_ref[...] = proj_ref[tile - WINDOW:tile, OFF_KA:OFF_KA + D_KV]
        vo_ref[...] = proj_ref[tile - WINDOW:tile, OFF_VA:OFF_VA + D_KV]

    abuf[0:CONV_PAD, :] = abuf[tile:tile + CONV_PAD, :]
    kbuf[0:ATTN_BLOCK, :] = kbuf[tile:tile + ATTN_BLOCK, :]
    vbuf[0:ATTN_BLOCK, :] = vbuf[tile:tile + ATTN_BLOCK, :]


def _carry_specs(carried):
    return [pl.BlockSpec(memory_space=pl.ANY)] * len(carried)


def _carried(kernel_fn, n_in, n_carried):
    if n_carried == 0:
        return kernel_fn
    return lambda *refs: kernel_fn(*refs[:n_in], *refs[n_in + n_carried:])


def _prompt_mixers(x, mod, norm_g, w_in, sinks, conv_w, conv_b, ln_g, ln_b, hgrn_lb, hng, bias_p, layer, carried):
    B, T = x.shape[:2]
    tile = MIX_TILE
    depth = hgrn_lb.shape[0]
    const2 = lambda b, t: (0, 0)
    inputs = (sinks, x, mod, mod, norm_g.reshape(1, D_MODEL), w_in, conv_w, conv_b.reshape(1, D_CONV),
              ln_g.reshape(1, D_CONV), ln_b.reshape(1, D_CONV), hgrn_lb, hng.reshape(1, DV_HGRN), bias_p)
    return pl.pallas_call(
        _carried(functools.partial(_prompt_mix_kernel, layer=layer, tile=tile), len(inputs), len(carried)),
        out_shape=(jax.ShapeDtypeStruct((B, T, D_MODEL), BF16),
                   jax.ShapeDtypeStruct((depth, B, CONV_WIDTH - 1, D_CONV), F32),
                   jax.ShapeDtypeStruct((depth, B, H_HGRN, DK_HGRN, DV_HGRN), F32),
                   jax.ShapeDtypeStruct((depth, B, WINDOW, D_KV), F32),
                   jax.ShapeDtypeStruct((depth, B, WINDOW, D_KV), F32)),
        grid=(B, T // tile),
        in_specs=[pl.BlockSpec(memory_space=pltpu.SMEM),
                  pl.BlockSpec((None, tile, D_MODEL), lambda b, t: (b, t, 0)),
                  pl.BlockSpec((None, 1, D_MODEL), lambda b, t: (b, 0, 0)),
                  pl.BlockSpec((None, 1, D_MODEL), lambda b, t: (b, 0, 1)),
                  pl.BlockSpec((1, D_MODEL), const2),
                  pl.BlockSpec((None, D_MODEL, IN_WIDTH), lambda b, t: (layer, 0, 0), pipeline_mode=pl.Buffered(1)),
                  pl.BlockSpec((CONV_WIDTH, D_CONV), const2),
                  pl.BlockSpec((1, D_CONV), const2),
                  pl.BlockSpec((1, D_CONV), const2),
                  pl.BlockSpec((1, D_CONV), const2),
                  pl.BlockSpec((depth, D_HGRN), const2),
                  pl.BlockSpec((1, DV_HGRN), const2),
                  pl.BlockSpec((H_ATTN, ATTN_BLOCK, 2 * ATTN_BLOCK), lambda b, t: (0, 0, 0))] + _carry_specs(carried),
        out_specs=(pl.BlockSpec((None, tile, D_MODEL), lambda b, t: (b, t, 0)),
                   pl.BlockSpec((None, None, CONV_WIDTH - 1, D_CONV), lambda b, t: (layer, b, 0, 0)),
                   pl.BlockSpec((None, None, H_HGRN, DK_HGRN, DV_HGRN), lambda b, t: (layer, b, 0, 0, 0)),
                   pl.BlockSpec((None, None, WINDOW, D_KV), lambda b, t: (layer, b, 0, 0)),
                   pl.BlockSpec((None, None, WINDOW, D_KV), lambda b, t: (layer, b, 0, 0))),
        input_output_aliases={len(inputs) + i: 1 + i for i in range(len(carried))},
        scratch_shapes=[pltpu.VMEM((tile, IN_WIDTH), F32),
                        pltpu.VMEM((CONV_PAD + tile + SUBLANES, D_CONV), F32),
                        pltpu.VMEM((ATTN_BLOCK + tile, D_KV), BF16),
                        pltpu.VMEM((ATTN_BLOCK + tile, D_KV), BF16),
                        pltpu.VMEM((H_HGRN, DV_HGRN, DK_HGRN), F32)],
        compiler_params=pltpu.CompilerParams(dimension_semantics=("arbitrary", "arbitrary"),
                                             vmem_limit_bytes=VMEM_LIMIT),
        name="prompt_mixers",
    )(*inputs, *carried)


def _sample_mix_kernel(sinks_ref, proj_ref, cconv_ref, state_ref, ck_ref, cv_ref, convw_ref, convb_ref, lng_ref,
                       lnb_ref, hlb_ref, hng_ref, bias_ref,
                       mix_ref, convo_ref, so_ref, ko_ref, vo_ref, kpad_ref, vpad_ref, *,
                       layer, block, seq, w_buf):
    hist = CONV_WIDTH - 1

    @pl.when(pl.program_id(0) == 0)
    def _():
        for ref in (kpad_ref, vpad_ref):
            ref[:, 0:w_buf - seq, :] = jnp.zeros((block, w_buf - seq, D_KV), F32)

    lb = _layer_lb(hlb_ref[...], layer)
    hng = hng_ref[...]
    scale = HEAD_DIM ** -0.5
    elems = range(block)
    row8 = lax.broadcasted_iota(jnp.int32, (SUBLANES, DV_HGRN), 0)
    ones_rows = jnp.where((row8 >= seq) & (row8 < seq + 3), 1.0, 0.0)
    zrow = jnp.zeros((1, DK_HGRN), BF16)
    prow = lax.broadcasted_iota(jnp.int32, (seq, seq), 0)
    pcol = lax.broadcasted_iota(jnp.int32, (seq, seq), 1)
    grow = lax.broadcasted_iota(jnp.int32, (GROUP * seq, 1), 0)

    proj = [proj_ref[e * seq:(e + 1) * seq, :] for e in elems]

    glu = [p[:, OFF_AVAL:OFF_AVAL + D_CONV] * jax.nn.sigmoid(p[:, OFF_AGATE:OFF_AGATE + D_CONV]) for p in proj]
    full = [cconv_ref[i] for i in range(hist)]
    full += [jnp.concatenate([glu[e][t:t + 1] for e in elems], axis=0) for t in range(seq)]
    for i in range(hist):
        convo_ref[i] = full[i + seq]
    conv_out = []
    for t in range(seq):
        acc = jnp.broadcast_to(convb_ref[...], (block, D_CONV))
        for j in range(CONV_WIDTH):
            acc = acc + convw_ref[j:j + 1, :] * full[t + j]
        conv_out.append(_conv_ln_swish(acc, lng_ref[...], lnb_ref[...]))
    out_a = [jnp.concatenate([conv_out[t][e:e + 1] for t in range(seq)], axis=0) for e in elems]

    units = [(e, h) for e in elems for h in range(H_HGRN)]
    ops = {}
    for e in elems:
        p = proj[e]
        g, k = _hgrn_gates(p[:, OFF_F:OFF_F + D_HGRN], lb)
        b = _cumsum_rows_small(g)
        for h in range(H_HGRN):
            cs = slice(h * DK_HGRN, (h + 1) * DK_HGRN)
            q = p[:, OFF_Q + h * DK_HGRN:OFF_Q + (h + 1) * DK_HGRN]
            v = p[:, OFF_I + h * DV_HGRN:OFF_I + (h + 1) * DV_HGRN]
            bb = b[:, cs]
            bl = bb[seq - 1:seq, :]
            kst = (k[:, cs] * jnp.exp(bl - bb)).astype(BF16)
            x = jnp.concatenate([kst.astype(F32), *(part.astype(F32) for part in _split3_bf16(jnp.exp(bl))),
                                 zrow.astype(F32)], axis=0).astype(BF16)
            vpad = jnp.concatenate([v, jnp.zeros((SUBLANES - seq, DV_HGRN), F32)], axis=0)
            ops[e, h] = dict(qp=(q * jnp.exp(bb - bl)).astype(BF16), kst=kst, v=v.astype(BF16),
                             qt=(q * jnp.exp(bb)).astype(BF16), x=x,
                             r=jnp.concatenate([vpad, ones_rows], axis=1).astype(BF16))
    for u in units:
        o = ops[u]
        st = state_ref[u[0], u[1]]
        o["p"] = lax.dot_general(o["qp"], o["kst"], NT_DIMS, preferred_element_type=F32)
        o["inter"] = jnp.dot(o["qt"], st.astype(BF16), preferred_element_type=F32)
        me = lax.dot_general(o["x"], o["r"], TN_DIMS, preferred_element_type=F32)
        so_ref[u[0], u[1]] = me[:, DV_HGRN:] * st + me[:, :DV_HGRN]
    out_b = {}
    for u in units:
        o = ops[u]
        pm = jnp.where(prow >= pcol, o["p"], 0.0).astype(BF16)
        out = o["inter"] + jnp.dot(pm, o["v"], preferred_element_type=F32)
        gate = proj[u[0]][:, OFF_G + u[1] * DV_HGRN:OFF_G + (u[1] + 1) * DV_HGRN]
        out_b[u] = _hgrn_out(out, hng, gate)

    scores = {}
    for e in elems:
        p = proj[e]
        for kv in range(KV_HEADS):
            hs = slice(kv * HEAD_DIM, (kv + 1) * HEAD_DIM)
            q2 = jnp.concatenate([p[:, OFF_QA + h * HEAD_DIM:OFF_QA + (h + 1) * HEAD_DIM]
                                  for h in range(kv * GROUP, (kv + 1) * GROUP)], axis=0)
            q2 = (q2 * scale).astype(BF16)
            bias = bias_ref[kv]
            s_c = jnp.dot(q2, ck_ref[e, hs, :].astype(BF16), preferred_element_type=F32) + bias[:, 0:w_buf]
            s_n = (lax.dot_general(q2, p[:, OFF_KA + kv * HEAD_DIM:OFF_KA + (kv + 1) * HEAD_DIM].astype(BF16), NT_DIMS,
                                   preferred_element_type=F32) + bias[:, w_buf:w_buf + seq])
            scores[e, kv] = (s_c, s_n)
    out_c = {}
    for e in elems:
        p = proj[e]
        for kv in range(KV_HEADS):
            hs = slice(kv * HEAD_DIM, (kv + 1) * HEAD_DIM)
            sink = jnp.zeros((GROUP * seq, 1), F32)
            for gi in range(GROUP):
                sink = jnp.where(grow >= gi * seq, sinks_ref[kv * GROUP + gi], sink)
            s_c, s_n = scores[e, kv]
            m = jnp.maximum(jnp.maximum(jnp.max(s_c, axis=-1, keepdims=True), jnp.max(s_n, axis=-1, keepdims=True)),
                            sink)
            p_c = jnp.exp(s_c - m)
            p_n = jnp.exp(s_n - m)
            den = jnp.sum(p_c, axis=-1, keepdims=True) + jnp.sum(p_n, axis=-1, keepdims=True) + jnp.exp(sink - m)
            vnew = p[:, OFF_VA + kv * HEAD_DIM:OFF_VA + (kv + 1) * HEAD_DIM].astype(BF16)
            o2 = (lax.dot_general(p_c.astype(BF16), cv_ref[e, hs, :].astype(BF16), NT_DIMS,
                                  preferred_element_type=F32)
                  + jnp.dot(p_n.astype(BF16), vnew, preferred_element_type=F32)) / den
            for gi in range(GROUP):
                out_c[e, kv * GROUP + gi] = o2[gi * seq:(gi + 1) * seq]
    lane = lax.broadcasted_iota(jnp.int32, (D_KV, w_buf), 1)
    for e in elems:
        p = proj[e]
        for pad, cache, new, out in ((kpad_ref, ck_ref, p[:, OFF_KA:OFF_KA + D_KV], ko_ref),
                                     (vpad_ref, cv_ref, p[:, OFF_VA:OFF_VA + D_KV], vo_ref)):
            pad[e, w_buf - seq:w_buf, :] = new
            out[e] = jnp.where(lane >= w_buf - seq, pad[e].T, pltpu.roll(cache[e], w_buf - seq, 1))

    for e in elems:
        parts = [out_a[e]] + [out_b[e, h] for h in range(H_HGRN)] + [out_c[e, h] for h in range(H_ATTN)]
        mix_ref[e * seq:(e + 1) * seq, :] = jnp.concatenate(parts, axis=1)


def _sample_mixers(proj2, sinks, cache_conv, state, cache_k, cache_v, conv_w, conv_b, ln_g, ln_b, hgrn_lb, hng,
                   bias_s, layer, carried):
    B = state.shape[1]
    seq = proj2.shape[0] // B
    w_buf = cache_k.shape[3]
    block = SAMPLE_BLOCK
    depth = hgrn_lb.shape[0]
    hist = CONV_WIDTH - 1
    const2 = lambda i: (0, 0)
    cache_specs = [pl.BlockSpec((None, hist, block, D_CONV), lambda i: (layer, 0, i, 0)),
                   pl.BlockSpec((None, block, H_HGRN, DK_HGRN, DV_HGRN), lambda i: (layer, i, 0, 0, 0)),
                   pl.BlockSpec((None, block, D_KV, w_buf), lambda i: (layer, i, 0, 0)),
                   pl.BlockSpec((None, block, D_KV, w_buf), lambda i: (layer, i, 0, 0))]
    inputs = (sinks, proj2, cache_conv, state, cache_k, cache_v, conv_w, conv_b.reshape(1, D_CONV),
              ln_g.reshape(1, D_CONV), ln_b.reshape(1, D_CONV), hgrn_lb, hng.reshape(1, DV_HGRN), bias_s)
    return pl.pallas_call(
        _carried(functools.partial(_sample_mix_kernel, layer=layer, block=block, seq=seq, w_buf=w_buf),
                 len(inputs), len(carried)),
        out_shape=(jax.ShapeDtypeStruct((B * seq, D_MODEL), F32),
                   jax.ShapeDtypeStruct((depth, hist, B, D_CONV), F32),
                   jax.ShapeDtypeStruct((depth, B, H_HGRN, DK_HGRN, DV_HGRN), F32),
                   jax.ShapeDtypeStruct((depth, B, D_KV, w_buf), F32),
                   jax.ShapeDtypeStruct((depth, B, D_KV, w_buf), F32)),
        grid=(B // block,),
        in_specs=[pl.BlockSpec(memory_space=pltpu.SMEM),
                  pl.BlockSpec((block * seq, IN_WIDTH), lambda i: (i, 0))] + cache_specs + [
                  pl.BlockSpec((CONV_WIDTH, D_CONV), const2),
                  pl.BlockSpec((1, D_CONV), const2),
                  pl.BlockSpec((1, D_CONV), const2),
                  pl.BlockSpec((1, D_CONV), const2),
                  pl.BlockSpec((depth, D_HGRN), const2),
                  pl.BlockSpec((1, DV_HGRN), const2),
                  pl.BlockSpec((KV_HEADS, GROUP * seq, 2 * ATTN_BLOCK), lambda i: (0, 0, 0))] + _carry_specs(carried),
        out_specs=tuple([pl.BlockSpec((block * seq, D_MODEL), lambda i: (i, 0))] + cache_specs),
        input_output_aliases={len(inputs) + i: 1 + i for i in range(len(carried))},
        scratch_shapes=[pltpu.VMEM((block, w_buf, D_KV), F32),
                        pltpu.VMEM((block, w_buf, D_KV), F32)],
        compiler_params=pltpu.CompilerParams(dimension_semantics=("arbitrary",), vmem_limit_bytes=VMEM_LIMIT),
        name="sample_mixers",
    )(*inputs, *carried)


def kernel(x_prompt, x_sample, cache_conv, state_hgrn, cache_swa_k, cache_swa_v, c_prompt, c_sample, rel_bias, w_ada, b_ada, norm_mix_g, w_in, conv_w, conv_b, conv_ln_g, conv_ln_b, hgrn_lb, hgrn_norm_g, attn_sinks, w_out, norm_mlp_g, w_up, w_down, final_g):
    Bp, Tp = x_prompt.shape[:2]
    Bs, Ts = x_sample.shape[:2]
    depth = w_in.shape[0]
    w_buf = cache_swa_k.shape[2]
    assert Tp % MIX_TILE == 0 and (Bp * Tp) % TOK_TILE == 0 and Tp % TOK_TILE == 0 and Bs % SAMPLE_BLOCK == 0
    assert w_buf == WINDOW and GROUP * Ts == SUBLANES

    bias_p, bias_s = _bias_tables(rel_bias, Ts, w_buf)
    mod = _modulation(jnp.concatenate([c_prompt, c_sample], axis=0), w_ada, b_ada)
    w_in_b, w_out_b, w_up_b, w_down_b = (w.astype(BF16) for w in (w_in, w_out, w_up, w_down))
    hlb = hgrn_lb.astype(F32)
    cc = jnp.swapaxes(cache_conv, 1, 2)
    ck = jnp.swapaxes(cache_swa_k.reshape(depth, Bs, w_buf, D_KV), 2, 3)
    cv = jnp.swapaxes(cache_swa_v.reshape(depth, Bs, w_buf, D_KV), 2, 3)

    xp = x_prompt.reshape(Bp * Tp, D_MODEL)
    xs = x_sample.reshape(Bs * Ts, D_MODEL)
    tile_s = Bs * Ts
    caches_p = ()
    caches_s = ()
    for l in range(depth):
        final = l == depth - 1
        mod_p = mod[l, :Bp].reshape(Bp, 1, N_MOD * D_MODEL)
        mod_s = mod[l, Bp:]
        mix_p, *caches_p = _prompt_mixers(xp.reshape(Bp, Tp, D_MODEL), mod_p, norm_mix_g[l], w_in_b, attn_sinks[l],
                                          conv_w[l], conv_b[l], conv_ln_g[l], conv_ln_b[l], hlb, hgrn_norm_g[l],
                                          bias_p, l, caches_p)
        xp = _out_mlp(mix_p.reshape(Bp * Tp, D_MODEL), xp, mod_p, norm_mlp_g[l], w_out_b, w_up_b, w_down_b,
                      final_g, l, TOK_TILE, Tp // TOK_TILE, final)
        proj_s = _inproj(xs, mod_s, norm_mix_g[l], w_in_b, l, tile_s, 1)
        mix_s, *caches_s = _sample_mixers(proj_s, attn_sinks[l], cc, state_hgrn, ck, cv, conv_w[l],
                                          conv_b[l], conv_ln_g[l], conv_ln_b[l], hlb, hgrn_norm_g[l], bias_s, l,
                                          caches_s)
        xs = _out_mlp(mix_s, xs, mod_s, norm_mlp_g[l], w_out_b, w_up_b, w_down_b, final_g, l, tile_s, 1, final)
    cp, sp, kp, vp = caches_p
    cs, ss, ksn, vsn = caches_s
    cs, ksn, vsn = jnp.swapaxes(cs, 1, 2), jnp.swapaxes(ksn, 2, 3), jnp.swapaxes(vsn, 2, 3)
    return (xp.reshape(Bp, Tp, D_MODEL), xs.reshape(Bs, Ts, D_MODEL), cp, cs, sp, ss,
            kp.reshape(depth, Bp, WINDOW, KV_HEADS, HEAD_DIM), ksn.reshape(depth, Bs, w_buf, KV_HEADS, HEAD_DIM),
            vp.reshape(depth, Bp, WINDOW, KV_HEADS, HEAD_DIM), vsn.reshape(depth, Bs, w_buf, KV_HEADS, HEAD_DIM))
```

```python
import functools
import math

import jax
import jax.numpy as jnp
from jax import lax
from jax.experimental import pallas as pl
from jax.experimental.pallas import tpu as pltpu

F32 = jnp.float32
BF16 = jnp.bfloat16

D_MODEL = 1024
D_CONV = 256
CONV_WIDTH = 31
H_HGRN = 4
DK_HGRN = 128
DV_HGRN = 128
D_HGRN = 512
HEAD_DIM = 64
H_ATTN = 4
KV_HEADS = 2
GROUP = H_ATTN // KV_HEADS
D_ATTN = H_ATTN * HEAD_DIM
D_KV = KV_HEADS * HEAD_DIM
WINDOW = 128
ATTN_BLOCK = 128
NUM_BUCKETS = 32
MAX_DISTANCE = 128
D_FF = 4 * D_MODEL
N_MOD = 6
EPS = 1e-6

OFF_AVAL = 0
OFF_AGATE = OFF_AVAL + D_CONV
OFF_Q = OFF_AGATE + D_CONV
OFF_F = OFF_Q + H_HGRN * DK_HGRN
OFF_I = OFF_F + H_HGRN * DK_HGRN
OFF_G = OFF_I + D_HGRN
OFF_QA = OFF_G + D_HGRN
OFF_KA = OFF_QA + D_ATTN
OFF_VA = OFF_KA + D_KV
IN_WIDTH = OFF_VA + D_KV

HGRN_CHUNK = 64
HGRN_KEYBLOCK = 32
HGRN_SPAN = 4
SUBLANES = 8
CONV_PAD = 32
MIX_TILE = 512
TOK_TILE = 512
SAMPLE_BLOCK = 16
FF_CHUNK = 1024
VMEM_LIMIT = 56 * 1024 * 1024

NT_DIMS = (((1,), (1,)), ((), ()))
TN_DIMS = (((0,), (0,)), ((), ()))


def _silu(x):
    return x * jax.nn.sigmoid(x)


def _rms_rows(x):
    return x * lax.rsqrt(jnp.mean(x * x, axis=-1, keepdims=True) + EPS)


def _layer_lb(hlb, layer):
    m = jnp.max(hlb, axis=0, keepdims=True)
    e = jnp.exp(hlb - m)
    p = e / jnp.sum(e, axis=0, keepdims=True)
    lb = jnp.zeros_like(m)
    for i in range(1, layer + 1):
        lb = lb + p[i:i + 1, :]
    return lb


def _split3_bf16(x):
    hi = x.astype(BF16)
    r = x - hi.astype(F32)
    mid = r.astype(BF16)
    return hi, mid, (r - mid.astype(F32)).astype(BF16)


def _select_rows_mxu(sel, x):
    return sum(jnp.dot(sel, part, preferred_element_type=F32) for part in _split3_bf16(x))


def _cumsum_rows_small(g):
    row = lax.broadcasted_iota(jnp.int32, g.shape, 0)
    b = jnp.zeros_like(g)
    for u in range(g.shape[0]):
        b = b + jnp.where(row >= u, g[u:u + 1, :], 0.0)
    return b


def _hgrn_span(proj_ref, mix_ref, st_ref, row0, lb, hng, tri, tick):
    L, KB = HGRN_CHUNK, HGRN_KEYBLOCK
    span = HGRN_SPAN * L
    g, k = _hgrn_gates(proj_ref[pl.ds(row0, span), OFF_F:OFF_F + D_HGRN], lb)
    b = _select_rows_mxu(tri, g)
    units = [(c, h) for c in range(HGRN_SPAN) for h in range(H_HGRN)]

    ops = {}
    for c, h in units:
        rows = pl.ds(row0 + c * L, L)
        cs = slice(h * DK_HGRN, (h + 1) * DK_HGRN)
        q = proj_ref[rows, OFF_Q + h * DK_HGRN:OFF_Q + (h + 1) * DK_HGRN]
        v = proj_ref[rows, OFF_I + h * DV_HGRN:OFF_I + (h + 1) * DV_HGRN].astype(BF16)
        kk = k[c * L:(c + 1) * L, cs]
        bb = b[c * L:(c + 1) * L, cs]
        qp, kp = [], []
        for lo in range(0, L, KB):
            r = bb[lo + KB // 2 - 1:lo + KB // 2, :]
            kp.append((kk[lo:lo + KB] * jnp.exp(r - bb[lo:lo + KB])).astype(BF16))
            qp.append((q[lo:] * jnp.exp(bb[lo:] - r)).astype(BF16))
        bl = bb[L - 1:L, :]
        ops[c, h] = dict(qp=qp, kp=kp, v=v, qt=(q * jnp.exp(bb)).astype(BF16),
                         kst=(kk * jnp.exp(bl - bb)).astype(BF16), e=jnp.exp(bl))
    tick()

    for u in units:
        o = ops[u]
        o["p"] = [lax.dot_general(qp, kp, NT_DIMS, preferred_element_type=F32) for qp, kp in zip(o["qp"], o["kp"])]
        o["m"] = lax.dot_general(o["v"], o["kst"], TN_DIMS, preferred_element_type=F32)
    tick()

    for u in units:
        pm = []
        for p in ops[u]["p"]:
            row = lax.broadcasted_iota(jnp.int32, p.shape, 0)
            col = lax.broadcasted_iota(jnp.int32, p.shape, 1)
            pm.append(jnp.where(row >= col, p, 0.0).astype(BF16))
        ops[u]["p"] = pm
    tick()

    for u in units:
        o = ops[u]
        blocks = [None] * (L // KB)
        for j, p in enumerate(o["p"]):
            cj = jnp.dot(p, o["v"][j * KB:(j + 1) * KB], preferred_element_type=F32)
            for i in range(j, L // KB):
                piece = cj[(i - j) * KB:(i - j + 1) * KB]
                blocks[i] = piece if blocks[i] is None else blocks[i] + piece
        o["o"] = jnp.concatenate(blocks, axis=0)
    tick()

    for h in range(H_HGRN):
        st = st_ref[h]
        for c in range(HGRN_SPAN):
            o = ops[c, h]
            out = o["o"] + lax.dot_general(o["qt"], st.astype(BF16), NT_DIMS, preferred_element_type=F32)
            st = o["e"] * st + o["m"]
            rows = pl.ds(row0 + c * L, L)
            gate = proj_ref[rows, OFF_G + h * DV_HGRN:OFF_G + (h + 1) * DV_HGRN]
            mix_ref[rows, D_CONV + h * DV_HGRN:D_CONV + (h + 1) * DV_HGRN] = _hgrn_out(out, hng, gate).astype(BF16)
        st_ref[h] = st


def _sink_softmax(s, sink):
    m = jnp.maximum(jnp.max(s, axis=-1, keepdims=True), sink)
    p = jnp.exp(s - m)
    return p, jnp.sum(p, axis=-1, keepdims=True) + jnp.exp(sink - m)


def _bias_kernel(tab_ref, bp_ref, bs_ref, op_ref, os_ref, *, seq):
    bk = bp_ref[...]
    for h in range(H_ATTN):
        acc = jnp.full(bk.shape, -jnp.inf, F32)
        for bkt in range(NUM_BUCKETS):
            acc = jnp.where(bk == bkt, tab_ref[bkt, h], acc)
        op_ref[h] = acc
    bk = bs_ref[...]
    row = lax.broadcasted_iota(jnp.int32, bk.shape, 0)
    for kv in range(KV_HEADS):
        acc = jnp.full(bk.shape, -jnp.inf, F32)
        for bkt in range(NUM_BUCKETS):
            val = jnp.full(bk.shape, tab_ref[bkt, kv * GROUP], F32)
            for gi in range(1, GROUP):
                val = jnp.where(row >= gi * seq, tab_ref[bkt, kv * GROUP + gi], val)
            acc = jnp.where(bk == bkt, val, acc)
        os_ref[kv] = acc


def _t5_bucket(rel):
    n = jnp.maximum(rel, 0)
    max_exact = NUM_BUCKETS // 2
    nf = jnp.maximum(n, max_exact).astype(F32)
    large = max_exact + (jnp.log(nf / max_exact) / math.log(MAX_DISTANCE / max_exact)
                         * (NUM_BUCKETS - max_exact)).astype(jnp.int32)
    large = jnp.minimum(large, NUM_BUCKETS - 1)
    return jnp.where(n < max_exact, n, large)


def _bias_tables(rel_bias, dec_seq, w_buf):
    qi = jnp.arange(ATTN_BLOCK, dtype=jnp.int32)[:, None]
    kc = jnp.arange(2 * ATTN_BLOCK, dtype=jnp.int32)[None, :]
    rel_p = qi + ATTN_BLOCK - kc
    bucket_p = jnp.where((rel_p >= 0) & (rel_p <= WINDOW), _t5_bucket(rel_p), -1)
    ts = (jnp.arange(GROUP * dec_seq, dtype=jnp.int32) % dec_seq)[:, None]
    js = jnp.arange(2 * ATTN_BLOCK, dtype=jnp.int32)[None, :]
    rel_s = w_buf + ts - js
    ok_s = (rel_s >= 0) & (rel_s <= WINDOW) & (js < w_buf + dec_seq)
    bucket_s = jnp.where(ok_s, _t5_bucket(rel_s), -1)
    return pl.pallas_call(
        functools.partial(_bias_kernel, seq=dec_seq),
        out_shape=(jax.ShapeDtypeStruct((H_ATTN, ATTN_BLOCK, 2 * ATTN_BLOCK), F32),
                   jax.ShapeDtypeStruct((KV_HEADS, GROUP * dec_seq, 2 * ATTN_BLOCK), F32)),
        in_specs=[pl.BlockSpec(memory_space=pltpu.SMEM),
                  pl.BlockSpec(memory_space=pltpu.VMEM),
                  pl.BlockSpec(memory_space=pltpu.VMEM)],
        out_specs=(pl.BlockSpec(memory_space=pltpu.VMEM), pl.BlockSpec(memory_space=pltpu.VMEM)),
        name="rel_bias_tables",
    )(rel_bias.astype(F32), bucket_p, bucket_s)


def _mod_kernel(c_ref, w_ref, b_ref, o_ref):
    s = _silu(c_ref[...]).astype(BF16)
    o_ref[...] = jnp.dot(s, w_ref[...].astype(BF16), preferred_element_type=F32) + b_ref[...]


def _modulation(c_all, w_ada, b_ada):
    depth = w_ada.shape[0]
    n = c_all.shape[0]
    return pl.pallas_call(
        _mod_kernel,
        out_shape=jax.ShapeDtypeStruct((depth, n, N_MOD * D_MODEL), F32),
        grid=(depth, N_MOD),
        in_specs=[pl.BlockSpec((n, D_MODEL), lambda l, j: (0, 0)),
                  pl.BlockSpec((None, D_MODEL, D_MODEL), lambda l, j: (l, 0, j)),
                  pl.BlockSpec((None, 1, D_MODEL), lambda l, j: (l, 0, j))],
        out_specs=pl.BlockSpec((None, n, D_MODEL), lambda l, j: (l, 0, j)),
        compiler_params=pltpu.CompilerParams(dimension_semantics=("arbitrary", "arbitrary"),
                                             vmem_limit_bytes=VMEM_LIMIT),
        name="adaln_modulation",
    )(c_all, w_ada, b_ada.reshape(depth, 1, N_MOD * D_MODEL))


def _mod_spec(mod, chunk, tiles_per_batch):
    if mod.ndim == 3:
        return pl.BlockSpec((None, 1, D_MODEL), lambda i: (i // tiles_per_batch, 0, chunk))
    return pl.BlockSpec((mod.shape[0], D_MODEL), lambda i: (0, chunk))


def _mod_rows(m, n_tokens):
    if m.shape[0] == 1:
        return m
    reps = n_tokens // m.shape[0]
    tok = lax.broadcasted_iota(jnp.int32, (n_tokens, m.shape[0]), 0)
    bat = lax.broadcasted_iota(jnp.int32, (n_tokens, m.shape[0]), 1)
    sel = jnp.where((tok >= bat * reps) & (tok < (bat + 1) * reps), 1.0, 0.0).astype(BF16)
    return _select_rows_mxu(sel, m)


def _inproj_kernel(x_ref, sh_ref, sc_ref, g_ref, w_ref, o_ref):
    n = x_ref.shape[0]
    h = _rms_rows(x_ref[...]) * g_ref[...] * (1.0 + _mod_rows(sc_ref[...], n)) + _mod_rows(sh_ref[...], n)
    o_ref[...] = jnp.dot(h.astype(BF16), w_ref[...], preferred_element_type=F32)


def _inproj(x2, mod, norm_g, w_in, layer, tile, tiles_per_batch):
    n = x2.shape[0]
    return pl.pallas_call(
        _inproj_kernel,
        out_shape=jax.ShapeDtypeStruct((n, IN_WIDTH), F32),
        grid=(n // tile,),
        in_specs=[pl.BlockSpec((tile, D_MODEL), lambda i: (i, 0)),
                  _mod_spec(mod, 0, tiles_per_batch),
                  _mod_spec(mod, 1, tiles_per_batch),
                  pl.BlockSpec((1, D_MODEL), lambda i: (0, 0)),
                  pl.BlockSpec((None, D_MODEL, IN_WIDTH), lambda i: (layer, 0, 0), pipeline_mode=pl.Buffered(1))],
        out_specs=pl.BlockSpec((tile, IN_WIDTH), lambda i: (i, 0)),
        compiler_params=pltpu.CompilerParams(dimension_semantics=("arbitrary",), vmem_limit_bytes=VMEM_LIMIT),
        name="in_projection",
    )(x2, mod, mod, norm_g.reshape(1, D_MODEL), w_in)


def _mlp_kernel(mix_ref, x_ref, g1_ref, sh_ref, sc_ref, g2_ref, ng_ref, wout_ref, wup_ref, wdn_ref, fg_ref, o_ref, *,
                final):
    n = x_ref.shape[0]
    x1 = x_ref[...] + _mod_rows(g1_ref[...], n) * jnp.dot(mix_ref[...].astype(BF16), wout_ref[...],
                                                          preferred_element_type=F32)
    h = (_rms_rows(x1) * ng_ref[...] * (1.0 + _mod_rows(sc_ref[...], n)) + _mod_rows(sh_ref[...], n)).astype(BF16)
    acc = None
    for c in range(D_FF // FF_CHUNK):
        u = jnp.dot(h, wup_ref[:, c * FF_CHUNK:(c + 1) * FF_CHUNK], preferred_element_type=F32)
        u = jnp.square(jnp.maximum(u, 0.0)).astype(BF16)
        d = jnp.dot(u, wdn_ref[c * FF_CHUNK:(c + 1) * FF_CHUNK, :], preferred_element_type=F32)
        acc = d if acc is None else acc + d
    x2 = x1 + _mod_rows(g2_ref[...], n) * acc
    if final:
        x2 = _rms_rows(x2) * fg_ref[...]
    o_ref[...] = x2


def _out_mlp(mix2, x2, mod, norm_g, w_out, w_up, w_down, final_g, layer, tile, tiles_per_batch, final):
    n = x2.shape[0]
    const = lambda i: (0, 0)
    of_layer = lambda i: (layer, 0, 0)
    return pl.pallas_call(
        functools.partial(_mlp_kernel, final=final),
        out_shape=jax.ShapeDtypeStruct((n, D_MODEL), F32),
        grid=(n // tile,),
        in_specs=[pl.BlockSpec((tile, D_MODEL), lambda i: (i, 0)),
                  pl.BlockSpec((tile, D_MODEL), lambda i: (i, 0)),
                  _mod_spec(mod, 2, tiles_per_batch),
                  _mod_spec(mod, 3, tiles_per_batch),
                  _mod_spec(mod, 4, tiles_per_batch),
                  _mod_spec(mod, 5, tiles_per_batch),
                  pl.BlockSpec((1, D_MODEL), const),
                  pl.BlockSpec((None, D_MODEL, D_MODEL), of_layer, pipeline_mode=pl.Buffered(1)),
                  pl.BlockSpec((None, D_MODEL, D_FF), of_layer, pipeline_mode=pl.Buffered(1)),
                  pl.BlockSpec((None, D_FF, D_MODEL), of_layer, pipeline_mode=pl.Buffered(1)),
                  pl.BlockSpec((1, D_MODEL), const)],
        out_specs=pl.BlockSpec((tile, D_MODEL), lambda i: (i, 0)),
        compiler_params=pltpu.CompilerParams(dimension_semantics=("arbitrary",), vmem_limit_bytes=VMEM_LIMIT),
        name="out_projection_mlp",
    )(mix2, x2, mod, mod, mod, mod, norm_g.reshape(1, D_MODEL), w_out, w_up, w_down, final_g.reshape(1, D_MODEL))


def _conv_ln_swish(acc, lng, lnb):
    mu = jnp.mean(acc, axis=-1, keepdims=True)
    xc = acc - mu
    y = xc * lax.rsqrt(jnp.mean(xc * xc, axis=-1, keepdims=True) + EPS) * lng + lnb
    return _silu(y)


def _hgrn_gates(fh, lb):
    f = lb + (1.0 - lb) * jax.nn.sigmoid(fh)
    return jnp.log(f), 1.0 - f


def _hgrn_out(o, hng, gate):
    return _rms_rows(o) * hng * _silu(gate)


def _ticker(pieces):
    it = iter(pieces)

    def tick():
        piece = next(it, None)
        if piece is not None:
            piece()

    def flush():
        for piece in it:
            piece()

    tick.flush = flush
    return tick


def _prompt_mix_kernel(sinks_ref, x_ref, sh_ref, sc_ref, ng_ref, win_ref, convw_ref, convb_ref, lng_ref, lnb_ref,
                       hlb_ref, hng_ref, bias_ref,
                       mix_ref, convo_ref, so_ref, ko_ref, vo_ref,
                       proj_ref, abuf, kbuf, vbuf, st_ref, *, layer, tile):
    t = pl.program_id(1)
    last = pl.num_programs(1) - 1

    @pl.when(t == 0)
    def _():
        abuf[0:CONV_PAD, :] = jnp.zeros((CONV_PAD, D_CONV), F32)
        abuf[CONV_PAD + tile:CONV_PAD + tile + SUBLANES, :] = jnp.zeros((SUBLANES, D_CONV), F32)
        kbuf[0:ATTN_BLOCK, :] = jnp.zeros((ATTN_BLOCK, D_KV), BF16)
        vbuf[0:ATTN_BLOCK, :] = jnp.zeros((ATTN_BLOCK, D_KV), BF16)
        st_ref[...] = jnp.zeros(st_ref.shape, F32)

    h_in = (_rms_rows(x_ref[...]) * ng_ref[...] * (1.0 + sc_ref[...]) + sh_ref[...]).astype(BF16)
    for lo, hi in ((OFF_AVAL, OFF_Q), (OFF_F, OFF_I), (OFF_Q, OFF_F), (OFF_I, OFF_G), (OFF_G, OFF_QA),
                   (OFF_QA, IN_WIDTH)):
        proj_ref[:, lo:hi] = jnp.dot(h_in, win_ref[:, lo:hi], preferred_element_type=F32)

    kbuf[ATTN_BLOCK:ATTN_BLOCK + tile, :] = proj_ref[:, OFF_KA:OFF_KA + D_KV].astype(BF16)
    vbuf[ATTN_BLOCK:ATTN_BLOCK + tile, :] = proj_ref[:, OFF_VA:OFF_VA + D_KV].astype(BF16)
    scale = HEAD_DIM ** -0.5
    attn = {}

    def attn_scores(blk):
        def run():
            r0 = blk * ATTN_BLOCK
            for h in range(H_ATTN):
                kv = h // GROUP
                q = (proj_ref[r0:r0 + ATTN_BLOCK, OFF_QA + h * HEAD_DIM:OFF_QA + (h + 1) * HEAD_DIM]
                     * scale).astype(BF16)
                kall = kbuf[r0:r0 + 2 * ATTN_BLOCK, kv * HEAD_DIM:(kv + 1) * HEAD_DIM]
                attn[blk, h] = lax.dot_general(q, kall, NT_DIMS, preferred_element_type=F32)
        return run

    def attn_softmax(blk):
        def run():
            for h in range(H_ATTN):
                s = attn[blk, h] + bias_ref[h]
                if blk == 0:
                    col = lax.broadcasted_iota(jnp.int32, s.shape, 1)
                    s = jnp.where(col + (t * tile - ATTN_BLOCK) >= 0, s, -jnp.inf)
                p, den = _sink_softmax(s, sinks_ref[h])
                attn[blk, h] = (p.astype(BF16), den)
        return run

    def attn_values(blk):
        def run():
            r0 = blk * ATTN_BLOCK
            heads = []
            for h in range(H_ATTN):
                kv = h // GROUP
                p, den = attn[blk, h]
                vall = vbuf[r0:r0 + 2 * ATTN_BLOCK, kv * HEAD_DIM:(kv + 1) * HEAD_DIM]
                heads.append(jnp.dot(p, vall, preferred_element_type=F32) / den)
            mix_ref[r0:r0 + ATTN_BLOCK, D_CONV + D_HGRN:D_MODEL] = jnp.concatenate(heads, axis=1).astype(BF16)
        return run

    tick = _ticker([stage(blk) for blk in range(tile // ATTN_BLOCK)
                    for stage in (attn_scores, attn_softmax, attn_values)])

    abuf[CONV_PAD:CONV_PAD + tile, :] = (proj_ref[:, OFF_AVAL:OFF_AVAL + D_CONV]
                                         * jax.nn.sigmoid(proj_ref[:, OFF_AGATE:OFF_AGATE + D_CONV]))
    first_row = CONV_PAD - (CONV_WIDTH - 1)
    acc = jnp.broadcast_to(convb_ref[...], (tile, D_CONV))
    for r in range(SUBLANES):
        z = None
        for off in range(r, first_row + CONV_WIDTH, SUBLANES):
            j = off - first_row
            if j < 0:
                continue
            term = convw_ref[j:j + 1, :] * abuf[off - r:off - r + tile + SUBLANES, :]
            z = term if z is None else z + term
        acc = acc + (z[0:tile] if r == 0 else pltpu.roll(z, tile + SUBLANES - r, 0)[0:tile])
    mix_ref[:, 0:D_CONV] = _conv_ln_swish(acc, lng_ref[...], lnb_ref[...]).astype(BF16)
    tick()

    lb = _layer_lb(hlb_ref[...], layer)
    hng = hng_ref[...]
    span = HGRN_SPAN * HGRN_CHUNK
    ri = lax.broadcasted_iota(jnp.int32, (span, span), 0)
    ci = lax.broadcasted_iota(jnp.int32, (span, span), 1)
    tri = jnp.where((ri >= ci) & (ri // HGRN_CHUNK == ci // HGRN_CHUNK), 1.0, 0.0).astype(BF16)
    for i in range(tile // span):
        _hgrn_span(proj_ref, mix_ref, st_ref, i * span, lb, hng, tri, tick)
    tick.flush()

    @pl.when(t == last)
    def _():
        convo_ref[...] = abuf[CONV_PAD + tile - (CONV_WIDTH - 1):CONV_PAD + tile, :]
        for h in range(H_HGRN):
            so_ref[h] = st_ref[h].T
        ko_ref[...] = proj_ref[tile - WINDOW:tile, OFF_KA:OFF_KA + D_KV]
        vo_ref[...] = proj_ref[tile - WINDOW:tile, OFF_VA:OFF_VA + D_KV]

    abuf[0:CONV_PAD, :] = abuf[tile:tile + CONV_PAD, :]
    kbuf[0:ATTN_BLOCK, :] = kbuf[tile:tile + ATTN_BLOCK, :]
    vbuf[0:ATTN_BLOCK, :] = vbuf[tile:tile + ATTN_BLOCK, :]


def _carry_specs(carried):
    return [pl.BlockSpec(memory_space=pl.ANY)] * len(carried)


def _carried(kernel_fn, n_in, n_carried):
    if n_carried == 0:
        return kernel_fn
    return lambda *refs: kernel_fn(*refs[:n_in], *refs[n_in + n_carried:])


def _prompt_mixers(x, mod, norm_g, w_in, sinks, conv_w, conv_b, ln_g, ln_b, hgrn_lb, hng, bias_p, layer, carried):
    B, T = x.shape[:2]
    tile = MIX_TILE
    depth = hgrn_lb.shape[0]
    const2 = lambda b, t: (0, 0)
    inputs = (sinks, x, mod, mod, norm_g.reshape(1, D_MODEL), w_in, conv_w, conv_b.reshape(1, D_CONV),
              ln_g.reshape(1, D_CONV), ln_b.reshape(1, D_CONV), hgrn_lb, hng.reshape(1, DV_HGRN), bias_p)
    return pl.pallas_call(
        _carried(functools.partial(_prompt_mix_kernel, layer=layer, tile=tile), len(inputs), len(carried)),
        out_shape=(jax.ShapeDtypeStruct((B, T, D_MODEL), BF16),
                   jax.ShapeDtypeStruct((depth, B, CONV_WIDTH - 1, D_CONV), F32),
                   jax.ShapeDtypeStruct((depth, B, H_HGRN, DK_HGRN, DV_HGRN), F32),
                   jax.ShapeDtypeStruct((depth, B, WINDOW, D_KV), F32),
                   jax.ShapeDtypeStruct((depth, B, WINDOW, D_KV), F32)),
        grid=(B, T // tile),
        in_specs=[pl.BlockSpec(memory_space=pltpu.SMEM),
                  pl.BlockSpec((None, tile, D_MODEL), lambda b, t: (b, t, 0)),
                  pl.BlockSpec((None, 1, D_MODEL), lambda b, t: (b, 0, 0)),
                  pl.BlockSpec((None, 1, D_MODEL), lambda b, t: (b, 0, 1)),
                  pl.BlockSpec((1, D_MODEL), const2),
                  pl.BlockSpec((None, D_MODEL, IN_WIDTH), lambda b, t: (layer, 0, 0), pipeline_mode=pl.Buffered(1)),
                  pl.BlockSpec((CONV_WIDTH, D_CONV), const2),
                  pl.BlockSpec((1, D_CONV), const2),
                  pl.BlockSpec((1, D_CONV), const2),
                  pl.BlockSpec((1, D_CONV), const2),
                  pl.BlockSpec((depth, D_HGRN), const2),
                  pl.BlockSpec((1, DV_HGRN), const2),
                  pl.BlockSpec((H_ATTN, ATTN_BLOCK, 2 * ATTN_BLOCK), lambda b, t: (0, 0, 0))] + _carry_specs(carried),
        out_specs=(pl.BlockSpec((None, tile, D_MODEL), lambda b, t: (b, t, 0)),
                   pl.BlockSpec((None, None, CONV_WIDTH - 1, D_CONV), lambda b, t: (layer, b, 0, 0)),
                   pl.BlockSpec((None, None, H_HGRN, DK_HGRN, DV_HGRN), lambda b, t: (layer, b, 0, 0, 0)),
                   pl.BlockSpec((None, None, WINDOW, D_KV), lambda b, t: (layer, b, 0, 0)),
                   pl.BlockSpec((None, None, WINDOW, D_KV), lambda b, t: (layer, b, 0, 0))),
        input_output_aliases={len(inputs) + i: 1 + i for i in range(len(carried))},
        scratch_shapes=[pltpu.VMEM((tile, IN_WIDTH), F32),
                        pltpu.VMEM((CONV_PAD + tile + SUBLANES, D_CONV), F32),
                        pltpu.VMEM((ATTN_BLOCK + tile, D_KV), BF16),
                        pltpu.VMEM((ATTN_BLOCK + tile, D_KV), BF16),
                        pltpu.VMEM((H_HGRN, DV_HGRN, DK_HGRN), F32)],
        compiler_params=pltpu.CompilerParams(dimension_semantics=("arbitrary", "arbitrary"),
                                             vmem_limit_bytes=VMEM_LIMIT),
        name="prompt_mixers",
    )(*inputs, *carried)


def _sample_mix_kernel(sinks_ref, proj_ref, cconv_ref, state_ref, ck_ref, cv_ref, convw_ref, convb_ref, lng_ref,
                       lnb_ref, hlb_ref, hng_ref, bias_ref,
                       mix_ref, convo_ref, so_ref, ko_ref, vo_ref, kpad_ref, vpad_ref, *,
                       layer, block, seq, w_buf):
    hist = CONV_WIDTH - 1

    @pl.when(pl.program_id(0) == 0)
    def _():
        for ref in (kpad_ref, vpad_ref):
            ref[:, 0:w_buf - seq, :] = jnp.zeros((block, w_buf - seq, D_KV), F32)

    lb = _layer_lb(hlb_ref[...], layer)
    hng = hng_ref[...]
    scale = HEAD_DIM ** -0.5
    elems = range(block)
    row8 = lax.broadcasted_iota(jnp.int32, (SUBLANES, DV_HGRN), 0)
    ones_rows = jnp.where((row8 >= seq) & (row8 < seq + 3), 1.0, 0.0)
    zrow = jnp.zeros((1, DK_HGRN), BF16)
    prow = lax.broadcasted_iota(jnp.int32, (seq, seq), 0)
    pcol = lax.broadcasted_iota(jnp.int32, (seq, seq), 1)
    grow = lax.broadcasted_iota(jnp.int32, (GROUP * seq, 1), 0)

    proj = [proj_ref[e * seq:(e + 1) * seq, :] for e in elems]

    glu = [p[:, OFF_AVAL:OFF_AVAL + D_CONV] * jax.nn.sigmoid(p[:, OFF_AGATE:OFF_AGATE + D_CONV]) for p in proj]
    full = [cconv_ref[i] for i in range(hist)]
    full += [jnp.concatenate([glu[e][t:t + 1] for e in elems], axis=0) for t in range(seq)]
    for i in range(hist):
        convo_ref[i] = full[i + seq]
    conv_out = []
    for t in range(seq):
        acc = jnp.broadcast_to(convb_ref[...], (block, D_CONV))
        for j in range(CONV_WIDTH):
            acc = acc + convw_ref[j:j + 1, :] * full[t + j]
        conv_out.append(_conv_ln_swish(acc, lng_ref[...], lnb_ref[...]))
    out_a = [jnp.concatenate([conv_out[t][e:e + 1] for t in range(seq)], axis=0) for e in elems]

    units = [(e, h) for e in elems for h in range(H_HGRN)]
    ops = {}
    for e in elems:
        p = proj[e]
        g, k = _hgrn_gates(p[:, OFF_F:OFF_F + D_HGRN], lb)
        b = _cumsum_rows_small(g)
        for h in range(H_HGRN):
            cs = slice(h * DK_HGRN, (h + 1) * DK_HGRN)
            q = p[:, OFF_Q + h * DK_HGRN:OFF_Q + (h + 1) * DK_HGRN]
            v = p[:, OFF_I + h * DV_HGRN:OFF_I + (h + 1) * DV_HGRN]
            bb = b[:, cs]
            bl = bb[seq - 1:seq, :]
            kst = (k[:, cs] * jnp.exp(bl - bb)).astype(BF16)
            x = jnp.concatenate([kst.astype(F32), *(part.astype(F32) for part in _split3_bf16(jnp.exp(bl))),
                                 zrow.astype(F32)], axis=0).astype(BF16)
            vpad = jnp.concatenate([v, jnp.zeros((SUBLANES - seq, DV_HGRN), F32)], axis=0)
            ops[e, h] = dict(qp=(q * jnp.exp(bb - bl)).astype(BF16), kst=kst, v=v.astype(BF16),
                             qt=(q * jnp.exp(bb)).astype(BF16), x=x,
                             r=jnp.concatenate([vpad, ones_rows], axis=1).astype(BF16))
    for u in units:
        o = ops[u]
        st = state_ref[u[0], u[1]]
        o["p"] = lax.dot_general(o["qp"], o["kst"], NT_DIMS, preferred_element_type=F32)
        o["inter"] = jnp.dot(o["qt"], st.astype(BF16), preferred_element_type=F32)
        me = lax.dot_general(o["x"], o["r"], TN_DIMS, preferred_element_type=F32)
        so_ref[u[0], u[1]] = me[:, DV_HGRN:] * st + me[:, :DV_HGRN]
    out_b = {}
    for u in units:
        o = ops[u]
        pm = jnp.where(prow >= pcol, o["p"], 0.0).astype(BF16)
        out = o["inter"] + jnp.dot(pm, o["v"], preferred_element_type=F32)
        gate = proj[u[0]][:, OFF_G + u[1] * DV_HGRN:OFF_G + (u[1] + 1) * DV_HGRN]
        out_b[u] = _hgrn_out(out, hng, gate)

    scores = {}
    for e in elems:
        p = proj[e]
        for kv in range(KV_HEADS):
            hs = slice(kv * HEAD_DIM, (kv + 1) * HEAD_DIM)
            q2 = jnp.concatenate([p[:, OFF_QA + h * HEAD_DIM:OFF_QA + (h + 1) * HEAD_DIM]
                                  for h in range(kv * GROUP, (kv + 1) * GROUP)], axis=0)
            q2 = (q2 * scale).astype(BF16)
            bias = bias_ref[kv]
            s_c = jnp.dot(q2, ck_ref[e, hs, :].astype(BF16), preferred_element_type=F32) + bias[:, 0:w_buf]
            s_n = (lax.dot_general(q2, p[:, OFF_KA + kv * HEAD_DIM:OFF_KA + (kv + 1) * HEAD_DIM].astype(BF16), NT_DIMS,
                                   preferred_element_type=F32) + bias[:, w_buf:w_buf + seq])
            scores[e, kv] = (s_c, s_n)
    out_c = {}
    for e in elems:
        p = proj[e]
        for kv in range(KV_HEADS):
            hs = slice(kv * HEAD_DIM, (kv + 1) * HEAD_DIM)
            sink = jnp.zeros((GROUP * seq, 1), F32)
            for gi in range(GROUP):
                sink = jnp.where(grow >= gi * seq, sinks_ref[kv * GROUP + gi], sink)
            s_c, s_n = scores[e, kv]
            m = jnp.maximum(jnp.maximum(jnp.max(s_c, axis=-1, keepdims=True), jnp.max(s_n, axis=-1, keepdims=True)),
                            sink)
            p_c = jnp.exp(s_c - m)
            p_n = jnp.exp(s_n - m)
            den = jnp.sum(p_c, axis=-1, keepdims=True) + jnp.sum(p_n, axis=-1, keepdims=True) + jnp.exp(sink - m)
            vnew = p[:, OFF_VA + kv * HEAD_DIM:OFF_VA + (kv + 1) * HEAD_DIM].astype(BF16)
            o2 = (lax.dot_general(p_c.astype(BF16), cv_ref[e, hs, :].astype(BF16), NT_DIMS,
                                  preferred_element_type=F32)
                  + jnp.dot(p_n.astype(BF16), vnew, preferred_element_type=F32)) / den
            for gi in range(GROUP):
                out_c[e, kv * GROUP + gi] = o2[gi * seq:(gi + 1) * seq]
    lane = lax.broadcasted_iota(jnp.int32, (D_KV, w_buf), 1)
    for e in elems:
        p = proj[e]
        for pad, cache, new, out in ((kpad_ref, ck_ref, p[:, OFF_KA:OFF_KA + D_KV], ko_ref),
                                     (vpad_ref, cv_ref, p[:, OFF_VA:OFF_VA + D_KV], vo_ref)):
            pad[e, w_buf - seq:w_buf, :] = new
            out[e] = jnp.where(lane >= w_buf - seq, pad[e].T, pltpu.roll(cache[e], w_buf - seq, 1))

    for e in elems:
        parts = [out_a[e]] + [out_b[e, h] for h in range(H_HGRN)] + [out_c[e, h] for h in range(H_ATTN)]
        mix_ref[e * seq:(e + 1) * seq, :] = jnp.concatenate(parts, axis=1)


def _sample_mixers(proj2, sinks, cache_conv, state, cache_k, cache_v, conv_w, conv_b, ln_g, ln_b, hgrn_lb, hng,
                   bias_s, layer, carried):
    B = state.shape[1]
    seq = proj2.shape[0] // B
    w_buf = cache_k.shape[3]
    block = SAMPLE_BLOCK
    depth = hgrn_lb.shape[0]
    hist = CONV_WIDTH - 1
    const2 = lambda i: (0, 0)
    cache_specs = [pl.BlockSpec((None, hist, block, D_CONV), lambda i: (layer, 0, i, 0)),
                   pl.BlockSpec((None, block, H_HGRN, DK_HGRN, DV_HGRN), lambda i: (layer, i, 0, 0, 0)),
                   pl.BlockSpec((None, block, D_KV, w_buf), lambda i: (layer, i, 0, 0)),
                   pl.BlockSpec((None, block, D_KV, w_buf), lambda i: (layer, i, 0, 0))]
    inputs = (sinks, proj2, cache_conv, state, cache_k, cache_v, conv_w, conv_b.reshape(1, D_CONV),
              ln_g.reshape(1, D_CONV), ln_b.reshape(1, D_CONV), hgrn_lb, hng.reshape(1, DV_HGRN), bias_s)
    return pl.pallas_call(
        _carried(functools.partial(_sample_mix_kernel, layer=layer, block=block, seq=seq, w_buf=w_buf),
                 len(inputs), len(carried)),
        out_shape=(jax.ShapeDtypeStruct((B * seq, D_MODEL), F32),
                   jax.ShapeDtypeStruct((depth, hist, B, D_CONV), F32),
                   jax.ShapeDtypeStruct((depth, B, H_HGRN, DK_HGRN, DV_HGRN), F32),
                   jax.ShapeDtypeStruct((depth, B, D_KV, w_buf), F32),
                   jax.ShapeDtypeStruct((depth, B, D_KV, w_buf), F32)),
        grid=(B // block,),
        in_specs=[pl.BlockSpec(memory_space=pltpu.SMEM),
                  pl.BlockSpec((block * seq, IN_WIDTH), lambda i: (i, 0))] + cache_specs + [
                  pl.BlockSpec((CONV_WIDTH, D_CONV), const2),
                  pl.BlockSpec((1, D_CONV), const2),
                  pl.BlockSpec((1, D_CONV), const2),
                  pl.BlockSpec((1, D_CONV), const2),
                  pl.BlockSpec((depth, D_HGRN), const2),
                  pl.BlockSpec((1, DV_HGRN), const2),
                  pl.BlockSpec((KV_HEADS, GROUP * seq, 2 * ATTN_BLOCK), lambda i: (0, 0, 0))] + _carry_specs(carried),
        out_specs=tuple([pl.BlockSpec((block * seq, D_MODEL), lambda i: (i, 0))] + cache_specs),
        input_output_aliases={len(inputs) + i: 1 + i for i in range(len(carried))},
        scratch_shapes=[pltpu.VMEM((block, w_buf, D_KV), F32),
                        pltpu.VMEM((block, w_buf, D_KV), F32)],
        compiler_params=pltpu.CompilerParams(dimension_semantics=("arbitrary",), vmem_limit_bytes=VMEM_LIMIT),
        name="sample_mixers",
    )(*inputs, *carried)


def kernel(x_prompt, x_sample, cache_conv, state_hgrn, cache_swa_k, cache_swa_v, c_prompt, c_sample, rel_bias, w_ada, b_ada, norm_mix_g, w_in, conv_w, conv_b, conv_ln_g, conv_ln_b, hgrn_lb, hgrn_norm_g, attn_sinks, w_out, norm_mlp_g, w_up, w_down, final_g):
    Bp, Tp = x_prompt.shape[:2]
    Bs, Ts = x_sample.shape[:2]
    depth = w_in.shape[0]
    w_buf = cache_swa_k.shape[2]
    assert Tp % MIX_TILE == 0 and (Bp * Tp) % TOK_TILE == 0 and Tp % TOK_TILE == 0 and Bs % SAMPLE_BLOCK == 0
    assert w_buf == WINDOW and GROUP * Ts == SUBLANES

    bias_p, bias_s = _bias_tables(rel_bias, Ts, w_buf)
    mod = _modulation(jnp.concatenate([c_prompt, c_sample], axis=0), w_ada, b_ada)
    w_in_b, w_out_b, w_up_b, w_down_b = (w.astype(BF16) for w in (w_in, w_out, w_up, w_down))
    hlb = hgrn_lb.astype(F32)
    cc = jnp.swapaxes(cache_conv, 1, 2)
    ck = jnp.swapaxes(cache_swa_k.reshape(depth, Bs, w_buf, D_KV), 2, 3)
    cv = jnp.swapaxes(cache_swa_v.reshape(depth, Bs, w_buf, D_KV), 2, 3)

    xp = x_prompt.reshape(Bp * Tp, D_MODEL)
    xs = x_sample.reshape(Bs * Ts, D_MODEL)
    tile_s = Bs * Ts
    caches_p = ()
    caches_s = ()
    for l in range(depth):
        final = l == depth - 1
        mod_p = mod[l, :Bp].reshape(Bp, 1, N_MOD * D_MODEL)
        mod_s = mod[l, Bp:]
        mix_p, *caches_p = _prompt_mixers(xp.reshape(Bp, Tp, D_MODEL), mod_p, norm_mix_g[l], w_in_b, attn_sinks[l],
                                          conv_w[l], conv_b[l], conv_ln_g[l], conv_ln_b[l], hlb, hgrn_norm_g[l],
                                          bias_p, l, caches_p)
        xp = _out_mlp(mix_p.reshape(Bp * Tp, D_MODEL), xp, mod_p, norm_mlp_g[l], w_out_b, w_up_b, w_down_b,
                      final_g, l, TOK_TILE, Tp // TOK_TILE, final)
        proj_s = _inproj(xs, mod_s, norm_mix_g[l], w_in_b, l, tile_s, 1)
        mix_s, *caches_s = _sample_mixers(proj_s, attn_sinks[l], cc, state_hgrn, ck, cv, conv_w[l],
                                          conv_b[l], conv_ln_g[l], conv_ln_b[l], hlb, hgrn_norm_g[l], bias_s, l,
                                          caches_s)
        xs = _out_mlp(mix_s, xs, mod_s, norm_mlp_g[l], w_out_b, w_up_b, w_down_b, final_g, l, tile_s, 1, final)
    cp, sp, kp, vp = caches_p
    cs, ss, ksn, vsn = caches_s
    cs, ksn, vsn = jnp.swapaxes(cs, 1, 2), jnp.swapaxes(ksn, 2, 3), jnp.swapaxes(vsn, 2, 3)
    return (xp.reshape(Bp, Tp, D_MODEL), xs.reshape(Bs, Ts, D_MODEL), cp, cs, sp, ss,
            kp.reshape(depth, Bp, WINDOW, KV_HEADS, HEAD_DIM), ksn.reshape(depth, Bs, w_buf, KV_HEADS, HEAD_DIM),
            vp.reshape(depth, Bp, WINDOW, KV_HEADS, HEAD_DIM), vsn.reshape(depth, Bs, w_buf, KV_HEADS, HEAD_DIM))
```

```python
import functools
import math

import jax
import jax.numpy as jnp
from jax import lax
from jax.experimental import pallas as pl
from jax.experimental.pallas import tpu as pltpu

F32 = jnp.float32
BF16 = jnp.bfloat16

D_MODEL = 1024
D_CONV = 256
CONV_WIDTH = 31
H_HGRN = 4
DK_HGRN = 128
DV_HGRN = 128
D_HGRN = 512
HEAD_DIM = 64
H_ATTN = 4
KV_HEADS = 2
GROUP = H_ATTN // KV_HEADS
D_ATTN = H_ATTN * HEAD_DIM
D_KV = KV_HEADS * HEAD_DIM
WINDOW = 128
ATTN_BLOCK = 128
NUM_BUCKETS = 32
MAX_DISTANCE = 128
D_FF = 4 * D_MODEL
N_MOD = 6
EPS = 1e-6

OFF_AVAL = 0
OFF_AGATE = OFF_AVAL + D_CONV
OFF_Q = OFF_AGATE + D_CONV
OFF_F = OFF_Q + H_HGRN * DK_HGRN
OFF_I = OFF_F + H_HGRN * DK_HGRN
OFF_G = OFF_I + D_HGRN
OFF_QA = OFF_G + D_HGRN
OFF_KA = OFF_QA + D_ATTN
OFF_VA = OFF_KA + D_KV
IN_WIDTH = OFF_VA + D_KV

HGRN_CHUNK = 64
HGRN_KEYBLOCK = 32
HGRN_SPAN = 4
SUBLANES = 8
CONV_PAD = 32
MIX_TILE = 512
TOK_TILE = 512
SAMPLE_BLOCK = 16
FF_CHUNK = 1024
VMEM_LIMIT = 56 * 1024 * 1024

NT_DIMS = (((1,), (1,)), ((), ()))
TN_DIMS = (((0,), (0,)), ((), ()))


def _silu(x):
    return x * jax.nn.sigmoid(x)


def _rms_rows(x):
    return x * lax.rsqrt(jnp.mean(x * x, axis=-1, keepdims=True) + EPS)


def _layer_lb(hlb, layer):
    m = jnp.max(hlb, axis=0, keepdims=True)
    e = jnp.exp(hlb - m)
    p = e / jnp.sum(e, axis=0, keepdims=True)
    lb = jnp.zeros_like(m)
    for i in range(1, layer + 1):
        lb = lb + p[i:i + 1, :]
    return lb


def _split3_bf16(x):
    hi = x.astype(BF16)
    r = x - hi.astype(F32)
    mid = r.astype(BF16)
    return hi, mid, (r - mid.astype(F32)).astype(BF16)


def _select_rows_mxu(sel, x):
    return sum(jnp.dot(sel, part, preferred_element_type=F32) for part in _split3_bf16(x))


def _cumsum_rows_small(g):
    row = lax.broadcasted_iota(jnp.int32, g.shape, 0)
    b = jnp.zeros_like(g)
    for u in range(g.shape[0]):
        b = b + jnp.where(row >= u, g[u:u + 1, :], 0.0)
    return b


def _hgrn_span(proj_ref, mix_ref, st_ref, row0, lb, hng, tri, tick):
    L, KB = HGRN_CHUNK, HGRN_KEYBLOCK
    span = HGRN_SPAN * L
    g, k = _hgrn_gates(proj_ref[pl.ds(row0, span), OFF_F:OFF_F + D_HGRN], lb)
    b = _select_rows_mxu(tri, g)
    units = [(c, h) for c in range(HGRN_SPAN) for h in range(H_HGRN)]

    ops = {}
    for c, h in units:
        rows = pl.ds(row0 + c * L, L)
        cs = slice(h * DK_HGRN, (h + 1) * DK_HGRN)
        q = proj_ref[rows, OFF_Q + h * DK_HGRN:OFF_Q + (h + 1) * DK_HGRN]
        v = proj_ref[rows, OFF_I + h * DV_HGRN:OFF_I + (h + 1) * DV_HGRN].astype(BF16)
        kk = k[c * L:(c + 1) * L, cs]
        bb = b[c * L:(c + 1) * L, cs]
        qp, kp = [], []
        for lo in range(0, L, KB):
            r = bb[lo + KB // 2 - 1:lo + KB // 2, :]
            kp.append((kk[lo:lo + KB] * jnp.exp(r - bb[lo:lo + KB])).astype(BF16))
            qp.append((q[lo:] * jnp.exp(bb[lo:] - r)).astype(BF16))
        bl = bb[L - 1:L, :]
        ops[c, h] = dict(qp=qp, kp=kp, v=v, qt=(q * jnp.exp(bb)).astype(BF16),
                         kst=(kk * jnp.exp(bl - bb)).astype(BF16), e=jnp.exp(bl))
    tick()

    for u in units:
        o = ops[u]
        o["p"] = [lax.dot_general(qp, kp, NT_DIMS, preferred_element_type=F32) for qp, kp in zip(o["qp"], o["kp"])]
        o["m"] = lax.dot_general(o["v"], o["kst"], TN_DIMS, preferred_element_type=F32)
    tick()

    for u in units:
        pm = []
        for p in ops[u]["p"]:
            row = lax.broadcasted_iota(jnp.int32, p.shape, 0)
            col = lax.broadcasted_iota(jnp.int32, p.shape, 1)
            pm.append(jnp.where(row >= col, p, 0.0).astype(BF16))
        ops[u]["p"] = pm
    tick()

    for u in units:
        o = ops[u]
        blocks = [None] * (L // KB)
        for j, p in enumerate(o["p"]):
            cj = jnp.dot(p, o["v"][j * KB:(j + 1) * KB], preferred_element_type=F32)
            for i in range(j, L // KB):
                piece = cj[(i - j) * KB:(i - j + 1) * KB]
                blocks[i] = piece if blocks[i] is None else blocks[i] + piece
        o["o"] = jnp.concatenate(blocks, axis=0)
    tick()

    for h in range(H_HGRN):
        st = st_ref[h]
        for c in range(HGRN_SPAN):
            o = ops[c, h]
            out = o["o"] + lax.dot_general(o["qt"], st.astype(BF16), NT_DIMS, preferred_element_type=F32)
            st = o["e"] * st + o["m"]
            rows = pl.ds(row0 + c * L, L)
            gate = proj_ref[rows, OFF_G + h * DV_HGRN:OFF_G + (h + 1) * DV_HGRN]
            mix_ref[rows, D_CONV + h * DV_HGRN:D_CONV + (h + 1) * DV_HGRN] = _hgrn_out(out, hng, gate).astype(BF16)
        st_ref[h] = st


def _sink_softmax(s, sink):
    m = jnp.maximum(jnp.max(s, axis=-1, keepdims=True), sink)
    p = jnp.exp(s - m)
    return p, jnp.sum(p, axis=-1, keepdims=True) + jnp.exp(sink - m)


def _bias_kernel(tab_ref, bp_ref, bs_ref, op_ref, os_ref, *, seq):
    bk = bp_ref[...]
    for h in range(H_ATTN):
        acc = jnp.full(bk.shape, -jnp.inf, F32)
        for bkt in range(NUM_BUCKETS):
            acc = jnp.where(bk == bkt, tab_ref[bkt, h], acc)
        op_ref[h] = acc
    bk = bs_ref[...]
    row = lax.broadcasted_iota(jnp.int32, bk.shape, 0)
    for kv in range(KV_HEADS):
        acc = jnp.full(bk.shape, -jnp.inf, F32)
        for bkt in range(NUM_BUCKETS):
            val = jnp.full(bk.shape, tab_ref[bkt, kv * GROUP], F32)
            for gi in range(1, GROUP):
                val = jnp.where(row >= gi * seq, tab_ref[bkt, kv * GROUP + gi], val)
            acc = jnp.where(bk == bkt, val, acc)
        os_ref[kv] = acc


def _t5_bucket(rel):
    n = jnp.maximum(rel, 0)
    max_exact = NUM_BUCKETS // 2
    nf = jnp.maximum(n, max_exact).astype(F32)
    large = max_exact + (jnp.log(nf / max_exact) / math.log(MAX_DISTANCE / max_exact)
                         * (NUM_BUCKETS - max_exact)).astype(jnp.int32)
    large = jnp.minimum(large, NUM_BUCKETS - 1)
    return jnp.where(n < max_exact, n, large)


def _bias_tables(rel_bias, dec_seq, w_buf):
    qi = jnp.arange(ATTN_BLOCK, dtype=jnp.int32)[:, None]
    kc = jnp.arange(2 * ATTN_BLOCK, dtype=jnp.int32)[None, :]
    rel_p = qi + ATTN_BLOCK - kc
    bucket_p = jnp.where((rel_p >= 0) & (rel_p <= WINDOW), _t5_bucket(rel_p), -1)
    ts = (jnp.arange(GROUP * dec_seq, dtype=jnp.int32) % dec_seq)[:, None]
    js = jnp.arange(2 * ATTN_BLOCK, dtype=jnp.int32)[None, :]
    rel_s = w_buf + ts - js
    ok_s = (rel_s >= 0) & (rel_s <= WINDOW) & (js < w_buf + dec_seq)
    bucket_s = jnp.where(ok_s, _t5_bucket(rel_s), -1)
    return pl.pallas_call(
        functools.partial(_bias_kernel, seq=dec_seq),
        out_shape=(jax.ShapeDtypeStruct((H_ATTN, ATTN_BLOCK, 2 * ATTN_BLOCK), F32),
                   jax.ShapeDtypeStruct((KV_HEADS, GROUP * dec_seq, 2 * ATTN_BLOCK), F32)),
        in_specs=[pl.BlockSpec(memory_space=pltpu.SMEM),
                  pl.BlockSpec(memory_space=pltpu.VMEM),
                  pl.BlockSpec(memory_space=pltpu.VMEM)],
        out_specs=(pl.BlockSpec(memory_space=pltpu.VMEM), pl.BlockSpec(memory_space=pltpu.VMEM)),
        name="rel_bias_tables",
    )(rel_bias.astype(F32), bucket_p, bucket_s)


def _mod_kernel(c_ref, w_ref, b_ref, o_ref):
    s = _silu(c_ref[...]).astype(BF16)
    o_ref[...] = jnp.dot(s, w_ref[...].astype(BF16), preferred_element_type=F32) + b_ref[...]


def _modulation(c_all, w_ada, b_ada):
    depth = w_ada.shape[0]
    n = c_all.shape[0]
    return pl.pallas_call(
        _mod_kernel,
        out_shape=jax.ShapeDtypeStruct((depth, n, N_MOD * D_MODEL), F32),
        grid=(depth, N_MOD),
        in_specs=[pl.BlockSpec((n, D_MODEL), lambda l, j: (0, 0)),
                  pl.BlockSpec((None, D_MODEL, D_MODEL), lambda l, j: (l, 0, j)),
                  pl.BlockSpec((None, 1, D_MODEL), lambda l, j: (l, 0, j))],
        out_specs=pl.BlockSpec((None, n, D_MODEL), lambda l, j: (l, 0, j)),
        compiler_params=pltpu.CompilerParams(dimension_semantics=("arbitrary", "arbitrary"),
                                             vmem_limit_bytes=VMEM_LIMIT),
        name="adaln_modulation",
    )(c_all, w_ada, b_ada.reshape(depth, 1, N_MOD * D_MODEL))


def _mod_rows(m, n_tokens):
    if m.shape[0] == 1:
        return m
    reps = n_tokens // m.shape[0]
    tok = lax.broadcasted_iota(jnp.int32, (n_tokens, m.shape[0]), 0)
    bat = lax.broadcasted_iota(jnp.int32, (n_tokens, m.shape[0]), 1)
    sel = jnp.where((tok >= bat * reps) & (tok < (bat + 1) * reps), 1.0, 0.0).astype(BF16)
    return _select_rows_mxu(sel, m)


def _modulated_norm(x, g, sc, sh):
    n = x.shape[0]
    return (_rms_rows(x) * g * (1.0 + _mod_rows(sc, n)) + _mod_rows(sh, n)).astype(BF16)


def _sample_mod_specs(mod_s, chunks, index_map):
    return [pl.BlockSpec((mod_s.shape[0], D_MODEL), functools.partial(index_map, chunk=c),
                         pipeline_mode=pl.Buffered(1)) for c in chunks]


def _mlp_tile(mix, x, g1, sh, sc, g2, ng, wout_ref, wup_ref, wdn_ref, fg, final):
    n = x.shape[0]
    x1 = x + _mod_rows(g1, n) * jnp.dot(mix.astype(BF16), wout_ref[...], preferred_element_type=F32)
    h = _modulated_norm(x1, ng, sc, sh)
    acc = None
    for c in range(D_FF // FF_CHUNK):
        u = jnp.dot(h, wup_ref[:, c * FF_CHUNK:(c + 1) * FF_CHUNK], preferred_element_type=F32)
        u = jnp.square(jnp.maximum(u, 0.0)).astype(BF16)
        d = jnp.dot(u, wdn_ref[c * FF_CHUNK:(c + 1) * FF_CHUNK, :], preferred_element_type=F32)
        acc = d if acc is None else acc + d
    x2 = x1 + _mod_rows(g2, n) * acc
    return _rms_rows(x2) * fg if final else x2


def _mlp_kernel(mix_ref, x_ref, g1_ref, sh_ref, sc_ref, g2_ref, smix_ref, sx_ref, sg1_ref, ssh_ref, ssc_ref, sg2_ref,
                ng_ref, wout_ref, wup_ref, wdn_ref, fg_ref, o_ref, so_ref, *, final):
    weights = (ng_ref[...], wout_ref, wup_ref, wdn_ref, fg_ref[...], final)
    o_ref[...] = _mlp_tile(mix_ref[...], x_ref[...], g1_ref[...], sh_ref[...], sc_ref[...], g2_ref[...], *weights)

    @pl.when(pl.program_id(0) == pl.num_programs(0) - 1)
    def _():
        so_ref[...] = _mlp_tile(smix_ref[...], sx_ref[...], sg1_ref[...], ssh_ref[...], ssc_ref[...], sg2_ref[...],
                                *weights)


def _out_mlp(mix2, x2, mod, mix_s, xs, mod_s, norm_g, w_out, w_up, w_down, final_g, layer, tiles_per_batch, final):
    n = x2.shape[0]
    tile = TOK_TILE
    const = lambda i: (0, 0)
    of_layer = lambda i: (layer, 0, 0)
    mod_p = lambda chunk: pl.BlockSpec((None, 1, D_MODEL), lambda i: (i // tiles_per_batch, 0, chunk))
    whole = lambda a: pl.BlockSpec(a.shape, const, pipeline_mode=pl.Buffered(1))
    return pl.pallas_call(
        functools.partial(_mlp_kernel, final=final),
        out_shape=(jax.ShapeDtypeStruct((n, D_MODEL), F32), jax.ShapeDtypeStruct(xs.shape, F32)),
        grid=(n // tile,),
        in_specs=[pl.BlockSpec((tile, D_MODEL), lambda i: (i, 0)),
                  pl.BlockSpec((tile, D_MODEL), lambda i: (i, 0)),
                  mod_p(2), mod_p(3), mod_p(4), mod_p(5),
                  whole(mix_s), whole(xs)] + _sample_mod_specs(mod_s, (2, 3, 4, 5), lambda i, chunk: (0, chunk)) + [
                  pl.BlockSpec((1, D_MODEL), const),
                  pl.BlockSpec((None, D_MODEL, D_MODEL), of_layer, pipeline_mode=pl.Buffered(1)),
                  pl.BlockSpec((None, D_MODEL, D_FF), of_layer, pipeline_mode=pl.Buffered(1)),
                  pl.BlockSpec((None, D_FF, D_MODEL), of_layer, pipeline_mode=pl.Buffered(1)),
                  pl.BlockSpec((1, D_MODEL), const)],
        out_specs=(pl.BlockSpec((tile, D_MODEL), lambda i: (i, 0)), pl.BlockSpec(xs.shape, const)),
        compiler_params=pltpu.CompilerParams(dimension_semantics=("arbitrary",), vmem_limit_bytes=VMEM_LIMIT),
        name="out_projection_mlp",
    )(mix2, x2, mod, mod, mod, mod, mix_s, xs, mod_s, mod_s, mod_s, mod_s, norm_g.reshape(1, D_MODEL), w_out, w_up,
      w_down, final_g.reshape(1, D_MODEL))


def _conv_ln_swish(acc, lng, lnb):
    mu = jnp.mean(acc, axis=-1, keepdims=True)
    xc = acc - mu
    y = xc * lax.rsqrt(jnp.mean(xc * xc, axis=-1, keepdims=True) + EPS) * lng + lnb
    return _silu(y)


def _hgrn_gates(fh, lb):
    f = lb + (1.0 - lb) * jax.nn.sigmoid(fh)
    return jnp.log(f), 1.0 - f


def _hgrn_out(o, hng, gate):
    return _rms_rows(o) * hng * _silu(gate)


def _ticker(pieces):
    it = iter(pieces)

    def tick():
        piece = next(it, None)
        if piece is not None:
            piece()

    def flush():
        for piece in it:
            piece()

    tick.flush = flush
    return tick


def _prompt_mix_kernel(sinks_ref, x_ref, sh_ref, sc_ref, sx_ref, ssh_ref, ssc_ref, ng_ref, win_ref, convw_ref,
                       convb_ref, lng_ref, lnb_ref, hlb_ref, hng_ref, bias_ref,
                       mix_ref, sproj_ref, convo_ref, so_ref, ko_ref, vo_ref,
                       proj_ref, abuf, kbuf, vbuf, st_ref, *, layer, tile):
    t = pl.program_id(1)
    last = pl.num_programs(1) - 1

    @pl.when(t == 0)
    def _():
        abuf[0:CONV_PAD, :] = jnp.zeros((CONV_PAD, D_CONV), F32)
        abuf[CONV_PAD + tile:CONV_PAD + tile + SUBLANES, :] = jnp.zeros((SUBLANES, D_CONV), F32)
        kbuf[0:ATTN_BLOCK, :] = jnp.zeros((ATTN_BLOCK, D_KV), BF16)
        vbuf[0:ATTN_BLOCK, :] = jnp.zeros((ATTN_BLOCK, D_KV), BF16)
        st_ref[...] = jnp.zeros(st_ref.shape, F32)

    h_in = _modulated_norm(x_ref[...], ng_ref[...], sc_ref[...], sh_ref[...])
    for lo, hi in ((OFF_AVAL, OFF_Q), (OFF_F, OFF_I), (OFF_Q, OFF_F), (OFF_I, OFF_G), (OFF_G, OFF_QA),
                   (OFF_QA, IN_WIDTH)):
        proj_ref[:, lo:hi] = jnp.dot(h_in, win_ref[:, lo:hi], preferred_element_type=F32)

    kbuf[ATTN_BLOCK:ATTN_BLOCK + tile, :] = proj_ref[:, OFF_KA:OFF_KA + D_KV].astype(BF16)
    vbuf[ATTN_BLOCK:ATTN_BLOCK + tile, :] = proj_ref[:, OFF_VA:OFF_VA + D_KV].astype(BF16)
    scale = HEAD_DIM ** -0.5
    attn = {}

    def attn_scores(blk):
        def run():
            r0 = blk * ATTN_BLOCK
            for h in range(H_ATTN):
                kv = h // GROUP
                q = (proj_ref[r0:r0 + ATTN_BLOCK, OFF_QA + h * HEAD_DIM:OFF_QA + (h + 1) * HEAD_DIM]
                     * scale).astype(BF16)
                kall = kbuf[r0:r0 + 2 * ATTN_BLOCK, kv * HEAD_DIM:(kv + 1) * HEAD_DIM]
                attn[blk, h] = lax.dot_general(q, kall, NT_DIMS, preferred_element_type=F32)
        return run

    def attn_softmax(blk):
        def run():
            for h in range(H_ATTN):
                s = attn[blk, h] + bias_ref[h]
                if blk == 0:
                    col = lax.broadcasted_iota(jnp.int32, s.shape, 1)
                    s = jnp.where(col + (t * tile - ATTN_BLOCK) >= 0, s, -jnp.inf)
                p, den = _sink_softmax(s, sinks_ref[h])
                attn[blk, h] = (p.astype(BF16), den)
        return run

    def attn_values(blk):
        def run():
            r0 = blk * ATTN_BLOCK
            heads = []
            for h in range(H_ATTN):
                kv = h // GROUP
                p, den = attn[blk, h]
                vall = vbuf[r0:r0 + 2 * ATTN_BLOCK, kv * HEAD_DIM:(kv + 1) * HEAD_DIM]
                heads.append(jnp.dot(p, vall, preferred_element_type=F32) / den)
            mix_ref[r0:r0 + ATTN_BLOCK, D_CONV + D_HGRN:D_MODEL] = jnp.concatenate(heads, axis=1).astype(BF16)
        return run

    tick = _ticker([stage(blk) for blk in range(tile // ATTN_BLOCK)
                    for stage in (attn_scores, attn_softmax, attn_values)])

    abuf[CONV_PAD:CONV_PAD + tile, :] = (proj_ref[:, OFF_AVAL:OFF_AVAL + D_CONV]
                                         * jax.nn.sigmoid(proj_ref[:, OFF_AGATE:OFF_AGATE + D_CONV]))
    first_row = CONV_PAD - (CONV_WIDTH - 1)
    acc = jnp.broadcast_to(convb_ref[...], (tile, D_CONV))
    for r in range(SUBLANES):
        z = None
        for off in range(r, first_row + CONV_WIDTH, SUBLANES):
            j = off - first_row
            if j < 0:
                continue
            term = convw_ref[j:j + 1, :] * abuf[off - r:off - r + tile + SUBLANES, :]
            z = term if z is None else z + term
        acc = acc + (z[0:tile] if r == 0 else pltpu.roll(z, tile + SUBLANES - r, 0)[0:tile])
    mix_ref[:, 0:D_CONV] = _conv_ln_swish(acc, lng_ref[...], lnb_ref[...]).astype(BF16)
    tick()

    lb = _layer_lb(hlb_ref[...], layer)
    hng = hng_ref[...]
    span = HGRN_SPAN * HGRN_CHUNK
    ri = lax.broadcasted_iota(jnp.int32, (span, span), 0)
    ci = lax.broadcasted_iota(jnp.int32, (span, span), 1)
    tri = jnp.where((ri >= ci) & (ri // HGRN_CHUNK == ci // HGRN_CHUNK), 1.0, 0.0).astype(BF16)
    for i in range(tile // span):
        _hgrn_span(proj_ref, mix_ref, st_ref, i * span, lb, hng, tri, tick)
    tick.flush()

    @pl.when(t == last)
    def _():
        convo_ref[...] = abuf[CONV_PAD + tile - (CONV_WIDTH - 1):CONV_PAD + tile, :]
        for h in range(H_HGRN):
            so_ref[h] = st_ref[h].T
        ko_ref[...] = proj_ref[tile - WINDOW:tile, OFF_KA:OFF_KA + D_KV]
        vo_ref[...] = proj_ref[tile - WINDOW:tile, OFF_VA:OFF_VA + D_KV]

    abuf[0:CONV_PAD, :] = abuf[tile:tile + CONV_PAD, :]
    kbuf[0:ATTN_BLOCK, :] = kbuf[tile:tile + ATTN_BLOCK, :]
    vbuf[0:ATTN_BLOCK, :] = vbuf[tile:tile + ATTN_BLOCK, :]

    @pl.when((pl.program_id(0) == pl.num_programs(0) - 1) & (t == last))
    def _():
        h_s = _modulated_norm(sx_ref[...], ng_ref[...], ssc_ref[...], ssh_ref[...])
        sproj_ref[...] = jnp.dot(h_s, win_ref[...], preferred_element_type=F32)


def _carry_specs(carried):
    return [pl.BlockSpec(memory_space=pl.ANY)] * len(carried)


def _carried(kernel_fn, n_in, n_carried):
    if n_carried == 0:
        return kernel_fn
    return lambda *refs: kernel_fn(*refs[:n_in], *refs[n_in + n_carried:])


def _prompt_mixers(x, mod, xs, mod_s, norm_g, w_in, sinks, conv_w, conv_b, ln_g, ln_b, hgrn_lb, hng, bias_p, layer,
                   carried):
    B, T = x.shape[:2]
    tile = MIX_TILE
    depth = hgrn_lb.shape[0]
    const2 = lambda b, t: (0, 0)
    inputs = (sinks, x, mod, mod, xs, mod_s, mod_s, norm_g.reshape(1, D_MODEL), w_in, conv_w,
              conv_b.reshape(1, D_CONV), ln_g.reshape(1, D_CONV), ln_b.reshape(1, D_CONV), hgrn_lb,
              hng.reshape(1, DV_HGRN), bias_p)
    return pl.pallas_call(
        _carried(functools.partial(_prompt_mix_kernel, layer=layer, tile=tile), len(inputs), len(carried)),
        out_shape=(jax.ShapeDtypeStruct((B, T, D_MODEL), BF16),
                   jax.ShapeDtypeStruct((xs.shape[0], IN_WIDTH), F32),
                   jax.ShapeDtypeStruct((depth, B, CONV_WIDTH - 1, D_CONV), F32),
                   jax.ShapeDtypeStruct((depth, B, H_HGRN, DK_HGRN, DV_HGRN), F32),
                   jax.ShapeDtypeStruct((depth, B, WINDOW, D_KV), F32),
                   jax.ShapeDtypeStruct((depth, B, WINDOW, D_KV), F32)),
        grid=(B, T // tile),
        in_specs=[pl.BlockSpec(memory_space=pltpu.SMEM),
                  pl.BlockSpec((None, tile, D_MODEL), lambda b, t: (b, t, 0)),
                  pl.BlockSpec((None, 1, D_MODEL), lambda b, t: (b, 0, 0)),
                  pl.BlockSpec((None, 1, D_MODEL), lambda b, t: (b, 0, 1)),
                  pl.BlockSpec(xs.shape, const2, pipeline_mode=pl.Buffered(1))]
                 + _sample_mod_specs(mod_s, (0, 1), lambda b, t, chunk: (0, chunk)) + [
                  pl.BlockSpec((1, D_MODEL), const2),
                  pl.BlockSpec((None, D_MODEL, IN_WIDTH), lambda b, t: (layer, 0, 0), pipeline_mode=pl.Buffered(1)),
                  pl.BlockSpec((CONV_WIDTH, D_CONV), const2),
                  pl.BlockSpec((1, D_CONV), const2),
                  pl.BlockSpec((1, D_CONV), const2),
                  pl.BlockSpec((1, D_CONV), const2),
                  pl.BlockSpec((depth, D_HGRN), const2),
                  pl.BlockSpec((1, DV_HGRN), const2),
                  pl.BlockSpec((H_ATTN, ATTN_BLOCK, 2 * ATTN_BLOCK), lambda b, t: (0, 0, 0))] + _carry_specs(carried),
        out_specs=(pl.BlockSpec((None, tile, D_MODEL), lambda b, t: (b, t, 0)),
                   pl.BlockSpec((xs.shape[0], IN_WIDTH), const2),
                   pl.BlockSpec((None, None, CONV_WIDTH - 1, D_CONV), lambda b, t: (layer, b, 0, 0)),
                   pl.BlockSpec((None, None, H_HGRN, DK_HGRN, DV_HGRN), lambda b, t: (layer, b, 0, 0, 0)),
                   pl.BlockSpec((None, None, WINDOW, D_KV), lambda b, t: (layer, b, 0, 0)),
                   pl.BlockSpec((None, None, WINDOW, D_KV), lambda b, t: (layer, b, 0, 0))),
        input_output_aliases={len(inputs) + i: 2 + i for i in range(len(carried))},
        scratch_shapes=[pltpu.VMEM((tile, IN_WIDTH), F32),
                        pltpu.VMEM((CONV_PAD + tile + SUBLANES, D_CONV), F32),
                        pltpu.VMEM((ATTN_BLOCK + tile, D_KV), BF16),
                        pltpu.VMEM((ATTN_BLOCK + tile, D_KV), BF16),
                        pltpu.VMEM((H_HGRN, DV_HGRN, DK_HGRN), F32)],
        compiler_params=pltpu.CompilerParams(dimension_semantics=("arbitrary", "arbitrary"),
                                             vmem_limit_bytes=VMEM_LIMIT),
        name="prompt_mixers",
    )(*inputs, *carried)


def _sample_mix_kernel(sinks_ref, proj_ref, cconv_ref, state_ref, ck_ref, cv_ref, convw_ref, convb_ref, lng_ref,
                       lnb_ref, hlb_ref, hng_ref, bias_ref,
                       mix_ref, convo_ref, so_ref, ko_ref, vo_ref, kpad_ref, vpad_ref, *,
                       layer, block, seq, w_buf):
    hist = CONV_WIDTH - 1

    @pl.when(pl.program_id(0) == 0)
    def _():
        for ref in (kpad_ref, vpad_ref):
            ref[:, 0:w_buf - seq, :] = jnp.zeros((block, w_buf - seq, D_KV), F32)

    lb = _layer_lb(hlb_ref[...], layer)
    hng = hng_ref[...]
    scale = HEAD_DIM ** -0.5
    elems = range(block)
    row8 = lax.broadcasted_iota(jnp.int32, (SUBLANES, DV_HGRN), 0)
    ones_rows = jnp.where((row8 >= seq) & (row8 < seq + 3), 1.0, 0.0)
    zrow = jnp.zeros((1, DK_HGRN), BF16)
    prow = lax.broadcasted_iota(jnp.int32, (seq, seq), 0)
    pcol = lax.broadcasted_iota(jnp.int32, (seq, seq), 1)
    grow = lax.broadcasted_iota(jnp.int32, (GROUP * seq, 1), 0)

    proj = [proj_ref[e * seq:(e + 1) * seq, :] for e in elems]

    glu = [p[:, OFF_AVAL:OFF_AVAL + D_CONV] * jax.nn.sigmoid(p[:, OFF_AGATE:OFF_AGATE + D_CONV]) for p in proj]
    full = [cconv_ref[i] for i in range(hist)]
    full += [jnp.concatenate([glu[e][t:t + 1] for e in elems], axis=0) for t in range(seq)]
    for i in range(hist):
        convo_ref[i] = full[i + seq]
    conv_out = []
    for t in range(seq):
        acc = jnp.broadcast_to(convb_ref[...], (block, D_CONV))
        for j in range(CONV_WIDTH):
            acc = acc + convw_ref[j:j + 1, :] * full[t + j]
        conv_out.append(_conv_ln_swish(acc, lng_ref[...], lnb_ref[...]))
    out_a = [jnp.concatenate([conv_out[t][e:e + 1] for t in range(seq)], axis=0) for e in elems]

    units = [(e, h) for e in elems for h in range(H_HGRN)]
    ops = {}
    for e in elems:
        p = proj[e]
        g, k = _hgrn_gates(p[:, OFF_F:OFF_F + D_HGRN], lb)
        b = _cumsum_rows_small(g)
        for h in range(H_HGRN):
            cs = slice(h * DK_HGRN, (h + 1) * DK_HGRN)
            q = p[:, OFF_Q + h * DK_HGRN:OFF_Q + (h + 1) * DK_HGRN]
            v = p[:, OFF_I + h * DV_HGRN:OFF_I + (h + 1) * DV_HGRN]
            bb = b[:, cs]
            bl = bb[seq - 1:seq, :]
            kst = (k[:, cs] * jnp.exp(bl - bb)).astype(BF16)
            x = jnp.concatenate([kst.astype(F32), *(part.astype(F32) for part in _split3_bf16(jnp.exp(bl))),
                                 zrow.astype(F32)], axis=0).astype(BF16)
            vpad = jnp.concatenate([v, jnp.zeros((SUBLANES - seq, DV_HGRN), F32)], axis=0)
            ops[e, h] = dict(qp=(q * jnp.exp(bb - bl)).astype(BF16), kst=kst, v=v.astype(BF16),
                             qt=(q * jnp.exp(bb)).astype(BF16), x=x,
                             r=jnp.concatenate([vpad, ones_rows], axis=1).astype(BF16))
    for u in units:
        o = ops[u]
        st = state_ref[u[0], u[1]]
        o["p"] = lax.dot_general(o["qp"], o["kst"], NT_DIMS, preferred_element_type=F32)
        o["inter"] = jnp.dot(o["qt"], st.astype(BF16), preferred_element_type=F32)
        me = lax.dot_general(o["x"], o["r"], TN_DIMS, preferred_element_type=F32)
        so_ref[u[0], u[1]] = me[:, DV_HGRN:] * st + me[:, :DV_HGRN]
    out_b = {}
    for u in units:
        o = ops[u]
        pm = jnp.where(prow >= pcol, o["p"], 0.0).astype(BF16)
        out = o["inter"] + jnp.dot(pm, o["v"], preferred_element_type=F32)
        gate = proj[u[0]][:, OFF_G + u[1] * DV_HGRN:OFF_G + (u[1] + 1) * DV_HGRN]
        out_b[u] = _hgrn_out(out, hng, gate)

    scores = {}
    for e in elems:
        p = proj[e]
        for kv in range(KV_HEADS):
            hs = slice(kv * HEAD_DIM, (kv + 1) * HEAD_DIM)
            q2 = jnp.concatenate([p[:, OFF_QA + h * HEAD_DIM:OFF_QA + (h + 1) * HEAD_DIM]
                                  for h in range(kv * GROUP, (kv + 1) * GROUP)], axis=0)
            q2 = (q2 * scale).astype(BF16)
            bias = bias_ref[kv]
            s_c = jnp.dot(q2, ck_ref[e, hs, :].astype(BF16), preferred_element_type=F32) + bias[:, 0:w_buf]
            s_n = (lax.dot_general(q2, p[:, OFF_KA + kv * HEAD_DIM:OFF_KA + (kv + 1) * HEAD_DIM].astype(BF16), NT_DIMS,
                                   preferred_element_type=F32) + bias[:, w_buf:w_buf + seq])
            scores[e, kv] = (s_c, s_n)
    out_c = {}
    for e in elems:
        p = proj[e]
        for kv in range(KV_HEADS):
            hs = slice(kv * HEAD_DIM, (kv + 1) * HEAD_DIM)
            sink = jnp.zeros((GROUP * seq, 1), F32)
            for gi in range(GROUP):
                sink = jnp.where(grow >= gi * seq, sinks_ref[kv * GROUP + gi], sink)
            s_c, s_n = scores[e, kv]
            m = jnp.maximum(jnp.maximum(jnp.max(s_c, axis=-1, keepdims=True), jnp.max(s_n, axis=-1, keepdims=True)),
                            sink)
            p_c = jnp.exp(s_c - m)
            p_n = jnp.exp(s_n - m)
            den = jnp.sum(p_c, axis=-1, keepdims=True) + jnp.sum(p_n, axis=-1, keepdims=True) + jnp.exp(sink - m)
            vnew = p[:, OFF_VA + kv * HEAD_DIM:OFF_VA + (kv + 1) * HEAD_DIM].astype(BF16)
            o2 = (lax.dot_general(p_c.astype(BF16), cv_ref[e, hs, :].astype(BF16), NT_DIMS,
                                  preferred_element_type=F32)
                  + jnp.dot(p_n.astype(BF16), vnew, preferred_element_type=F32)) / den
            for gi in range(GROUP):
                out_c[e, kv * GROUP + gi] = o2[gi * seq:(gi + 1) * seq]
    lane = lax.broadcasted_iota(jnp.int32, (D_KV, w_buf), 1)
    for e in elems:
        p = proj[e]
        for pad, cache, new, out in ((kpad_ref, ck_ref, p[:, OFF_KA:OFF_KA + D_KV], ko_ref),
                                     (vpad_ref, cv_ref, p[:, OFF_VA:OFF_VA + D_KV], vo_ref)):
            pad[e, w_buf - seq:w_buf, :] = new
            out[e] = jnp.where(lane >= w_buf - seq, pad[e].T, pltpu.roll(cache[e], w_buf - seq, 1))

    for e in elems:
        parts = [out_a[e]] + [out_b[e, h] for h in range(H_HGRN)] + [out_c[e, h] for h in range(H_ATTN)]
        mix_ref[e * seq:(e + 1) * seq, :] = jnp.concatenate(parts, axis=1)


def _sample_mixers(proj2, sinks, cache_conv, state, cache_k, cache_v, conv_w, conv_b, ln_g, ln_b, hgrn_lb, hng,
                   bias_s, layer, carried):
    B = state.shape[1]
    seq = proj2.shape[0] // B
    w_buf = cache_k.shape[3]
    block = SAMPLE_BLOCK
    depth = hgrn_lb.shape[0]
    hist = CONV_WIDTH - 1
    const2 = lambda i: (0, 0)
    cache_specs = [pl.BlockSpec((None, hist, block, D_CONV), lambda i: (layer, 0, i, 0)),
                   pl.BlockSpec((None, block, H_HGRN, DK_HGRN, DV_HGRN), lambda i: (layer, i, 0, 0, 0)),
                   pl.BlockSpec((None, block, D_KV, w_buf), lambda i: (layer, i, 0, 0)),
                   pl.BlockSpec((None, block, D_KV, w_buf), lambda i: (layer, i, 0, 0))]
    inputs = (sinks, proj2, cache_conv, state, cache_k, cache_v, conv_w, conv_b.reshape(1, D_CONV),
              ln_g.reshape(1, D_CONV), ln_b.reshape(1, D_CONV), hgrn_lb, hng.reshape(1, DV_HGRN), bias_s)
    return pl.pallas_call(
        _carried(functools.partial(_sample_mix_kernel, layer=layer, block=block, seq=seq, w_buf=w_buf),
                 len(inputs), len(carried)),
        out_shape=(jax.ShapeDtypeStruct((B * seq, D_MODEL), F32),
                   jax.ShapeDtypeStruct((depth, hist, B, D_CONV), F32),
                   jax.ShapeDtypeStruct((depth, B, H_HGRN, DK_HGRN, DV_HGRN), F32),
                   jax.ShapeDtypeStruct((depth, B, D_KV, w_buf), F32),
                   jax.ShapeDtypeStruct((depth, B, D_KV, w_buf), F32)),
        grid=(B // block,),
        in_specs=[pl.BlockSpec(memory_space=pltpu.SMEM),
                  pl.BlockSpec((block * seq, IN_WIDTH), lambda i: (i, 0))] + cache_specs + [
                  pl.BlockSpec((CONV_WIDTH, D_CONV), const2),
                  pl.BlockSpec((1, D_CONV), const2),
                  pl.BlockSpec((1, D_CONV), const2),
                  pl.BlockSpec((1, D_CONV), const2),
                  pl.BlockSpec((depth, D_HGRN), const2),
                  pl.BlockSpec((1, DV_HGRN), const2),
                  pl.BlockSpec((KV_HEADS, GROUP * seq, 2 * ATTN_BLOCK), lambda i: (0, 0, 0))] + _carry_specs(carried),
        out_specs=tuple([pl.BlockSpec((block * seq, D_MODEL), lambda i: (i, 0))] + cache_specs),
        input_output_aliases={len(inputs) + i: 1 + i for i in range(len(carried))},
        scratch_shapes=[pltpu.VMEM((block, w_buf, D_KV), F32),
                        pltpu.VMEM((block, w_buf, D_KV), F32)],
        compiler_params=pltpu.CompilerParams(dimension_semantics=("arbitrary",), vmem_limit_bytes=VMEM_LIMIT),
        name="sample_mixers",
    )(*inputs, *carried)


def kernel(x_prompt, x_sample, cache_conv, state_hgrn, cache_swa_k, cache_swa_v, c_prompt, c_sample, rel_bias, w_ada, b_ada, norm_mix_g, w_in, conv_w, conv_b, conv_ln_g, conv_ln_b, hgrn_lb, hgrn_norm_g, attn_sinks, w_out, norm_mlp_g, w_up, w_down, final_g):
    Bp, Tp = x_prompt.shape[:2]
    Bs, Ts = x_sample.shape[:2]
    depth = w_in.shape[0]
    w_buf = cache_swa_k.shape[2]
    assert Tp % MIX_TILE == 0 and (Bp * Tp) % TOK_TILE == 0 and Tp % TOK_TILE == 0 and Bs % SAMPLE_BLOCK == 0
    assert w_buf == WINDOW and GROUP * Ts == SUBLANES

    bias_p, bias_s = _bias_tables(rel_bias, Ts, w_buf)
    mod = _modulation(jnp.concatenate([c_prompt, c_sample], axis=0), w_ada, b_ada)
    w_in_b, w_out_b, w_up_b, w_down_b = (w.astype(BF16) for w in (w_in, w_out, w_up, w_down))
    hlb = hgrn_lb.astype(F32)
    cc = jnp.swapaxes(cache_conv, 1, 2)
    ck = jnp.swapaxes(cache_swa_k.reshape(depth, Bs, w_buf, D_KV), 2, 3)
    cv = jnp.swapaxes(cache_swa_v.reshape(depth, Bs, w_buf, D_KV), 2, 3)

    xp = x_prompt.reshape(Bp * Tp, D_MODEL)
    xs = x_sample.reshape(Bs * Ts, D_MODEL)
    caches_p = ()
    caches_s = ()
    for l in range(depth):
        final = l == depth - 1
        mod_p = mod[l, :Bp].reshape(Bp, 1, N_MOD * D_MODEL)
        mod_s = mod[l, Bp:]
        mix_p, proj_s, *caches_p = _prompt_mixers(xp.reshape(Bp, Tp, D_MODEL), mod_p, xs, mod_s, norm_mix_g[l], w_in_b,
                                                  attn_sinks[l], conv_w[l], conv_b[l], conv_ln_g[l], conv_ln_b[l], hlb,
                                                  hgrn_norm_g[l], bias_p, l, caches_p)
        mix_s, *caches_s = _sample_mixers(proj_s, attn_sinks[l], cc, state_hgrn, ck, cv, conv_w[l],
                                          conv_b[l], conv_ln_g[l], conv_ln_b[l], hlb, hgrn_norm_g[l], bias_s, l,
                                          caches_s)
        xp, xs = _out_mlp(mix_p.reshape(Bp * Tp, D_MODEL), xp, mod_p, mix_s, xs, mod_s, norm_mlp_g[l], w_out_b,
                          w_up_b, w_down_b, final_g, l, Tp // TOK_TILE, final)
    cp, sp, kp, vp = caches_p
    cs, ss, ksn, vsn = caches_s
    cs, ksn, vsn = jnp.swapaxes(cs, 1, 2), jnp.swapaxes(ksn, 2, 3), jnp.swapaxes(vsn, 2, 3)
    return (xp.reshape(Bp, Tp, D_MODEL), xs.reshape(Bs, Ts, D_MODEL), cp, cs, sp, ss,
            kp.reshape(depth, Bp, WINDOW, KV_HEADS, HEAD_DIM), ksn.reshape(depth, Bs, w_buf, KV_HEADS, HEAD_DIM),
            vp.reshape(depth, Bp, WINDOW, KV_HEADS, HEAD_DIM), vsn.reshape(depth, Bs, w_buf, KV_HEADS, HEAD_DIM))
```

```python
import functools
import math

import jax
import jax.numpy as jnp
from jax import lax
from jax.experimental import pallas as pl
from jax.experimental.pallas import tpu as pltpu

F32 = jnp.float32
BF16 = jnp.bfloat16

D_MODEL = 1024
D_CONV = 256
CONV_WIDTH = 31
H_HGRN = 4
DK_HGRN = 128
DV_HGRN = 128
D_HGRN = 512
HEAD_DIM = 64
H_ATTN = 4
KV_HEADS = 2
GROUP = H_ATTN // KV_HEADS
D_ATTN = H_ATTN * HEAD_DIM
D_KV = KV_HEADS * HEAD_DIM
WINDOW = 128
ATTN_BLOCK = 128
NUM_BUCKETS = 32
MAX_DISTANCE = 128
D_FF = 4 * D_MODEL
N_MOD = 6
EPS = 1e-6

OFF_AVAL = 0
OFF_AGATE = OFF_AVAL + D_CONV
OFF_Q = OFF_AGATE + D_CONV
OFF_F = OFF_Q + H_HGRN * DK_HGRN
OFF_I = OFF_F + H_HGRN * DK_HGRN
OFF_G = OFF_I + D_HGRN
OFF_QA = OFF_G + D_HGRN
OFF_KA = OFF_QA + D_ATTN
OFF_VA = OFF_KA + D_KV
IN_WIDTH = OFF_VA + D_KV

HGRN_CHUNK = 64
HGRN_KEYBLOCK = 32
HGRN_SPAN = 2
SUBLANES = 8
CONV_PAD = 32
MIX_TILE = 256
INPROJ_PIECE = 256
TOK_TILE = 512
SAMPLE_BLOCK = 16
FF_CHUNK = 1024
VMEM_LIMIT = 56 * 1024 * 1024

NT_DIMS = (((1,), (1,)), ((), ()))
TN_DIMS = (((0,), (0,)), ((), ()))


def _silu(x):
    return x * jax.nn.sigmoid(x)


def _rms_rows(x):
    return x * lax.rsqrt(jnp.mean(x * x, axis=-1, keepdims=True) + EPS)


def _layer_lb(hlb, layer):
    m = jnp.max(hlb, axis=0, keepdims=True)
    e = jnp.exp(hlb - m)
    p = e / jnp.sum(e, axis=0, keepdims=True)
    lb = jnp.zeros_like(m)
    for i in range(1, layer + 1):
        lb = lb + p[i:i + 1, :]
    return lb


def _split3_bf16(x):
    hi = x.astype(BF16)
    r = x - hi.astype(F32)
    mid = r.astype(BF16)
    return hi, mid, (r - mid.astype(F32)).astype(BF16)


def _select_rows_mxu(sel, x):
    return sum(jnp.dot(sel, part, preferred_element_type=F32) for part in _split3_bf16(x))


def _cumsum_rows_small(g):
    row = lax.broadcasted_iota(jnp.int32, g.shape, 0)
    b = jnp.zeros_like(g)
    for u in range(g.shape[0]):
        b = b + jnp.where(row >= u, g[u:u + 1, :], 0.0)
    return b


def _hgrn_span(proj_ref, mix_ref, mix_row0, st_ref, row0, lb, hng, tri, tick):
    L, KB = HGRN_CHUNK, HGRN_KEYBLOCK
    span = HGRN_SPAN * L
    g, k = _hgrn_gates(proj_ref[pl.ds(row0, span), OFF_F:OFF_F + D_HGRN], lb)
    b = _select_rows_mxu(tri, g)
    units = [(c, h) for c in range(HGRN_SPAN) for h in range(H_HGRN)]

    ops = {}
    for c, h in units:
        rows = pl.ds(row0 + c * L, L)
        cs = slice(h * DK_HGRN, (h + 1) * DK_HGRN)
        q = proj_ref[rows, OFF_Q + h * DK_HGRN:OFF_Q + (h + 1) * DK_HGRN]
        v = proj_ref[rows, OFF_I + h * DV_HGRN:OFF_I + (h + 1) * DV_HGRN].astype(BF16)
        kk = k[c * L:(c + 1) * L, cs]
        bb = b[c * L:(c + 1) * L, cs]
        qp, kp = [], []
        for lo in range(0, L, KB):
            r = bb[lo + KB // 2 - 1:lo + KB // 2, :]
            kp.append((kk[lo:lo + KB] * jnp.exp(r - bb[lo:lo + KB])).astype(BF16))
            qp.append((q[lo:] * jnp.exp(bb[lo:] - r)).astype(BF16))
        bl = bb[L - 1:L, :]
        ops[c, h] = dict(qp=qp, kp=kp, v=v, qt=(q * jnp.exp(bb)).astype(BF16),
                         kst=(kk * jnp.exp(bl - bb)).astype(BF16), e=jnp.exp(bl))
    tick()

    for n, u in enumerate(units):
        o = ops[u]
        o["p"] = [lax.dot_general(qp, kp, NT_DIMS, preferred_element_type=F32) for qp, kp in zip(o["qp"], o["kp"])]
        o["m"] = lax.dot_general(o["v"], o["kst"], TN_DIMS, preferred_element_type=F32)
        if n % H_HGRN == H_HGRN - 1:
            tick()

    for u in units:
        pm = []
        for p in ops[u]["p"]:
            row = lax.broadcasted_iota(jnp.int32, p.shape, 0)
            col = lax.broadcasted_iota(jnp.int32, p.shape, 1)
            pm.append(jnp.where(row >= col, p, 0.0).astype(BF16))
        ops[u]["p"] = pm
    tick()

    for n, u in enumerate(units):
        o = ops[u]
        blocks = [None] * (L // KB)
        for j, p in enumerate(o["p"]):
            cj = jnp.dot(p, o["v"][j * KB:(j + 1) * KB], preferred_element_type=F32)
            for i in range(j, L // KB):
                piece = cj[(i - j) * KB:(i - j + 1) * KB]
                blocks[i] = piece if blocks[i] is None else blocks[i] + piece
        o["o"] = jnp.concatenate(blocks, axis=0)
        if n % H_HGRN == H_HGRN - 1:
            tick()

    for h in range(H_HGRN):
        st = st_ref[h]
        for c in range(HGRN_SPAN):
            o = ops[c, h]
            out = o["o"] + lax.dot_general(o["qt"], st.astype(BF16), NT_DIMS, preferred_element_type=F32)
            st = o["e"] * st + o["m"]
            gate = proj_ref[pl.ds(row0 + c * L, L), OFF_G + h * DV_HGRN:OFF_G + (h + 1) * DV_HGRN]
            mix_ref[pl.ds(mix_row0 + row0 + c * L, L), D_CONV + h * DV_HGRN:D_CONV + (h + 1) * DV_HGRN] = (
                _hgrn_out(out, hng, gate).astype(BF16))
        st_ref[h] = st


def _sink_softmax(s, sink):
    m = jnp.maximum(jnp.max(s, axis=-1, keepdims=True), sink)
    p = jnp.exp(s - m)
    return p, jnp.sum(p, axis=-1, keepdims=True) + jnp.exp(sink - m)


def _bias_kernel(tab_ref, bp_ref, bs_ref, op_ref, os_ref, *, seq):
    bk = bp_ref[...]
    for h in range(H_ATTN):
        acc = jnp.full(bk.shape, -jnp.inf, F32)
        for bkt in range(NUM_BUCKETS):
            acc = jnp.where(bk == bkt, tab_ref[bkt, h], acc)
        op_ref[h] = acc
    bk = bs_ref[...]
    row = lax.broadcasted_iota(jnp.int32, bk.shape, 0)
    for kv in range(KV_HEADS):
        acc = jnp.full(bk.shape, -jnp.inf, F32)
        for bkt in range(NUM_BUCKETS):
            val = jnp.full(bk.shape, tab_ref[bkt, kv * GROUP], F32)
            for gi in range(1, GROUP):
                val = jnp.where(row >= gi * seq, tab_ref[bkt, kv * GROUP + gi], val)
            acc = jnp.where(bk == bkt, val, acc)
        os_ref[kv] = acc


def _t5_bucket(rel):
    n = jnp.maximum(rel, 0)
    max_exact = NUM_BUCKETS // 2
    nf = jnp.maximum(n, max_exact).astype(F32)
    large = max_exact + (jnp.log(nf / max_exact) / math.log(MAX_DISTANCE / max_exact)
                         * (NUM_BUCKETS - max_exact)).astype(jnp.int32)
    large = jnp.minimum(large, NUM_BUCKETS - 1)
    return jnp.where(n < max_exact, n, large)


def _bias_tables(rel_bias, dec_seq, w_buf):
    qi = jnp.arange(ATTN_BLOCK, dtype=jnp.int32)[:, None]
    kc = jnp.arange(2 * ATTN_BLOCK, dtype=jnp.int32)[None, :]
    rel_p = qi + ATTN_BLOCK - kc
    bucket_p = jnp.where((rel_p >= 0) & (rel_p <= WINDOW), _t5_bucket(rel_p), -1)
    ts = (jnp.arange(GROUP * dec_seq, dtype=jnp.int32) % dec_seq)[:, None]
    js = jnp.arange(2 * ATTN_BLOCK, dtype=jnp.int32)[None, :]
    rel_s = w_buf + ts - js
    ok_s = (rel_s >= 0) & (rel_s <= WINDOW) & (js < w_buf + dec_seq)
    bucket_s = jnp.where(ok_s, _t5_bucket(rel_s), -1)
    return pl.pallas_call(
        functools.partial(_bias_kernel, seq=dec_seq),
        out_shape=(jax.ShapeDtypeStruct((H_ATTN, ATTN_BLOCK, 2 * ATTN_BLOCK), F32),
                   jax.ShapeDtypeStruct((KV_HEADS, GROUP * dec_seq, 2 * ATTN_BLOCK), F32)),
        in_specs=[pl.BlockSpec(memory_space=pltpu.SMEM),
                  pl.BlockSpec(memory_space=pltpu.VMEM),
                  pl.BlockSpec(memory_space=pltpu.VMEM)],
        out_specs=(pl.BlockSpec(memory_space=pltpu.VMEM), pl.BlockSpec(memory_space=pltpu.VMEM)),
        name="rel_bias_tables",
    )(rel_bias.astype(F32), bucket_p, bucket_s)


def _mod_kernel(c_ref, w_ref, b_ref, o_ref):
    s = _silu(c_ref[...]).astype(BF16)
    o_ref[...] = jnp.dot(s, w_ref[...].astype(BF16), preferred_element_type=F32) + b_ref[...]


def _modulation(c_all, w_ada, b_ada):
    depth = w_ada.shape[0]
    n = c_all.shape[0]
    return pl.pallas_call(
        _mod_kernel,
        out_shape=jax.ShapeDtypeStruct((depth, n, N_MOD * D_MODEL), F32),
        grid=(depth, N_MOD),
        in_specs=[pl.BlockSpec((n, D_MODEL), lambda l, j: (0, 0)),
                  pl.BlockSpec((None, D_MODEL, D_MODEL), lambda l, j: (l, 0, j)),
                  pl.BlockSpec((None, 1, D_MODEL), lambda l, j: (l, 0, j))],
        out_specs=pl.BlockSpec((None, n, D_MODEL), lambda l, j: (l, 0, j)),
        compiler_params=pltpu.CompilerParams(dimension_semantics=("arbitrary", "arbitrary"),
                                             vmem_limit_bytes=VMEM_LIMIT),
        name="adaln_modulation",
    )(c_all, w_ada, b_ada.reshape(depth, 1, N_MOD * D_MODEL))


def _mod_rows(m, n_tokens):
    if m.shape[0] == 1:
        return m
    reps = n_tokens // m.shape[0]
    tok = lax.broadcasted_iota(jnp.int32, (n_tokens, m.shape[0]), 0)
    bat = lax.broadcasted_iota(jnp.int32, (n_tokens, m.shape[0]), 1)
    sel = jnp.where((tok >= bat * reps) & (tok < (bat + 1) * reps), 1.0, 0.0).astype(BF16)
    return _select_rows_mxu(sel, m)


def _modulated_norm(x, g, sc, sh):
    n = x.shape[0]
    return (_rms_rows(x) * g * (1.0 + _mod_rows(sc, n)) + _mod_rows(sh, n)).astype(BF16)


def _sample_mod_specs(mod_s, chunks, index_map):
    return [pl.BlockSpec((mod_s.shape[0], D_MODEL), functools.partial(index_map, chunk=c),
                         pipeline_mode=pl.Buffered(1)) for c in chunks]


def _mlp_tile(mix, x, g1, sh, sc, g2, ng, wout_ref, wup_ref, wdn_ref, fg, final):
    n = x.shape[0]
    x1 = x + _mod_rows(g1, n) * jnp.dot(mix.astype(BF16), wout_ref[...], preferred_element_type=F32)
    h = _modulated_norm(x1, ng, sc, sh)
    acc = None
    for c in range(D_FF // FF_CHUNK):
        u = jnp.dot(h, wup_ref[:, c * FF_CHUNK:(c + 1) * FF_CHUNK], preferred_element_type=F32)
        u = jnp.square(jnp.maximum(u, 0.0)).astype(BF16)
        d = jnp.dot(u, wdn_ref[c * FF_CHUNK:(c + 1) * FF_CHUNK, :], preferred_element_type=F32)
        acc = d if acc is None else acc + d
    x2 = x1 + _mod_rows(g2, n) * acc
    return _rms_rows(x2) * fg if final else x2


def _mlp_kernel(mix_ref, x_ref, g1_ref, sh_ref, sc_ref, g2_ref, smix_ref, sx_ref, sg1_ref, ssh_ref, ssc_ref, sg2_ref,
                ng_ref, wout_ref, wup_ref, wdn_ref, fg_ref, o_ref, so_ref, *, final):
    weights = (ng_ref[...], wout_ref, wup_ref, wdn_ref, fg_ref[...], final)
    o_ref[...] = _mlp_tile(mix_ref[...], x_ref[...], g1_ref[...], sh_ref[...], sc_ref[...], g2_ref[...], *weights)

    @pl.when(pl.program_id(0) == pl.num_programs(0) - 1)
    def _():
        so_ref[...] = _mlp_tile(smix_ref[...], sx_ref[...], sg1_ref[...], ssh_ref[...], ssc_ref[...], sg2_ref[...],
                                *weights)


def _out_mlp(mix2, x2, mod, mix_s, xs, mod_s, norm_g, w_out, w_up, w_down, final_g, layer, tiles_per_batch, final):
    n = x2.shape[0]
    tile = TOK_TILE
    const = lambda i: (0, 0)
    of_layer = lambda i: (layer, 0, 0)
    mod_p = lambda chunk: pl.BlockSpec((None, 1, D_MODEL), lambda i: (i // tiles_per_batch, 0, chunk))
    whole = lambda a: pl.BlockSpec(a.shape, const, pipeline_mode=pl.Buffered(1))
    return pl.pallas_call(
        functools.partial(_mlp_kernel, final=final),
        out_shape=(jax.ShapeDtypeStruct((n, D_MODEL), F32), jax.ShapeDtypeStruct(xs.shape, F32)),
        grid=(n // tile,),
        in_specs=[pl.BlockSpec((tile, D_MODEL), lambda i: (i, 0)),
                  pl.BlockSpec((tile, D_MODEL), lambda i: (i, 0)),
                  mod_p(2), mod_p(3), mod_p(4), mod_p(5),
                  whole(mix_s), whole(xs)] + _sample_mod_specs(mod_s, (2, 3, 4, 5), lambda i, chunk: (0, chunk)) + [
                  pl.BlockSpec((1, D_MODEL), const),
                  pl.BlockSpec((None, D_MODEL, D_MODEL), of_layer, pipeline_mode=pl.Buffered(1)),
                  pl.BlockSpec((None, D_MODEL, D_FF), of_layer, pipeline_mode=pl.Buffered(1)),
                  pl.BlockSpec((None, D_FF, D_MODEL), of_layer, pipeline_mode=pl.Buffered(1)),
                  pl.BlockSpec((1, D_MODEL), const)],
        out_specs=(pl.BlockSpec((tile, D_MODEL), lambda i: (i, 0)), pl.BlockSpec(xs.shape, const)),
        compiler_params=pltpu.CompilerParams(dimension_semantics=("arbitrary",), vmem_limit_bytes=VMEM_LIMIT),
        name="out_projection_mlp",
    )(mix2, x2, mod, mod, mod, mod, mix_s, xs, mod_s, mod_s, mod_s, mod_s, norm_g.reshape(1, D_MODEL), w_out, w_up,
      w_down, final_g.reshape(1, D_MODEL))


def _conv_ln_swish(acc, lng, lnb):
    mu = jnp.mean(acc, axis=-1, keepdims=True)
    xc = acc - mu
    y = xc * lax.rsqrt(jnp.mean(xc * xc, axis=-1, keepdims=True) + EPS) * lng + lnb
    return _silu(y)


def _hgrn_gates(fh, lb):
    f = lb + (1.0 - lb) * jax.nn.sigmoid(fh)
    return jnp.log(f), 1.0 - f


def _hgrn_out(o, hng, gate):
    return _rms_rows(o) * hng * _silu(gate)


def _ticker(pieces):
    it = iter(pieces)

    def tick():
        piece = next(it, None)
        if piece is not None:
            piece()

    def flush():
        for piece in it:
            piece()

    tick.flush = flush
    return tick


def _mixer_tile(proj_ref, mix_ref, mix_row0, first_tile, pieces, sinks_ref, convw_ref, convb_ref, lng_ref, lnb_ref,
                bias_ref, lb, hng, tri, abuf, kbuf, vbuf, st_ref, tile):
    kbuf[ATTN_BLOCK:ATTN_BLOCK + tile, :] = proj_ref[:, OFF_KA:OFF_KA + D_KV].astype(BF16)
    vbuf[ATTN_BLOCK:ATTN_BLOCK + tile, :] = proj_ref[:, OFF_VA:OFF_VA + D_KV].astype(BF16)
    scale = HEAD_DIM ** -0.5
    attn = {}

    def attn_scores(blk):
        def run():
            r0 = blk * ATTN_BLOCK
            for h in range(H_ATTN):
                kv = h // GROUP
                q = (proj_ref[r0:r0 + ATTN_BLOCK, OFF_QA + h * HEAD_DIM:OFF_QA + (h + 1) * HEAD_DIM]
                     * scale).astype(BF16)
                kall = kbuf[r0:r0 + 2 * ATTN_BLOCK, kv * HEAD_DIM:(kv + 1) * HEAD_DIM]
                attn[blk, h] = lax.dot_general(q, kall, NT_DIMS, preferred_element_type=F32)
        return run

    def attn_softmax(blk):
        def run():
            for h in range(H_ATTN):
                s = attn[blk, h] + bias_ref[h]
                if blk == 0 and first_tile is not False:
                    col = lax.broadcasted_iota(jnp.int32, s.shape, 1)
                    s = jnp.where((col >= ATTN_BLOCK) | jnp.logical_not(first_tile), s, -jnp.inf)
                p, den = _sink_softmax(s, sinks_ref[h])
                attn[blk, h] = (p.astype(BF16), den)
        return run

    def attn_values(blk):
        def run():
            r0 = blk * ATTN_BLOCK
            heads = []
            for h in range(H_ATTN):
                kv = h // GROUP
                p, den = attn[blk, h]
                vall = vbuf[r0:r0 + 2 * ATTN_BLOCK, kv * HEAD_DIM:(kv + 1) * HEAD_DIM]
                heads.append(jnp.dot(p, vall, preferred_element_type=F32) / den)
            mix_ref[mix_row0 + r0:mix_row0 + r0 + ATTN_BLOCK, D_CONV + D_HGRN:D_MODEL] = (
                jnp.concatenate(heads, axis=1).astype(BF16))
        return run

    stages = [stage(blk) for blk in range(tile // ATTN_BLOCK) for stage in (attn_scores, attn_softmax, attn_values)]
    per_stage = -(-len(pieces) // len(stages))
    merged = []
    for i, stage in enumerate(stages):
        merged += pieces[i * per_stage:(i + 1) * per_stage] + [stage]
    tick = _ticker(merged)

    abuf[CONV_PAD:CONV_PAD + tile, :] = (proj_ref[:, OFF_AVAL:OFF_AVAL + D_CONV]
                                         * jax.nn.sigmoid(proj_ref[:, OFF_AGATE:OFF_AGATE + D_CONV]))
    first_row = CONV_PAD - (CONV_WIDTH - 1)
    acc = jnp.broadcast_to(convb_ref[...], (tile, D_CONV))
    for r in range(SUBLANES):
        z = None
        for off in range(r, first_row + CONV_WIDTH, SUBLANES):
            j = off - first_row
            if j < 0:
                continue
            term = convw_ref[j:j + 1, :] * abuf[off - r:off - r + tile + SUBLANES, :]
            z = term if z is None else z + term
        acc = acc + (z[0:tile] if r == 0 else pltpu.roll(z, tile + SUBLANES - r, 0)[0:tile])
        tick()
    mix_ref[mix_row0:mix_row0 + tile, 0:D_CONV] = _conv_ln_swish(acc, lng_ref[...], lnb_ref[...]).astype(BF16)
    tick()

    span = HGRN_SPAN * HGRN_CHUNK
    for i in range(tile // span):
        _hgrn_span(proj_ref, mix_ref, mix_row0, st_ref, i * span, lb, hng, tri, tick)
    tick.flush()

    abuf[0:CONV_PAD, :] = abuf[tile:tile + CONV_PAD, :]
    kbuf[0:ATTN_BLOCK, :] = kbuf[tile:tile + ATTN_BLOCK, :]
    vbuf[0:ATTN_BLOCK, :] = vbuf[tile:tile + ATTN_BLOCK, :]


def _prompt_mix_kernel(sinks_ref, x_ref, xn_ref, sh_ref, sc_ref, sx_ref, ssh_ref, ssc_ref, ng_ref, win_ref, convw_ref,
                       convb_ref, lng_ref, lnb_ref, hlb_ref, hng_ref, bias_ref,
                       mix_ref, sproj_ref, convo_ref, so_ref, ko_ref, vo_ref,
                       proj_a, proj_b, abuf, kbuf, vbuf, st_ref, *, layer, tile):
    t = pl.program_id(1)
    last = pl.num_programs(1) - 1

    def norm(x):
        return _modulated_norm(x, ng_ref[...], sc_ref[...], sh_ref[...])

    @pl.when(t == 0)
    def _():
        abuf[0:CONV_PAD, :] = jnp.zeros((CONV_PAD, D_CONV), F32)
        abuf[CONV_PAD + tile:CONV_PAD + tile + SUBLANES, :] = jnp.zeros((SUBLANES, D_CONV), F32)
        kbuf[0:ATTN_BLOCK, :] = jnp.zeros((ATTN_BLOCK, D_KV), BF16)
        vbuf[0:ATTN_BLOCK, :] = jnp.zeros((ATTN_BLOCK, D_KV), BF16)
        st_ref[...] = jnp.zeros(st_ref.shape, F32)
        proj_a[...] = jnp.dot(norm(x_ref[0:tile, :]), win_ref[...], preferred_element_type=F32)

    lb = _layer_lb(hlb_ref[...], layer)
    hng = hng_ref[...]
    span = HGRN_SPAN * HGRN_CHUNK
    ri = lax.broadcasted_iota(jnp.int32, (span, span), 0)
    ci = lax.broadcasted_iota(jnp.int32, (span, span), 1)
    tri = jnp.where((ri >= ci) & (ri // HGRN_CHUNK == ci // HGRN_CHUNK), 1.0, 0.0).astype(BF16)

    def in_projection(h, dst, lo):
        def run():
            dst[:, lo:lo + INPROJ_PIECE] = jnp.dot(h, win_ref[:, lo:lo + INPROJ_PIECE], preferred_element_type=F32)
        return run

    for half, (cur, nxt) in enumerate(((proj_a, proj_b), (proj_b, proj_a))):
        h_next = norm(x_ref[tile:2 * tile, :] if half == 0 else xn_ref[...])
        pieces = [in_projection(h_next, nxt, lo) for lo in range(0, IN_WIDTH, INPROJ_PIECE)]
        _mixer_tile(cur, mix_ref, half * tile, (t == 0) if half == 0 else False, pieces, sinks_ref, convw_ref,
                    convb_ref, lng_ref, lnb_ref, bias_ref, lb, hng, tri, abuf, kbuf, vbuf, st_ref, tile)

    @pl.when(t == last)
    def _():
        convo_ref[...] = abuf[CONV_PAD - (CONV_WIDTH - 1):CONV_PAD, :]
        for h in range(H_HGRN):
            so_ref[h] = st_ref[h].T
        ko_ref[...] = proj_b[tile - WINDOW:tile, OFF_KA:OFF_KA + D_KV]
        vo_ref[...] = proj_b[tile - WINDOW:tile, OFF_VA:OFF_VA + D_KV]

    @pl.when((pl.program_id(0) == pl.num_programs(0) - 1) & (t == last))
    def _():
        h_s = _modulated_norm(sx_ref[...], ng_ref[...], ssc_ref[...], ssh_ref[...])
        sproj_ref[...] = jnp.dot(h_s, win_ref[...], preferred_element_type=F32)


def _carry_specs(carried):
    return [pl.BlockSpec(memory_space=pl.ANY)] * len(carried)


def _carried(kernel_fn, n_in, n_carried):
    if n_carried == 0:
        return kernel_fn
    return lambda *refs: kernel_fn(*refs[:n_in], *refs[n_in + n_carried:])


def _prompt_mixers(x, mod, xs, mod_s, norm_g, w_in, sinks, conv_w, conv_b, ln_g, ln_b, hgrn_lb, hng, bias_p, layer,
                   carried):
    B, T = x.shape[:2]
    tile = MIX_TILE
    n_tiles = T // tile
    depth = hgrn_lb.shape[0]
    const2 = lambda b, t: (0, 0)
    inputs = (sinks, x, x, mod, mod, xs, mod_s, mod_s, norm_g.reshape(1, D_MODEL), w_in, conv_w,
              conv_b.reshape(1, D_CONV), ln_g.reshape(1, D_CONV), ln_b.reshape(1, D_CONV), hgrn_lb,
              hng.reshape(1, DV_HGRN), bias_p)
    return pl.pallas_call(
        _carried(functools.partial(_prompt_mix_kernel, layer=layer, tile=tile), len(inputs), len(carried)),
        out_shape=(jax.ShapeDtypeStruct((B, T, D_MODEL), BF16),
                   jax.ShapeDtypeStruct((xs.shape[0], IN_WIDTH), F32),
                   jax.ShapeDtypeStruct((depth, B, CONV_WIDTH - 1, D_CONV), F32),
                   jax.ShapeDtypeStruct((depth, B, H_HGRN, DK_HGRN, DV_HGRN), F32),
                   jax.ShapeDtypeStruct((depth, B, WINDOW, D_KV), F32),
                   jax.ShapeDtypeStruct((depth, B, WINDOW, D_KV), F32)),
        grid=(B, n_tiles // 2),
        in_specs=[pl.BlockSpec(memory_space=pltpu.SMEM),
                  pl.BlockSpec((None, 2 * tile, D_MODEL), lambda b, t: (b, t, 0)),
                  pl.BlockSpec((None, tile, D_MODEL), lambda b, t: (b, jnp.minimum(2 * t + 2, n_tiles - 1), 0)),
                  pl.BlockSpec((None, 1, D_MODEL), lambda b, t: (b, 0, 0)),
                  pl.BlockSpec((None, 1, D_MODEL), lambda b, t: (b, 0, 1)),
                  pl.BlockSpec(xs.shape, const2, pipeline_mode=pl.Buffered(1))]
                 + _sample_mod_specs(mod_s, (0, 1), lambda b, t, chunk: (0, chunk)) + [
                  pl.BlockSpec((1, D_MODEL), const2),
                  pl.BlockSpec((None, D_MODEL, IN_WIDTH), lambda b, t: (layer, 0, 0), pipeline_mode=pl.Buffered(1)),
                  pl.BlockSpec((CONV_WIDTH, D_CONV), const2),
                  pl.BlockSpec((1, D_CONV), const2),
                  pl.BlockSpec((1, D_CONV), const2),
                  pl.BlockSpec((1, D_CONV), const2),
                  pl.BlockSpec((depth, D_HGRN), const2),
                  pl.BlockSpec((1, DV_HGRN), const2),
                  pl.BlockSpec((H_ATTN, ATTN_BLOCK, 2 * ATTN_BLOCK), lambda b, t: (0, 0, 0))] + _carry_specs(carried),
        out_specs=(pl.BlockSpec((None, 2 * tile, D_MODEL), lambda b, t: (b, t, 0)),
                   pl.BlockSpec((xs.shape[0], IN_WIDTH), const2),
                   pl.BlockSpec((None, None, CONV_WIDTH - 1, D_CONV), lambda b, t: (layer, b, 0, 0)),
                   pl.BlockSpec((None, None, H_HGRN, DK_HGRN, DV_HGRN), lambda b, t: (layer, b, 0, 0, 0)),
                   pl.BlockSpec((None, None, WINDOW, D_KV), lambda b, t: (layer, b, 0, 0)),
                   pl.BlockSpec((None, None, WINDOW, D_KV), lambda b, t: (layer, b, 0, 0))),
        input_output_aliases={len(inputs) + i: 2 + i for i in range(len(carried))},
        scratch_shapes=[pltpu.VMEM((tile, IN_WIDTH), F32),
                        pltpu.VMEM((tile, IN_WIDTH), F32),
                        pltpu.VMEM((CONV_PAD + tile + SUBLANES, D_CONV), F32),
                        pltpu.VMEM((ATTN_BLOCK + tile, D_KV), BF16),
                        pltpu.VMEM((ATTN_BLOCK + tile, D_KV), BF16),
                        pltpu.VMEM((H_HGRN, DV_HGRN, DK_HGRN), F32)],
        compiler_params=pltpu.CompilerParams(dimension_semantics=("arbitrary", "arbitrary"),
                                             vmem_limit_bytes=VMEM_LIMIT),
        name="prompt_mixers",
    )(*inputs, *carried)


def _sample_mix_kernel(sinks_ref, proj_ref, cconv_ref, state_ref, ck_ref, cv_ref, convw_ref, convb_ref, lng_ref,
                       lnb_ref, hlb_ref, hng_ref, bias_ref,
                       mix_ref, convo_ref, so_ref, ko_ref, vo_ref, kpad_ref, vpad_ref, *,
                       layer, block, seq, w_buf):
    hist = CONV_WIDTH - 1

    @pl.when(pl.program_id(0) == 0)
    def _():
        for ref in (kpad_ref, vpad_ref):
            ref[:, 0:w_buf - seq, :] = jnp.zeros((block, w_buf - seq, D_KV), F32)

    lb = _layer_lb(hlb_ref[...], layer)
    hng = hng_ref[...]
    scale = HEAD_DIM ** -0.5
    elems = range(block)
    row8 = lax.broadcasted_iota(jnp.int32, (SUBLANES, DV_HGRN), 0)
    ones_rows = jnp.where((row8 >= seq) & (row8 < seq + 3), 1.0, 0.0)
    zrow = jnp.zeros((1, DK_HGRN), BF16)
    prow = lax.broadcasted_iota(jnp.int32, (seq, seq), 0)
    pcol = lax.broadcasted_iota(jnp.int32, (seq, seq), 1)
    grow = lax.broadcasted_iota(jnp.int32, (GROUP * seq, 1), 0)

    proj = [proj_ref[e * seq:(e + 1) * seq, :] for e in elems]

    glu = [p[:, OFF_AVAL:OFF_AVAL + D_CONV] * jax.nn.sigmoid(p[:, OFF_AGATE:OFF_AGATE + D_CONV]) for p in proj]
    full = [cconv_ref[i] for i in range(hist)]
    full += [jnp.concatenate([glu[e][t:t + 1] for e in elems], axis=0) for t in range(seq)]
    for i in range(hist):
        convo_ref[i] = full[i + seq]
    conv_out = []
    for t in range(seq):
        acc = jnp.broadcast_to(convb_ref[...], (block, D_CONV))
        for j in range(CONV_WIDTH):
            acc = acc + convw_ref[j:j + 1, :] * full[t + j]
        conv_out.append(_conv_ln_swish(acc, lng_ref[...], lnb_ref[...]))
    out_a = [jnp.concatenate([conv_out[t][e:e + 1] for t in range(seq)], axis=0) for e in elems]

    units = [(e, h) for e in elems for h in range(H_HGRN)]
    ops = {}
    for e in elems:
        p = proj[e]
        g, k = _hgrn_gates(p[:, OFF_F:OFF_F + D_HGRN], lb)
        b = _cumsum_rows_small(g)
        for h in range(H_HGRN):
            cs = slice(h * DK_HGRN, (h + 1) * DK_HGRN)
            q = p[:, OFF_Q + h * DK_HGRN:OFF_Q + (h + 1) * DK_HGRN]
            v = p[:, OFF_I + h * DV_HGRN:OFF_I + (h + 1) * DV_HGRN]
            bb = b[:, cs]
            bl = bb[seq - 1:seq, :]
            kst = (k[:, cs] * jnp.exp(bl - bb)).astype(BF16)
            x = jnp.concatenate([kst.astype(F32), *(part.astype(F32) for part in _split3_bf16(jnp.exp(bl))),
                                 zrow.astype(F32)], axis=0).astype(BF16)
            vpad = jnp.concatenate([v, jnp.zeros((SUBLANES - seq, DV_HGRN), F32)], axis=0)
            ops[e, h] = dict(qp=(q * jnp.exp(bb - bl)).astype(BF16), kst=kst, v=v.astype(BF16),
                             qt=(q * jnp.exp(bb)).astype(BF16), x=x,
                             r=jnp.concatenate([vpad, ones_rows], axis=1).astype(BF16))
    for u in units:
        o = ops[u]
        st = state_ref[u[0], u[1]]
        o["p"] = lax.dot_general(o["qp"], o["kst"], NT_DIMS, preferred_element_type=F32)
        o["inter"] = jnp.dot(o["qt"], st.astype(BF16), preferred_element_type=F32)
        me = lax.dot_general(o["x"], o["r"], TN_DIMS, preferred_element_type=F32)
        so_ref[u[0], u[1]] = me[:, DV_HGRN:] * st + me[:, :DV_HGRN]
    out_b = {}
    for u in units:
        o = ops[u]
        pm = jnp.where(prow >= pcol, o["p"], 0.0).astype(BF16)
        out = o["inter"] + jnp.dot(pm, o["v"], preferred_element_type=F32)
        gate = proj[u[0]][:, OFF_G + u[1] * DV_HGRN:OFF_G + (u[1] + 1) * DV_HGRN]
        out_b[u] = _hgrn_out(out, hng, gate)

    scores = {}
    for e in elems:
        p = proj[e]
        for kv in range(KV_HEADS):
            hs = slice(kv * HEAD_DIM, (kv + 1) * HEAD_DIM)
            q2 = jnp.concatenate([p[:, OFF_QA + h * HEAD_DIM:OFF_QA + (h + 1) * HEAD_DIM]
                                  for h in range(kv * GROUP, (kv + 1) * GROUP)], axis=0)
            q2 = (q2 * scale).astype(BF16)
            bias = bias_ref[kv]
            s_c = jnp.dot(q2, ck_ref[e, hs, :].astype(BF16), preferred_element_type=F32) + bias[:, 0:w_buf]
            s_n = (lax.dot_general(q2, p[:, OFF_KA + kv * HEAD_DIM:OFF_KA + (kv + 1) * HEAD_DIM].astype(BF16), NT_DIMS,
                                   preferred_element_type=F32) + bias[:, w_buf:w_buf + seq])
            scores[e, kv] = (s_c, s_n)
    out_c = {}
    for e in elems:
        p = proj[e]
        for kv in range(KV_HEADS):
            hs = slice(kv * HEAD_DIM, (kv + 1) * HEAD_DIM)
            sink = jnp.zeros((GROUP * seq, 1), F32)
            for gi in range(GROUP):
                sink = jnp.where(grow >= gi * seq, sinks_ref[kv * GROUP + gi], sink)
            s_c, s_n = scores[e, kv]
            m = jnp.maximum(jnp.maximum(jnp.max(s_c, axis=-1, keepdims=True), jnp.max(s_n, axis=-1, keepdims=True)),
                            sink)
            p_c = jnp.exp(s_c - m)
            p_n = jnp.exp(s_n - m)
            den = jnp.sum(p_c, axis=-1, keepdims=True) + jnp.sum(p_n, axis=-1, keepdims=True) + jnp.exp(sink - m)
            vnew = p[:, OFF_VA + kv * HEAD_DIM:OFF_VA + (kv + 1) * HEAD_DIM].astype(BF16)
            o2 = (lax.dot_general(p_c.astype(BF16), cv_ref[e, hs, :].astype(BF16), NT_DIMS,
                                  preferred_element_type=F32)
                  + jnp.dot(p_n.astype(BF16), vnew, preferred_element_type=F32)) / den
            for gi in range(GROUP):
                out_c[e, kv * GROUP + gi] = o2[gi * seq:(gi + 1) * seq]
    lane = lax.broadcasted_iota(jnp.int32, (D_KV, w_buf), 1)
    for e in elems:
        p = proj[e]
        for pad, cache, new, out in ((kpad_ref, ck_ref, p[:, OFF_KA:OFF_KA + D_KV], ko_ref),
                                     (vpad_ref, cv_ref, p[:, OFF_VA:OFF_VA + D_KV], vo_ref)):
            pad[e, w_buf - seq:w_buf, :] = new
            out[e] = jnp.where(lane >= w_buf - seq, pad[e].T, pltpu.roll(cache[e], w_buf - seq, 1))

    for e in elems:
        parts = [out_a[e]] + [out_b[e, h] for h in range(H_HGRN)] + [out_c[e, h] for h in range(H_ATTN)]
        mix_ref[e * seq:(e + 1) * seq, :] = jnp.concatenate(parts, axis=1)


def _sample_mixers(proj2, sinks, cache_conv, state, cache_k, cache_v, conv_w, conv_b, ln_g, ln_b, hgrn_lb, hng,
                   bias_s, layer, carried):
    B = state.shape[1]
    seq = proj2.shape[0] // B
    w_buf = cache_k.shape[3]
    block = SAMPLE_BLOCK
    depth = hgrn_lb.shape[0]
    hist = CONV_WIDTH - 1
    const2 = lambda i: (0, 0)
    cache_specs = [pl.BlockSpec((None, hist, block, D_CONV), lambda i: (layer, 0, i, 0)),
                   pl.BlockSpec((None, block, H_HGRN, DK_HGRN, DV_HGRN), lambda i: (layer, i, 0, 0, 0)),
                   pl.BlockSpec((None, block, D_KV, w_buf), lambda i: (layer, i, 0, 0)),
                   pl.BlockSpec((None, block, D_KV, w_buf), lambda i: (layer, i, 0, 0))]
    inputs = (sinks, proj2, cache_conv, state, cache_k, cache_v, conv_w, conv_b.reshape(1, D_CONV),
              ln_g.reshape(1, D_CONV), ln_b.reshape(1, D_CONV), hgrn_lb, hng.reshape(1, DV_HGRN), bias_s)
    return pl.pallas_call(
        _carried(functools.partial(_sample_mix_kernel, layer=layer, block=block, seq=seq, w_buf=w_buf),
                 len(inputs), len(carried)),
        out_shape=(jax.ShapeDtypeStruct((B * seq, D_MODEL), F32),
                   jax.ShapeDtypeStruct((depth, hist, B, D_CONV), F32),
                   jax.ShapeDtypeStruct((depth, B, H_HGRN, DK_HGRN, DV_HGRN), F32),
                   jax.ShapeDtypeStruct((depth, B, D_KV, w_buf), F32),
                   jax.ShapeDtypeStruct((depth, B, D_KV, w_buf), F32)),
        grid=(B // block,),
        in_specs=[pl.BlockSpec(memory_space=pltpu.SMEM),
                  pl.BlockSpec((block * seq, IN_WIDTH), lambda i: (i, 0))] + cache_specs + [
                  pl.BlockSpec((CONV_WIDTH, D_CONV), const2),
                  pl.BlockSpec((1, D_CONV), const2),
                  pl.BlockSpec((1, D_CONV), const2),
                  pl.BlockSpec((1, D_CONV), const2),
                  pl.BlockSpec((depth, D_HGRN), const2),
                  pl.BlockSpec((1, DV_HGRN), const2),
                  pl.BlockSpec((KV_HEADS, GROUP * seq, 2 * ATTN_BLOCK), lambda i: (0, 0, 0))] + _carry_specs(carried),
        out_specs=tuple([pl.BlockSpec((block * seq, D_MODEL), lambda i: (i, 0))] + cache_specs),
        input_output_aliases={len(inputs) + i: 1 + i for i in range(len(carried))},
        scratch_shapes=[pltpu.VMEM((block, w_buf, D_KV), F32),
                        pltpu.VMEM((block, w_buf, D_KV), F32)],
        compiler_params=pltpu.CompilerParams(dimension_semantics=("arbitrary",), vmem_limit_bytes=VMEM_LIMIT),
        name="sample_mixers",
    )(*inputs, *carried)


def kernel(x_prompt, x_sample, cache_conv, state_hgrn, cache_swa_k, cache_swa_v, c_prompt, c_sample, rel_bias, w_ada, b_ada, norm_mix_g, w_in, conv_w, conv_b, conv_ln_g, conv_ln_b, hgrn_lb, hgrn_norm_g, attn_sinks, w_out, norm_mlp_g, w_up, w_down, final_g):
    Bp, Tp = x_prompt.shape[:2]
    Bs, Ts = x_sample.shape[:2]
    depth = w_in.shape[0]
    w_buf = cache_swa_k.shape[2]
    assert Tp % (2 * MIX_TILE) == 0 and (Bp * Tp) % TOK_TILE == 0 and Tp % TOK_TILE == 0 and Bs % SAMPLE_BLOCK == 0
    assert w_buf == WINDOW and GROUP * Ts == SUBLANES

    bias_p, bias_s = _bias_tables(rel_bias, Ts, w_buf)
    mod = _modulation(jnp.concatenate([c_prompt, c_sample], axis=0), w_ada, b_ada)
    w_in_b, w_out_b, w_up_b, w_down_b = (w.astype(BF16) for w in (w_in, w_out, w_up, w_down))
    hlb = hgrn_lb.astype(F32)
    cc = jnp.swapaxes(cache_conv, 1, 2)
    ck = jnp.swapaxes(cache_swa_k.reshape(depth, Bs, w_buf, D_KV), 2, 3)
    cv = jnp.swapaxes(cache_swa_v.reshape(depth, Bs, w_buf, D_KV), 2, 3)

    xp = x_prompt.reshape(Bp * Tp, D_MODEL)
    xs = x_sample.reshape(Bs * Ts, D_MODEL)
    caches_p = ()
    caches_s = ()
    for l in range(depth):
        final = l == depth - 1
        mod_p = mod[l, :Bp].reshape(Bp, 1, N_MOD * D_MODEL)
        mod_s = mod[l, Bp:]
        mix_p, proj_s, *caches_p = _prompt_mixers(xp.reshape(Bp, Tp, D_MODEL), mod_p, xs, mod_s, norm_mix_g[l], w_in_b,
                                                  attn_sinks[l], conv_w[l], conv_b[l], conv_ln_g[l], conv_ln_b[l], hlb,
                                                  hgrn_norm_g[l], bias_p, l, caches_p)
        mix_s, *caches_s = _sample_mixers(proj_s, attn_sinks[l], cc, state_hgrn, ck, cv, conv_w[l],
                                          conv_b[l], conv_ln_g[l], conv_ln_b[l], hlb, hgrn_norm_g[l], bias_s, l,
                                          caches_s)
        xp, xs = _out_mlp(mix_p.reshape(Bp * Tp, D_MODEL), xp, mod_p, mix_s, xs, mod_s, norm_mlp_g[l], w_out_b,
                          w_up_b, w_down_b, final_g, l, Tp // TOK_TILE, final)
    cp, sp, kp, vp = caches_p
    cs, ss, ksn, vsn = caches_s
    cs, ksn, vsn = jnp.swapaxes(cs, 1, 2), jnp.swapaxes(ksn, 2, 3), jnp.swapaxes(vsn, 2, 3)
    return (xp.reshape(Bp, Tp, D_MODEL), xs.reshape(Bs, Ts, D_MODEL), cp, cs, sp, ss,
            kp.reshape(depth, Bp, WINDOW, KV_HEADS, HEAD_DIM), ksn.reshape(depth, Bs, w_buf, KV_HEADS, HEAD_DIM),
            vp.reshape(depth, Bp, WINDOW, KV_HEADS, HEAD_DIM), vsn.reshape(depth, Bs, w_buf, KV_HEADS, HEAD_DIM))
```

```python
import functools
import math

import jax
import jax.numpy as jnp
from jax import lax
from jax.experimental import pallas as pl
from jax.experimental.pallas import tpu as pltpu

F32 = jnp.float32
BF16 = jnp.bfloat16

D_MODEL = 1024
D_CONV = 256
CONV_WIDTH = 31
H_HGRN = 4
DK_HGRN = 128
DV_HGRN = 128
D_HGRN = 512
HEAD_DIM = 64
H_ATTN = 4
KV_HEADS = 2
GROUP = H_ATTN // KV_HEADS
D_ATTN = H_ATTN * HEAD_DIM
D_KV = KV_HEADS * HEAD_DIM
WINDOW = 128
ATTN_BLOCK = 128
NUM_BUCKETS = 32
MAX_DISTANCE = 128
D_FF = 4 * D_MODEL
N_MOD = 6
EPS = 1e-6

OFF_AVAL = 0
OFF_AGATE = OFF_AVAL + D_CONV
OFF_Q = OFF_AGATE + D_CONV
OFF_F = OFF_Q + H_HGRN * DK_HGRN
OFF_I = OFF_F + H_HGRN * DK_HGRN
OFF_G = OFF_I + D_HGRN
OFF_QA = OFF_G + D_HGRN
OFF_KA = OFF_QA + D_ATTN
OFF_VA = OFF_KA + D_KV
IN_WIDTH = OFF_VA + D_KV

HGRN_CHUNK = 64
HGRN_KEYBLOCK = 32
HGRN_SPAN = 4
SUBLANES = 8
CONV_PAD = 32
MIX_TILE = 512
TOK_TILE = 512
SAMPLE_BLOCK = 16
FF_CHUNK = 1024
MOD_COLS = 2048
VMEM_LIMIT = 56 * 1024 * 1024

NT_DIMS = (((1,), (1,)), ((), ()))
TN_DIMS = (((0,), (0,)), ((), ()))


def _silu(x):
    return x * jax.nn.sigmoid(x)


def _rms_rows(x):
    return x * lax.rsqrt(jnp.mean(x * x, axis=-1, keepdims=True) + EPS)


def _layer_lb(hlb, layer):
    m = jnp.max(hlb, axis=0, keepdims=True)
    e = jnp.exp(hlb - m)
    p = e / jnp.sum(e, axis=0, keepdims=True)
    lb = jnp.zeros_like(m)
    for i in range(1, layer + 1):
        lb = lb + p[i:i + 1, :]
    return lb


def _split3_bf16(x):
    hi = x.astype(BF16)
    r = x - hi.astype(F32)
    mid = r.astype(BF16)
    return hi, mid, (r - mid.astype(F32)).astype(BF16)


def _select_rows_mxu(sel, x):
    return sum(jnp.dot(sel, part, preferred_element_type=F32) for part in _split3_bf16(x))


def _cumsum_rows_small(g):
    row = lax.broadcasted_iota(jnp.int32, g.shape, 0)
    b = jnp.zeros_like(g)
    for u in range(g.shape[0]):
        b = b + jnp.where(row >= u, g[u:u + 1, :], 0.0)
    return b


def _hgrn_span(proj_ref, mix_ref, st_ref, row0, lb, hng, tri, tick):
    L, KB = HGRN_CHUNK, HGRN_KEYBLOCK
    span = HGRN_SPAN * L
    g, k = _hgrn_gates(proj_ref[pl.ds(row0, span), OFF_F:OFF_F + D_HGRN], lb)
    b = _select_rows_mxu(tri, g)
    units = [(c, h) for c in range(HGRN_SPAN) for h in range(H_HGRN)]

    ops = {}
    for c, h in units:
        rows = pl.ds(row0 + c * L, L)
        cs = slice(h * DK_HGRN, (h + 1) * DK_HGRN)
        q = proj_ref[rows, OFF_Q + h * DK_HGRN:OFF_Q + (h + 1) * DK_HGRN]
        v = proj_ref[rows, OFF_I + h * DV_HGRN:OFF_I + (h + 1) * DV_HGRN].astype(BF16)
        kk = k[c * L:(c + 1) * L, cs]
        bb = b[c * L:(c + 1) * L, cs]
        qp, kp = [], []
        for lo in range(0, L, KB):
            r = bb[lo + KB // 2 - 1:lo + KB // 2, :]
            kp.append((kk[lo:lo + KB] * jnp.exp(r - bb[lo:lo + KB])).astype(BF16))
            qp.append((q[lo:] * jnp.exp(bb[lo:] - r)).astype(BF16))
        bl = bb[L - 1:L, :]
        ops[c, h] = dict(qp=qp, kp=kp, v=v, qt=(q * jnp.exp(bb)).astype(BF16),
                         kst=(kk * jnp.exp(bl - bb)).astype(BF16), e=jnp.exp(bl))
    tick()

    for u in units:
        o = ops[u]
        o["p"] = [lax.dot_general(qp, kp, NT_DIMS, preferred_element_type=F32) for qp, kp in zip(o["qp"], o["kp"])]
        o["m"] = lax.dot_general(o["v"], o["kst"], TN_DIMS, preferred_element_type=F32)
    tick()

    for u in units:
        pm = []
        for p in ops[u]["p"]:
            row = lax.broadcasted_iota(jnp.int32, p.shape, 0)
            col = lax.broadcasted_iota(jnp.int32, p.shape, 1)
            pm.append(jnp.where(row >= col, p, 0.0).astype(BF16))
        ops[u]["p"] = pm
    tick()

    for u in units:
        o = ops[u]
        blocks = [None] * (L // KB)
        for j, p in enumerate(o["p"]):
            cj = jnp.dot(p, o["v"][j * KB:(j + 1) * KB], preferred_element_type=F32)
            for i in range(j, L // KB):
                piece = cj[(i - j) * KB:(i - j + 1) * KB]
                blocks[i] = piece if blocks[i] is None else blocks[i] + piece
        o["o"] = jnp.concatenate(blocks, axis=0)
    tick()

    for h in range(H_HGRN):
        st = st_ref[h]
        for c in range(HGRN_SPAN):
            o = ops[c, h]
            out = o["o"] + lax.dot_general(o["qt"], st.astype(BF16), NT_DIMS, preferred_element_type=F32)
            st = o["e"] * st + o["m"]
            rows = pl.ds(row0 + c * L, L)
            gate = proj_ref[rows, OFF_G + h * DV_HGRN:OFF_G + (h + 1) * DV_HGRN]
            mix_ref[rows, D_CONV + h * DV_HGRN:D_CONV + (h + 1) * DV_HGRN] = _hgrn_out(out, hng, gate).astype(BF16)
        st_ref[h] = st


def _sink_softmax(s, sink):
    m = jnp.maximum(jnp.max(s, axis=-1, keepdims=True), sink)
    p = jnp.exp(s - m)
    return p, jnp.sum(p, axis=-1, keepdims=True) + jnp.exp(sink - m)


def _bias_kernel(tab_ref, bp_ref, bs_ref, op_ref, os_ref, *, seq):
    bk = bp_ref[...]
    for h in range(H_ATTN):
        acc = jnp.full(bk.shape, -jnp.inf, F32)
        for bkt in range(NUM_BUCKETS):
            acc = jnp.where(bk == bkt, tab_ref[bkt, h], acc)
        op_ref[h] = acc
    bk = bs_ref[...]
    row = lax.broadcasted_iota(jnp.int32, bk.shape, 0)
    for kv in range(KV_HEADS):
        acc = jnp.full(bk.shape, -jnp.inf, F32)
        for bkt in range(NUM_BUCKETS):
            val = jnp.full(bk.shape, tab_ref[bkt, kv * GROUP], F32)
            for gi in range(1, GROUP):
                val = jnp.where(row >= gi * seq, tab_ref[bkt, kv * GROUP + gi], val)
            acc = jnp.where(bk == bkt, val, acc)
        os_ref[kv] = acc


def _t5_bucket(rel):
    n = jnp.maximum(rel, 0)
    max_exact = NUM_BUCKETS // 2
    nf = jnp.maximum(n, max_exact).astype(F32)
    large = max_exact + (jnp.log(nf / max_exact) / math.log(MAX_DISTANCE / max_exact)
                         * (NUM_BUCKETS - max_exact)).astype(jnp.int32)
    large = jnp.minimum(large, NUM_BUCKETS - 1)
    return jnp.where(n < max_exact, n, large)


def _bias_tables(rel_bias, dec_seq, w_buf):
    qi = jnp.arange(ATTN_BLOCK, dtype=jnp.int32)[:, None]
    kc = jnp.arange(2 * ATTN_BLOCK, dtype=jnp.int32)[None, :]
    rel_p = qi + ATTN_BLOCK - kc
    bucket_p = jnp.where((rel_p >= 0) & (rel_p <= WINDOW), _t5_bucket(rel_p), -1)
    ts = (jnp.arange(GROUP * dec_seq, dtype=jnp.int32) % dec_seq)[:, None]
    js = jnp.arange(2 * ATTN_BLOCK, dtype=jnp.int32)[None, :]
    rel_s = w_buf + ts - js
    ok_s = (rel_s >= 0) & (rel_s <= WINDOW) & (js < w_buf + dec_seq)
    bucket_s = jnp.where(ok_s, _t5_bucket(rel_s), -1)
    return pl.pallas_call(
        functools.partial(_bias_kernel, seq=dec_seq),
        out_shape=(jax.ShapeDtypeStruct((H_ATTN, ATTN_BLOCK, 2 * ATTN_BLOCK), F32),
                   jax.ShapeDtypeStruct((KV_HEADS, GROUP * dec_seq, 2 * ATTN_BLOCK), F32)),
        in_specs=[pl.BlockSpec(memory_space=pltpu.SMEM),
                  pl.BlockSpec(memory_space=pltpu.VMEM),
                  pl.BlockSpec(memory_space=pltpu.VMEM)],
        out_specs=(pl.BlockSpec(memory_space=pltpu.VMEM), pl.BlockSpec(memory_space=pltpu.VMEM)),
        name="rel_bias_tables",
    )(rel_bias.astype(F32), bucket_p, bucket_s)


def _mod_kernel(c_ref, w_ref, b_ref, o_ref):
    s = _silu(c_ref[...]).astype(BF16)
    o_ref[...] = jnp.dot(s, w_ref[...].astype(BF16), preferred_element_type=F32) + b_ref[...]


def _modulation(c_all, w_ada, b_ada):
    depth = w_ada.shape[0]
    n = c_all.shape[0]
    return pl.pallas_call(
        _mod_kernel,
        out_shape=jax.ShapeDtypeStruct((depth, n, N_MOD * D_MODEL), F32),
        grid=(depth, N_MOD * D_MODEL // MOD_COLS),
        in_specs=[pl.BlockSpec((n, D_MODEL), lambda l, j: (0, 0)),
                  pl.BlockSpec((None, D_MODEL, MOD_COLS), lambda l, j: (l, 0, j)),
                  pl.BlockSpec((None, 1, MOD_COLS), lambda l, j: (l, 0, j))],
        out_specs=pl.BlockSpec((None, n, MOD_COLS), lambda l, j: (l, 0, j)),
        compiler_params=pltpu.CompilerParams(dimension_semantics=("arbitrary", "arbitrary"),
                                             vmem_limit_bytes=VMEM_LIMIT),
        name="adaln_modulation",
    )(c_all, w_ada, b_ada.reshape(depth, 1, N_MOD * D_MODEL))


def _mod_rows(m, n_tokens):
    if m.shape[0] == 1:
        return m
    reps = n_tokens // m.shape[0]
    tok = lax.broadcasted_iota(jnp.int32, (n_tokens, m.shape[0]), 0)
    bat = lax.broadcasted_iota(jnp.int32, (n_tokens, m.shape[0]), 1)
    sel = jnp.where((tok >= bat * reps) & (tok < (bat + 1) * reps), 1.0, 0.0).astype(BF16)
    return _select_rows_mxu(sel, m)


def _modulated_norm(x, g, sc, sh):
    n = x.shape[0]
    return (_rms_rows(x) * (g * (1.0 + _mod_rows(sc, n))) + _mod_rows(sh, n)).astype(BF16)


def _sample_mod_specs(mod_s, chunks, index_map):
    return [pl.BlockSpec((mod_s.shape[0], D_MODEL), functools.partial(index_map, chunk=c),
                         pipeline_mode=pl.Buffered(1)) for c in chunks]


def _mlp_tile(mix, x, g1, sh, sc, g2, ng, wout_ref, wup_ref, wdn_ref, fg, final):
    n = x.shape[0]
    x1 = x + _mod_rows(g1, n) * jnp.dot(mix.astype(BF16), wout_ref[...], preferred_element_type=F32)
    h = _modulated_norm(x1, ng, sc, sh)
    acc = None
    for c in range(D_FF // FF_CHUNK):
        u = jnp.dot(h, wup_ref[:, c * FF_CHUNK:(c + 1) * FF_CHUNK], preferred_element_type=F32)
        u = jnp.square(jnp.maximum(u, 0.0)).astype(BF16)
        d = jnp.dot(u, wdn_ref[c * FF_CHUNK:(c + 1) * FF_CHUNK, :], preferred_element_type=F32)
        acc = d if acc is None else acc + d
    x2 = x1 + _mod_rows(g2, n) * acc
    return _rms_rows(x2) * fg if final else x2


def _mlp_kernel(mix_ref, x_ref, g1_ref, sh_ref, sc_ref, g2_ref, smix_ref, sx_ref, sg1_ref, ssh_ref, ssc_ref, sg2_ref,
                ng_ref, wout_ref, wup_ref, wdn_ref, fg_ref, o_ref, so_ref, *, final):
    weights = (ng_ref[...], wout_ref, wup_ref, wdn_ref, fg_ref[...], final)
    o_ref[...] = _mlp_tile(mix_ref[...], x_ref[...], g1_ref[...], sh_ref[...], sc_ref[...], g2_ref[...], *weights)

    @pl.when(pl.program_id(0) == pl.num_programs(0) - 1)
    def _():
        so_ref[...] = _mlp_tile(smix_ref[...], sx_ref[...], sg1_ref[...], ssh_ref[...], ssc_ref[...], sg2_ref[...],
                                *weights)


def _out_mlp(mix2, x2, mod, mix_s, xs, mod_s, norm_g, w_out, w_up, w_down, final_g, layer, tiles_per_batch, final):
    n = x2.shape[0]
    tile = TOK_TILE
    const = lambda i: (0, 0)
    of_layer = lambda i: (layer, 0, 0)
    mod_p = lambda chunk: pl.BlockSpec((None, 1, D_MODEL), lambda i: (i // tiles_per_batch, 0, chunk))
    whole = lambda a: pl.BlockSpec(a.shape, const, pipeline_mode=pl.Buffered(1))
    return pl.pallas_call(
        functools.partial(_mlp_kernel, final=final),
        out_shape=(jax.ShapeDtypeStruct((n, D_MODEL), F32), jax.ShapeDtypeStruct(xs.shape, F32)),
        grid=(n // tile,),
        in_specs=[pl.BlockSpec((tile, D_MODEL), lambda i: (i, 0)),
                  pl.BlockSpec((tile, D_MODEL), lambda i: (i, 0)),
                  mod_p(2), mod_p(3), mod_p(4), mod_p(5),
                  whole(mix_s), whole(xs)] + _sample_mod_specs(mod_s, (2, 3, 4, 5), lambda i, chunk: (0, chunk)) + [
                  pl.BlockSpec((1, D_MODEL), const),
                  pl.BlockSpec((None, D_MODEL, D_MODEL), of_layer, pipeline_mode=pl.Buffered(1)),
                  pl.BlockSpec((None, D_MODEL, D_FF), of_layer, pipeline_mode=pl.Buffered(1)),
                  pl.BlockSpec((None, D_FF, D_MODEL), of_layer, pipeline_mode=pl.Buffered(1)),
                  pl.BlockSpec((1, D_MODEL), const)],
        out_specs=(pl.BlockSpec((tile, D_MODEL), lambda i: (i, 0)), pl.BlockSpec(xs.shape, const)),
        compiler_params=pltpu.CompilerParams(dimension_semantics=("arbitrary",), vmem_limit_bytes=VMEM_LIMIT),
        name="out_projection_mlp",
    )(mix2, x2, mod, mod, mod, mod, mix_s, xs, mod_s, mod_s, mod_s, mod_s, norm_g.reshape(1, D_MODEL), w_out, w_up,
      w_down, final_g.reshape(1, D_MODEL))


def _conv_ln_swish(acc, lng, lnb):
    mu = jnp.mean(acc, axis=-1, keepdims=True)
    xc = acc - mu
    y = xc * lax.rsqrt(jnp.mean(xc * xc, axis=-1, keepdims=True) + EPS) * lng + lnb
    return _silu(y)


def _hgrn_gates(fh, lb):
    f = lb + (1.0 - lb) * jax.nn.sigmoid(fh)
    return jnp.log(f), 1.0 - f


def _hgrn_out(o, hng, gate):
    return _rms_rows(o) * hng * _silu(gate)


def _ticker(pieces):
    it = iter(pieces)

    def tick():
        piece = next(it, None)
        if piece is not None:
            piece()

    def flush():
        for piece in it:
            piece()

    tick.flush = flush
    return tick


def _prompt_mix_kernel(sinks_ref, x_ref, sh_ref, sc_ref, sx_ref, ssh_ref, ssc_ref, ng_ref, win_ref, convw_ref,
                       convb_ref, lng_ref, lnb_ref, hlb_ref, hng_ref, bias_ref,
                       mix_ref, sproj_ref, convo_ref, so_ref, ko_ref, vo_ref,
                       proj_ref, wbf_ref, abuf, kbuf, vbuf, st_ref, *, layer, tile):
    t = pl.program_id(1)
    last = pl.num_programs(1) - 1

    @pl.when((pl.program_id(0) == 0) & (t == 0))
    def _():
        wbf_ref[...] = win_ref[...].astype(BF16)

    @pl.when(t == 0)
    def _():
        abuf[0:CONV_PAD, :] = jnp.zeros((CONV_PAD, D_CONV), F32)
        abuf[CONV_PAD + tile:CONV_PAD + tile + SUBLANES, :] = jnp.zeros((SUBLANES, D_CONV), F32)
        kbuf[0:ATTN_BLOCK, :] = jnp.zeros((ATTN_BLOCK, D_KV), BF16)
        vbuf[0:ATTN_BLOCK, :] = jnp.zeros((ATTN_BLOCK, D_KV), BF16)
        st_ref[...] = jnp.zeros(st_ref.shape, F32)

    h_in = _modulated_norm(x_ref[...], ng_ref[...], sc_ref[...], sh_ref[...])
    for lo, hi in ((OFF_AVAL, OFF_Q), (OFF_F, OFF_I), (OFF_Q, OFF_F), (OFF_I, OFF_G), (OFF_G, OFF_QA),
                   (OFF_QA, IN_WIDTH)):
        proj_ref[:, lo:hi] = jnp.dot(h_in, wbf_ref[:, lo:hi], preferred_element_type=F32)

    kbuf[ATTN_BLOCK:ATTN_BLOCK + tile, :] = proj_ref[:, OFF_KA:OFF_KA + D_KV].astype(BF16)
    vbuf[ATTN_BLOCK:ATTN_BLOCK + tile, :] = proj_ref[:, OFF_VA:OFF_VA + D_KV].astype(BF16)
    scale = HEAD_DIM ** -0.5
    attn = {}

    def attn_scores(blk):
        def run():
            r0 = blk * ATTN_BLOCK
            for h in range(H_ATTN):
                kv = h // GROUP
                q = (proj_ref[r0:r0 + ATTN_BLOCK, OFF_QA + h * HEAD_DIM:OFF_QA + (h + 1) * HEAD_DIM]
                     * scale).astype(BF16)
                kall = kbuf[r0:r0 + 2 * ATTN_BLOCK, kv * HEAD_DIM:(kv + 1) * HEAD_DIM]
                attn[blk, h] = lax.dot_general(q, kall, NT_DIMS, preferred_element_type=F32)
        return run

    def attn_softmax(blk):
        def run():
            for h in range(H_ATTN):
                s = attn[blk, h] + bias_ref[h]
                if blk == 0:
                    col = lax.broadcasted_iota(jnp.int32, s.shape, 1)
                    s = jnp.where(col + (t * tile - ATTN_BLOCK) >= 0, s, -jnp.inf)
                p, den = _sink_softmax(s, sinks_ref[h])
                attn[blk, h] = (p.astype(BF16), den)
        return run

    def attn_values(blk):
        def run():
            r0 = blk * ATTN_BLOCK
            heads = []
            for h in range(H_ATTN):
                kv = h // GROUP
                p, den = attn[blk, h]
                vall = vbuf[r0:r0 + 2 * ATTN_BLOCK, kv * HEAD_DIM:(kv + 1) * HEAD_DIM]
                heads.append(jnp.dot(p, vall, preferred_element_type=F32) / den)
            mix_ref[r0:r0 + ATTN_BLOCK, D_CONV + D_HGRN:D_MODEL] = jnp.concatenate(heads, axis=1).astype(BF16)
        return run

    tick = _ticker([stage(blk) for blk in range(tile // ATTN_BLOCK)
                    for stage in (attn_scores, attn_softmax, attn_values)])

    abuf[CONV_PAD:CONV_PAD + tile, :] = (proj_ref[:, OFF_AVAL:OFF_AVAL + D_CONV]
                                         * jax.nn.sigmoid(proj_ref[:, OFF_AGATE:OFF_AGATE + D_CONV]))
    first_row = CONV_PAD - (CONV_WIDTH - 1)
    acc = jnp.broadcast_to(convb_ref[...], (tile, D_CONV))
    for r in range(SUBLANES):
        z = None
        for off in range(r, first_row + CONV_WIDTH, SUBLANES):
            j = off - first_row
            if j < 0:
                continue
            term = convw_ref[j:j + 1, :] * abuf[off - r:off - r + tile + SUBLANES, :]
            z = term if z is None else z + term
        acc = acc + (z[0:tile] if r == 0 else pltpu.roll(z, tile + SUBLANES - r, 0)[0:tile])
    mix_ref[:, 0:D_CONV] = _conv_ln_swish(acc, lng_ref[...], lnb_ref[...]).astype(BF16)
    tick()

    lb = _layer_lb(hlb_ref[...], layer)
    hng = hng_ref[...]
    span = HGRN_SPAN * HGRN_CHUNK
    ri = lax.broadcasted_iota(jnp.int32, (span, span), 0)
    ci = lax.broadcasted_iota(jnp.int32, (span, span), 1)
    tri = jnp.where((ri >= ci) & (ri // HGRN_CHUNK == ci // HGRN_CHUNK), 1.0, 0.0).astype(BF16)
    for i in range(tile // span):
        _hgrn_span(proj_ref, mix_ref, st_ref, i * span, lb, hng, tri, tick)
    tick.flush()

    @pl.when(t == last)
    def _():
        convo_ref[...] = abuf[CONV_PAD + tile - (CONV_WIDTH - 1):CONV_PAD + tile, :]
        for h in range(H_HGRN):
            so_ref[h] = st_ref[h].T
        ko_ref[...] = proj_ref[tile - WINDOW:tile, OFF_KA:OFF_KA + D_KV]
        vo_ref[...] = proj_ref[tile - WINDOW:tile, OFF_VA:OFF_VA + D_KV]

    abuf[0:CONV_PAD, :] = abuf[tile:tile + CONV_PAD, :]
    kbuf[0:ATTN_BLOCK, :] = kbuf[tile:tile + ATTN_BLOCK, :]
    vbuf[0:ATTN_BLOCK, :] = vbuf[tile:tile + ATTN_BLOCK, :]

    @pl.when((pl.program_id(0) == pl.num_programs(0) - 1) & (t == last))
    def _():
        h_s = _modulated_norm(sx_ref[...], ng_ref[...], ssc_ref[...], ssh_ref[...])
        sproj_ref[...] = jnp.dot(h_s, wbf_ref[...], preferred_element_type=F32)


def _carry_specs(carried):
    return [pl.BlockSpec(memory_space=pl.ANY)] * len(carried)


def _carried(kernel_fn, n_in, n_carried):
    if n_carried == 0:
        return kernel_fn
    return lambda *refs: kernel_fn(*refs[:n_in], *refs[n_in + n_carried:])


def _prompt_mixers(x, mod, xs, mod_s, norm_g, w_in, sinks, conv_w, conv_b, ln_g, ln_b, hgrn_lb, hng, bias_p, layer,
                   carried):
    B, T = x.shape[:2]
    tile = MIX_TILE
    depth = hgrn_lb.shape[0]
    const2 = lambda b, t: (0, 0)
    inputs = (sinks, x, mod, mod, xs, mod_s, mod_s, norm_g.reshape(1, D_MODEL), w_in, conv_w,
              conv_b.reshape(1, D_CONV), ln_g.reshape(1, D_CONV), ln_b.reshape(1, D_CONV), hgrn_lb,
              hng.reshape(1, DV_HGRN), bias_p)
    return pl.pallas_call(
        _carried(functools.partial(_prompt_mix_kernel, layer=layer, tile=tile), len(inputs), len(carried)),
        out_shape=(jax.ShapeDtypeStruct((B, T, D_MODEL), BF16),
                   jax.ShapeDtypeStruct((xs.shape[0], IN_WIDTH), F32),
                   jax.ShapeDtypeStruct((depth, B, CONV_WIDTH - 1, D_CONV), F32),
                   jax.ShapeDtypeStruct((depth, B, H_HGRN, DK_HGRN, DV_HGRN), F32),
                   jax.ShapeDtypeStruct((depth, B, WINDOW, D_KV), F32),
                   jax.ShapeDtypeStruct((depth, B, WINDOW, D_KV), F32)),
        grid=(B, T // tile),
        in_specs=[pl.BlockSpec(memory_space=pltpu.SMEM),
                  pl.BlockSpec((None, tile, D_MODEL), lambda b, t: (b, t, 0)),
                  pl.BlockSpec((None, 1, D_MODEL), lambda b, t: (b, 0, 0)),
                  pl.BlockSpec((None, 1, D_MODEL), lambda b, t: (b, 0, 1)),
                  pl.BlockSpec(xs.shape, const2, pipeline_mode=pl.Buffered(1))]
                 + _sample_mod_specs(mod_s, (0, 1), lambda b, t, chunk: (0, chunk)) + [
                  pl.BlockSpec((1, D_MODEL), const2),
                  pl.BlockSpec((None, D_MODEL, IN_WIDTH), lambda b, t: (layer, 0, 0), pipeline_mode=pl.Buffered(1)),
                  pl.BlockSpec((CONV_WIDTH, D_CONV), const2),
                  pl.BlockSpec((1, D_CONV), const2),
                  pl.BlockSpec((1, D_CONV), const2),
                  pl.BlockSpec((1, D_CONV), const2),
                  pl.BlockSpec((depth, D_HGRN), const2),
                  pl.BlockSpec((1, DV_HGRN), const2),
                  pl.BlockSpec((H_ATTN, ATTN_BLOCK, 2 * ATTN_BLOCK), lambda b, t: (0, 0, 0))] + _carry_specs(carried),
        out_specs=(pl.BlockSpec((None, tile, D_MODEL), lambda b, t: (b, t, 0)),
                   pl.BlockSpec((xs.shape[0], IN_WIDTH), const2),
                   pl.BlockSpec((None, None, CONV_WIDTH - 1, D_CONV), lambda b, t: (layer, b, 0, 0)),
                   pl.BlockSpec((None, None, H_HGRN, DK_HGRN, DV_HGRN), lambda b, t: (layer, b, 0, 0, 0)),
                   pl.BlockSpec((None, None, WINDOW, D_KV), lambda b, t: (layer, b, 0, 0)),
                   pl.BlockSpec((None, None, WINDOW, D_KV), lambda b, t: (layer, b, 0, 0))),
        input_output_aliases={len(inputs) + i: 2 + i for i in range(len(carried))},
        scratch_shapes=[pltpu.VMEM((tile, IN_WIDTH), F32),
                        pltpu.VMEM((D_MODEL, IN_WIDTH), BF16),
                        pltpu.VMEM((CONV_PAD + tile + SUBLANES, D_CONV), F32),
                        pltpu.VMEM((ATTN_BLOCK + tile, D_KV), BF16),
                        pltpu.VMEM((ATTN_BLOCK + tile, D_KV), BF16),
                        pltpu.VMEM((H_HGRN, DV_HGRN, DK_HGRN), F32)],
        compiler_params=pltpu.CompilerParams(dimension_semantics=("arbitrary", "arbitrary"),
                                             vmem_limit_bytes=VMEM_LIMIT),
        name="prompt_mixers",
    )(*inputs, *carried)


def _sample_mix_kernel(sinks_ref, proj_ref, cconv_ref, state_ref, ck_ref, cv_ref, convw_ref, convb_ref, lng_ref,
                       lnb_ref, hlb_ref, hng_ref, bias_ref,
                       mix_ref, convo_ref, so_ref, ko_ref, vo_ref, kpad_ref, vpad_ref, *,
                       layer, block, seq, w_buf):
    hist = CONV_WIDTH - 1

    @pl.when(pl.program_id(0) == 0)
    def _():
        for ref in (kpad_ref, vpad_ref):
            ref[:, 0:w_buf - seq, :] = jnp.zeros((block, w_buf - seq, D_KV), F32)

    lb = _layer_lb(hlb_ref[...], layer)
    hng = hng_ref[...]
    scale = HEAD_DIM ** -0.5
    elems = range(block)
    row8 = lax.broadcasted_iota(jnp.int32, (SUBLANES, DV_HGRN), 0)
    ones_rows = jnp.where((row8 >= seq) & (row8 < seq + 3), 1.0, 0.0)
    zrow = jnp.zeros((1, DK_HGRN), BF16)
    prow = lax.broadcasted_iota(jnp.int32, (seq, seq), 0)
    pcol = lax.broadcasted_iota(jnp.int32, (seq, seq), 1)
    grow = lax.broadcasted_iota(jnp.int32, (GROUP * seq, 1), 0)

    proj = [proj_ref[e * seq:(e + 1) * seq, :] for e in elems]

    glu = [p[:, OFF_AVAL:OFF_AVAL + D_CONV] * jax.nn.sigmoid(p[:, OFF_AGATE:OFF_AGATE + D_CONV]) for p in proj]
    full = [cconv_ref[i] for i in range(hist)]
    full += [jnp.concatenate([glu[e][t:t + 1] for e in elems], axis=0) for t in range(seq)]
    for i in range(hist):
        convo_ref[i] = full[i + seq]
    conv_out = []
    for t in range(seq):
        acc = jnp.broadcast_to(convb_ref[...], (block, D_CONV))
        for j in range(CONV_WIDTH):
            acc = acc + convw_ref[j:j + 1, :] * full[t + j]
        conv_out.append(_conv_ln_swish(acc, lng_ref[...], lnb_ref[...]))
    out_a = [jnp.concatenate([conv_out[t][e:e + 1] for t in range(seq)], axis=0) for e in elems]

    units = [(e, h) for e in elems for h in range(H_HGRN)]
    ops = {}
    for e in elems:
        p = proj[e]
        g, k = _hgrn_gates(p[:, OFF_F:OFF_F + D_HGRN], lb)
        b = _cumsum_rows_small(g)
        for h in range(H_HGRN):
            cs = slice(h * DK_HGRN, (h + 1) * DK_HGRN)
            q = p[:, OFF_Q + h * DK_HGRN:OFF_Q + (h + 1) * DK_HGRN]
            v = p[:, OFF_I + h * DV_HGRN:OFF_I + (h + 1) * DV_HGRN]
            bb = b[:, cs]
            bl = bb[seq - 1:seq, :]
            kst = (k[:, cs] * jnp.exp(bl - bb)).astype(BF16)
            x = jnp.concatenate([kst.astype(F32), *(part.astype(F32) for part in _split3_bf16(jnp.exp(bl))),
                                 zrow.astype(F32)], axis=0).astype(BF16)
            vpad = jnp.concatenate([v, jnp.zeros((SUBLANES - seq, DV_HGRN), F32)], axis=0)
            ops[e, h] = dict(qp=(q * jnp.exp(bb - bl)).astype(BF16), kst=kst, v=v.astype(BF16),
                             qt=(q * jnp.exp(bb)).astype(BF16), x=x,
                             r=jnp.concatenate([vpad, ones_rows], axis=1).astype(BF16))
    for u in units:
        o = ops[u]
        st = state_ref[u[0], u[1]]
        o["p"] = lax.dot_general(o["qp"], o["kst"], NT_DIMS, preferred_element_type=F32)
        o["inter"] = jnp.dot(o["qt"], st.astype(BF16), preferred_element_type=F32)
        me = lax.dot_general(o["x"], o["r"], TN_DIMS, preferred_element_type=F32)
        so_ref[u[0], u[1]] = me[:, DV_HGRN:] * st + me[:, :DV_HGRN]
    out_b = {}
    for u in units:
        o = ops[u]
        pm = jnp.where(prow >= pcol, o["p"], 0.0).astype(BF16)
        out = o["inter"] + jnp.dot(pm, o["v"], preferred_element_type=F32)
        gate = proj[u[0]][:, OFF_G + u[1] * DV_HGRN:OFF_G + (u[1] + 1) * DV_HGRN]
        out_b[u] = _hgrn_out(out, hng, gate)

    scores = {}
    for e in elems:
        p = proj[e]
        for kv in range(KV_HEADS):
            hs = slice(kv * HEAD_DIM, (kv + 1) * HEAD_DIM)
            q2 = jnp.concatenate([p[:, OFF_QA + h * HEAD_DIM:OFF_QA + (h + 1) * HEAD_DIM]
                                  for h in range(kv * GROUP, (kv + 1) * GROUP)], axis=0)
            q2 = (q2 * scale).astype(BF16)
            bias = bias_ref[kv]
            s_c = jnp.dot(q2, ck_ref[e, hs, :].astype(BF16), preferred_element_type=F32) + bias[:, 0:w_buf]
            s_n = (lax.dot_general(q2, p[:, OFF_KA + kv * HEAD_DIM:OFF_KA + (kv + 1) * HEAD_DIM].astype(BF16), NT_DIMS,
                                   preferred_element_type=F32) + bias[:, w_buf:w_buf + seq])
            scores[e, kv] = (s_c, s_n)
    out_c = {}
    for e in elems:
        p = proj[e]
        for kv in range(KV_HEADS):
            hs = slice(kv * HEAD_DIM, (kv + 1) * HEAD_DIM)
            sink = jnp.zeros((GROUP * seq, 1), F32)
            for gi in range(GROUP):
                sink = jnp.where(grow >= gi * seq, sinks_ref[kv * GROUP + gi], sink)
            s_c, s_n = scores[e, kv]
            m = jnp.maximum(jnp.maximum(jnp.max(s_c, axis=-1, keepdims=True), jnp.max(s_n, axis=-1, keepdims=True)),
                            sink)
            p_c = jnp.exp(s_c - m)
            p_n = jnp.exp(s_n - m)
            den = jnp.sum(p_c, axis=-1, keepdims=True) + jnp.sum(p_n, axis=-1, keepdims=True) + jnp.exp(sink - m)
            vnew = p[:, OFF_VA + kv * HEAD_DIM:OFF_VA + (kv + 1) * HEAD_DIM].astype(BF16)
            o2 = (lax.dot_general(p_c.astype(BF16), cv_ref[e, hs, :].astype(BF16), NT_DIMS,
                                  preferred_element_type=F32)
                  + jnp.dot(p_n.astype(BF16), vnew, preferred_element_type=F32)) / den
            for gi in range(GROUP):
                out_c[e, kv * GROUP + gi] = o2[gi * seq:(gi + 1) * seq]
    lane = lax.broadcasted_iota(jnp.int32, (D_KV, w_buf), 1)
    for e in elems:
        p = proj[e]
        for pad, cache, new, out in ((kpad_ref, ck_ref, p[:, OFF_KA:OFF_KA + D_KV], ko_ref),
                                     (vpad_ref, cv_ref, p[:, OFF_VA:OFF_VA + D_KV], vo_ref)):
            pad[e, w_buf - seq:w_buf, :] = new
            out[e] = jnp.where(lane >= w_buf - seq, pad[e].T, pltpu.roll(cache[e], w_buf - seq, 1))

    for e in elems:
        parts = [out_a[e]] + [out_b[e, h] for h in range(H_HGRN)] + [out_c[e, h] for h in range(H_ATTN)]
        mix_ref[e * seq:(e + 1) * seq, :] = jnp.concatenate(parts, axis=1)


def _sample_mixers(proj2, sinks, cache_conv, state, cache_k, cache_v, conv_w, conv_b, ln_g, ln_b, hgrn_lb, hng,
                   bias_s, layer, carried):
    B = state.shape[1]
    seq = proj2.shape[0] // B
    w_buf = cache_k.shape[3]
    block = SAMPLE_BLOCK
    depth = hgrn_lb.shape[0]
    hist = CONV_WIDTH - 1
    const2 = lambda i: (0, 0)
    cache_specs = [pl.BlockSpec((None, hist, block, D_CONV), lambda i: (layer, 0, i, 0)),
                   pl.BlockSpec((None, block, H_HGRN, DK_HGRN, DV_HGRN), lambda i: (layer, i, 0, 0, 0)),
                   pl.BlockSpec((None, block, D_KV, w_buf), lambda i: (layer, i, 0, 0)),
                   pl.BlockSpec((None, block, D_KV, w_buf), lambda i: (layer, i, 0, 0))]
    inputs = (sinks, proj2, cache_conv, state, cache_k, cache_v, conv_w, conv_b.reshape(1, D_CONV),
              ln_g.reshape(1, D_CONV), ln_b.reshape(1, D_CONV), hgrn_lb, hng.reshape(1, DV_HGRN), bias_s)
    return pl.pallas_call(
        _carried(functools.partial(_sample_mix_kernel, layer=layer, block=block, seq=seq, w_buf=w_buf),
                 len(inputs), len(carried)),
        out_shape=(jax.ShapeDtypeStruct((B * seq, D_MODEL), F32),
                   jax.ShapeDtypeStruct((depth, hist, B, D_CONV), F32),
                   jax.ShapeDtypeStruct((depth, B, H_HGRN, DK_HGRN, DV_HGRN), F32),
                   jax.ShapeDtypeStruct((depth, B, D_KV, w_buf), F32),
                   jax.ShapeDtypeStruct((depth, B, D_KV, w_buf), F32)),
        grid=(B // block,),
        in_specs=[pl.BlockSpec(memory_space=pltpu.SMEM),
                  pl.BlockSpec((block * seq, IN_WIDTH), lambda i: (i, 0))] + cache_specs + [
                  pl.BlockSpec((CONV_WIDTH, D_CONV), const2),
                  pl.BlockSpec((1, D_CONV), const2),
                  pl.BlockSpec((1, D_CONV), const2),
                  pl.BlockSpec((1, D_CONV), const2),
                  pl.BlockSpec((depth, D_HGRN), const2),
                  pl.BlockSpec((1, DV_HGRN), const2),
                  pl.BlockSpec((KV_HEADS, GROUP * seq, 2 * ATTN_BLOCK), lambda i: (0, 0, 0))] + _carry_specs(carried),
        out_specs=tuple([pl.BlockSpec((block * seq, D_MODEL), lambda i: (i, 0))] + cache_specs),
        input_output_aliases={len(inputs) + i: 1 + i for i in range(len(carried))},
        scratch_shapes=[pltpu.VMEM((block, w_buf, D_KV), F32),
                        pltpu.VMEM((block, w_buf, D_KV), F32)],
        compiler_params=pltpu.CompilerParams(dimension_semantics=("arbitrary",), vmem_limit_bytes=VMEM_LIMIT),
        name="sample_mixers",
    )(*inputs, *carried)


def kernel(x_prompt, x_sample, cache_conv, state_hgrn, cache_swa_k, cache_swa_v, c_prompt, c_sample, rel_bias, w_ada, b_ada, norm_mix_g, w_in, conv_w, conv_b, conv_ln_g, conv_ln_b, hgrn_lb, hgrn_norm_g, attn_sinks, w_out, norm_mlp_g, w_up, w_down, final_g):
    Bp, Tp = x_prompt.shape[:2]
    Bs, Ts = x_sample.shape[:2]
    depth = w_in.shape[0]
    w_buf = cache_swa_k.shape[2]
    assert Tp % MIX_TILE == 0 and (Bp * Tp) % TOK_TILE == 0 and Tp % TOK_TILE == 0 and Bs % SAMPLE_BLOCK == 0
    assert w_buf == WINDOW and GROUP * Ts == SUBLANES

    bias_p, bias_s = _bias_tables(rel_bias, Ts, w_buf)
    mod = _modulation(jnp.concatenate([c_prompt, c_sample], axis=0), w_ada, b_ada)
    w_out_b, w_up_b, w_down_b = (w.astype(BF16) for w in (w_out, w_up, w_down))
    hlb = hgrn_lb.astype(F32)
    cc = jnp.swapaxes(cache_conv, 1, 2)
    ck = jnp.swapaxes(cache_swa_k.reshape(depth, Bs, w_buf, D_KV), 2, 3)
    cv = jnp.swapaxes(cache_swa_v.reshape(depth, Bs, w_buf, D_KV), 2, 3)

    xp = x_prompt.reshape(Bp * Tp, D_MODEL)
    xs = x_sample.reshape(Bs * Ts, D_MODEL)
    caches_p = ()
    caches_s = ()
    for l in range(depth):
        final = l == depth - 1
        mod_p = mod[l, :Bp].reshape(Bp, 1, N_MOD * D_MODEL)
        mod_s = mod[l, Bp:]
        mix_p, proj_s, *caches_p = _prompt_mixers(xp.reshape(Bp, Tp, D_MODEL), mod_p, xs, mod_s, norm_mix_g[l], w_in,
                                                  attn_sinks[l], conv_w[l], conv_b[l], conv_ln_g[l], conv_ln_b[l], hlb,
                                                  hgrn_norm_g[l], bias_p, l, caches_p)
        mix_s, *caches_s = _sample_mixers(proj_s, attn_sinks[l], cc, state_hgrn, ck, cv, conv_w[l],
                                          conv_b[l], conv_ln_g[l], conv_ln_b[l], hlb, hgrn_norm_g[l], bias_s, l,
                                          caches_s)
        xp, xs = _out_mlp(mix_p.reshape(Bp * Tp, D_MODEL), xp, mod_p, mix_s, xs, mod_s, norm_mlp_g[l], w_out_b,
                          w_up_b, w_down_b, final_g, l, Tp // TOK_TILE, final)
    cp, sp, kp, vp = caches_p
    cs, ss, ksn, vsn = caches_s
    cs, ksn, vsn = jnp.swapaxes(cs, 1, 2), jnp.swapaxes(ksn, 2, 3), jnp.swapaxes(vsn, 2, 3)
    return (xp.reshape(Bp, Tp, D_MODEL), xs.reshape(Bs, Ts, D_MODEL), cp, cs, sp, ss,
            kp.reshape(depth, Bp, WINDOW, KV_HEADS, HEAD_DIM), ksn.reshape(depth, Bs, w_buf, KV_HEADS, HEAD_DIM),
            vp.reshape(depth, Bp, WINDOW, KV_HEADS, HEAD_DIM), vsn.reshape(depth, Bs, w_buf, KV_HEADS, HEAD_DIM))
```

```python
import functools
import math

import jax
import jax.numpy as jnp
from jax import lax
from jax.experimental import pallas as pl
from jax.experimental.pallas import tpu as pltpu

F32 = jnp.float32
BF16 = jnp.bfloat16

D_MODEL = 1024
D_CONV = 256
CONV_WIDTH = 31
H_HGRN = 4
DK_HGRN = 128
DV_HGRN = 128
D_HGRN = 512
HEAD_DIM = 64
H_ATTN = 4
KV_HEADS = 2
GROUP = H_ATTN // KV_HEADS
D_ATTN = H_ATTN * HEAD_DIM
D_KV = KV_HEADS * HEAD_DIM
WINDOW = 128
ATTN_BLOCK = 128
NUM_BUCKETS = 32
MAX_DISTANCE = 128
D_FF = 4 * D_MODEL
N_MOD = 6
EPS = 1e-6

OFF_AVAL = 0
OFF_AGATE = OFF_AVAL + D_CONV
OFF_Q = OFF_AGATE + D_CONV
OFF_F = OFF_Q + H_HGRN * DK_HGRN
OFF_I = OFF_F + H_HGRN * DK_HGRN
OFF_G = OFF_I + D_HGRN
OFF_QA = OFF_G + D_HGRN
OFF_KA = OFF_QA + D_ATTN
OFF_VA = OFF_KA + D_KV
IN_WIDTH = OFF_VA + D_KV

HGRN_CHUNK = 64
HGRN_KEYBLOCK = 32
HGRN_SPAN = 4
SUBLANES = 8
CONV_PAD = 32
MIX_TILE = 512
TOK_TILE = 512
SAMPLE_BLOCK = 16
FF_CHUNK = 1024
MOD_COLS = 2048
VMEM_LIMIT = 56 * 1024 * 1024

NT_DIMS = (((1,), (1,)), ((), ()))
TN_DIMS = (((0,), (0,)), ((), ()))


def _silu(x):
    return x * jax.nn.sigmoid(x)


def _rms_rows(x):
    return x * lax.rsqrt(jnp.mean(x * x, axis=-1, keepdims=True) + EPS)


def _layer_lb(hlb, layer):
    m = jnp.max(hlb, axis=0, keepdims=True)
    e = jnp.exp(hlb - m)
    p = e / jnp.sum(e, axis=0, keepdims=True)
    lb = jnp.zeros_like(m)
    for i in range(1, layer + 1):
        lb = lb + p[i:i + 1, :]
    return lb


def _split3_bf16(x):
    hi = x.astype(BF16)
    r = x - hi.astype(F32)
    mid = r.astype(BF16)
    return hi, mid, (r - mid.astype(F32)).astype(BF16)


def _select_rows_mxu(sel, x):
    return sum(jnp.dot(sel, part, preferred_element_type=F32) for part in _split3_bf16(x))


def _cumsum_rows_small(g):
    row = lax.broadcasted_iota(jnp.int32, g.shape, 0)
    b = jnp.zeros_like(g)
    for u in range(g.shape[0]):
        b = b + jnp.where(row >= u, g[u:u + 1, :], 0.0)
    return b


def _hgrn_span(proj_ref, mix_ref, st_ref, row0, lb, hng, tri, tick):
    L, KB = HGRN_CHUNK, HGRN_KEYBLOCK
    span = HGRN_SPAN * L
    g, k = _hgrn_gates(proj_ref[pl.ds(row0, span), OFF_F:OFF_F + D_HGRN], lb)
    b = _select_rows_mxu(tri, g)
    units = [(c, h) for c in range(HGRN_SPAN) for h in range(H_HGRN)]

    ops = {}
    for c, h in units:
        rows = pl.ds(row0 + c * L, L)
        cs = slice(h * DK_HGRN, (h + 1) * DK_HGRN)
        q = proj_ref[rows, OFF_Q + h * DK_HGRN:OFF_Q + (h + 1) * DK_HGRN]
        v = proj_ref[rows, OFF_I + h * DV_HGRN:OFF_I + (h + 1) * DV_HGRN].astype(BF16)
        kk = k[c * L:(c + 1) * L, cs]
        bb = b[c * L:(c + 1) * L, cs]
        qp, kp = [], []
        for lo in range(0, L, KB):
            r = bb[lo + KB // 2 - 1:lo + KB // 2, :]
            kp.append((kk[lo:lo + KB] * jnp.exp(r - bb[lo:lo + KB])).astype(BF16))
            qp.append((q[lo:] * jnp.exp(bb[lo:] - r)).astype(BF16))
        bl = bb[L - 1:L, :]
        ops[c, h] = dict(qp=qp, kp=kp, v=v, qt=(q * jnp.exp(bb)).astype(BF16),
                         kst=(kk * jnp.exp(bl - bb)).astype(BF16), e=jnp.exp(bl))
    tick()

    for u in units:
        o = ops[u]
        o["p"] = [lax.dot_general(qp, kp, NT_DIMS, preferred_element_type=F32) for qp, kp in zip(o["qp"], o["kp"])]
        o["m"] = lax.dot_general(o["v"], o["kst"], TN_DIMS, preferred_element_type=F32)
    tick()

    for u in units:
        pm = []
        for p in ops[u]["p"]:
            row = lax.broadcasted_iota(jnp.int32, p.shape, 0)
            col = lax.broadcasted_iota(jnp.int32, p.shape, 1)
            pm.append(jnp.where(row >= col, p, 0.0).astype(BF16))
        ops[u]["p"] = pm
    tick()

    for u in units:
        o = ops[u]
        blocks = [None] * (L // KB)
        for j, p in enumerate(o["p"]):
            cj = jnp.dot(p, o["v"][j * KB:(j + 1) * KB], preferred_element_type=F32)
            for i in range(j, L // KB):
                piece = cj[(i - j) * KB:(i - j + 1) * KB]
                blocks[i] = piece if blocks[i] is None else blocks[i] + piece
        o["o"] = jnp.concatenate(blocks, axis=0)
    tick()

    for h in range(H_HGRN):
        st = st_ref[h]
        for c in range(HGRN_SPAN):
            o = ops[c, h]
            out = o["o"] + lax.dot_general(o["qt"], st.astype(BF16), NT_DIMS, preferred_element_type=F32)
            st = o["e"] * st + o["m"]
            rows = pl.ds(row0 + c * L, L)
            gate = proj_ref[rows, OFF_G + h * DV_HGRN:OFF_G + (h + 1) * DV_HGRN]
            mix_ref[rows, D_CONV + h * DV_HGRN:D_CONV + (h + 1) * DV_HGRN] = _hgrn_out(out, hng, gate).astype(BF16)
        st_ref[h] = st


def _sink_softmax(s, sink):
    m = jnp.maximum(jnp.max(s, axis=-1, keepdims=True), sink)
    p = jnp.exp(s - m)
    return p, jnp.sum(p, axis=-1, keepdims=True) + jnp.exp(sink - m)


def _bias_kernel(tab_ref, bp_ref, bs_ref, op_ref, os_ref, *, seq):
    bk = bp_ref[...]
    for h in range(H_ATTN):
        acc = jnp.full(bk.shape, -jnp.inf, F32)
        for bkt in range(NUM_BUCKETS):
            acc = jnp.where(bk == bkt, tab_ref[bkt, h], acc)
        op_ref[h] = acc
    bk = bs_ref[...]
    row = lax.broadcasted_iota(jnp.int32, bk.shape, 0)
    for kv in range(KV_HEADS):
        acc = jnp.full(bk.shape, -jnp.inf, F32)
        for bkt in range(NUM_BUCKETS):
            val = jnp.full(bk.shape, tab_ref[bkt, kv * GROUP], F32)
            for gi in range(1, GROUP):
                val = jnp.where(row >= gi * seq, tab_ref[bkt, kv * GROUP + gi], val)
            acc = jnp.where(bk == bkt, val, acc)
        os_ref[kv] = acc


def _t5_bucket(rel):
    n = jnp.maximum(rel, 0)
    max_exact = NUM_BUCKETS // 2
    nf = jnp.maximum(n, max_exact).astype(F32)
    large = max_exact + (jnp.log(nf / max_exact) / math.log(MAX_DISTANCE / max_exact)
                         * (NUM_BUCKETS - max_exact)).astype(jnp.int32)
    large = jnp.minimum(large, NUM_BUCKETS - 1)
    return jnp.where(n < max_exact, n, large)


def _bias_tables(rel_bias, dec_seq, w_buf):
    qi = jnp.arange(ATTN_BLOCK, dtype=jnp.int32)[:, None]
    kc = jnp.arange(2 * ATTN_BLOCK, dtype=jnp.int32)[None, :]
    rel_p = qi + ATTN_BLOCK - kc
    bucket_p = jnp.where((rel_p >= 0) & (rel_p <= WINDOW), _t5_bucket(rel_p), -1)
    ts = (jnp.arange(GROUP * dec_seq, dtype=jnp.int32) % dec_seq)[:, None]
    js = jnp.arange(2 * ATTN_BLOCK, dtype=jnp.int32)[None, :]
    rel_s = w_buf + ts - js
    ok_s = (rel_s >= 0) & (rel_s <= WINDOW) & (js < w_buf + dec_seq)
    bucket_s = jnp.where(ok_s, _t5_bucket(rel_s), -1)
    return pl.pallas_call(
        functools.partial(_bias_kernel, seq=dec_seq),
        out_shape=(jax.ShapeDtypeStruct((H_ATTN, ATTN_BLOCK, 2 * ATTN_BLOCK), F32),
                   jax.ShapeDtypeStruct((KV_HEADS, GROUP * dec_seq, 2 * ATTN_BLOCK), F32)),
        in_specs=[pl.BlockSpec(memory_space=pltpu.SMEM),
                  pl.BlockSpec(memory_space=pltpu.VMEM),
                  pl.BlockSpec(memory_space=pltpu.VMEM)],
        out_specs=(pl.BlockSpec(memory_space=pltpu.VMEM), pl.BlockSpec(memory_space=pltpu.VMEM)),
        name="rel_bias_tables",
    )(rel_bias.astype(F32), bucket_p, bucket_s)


def _mod_kernel(c_ref, w_ref, b_ref, o_ref):
    s = _silu(c_ref[...]).astype(BF16)
    o_ref[...] = jnp.dot(s, w_ref[...].astype(BF16), preferred_element_type=F32) + b_ref[...]


def _modulation(c_all, w_ada, b_ada):
    depth = w_ada.shape[0]
    n = c_all.shape[0]
    return pl.pallas_call(
        _mod_kernel,
        out_shape=jax.ShapeDtypeStruct((depth, n, N_MOD * D_MODEL), F32),
        grid=(depth, N_MOD * D_MODEL // MOD_COLS),
        in_specs=[pl.BlockSpec((n, D_MODEL), lambda l, j: (0, 0)),
                  pl.BlockSpec((None, D_MODEL, MOD_COLS), lambda l, j: (l, 0, j)),
                  pl.BlockSpec((None, 1, MOD_COLS), lambda l, j: (l, 0, j))],
        out_specs=pl.BlockSpec((None, n, MOD_COLS), lambda l, j: (l, 0, j)),
        compiler_params=pltpu.CompilerParams(dimension_semantics=("arbitrary", "arbitrary"),
                                             vmem_limit_bytes=VMEM_LIMIT),
        name="adaln_modulation",
    )(c_all, w_ada, b_ada.reshape(depth, 1, N_MOD * D_MODEL))


def _mod_rows(m, n_tokens):
    if m.shape[0] == 1:
        return m
    reps = n_tokens // m.shape[0]
    tok = lax.broadcasted_iota(jnp.int32, (n_tokens, m.shape[0]), 0)
    bat = lax.broadcasted_iota(jnp.int32, (n_tokens, m.shape[0]), 1)
    sel = jnp.where((tok >= bat * reps) & (tok < (bat + 1) * reps), 1.0, 0.0).astype(BF16)
    return _select_rows_mxu(sel, m)


def _modulated_norm(x, g, sc, sh):
    n = x.shape[0]
    return (_rms_rows(x) * (g * (1.0 + _mod_rows(sc, n))) + _mod_rows(sh, n)).astype(BF16)


def _sample_mod_specs(n_rows, chunks, index_map):
    return [pl.BlockSpec((None, n_rows, D_MODEL), functools.partial(index_map, chunk=c),
                         pipeline_mode=pl.Buffered(1)) for c in chunks]


def _mlp_tile(mix, x, g1, sh, sc, g2, ng, wout_ref, wup_ref, wdn_ref, fg, final):
    n = x.shape[0]
    x1 = x + _mod_rows(g1, n) * jnp.dot(mix.astype(BF16), wout_ref[...], preferred_element_type=F32)
    h = _modulated_norm(x1, ng, sc, sh)
    acc = None
    for c in range(D_FF // FF_CHUNK):
        u = jnp.dot(h, wup_ref[:, c * FF_CHUNK:(c + 1) * FF_CHUNK], preferred_element_type=F32)
        u = jnp.square(jnp.maximum(u, 0.0)).astype(BF16)
        d = jnp.dot(u, wdn_ref[c * FF_CHUNK:(c + 1) * FF_CHUNK, :], preferred_element_type=F32)
        acc = d if acc is None else acc + d
    x2 = x1 + _mod_rows(g2, n) * acc
    return _rms_rows(x2) * fg if final else x2


def _mlp_kernel(mix_ref, x_ref, g1_ref, sh_ref, sc_ref, g2_ref, smix_ref, sx_ref, sg1_ref, ssh_ref, ssc_ref, sg2_ref,
                ng_ref, wout_ref, wup_ref, wdn_ref, fg_ref, o_ref, so_ref, *, final):
    weights = (ng_ref[...], wout_ref, wup_ref, wdn_ref, fg_ref[...], final)
    o_ref[...] = _mlp_tile(mix_ref[...], x_ref[...], g1_ref[...], sh_ref[...], sc_ref[...], g2_ref[...], *weights)

    @pl.when(pl.program_id(0) == pl.num_programs(0) - 1)
    def _():
        so_ref[...] = _mlp_tile(smix_ref[...], sx_ref[...], sg1_ref[...], ssh_ref[...], ssc_ref[...], sg2_ref[...],
                                *weights)


def _out_mlp(mix2, x2, mod, mix_s, xs, mod_all, n_sample, norm_g, w_out, w_up, w_down, final_g, layer,
             tiles_per_batch, final):
    n = x2.shape[0]
    tile = TOK_TILE
    const = lambda i: (0, 0)
    of_layer = lambda i: (layer, 0, 0)
    mod_p = lambda chunk: pl.BlockSpec((None, 1, D_MODEL), lambda i: (i // tiles_per_batch, 0, chunk))
    whole = lambda a: pl.BlockSpec(a.shape, const, pipeline_mode=pl.Buffered(1))
    return pl.pallas_call(
        functools.partial(_mlp_kernel, final=final),
        out_shape=(jax.ShapeDtypeStruct((n, D_MODEL), F32), jax.ShapeDtypeStruct(xs.shape, F32)),
        grid=(n // tile,),
        in_specs=[pl.BlockSpec((tile, D_MODEL), lambda i: (i, 0)),
                  pl.BlockSpec((tile, D_MODEL), lambda i: (i, 0)),
                  mod_p(2), mod_p(3), mod_p(4), mod_p(5),
                  whole(mix_s), whole(xs)]
                 + _sample_mod_specs(n_sample, (2, 3, 4, 5), lambda i, chunk: (layer, 0, chunk)) + [
                  pl.BlockSpec((1, D_MODEL), const),
                  pl.BlockSpec((None, D_MODEL, D_MODEL), of_layer, pipeline_mode=pl.Buffered(1)),
                  pl.BlockSpec((None, D_MODEL, D_FF), of_layer, pipeline_mode=pl.Buffered(1)),
                  pl.BlockSpec((None, D_FF, D_MODEL), of_layer, pipeline_mode=pl.Buffered(1)),
                  pl.BlockSpec((1, D_MODEL), const)],
        out_specs=(pl.BlockSpec((tile, D_MODEL), lambda i: (i, 0)), pl.BlockSpec(xs.shape, const)),
        compiler_params=pltpu.CompilerParams(dimension_semantics=("arbitrary",), vmem_limit_bytes=VMEM_LIMIT),
        name="out_projection_mlp",
    )(mix2, x2, mod, mod, mod, mod, mix_s, xs, mod_all, mod_all, mod_all, mod_all, norm_g.reshape(1, D_MODEL), w_out,
      w_up, w_down, final_g.reshape(1, D_MODEL))


def _conv_ln_swish(acc, lng, lnb):
    mu = jnp.mean(acc, axis=-1, keepdims=True)
    xc = acc - mu
    y = xc * lax.rsqrt(jnp.mean(xc * xc, axis=-1, keepdims=True) + EPS) * lng + lnb
    return _silu(y)


def _hgrn_gates(fh, lb):
    f = lb + (1.0 - lb) * jax.nn.sigmoid(fh)
    return jnp.log(f), 1.0 - f


def _hgrn_out(o, hng, gate):
    return _rms_rows(o) * hng * _silu(gate)


def _ticker(pieces):
    it = iter(pieces)

    def tick():
        piece = next(it, None)
        if piece is not None:
            piece()

    def flush():
        for piece in it:
            piece()

    tick.flush = flush
    return tick


def _prompt_mix_kernel(sinks_ref, x_ref, sh_ref, sc_ref, sx_ref, ssh_ref, ssc_ref, ng_ref, win_ref, convw_ref,
                       convb_ref, lng_ref, lnb_ref, hlb_ref, hng_ref, bias_ref,
                       mix_ref, sproj_ref, convo_ref, so_ref, ko_ref, vo_ref,
                       proj_ref, wbf_ref, abuf, kbuf, vbuf, st_ref, *, layer, tile):
    t = pl.program_id(1)
    last = pl.num_programs(1) - 1

    @pl.when((pl.program_id(0) == 0) & (t == 0))
    def _():
        wbf_ref[...] = win_ref[...].astype(BF16)

    @pl.when(t == 0)
    def _():
        abuf[0:CONV_PAD, :] = jnp.zeros((CONV_PAD, D_CONV), F32)
        abuf[CONV_PAD + tile:CONV_PAD + tile + SUBLANES, :] = jnp.zeros((SUBLANES, D_CONV), F32)
        kbuf[0:ATTN_BLOCK, :] = jnp.zeros((ATTN_BLOCK, D_KV), BF16)
        vbuf[0:ATTN_BLOCK, :] = jnp.zeros((ATTN_BLOCK, D_KV), BF16)
        st_ref[...] = jnp.zeros(st_ref.shape, F32)

    h_in = _modulated_norm(x_ref[...], ng_ref[...], sc_ref[...], sh_ref[...])
    for lo, hi in ((OFF_AVAL, OFF_Q), (OFF_F, OFF_I), (OFF_Q, OFF_F), (OFF_I, OFF_G), (OFF_G, OFF_QA),
                   (OFF_QA, IN_WIDTH)):
        proj_ref[:, lo:hi] = jnp.dot(h_in, wbf_ref[:, lo:hi], preferred_element_type=F32)

    kbuf[ATTN_BLOCK:ATTN_BLOCK + tile, :] = proj_ref[:, OFF_KA:OFF_KA + D_KV].astype(BF16)
    vbuf[ATTN_BLOCK:ATTN_BLOCK + tile, :] = proj_ref[:, OFF_VA:OFF_VA + D_KV].astype(BF16)
    scale = HEAD_DIM ** -0.5
    attn = {}

    def attn_scores(blk):
        def run():
            r0 = blk * ATTN_BLOCK
            for h in range(H_ATTN):
                kv = h // GROUP
                q = (proj_ref[r0:r0 + ATTN_BLOCK, OFF_QA + h * HEAD_DIM:OFF_QA + (h + 1) * HEAD_DIM]
                     * scale).astype(BF16)
                kall = kbuf[r0:r0 + 2 * ATTN_BLOCK, kv * HEAD_DIM:(kv + 1) * HEAD_DIM]
                attn[blk, h] = lax.dot_general(q, kall, NT_DIMS, preferred_element_type=F32)
        return run

    def attn_softmax(blk):
        def run():
            for h in range(H_ATTN):
                s = attn[blk, h] + bias_ref[h]
                if blk == 0:
                    col = lax.broadcasted_iota(jnp.int32, s.shape, 1)
                    s = jnp.where(col + (t * tile - ATTN_BLOCK) >= 0, s, -jnp.inf)
                p, den = _sink_softmax(s, sinks_ref[h])
                attn[blk, h] = (p.astype(BF16), den)
        return run

    def attn_values(blk):
        def run():
            r0 = blk * ATTN_BLOCK
            heads = []
            for h in range(H_ATTN):
                kv = h // GROUP
                p, den = attn[blk, h]
                vall = vbuf[r0:r0 + 2 * ATTN_BLOCK, kv * HEAD_DIM:(kv + 1) * HEAD_DIM]
                heads.append(jnp.dot(p, vall, preferred_element_type=F32) / den)
            mix_ref[r0:r0 + ATTN_BLOCK, D_CONV + D_HGRN:D_MODEL] = jnp.concatenate(heads, axis=1).astype(BF16)
        return run

    tick = _ticker([stage(blk) for blk in range(tile // ATTN_BLOCK)
                    for stage in (attn_scores, attn_softmax, attn_values)])

    abuf[CONV_PAD:CONV_PAD + tile, :] = (proj_ref[:, OFF_AVAL:OFF_AVAL + D_CONV]
                                         * jax.nn.sigmoid(proj_ref[:, OFF_AGATE:OFF_AGATE + D_CONV]))
    first_row = CONV_PAD - (CONV_WIDTH - 1)
    acc = jnp.broadcast_to(convb_ref[...], (tile, D_CONV))
    for r in range(SUBLANES):
        z = None
        for off in range(r, first_row + CONV_WIDTH, SUBLANES):
            j = off - first_row
            if j < 0:
                continue
            term = convw_ref[j:j + 1, :] * abuf[off - r:off - r + tile + SUBLANES, :]
            z = term if z is None else z + term
        acc = acc + (z[0:tile] if r == 0 else pltpu.roll(z, tile + SUBLANES - r, 0)[0:tile])
    mix_ref[:, 0:D_CONV] = _conv_ln_swish(acc, lng_ref[...], lnb_ref[...]).astype(BF16)
    tick()

    lb = _layer_lb(hlb_ref[...], layer)
    hng = hng_ref[...]
    span = HGRN_SPAN * HGRN_CHUNK
    ri = lax.broadcasted_iota(jnp.int32, (span, span), 0)
    ci = lax.broadcasted_iota(jnp.int32, (span, span), 1)
    tri = jnp.where((ri >= ci) & (ri // HGRN_CHUNK == ci // HGRN_CHUNK), 1.0, 0.0).astype(BF16)
    for i in range(tile // span):
        _hgrn_span(proj_ref, mix_ref, st_ref, i * span, lb, hng, tri, tick)
    tick.flush()

    @pl.when(t == last)
    def _():
        convo_ref[...] = abuf[CONV_PAD + tile - (CONV_WIDTH - 1):CONV_PAD + tile, :]
        for h in range(H_HGRN):
            so_ref[h] = st_ref[h].T
        ko_ref[...] = proj_ref[tile - WINDOW:tile, OFF_KA:OFF_KA + D_KV]
        vo_ref[...] = proj_ref[tile - WINDOW:tile, OFF_VA:OFF_VA + D_KV]

    abuf[0:CONV_PAD, :] = abuf[tile:tile + CONV_PAD, :]
    kbuf[0:ATTN_BLOCK, :] = kbuf[tile:tile + ATTN_BLOCK, :]
    vbuf[0:ATTN_BLOCK, :] = vbuf[tile:tile + ATTN_BLOCK, :]

    @pl.when((pl.program_id(0) == pl.num_programs(0) - 1) & (t == last))
    def _():
        h_s = _modulated_norm(sx_ref[...], ng_ref[...], ssc_ref[...], ssh_ref[...])
        sproj_ref[...] = jnp.dot(h_s, wbf_ref[...], preferred_element_type=F32)


def _carry_specs(carried):
    return [pl.BlockSpec(memory_space=pl.ANY)] * len(carried)


def _without_carry(kernel_fn, n_in, n_carried, *refs):
    return kernel_fn(*refs[:n_in], *refs[n_in + n_carried:])


def _carried(kernel_fn, n_in, n_carried):
    return functools.partial(_without_carry, kernel_fn, n_in, n_carried)


def _prompt_mixers(x, mod, xs, mod_all, n_sample, norm_g, w_in, sinks, conv_w, conv_b, ln_g, ln_b, hgrn_lb, hng,
                   bias_p, layer, carried):
    B, T = x.shape[:2]
    tile = MIX_TILE
    depth = hgrn_lb.shape[0]
    const2 = lambda b, t: (0, 0)
    inputs = (sinks, x, mod, mod, xs, mod_all, mod_all, norm_g.reshape(1, D_MODEL), w_in, conv_w,
              conv_b.reshape(1, D_CONV), ln_g.reshape(1, D_CONV), ln_b.reshape(1, D_CONV), hgrn_lb,
              hng.reshape(1, DV_HGRN), bias_p)
    return pl.pallas_call(
        _carried(functools.partial(_prompt_mix_kernel, layer=layer, tile=tile), len(inputs), len(carried)),
        out_shape=(jax.ShapeDtypeStruct((B, T, D_MODEL), BF16),
                   jax.ShapeDtypeStruct((xs.shape[0], IN_WIDTH), F32),
                   jax.ShapeDtypeStruct((depth, B, CONV_WIDTH - 1, D_CONV), F32),
                   jax.ShapeDtypeStruct((depth, B, H_HGRN, DK_HGRN, DV_HGRN), F32),
                   jax.ShapeDtypeStruct((depth, B, WINDOW, D_KV), F32),
                   jax.ShapeDtypeStruct((depth, B, WINDOW, D_KV), F32)),
        grid=(B, T // tile),
        in_specs=[pl.BlockSpec(memory_space=pltpu.SMEM),
                  pl.BlockSpec((None, tile, D_MODEL), lambda b, t: (b, t, 0)),
                  pl.BlockSpec((None, 1, D_MODEL), lambda b, t: (b, 0, 0)),
                  pl.BlockSpec((None, 1, D_MODEL), lambda b, t: (b, 0, 1)),
                  pl.BlockSpec(xs.shape, const2, pipeline_mode=pl.Buffered(1))]
                 + _sample_mod_specs(n_sample, (0, 1), lambda b, t, chunk: (layer, 0, chunk)) + [
                  pl.BlockSpec((1, D_MODEL), const2),
                  pl.BlockSpec((None, D_MODEL, IN_WIDTH), lambda b, t: (layer, 0, 0), pipeline_mode=pl.Buffered(1)),
                  pl.BlockSpec((CONV_WIDTH, D_CONV), const2),
                  pl.BlockSpec((1, D_CONV), const2),
                  pl.BlockSpec((1, D_CONV), const2),
                  pl.BlockSpec((1, D_CONV), const2),
                  pl.BlockSpec((depth, D_HGRN), const2),
                  pl.BlockSpec((1, DV_HGRN), const2),
                  pl.BlockSpec((H_ATTN, ATTN_BLOCK, 2 * ATTN_BLOCK), lambda b, t: (0, 0, 0))] + _carry_specs(carried),
        out_specs=(pl.BlockSpec((None, tile, D_MODEL), lambda b, t: (b, t, 0)),
                   pl.BlockSpec((xs.shape[0], IN_WIDTH), const2),
                   pl.BlockSpec((None, None, CONV_WIDTH - 1, D_CONV), lambda b, t: (layer, b, 0, 0)),
                   pl.BlockSpec((None, None, H_HGRN, DK_HGRN, DV_HGRN), lambda b, t: (layer, b, 0, 0, 0)),
                   pl.BlockSpec((None, None, WINDOW, D_KV), lambda b, t: (layer, b, 0, 0)),
                   pl.BlockSpec((None, None, WINDOW, D_KV), lambda b, t: (layer, b, 0, 0))),
        input_output_aliases={len(inputs) + i: 2 + i for i in range(len(carried))},
        scratch_shapes=[pltpu.VMEM((tile, IN_WIDTH), F32),
                        pltpu.VMEM((D_MODEL, IN_WIDTH), BF16),
                        pltpu.VMEM((CONV_PAD + tile + SUBLANES, D_CONV), F32),
                        pltpu.VMEM((ATTN_BLOCK + tile, D_KV), BF16),
                        pltpu.VMEM((ATTN_BLOCK + tile, D_KV), BF16),
                        pltpu.VMEM((H_HGRN, DV_HGRN, DK_HGRN), F32)],
        compiler_params=pltpu.CompilerParams(dimension_semantics=("arbitrary", "arbitrary"),
                                             vmem_limit_bytes=VMEM_LIMIT),
        name="prompt_mixers",
    )(*inputs, *carried)


def _sample_mix_kernel(sinks_ref, proj_ref, cconv_ref, state_ref, ck_ref, cv_ref, convw_ref, convb_ref, lng_ref,
                       lnb_ref, hlb_ref, hng_ref, bias_ref,
                       mix_ref, convo_ref, so_ref, ko_ref, vo_ref, kpad_ref, vpad_ref, *,
                       layer, block, seq, w_buf):
    hist = CONV_WIDTH - 1

    @pl.when(pl.program_id(0) == 0)
    def _():
        for ref in (kpad_ref, vpad_ref):
            ref[:, 0:w_buf - seq, :] = jnp.zeros((block, w_buf - seq, D_KV), F32)

    lb = _layer_lb(hlb_ref[...], layer)
    hng = hng_ref[...]
    scale = HEAD_DIM ** -0.5
    elems = range(block)
    row8 = lax.broadcasted_iota(jnp.int32, (SUBLANES, DV_HGRN), 0)
    ones_rows = jnp.where((row8 >= seq) & (row8 < seq + 3), 1.0, 0.0)
    zrow = jnp.zeros((1, DK_HGRN), BF16)
    prow = lax.broadcasted_iota(jnp.int32, (seq, seq), 0)
    pcol = lax.broadcasted_iota(jnp.int32, (seq, seq), 1)
    grow = lax.broadcasted_iota(jnp.int32, (GROUP * seq, 1), 0)

    proj = [proj_ref[e * seq:(e + 1) * seq, :] for e in elems]

    glu = [p[:, OFF_AVAL:OFF_AVAL + D_CONV] * jax.nn.sigmoid(p[:, OFF_AGATE:OFF_AGATE + D_CONV]) for p in proj]
    full = [cconv_ref[i] for i in range(hist)]
    full += [jnp.concatenate([glu[e][t:t + 1] for e in elems], axis=0) for t in range(seq)]
    for i in range(hist):
        convo_ref[i] = full[i + seq]
    conv_out = []
    for t in range(seq):
        acc = jnp.broadcast_to(convb_ref[...], (block, D_CONV))
        for j in range(CONV_WIDTH):
            acc = acc + convw_ref[j:j + 1, :] * full[t + j]
        conv_out.append(_conv_ln_swish(acc, lng_ref[...], lnb_ref[...]))
    out_a = [jnp.concatenate([conv_out[t][e:e + 1] for t in range(seq)], axis=0) for e in elems]

    units = [(e, h) for e in elems for h in range(H_HGRN)]
    ops = {}
    for e in elems:
        p = proj[e]
        g, k = _hgrn_gates(p[:, OFF_F:OFF_F + D_HGRN], lb)
        b = _cumsum_rows_small(g)
        for h in range(H_HGRN):
            cs = slice(h * DK_HGRN, (h + 1) * DK_HGRN)
            q = p[:, OFF_Q + h * DK_HGRN:OFF_Q + (h + 1) * DK_HGRN]
            v = p[:, OFF_I + h * DV_HGRN:OFF_I + (h + 1) * DV_HGRN]
            bb = b[:, cs]
            bl = bb[seq - 1:seq, :]
            kst = (k[:, cs] * jnp.exp(bl - bb)).astype(BF16)
            x = jnp.concatenate([kst.astype(F32), *(part.astype(F32) for part in _split3_bf16(jnp.exp(bl))),
                                 zrow.astype(F32)], axis=0).astype(BF16)
            vpad = jnp.concatenate([v, jnp.zeros((SUBLANES - seq, DV_HGRN), F32)], axis=0)
            ops[e, h] = dict(qp=(q * jnp.exp(bb - bl)).astype(BF16), kst=kst, v=v.astype(BF16),
                             qt=(q * jnp.exp(bb)).astype(BF16), x=x,
                             r=jnp.concatenate([vpad, ones_rows], axis=1).astype(BF16))
    for u in units:
        o = ops[u]
        st = state_ref[u[0], u[1]]
        o["p"] = lax.dot_general(o["qp"], o["kst"], NT_DIMS, preferred_element_type=F32)
        o["inter"] = jnp.dot(o["qt"], st.astype(BF16), preferred_element_type=F32)
        me = lax.dot_general(o["x"], o["r"], TN_DIMS, preferred_element_type=F32)
        so_ref[u[0], u[1]] = me[:, DV_HGRN:] * st + me[:, :DV_HGRN]
    out_b = {}
    for u in units:
        o = ops[u]
        pm = jnp.where(prow >= pcol, o["p"], 0.0).astype(BF16)
        out = o["inter"] + jnp.dot(pm, o["v"], preferred_element_type=F32)
        gate = proj[u[0]][:, OFF_G + u[1] * DV_HGRN:OFF_G + (u[1] + 1) * DV_HGRN]
        out_b[u] = _hgrn_out(out, hng, gate)

    scores = {}
    for e in elems:
        p = proj[e]
        for kv in range(KV_HEADS):
            hs = slice(kv * HEAD_DIM, (kv + 1) * HEAD_DIM)
            q2 = jnp.concatenate([p[:, OFF_QA + h * HEAD_DIM:OFF_QA + (h + 1) * HEAD_DIM]
                                  for h in range(kv * GROUP, (kv + 1) * GROUP)], axis=0)
            q2 = (q2 * scale).astype(BF16)
            bias = bias_ref[kv]
            s_c = jnp.dot(q2, ck_ref[e, hs, :].astype(BF16), preferred_element_type=F32) + bias[:, 0:w_buf]
            s_n = (lax.dot_general(q2, p[:, OFF_KA + kv * HEAD_DIM:OFF_KA + (kv + 1) * HEAD_DIM].astype(BF16), NT_DIMS,
                                   preferred_element_type=F32) + bias[:, w_buf:w_buf + seq])
            scores[e, kv] = (s_c, s_n)
    out_c = {}
    for e in elems:
        p = proj[e]
        for kv in range(KV_HEADS):
            hs = slice(kv * HEAD_DIM, (kv + 1) * HEAD_DIM)
            sink = jnp.zeros((GROUP * seq, 1), F32)
            for gi in range(GROUP):
                sink = jnp.where(grow >= gi * seq, sinks_ref[kv * GROUP + gi], sink)
            s_c, s_n = scores[e, kv]
            m = jnp.maximum(jnp.maximum(jnp.max(s_c, axis=-1, keepdims=True), jnp.max(s_n, axis=-1, keepdims=True)),
                            sink)
            p_c = jnp.exp(s_c - m)
            p_n = jnp.exp(s_n - m)
            den = jnp.sum(p_c, axis=-1, keepdims=True) + jnp.sum(p_n, axis=-1, keepdims=True) + jnp.exp(sink - m)
            vnew = p[:, OFF_VA + kv * HEAD_DIM:OFF_VA + (kv + 1) * HEAD_DIM].astype(BF16)
            o2 = (lax.dot_general(p_c.astype(BF16), cv_ref[e, hs, :].astype(BF16), NT_DIMS,
                                  preferred_element_type=F32)
                  + jnp.dot(p_n.astype(BF16), vnew, preferred_element_type=F32)) / den
            for gi in range(GROUP):
                out_c[e, kv * GROUP + gi] = o2[gi * seq:(gi + 1) * seq]
    lane = lax.broadcasted_iota(jnp.int32, (D_KV, w_buf), 1)
    for e in elems:
        p = proj[e]
        for pad, cache, new, out in ((kpad_ref, ck_ref, p[:, OFF_KA:OFF_KA + D_KV], ko_ref),
                                     (vpad_ref, cv_ref, p[:, OFF_VA:OFF_VA + D_KV], vo_ref)):
            pad[e, w_buf - seq:w_buf, :] = new
            out[e] = jnp.where(lane >= w_buf - seq, pad[e].T, pltpu.roll(cache[e], w_buf - seq, 1))

    for e in elems:
        parts = [out_a[e]] + [out_b[e, h] for h in range(H_HGRN)] + [out_c[e, h] for h in range(H_ATTN)]
        mix_ref[e * seq:(e + 1) * seq, :] = jnp.concatenate(parts, axis=1)


def _sample_mixers(proj2, sinks, cache_conv, state, cache_k, cache_v, conv_w, conv_b, ln_g, ln_b, hgrn_lb, hng,
                   bias_s, layer, carried):
    B = state.shape[1]
    seq = proj2.shape[0] // B
    w_buf = cache_k.shape[3]
    block = SAMPLE_BLOCK
    depth = hgrn_lb.shape[0]
    hist = CONV_WIDTH - 1
    const2 = lambda i: (0, 0)
    cache_specs = [pl.BlockSpec((None, hist, block, D_CONV), lambda i: (layer, 0, i, 0)),
                   pl.BlockSpec((None, block, H_HGRN, DK_HGRN, DV_HGRN), lambda i: (layer, i, 0, 0, 0)),
                   pl.BlockSpec((None, block, D_KV, w_buf), lambda i: (layer, i, 0, 0)),
                   pl.BlockSpec((None, block, D_KV, w_buf), lambda i: (layer, i, 0, 0))]
    inputs = (sinks, proj2, cache_conv, state, cache_k, cache_v, conv_w, conv_b.reshape(1, D_CONV),
              ln_g.reshape(1, D_CONV), ln_b.reshape(1, D_CONV), hgrn_lb, hng.reshape(1, DV_HGRN), bias_s)
    return pl.pallas_call(
        _carried(functools.partial(_sample_mix_kernel, layer=layer, block=block, seq=seq, w_buf=w_buf),
                 len(inputs), len(carried)),
        out_shape=(jax.ShapeDtypeStruct((B * seq, D_MODEL), F32),
                   jax.ShapeDtypeStruct((depth, hist, B, D_CONV), F32),
                   jax.ShapeDtypeStruct((depth, B, H_HGRN, DK_HGRN, DV_HGRN), F32),
                   jax.ShapeDtypeStruct((depth, B, D_KV, w_buf), F32),
                   jax.ShapeDtypeStruct((depth, B, D_KV, w_buf), F32)),
        grid=(B // block,),
        in_specs=[pl.BlockSpec(memory_space=pltpu.SMEM),
                  pl.BlockSpec((block * seq, IN_WIDTH), lambda i: (i, 0))] + cache_specs + [
                  pl.BlockSpec((CONV_WIDTH, D_CONV), const2),
                  pl.BlockSpec((1, D_CONV), const2),
                  pl.BlockSpec((1, D_CONV), const2),
                  pl.BlockSpec((1, D_CONV), const2),
                  pl.BlockSpec((depth, D_HGRN), const2),
                  pl.BlockSpec((1, DV_HGRN), const2),
                  pl.BlockSpec((KV_HEADS, GROUP * seq, 2 * ATTN_BLOCK), lambda i: (0, 0, 0))] + _carry_specs(carried),
        out_specs=tuple([pl.BlockSpec((block * seq, D_MODEL), lambda i: (i, 0))] + cache_specs),
        input_output_aliases={len(inputs) + i: 1 + i for i in range(len(carried))},
        scratch_shapes=[pltpu.VMEM((block, w_buf, D_KV), F32),
                        pltpu.VMEM((block, w_buf, D_KV), F32)],
        compiler_params=pltpu.CompilerParams(dimension_semantics=("arbitrary",), vmem_limit_bytes=VMEM_LIMIT),
        name="sample_mixers",
    )(*inputs, *carried)


def kernel(x_prompt, x_sample, cache_conv, state_hgrn, cache_swa_k, cache_swa_v, c_prompt, c_sample, rel_bias, w_ada, b_ada, norm_mix_g, w_in, conv_w, conv_b, conv_ln_g, conv_ln_b, hgrn_lb, hgrn_norm_g, attn_sinks, w_out, norm_mlp_g, w_up, w_down, final_g):
    Bp, Tp = x_prompt.shape[:2]
    Bs, Ts = x_sample.shape[:2]
    depth = w_in.shape[0]
    w_buf = cache_swa_k.shape[2]
    assert Tp % MIX_TILE == 0 and (Bp * Tp) % TOK_TILE == 0 and Tp % TOK_TILE == 0 and Bs % SAMPLE_BLOCK == 0
    assert w_buf == WINDOW and GROUP * Ts == SUBLANES

    bias_p, bias_s = _bias_tables(rel_bias, Ts, w_buf)
    mod = _modulation(jnp.concatenate([c_sample, c_prompt], axis=0), w_ada, b_ada)
    w_out_b, w_up_b, w_down_b = (w.astype(BF16) for w in (w_out, w_up, w_down))
    hlb = hgrn_lb.astype(F32)
    cc = jnp.swapaxes(cache_conv, 1, 2)
    ck = jnp.swapaxes(cache_swa_k.reshape(depth, Bs, w_buf, D_KV), 2, 3)
    cv = jnp.swapaxes(cache_swa_v.reshape(depth, Bs, w_buf, D_KV), 2, 3)

    xp = x_prompt.reshape(Bp * Tp, D_MODEL)
    xs = x_sample.reshape(Bs * Ts, D_MODEL)
    caches_p = ()
    caches_s = ()
    for l in range(depth):
        final = l == depth - 1
        mod_p = mod[l, Bs:].reshape(Bp, 1, N_MOD * D_MODEL)
        mix_p, proj_s, *caches_p = _prompt_mixers(xp.reshape(Bp, Tp, D_MODEL), mod_p, xs, mod, Bs, norm_mix_g[l], w_in,
                                                  attn_sinks[l], conv_w[l], conv_b[l], conv_ln_g[l], conv_ln_b[l], hlb,
                                                  hgrn_norm_g[l], bias_p, l, caches_p)
        mix_s, *caches_s = _sample_mixers(proj_s, attn_sinks[l], cc, state_hgrn, ck, cv, conv_w[l],
                                          conv_b[l], conv_ln_g[l], conv_ln_b[l], hlb, hgrn_norm_g[l], bias_s, l,
                                          caches_s)
        xp, xs = _out_mlp(mix_p.reshape(Bp * Tp, D_MODEL), xp, mod_p, mix_s, xs, mod, Bs, norm_mlp_g[l], w_out_b,
                          w_up_b, w_down_b, final_g, l, Tp // TOK_TILE, final)
    cp, sp, kp, vp = caches_p
    cs, ss, ksn, vsn = caches_s
    cs, ksn, vsn = jnp.swapaxes(cs, 1, 2), jnp.swapaxes(ksn, 2, 3), jnp.swapaxes(vsn, 2, 3)
    return (xp.reshape(Bp, Tp, D_MODEL), xs.reshape(Bs, Ts, D_MODEL), cp, cs, sp, ss,
            kp.reshape(depth, Bp, WINDOW, KV_HEADS, HEAD_DIM), ksn.reshape(depth, Bs, w_buf, KV_HEADS, HEAD_DIM),
            vp.reshape(depth, Bp, WINDOW, KV_HEADS, HEAD_DIM), vsn.reshape(depth, Bs, w_buf, KV_HEADS, HEAD_DIM))
```

```python
import functools
import math

import jax
import jax.numpy as jnp
from jax import lax
from jax.experimental import pallas as pl
from jax.experimental.pallas import tpu as pltpu

F32 = jnp.float32
BF16 = jnp.bfloat16

D_MODEL = 1024
D_CONV = 256
CONV_WIDTH = 31
H_HGRN = 4
DK_HGRN = 128
DV_HGRN = 128
D_HGRN = 512
HEAD_DIM = 64
H_ATTN = 4
KV_HEADS = 2
GROUP = H_ATTN // KV_HEADS
D_ATTN = H_ATTN * HEAD_DIM
D_KV = KV_HEADS * HEAD_DIM
WINDOW = 128
ATTN_BLOCK = 128
NUM_BUCKETS = 32
MAX_DISTANCE = 128
D_FF = 4 * D_MODEL
N_MOD = 6
EPS = 1e-6

OFF_AVAL = 0
OFF_AGATE = OFF_AVAL + D_CONV
OFF_Q = OFF_AGATE + D_CONV
OFF_F = OFF_Q + H_HGRN * DK_HGRN
OFF_I = OFF_F + H_HGRN * DK_HGRN
OFF_G = OFF_I + D_HGRN
OFF_QA = OFF_G + D_HGRN
OFF_KA = OFF_QA + D_ATTN
OFF_VA = OFF_KA + D_KV
IN_WIDTH = OFF_VA + D_KV

HGRN_CHUNK = 64
HGRN_KEYBLOCK = 32
HGRN_SPAN = 4
SUBLANES = 8
CONV_PAD = 32
MIX_TILE = 512
TOK_TILE = 512
SAMPLE_BLOCK = 16
FF_CHUNK = 1024
CAST_STEPS = 16
MOD_COLS = 2048
VMEM_LIMIT = 56 * 1024 * 1024

NT_DIMS = (((1,), (1,)), ((), ()))
TN_DIMS = (((0,), (0,)), ((), ()))


def _silu(x):
    return x * jax.nn.sigmoid(x)


def _rms_rows(x):
    return x * lax.rsqrt(jnp.mean(x * x, axis=-1, keepdims=True) + EPS)


def _layer_lb(hlb, layer):
    m = jnp.max(hlb, axis=0, keepdims=True)
    e = jnp.exp(hlb - m)
    p = e / jnp.sum(e, axis=0, keepdims=True)
    lb = jnp.zeros_like(m)
    for i in range(1, layer + 1):
        lb = lb + p[i:i + 1, :]
    return lb


def _split3_bf16(x):
    hi = x.astype(BF16)
    r = x - hi.astype(F32)
    mid = r.astype(BF16)
    return hi, mid, (r - mid.astype(F32)).astype(BF16)


def _select_rows_mxu(sel, x):
    return sum(jnp.dot(sel, part, preferred_element_type=F32) for part in _split3_bf16(x))


def _cumsum_rows_small(g):
    row = lax.broadcasted_iota(jnp.int32, g.shape, 0)
    b = jnp.zeros_like(g)
    for u in range(g.shape[0]):
        b = b + jnp.where(row >= u, g[u:u + 1, :], 0.0)
    return b


def _hgrn_span(proj_ref, mix_ref, st_ref, row0, lb, hng, tri, tick):
    L, KB = HGRN_CHUNK, HGRN_KEYBLOCK
    span = HGRN_SPAN * L
    g, k = _hgrn_gates(proj_ref[pl.ds(row0, span), OFF_F:OFF_F + D_HGRN], lb)
    b = _select_rows_mxu(tri, g)
    units = [(c, h) for c in range(HGRN_SPAN) for h in range(H_HGRN)]

    ops = {}
    for c, h in units:
        rows = pl.ds(row0 + c * L, L)
        cs = slice(h * DK_HGRN, (h + 1) * DK_HGRN)
        q = proj_ref[rows, OFF_Q + h * DK_HGRN:OFF_Q + (h + 1) * DK_HGRN]
        v = proj_ref[rows, OFF_I + h * DV_HGRN:OFF_I + (h + 1) * DV_HGRN].astype(BF16)
        kk = k[c * L:(c + 1) * L, cs]
        bb = b[c * L:(c + 1) * L, cs]
        qp, kp = [], []
        for lo in range(0, L, KB):
            r = bb[lo + KB // 2 - 1:lo + KB // 2, :]
            kp.append((kk[lo:lo + KB] * jnp.exp(r - bb[lo:lo + KB])).astype(BF16))
            qp.append((q[lo:] * jnp.exp(bb[lo:] - r)).astype(BF16))
        bl = bb[L - 1:L, :]
        ops[c, h] = dict(qp=qp, kp=kp, v=v, qt=(q * jnp.exp(bb)).astype(BF16),
                         kst=(kk * jnp.exp(bl - bb)).astype(BF16), e=jnp.exp(bl))
    tick()

    for u in units:
        o = ops[u]
        o["p"] = [lax.dot_general(qp, kp, NT_DIMS, preferred_element_type=F32) for qp, kp in zip(o["qp"], o["kp"])]
        o["m"] = lax.dot_general(o["v"], o["kst"], TN_DIMS, preferred_element_type=F32)
    tick()

    for u in units:
        pm = []
        for p in ops[u]["p"]:
            row = lax.broadcasted_iota(jnp.int32, p.shape, 0)
            col = lax.broadcasted_iota(jnp.int32, p.shape, 1)
            pm.append(jnp.where(row >= col, p, 0.0).astype(BF16))
        ops[u]["p"] = pm
    tick()

    for u in units:
        o = ops[u]
        blocks = [None] * (L // KB)
        for j, p in enumerate(o["p"]):
            cj = jnp.dot(p, o["v"][j * KB:(j + 1) * KB], preferred_element_type=F32)
            for i in range(j, L // KB):
                piece = cj[(i - j) * KB:(i - j + 1) * KB]
                blocks[i] = piece if blocks[i] is None else blocks[i] + piece
        o["o"] = jnp.concatenate(blocks, axis=0)
    tick()

    for h in range(H_HGRN):
        st = st_ref[h]
        for c in range(HGRN_SPAN):
            o = ops[c, h]
            out = o["o"] + lax.dot_general(o["qt"], st.astype(BF16), NT_DIMS, preferred_element_type=F32)
            st = o["e"] * st + o["m"]
            rows = pl.ds(row0 + c * L, L)
            gate = proj_ref[rows, OFF_G + h * DV_HGRN:OFF_G + (h + 1) * DV_HGRN]
            mix_ref[rows, D_CONV + h * DV_HGRN:D_CONV + (h + 1) * DV_HGRN] = _hgrn_out(out, hng, gate).astype(BF16)
        st_ref[h] = st


def _sink_softmax(s, sink):
    m = jnp.maximum(jnp.max(s, axis=-1, keepdims=True), sink)
    p = jnp.exp(s - m)
    return p, jnp.sum(p, axis=-1, keepdims=True) + jnp.exp(sink - m)


def _bias_kernel(tab_ref, bp_ref, bs_ref, op_ref, os_ref, *, seq):
    bk = bp_ref[...]
    for h in range(H_ATTN):
        acc = jnp.full(bk.shape, -jnp.inf, F32)
        for bkt in range(NUM_BUCKETS):
            acc = jnp.where(bk == bkt, tab_ref[bkt, h], acc)
        op_ref[h] = acc
    bk = bs_ref[...]
    row = lax.broadcasted_iota(jnp.int32, bk.shape, 0)
    for kv in range(KV_HEADS):
        acc = jnp.full(bk.shape, -jnp.inf, F32)
        for bkt in range(NUM_BUCKETS):
            val = jnp.full(bk.shape, tab_ref[bkt, kv * GROUP], F32)
            for gi in range(1, GROUP):
                val = jnp.where(row >= gi * seq, tab_ref[bkt, kv * GROUP + gi], val)
            acc = jnp.where(bk == bkt, val, acc)
        os_ref[kv] = acc


def _t5_bucket(rel):
    n = jnp.maximum(rel, 0)
    max_exact = NUM_BUCKETS // 2
    nf = jnp.maximum(n, max_exact).astype(F32)
    large = max_exact + (jnp.log(nf / max_exact) / math.log(MAX_DISTANCE / max_exact)
                         * (NUM_BUCKETS - max_exact)).astype(jnp.int32)
    large = jnp.minimum(large, NUM_BUCKETS - 1)
    return jnp.where(n < max_exact, n, large)


def _bias_tables(rel_bias, dec_seq, w_buf):
    qi = jnp.arange(ATTN_BLOCK, dtype=jnp.int32)[:, None]
    kc = jnp.arange(2 * ATTN_BLOCK, dtype=jnp.int32)[None, :]
    rel_p = qi + ATTN_BLOCK - kc
    bucket_p = jnp.where((rel_p >= 0) & (rel_p <= WINDOW), _t5_bucket(rel_p), -1)
    ts = (jnp.arange(GROUP * dec_seq, dtype=jnp.int32) % dec_seq)[:, None]
    js = jnp.arange(2 * ATTN_BLOCK, dtype=jnp.int32)[None, :]
    rel_s = w_buf + ts - js
    ok_s = (rel_s >= 0) & (rel_s <= WINDOW) & (js < w_buf + dec_seq)
    bucket_s = jnp.where(ok_s, _t5_bucket(rel_s), -1)
    return pl.pallas_call(
        functools.partial(_bias_kernel, seq=dec_seq),
        out_shape=(jax.ShapeDtypeStruct((H_ATTN, ATTN_BLOCK, 2 * ATTN_BLOCK), F32),
                   jax.ShapeDtypeStruct((KV_HEADS, GROUP * dec_seq, 2 * ATTN_BLOCK), F32)),
        in_specs=[pl.BlockSpec(memory_space=pltpu.SMEM),
                  pl.BlockSpec(memory_space=pltpu.VMEM),
                  pl.BlockSpec(memory_space=pltpu.VMEM)],
        out_specs=(pl.BlockSpec(memory_space=pltpu.VMEM), pl.BlockSpec(memory_space=pltpu.VMEM)),
        name="rel_bias_tables",
    )(rel_bias.astype(F32), bucket_p, bucket_s)


def _mod_kernel(c_ref, w_ref, b_ref, o_ref):
    s = _silu(c_ref[...]).astype(BF16)
    o_ref[...] = jnp.dot(s, w_ref[...].astype(BF16), preferred_element_type=F32) + b_ref[...]


def _modulation(c_all, w_ada, b_ada):
    depth = w_ada.shape[0]
    n = c_all.shape[0]
    return pl.pallas_call(
        _mod_kernel,
        out_shape=jax.ShapeDtypeStruct((depth, n, N_MOD * D_MODEL), F32),
        grid=(depth, N_MOD * D_MODEL // MOD_COLS),
        in_specs=[pl.BlockSpec((n, D_MODEL), lambda l, j: (0, 0)),
                  pl.BlockSpec((None, D_MODEL, MOD_COLS), lambda l, j: (l, 0, j)),
                  pl.BlockSpec((None, 1, MOD_COLS), lambda l, j: (l, 0, j))],
        out_specs=pl.BlockSpec((None, n, MOD_COLS), lambda l, j: (l, 0, j)),
        compiler_params=pltpu.CompilerParams(dimension_semantics=("arbitrary", "arbitrary"),
                                             vmem_limit_bytes=VMEM_LIMIT),
        name="adaln_modulation",
    )(c_all, w_ada, b_ada.reshape(depth, 1, N_MOD * D_MODEL))


def _mod_rows(m, n_tokens):
    if m.shape[0] == 1:
        return m
    reps = n_tokens // m.shape[0]
    tok = lax.broadcasted_iota(jnp.int32, (n_tokens, m.shape[0]), 0)
    bat = lax.broadcasted_iota(jnp.int32, (n_tokens, m.shape[0]), 1)
    sel = jnp.where((tok >= bat * reps) & (tok < (bat + 1) * reps), 1.0, 0.0).astype(BF16)
    return _select_rows_mxu(sel, m)


def _modulated_norm(x, g, sc, sh):
    n = x.shape[0]
    return (_rms_rows(x) * (g * (1.0 + _mod_rows(sc, n))) + _mod_rows(sh, n)).astype(BF16)


def _sample_mod_specs(n_rows, chunks, index_map):
    return [pl.BlockSpec((None, n_rows, D_MODEL), functools.partial(index_map, chunk=c),
                         pipeline_mode=pl.Buffered(1)) for c in chunks]


def _mlp_tile(mix, x, g1, sh, sc, g2, ng, wout_ref, wup_ref, wdn_ref, fg, final):
    n = x.shape[0]
    x1 = x + _mod_rows(g1, n) * jnp.dot(mix.astype(BF16), wout_ref[...], preferred_element_type=F32)
    h = _modulated_norm(x1, ng, sc, sh)
    acc = None
    for c in range(D_FF // FF_CHUNK):
        u = jnp.dot(h, wup_ref[:, c * FF_CHUNK:(c + 1) * FF_CHUNK], preferred_element_type=F32)
        u = jnp.square(jnp.maximum(u, 0.0)).astype(BF16)
        d = jnp.dot(u, wdn_ref[c * FF_CHUNK:(c + 1) * FF_CHUNK, :], preferred_element_type=F32)
        acc = d if acc is None else acc + d
    x2 = x1 + _mod_rows(g2, n) * acc
    return _rms_rows(x2) * fg if final else x2


def _mlp_kernel(mix_ref, x_ref, g1_ref, sh_ref, sc_ref, g2_ref, smix_ref, sx_ref, sg1_ref, ssh_ref, ssc_ref, sg2_ref,
                ng_ref, wout_ref, wup_ref, wdn_ref, fg_ref, o_ref, so_ref, wout_s, wup_s, wdn_s, *, final):
    s = pl.program_id(0)

    @pl.when(s < CAST_STEPS)
    def _():
        for src, dst in ((wout_ref, wout_s), (wup_ref, wup_s), (wdn_ref, wdn_s)):
            rows = src.shape[0]
            dst[pl.ds(pl.multiple_of(s * rows, rows), rows), :] = src[...].astype(BF16)

    weights = (ng_ref[...], wout_s, wup_s, wdn_s, fg_ref[...], final)

    @pl.when(s >= CAST_STEPS)
    def _():
        o_ref[...] = _mlp_tile(mix_ref[...], x_ref[...], g1_ref[...], sh_ref[...], sc_ref[...], g2_ref[...], *weights)

    @pl.when(s == pl.num_programs(0) - 1)
    def _():
        so_ref[...] = _mlp_tile(smix_ref[...], sx_ref[...], sg1_ref[...], ssh_ref[...], ssc_ref[...], sg2_ref[...],
                                *weights)


def _out_mlp(mix2, x2, mod, mix_s, xs, mod_all, n_sample, norm_g, w_out, w_up, w_down, final_g, layer,
             tiles_per_batch, final):
    n = x2.shape[0]
    tile = TOK_TILE
    const = lambda i: (0, 0)
    tok = lambda i: (jnp.maximum(i - CAST_STEPS, 0), 0)
    chunk_of_layer = lambda i: (layer, jnp.minimum(i, CAST_STEPS - 1), 0)
    mod_p = lambda chunk: pl.BlockSpec((None, 1, D_MODEL),
                                       lambda i: (jnp.maximum(i - CAST_STEPS, 0) // tiles_per_batch, 0, chunk))
    whole = lambda a: pl.BlockSpec(a.shape, const, pipeline_mode=pl.Buffered(1))
    return pl.pallas_call(
        functools.partial(_mlp_kernel, final=final),
        out_shape=(jax.ShapeDtypeStruct((n, D_MODEL), F32), jax.ShapeDtypeStruct(xs.shape, F32)),
        grid=(CAST_STEPS + n // tile,),
        in_specs=[pl.BlockSpec((tile, D_MODEL), tok),
                  pl.BlockSpec((tile, D_MODEL), tok),
                  mod_p(2), mod_p(3), mod_p(4), mod_p(5),
                  whole(mix_s), whole(xs)]
                 + _sample_mod_specs(n_sample, (2, 3, 4, 5), lambda i, chunk: (layer, 0, chunk)) + [
                  pl.BlockSpec((1, D_MODEL), const),
                  pl.BlockSpec((None, D_MODEL // CAST_STEPS, D_MODEL), chunk_of_layer),
                  pl.BlockSpec((None, D_MODEL // CAST_STEPS, D_FF), chunk_of_layer),
                  pl.BlockSpec((None, D_FF // CAST_STEPS, D_MODEL), chunk_of_layer),
                  pl.BlockSpec((1, D_MODEL), const)],
        out_specs=(pl.BlockSpec((tile, D_MODEL), tok), pl.BlockSpec(xs.shape, const, pipeline_mode=pl.Buffered(1))),
        scratch_shapes=[pltpu.VMEM((D_MODEL, D_MODEL), BF16),
                        pltpu.VMEM((D_MODEL, D_FF), BF16),
                        pltpu.VMEM((D_FF, D_MODEL), BF16)],
        compiler_params=pltpu.CompilerParams(dimension_semantics=("arbitrary",), vmem_limit_bytes=VMEM_LIMIT),
        name="out_projection_mlp",
    )(mix2, x2, mod, mod, mod, mod, mix_s, xs, mod_all, mod_all, mod_all, mod_all, norm_g.reshape(1, D_MODEL), w_out,
      w_up, w_down, final_g.reshape(1, D_MODEL))


def _conv_ln_swish(acc, lng, lnb):
    mu = jnp.mean(acc, axis=-1, keepdims=True)
    xc = acc - mu
    y = xc * lax.rsqrt(jnp.mean(xc * xc, axis=-1, keepdims=True) + EPS) * lng + lnb
    return _silu(y)


def _hgrn_gates(fh, lb):
    f = lb + (1.0 - lb) * jax.nn.sigmoid(fh)
    return jnp.log(f), 1.0 - f


def _hgrn_out(o, hng, gate):
    return _rms_rows(o) * hng * _silu(gate)


def _ticker(pieces):
    it = iter(pieces)

    def tick():
        piece = next(it, None)
        if piece is not None:
            piece()

    def flush():
        for piece in it:
            piece()

    tick.flush = flush
    return tick


def _prompt_mix_kernel(sinks_ref, x_ref, sh_ref, sc_ref, sx_ref, ssh_ref, ssc_ref, ng_ref, win_ref, convw_ref,
                       convb_ref, lng_ref, lnb_ref, hlb_ref, hng_ref, bias_ref,
                       mix_ref, sproj_ref, convo_ref, so_ref, ko_ref, vo_ref,
                       proj_ref, wbf_ref, abuf, kbuf, vbuf, st_ref, *, layer, tile):
    t = pl.program_id(1)
    last = pl.num_programs(1) - 1

    @pl.when((pl.program_id(0) == 0) & (t == 0))
    def _():
        wbf_ref[...] = win_ref[...].astype(BF16)

    @pl.when(t == 0)
    def _():
        abuf[0:CONV_PAD, :] = jnp.zeros((CONV_PAD, D_CONV), F32)
        abuf[CONV_PAD + tile:CONV_PAD + tile + SUBLANES, :] = jnp.zeros((SUBLANES, D_CONV), F32)
        kbuf[0:ATTN_BLOCK, :] = jnp.zeros((ATTN_BLOCK, D_KV), BF16)
        vbuf[0:ATTN_BLOCK, :] = jnp.zeros((ATTN_BLOCK, D_KV), BF16)
        st_ref[...] = jnp.zeros(st_ref.shape, F32)

    h_in = _modulated_norm(x_ref[...], ng_ref[...], sc_ref[...], sh_ref[...])
    for lo, hi in ((OFF_AVAL, OFF_Q), (OFF_F, OFF_I), (OFF_Q, OFF_F), (OFF_I, OFF_G), (OFF_G, OFF_QA),
                   (OFF_QA, IN_WIDTH)):
        proj_ref[:, lo:hi] = jnp.dot(h_in, wbf_ref[:, lo:hi], preferred_element_type=F32)

    kbuf[ATTN_BLOCK:ATTN_BLOCK + tile, :] = proj_ref[:, OFF_KA:OFF_KA + D_KV].astype(BF16)
    vbuf[ATTN_BLOCK:ATTN_BLOCK + tile, :] = proj_ref[:, OFF_VA:OFF_VA + D_KV].astype(BF16)
    scale = HEAD_DIM ** -0.5
    attn = {}

    def attn_scores(blk):
        def run():
            r0 = blk * ATTN_BLOCK
            for h in range(H_ATTN):
                kv = h // GROUP
                q = (proj_ref[r0:r0 + ATTN_BLOCK, OFF_QA + h * HEAD_DIM:OFF_QA + (h + 1) * HEAD_DIM]
                     * scale).astype(BF16)
                kall = kbuf[r0:r0 + 2 * ATTN_BLOCK, kv * HEAD_DIM:(kv + 1) * HEAD_DIM]
                attn[blk, h] = lax.dot_general(q, kall, NT_DIMS, preferred_element_type=F32)
        return run

    def attn_softmax(blk):
        def run():
            for h in range(H_ATTN):
                s = attn[blk, h] + bias_ref[h]
                if blk == 0:
                    col = lax.broadcasted_iota(jnp.int32, s.shape, 1)
                    s = jnp.where(col + (t * tile - ATTN_BLOCK) >= 0, s, -jnp.inf)
                p, den = _sink_softmax(s, sinks_ref[h])
                attn[blk, h] = (p.astype(BF16), den)
        return run

    def attn_values(blk):
        def run():
            r0 = blk * ATTN_BLOCK
            heads = []
            for h in range(H_ATTN):
                kv = h // GROUP
                p, den = attn[blk, h]
                vall = vbuf[r0:r0 + 2 * ATTN_BLOCK, kv * HEAD_DIM:(kv + 1) * HEAD_DIM]
                heads.append(jnp.dot(p, vall, preferred_element_type=F32) / den)
            mix_ref[r0:r0 + ATTN_BLOCK, D_CONV + D_HGRN:D_MODEL] = jnp.concatenate(heads, axis=1).astype(BF16)
        return run

    tick = _ticker([stage(blk) for blk in range(tile // ATTN_BLOCK)
                    for stage in (attn_scores, attn_softmax, attn_values)])

    abuf[CONV_PAD:CONV_PAD + tile, :] = (proj_ref[:, OFF_AVAL:OFF_AVAL + D_CONV]
                                         * jax.nn.sigmoid(proj_ref[:, OFF_AGATE:OFF_AGATE + D_CONV]))
    first_row = CONV_PAD - (CONV_WIDTH - 1)
    acc = jnp.broadcast_to(convb_ref[...], (tile, D_CONV))
    for r in range(SUBLANES):
        z = None
        for off in range(r, first_row + CONV_WIDTH, SUBLANES):
            j = off - first_row
            if j < 0:
                continue
            term = convw_ref[j:j + 1, :] * abuf[off - r:off - r + tile + SUBLANES, :]
            z = term if z is None else z + term
        acc = acc + (z[0:tile] if r == 0 else pltpu.roll(z, tile + SUBLANES - r, 0)[0:tile])
    mix_ref[:, 0:D_CONV] = _conv_ln_swish(acc, lng_ref[...], lnb_ref[...]).astype(BF16)
    tick()

    lb = _layer_lb(hlb_ref[...], layer)
    hng = hng_ref[...]
    span = HGRN_SPAN * HGRN_CHUNK
    ri = lax.broadcasted_iota(jnp.int32, (span, span), 0)
    ci = lax.broadcasted_iota(jnp.int32, (span, span), 1)
    tri = jnp.where((ri >= ci) & (ri // HGRN_CHUNK == ci // HGRN_CHUNK), 1.0, 0.0).astype(BF16)
    for i in range(tile // span):
        _hgrn_span(proj_ref, mix_ref, st_ref, i * span, lb, hng, tri, tick)
    tick.flush()

    @pl.when(t == last)
    def _():
        convo_ref[...] = abuf[CONV_PAD + tile - (CONV_WIDTH - 1):CONV_PAD + tile, :]
        for h in range(H_HGRN):
            so_ref[h] = st_ref[h].T
        ko_ref[...] = proj_ref[tile - WINDOW:tile, OFF_KA:OFF_KA + D_KV]
        vo_ref[...] = proj_ref[tile - WINDOW:tile, OFF_VA:OFF_VA + D_KV]

    abuf[0:CONV_PAD, :] = abuf[tile:tile + CONV_PAD, :]
    kbuf[0:ATTN_BLOCK, :] = kbuf[tile:tile + ATTN_BLOCK, :]
    vbuf[0:ATTN_BLOCK, :] = vbuf[tile:tile + ATTN_BLOCK, :]

    @pl.when((pl.program_id(0) == pl.num_programs(0) - 1) & (t == last))
    def _():
        h_s = _modulated_norm(sx_ref[...], ng_ref[...], ssc_ref[...], ssh_ref[...])
        sproj_ref[...] = jnp.dot(h_s, wbf_ref[...], preferred_element_type=F32)


def _carry_specs(carried):
    return [pl.BlockSpec(memory_space=pl.ANY)] * len(carried)


def _without_carry(kernel_fn, n_in, n_carried, *refs):
    return kernel_fn(*refs[:n_in], *refs[n_in + n_carried:])


def _carried(kernel_fn, n_in, n_carried):
    return functools.partial(_without_carry, kernel_fn, n_in, n_carried)


def _prompt_mixers(x, mod, xs, mod_all, n_sample, norm_g, w_in, sinks, conv_w, conv_b, ln_g, ln_b, hgrn_lb, hng,
                   bias_p, layer, carried):
    B, T = x.shape[:2]
    tile = MIX_TILE
    depth = hgrn_lb.shape[0]
    const2 = lambda b, t: (0, 0)
    inputs = (sinks, x, mod, mod, xs, mod_all, mod_all, norm_g.reshape(1, D_MODEL), w_in, conv_w,
              conv_b.reshape(1, D_CONV), ln_g.reshape(1, D_CONV), ln_b.reshape(1, D_CONV), hgrn_lb,
              hng.reshape(1, DV_HGRN), bias_p)
    return pl.pallas_call(
        _carried(functools.partial(_prompt_mix_kernel, layer=layer, tile=tile), len(inputs), len(carried)),
        out_shape=(jax.ShapeDtypeStruct((B, T, D_MODEL), BF16),
                   jax.ShapeDtypeStruct((xs.shape[0], IN_WIDTH), F32),
                   jax.ShapeDtypeStruct((depth, B, CONV_WIDTH - 1, D_CONV), F32),
                   jax.ShapeDtypeStruct((depth, B, H_HGRN, DK_HGRN, DV_HGRN), F32),
                   jax.ShapeDtypeStruct((depth, B, WINDOW, D_KV), F32),
                   jax.ShapeDtypeStruct((depth, B, WINDOW, D_KV), F32)),
        grid=(B, T // tile),
        in_specs=[pl.BlockSpec(memory_space=pltpu.SMEM),
                  pl.BlockSpec((None, tile, D_MODEL), lambda b, t: (b, t, 0)),
                  pl.BlockSpec((None, 1, D_MODEL), lambda b, t: (b, 0, 0)),
                  pl.BlockSpec((None, 1, D_MODEL), lambda b, t: (b, 0, 1)),
                  pl.BlockSpec(xs.shape, const2, pipeline_mode=pl.Buffered(1))]
                 + _sample_mod_specs(n_sample, (0, 1), lambda b, t, chunk: (layer, 0, chunk)) + [
                  pl.BlockSpec((1, D_MODEL), const2),
                  pl.BlockSpec((None, D_MODEL, IN_WIDTH), lambda b, t: (layer, 0, 0), pipeline_mode=pl.Buffered(1)),
                  pl.BlockSpec((CONV_WIDTH, D_CONV), const2),
                  pl.BlockSpec((1, D_CONV), const2),
                  pl.BlockSpec((1, D_CONV), const2),
                  pl.BlockSpec((1, D_CONV), const2),
                  pl.BlockSpec((depth, D_HGRN), const2),
                  pl.BlockSpec((1, DV_HGRN), const2),
                  pl.BlockSpec((H_ATTN, ATTN_BLOCK, 2 * ATTN_BLOCK), lambda b, t: (0, 0, 0))] + _carry_specs(carried),
        out_specs=(pl.BlockSpec((None, tile, D_MODEL), lambda b, t: (b, t, 0)),
                   pl.BlockSpec((xs.shape[0], IN_WIDTH), const2),
                   pl.BlockSpec((None, None, CONV_WIDTH - 1, D_CONV), lambda b, t: (layer, b, 0, 0)),
                   pl.BlockSpec((None, None, H_HGRN, DK_HGRN, DV_HGRN), lambda b, t: (layer, b, 0, 0, 0)),
                   pl.BlockSpec((None, None, WINDOW, D_KV), lambda b, t: (layer, b, 0, 0)),
                   pl.BlockSpec((None, None, WINDOW, D_KV), lambda b, t: (layer, b, 0, 0))),
        input_output_aliases={len(inputs) + i: 2 + i for i in range(len(carried))},
        scratch_shapes=[pltpu.VMEM((tile, IN_WIDTH), F32),
                        pltpu.VMEM((D_MODEL, IN_WIDTH), BF16),
                        pltpu.VMEM((CONV_PAD + tile + SUBLANES, D_CONV), F32),
                        pltpu.VMEM((ATTN_BLOCK + tile, D_KV), BF16),
                        pltpu.VMEM((ATTN_BLOCK + tile, D_KV), BF16),
                        pltpu.VMEM((H_HGRN, DV_HGRN, DK_HGRN), F32)],
        compiler_params=pltpu.CompilerParams(dimension_semantics=("arbitrary", "arbitrary"),
                                             vmem_limit_bytes=VMEM_LIMIT),
        name="prompt_mixers",
    )(*inputs, *carried)


def _sample_mix_kernel(sinks_ref, proj_ref, cconv_ref, state_ref, ck_ref, cv_ref, convw_ref, convb_ref, lng_ref,
                       lnb_ref, hlb_ref, hng_ref, bias_ref,
                       mix_ref, convo_ref, so_ref, ko_ref, vo_ref, kpad_ref, vpad_ref, *,
                       layer, block, seq, w_buf):
    hist = CONV_WIDTH - 1

    @pl.when(pl.program_id(0) == 0)
    def _():
        for ref in (kpad_ref, vpad_ref):
            ref[:, 0:w_buf - seq, :] = jnp.zeros((block, w_buf - seq, D_KV), F32)

    lb = _layer_lb(hlb_ref[...], layer)
    hng = hng_ref[...]
    scale = HEAD_DIM ** -0.5
    elems = range(block)
    row8 = lax.broadcasted_iota(jnp.int32, (SUBLANES, DV_HGRN), 0)
    ones_rows = jnp.where((row8 >= seq) & (row8 < seq + 3), 1.0, 0.0)
    zrow = jnp.zeros((1, DK_HGRN), BF16)
    prow = lax.broadcasted_iota(jnp.int32, (seq, seq), 0)
    pcol = lax.broadcasted_iota(jnp.int32, (seq, seq), 1)
    grow = lax.broadcasted_iota(jnp.int32, (GROUP * seq, 1), 0)

    proj = [proj_ref[e * seq:(e + 1) * seq, :] for e in elems]

    glu = [p[:, OFF_AVAL:OFF_AVAL + D_CONV] * jax.nn.sigmoid(p[:, OFF_AGATE:OFF_AGATE + D_CONV]) for p in proj]
    full = [cconv_ref[i] for i in range(hist)]
    full += [jnp.concatenate([glu[e][t:t + 1] for e in elems], axis=0) for t in range(seq)]
    for i in range(hist):
        convo_ref[i] = full[i + seq]
    conv_out = []
    for t in range(seq):
        acc = jnp.broadcast_to(convb_ref[...], (block, D_CONV))
        for j in range(CONV_WIDTH):
            acc = acc + convw_ref[j:j + 1, :] * full[t + j]
        conv_out.append(_conv_ln_swish(acc, lng_ref[...], lnb_ref[...]))
    out_a = [jnp.concatenate([conv_out[t][e:e + 1] for t in range(seq)], axis=0) for e in elems]

    units = [(e, h) for e in elems for h in range(H_HGRN)]
    ops = {}
    for e in elems:
        p = proj[e]
        g, k = _hgrn_gates(p[:, OFF_F:OFF_F + D_HGRN], lb)
        b = _cumsum_rows_small(g)
        for h in range(H_HGRN):
            cs = slice(h * DK_HGRN, (h + 1) * DK_HGRN)
            q = p[:, OFF_Q + h * DK_HGRN:OFF_Q + (h + 1) * DK_HGRN]
            v = p[:, OFF_I + h * DV_HGRN:OFF_I + (h + 1) * DV_HGRN]
            bb = b[:, cs]
            bl = bb[seq - 1:seq, :]
            kst = (k[:, cs] * jnp.exp(bl - bb)).astype(BF16)
            x = jnp.concatenate([kst.astype(F32), *(part.astype(F32) for part in _split3_bf16(jnp.exp(bl))),
                                 zrow.astype(F32)], axis=0).astype(BF16)
            vpad = jnp.concatenate([v, jnp.zeros((SUBLANES - seq, DV_HGRN), F32)], axis=0)
            ops[e, h] = dict(qp=(q * jnp.exp(bb - bl)).astype(BF16), kst=kst, v=v.astype(BF16),
                             qt=(q * jnp.exp(bb)).astype(BF16), x=x,
                             r=jnp.concatenate([vpad, ones_rows], axis=1).astype(BF16))
    for u in units:
        o = ops[u]
        st = state_ref[u[0], u[1]]
        o["p"] = lax.dot_general(o["qp"], o["kst"], NT_DIMS, preferred_element_type=F32)
        o["inter"] = jnp.dot(o["qt"], st.astype(BF16), preferred_element_type=F32)
        me = lax.dot_general(o["x"], o["r"], TN_DIMS, preferred_element_type=F32)
        so_ref[u[0], u[1]] = me[:, DV_HGRN:] * st + me[:, :DV_HGRN]
    out_b = {}
    for u in units:
        o = ops[u]
        pm = jnp.where(prow >= pcol, o["p"], 0.0).astype(BF16)
        out = o["inter"] + jnp.dot(pm, o["v"], preferred_element_type=F32)
        gate = proj[u[0]][:, OFF_G + u[1] * DV_HGRN:OFF_G + (u[1] + 1) * DV_HGRN]
        out_b[u] = _hgrn_out(out, hng, gate)

    scores = {}
    for e in elems:
        p = proj[e]
        for kv in range(KV_HEADS):
            hs = slice(kv * HEAD_DIM, (kv + 1) * HEAD_DIM)
            q2 = jnp.concatenate([p[:, OFF_QA + h * HEAD_DIM:OFF_QA + (h + 1) * HEAD_DIM]
                                  for h in range(kv * GROUP, (kv + 1) * GROUP)], axis=0)
            q2 = (q2 * scale).astype(BF16)
            bias = bias_ref[kv]
            s_c = jnp.dot(q2, ck_ref[e, hs, :].astype(BF16), preferred_element_type=F32) + bias[:, 0:w_buf]
            s_n = (lax.dot_general(q2, p[:, OFF_KA + kv * HEAD_DIM:OFF_KA + (kv + 1) * HEAD_DIM].astype(BF16), NT_DIMS,
                                   preferred_element_type=F32) + bias[:, w_buf:w_buf + seq])
            scores[e, kv] = (s_c, s_n)
    out_c = {}
    for e in elems:
        p = proj[e]
        for kv in range(KV_HEADS):
            hs = slice(kv * HEAD_DIM, (kv + 1) * HEAD_DIM)
            sink = jnp.zeros((GROUP * seq, 1), F32)
            for gi in range(GROUP):
                sink = jnp.where(grow >= gi * seq, sinks_ref[kv * GROUP + gi], sink)
            s_c, s_n = scores[e, kv]
            m = jnp.maximum(jnp.maximum(jnp.max(s_c, axis=-1, keepdims=True), jnp.max(s_n, axis=-1, keepdims=True)),
                            sink)
            p_c = jnp.exp(s_c - m)
            p_n = jnp.exp(s_n - m)
            den = jnp.sum(p_c, axis=-1, keepdims=True) + jnp.sum(p_n, axis=-1, keepdims=True) + jnp.exp(sink - m)
            vnew = p[:, OFF_VA + kv * HEAD_DIM:OFF_VA + (kv + 1) * HEAD_DIM].astype(BF16)
            o2 = (lax.dot_general(p_c.astype(BF16), cv_ref[e, hs, :].astype(BF16), NT_DIMS,
                                  preferred_element_type=F32)
                  + jnp.dot(p_n.astype(BF16), vnew, preferred_element_type=F32)) / den
            for gi in range(GROUP):
                out_c[e, kv * GROUP + gi] = o2[gi * seq:(gi + 1) * seq]
    lane = lax.broadcasted_iota(jnp.int32, (D_KV, w_buf), 1)
    for e in elems:
        p = proj[e]
        for pad, cache, new, out in ((kpad_ref, ck_ref, p[:, OFF_KA:OFF_KA + D_KV], ko_ref),
                                     (vpad_ref, cv_ref, p[:, OFF_VA:OFF_VA + D_KV], vo_ref)):
            pad[e, w_buf - seq:w_buf, :] = new
            out[e] = jnp.where(lane >= w_buf - seq, pad[e].T, pltpu.roll(cache[e], w_buf - seq, 1))

    for e in elems:
        parts = [out_a[e]] + [out_b[e, h] for h in range(H_HGRN)] + [out_c[e, h] for h in range(H_ATTN)]
        mix_ref[e * seq:(e + 1) * seq, :] = jnp.concatenate(parts, axis=1)


def _sample_mixers(proj2, sinks, cache_conv, state, cache_k, cache_v, conv_w, conv_b, ln_g, ln_b, hgrn_lb, hng,
                   bias_s, layer, carried):
    B = state.shape[1]
    seq = proj2.shape[0] // B
    w_buf = cache_k.shape[3]
    block = SAMPLE_BLOCK
    depth = hgrn_lb.shape[0]
    hist = CONV_WIDTH - 1
    const2 = lambda i: (0, 0)
    cache_specs = [pl.BlockSpec((None, hist, block, D_CONV), lambda i: (layer, 0, i, 0)),
                   pl.BlockSpec((None, block, H_HGRN, DK_HGRN, DV_HGRN), lambda i: (layer, i, 0, 0, 0)),
                   pl.BlockSpec((None, block, D_KV, w_buf), lambda i: (layer, i, 0, 0)),
                   pl.BlockSpec((None, block, D_KV, w_buf), lambda i: (layer, i, 0, 0))]
    inputs = (sinks, proj2, cache_conv, state, cache_k, cache_v, conv_w, conv_b.reshape(1, D_CONV),
              ln_g.reshape(1, D_CONV), ln_b.reshape(1, D_CONV), hgrn_lb, hng.reshape(1, DV_HGRN), bias_s)
    return pl.pallas_call(
        _carried(functools.partial(_sample_mix_kernel, layer=layer, block=block, seq=seq, w_buf=w_buf),
                 len(inputs), len(carried)),
        out_shape=(jax.ShapeDtypeStruct((B * seq, D_MODEL), F32),
                   jax.ShapeDtypeStruct((depth, hist, B, D_CONV), F32),
                   jax.ShapeDtypeStruct((depth, B, H_HGRN, DK_HGRN, DV_HGRN), F32),
                   jax.ShapeDtypeStruct((depth, B, D_KV, w_buf), F32),
                   jax.ShapeDtypeStruct((depth, B, D_KV, w_buf), F32)),
        grid=(B // block,),
        in_specs=[pl.BlockSpec(memory_space=pltpu.SMEM),
                  pl.BlockSpec((block * seq, IN_WIDTH), lambda i: (i, 0))] + cache_specs + [
                  pl.BlockSpec((CONV_WIDTH, D_CONV), const2),
                  pl.BlockSpec((1, D_CONV), const2),
                  pl.BlockSpec((1, D_CONV), const2),
                  pl.BlockSpec((1, D_CONV), const2),
                  pl.BlockSpec((depth, D_HGRN), const2),
                  pl.BlockSpec((1, DV_HGRN), const2),
                  pl.BlockSpec((KV_HEADS, GROUP * seq, 2 * ATTN_BLOCK), lambda i: (0, 0, 0))] + _carry_specs(carried),
        out_specs=tuple([pl.BlockSpec((block * seq, D_MODEL), lambda i: (i, 0))] + cache_specs),
        input_output_aliases={len(inputs) + i: 1 + i for i in range(len(carried))},
        scratch_shapes=[pltpu.VMEM((block, w_buf, D_KV), F32),
                        pltpu.VMEM((block, w_buf, D_KV), F32)],
        compiler_params=pltpu.CompilerParams(dimension_semantics=("arbitrary",), vmem_limit_bytes=VMEM_LIMIT),
        name="sample_mixers",
    )(*inputs, *carried)


def kernel(x_prompt, x_sample, cache_conv, state_hgrn, cache_swa_k, cache_swa_v, c_prompt, c_sample, rel_bias, w_ada, b_ada, norm_mix_g, w_in, conv_w, conv_b, conv_ln_g, conv_ln_b, hgrn_lb, hgrn_norm_g, attn_sinks, w_out, norm_mlp_g, w_up, w_down, final_g):
    Bp, Tp = x_prompt.shape[:2]
    Bs, Ts = x_sample.shape[:2]
    depth = w_in.shape[0]
    w_buf = cache_swa_k.shape[2]
    assert Tp % MIX_TILE == 0 and (Bp * Tp) % TOK_TILE == 0 and Tp % TOK_TILE == 0 and Bs % SAMPLE_BLOCK == 0
    assert w_buf == WINDOW and GROUP * Ts == SUBLANES

    bias_p, bias_s = _bias_tables(rel_bias, Ts, w_buf)
    mod = _modulation(jnp.concatenate([c_sample, c_prompt], axis=0), w_ada, b_ada)
    hlb = hgrn_lb.astype(F32)
    cc = jnp.swapaxes(cache_conv, 1, 2)
    ck = jnp.swapaxes(cache_swa_k.reshape(depth, Bs, w_buf, D_KV), 2, 3)
    cv = jnp.swapaxes(cache_swa_v.reshape(depth, Bs, w_buf, D_KV), 2, 3)

    xp = x_prompt.reshape(Bp * Tp, D_MODEL)
    xs = x_sample.reshape(Bs * Ts, D_MODEL)
    caches_p = ()
    caches_s = ()
    for l in range(depth):
        final = l == depth - 1
        mod_p = mod[l, Bs:].reshape(Bp, 1, N_MOD * D_MODEL)
        mix_p, proj_s, *caches_p = _prompt_mixers(xp.reshape(Bp, Tp, D_MODEL), mod_p, xs, mod, Bs, norm_mix_g[l], w_in,
                                                  attn_sinks[l], conv_w[l], conv_b[l], conv_ln_g[l], conv_ln_b[l], hlb,
                                                  hgrn_norm_g[l], bias_p, l, caches_p)
        mix_s, *caches_s = _sample_mixers(proj_s, attn_sinks[l], cc, state_hgrn, ck, cv, conv_w[l],
                                          conv_b[l], conv_ln_g[l], conv_ln_b[l], hlb, hgrn_norm_g[l], bias_s, l,
                                          caches_s)
        xp, xs = _out_mlp(mix_p.reshape(Bp * Tp, D_MODEL), xp, mod_p, mix_s, xs, mod, Bs, norm_mlp_g[l], w_out,
                          w_up, w_down, final_g, l, Tp // TOK_TILE, final)
    cp, sp, kp, vp = caches_p
    cs, ss, ksn, vsn = caches_s
    cs, ksn, vsn = jnp.swapaxes(cs, 1, 2), jnp.swapaxes(ksn, 2, 3), jnp.swapaxes(vsn, 2, 3)
    return (xp.reshape(Bp, Tp, D_MODEL), xs.reshape(Bs, Ts, D_MODEL), cp, cs, sp, ss,
            kp.reshape(depth, Bp, WINDOW, KV_HEADS, HEAD_DIM), ksn.reshape(depth, Bs, w_buf, KV_HEADS, HEAD_DIM),
            vp.reshape(depth, Bp, WINDOW, KV_HEADS, HEAD_DIM), vsn.reshape(depth, Bs, w_buf, KV_HEADS, HEAD_DIM))
```

```python
import functools
import math

import jax
import jax.numpy as jnp
from jax import lax
from jax.experimental import pallas as pl
from jax.experimental.pallas import tpu as pltpu

F32 = jnp.float32
BF16 = jnp.bfloat16

D_MODEL = 1024
D_CONV = 256
CONV_WIDTH = 31
H_HGRN = 4
DK_HGRN = 128
DV_HGRN = 128
D_HGRN = 512
HEAD_DIM = 64
H_ATTN = 4
KV_HEADS = 2
GROUP = H_ATTN // KV_HEADS
D_ATTN = H_ATTN * HEAD_DIM
D_KV = KV_HEADS * HEAD_DIM
WINDOW = 128
ATTN_BLOCK = 128
NUM_BUCKETS = 32
MAX_DISTANCE = 128
D_FF = 4 * D_MODEL
N_MOD = 6
EPS = 1e-6

OFF_AVAL = 0
OFF_AGATE = OFF_AVAL + D_CONV
OFF_Q = OFF_AGATE + D_CONV
OFF_F = OFF_Q + H_HGRN * DK_HGRN
OFF_I = OFF_F + H_HGRN * DK_HGRN
OFF_G = OFF_I + D_HGRN
OFF_QA = OFF_G + D_HGRN
OFF_KA = OFF_QA + D_ATTN
OFF_VA = OFF_KA + D_KV
IN_WIDTH = OFF_VA + D_KV

HGRN_CHUNK = 64
HGRN_KEYBLOCK = 32
HGRN_SPAN = 4
SUBLANES = 8
CONV_PAD = 32
MIX_TILE = 512
TOK_TILE = 512
SAMPLE_BLOCK = 16
FF_CHUNK = 1024
CAST_STEPS = 8
MOD_COLS = 2048
VMEM_LIMIT = 56 * 1024 * 1024

NT_DIMS = (((1,), (1,)), ((), ()))
TN_DIMS = (((0,), (0,)), ((), ()))


def _silu(x):
    return x * jax.nn.sigmoid(x)


def _rms_rows(x):
    return x * lax.rsqrt(jnp.mean(x * x, axis=-1, keepdims=True) + EPS)


def _layer_lb(hlb, layer):
    m = jnp.max(hlb, axis=0, keepdims=True)
    e = jnp.exp(hlb - m)
    p = e / jnp.sum(e, axis=0, keepdims=True)
    lb = jnp.zeros_like(m)
    for i in range(1, layer + 1):
        lb = lb + p[i:i + 1, :]
    return lb


def _split3_bf16(x):
    hi = x.astype(BF16)
    r = x - hi.astype(F32)
    mid = r.astype(BF16)
    return hi, mid, (r - mid.astype(F32)).astype(BF16)


def _select_rows_mxu(sel, x):
    return sum(jnp.dot(sel, part, preferred_element_type=F32) for part in _split3_bf16(x))


def _cumsum_rows_small(g):
    row = lax.broadcasted_iota(jnp.int32, g.shape, 0)
    b = jnp.zeros_like(g)
    for u in range(g.shape[0]):
        b = b + jnp.where(row >= u, g[u:u + 1, :], 0.0)
    return b


def _hgrn_span(proj_ref, mix_ref, st_ref, row0, lb, hng, tri, tick):
    L, KB = HGRN_CHUNK, HGRN_KEYBLOCK
    span = HGRN_SPAN * L
    g, k = _hgrn_gates(proj_ref[pl.ds(row0, span), OFF_F:OFF_F + D_HGRN], lb)
    b = _select_rows_mxu(tri, g)
    units = [(c, h) for c in range(HGRN_SPAN) for h in range(H_HGRN)]

    ops = {}
    for c, h in units:
        rows = pl.ds(row0 + c * L, L)
        cs = slice(h * DK_HGRN, (h + 1) * DK_HGRN)
        q = proj_ref[rows, OFF_Q + h * DK_HGRN:OFF_Q + (h + 1) * DK_HGRN]
        v = proj_ref[rows, OFF_I + h * DV_HGRN:OFF_I + (h + 1) * DV_HGRN].astype(BF16)
        kk = k[c * L:(c + 1) * L, cs]
        bb = b[c * L:(c + 1) * L, cs]
        qp, kp = [], []
        for lo in range(0, L, KB):
            r = bb[lo + KB // 2 - 1:lo + KB // 2, :]
            kp.append((kk[lo:lo + KB] * jnp.exp(r - bb[lo:lo + KB])).astype(BF16))
            qp.append((q[lo:] * jnp.exp(bb[lo:] - r)).astype(BF16))
        bl = bb[L - 1:L, :]
        ops[c, h] = dict(qp=qp, kp=kp, v=v, qt=(q * jnp.exp(bb)).astype(BF16),
                         kst=(kk * jnp.exp(bl - bb)).astype(BF16), e=jnp.exp(bl))
    tick()

    for u in units:
        o = ops[u]
        o["p"] = [lax.dot_general(qp, kp, NT_DIMS, preferred_element_type=F32) for qp, kp in zip(o["qp"], o["kp"])]
        o["m"] = lax.dot_general(o["v"], o["kst"], TN_DIMS, preferred_element_type=F32)
    tick()

    for u in units:
        pm = []
        for p in ops[u]["p"]:
            row = lax.broadcasted_iota(jnp.int32, p.shape, 0)
            col = lax.broadcasted_iota(jnp.int32, p.shape, 1)
            pm.append(jnp.where(row >= col, p, 0.0).astype(BF16))
        ops[u]["p"] = pm
    tick()

    for u in units:
        o = ops[u]
        blocks = [None] * (L // KB)
        for j, p in enumerate(o["p"]):
            cj = jnp.dot(p, o["v"][j * KB:(j + 1) * KB], preferred_element_type=F32)
            for i in range(j, L // KB):
                piece = cj[(i - j) * KB:(i - j + 1) * KB]
                blocks[i] = piece if blocks[i] is None else blocks[i] + piece
        o["o"] = jnp.concatenate(blocks, axis=0)
    tick()

    for h in range(H_HGRN):
        st = st_ref[h]
        for c in range(HGRN_SPAN):
            o = ops[c, h]
            out = o["o"] + lax.dot_general(o["qt"], st.astype(BF16), NT_DIMS, preferred_element_type=F32)
            st = o["e"] * st + o["m"]
            rows = pl.ds(row0 + c * L, L)
            gate = proj_ref[rows, OFF_G + h * DV_HGRN:OFF_G + (h + 1) * DV_HGRN]
            mix_ref[rows, D_CONV + h * DV_HGRN:D_CONV + (h + 1) * DV_HGRN] = _hgrn_out(out, hng, gate).astype(BF16)
        st_ref[h] = st


def _sink_softmax(s, sink):
    m = jnp.maximum(jnp.max(s, axis=-1, keepdims=True), sink)
    p = jnp.exp(s - m)
    return p, jnp.sum(p, axis=-1, keepdims=True) + jnp.exp(sink - m)


def _bias_kernel(tab_ref, bp_ref, bs_ref, op_ref, os_ref, *, seq):
    bk = bp_ref[...]
    for h in range(H_ATTN):
        acc = jnp.full(bk.shape, -jnp.inf, F32)
        for bkt in range(NUM_BUCKETS):
            acc = jnp.where(bk == bkt, tab_ref[bkt, h], acc)
        op_ref[h] = acc
    bk = bs_ref[...]
    row = lax.broadcasted_iota(jnp.int32, bk.shape, 0)
    for kv in range(KV_HEADS):
        acc = jnp.full(bk.shape, -jnp.inf, F32)
        for bkt in range(NUM_BUCKETS):
            val = jnp.full(bk.shape, tab_ref[bkt, kv * GROUP], F32)
            for gi in range(1, GROUP):
                val = jnp.where(row >= gi * seq, tab_ref[bkt, kv * GROUP + gi], val)
            acc = jnp.where(bk == bkt, val, acc)
        os_ref[kv] = acc


def _t5_bucket(rel):
    n = jnp.maximum(rel, 0)
    max_exact = NUM_BUCKETS // 2
    nf = jnp.maximum(n, max_exact).astype(F32)
    large = max_exact + (jnp.log(nf / max_exact) / math.log(MAX_DISTANCE / max_exact)
                         * (NUM_BUCKETS - max_exact)).astype(jnp.int32)
    large = jnp.minimum(large, NUM_BUCKETS - 1)
    return jnp.where(n < max_exact, n, large)


def _bias_tables(rel_bias, dec_seq, w_buf):
    qi = jnp.arange(ATTN_BLOCK, dtype=jnp.int32)[:, None]
    kc = jnp.arange(2 * ATTN_BLOCK, dtype=jnp.int32)[None, :]
    rel_p = qi + ATTN_BLOCK - kc
    bucket_p = jnp.where((rel_p >= 0) & (rel_p <= WINDOW), _t5_bucket(rel_p), -1)
    ts = (jnp.arange(GROUP * dec_seq, dtype=jnp.int32) % dec_seq)[:, None]
    js = jnp.arange(2 * ATTN_BLOCK, dtype=jnp.int32)[None, :]
    rel_s = w_buf + ts - js
    ok_s = (rel_s >= 0) & (rel_s <= WINDOW) & (js < w_buf + dec_seq)
    bucket_s = jnp.where(ok_s, _t5_bucket(rel_s), -1)
    return pl.pallas_call(
        functools.partial(_bias_kernel, seq=dec_seq),
        out_shape=(jax.ShapeDtypeStruct((H_ATTN, ATTN_BLOCK, 2 * ATTN_BLOCK), F32),
                   jax.ShapeDtypeStruct((KV_HEADS, GROUP * dec_seq, 2 * ATTN_BLOCK), F32)),
        in_specs=[pl.BlockSpec(memory_space=pltpu.SMEM),
                  pl.BlockSpec(memory_space=pltpu.VMEM),
                  pl.BlockSpec(memory_space=pltpu.VMEM)],
        out_specs=(pl.BlockSpec(memory_space=pltpu.VMEM), pl.BlockSpec(memory_space=pltpu.VMEM)),
        name="rel_bias_tables",
    )(rel_bias.astype(F32), bucket_p, bucket_s)


def _mod_kernel(c_ref, w_ref, b_ref, o_ref):
    s = _silu(c_ref[...]).astype(BF16)
    o_ref[...] = jnp.dot(s, w_ref[...].astype(BF16), preferred_element_type=F32) + b_ref[...]


def _modulation(c_all, w_ada, b_ada):
    depth = w_ada.shape[0]
    n = c_all.shape[0]
    return pl.pallas_call(
        _mod_kernel,
        out_shape=jax.ShapeDtypeStruct((depth, n, N_MOD * D_MODEL), F32),
        grid=(depth, N_MOD * D_MODEL // MOD_COLS),
        in_specs=[pl.BlockSpec((n, D_MODEL), lambda l, j: (0, 0)),
                  pl.BlockSpec((None, D_MODEL, MOD_COLS), lambda l, j: (l, 0, j)),
                  pl.BlockSpec((None, 1, MOD_COLS), lambda l, j: (l, 0, j))],
        out_specs=pl.BlockSpec((None, n, MOD_COLS), lambda l, j: (l, 0, j)),
        compiler_params=pltpu.CompilerParams(dimension_semantics=("arbitrary", "arbitrary"),
                                             vmem_limit_bytes=VMEM_LIMIT),
        name="adaln_modulation",
    )(c_all, w_ada, b_ada.reshape(depth, 1, N_MOD * D_MODEL))


def _mod_rows(m, n_tokens):
    if m.shape[0] == 1:
        return m
    reps = n_tokens // m.shape[0]
    tok = lax.broadcasted_iota(jnp.int32, (n_tokens, m.shape[0]), 0)
    bat = lax.broadcasted_iota(jnp.int32, (n_tokens, m.shape[0]), 1)
    sel = jnp.where((tok >= bat * reps) & (tok < (bat + 1) * reps), 1.0, 0.0).astype(BF16)
    return _select_rows_mxu(sel, m)


def _modulated_norm(x, g, sc, sh):
    n = x.shape[0]
    return (_rms_rows(x) * (g * (1.0 + _mod_rows(sc, n))) + _mod_rows(sh, n)).astype(BF16)


def _sample_mod_specs(n_rows, chunks, index_map):
    return [pl.BlockSpec((None, n_rows, D_MODEL), functools.partial(index_map, chunk=c),
                         pipeline_mode=pl.Buffered(1)) for c in chunks]


def _mlp_tile(mix, x, g1, sh, sc, g2, ng, wout_ref, wup_ref, wdn_ref, fg, final):
    n = x.shape[0]
    x1 = x + _mod_rows(g1, n) * jnp.dot(mix.astype(BF16), wout_ref[...], preferred_element_type=F32)
    h = _modulated_norm(x1, ng, sc, sh)
    acc = None
    for c in range(D_FF // FF_CHUNK):
        u = jnp.dot(h, wup_ref[:, c * FF_CHUNK:(c + 1) * FF_CHUNK], preferred_element_type=F32)
        u = jnp.square(jnp.maximum(u, 0.0)).astype(BF16)
        d = jnp.dot(u, wdn_ref[c * FF_CHUNK:(c + 1) * FF_CHUNK, :], preferred_element_type=F32)
        acc = d if acc is None else acc + d
    x2 = x1 + _mod_rows(g2, n) * acc
    return _rms_rows(x2) * fg if final else x2


def _mlp_kernel(mix_ref, x_ref, g1_ref, sh_ref, sc_ref, g2_ref, smix_ref, sx_ref, sg1_ref, ssh_ref, ssc_ref, sg2_ref,
                ng_ref, wout_ref, wup_ref, wdn_ref, fg_ref, o_ref, so_ref, wout_s, wup_s, wdn_s, *, final):
    s = pl.program_id(0)

    @pl.when(s < CAST_STEPS)
    def _():
        for src, dst in ((wout_ref, wout_s), (wup_ref, wup_s), (wdn_ref, wdn_s)):
            rows = src.shape[0]
            dst[pl.ds(pl.multiple_of(s * rows, rows), rows), :] = src[...].astype(BF16)

    weights = (ng_ref[...], wout_s, wup_s, wdn_s, fg_ref[...], final)

    @pl.when(s >= CAST_STEPS)
    def _():
        o_ref[...] = _mlp_tile(mix_ref[...], x_ref[...], g1_ref[...], sh_ref[...], sc_ref[...], g2_ref[...], *weights)

    @pl.when(s == pl.num_programs(0) - 1)
    def _():
        so_ref[...] = _mlp_tile(smix_ref[...], sx_ref[...], sg1_ref[...], ssh_ref[...], ssc_ref[...], sg2_ref[...],
                                *weights)


def _out_mlp(mix2, x2, mod, mix_s, xs, mod_all, n_sample, norm_g, w_out, w_up, w_down, final_g, layer,
             tiles_per_batch, final):
    n = x2.shape[0]
    tile = TOK_TILE
    const = lambda i: (0, 0)
    tok = lambda i: (jnp.maximum(i - CAST_STEPS, 0), 0)
    chunk_of_layer = lambda i: (layer, jnp.minimum(i, CAST_STEPS - 1), 0)
    mod_p = lambda chunk: pl.BlockSpec((None, 1, D_MODEL),
                                       lambda i: (jnp.maximum(i - CAST_STEPS, 0) // tiles_per_batch, 0, chunk))
    whole = lambda a: pl.BlockSpec(a.shape, const, pipeline_mode=pl.Buffered(1))
    return pl.pallas_call(
        functools.partial(_mlp_kernel, final=final),
        out_shape=(jax.ShapeDtypeStruct((n, D_MODEL), F32), jax.ShapeDtypeStruct(xs.shape, F32)),
        grid=(CAST_STEPS + n // tile,),
        in_specs=[pl.BlockSpec((tile, D_MODEL), tok),
                  pl.BlockSpec((tile, D_MODEL), tok),
                  mod_p(2), mod_p(3), mod_p(4), mod_p(5),
                  whole(mix_s), whole(xs)]
                 + _sample_mod_specs(n_sample, (2, 3, 4, 5), lambda i, chunk: (layer, 0, chunk)) + [
                  pl.BlockSpec((1, D_MODEL), const),
                  pl.BlockSpec((None, D_MODEL // CAST_STEPS, D_MODEL), chunk_of_layer),
                  pl.BlockSpec((None, D_MODEL // CAST_STEPS, D_FF), chunk_of_layer),
                  pl.BlockSpec((None, D_FF // CAST_STEPS, D_MODEL), chunk_of_layer),
                  pl.BlockSpec((1, D_MODEL), const)],
        out_specs=(pl.BlockSpec((tile, D_MODEL), tok), pl.BlockSpec(xs.shape, const, pipeline_mode=pl.Buffered(1))),
        scratch_shapes=[pltpu.VMEM((D_MODEL, D_MODEL), BF16),
                        pltpu.VMEM((D_MODEL, D_FF), BF16),
                        pltpu.VMEM((D_FF, D_MODEL), BF16)],
        compiler_params=pltpu.CompilerParams(dimension_semantics=("arbitrary",), vmem_limit_bytes=VMEM_LIMIT),
        name="out_projection_mlp",
    )(mix2, x2, mod, mod, mod, mod, mix_s, xs, mod_all, mod_all, mod_all, mod_all, norm_g.reshape(1, D_MODEL), w_out,
      w_up, w_down, final_g.reshape(1, D_MODEL))


def _conv_ln_swish(acc, lng, lnb):
    mu = jnp.mean(acc, axis=-1, keepdims=True)
    xc = acc - mu
    y = xc * lax.rsqrt(jnp.mean(xc * xc, axis=-1, keepdims=True) + EPS) * lng + lnb
    return _silu(y)


def _hgrn_gates(fh, lb):
    f = lb + (1.0 - lb) * jax.nn.sigmoid(fh)
    return jnp.log(f), 1.0 - f


def _hgrn_out(o, hng, gate):
    return _rms_rows(o) * hng * _silu(gate)


def _ticker(pieces):
    it = iter(pieces)

    def tick():
        piece = next(it, None)
        if piece is not None:
            piece()

    def flush():
        for piece in it:
            piece()

    tick.flush = flush
    return tick


def _prompt_mix_kernel(sinks_ref, x_ref, sh_ref, sc_ref, sx_ref, ssh_ref, ssc_ref, ng_ref, win_ref, convw_ref,
                       convb_ref, lng_ref, lnb_ref, hlb_ref, hng_ref, bias_ref,
                       mix_ref, sproj_ref, convo_ref, so_ref, ko_ref, vo_ref,
                       proj_ref, wbf_ref, abuf, kbuf, vbuf, st_ref, *, layer, tile):
    t = pl.program_id(1)
    last = pl.num_programs(1) - 1

    @pl.when((pl.program_id(0) == 0) & (t == 0))
    def _():
        wbf_ref[...] = win_ref[...].astype(BF16)

    @pl.when(t == 0)
    def _():
        abuf[0:CONV_PAD, :] = jnp.zeros((CONV_PAD, D_CONV), F32)
        abuf[CONV_PAD + tile:CONV_PAD + tile + SUBLANES, :] = jnp.zeros((SUBLANES, D_CONV), F32)
        kbuf[0:ATTN_BLOCK, :] = jnp.zeros((ATTN_BLOCK, D_KV), BF16)
        vbuf[0:ATTN_BLOCK, :] = jnp.zeros((ATTN_BLOCK, D_KV), BF16)
        st_ref[...] = jnp.zeros(st_ref.shape, F32)

    h_in = _modulated_norm(x_ref[...], ng_ref[...], sc_ref[...], sh_ref[...])
    for lo, hi in ((OFF_AVAL, OFF_Q), (OFF_F, OFF_I), (OFF_Q, OFF_F), (OFF_I, OFF_G), (OFF_G, OFF_QA),
                   (OFF_QA, IN_WIDTH)):
        proj_ref[:, lo:hi] = jnp.dot(h_in, wbf_ref[:, lo:hi], preferred_element_type=F32)

    kbuf[ATTN_BLOCK:ATTN_BLOCK + tile, :] = proj_ref[:, OFF_KA:OFF_KA + D_KV].astype(BF16)
    vbuf[ATTN_BLOCK:ATTN_BLOCK + tile, :] = proj_ref[:, OFF_VA:OFF_VA + D_KV].astype(BF16)
    scale = HEAD_DIM ** -0.5
    attn = {}

    def attn_scores(blk):
        def run():
            r0 = blk * ATTN_BLOCK
            for h in range(H_ATTN):
                kv = h // GROUP
                q = (proj_ref[r0:r0 + ATTN_BLOCK, OFF_QA + h * HEAD_DIM:OFF_QA + (h + 1) * HEAD_DIM]
                     * scale).astype(BF16)
                kall = kbuf[r0:r0 + 2 * ATTN_BLOCK, kv * HEAD_DIM:(kv + 1) * HEAD_DIM]
                attn[blk, h] = lax.dot_general(q, kall, NT_DIMS, preferred_element_type=F32)
        return run

    def attn_softmax(blk):
        def run():
            for h in range(H_ATTN):
                s = attn[blk, h] + bias_ref[h]
                if blk == 0:
                    col = lax.broadcasted_iota(jnp.int32, s.shape, 1)
                    s = jnp.where(col + (t * tile - ATTN_BLOCK) >= 0, s, -jnp.inf)
                p, den = _sink_softmax(s, sinks_ref[h])
                attn[blk, h] = (p.astype(BF16), den)
        return run

    def attn_values(blk):
        def run():
            r0 = blk * ATTN_BLOCK
            heads = []
            for h in range(H_ATTN):
                kv = h // GROUP
                p, den = attn[blk, h]
                vall = vbuf[r0:r0 + 2 * ATTN_BLOCK, kv * HEAD_DIM:(kv + 1) * HEAD_DIM]
                heads.append(jnp.dot(p, vall, preferred_element_type=F32) / den)
            mix_ref[r0:r0 + ATTN_BLOCK, D_CONV + D_HGRN:D_MODEL] = jnp.concatenate(heads, axis=1).astype(BF16)
        return run

    tick = _ticker([stage(blk) for blk in range(tile // ATTN_BLOCK)
                    for stage in (attn_scores, attn_softmax, attn_values)])

    abuf[CONV_PAD:CONV_PAD + tile, :] = (proj_ref[:, OFF_AVAL:OFF_AVAL + D_CONV]
                                         * jax.nn.sigmoid(proj_ref[:, OFF_AGATE:OFF_AGATE + D_CONV]))
    first_row = CONV_PAD - (CONV_WIDTH - 1)
    acc = jnp.broadcast_to(convb_ref[...], (tile, D_CONV))
    for r in range(SUBLANES):
        z = None
        for off in range(r, first_row + CONV_WIDTH, SUBLANES):
            j = off - first_row
            if j < 0:
                continue
            term = convw_ref[j:j + 1, :] * abuf[off - r:off - r + tile + SUBLANES, :]
            z = term if z is None else z + term
        acc = acc + (z[0:tile] if r == 0 else pltpu.roll(z, tile + SUBLANES - r, 0)[0:tile])
    mix_ref[:, 0:D_CONV] = _conv_ln_swish(acc, lng_ref[...], lnb_ref[...]).astype(BF16)
    tick()

    lb = _layer_lb(hlb_ref[...], layer)
    hng = hng_ref[...]
    span = HGRN_SPAN * HGRN_CHUNK
    ri = lax.broadcasted_iota(jnp.int32, (span, span), 0)
    ci = lax.broadcasted_iota(jnp.int32, (span, span), 1)
    tri = jnp.where((ri >= ci) & (ri // HGRN_CHUNK == ci // HGRN_CHUNK), 1.0, 0.0).astype(BF16)
    for i in range(tile // span):
        _hgrn_span(proj_ref, mix_ref, st_ref, i * span, lb, hng, tri, tick)
    tick.flush()

    @pl.when(t == last)
    def _():
        convo_ref[...] = abuf[CONV_PAD + tile - (CONV_WIDTH - 1):CONV_PAD + tile, :]
        for h in range(H_HGRN):
            so_ref[h] = st_ref[h].T
        ko_ref[...] = proj_ref[tile - WINDOW:tile, OFF_KA:OFF_KA + D_KV]
        vo_ref[...] = proj_ref[tile - WINDOW:tile, OFF_VA:OFF_VA + D_KV]

    abuf[0:CONV_PAD, :] = abuf[tile:tile + CONV_PAD, :]
    kbuf[0:ATTN_BLOCK, :] = kbuf[tile:tile + ATTN_BLOCK, :]
    vbuf[0:ATTN_BLOCK, :] = vbuf[tile:tile + ATTN_BLOCK, :]

    @pl.when((pl.program_id(0) == pl.num_programs(0) - 1) & (t == last))
    def _():
        h_s = _modulated_norm(sx_ref[...], ng_ref[...], ssc_ref[...], ssh_ref[...])
        sproj_ref[...] = jnp.dot(h_s, wbf_ref[...], preferred_element_type=F32)


def _carry_specs(carried):
    return [pl.BlockSpec(memory_space=pl.ANY)] * len(carried)


def _without_carry(kernel_fn, n_in, n_carried, *refs):
    return kernel_fn(*refs[:n_in], *refs[n_in + n_carried:])


def _carried(kernel_fn, n_in, n_carried):
    return functools.partial(_without_carry, kernel_fn, n_in, n_carried)


def _prompt_mixers(x, mod, xs, mod_all, n_sample, norm_g, w_in, sinks, conv_w, conv_b, ln_g, ln_b, hgrn_lb, hng,
                   bias_p, layer, carried):
    B, T = x.shape[:2]
    tile = MIX_TILE
    depth = hgrn_lb.shape[0]
    const2 = lambda b, t: (0, 0)
    inputs = (sinks, x, mod, mod, xs, mod_all, mod_all, norm_g.reshape(1, D_MODEL), w_in, conv_w,
              conv_b.reshape(1, D_CONV), ln_g.reshape(1, D_CONV), ln_b.reshape(1, D_CONV), hgrn_lb,
              hng.reshape(1, DV_HGRN), bias_p)
    return pl.pallas_call(
        _carried(functools.partial(_prompt_mix_kernel, layer=layer, tile=tile), len(inputs), len(carried)),
        out_shape=(jax.ShapeDtypeStruct((B, T, D_MODEL), BF16),
                   jax.ShapeDtypeStruct((xs.shape[0], IN_WIDTH), F32),
                   jax.ShapeDtypeStruct((depth, B, CONV_WIDTH - 1, D_CONV), F32),
                   jax.ShapeDtypeStruct((depth, B, H_HGRN, DK_HGRN, DV_HGRN), F32),
                   jax.ShapeDtypeStruct((depth, B, WINDOW, D_KV), F32),
                   jax.ShapeDtypeStruct((depth, B, WINDOW, D_KV), F32)),
        grid=(B, T // tile),
        in_specs=[pl.BlockSpec(memory_space=pltpu.SMEM),
                  pl.BlockSpec((None, tile, D_MODEL), lambda b, t: (b, t, 0)),
                  pl.BlockSpec((None, 1, D_MODEL), lambda b, t: (b, 0, 0)),
                  pl.BlockSpec((None, 1, D_MODEL), lambda b, t: (b, 0, 1)),
                  pl.BlockSpec(xs.shape, const2, pipeline_mode=pl.Buffered(1))]
                 + _sample_mod_specs(n_sample, (0, 1), lambda b, t, chunk: (layer, 0, chunk)) + [
                  pl.BlockSpec((1, D_MODEL), const2),
                  pl.BlockSpec((None, D_MODEL, IN_WIDTH), lambda b, t: (layer, 0, 0), pipeline_mode=pl.Buffered(1)),
                  pl.BlockSpec((CONV_WIDTH, D_CONV), const2),
                  pl.BlockSpec((1, D_CONV), const2),
                  pl.BlockSpec((1, D_CONV), const2),
                  pl.BlockSpec((1, D_CONV), const2),
                  pl.BlockSpec((depth, D_HGRN), const2),
                  pl.BlockSpec((1, DV_HGRN), const2),
                  pl.BlockSpec((H_ATTN, ATTN_BLOCK, 2 * ATTN_BLOCK), lambda b, t: (0, 0, 0))] + _carry_specs(carried),
        out_specs=(pl.BlockSpec((None, tile, D_MODEL), lambda b, t: (b, t, 0)),
                   pl.BlockSpec((xs.shape[0], IN_WIDTH), const2),
                   pl.BlockSpec((None, None, CONV_WIDTH - 1, D_CONV), lambda b, t: (layer, b, 0, 0)),
                   pl.BlockSpec((None, None, H_HGRN, DK_HGRN, DV_HGRN), lambda b, t: (layer, b, 0, 0, 0)),
                   pl.BlockSpec((None, None, WINDOW, D_KV), lambda b, t: (layer, b, 0, 0)),
                   pl.BlockSpec((None, None, WINDOW, D_KV), lambda b, t: (layer, b, 0, 0))),
        input_output_aliases={len(inputs) + i: 2 + i for i in range(len(carried))},
        scratch_shapes=[pltpu.VMEM((tile, IN_WIDTH), F32),
                        pltpu.VMEM((D_MODEL, IN_WIDTH), BF16),
                        pltpu.VMEM((CONV_PAD + tile + SUBLANES, D_CONV), F32),
                        pltpu.VMEM((ATTN_BLOCK + tile, D_KV), BF16),
                        pltpu.VMEM((ATTN_BLOCK + tile, D_KV), BF16),
                        pltpu.VMEM((H_HGRN, DV_HGRN, DK_HGRN), F32)],
        compiler_params=pltpu.CompilerParams(dimension_semantics=("arbitrary", "arbitrary"),
                                             vmem_limit_bytes=VMEM_LIMIT),
        name="prompt_mixers",
    )(*inputs, *carried)


def _sample_mix_kernel(sinks_ref, proj_ref, cconv_ref, state_ref, ck_ref, cv_ref, convw_ref, convb_ref, lng_ref,
                       lnb_ref, hlb_ref, hng_ref, bias_ref,
                       mix_ref, convo_ref, so_ref, ko_ref, vo_ref, kpad_ref, vpad_ref, *,
                       layer, block, seq, w_buf):
    hist = CONV_WIDTH - 1

    @pl.when(pl.program_id(0) == 0)
    def _():
        for ref in (kpad_ref, vpad_ref):
            ref[:, 0:w_buf - seq, :] = jnp.zeros((block, w_buf - seq, D_KV), F32)

    lb = _layer_lb(hlb_ref[...], layer)
    hng = hng_ref[...]
    scale = HEAD_DIM ** -0.5
    elems = range(block)
    row8 = lax.broadcasted_iota(jnp.int32, (SUBLANES, DV_HGRN), 0)
    ones_rows = jnp.where((row8 >= seq) & (row8 < seq + 3), 1.0, 0.0)
    zrow = jnp.zeros((1, DK_HGRN), BF16)
    prow = lax.broadcasted_iota(jnp.int32, (seq, seq), 0)
    pcol = lax.broadcasted_iota(jnp.int32, (seq, seq), 1)
    grow = lax.broadcasted_iota(jnp.int32, (GROUP * seq, 1), 0)

    proj = [proj_ref[e * seq:(e + 1) * seq, :] for e in elems]

    glu = [p[:, OFF_AVAL:OFF_AVAL + D_CONV] * jax.nn.sigmoid(p[:, OFF_AGATE:OFF_AGATE + D_CONV]) for p in proj]
    full = [cconv_ref[i] for i in range(hist)]
    full += [jnp.concatenate([glu[e][t:t + 1] for e in elems], axis=0) for t in range(seq)]
    for i in range(hist):
        convo_ref[i] = full[i + seq]
    conv_out = []
    for t in range(seq):
        acc = jnp.broadcast_to(convb_ref[...], (block, D_CONV))
        for j in range(CONV_WIDTH):
            acc = acc + convw_ref[j:j + 1, :] * full[t + j]
        conv_out.append(_conv_ln_swish(acc, lng_ref[...], lnb_ref[...]))
    out_a = [jnp.concatenate([conv_out[t][e:e + 1] for t in range(seq)], axis=0) for e in elems]

    units = [(e, h) for e in elems for h in range(H_HGRN)]
    ops = {}
    for e in elems:
        p = proj[e]
        g, k = _hgrn_gates(p[:, OFF_F:OFF_F + D_HGRN], lb)
        b = _cumsum_rows_small(g)
        for h in range(H_HGRN):
            cs = slice(h * DK_HGRN, (h + 1) * DK_HGRN)
            q = p[:, OFF_Q + h * DK_HGRN:OFF_Q + (h + 1) * DK_HGRN]
            v = p[:, OFF_I + h * DV_HGRN:OFF_I + (h + 1) * DV_HGRN]
            bb = b[:, cs]
            bl = bb[seq - 1:seq, :]
            kst = (k[:, cs] * jnp.exp(bl - bb)).astype(BF16)
            x = jnp.concatenate([kst.astype(F32), *(part.astype(F32) for part in _split3_bf16(jnp.exp(bl))),
                                 zrow.astype(F32)], axis=0).astype(BF16)
            vpad = jnp.concatenate([v, jnp.zeros((SUBLANES - seq, DV_HGRN), F32)], axis=0)
            ops[e, h] = dict(qp=(q * jnp.exp(bb - bl)).astype(BF16), kst=kst, v=v.astype(BF16),
                             qt=(q * jnp.exp(bb)).astype(BF16), x=x,
                             r=jnp.concatenate([vpad, ones_rows], axis=1).astype(BF16))
    for u in units:
        o = ops[u]
        st = state_ref[u[0], u[1]]
        o["p"] = lax.dot_general(o["qp"], o["kst"], NT_DIMS, preferred_element_type=F32)
        o["inter"] = jnp.dot(o["qt"], st.astype(BF16), preferred_element_type=F32)
        me = lax.dot_general(o["x"], o["r"], TN_DIMS, preferred_element_type=F32)
        so_ref[u[0], u[1]] = me[:, DV_HGRN:] * st + me[:, :DV_HGRN]
    out_b = {}
    for u in units:
        o = ops[u]
        pm = jnp.where(prow >= pcol, o["p"], 0.0).astype(BF16)
        out = o["inter"] + jnp.dot(pm, o["v"], preferred_element_type=F32)
        gate = proj[u[0]][:, OFF_G + u[1] * DV_HGRN:OFF_G + (u[1] + 1) * DV_HGRN]
        out_b[u] = _hgrn_out(out, hng, gate)

    scores = {}
    for e in elems:
        p = proj[e]
        for kv in range(KV_HEADS):
            hs = slice(kv * HEAD_DIM, (kv + 1) * HEAD_DIM)
            q2 = jnp.concatenate([p[:, OFF_QA + h * HEAD_DIM:OFF_QA + (h + 1) * HEAD_DIM]
                                  for h in range(kv * GROUP, (kv + 1) * GROUP)], axis=0)
            q2 = (q2 * scale).astype(BF16)
            bias = bias_ref[kv]
            s_c = jnp.dot(q2, ck_ref[e, hs, :].astype(BF16), preferred_element_type=F32) + bias[:, 0:w_buf]
            s_n = (lax.dot_general(q2, p[:, OFF_KA + kv * HEAD_DIM:OFF_KA + (kv + 1) * HEAD_DIM].astype(BF16), NT_DIMS,
                                   preferred_element_type=F32) + bias[:, w_buf:w_buf + seq])
            scores[e, kv] = (s_c, s_n)
    out_c = {}
    for e in elems:
        p = proj[e]
        for kv in range(KV_HEADS):
            hs = slice(kv * HEAD_DIM, (kv + 1) * HEAD_DIM)
            sink = jnp.zeros((GROUP * seq, 1), F32)
            for gi in range(GROUP):
                sink = jnp.where(grow >= gi * seq, sinks_ref[kv * GROUP + gi], sink)
            s_c, s_n = scores[e, kv]
            m = jnp.maximum(jnp.maximum(jnp.max(s_c, axis=-1, keepdims=True), jnp.max(s_n, axis=-1, keepdims=True)),
                            sink)
            p_c = jnp.exp(s_c - m)
            p_n = jnp.exp(s_n - m)
            den = jnp.sum(p_c, axis=-1, keepdims=True) + jnp.sum(p_n, axis=-1, keepdims=True) + jnp.exp(sink - m)
            vnew = p[:, OFF_VA + kv * HEAD_DIM:OFF_VA + (kv + 1) * HEAD_DIM].astype(BF16)
            o2 = (lax.dot_general(p_c.astype(BF16), cv_ref[e, hs, :].astype(BF16), NT_DIMS,
                                  preferred_element_type=F32)
                  + jnp.dot(p_n.astype(BF16), vnew, preferred_element_type=F32)) / den
            for gi in range(GROUP):
                out_c[e, kv * GROUP + gi] = o2[gi * seq:(gi + 1) * seq]
    lane = lax.broadcasted_iota(jnp.int32, (D_KV, w_buf), 1)
    for e in elems:
        p = proj[e]
        for pad, cache, new, out in ((kpad_ref, ck_ref, p[:, OFF_KA:OFF_KA + D_KV], ko_ref),
                                     (vpad_ref, cv_ref, p[:, OFF_VA:OFF_VA + D_KV], vo_ref)):
            pad[e, w_buf - seq:w_buf, :] = new
            out[e] = jnp.where(lane >= w_buf - seq, pad[e].T, pltpu.roll(cache[e], w_buf - seq, 1))

    for e in elems:
        parts = [out_a[e]] + [out_b[e, h] for h in range(H_HGRN)] + [out_c[e, h] for h in range(H_ATTN)]
        mix_ref[e * seq:(e + 1) * seq, :] = jnp.concatenate(parts, axis=1)


def _sample_mixers(proj2, sinks, cache_conv, state, cache_k, cache_v, conv_w, conv_b, ln_g, ln_b, hgrn_lb, hng,
                   bias_s, layer, carried):
    B = state.shape[1]
    seq = proj2.shape[0] // B
    w_buf = cache_k.shape[3]
    block = SAMPLE_BLOCK
    depth = hgrn_lb.shape[0]
    hist = CONV_WIDTH - 1
    const2 = lambda i: (0, 0)
    cache_specs = [pl.BlockSpec((None, hist, block, D_CONV), lambda i: (layer, 0, i, 0)),
                   pl.BlockSpec((None, block, H_HGRN, DK_HGRN, DV_HGRN), lambda i: (layer, i, 0, 0, 0)),
                   pl.BlockSpec((None, block, D_KV, w_buf), lambda i: (layer, i, 0, 0)),
                   pl.BlockSpec((None, block, D_KV, w_buf), lambda i: (layer, i, 0, 0))]
    inputs = (sinks, proj2, cache_conv, state, cache_k, cache_v, conv_w, conv_b.reshape(1, D_CONV),
              ln_g.reshape(1, D_CONV), ln_b.reshape(1, D_CONV), hgrn_lb, hng.reshape(1, DV_HGRN), bias_s)
    return pl.pallas_call(
        _carried(functools.partial(_sample_mix_kernel, layer=layer, block=block, seq=seq, w_buf=w_buf),
                 len(inputs), len(carried)),
        out_shape=(jax.ShapeDtypeStruct((B * seq, D_MODEL), F32),
                   jax.ShapeDtypeStruct((depth, hist, B, D_CONV), F32),
                   jax.ShapeDtypeStruct((depth, B, H_HGRN, DK_HGRN, DV_HGRN), F32),
                   jax.ShapeDtypeStruct((depth, B, D_KV, w_buf), F32),
                   jax.ShapeDtypeStruct((depth, B, D_KV, w_buf), F32)),
        grid=(B // block,),
        in_specs=[pl.BlockSpec(memory_space=pltpu.SMEM),
                  pl.BlockSpec((block * seq, IN_WIDTH), lambda i: (i, 0))] + cache_specs + [
                  pl.BlockSpec((CONV_WIDTH, D_CONV), const2),
                  pl.BlockSpec((1, D_CONV), const2),
                  pl.BlockSpec((1, D_CONV), const2),
                  pl.BlockSpec((1, D_CONV), const2),
                  pl.BlockSpec((depth, D_HGRN), const2),
                  pl.BlockSpec((1, DV_HGRN), const2),
                  pl.BlockSpec((KV_HEADS, GROUP * seq, 2 * ATTN_BLOCK), lambda i: (0, 0, 0))] + _carry_specs(carried),
        out_specs=tuple([pl.BlockSpec((block * seq, D_MODEL), lambda i: (i, 0))] + cache_specs),
        input_output_aliases={len(inputs) + i: 1 + i for i in range(len(carried))},
        scratch_shapes=[pltpu.VMEM((block, w_buf, D_KV), F32),
                        pltpu.VMEM((block, w_buf, D_KV), F32)],
        compiler_params=pltpu.CompilerParams(dimension_semantics=("arbitrary",), vmem_limit_bytes=VMEM_LIMIT),
        name="sample_mixers",
    )(*inputs, *carried)


def kernel(x_prompt, x_sample, cache_conv, state_hgrn, cache_swa_k, cache_swa_v, c_prompt, c_sample, rel_bias, w_ada, b_ada, norm_mix_g, w_in, conv_w, conv_b, conv_ln_g, conv_ln_b, hgrn_lb, hgrn_norm_g, attn_sinks, w_out, norm_mlp_g, w_up, w_down, final_g):
    Bp, Tp = x_prompt.shape[:2]
    Bs, Ts = x_sample.shape[:2]
    depth = w_in.shape[0]
    w_buf = cache_swa_k.shape[2]
    assert Tp % MIX_TILE == 0 and (Bp * Tp) % TOK_TILE == 0 and Tp % TOK_TILE == 0 and Bs % SAMPLE_BLOCK == 0
    assert w_buf == WINDOW and GROUP * Ts == SUBLANES

    bias_p, bias_s = _bias_tables(rel_bias, Ts, w_buf)
    mod = _modulation(jnp.concatenate([c_sample, c_prompt], axis=0), w_ada, b_ada)
    hlb = hgrn_lb.astype(F32)
    cc = jnp.swapaxes(cache_conv, 1, 2)
    ck = jnp.swapaxes(cache_swa_k.reshape(depth, Bs, w_buf, D_KV), 2, 3)
    cv = jnp.swapaxes(cache_swa_v.reshape(depth, Bs, w_buf, D_KV), 2, 3)

    xp = x_prompt.reshape(Bp * Tp, D_MODEL)
    xs = x_sample.reshape(Bs * Ts, D_MODEL)
    caches_p = ()
    caches_s = ()
    for l in range(depth):
        final = l == depth - 1
        mod_p = mod[l, Bs:].reshape(Bp, 1, N_MOD * D_MODEL)
        mix_p, proj_s, *caches_p = _prompt_mixers(xp.reshape(Bp, Tp, D_MODEL), mod_p, xs, mod, Bs, norm_mix_g[l], w_in,
                                                  attn_sinks[l], conv_w[l], conv_b[l], conv_ln_g[l], conv_ln_b[l], hlb,
                                                  hgrn_norm_g[l], bias_p, l, caches_p)
        mix_s, *caches_s = _sample_mixers(proj_s, attn_sinks[l], cc, state_hgrn, ck, cv, conv_w[l],
                                          conv_b[l], conv_ln_g[l], conv_ln_b[l], hlb, hgrn_norm_g[l], bias_s, l,
                                          caches_s)
        xp, xs = _out_mlp(mix_p.reshape(Bp * Tp, D_MODEL), xp, mod_p, mix_s, xs, mod, Bs, norm_mlp_g[l], w_out,
                          w_up, w_down, final_g, l, Tp // TOK_TILE, final)
    cp, sp, kp, vp = caches_p
    cs, ss, ksn, vsn = caches_s
    cs, ksn, vsn = jnp.swapaxes(cs, 1, 2), jnp.swapaxes(ksn, 2, 3), jnp.swapaxes(vsn, 2, 3)
    return (xp.reshape(Bp, Tp, D_MODEL), xs.reshape(Bs, Ts, D_MODEL), cp, cs, sp, ss,
            kp.reshape(depth, Bp, WINDOW, KV_HEADS, HEAD_DIM), ksn.reshape(depth, Bs, w_buf, KV_HEADS, HEAD_DIM),
            vp.reshape(depth, Bp, WINDOW, KV_HEADS, HEAD_DIM), vsn.reshape(depth, Bs, w_buf, KV_HEADS, HEAD_DIM))
```

```python
import functools
import math

import jax
import jax.numpy as jnp
from jax import lax
from jax.experimental import pallas as pl
from jax.experimental.pallas import tpu as pltpu

F32 = jnp.float32
BF16 = jnp.bfloat16

D_MODEL = 1024
D_CONV = 256
CONV_WIDTH = 31
H_HGRN = 4
DK_HGRN = 128
DV_HGRN = 128
D_HGRN = 512
HEAD_DIM = 64
H_ATTN = 4
KV_HEADS = 2
GROUP = H_ATTN // KV_HEADS
D_ATTN = H_ATTN * HEAD_DIM
D_KV = KV_HEADS * HEAD_DIM
WINDOW = 128
ATTN_BLOCK = 128
NUM_BUCKETS = 32
MAX_DISTANCE = 128
D_FF = 4 * D_MODEL
N_MOD = 6
EPS = 1e-6

OFF_AVAL = 0
OFF_AGATE = OFF_AVAL + D_CONV
OFF_Q = OFF_AGATE + D_CONV
OFF_F = OFF_Q + H_HGRN * DK_HGRN
OFF_I = OFF_F + H_HGRN * DK_HGRN
OFF_G = OFF_I + D_HGRN
OFF_QA = OFF_G + D_HGRN
OFF_KA = OFF_QA + D_ATTN
OFF_VA = OFF_KA + D_KV
IN_WIDTH = OFF_VA + D_KV

HGRN_CHUNK = 64
HGRN_KEYBLOCK = 32
HGRN_SPAN = 4
SUBLANES = 8
CONV_PAD = 32
MIX_TILE = 512
TOK_TILE = 512
SAMPLE_BLOCK = 16
FF_CHUNK = 1024
CAST_STEPS = 8
MOD_COLS = 3072
VMEM_LIMIT = 56 * 1024 * 1024

NT_DIMS = (((1,), (1,)), ((), ()))
TN_DIMS = (((0,), (0,)), ((), ()))


def _silu(x):
    return x * jax.nn.sigmoid(x)


def _rms_rows(x):
    return x * lax.rsqrt(jnp.mean(x * x, axis=-1, keepdims=True) + EPS)


def _layer_lb(hlb, layer):
    m = jnp.max(hlb, axis=0, keepdims=True)
    e = jnp.exp(hlb - m)
    p = e / jnp.sum(e, axis=0, keepdims=True)
    lb = jnp.zeros_like(m)
    for i in range(1, layer + 1):
        lb = lb + p[i:i + 1, :]
    return lb


def _split3_bf16(x):
    hi = x.astype(BF16)
    r = x - hi.astype(F32)
    mid = r.astype(BF16)
    return hi, mid, (r - mid.astype(F32)).astype(BF16)


def _select_rows_mxu(sel, x):
    return sum(jnp.dot(sel, part, preferred_element_type=F32) for part in _split3_bf16(x))


def _cumsum_rows_small(g):
    row = lax.broadcasted_iota(jnp.int32, g.shape, 0)
    b = jnp.zeros_like(g)
    for u in range(g.shape[0]):
        b = b + jnp.where(row >= u, g[u:u + 1, :], 0.0)
    return b


def _hgrn_span(proj_ref, mix_ref, st_ref, row0, lb, hng, tri, tick):
    L, KB = HGRN_CHUNK, HGRN_KEYBLOCK
    span = HGRN_SPAN * L
    g, k = _hgrn_gates(proj_ref[pl.ds(row0, span), OFF_F:OFF_F + D_HGRN], lb)
    b = _select_rows_mxu(tri, g)
    units = [(c, h) for c in range(HGRN_SPAN) for h in range(H_HGRN)]

    ops = {}
    for c, h in units:
        rows = pl.ds(row0 + c * L, L)
        cs = slice(h * DK_HGRN, (h + 1) * DK_HGRN)
        q = proj_ref[rows, OFF_Q + h * DK_HGRN:OFF_Q + (h + 1) * DK_HGRN]
        v = proj_ref[rows, OFF_I + h * DV_HGRN:OFF_I + (h + 1) * DV_HGRN].astype(BF16)
        kk = k[c * L:(c + 1) * L, cs]
        bb = b[c * L:(c + 1) * L, cs]
        qp, kp = [], []
        for lo in range(0, L, KB):
            r = bb[lo + KB // 2 - 1:lo + KB // 2, :]
            kp.append((kk[lo:lo + KB] * jnp.exp(r - bb[lo:lo + KB])).astype(BF16))
            qp.append((q[lo:] * jnp.exp(bb[lo:] - r)).astype(BF16))
        bl = bb[L - 1:L, :]
        ops[c, h] = dict(qp=qp, kp=kp, v=v, qt=(q * jnp.exp(bb)).astype(BF16),
                         kst=(kk * jnp.exp(bl - bb)).astype(BF16), e=jnp.exp(bl))
    tick()

    for u in units:
        o = ops[u]
        o["p"] = [lax.dot_general(qp, kp, NT_DIMS, preferred_element_type=F32) for qp, kp in zip(o["qp"], o["kp"])]
        o["m"] = lax.dot_general(o["v"], o["kst"], TN_DIMS, preferred_element_type=F32)
    tick()

    for u in units:
        pm = []
        for p in ops[u]["p"]:
            row = lax.broadcasted_iota(jnp.int32, p.shape, 0)
            col = lax.broadcasted_iota(jnp.int32, p.shape, 1)
            pm.append(jnp.where(row >= col, p, 0.0).astype(BF16))
        ops[u]["p"] = pm
    tick()

    for u in units:
        o = ops[u]
        blocks = [None] * (L // KB)
        for j, p in enumerate(o["p"]):
            cj = jnp.dot(p, o["v"][j * KB:(j + 1) * KB], preferred_element_type=F32)
            for i in range(j, L // KB):
                piece = cj[(i - j) * KB:(i - j + 1) * KB]
                blocks[i] = piece if blocks[i] is None else blocks[i] + piece
        o["o"] = jnp.concatenate(blocks, axis=0)
    tick()

    for h in range(H_HGRN):
        st = st_ref[h]
        for c in range(HGRN_SPAN):
            o = ops[c, h]
            out = o["o"] + lax.dot_general(o["qt"], st.astype(BF16), NT_DIMS, preferred_element_type=F32)
            st = o["e"] * st + o["m"]
            rows = pl.ds(row0 + c * L, L)
            gate = proj_ref[rows, OFF_G + h * DV_HGRN:OFF_G + (h + 1) * DV_HGRN]
            mix_ref[rows, D_CONV + h * DV_HGRN:D_CONV + (h + 1) * DV_HGRN] = _hgrn_out(out, hng, gate).astype(BF16)
        st_ref[h] = st


def _sink_softmax(s, sink):
    m = jnp.maximum(jnp.max(s, axis=-1, keepdims=True), sink)
    p = jnp.exp(s - m)
    return p, jnp.sum(p, axis=-1, keepdims=True) + jnp.exp(sink - m)


def _bias_kernel(tab_ref, bp_ref, bs_ref, op_ref, os_ref, *, seq):
    bk = bp_ref[...]
    for h in range(H_ATTN):
        acc = jnp.full(bk.shape, -jnp.inf, F32)
        for bkt in range(NUM_BUCKETS):
            acc = jnp.where(bk == bkt, tab_ref[bkt, h], acc)
        op_ref[h] = acc
    bk = bs_ref[...]
    row = lax.broadcasted_iota(jnp.int32, bk.shape, 0)
    for kv in range(KV_HEADS):
        acc = jnp.full(bk.shape, -jnp.inf, F32)
        for bkt in range(NUM_BUCKETS):
            val = jnp.full(bk.shape, tab_ref[bkt, kv * GROUP], F32)
            for gi in range(1, GROUP):
                val = jnp.where(row >= gi * seq, tab_ref[bkt, kv * GROUP + gi], val)
            acc = jnp.where(bk == bkt, val, acc)
        os_ref[kv] = acc


def _t5_bucket(rel):
    n = jnp.maximum(rel, 0)
    max_exact = NUM_BUCKETS // 2
    nf = jnp.maximum(n, max_exact).astype(F32)
    large = max_exact + (jnp.log(nf / max_exact) / math.log(MAX_DISTANCE / max_exact)
                         * (NUM_BUCKETS - max_exact)).astype(jnp.int32)
    large = jnp.minimum(large, NUM_BUCKETS - 1)
    return jnp.where(n < max_exact, n, large)


def _bias_tables(rel_bias, dec_seq, w_buf):
    qi = jnp.arange(ATTN_BLOCK, dtype=jnp.int32)[:, None]
    kc = jnp.arange(2 * ATTN_BLOCK, dtype=jnp.int32)[None, :]
    rel_p = qi + ATTN_BLOCK - kc
    bucket_p = jnp.where((rel_p >= 0) & (rel_p <= WINDOW), _t5_bucket(rel_p), -1)
    ts = (jnp.arange(GROUP * dec_seq, dtype=jnp.int32) % dec_seq)[:, None]
    js = jnp.arange(2 * ATTN_BLOCK, dtype=jnp.int32)[None, :]
    rel_s = w_buf + ts - js
    ok_s = (rel_s >= 0) & (rel_s <= WINDOW) & (js < w_buf + dec_seq)
    bucket_s = jnp.where(ok_s, _t5_bucket(rel_s), -1)
    return pl.pallas_call(
        functools.partial(_bias_kernel, seq=dec_seq),
        out_shape=(jax.ShapeDtypeStruct((H_ATTN, ATTN_BLOCK, 2 * ATTN_BLOCK), F32),
                   jax.ShapeDtypeStruct((KV_HEADS, GROUP * dec_seq, 2 * ATTN_BLOCK), F32)),
        in_specs=[pl.BlockSpec(memory_space=pltpu.SMEM),
                  pl.BlockSpec(memory_space=pltpu.VMEM),
                  pl.BlockSpec(memory_space=pltpu.VMEM)],
        out_specs=(pl.BlockSpec(memory_space=pltpu.VMEM), pl.BlockSpec(memory_space=pltpu.VMEM)),
        name="rel_bias_tables",
    )(rel_bias.astype(F32), bucket_p, bucket_s)


def _mod_kernel(c_ref, w_ref, b_ref, o_ref):
    s = _silu(c_ref[...]).astype(BF16)
    o_ref[...] = jnp.dot(s, w_ref[...].astype(BF16), preferred_element_type=F32) + b_ref[...]


def _modulation(c_all, w_ada, b_ada):
    depth = w_ada.shape[0]
    n = c_all.shape[0]
    return pl.pallas_call(
        _mod_kernel,
        out_shape=jax.ShapeDtypeStruct((depth, n, N_MOD * D_MODEL), F32),
        grid=(depth, N_MOD * D_MODEL // MOD_COLS),
        in_specs=[pl.BlockSpec((n, D_MODEL), lambda l, j: (0, 0)),
                  pl.BlockSpec((None, D_MODEL, MOD_COLS), lambda l, j: (l, 0, j)),
                  pl.BlockSpec((None, 1, MOD_COLS), lambda l, j: (l, 0, j))],
        out_specs=pl.BlockSpec((None, n, MOD_COLS), lambda l, j: (l, 0, j)),
        compiler_params=pltpu.CompilerParams(dimension_semantics=("arbitrary", "arbitrary"),
                                             vmem_limit_bytes=VMEM_LIMIT),
        name="adaln_modulation",
    )(c_all, w_ada, b_ada.reshape(depth, 1, N_MOD * D_MODEL))


def _mod_rows(m, n_tokens):
    if m.shape[0] == 1:
        return m
    reps = n_tokens // m.shape[0]
    tok = lax.broadcasted_iota(jnp.int32, (n_tokens, m.shape[0]), 0)
    bat = lax.broadcasted_iota(jnp.int32, (n_tokens, m.shape[0]), 1)
    sel = jnp.where((tok >= bat * reps) & (tok < (bat + 1) * reps), 1.0, 0.0).astype(BF16)
    return _select_rows_mxu(sel, m)


def _modulated_norm(x, g, sc, sh):
    n = x.shape[0]
    return (_rms_rows(x) * (g * (1.0 + _mod_rows(sc, n))) + _mod_rows(sh, n)).astype(BF16)


def _sample_mod_specs(n_rows, chunks, index_map):
    return [pl.BlockSpec((None, n_rows, D_MODEL), functools.partial(index_map, chunk=c),
                         pipeline_mode=pl.Buffered(1)) for c in chunks]


def _mlp_tile(mix, x, g1, sh, sc, g2, ng, wout_ref, wup_ref, wdn_ref, fg, final):
    n = x.shape[0]
    x1 = x + _mod_rows(g1, n) * jnp.dot(mix.astype(BF16), wout_ref[...], preferred_element_type=F32)
    h = _modulated_norm(x1, ng, sc, sh)
    acc = None
    for c in range(D_FF // FF_CHUNK):
        u = jnp.dot(h, wup_ref[:, c * FF_CHUNK:(c + 1) * FF_CHUNK], preferred_element_type=F32)
        u = jnp.square(jnp.maximum(u, 0.0)).astype(BF16)
        d = jnp.dot(u, wdn_ref[c * FF_CHUNK:(c + 1) * FF_CHUNK, :], preferred_element_type=F32)
        acc = d if acc is None else acc + d
    x2 = x1 + _mod_rows(g2, n) * acc
    return _rms_rows(x2) * fg if final else x2


def _mlp_kernel(mix_ref, x_ref, g1_ref, sh_ref, sc_ref, g2_ref, smix_ref, sx_ref, sg1_ref, ssh_ref, ssc_ref, sg2_ref,
                ng_ref, wout_ref, wup_ref, wdn_ref, fg_ref, o_ref, so_ref, wout_s, wup_s, wdn_s, *, final):
    s = pl.program_id(0)

    @pl.when(s < CAST_STEPS)
    def _():
        for src, dst in ((wout_ref, wout_s), (wup_ref, wup_s), (wdn_ref, wdn_s)):
            rows = src.shape[0]
            dst[pl.ds(pl.multiple_of(s * rows, rows), rows), :] = src[...].astype(BF16)

    weights = (ng_ref[...], wout_s, wup_s, wdn_s, fg_ref[...], final)

    @pl.when(s >= CAST_STEPS)
    def _():
        o_ref[...] = _mlp_tile(mix_ref[...], x_ref[...], g1_ref[...], sh_ref[...], sc_ref[...], g2_ref[...], *weights)

    @pl.when(s == pl.num_programs(0) - 1)
    def _():
        so_ref[...] = _mlp_tile(smix_ref[...], sx_ref[...], sg1_ref[...], ssh_ref[...], ssc_ref[...], sg2_ref[...],
                                *weights)


def _out_mlp(mix2, x2, mod, mix_s, xs, mod_all, n_sample, norm_g, w_out, w_up, w_down, final_g, layer,
             tiles_per_batch, final):
    n = x2.shape[0]
    tile = TOK_TILE
    const = lambda i: (0, 0)
    tok = lambda i: (jnp.maximum(i - CAST_STEPS, 0), 0)
    chunk_of_layer = lambda i: (layer, jnp.minimum(i, CAST_STEPS - 1), 0)
    mod_p = lambda chunk: pl.BlockSpec((None, 1, D_MODEL),
                                       lambda i: (jnp.maximum(i - CAST_STEPS, 0) // tiles_per_batch, 0, chunk))
    whole = lambda a: pl.BlockSpec(a.shape, const, pipeline_mode=pl.Buffered(1))
    return pl.pallas_call(
        functools.partial(_mlp_kernel, final=final),
        out_shape=(jax.ShapeDtypeStruct((n, D_MODEL), F32), jax.ShapeDtypeStruct(xs.shape, F32)),
        grid=(CAST_STEPS + n // tile,),
        in_specs=[pl.BlockSpec((tile, D_MODEL), tok),
                  pl.BlockSpec((tile, D_MODEL), tok),
                  mod_p(2), mod_p(3), mod_p(4), mod_p(5),
                  whole(mix_s), whole(xs)]
                 + _sample_mod_specs(n_sample, (2, 3, 4, 5), lambda i, chunk: (layer, 0, chunk)) + [
                  pl.BlockSpec((1, D_MODEL), const),
                  pl.BlockSpec((None, D_MODEL // CAST_STEPS, D_MODEL), chunk_of_layer),
                  pl.BlockSpec((None, D_MODEL // CAST_STEPS, D_FF), chunk_of_layer),
                  pl.BlockSpec((None, D_FF // CAST_STEPS, D_MODEL), chunk_of_layer),
                  pl.BlockSpec((1, D_MODEL), const)],
        out_specs=(pl.BlockSpec((tile, D_MODEL), tok), pl.BlockSpec(xs.shape, const, pipeline_mode=pl.Buffered(1))),
        scratch_shapes=[pltpu.VMEM((D_MODEL, D_MODEL), BF16),
                        pltpu.VMEM((D_MODEL, D_FF), BF16),
                        pltpu.VMEM((D_FF, D_MODEL), BF16)],
        compiler_params=pltpu.CompilerParams(dimension_semantics=("arbitrary",), vmem_limit_bytes=VMEM_LIMIT),
        name="out_projection_mlp",
    )(mix2, x2, mod, mod, mod, mod, mix_s, xs, mod_all, mod_all, mod_all, mod_all, norm_g.reshape(1, D_MODEL), w_out,
      w_up, w_down, final_g.reshape(1, D_MODEL))


def _conv_ln_swish(acc, lng, lnb):
    mu = jnp.mean(acc, axis=-1, keepdims=True)
    xc = acc - mu
    y = xc * lax.rsqrt(jnp.mean(xc * xc, axis=-1, keepdims=True) + EPS) * lng + lnb
    return _silu(y)


def _hgrn_gates(fh, lb):
    f = lb + (1.0 - lb) * jax.nn.sigmoid(fh)
    return jnp.log(f), 1.0 - f


def _hgrn_out(o, hng, gate):
    return _rms_rows(o) * hng * _silu(gate)


def _ticker(pieces):
    it = iter(pieces)

    def tick():
        piece = next(it, None)
        if piece is not None:
            piece()

    def flush():
        for piece in it:
            piece()

    tick.flush = flush
    return tick


def _prompt_mix_kernel(sinks_ref, x_ref, sh_ref, sc_ref, sx_ref, ssh_ref, ssc_ref, ng_ref, win_ref, convw_ref,
                       convb_ref, lng_ref, lnb_ref, hlb_ref, hng_ref, bias_ref,
                       mix_ref, sproj_ref, convo_ref, so_ref, ko_ref, vo_ref,
                       proj_ref, wbf_ref, abuf, kbuf, vbuf, st_ref, *, layer, tile):
    t = pl.program_id(1)
    last = pl.num_programs(1) - 1

    @pl.when((pl.program_id(0) == 0) & (t == 0))
    def _():
        wbf_ref[...] = win_ref[...].astype(BF16)

    @pl.when(t == 0)
    def _():
        abuf[0:CONV_PAD, :] = jnp.zeros((CONV_PAD, D_CONV), F32)
        abuf[CONV_PAD + tile:CONV_PAD + tile + SUBLANES, :] = jnp.zeros((SUBLANES, D_CONV), F32)
        kbuf[0:ATTN_BLOCK, :] = jnp.zeros((ATTN_BLOCK, D_KV), BF16)
        vbuf[0:ATTN_BLOCK, :] = jnp.zeros((ATTN_BLOCK, D_KV), BF16)
        st_ref[...] = jnp.zeros(st_ref.shape, F32)

    h_in = _modulated_norm(x_ref[...], ng_ref[...], sc_ref[...], sh_ref[...])
    for lo, hi in ((OFF_AVAL, OFF_Q), (OFF_F, OFF_I), (OFF_Q, OFF_F), (OFF_I, OFF_G), (OFF_G, OFF_QA),
                   (OFF_QA, IN_WIDTH)):
        proj_ref[:, lo:hi] = jnp.dot(h_in, wbf_ref[:, lo:hi], preferred_element_type=F32)

    kbuf[ATTN_BLOCK:ATTN_BLOCK + tile, :] = proj_ref[:, OFF_KA:OFF_KA + D_KV].astype(BF16)
    vbuf[ATTN_BLOCK:ATTN_BLOCK + tile, :] = proj_ref[:, OFF_VA:OFF_VA + D_KV].astype(BF16)
    scale = HEAD_DIM ** -0.5
    attn = {}

    def attn_scores(blk):
        def run():
            r0 = blk * ATTN_BLOCK
            for h in range(H_ATTN):
                kv = h // GROUP
                q = (proj_ref[r0:r0 + ATTN_BLOCK, OFF_QA + h * HEAD_DIM:OFF_QA + (h + 1) * HEAD_DIM]
                     * scale).astype(BF16)
                kall = kbuf[r0:r0 + 2 * ATTN_BLOCK, kv * HEAD_DIM:(kv + 1) * HEAD_DIM]
                attn[blk, h] = lax.dot_general(q, kall, NT_DIMS, preferred_element_type=F32)
        return run

    def attn_softmax(blk):
        def run():
            for h in range(H_ATTN):
                s = attn[blk, h] + bias_ref[h]
                if blk == 0:
                    col = lax.broadcasted_iota(jnp.int32, s.shape, 1)
                    s = jnp.where(col + (t * tile - ATTN_BLOCK) >= 0, s, -jnp.inf)
                p, den = _sink_softmax(s, sinks_ref[h])
                attn[blk, h] = (p.astype(BF16), den)
        return run

    def attn_values(blk):
        def run():
            r0 = blk * ATTN_BLOCK
            heads = []
            for h in range(H_ATTN):
                kv = h // GROUP
                p, den = attn[blk, h]
                vall = vbuf[r0:r0 + 2 * ATTN_BLOCK, kv * HEAD_DIM:(kv + 1) * HEAD_DIM]
                heads.append(jnp.dot(p, vall, preferred_element_type=F32) / den)
            mix_ref[r0:r0 + ATTN_BLOCK, D_CONV + D_HGRN:D_MODEL] = jnp.concatenate(heads, axis=1).astype(BF16)
        return run

    tick = _ticker([stage(blk) for blk in range(tile // ATTN_BLOCK)
                    for stage in (attn_scores, attn_softmax, attn_values)])

    abuf[CONV_PAD:CONV_PAD + tile, :] = (proj_ref[:, OFF_AVAL:OFF_AVAL + D_CONV]
                                         * jax.nn.sigmoid(proj_ref[:, OFF_AGATE:OFF_AGATE + D_CONV]))
    first_row = CONV_PAD - (CONV_WIDTH - 1)
    acc = jnp.broadcast_to(convb_ref[...], (tile, D_CONV))
    for r in range(SUBLANES):
        z = None
        for off in range(r, first_row + CONV_WIDTH, SUBLANES):
            j = off - first_row
            if j < 0:
                continue
            term = convw_ref[j:j + 1, :] * abuf[off - r:off - r + tile + SUBLANES, :]
            z = term if z is None else z + term
        acc = acc + (z[0:tile] if r == 0 else pltpu.roll(z, tile + SUBLANES - r, 0)[0:tile])
    mix_ref[:, 0:D_CONV] = _conv_ln_swish(acc, lng_ref[...], lnb_ref[...]).astype(BF16)
    tick()

    lb = _layer_lb(hlb_ref[...], layer)
    hng = hng_ref[...]
    span = HGRN_SPAN * HGRN_CHUNK
    ri = lax.broadcasted_iota(jnp.int32, (span, span), 0)
    ci = lax.broadcasted_iota(jnp.int32, (span, span), 1)
    tri = jnp.where((ri >= ci) & (ri // HGRN_CHUNK == ci // HGRN_CHUNK), 1.0, 0.0).astype(BF16)
    for i in range(tile // span):
        _hgrn_span(proj_ref, mix_ref, st_ref, i * span, lb, hng, tri, tick)
    tick.flush()

    @pl.when(t == last)
    def _():
        convo_ref[...] = abuf[CONV_PAD + tile - (CONV_WIDTH - 1):CONV_PAD + tile, :]
        for h in range(H_HGRN):
            so_ref[h] = st_ref[h].T
        ko_ref[...] = proj_ref[tile - WINDOW:tile, OFF_KA:OFF_KA + D_KV]
        vo_ref[...] = proj_ref[tile - WINDOW:tile, OFF_VA:OFF_VA + D_KV]

    abuf[0:CONV_PAD, :] = abuf[tile:tile + CONV_PAD, :]
    kbuf[0:ATTN_BLOCK, :] = kbuf[tile:tile + ATTN_BLOCK, :]
    vbuf[0:ATTN_BLOCK, :] = vbuf[tile:tile + ATTN_BLOCK, :]

    @pl.when((pl.program_id(0) == pl.num_programs(0) - 1) & (t == last))
    def _():
        h_s = _modulated_norm(sx_ref[...], ng_ref[...], ssc_ref[...], ssh_ref[...])
        sproj_ref[...] = jnp.dot(h_s, wbf_ref[...], preferred_element_type=F32)


def _carry_specs(carried):
    return [pl.BlockSpec(memory_space=pl.ANY)] * len(carried)


def _without_carry(kernel_fn, n_in, n_carried, *refs):
    return kernel_fn(*refs[:n_in], *refs[n_in + n_carried:])


def _carried(kernel_fn, n_in, n_carried):
    return functools.partial(_without_carry, kernel_fn, n_in, n_carried)


def _prompt_mixers(x, mod, xs, mod_all, n_sample, norm_g, w_in, sinks, conv_w, conv_b, ln_g, ln_b, hgrn_lb, hng,
                   bias_p, layer, carried):
    B, T = x.shape[:2]
    tile = MIX_TILE
    depth = hgrn_lb.shape[0]
    const2 = lambda b, t: (0, 0)
    inputs = (sinks, x, mod, mod, xs, mod_all, mod_all, norm_g.reshape(1, D_MODEL), w_in, conv_w,
              conv_b.reshape(1, D_CONV), ln_g.reshape(1, D_CONV), ln_b.reshape(1, D_CONV), hgrn_lb,
              hng.reshape(1, DV_HGRN), bias_p)
    return pl.pallas_call(
        _carried(functools.partial(_prompt_mix_kernel, layer=layer, tile=tile), len(inputs), len(carried)),
        out_shape=(jax.ShapeDtypeStruct((B, T, D_MODEL), BF16),
                   jax.ShapeDtypeStruct((xs.shape[0], IN_WIDTH), F32),
                   jax.ShapeDtypeStruct((depth, B, CONV_WIDTH - 1, D_CONV), F32),
                   jax.ShapeDtypeStruct((depth, B, H_HGRN, DK_HGRN, DV_HGRN), F32),
                   jax.ShapeDtypeStruct((depth, B, WINDOW, D_KV), F32),
                   jax.ShapeDtypeStruct((depth, B, WINDOW, D_KV), F32)),
        grid=(B, T // tile),
        in_specs=[pl.BlockSpec(memory_space=pltpu.SMEM),
                  pl.BlockSpec((None, tile, D_MODEL), lambda b, t: (b, t, 0)),
                  pl.BlockSpec((None, 1, D_MODEL), lambda b, t: (b, 0, 0)),
                  pl.BlockSpec((None, 1, D_MODEL), lambda b, t: (b, 0, 1)),
                  pl.BlockSpec(xs.shape, const2, pipeline_mode=pl.Buffered(1))]
                 + _sample_mod_specs(n_sample, (0, 1), lambda b, t, chunk: (layer, 0, chunk)) + [
                  pl.BlockSpec((1, D_MODEL), const2),
                  pl.BlockSpec((None, D_MODEL, IN_WIDTH), lambda b, t: (layer, 0, 0), pipeline_mode=pl.Buffered(1)),
                  pl.BlockSpec((CONV_WIDTH, D_CONV), const2),
                  pl.BlockSpec((1, D_CONV), const2),
                  pl.BlockSpec((1, D_CONV), const2),
                  pl.BlockSpec((1, D_CONV), const2),
                  pl.BlockSpec((depth, D_HGRN), const2),
                  pl.BlockSpec((1, DV_HGRN), const2),
                  pl.BlockSpec((H_ATTN, ATTN_BLOCK, 2 * ATTN_BLOCK), lambda b, t: (0, 0, 0))] + _carry_specs(carried),
        out_specs=(pl.BlockSpec((None, tile, D_MODEL), lambda b, t: (b, t, 0)),
                   pl.BlockSpec((xs.shape[0], IN_WIDTH), const2),
                   pl.BlockSpec((None, None, CONV_WIDTH - 1, D_CONV), lambda b, t: (layer, b, 0, 0)),
                   pl.BlockSpec((None, None, H_HGRN, DK_HGRN, DV_HGRN), lambda b, t: (layer, b, 0, 0, 0)),
                   pl.BlockSpec((None, None, WINDOW, D_KV), lambda b, t: (layer, b, 0, 0)),
                   pl.BlockSpec((None, None, WINDOW, D_KV), lambda b, t: (layer, b, 0, 0))),
        input_output_aliases={len(inputs) + i: 2 + i for i in range(len(carried))},
        scratch_shapes=[pltpu.VMEM((tile, IN_WIDTH), F32),
                        pltpu.VMEM((D_MODEL, IN_WIDTH), BF16),
                        pltpu.VMEM((CONV_PAD + tile + SUBLANES, D_CONV), F32),
                        pltpu.VMEM((ATTN_BLOCK + tile, D_KV), BF16),
                        pltpu.VMEM((ATTN_BLOCK + tile, D_KV), BF16),
                        pltpu.VMEM((H_HGRN, DV_HGRN, DK_HGRN), F32)],
        compiler_params=pltpu.CompilerParams(dimension_semantics=("arbitrary", "arbitrary"),
                                             vmem_limit_bytes=VMEM_LIMIT),
        name="prompt_mixers",
    )(*inputs, *carried)


def _sample_mix_kernel(sinks_ref, proj_ref, cconv_ref, state_ref, ck_ref, cv_ref, convw_ref, convb_ref, lng_ref,
                       lnb_ref, hlb_ref, hng_ref, bias_ref,
                       mix_ref, convo_ref, so_ref, ko_ref, vo_ref, kpad_ref, vpad_ref, *,
                       layer, block, seq, w_buf):
    hist = CONV_WIDTH - 1

    @pl.when(pl.program_id(0) == 0)
    def _():
        for ref in (kpad_ref, vpad_ref):
            ref[:, 0:w_buf - seq, :] = jnp.zeros((block, w_buf - seq, D_KV), F32)

    lb = _layer_lb(hlb_ref[...], layer)
    hng = hng_ref[...]
    scale = HEAD_DIM ** -0.5
    elems = range(block)
    row8 = lax.broadcasted_iota(jnp.int32, (SUBLANES, DV_HGRN), 0)
    ones_rows = jnp.where((row8 >= seq) & (row8 < seq + 3), 1.0, 0.0)
    zrow = jnp.zeros((1, DK_HGRN), BF16)
    prow = lax.broadcasted_iota(jnp.int32, (seq, seq), 0)
    pcol = lax.broadcasted_iota(jnp.int32, (seq, seq), 1)
    grow = lax.broadcasted_iota(jnp.int32, (GROUP * seq, 1), 0)

    proj = [proj_ref[e * seq:(e + 1) * seq, :] for e in elems]

    glu = [p[:, OFF_AVAL:OFF_AVAL + D_CONV] * jax.nn.sigmoid(p[:, OFF_AGATE:OFF_AGATE + D_CONV]) for p in proj]
    full = [cconv_ref[i] for i in range(hist)]
    full += [jnp.concatenate([glu[e][t:t + 1] for e in elems], axis=0) for t in range(seq)]
    for i in range(hist):
        convo_ref[i] = full[i + seq]
    conv_out = []
    for t in range(seq):
        acc = jnp.broadcast_to(convb_ref[...], (block, D_CONV))
        for j in range(CONV_WIDTH):
            acc = acc + convw_ref[j:j + 1, :] * full[t + j]
        conv_out.append(_conv_ln_swish(acc, lng_ref[...], lnb_ref[...]))
    out_a = [jnp.concatenate([conv_out[t][e:e + 1] for t in range(seq)], axis=0) for e in elems]

    units = [(e, h) for e in elems for h in range(H_HGRN)]
    ops = {}
    for e in elems:
        p = proj[e]
        g, k = _hgrn_gates(p[:, OFF_F:OFF_F + D_HGRN], lb)
        b = _cumsum_rows_small(g)
        for h in range(H_HGRN):
            cs = slice(h * DK_HGRN, (h + 1) * DK_HGRN)
            q = p[:, OFF_Q + h * DK_HGRN:OFF_Q + (h + 1) * DK_HGRN]
            v = p[:, OFF_I + h * DV_HGRN:OFF_I + (h + 1) * DV_HGRN]
            bb = b[:, cs]
            bl = bb[seq - 1:seq, :]
            kst = (k[:, cs] * jnp.exp(bl - bb)).astype(BF16)
            x = jnp.concatenate([kst.astype(F32), *(part.astype(F32) for part in _split3_bf16(jnp.exp(bl))),
                                 zrow.astype(F32)], axis=0).astype(BF16)
            vpad = jnp.concatenate([v, jnp.zeros((SUBLANES - seq, DV_HGRN), F32)], axis=0)
            ops[e, h] = dict(qp=(q * jnp.exp(bb - bl)).astype(BF16), kst=kst, v=v.astype(BF16),
                             qt=(q * jnp.exp(bb)).astype(BF16), x=x,
                             r=jnp.concatenate([vpad, ones_rows], axis=1).astype(BF16))
    for u in units:
        o = ops[u]
        st = state_ref[u[0], u[1]]
        o["p"] = lax.dot_general(o["qp"], o["kst"], NT_DIMS, preferred_element_type=F32)
        o["inter"] = jnp.dot(o["qt"], st.astype(BF16), preferred_element_type=F32)
        me = lax.dot_general(o["x"], o["r"], TN_DIMS, preferred_element_type=F32)
        so_ref[u[0], u[1]] = me[:, DV_HGRN:] * st + me[:, :DV_HGRN]
    out_b = {}
    for u in units:
        o = ops[u]
        pm = jnp.where(prow >= pcol, o["p"], 0.0).astype(BF16)
        out = o["inter"] + jnp.dot(pm, o["v"], preferred_element_type=F32)
        gate = proj[u[0]][:, OFF_G + u[1] * DV_HGRN:OFF_G + (u[1] + 1) * DV_HGRN]
        out_b[u] = _hgrn_out(out, hng, gate)

    scores = {}
    for e in elems:
        p = proj[e]
        for kv in range(KV_HEADS):
            hs = slice(kv * HEAD_DIM, (kv + 1) * HEAD_DIM)
            q2 = jnp.concatenate([p[:, OFF_QA + h * HEAD_DIM:OFF_QA + (h + 1) * HEAD_DIM]
                                  for h in range(kv * GROUP, (kv + 1) * GROUP)], axis=0)
            q2 = (q2 * scale).astype(BF16)
            bias = bias_ref[kv]
            s_c = jnp.dot(q2, ck_ref[e, hs, :].astype(BF16), preferred_element_type=F32) + bias[:, 0:w_buf]
            s_n = (lax.dot_general(q2, p[:, OFF_KA + kv * HEAD_DIM:OFF_KA + (kv + 1) * HEAD_DIM].astype(BF16), NT_DIMS,
                                   preferred_element_type=F32) + bias[:, w_buf:w_buf + seq])
            scores[e, kv] = (s_c, s_n)
    out_c = {}
    for e in elems:
        p = proj[e]
        for kv in range(KV_HEADS):
            hs = slice(kv * HEAD_DIM, (kv + 1) * HEAD_DIM)
            sink = jnp.zeros((GROUP * seq, 1), F32)
            for gi in range(GROUP):
                sink = jnp.where(grow >= gi * seq, sinks_ref[kv * GROUP + gi], sink)
            s_c, s_n = scores[e, kv]
            m = jnp.maximum(jnp.maximum(jnp.max(s_c, axis=-1, keepdims=True), jnp.max(s_n, axis=-1, keepdims=True)),
                            sink)
            p_c = jnp.exp(s_c - m)
            p_n = jnp.exp(s_n - m)
            den = jnp.sum(p_c, axis=-1, keepdims=True) + jnp.sum(p_n, axis=-1, keepdims=True) + jnp.exp(sink - m)
            vnew = p[:, OFF_VA + kv * HEAD_DIM:OFF_VA + (kv + 1) * HEAD_DIM].astype(BF16)
            o2 = (lax.dot_general(p_c.astype(BF16), cv_ref[e, hs, :].astype(BF16), NT_DIMS,
                                  preferred_element_type=F32)
                  + jnp.dot(p_n.astype(BF16), vnew, preferred_element_type=F32)) / den
            for gi in range(GROUP):
                out_c[e, kv * GROUP + gi] = o2[gi * seq:(gi + 1) * seq]
    lane = lax.broadcasted_iota(jnp.int32, (D_KV, w_buf), 1)
    for e in elems:
        p = proj[e]
        for pad, cache, new, out in ((kpad_ref, ck_ref, p[:, OFF_KA:OFF_KA + D_KV], ko_ref),
                                     (vpad_ref, cv_ref, p[:, OFF_VA:OFF_VA + D_KV], vo_ref)):
            pad[e, w_buf - seq:w_buf, :] = new
            out[e] = jnp.where(lane >= w_buf - seq, pad[e].T, pltpu.roll(cache[e], w_buf - seq, 1))

    for e in elems:
        parts = [out_a[e]] + [out_b[e, h] for h in range(H_HGRN)] + [out_c[e, h] for h in range(H_ATTN)]
        mix_ref[e * seq:(e + 1) * seq, :] = jnp.concatenate(parts, axis=1)


def _sample_mixers(proj2, sinks, cache_conv, state, cache_k, cache_v, conv_w, conv_b, ln_g, ln_b, hgrn_lb, hng,
                   bias_s, layer, carried):
    B = state.shape[1]
    seq = proj2.shape[0] // B
    w_buf = cache_k.shape[3]
    block = SAMPLE_BLOCK
    depth = hgrn_lb.shape[0]
    hist = CONV_WIDTH - 1
    const2 = lambda i: (0, 0)
    cache_specs = [pl.BlockSpec((None, hist, block, D_CONV), lambda i: (layer, 0, i, 0)),
                   pl.BlockSpec((None, block, H_HGRN, DK_HGRN, DV_HGRN), lambda i: (layer, i, 0, 0, 0)),
                   pl.BlockSpec((None, block, D_KV, w_buf), lambda i: (layer, i, 0, 0)),
                   pl.BlockSpec((None, block, D_KV, w_buf), lambda i: (layer, i, 0, 0))]
    inputs = (sinks, proj2, cache_conv, state, cache_k, cache_v, conv_w, conv_b.reshape(1, D_CONV),
              ln_g.reshape(1, D_CONV), ln_b.reshape(1, D_CONV), hgrn_lb, hng.reshape(1, DV_HGRN), bias_s)
    return pl.pallas_call(
        _carried(functools.partial(_sample_mix_kernel, layer=layer, block=block, seq=seq, w_buf=w_buf),
                 len(inputs), len(carried)),
        out_shape=(jax.ShapeDtypeStruct((B * seq, D_MODEL), F32),
                   jax.ShapeDtypeStruct((depth, hist, B, D_CONV), F32),
                   jax.ShapeDtypeStruct((depth, B, H_HGRN, DK_HGRN, DV_HGRN), F32),
                   jax.ShapeDtypeStruct((depth, B, D_KV, w_buf), F32),
                   jax.ShapeDtypeStruct((depth, B, D_KV, w_buf), F32)),
        grid=(B // block,),
        in_specs=[pl.BlockSpec(memory_space=pltpu.SMEM),
                  pl.BlockSpec((block * seq, IN_WIDTH), lambda i: (i, 0))] + cache_specs + [
                  pl.BlockSpec((CONV_WIDTH, D_CONV), const2),
                  pl.BlockSpec((1, D_CONV), const2),
                  pl.BlockSpec((1, D_CONV), const2),
                  pl.BlockSpec((1, D_CONV), const2),
                  pl.BlockSpec((depth, D_HGRN), const2),
                  pl.BlockSpec((1, DV_HGRN), const2),
                  pl.BlockSpec((KV_HEADS, GROUP * seq, 2 * ATTN_BLOCK), lambda i: (0, 0, 0))] + _carry_specs(carried),
        out_specs=tuple([pl.BlockSpec((block * seq, D_MODEL), lambda i: (i, 0))] + cache_specs),
        input_output_aliases={len(inputs) + i: 1 + i for i in range(len(carried))},
        scratch_shapes=[pltpu.VMEM((block, w_buf, D_KV), F32),
                        pltpu.VMEM((block, w_buf, D_KV), F32)],
        compiler_params=pltpu.CompilerParams(dimension_semantics=("arbitrary",), vmem_limit_bytes=VMEM_LIMIT),
        name="sample_mixers",
    )(*inputs, *carried)


def kernel(x_prompt, x_sample, cache_conv, state_hgrn, cache_swa_k, cache_swa_v, c_prompt, c_sample, rel_bias, w_ada, b_ada, norm_mix_g, w_in, conv_w, conv_b, conv_ln_g, conv_ln_b, hgrn_lb, hgrn_norm_g, attn_sinks, w_out, norm_mlp_g, w_up, w_down, final_g):
    Bp, Tp = x_prompt.shape[:2]
    Bs, Ts = x_sample.shape[:2]
    depth = w_in.shape[0]
    w_buf = cache_swa_k.shape[2]
    assert Tp % MIX_TILE == 0 and (Bp * Tp) % TOK_TILE == 0 and Tp % TOK_TILE == 0 and Bs % SAMPLE_BLOCK == 0
    assert w_buf == WINDOW and GROUP * Ts == SUBLANES

    bias_p, bias_s = _bias_tables(rel_bias, Ts, w_buf)
    mod = _modulation(jnp.concatenate([c_sample, c_prompt], axis=0), w_ada, b_ada)
    hlb = hgrn_lb.astype(F32)
    cc = jnp.swapaxes(cache_conv, 1, 2)
    ck = jnp.swapaxes(cache_swa_k.reshape(depth, Bs, w_buf, D_KV), 2, 3)
    cv = jnp.swapaxes(cache_swa_v.reshape(depth, Bs, w_buf, D_KV), 2, 3)

    xp = x_prompt.reshape(Bp * Tp, D_MODEL)
    xs = x_sample.reshape(Bs * Ts, D_MODEL)
    caches_p = ()
    caches_s = ()
    for l in range(depth):
        final = l == depth - 1
        mod_p = mod[l, Bs:].reshape(Bp, 1, N_MOD * D_MODEL)
        mix_p, proj_s, *caches_p = _prompt_mixers(xp.reshape(Bp, Tp, D_MODEL), mod_p, xs, mod, Bs, norm_mix_g[l], w_in,
                                                  attn_sinks[l], conv_w[l], conv_b[l], conv_ln_g[l], conv_ln_b[l], hlb,
                                                  hgrn_norm_g[l], bias_p, l, caches_p)
        mix_s, *caches_s = _sample_mixers(proj_s, attn_sinks[l], cc, state_hgrn, ck, cv, conv_w[l],
                                          conv_b[l], conv_ln_g[l], conv_ln_b[l], hlb, hgrn_norm_g[l], bias_s, l,
                                          caches_s)
        xp, xs = _out_mlp(mix_p.reshape(Bp * Tp, D_MODEL), xp, mod_p, mix_s, xs, mod, Bs, norm_mlp_g[l], w_out,
                          w_up, w_down, final_g, l, Tp // TOK_TILE, final)
    cp, sp, kp, vp = caches_p
    cs, ss, ksn, vsn = caches_s
    cs, ksn, vsn = jnp.swapaxes(cs, 1, 2), jnp.swapaxes(ksn, 2, 3), jnp.swapaxes(vsn, 2, 3)
    return (xp.reshape(Bp, Tp, D_MODEL), xs.reshape(Bs, Ts, D_MODEL), cp, cs, sp, ss,
            kp.reshape(depth, Bp, WINDOW, KV_HEADS, HEAD_DIM), ksn.reshape(depth, Bs, w_buf, KV_HEADS, HEAD_DIM),
            vp.reshape(depth, Bp, WINDOW, KV_HEADS, HEAD_DIM), vsn.reshape(depth, Bs, w_buf, KV_HEADS, HEAD_DIM))
```

```python
import functools
import math

import jax
import jax.numpy as jnp
from jax import lax
from jax.experimental import pallas as pl
from jax.experimental.pallas import tpu as pltpu

F32 = jnp.float32
BF16 = jnp.bfloat16

D_MODEL = 1024
D_CONV = 256
CONV_WIDTH = 31
H_HGRN = 4
DK_HGRN = 128
DV_HGRN = 128
D_HGRN = 512
HEAD_DIM = 64
H_ATTN = 4
KV_HEADS = 2
GROUP = H_ATTN // KV_HEADS
D_ATTN = H_ATTN * HEAD_DIM
D_KV = KV_HEADS * HEAD_DIM
WINDOW = 128
ATTN_BLOCK = 128
NUM_BUCKETS = 32
MAX_DISTANCE = 128
D_FF = 4 * D_MODEL
N_MOD = 6
EPS = 1e-6

OFF_AVAL = 0
OFF_AGATE = OFF_AVAL + D_CONV
OFF_Q = OFF_AGATE + D_CONV
OFF_F = OFF_Q + H_HGRN * DK_HGRN
OFF_I = OFF_F + H_HGRN * DK_HGRN
OFF_G = OFF_I + D_HGRN
OFF_QA = OFF_G + D_HGRN
OFF_KA = OFF_QA + D_ATTN
OFF_VA = OFF_KA + D_KV
IN_WIDTH = OFF_VA + D_KV

HGRN_CHUNK = 64
HGRN_KEYBLOCK = 32
HGRN_SPAN = 4
SUBLANES = 8
CONV_PAD = 32
MIX_TILE = 512
TOK_TILE = 512
SAMPLE_BLOCK = 16
FF_CHUNK = 1024
CAST_STEPS = 8
MOD_COLS = 2048
VMEM_LIMIT = 56 * 1024 * 1024

NT_DIMS = (((1,), (1,)), ((), ()))
TN_DIMS = (((0,), (0,)), ((), ()))


def _silu(x):
    return x * jax.nn.sigmoid(x)


def _rms_rows(x):
    return x * lax.rsqrt(jnp.mean(x * x, axis=-1, keepdims=True) + EPS)


def _layer_lb(hlb, layer):
    m = jnp.max(hlb, axis=0, keepdims=True)
    e = jnp.exp(hlb - m)
    p = e / jnp.sum(e, axis=0, keepdims=True)
    lb = jnp.zeros_like(m)
    for i in range(1, layer + 1):
        lb = lb + p[i:i + 1, :]
    return lb


def _split3_bf16(x):
    hi = x.astype(BF16)
    r = x - hi.astype(F32)
    mid = r.astype(BF16)
    return hi, mid, (r - mid.astype(F32)).astype(BF16)


def _select_rows_mxu(sel, x):
    return sum(jnp.dot(sel, part, preferred_element_type=F32) for part in _split3_bf16(x))


def _cumsum_rows_small(g):
    row = lax.broadcasted_iota(jnp.int32, g.shape, 0)
    b = jnp.zeros_like(g)
    for u in range(g.shape[0]):
        b = b + jnp.where(row >= u, g[u:u + 1, :], 0.0)
    return b


def _hgrn_span(proj_ref, mix_ref, st_ref, row0, lb, hng, tri, tick):
    L, KB = HGRN_CHUNK, HGRN_KEYBLOCK
    span = HGRN_SPAN * L
    g, k = _hgrn_gates(proj_ref[pl.ds(row0, span), OFF_F:OFF_F + D_HGRN], lb)
    b = _select_rows_mxu(tri, g)
    units = [(c, h) for c in range(HGRN_SPAN) for h in range(H_HGRN)]

    ops = {}
    for c, h in units:
        rows = pl.ds(row0 + c * L, L)
        cs = slice(h * DK_HGRN, (h + 1) * DK_HGRN)
        q = proj_ref[rows, OFF_Q + h * DK_HGRN:OFF_Q + (h + 1) * DK_HGRN]
        v = proj_ref[rows, OFF_I + h * DV_HGRN:OFF_I + (h + 1) * DV_HGRN].astype(BF16)
        kk = k[c * L:(c + 1) * L, cs]
        bb = b[c * L:(c + 1) * L, cs]
        qp, kp = [], []
        for lo in range(0, L, KB):
            r = bb[lo + KB // 2 - 1:lo + KB // 2, :]
            kp.append((kk[lo:lo + KB] * jnp.exp(r - bb[lo:lo + KB])).astype(BF16))
            qp.append((q[lo:] * jnp.exp(bb[lo:] - r)).astype(BF16))
        bl = bb[L - 1:L, :]
        ops[c, h] = dict(qp=qp, kp=kp, v=v, qt=(q * jnp.exp(bb)).astype(BF16),
                         kst=(kk * jnp.exp(bl - bb)).astype(BF16), e=jnp.exp(bl))
    tick()

    for u in units:
        o = ops[u]
        o["p"] = [lax.dot_general(qp, kp, NT_DIMS, preferred_element_type=F32) for qp, kp in zip(o["qp"], o["kp"])]
        o["m"] = lax.dot_general(o["v"], o["kst"], TN_DIMS, preferred_element_type=F32)
    tick()

    for u in units:
        pm = []
        for p in ops[u]["p"]:
            row = lax.broadcasted_iota(jnp.int32, p.shape, 0)
            col = lax.broadcasted_iota(jnp.int32, p.shape, 1)
            pm.append(jnp.where(row >= col, p, 0.0).astype(BF16))
        ops[u]["p"] = pm
    tick()

    for u in units:
        o = ops[u]
        blocks = [None] * (L // KB)
        for j, p in enumerate(o["p"]):
            cj = jnp.dot(p, o["v"][j * KB:(j + 1) * KB], preferred_element_type=F32)
            for i in range(j, L // KB):
                piece = cj[(i - j) * KB:(i - j + 1) * KB]
                blocks[i] = piece if blocks[i] is None else blocks[i] + piece
        o["o"] = jnp.concatenate(blocks, axis=0)
    tick()

    for h in range(H_HGRN):
        st = st_ref[h]
        for c in range(HGRN_SPAN):
            o = ops[c, h]
            out = o["o"] + lax.dot_general(o["qt"], st.astype(BF16), NT_DIMS, preferred_element_type=F32)
            st = o["e"] * st + o["m"]
            rows = pl.ds(row0 + c * L, L)
            gate = proj_ref[rows, OFF_G + h * DV_HGRN:OFF_G + (h + 1) * DV_HGRN]
            mix_ref[rows, D_CONV + h * DV_HGRN:D_CONV + (h + 1) * DV_HGRN] = _hgrn_out(out, hng, gate).astype(BF16)
        st_ref[h] = st


def _sink_softmax(s, sink):
    m = jnp.maximum(jnp.max(s, axis=-1, keepdims=True), sink)
    p = jnp.exp(s - m)
    return p, jnp.sum(p, axis=-1, keepdims=True) + jnp.exp(sink - m)


def _bias_kernel(tab_ref, bp_ref, bs_ref, op_ref, os_ref, *, seq):
    bk = bp_ref[...]
    for h in range(H_ATTN):
        acc = jnp.full(bk.shape, -jnp.inf, F32)
        for bkt in range(NUM_BUCKETS):
            acc = jnp.where(bk == bkt, tab_ref[bkt, h], acc)
        op_ref[h] = acc
    bk = bs_ref[...]
    row = lax.broadcasted_iota(jnp.int32, bk.shape, 0)
    for kv in range(KV_HEADS):
        acc = jnp.full(bk.shape, -jnp.inf, F32)
        for bkt in range(NUM_BUCKETS):
            val = jnp.full(bk.shape, tab_ref[bkt, kv * GROUP], F32)
            for gi in range(1, GROUP):
                val = jnp.where(row >= gi * seq, tab_ref[bkt, kv * GROUP + gi], val)
            acc = jnp.where(bk == bkt, val, acc)
        os_ref[kv] = acc


def _t5_bucket(rel):
    n = jnp.maximum(rel, 0)
    max_exact = NUM_BUCKETS // 2
    nf = jnp.maximum(n, max_exact).astype(F32)
    large = max_exact + (jnp.log(nf / max_exact) / math.log(MAX_DISTANCE / max_exact)
                         * (NUM_BUCKETS - max_exact)).astype(jnp.int32)
    large = jnp.minimum(large, NUM_BUCKETS - 1)
    return jnp.where(n < max_exact, n, large)


def _bias_tables(rel_bias, dec_seq, w_buf):
    qi = jnp.arange(ATTN_BLOCK, dtype=jnp.int32)[:, None]
    kc = jnp.arange(2 * ATTN_BLOCK, dtype=jnp.int32)[None, :]
    rel_p = qi + ATTN_BLOCK - kc
    bucket_p = jnp.where((rel_p >= 0) & (rel_p <= WINDOW), _t5_bucket(rel_p), -1)
    ts = (jnp.arange(GROUP * dec_seq, dtype=jnp.int32) % dec_seq)[:, None]
    js = jnp.arange(2 * ATTN_BLOCK, dtype=jnp.int32)[None, :]
    rel_s = w_buf + ts - js
    ok_s = (rel_s >= 0) & (rel_s <= WINDOW) & (js < w_buf + dec_seq)
    bucket_s = jnp.where(ok_s, _t5_bucket(rel_s), -1)
    return pl.pallas_call(
        functools.partial(_bias_kernel, seq=dec_seq),
        out_shape=(jax.ShapeDtypeStruct((H_ATTN, ATTN_BLOCK, 2 * ATTN_BLOCK), F32),
                   jax.ShapeDtypeStruct((KV_HEADS, GROUP * dec_seq, 2 * ATTN_BLOCK), F32)),
        in_specs=[pl.BlockSpec(memory_space=pltpu.SMEM),
                  pl.BlockSpec(memory_space=pltpu.VMEM),
                  pl.BlockSpec(memory_space=pltpu.VMEM)],
        out_specs=(pl.BlockSpec(memory_space=pltpu.VMEM), pl.BlockSpec(memory_space=pltpu.VMEM)),
        name="rel_bias_tables",
    )(rel_bias.astype(F32), bucket_p, bucket_s)


def _mod_kernel(c_ref, w_ref, b_ref, o_ref):
    s = _silu(c_ref[...]).astype(BF16)
    o_ref[...] = jnp.dot(s, w_ref[...].astype(BF16), preferred_element_type=F32) + b_ref[...]


def _modulation(c_all, w_ada, b_ada):
    depth = w_ada.shape[0]
    n = c_all.shape[0]
    return pl.pallas_call(
        _mod_kernel,
        out_shape=jax.ShapeDtypeStruct((depth, n, N_MOD * D_MODEL), F32),
        grid=(depth, N_MOD * D_MODEL // MOD_COLS),
        in_specs=[pl.BlockSpec((n, D_MODEL), lambda l, j: (0, 0)),
                  pl.BlockSpec((None, D_MODEL, MOD_COLS), lambda l, j: (l, 0, j)),
                  pl.BlockSpec((None, 1, MOD_COLS), lambda l, j: (l, 0, j))],
        out_specs=pl.BlockSpec((None, n, MOD_COLS), lambda l, j: (l, 0, j)),
        compiler_params=pltpu.CompilerParams(dimension_semantics=("arbitrary", "arbitrary"),
                                             vmem_limit_bytes=VMEM_LIMIT),
        name="adaln_modulation",
    )(c_all, w_ada, b_ada.reshape(depth, 1, N_MOD * D_MODEL))


def _mod_rows(m, n_tokens):
    if m.shape[0] == 1:
        return m
    reps = n_tokens // m.shape[0]
    tok = lax.broadcasted_iota(jnp.int32, (n_tokens, m.shape[0]), 0)
    bat = lax.broadcasted_iota(jnp.int32, (n_tokens, m.shape[0]), 1)
    sel = jnp.where((tok >= bat * reps) & (tok < (bat + 1) * reps), 1.0, 0.0).astype(BF16)
    return _select_rows_mxu(sel, m)


def _modulated_norm(x, g, sc, sh):
    n = x.shape[0]
    return (_rms_rows(x) * (g * (1.0 + _mod_rows(sc, n))) + _mod_rows(sh, n)).astype(BF16)


def _sample_mod_specs(n_rows, chunks, index_map):
    return [pl.BlockSpec((None, n_rows, D_MODEL), functools.partial(index_map, chunk=c),
                         pipeline_mode=pl.Buffered(1)) for c in chunks]


def _mlp_tile(mix, x, g1, sh, sc, g2, ng, wout_ref, wup_ref, wdn_ref, fg, final):
    n = x.shape[0]
    x1 = x + _mod_rows(g1, n) * jnp.dot(mix.astype(BF16), wout_ref[...], preferred_element_type=F32)
    h = _modulated_norm(x1, ng, sc, sh)
    acc = None
    for c in range(D_FF // FF_CHUNK):
        u = jnp.dot(h, wup_ref[:, c * FF_CHUNK:(c + 1) * FF_CHUNK], preferred_element_type=F32)
        u = jnp.square(jnp.maximum(u, 0.0)).astype(BF16)
        d = jnp.dot(u, wdn_ref[c * FF_CHUNK:(c + 1) * FF_CHUNK, :], preferred_element_type=F32)
        acc = d if acc is None else acc + d
    x2 = x1 + _mod_rows(g2, n) * acc
    return _rms_rows(x2) * fg if final else x2


def _mlp_kernel(mix_ref, x_ref, g1_ref, sh_ref, sc_ref, g2_ref, smix_ref, sx_ref, sg1_ref, ssh_ref, ssc_ref, sg2_ref,
                ng_ref, wout_ref, wup_ref, wdn_ref, fg_ref, o_ref, so_ref, wout_s, wup_s, wdn_s, *, layer, final):
    s = pl.program_id(0)

    @pl.when(s < CAST_STEPS)
    def _():
        for src, dst in ((wout_ref, wout_s), (wup_ref, wup_s), (wdn_ref, wdn_s)):
            rows = src.shape[0]
            dst[pl.ds(pl.multiple_of(s * rows, rows), rows), :] = src[...].astype(BF16)

    weights = (ng_ref[layer:layer + 1, :], wout_s, wup_s, wdn_s, fg_ref[...], final)

    @pl.when(s >= CAST_STEPS)
    def _():
        o_ref[...] = _mlp_tile(mix_ref[...], x_ref[...], g1_ref[...], sh_ref[...], sc_ref[...], g2_ref[...], *weights)

    @pl.when(s == pl.num_programs(0) - 1)
    def _():
        so_ref[...] = _mlp_tile(smix_ref[...], sx_ref[...], sg1_ref[...], ssh_ref[...], ssc_ref[...], sg2_ref[...],
                                *weights)


def _out_mlp(mix2, x2, mod, mix_s, xs, mod_all, n_sample, norm_g, w_out, w_up, w_down, final_g, layer,
             tiles_per_batch, final):
    n = x2.shape[0]
    tile = TOK_TILE
    const = lambda i: (0, 0)
    tok = lambda i: (jnp.maximum(i - CAST_STEPS, 0), 0)
    chunk_of_layer = lambda i: (layer, jnp.minimum(i, CAST_STEPS - 1), 0)
    mod_p = lambda chunk: pl.BlockSpec((None, 1, D_MODEL),
                                       lambda i: (jnp.maximum(i - CAST_STEPS, 0) // tiles_per_batch, 0, chunk))
    whole = lambda a: pl.BlockSpec(a.shape, const, pipeline_mode=pl.Buffered(1))
    return pl.pallas_call(
        functools.partial(_mlp_kernel, layer=layer, final=final),
        out_shape=(jax.ShapeDtypeStruct((n, D_MODEL), F32), jax.ShapeDtypeStruct(xs.shape, F32)),
        grid=(CAST_STEPS + n // tile,),
        in_specs=[pl.BlockSpec((tile, D_MODEL), tok),
                  pl.BlockSpec((tile, D_MODEL), tok),
                  mod_p(2), mod_p(3), mod_p(4), mod_p(5),
                  whole(mix_s), whole(xs)]
                 + _sample_mod_specs(n_sample, (2, 3, 4, 5), lambda i, chunk: (layer, 0, chunk)) + [
                  pl.BlockSpec(norm_g.shape, const),
                  pl.BlockSpec((None, D_MODEL // CAST_STEPS, D_MODEL), chunk_of_layer),
                  pl.BlockSpec((None, D_MODEL // CAST_STEPS, D_FF), chunk_of_layer),
                  pl.BlockSpec((None, D_FF // CAST_STEPS, D_MODEL), chunk_of_layer),
                  pl.BlockSpec((1, D_MODEL), const)],
        out_specs=(pl.BlockSpec((tile, D_MODEL), tok), pl.BlockSpec(xs.shape, const, pipeline_mode=pl.Buffered(1))),
        scratch_shapes=[pltpu.VMEM((D_MODEL, D_MODEL), BF16),
                        pltpu.VMEM((D_MODEL, D_FF), BF16),
                        pltpu.VMEM((D_FF, D_MODEL), BF16)],
        compiler_params=pltpu.CompilerParams(dimension_semantics=("arbitrary",), vmem_limit_bytes=VMEM_LIMIT),
        name="out_projection_mlp",
    )(mix2, x2, mod, mod, mod, mod, mix_s, xs, mod_all, mod_all, mod_all, mod_all, norm_g, w_out,
      w_up, w_down, final_g.reshape(1, D_MODEL))


def _conv_ln_swish(acc, lng, lnb):
    mu = jnp.mean(acc, axis=-1, keepdims=True)
    xc = acc - mu
    y = xc * lax.rsqrt(jnp.mean(xc * xc, axis=-1, keepdims=True) + EPS) * lng + lnb
    return _silu(y)


def _hgrn_gates(fh, lb):
    f = lb + (1.0 - lb) * jax.nn.sigmoid(fh)
    return jnp.log(f), 1.0 - f


def _hgrn_out(o, hng, gate):
    return _rms_rows(o) * hng * _silu(gate)


def _ticker(pieces):
    it = iter(pieces)

    def tick():
        piece = next(it, None)
        if piece is not None:
            piece()

    def flush():
        for piece in it:
            piece()

    tick.flush = flush
    return tick


def _prompt_mix_kernel(sinks_ref, x_ref, sh_ref, sc_ref, sx_ref, ssh_ref, ssc_ref, ng_ref, win_ref, convw_ref,
                       convb_ref, lng_ref, lnb_ref, hlb_ref, hng_ref, bias_ref,
                       mix_ref, sproj_ref, convo_ref, so_ref, ko_ref, vo_ref,
                       proj_ref, wbf_ref, abuf, kbuf, vbuf, st_ref, *, layer, tile):
    t = pl.program_id(1)
    last = pl.num_programs(1) - 1

    @pl.when((pl.program_id(0) == 0) & (t == 0))
    def _():
        wbf_ref[...] = win_ref[...].astype(BF16)

    @pl.when(t == 0)
    def _():
        abuf[0:CONV_PAD, :] = jnp.zeros((CONV_PAD, D_CONV), F32)
        abuf[CONV_PAD + tile:CONV_PAD + tile + SUBLANES, :] = jnp.zeros((SUBLANES, D_CONV), F32)
        kbuf[0:ATTN_BLOCK, :] = jnp.zeros((ATTN_BLOCK, D_KV), BF16)
        vbuf[0:ATTN_BLOCK, :] = jnp.zeros((ATTN_BLOCK, D_KV), BF16)
        st_ref[...] = jnp.zeros(st_ref.shape, F32)

    h_in = _modulated_norm(x_ref[...], ng_ref[layer:layer + 1, :], sc_ref[...], sh_ref[...])
    for lo, hi in ((OFF_AVAL, OFF_Q), (OFF_F, OFF_I), (OFF_Q, OFF_F), (OFF_I, OFF_G), (OFF_G, OFF_QA),
                   (OFF_QA, IN_WIDTH)):
        proj_ref[:, lo:hi] = jnp.dot(h_in, wbf_ref[:, lo:hi], preferred_element_type=F32)

    kbuf[ATTN_BLOCK:ATTN_BLOCK + tile, :] = proj_ref[:, OFF_KA:OFF_KA + D_KV].astype(BF16)
    vbuf[ATTN_BLOCK:ATTN_BLOCK + tile, :] = proj_ref[:, OFF_VA:OFF_VA + D_KV].astype(BF16)
    scale = HEAD_DIM ** -0.5
    attn = {}

    def attn_scores(blk):
        def run():
            r0 = blk * ATTN_BLOCK
            for h in range(H_ATTN):
                kv = h // GROUP
                q = (proj_ref[r0:r0 + ATTN_BLOCK, OFF_QA + h * HEAD_DIM:OFF_QA + (h + 1) * HEAD_DIM]
                     * scale).astype(BF16)
                kall = kbuf[r0:r0 + 2 * ATTN_BLOCK, kv * HEAD_DIM:(kv + 1) * HEAD_DIM]
                attn[blk, h] = lax.dot_general(q, kall, NT_DIMS, preferred_element_type=F32)
        return run

    def attn_softmax(blk):
        def run():
            for h in range(H_ATTN):
                s = attn[blk, h] + bias_ref[h]
                if blk == 0:
                    col = lax.broadcasted_iota(jnp.int32, s.shape, 1)
                    s = jnp.where(col + (t * tile - ATTN_BLOCK) >= 0, s, -jnp.inf)
                p, den = _sink_softmax(s, sinks_ref[layer, h])
                attn[blk, h] = (p.astype(BF16), den)
        return run

    def attn_values(blk):
        def run():
            r0 = blk * ATTN_BLOCK
            heads = []
            for h in range(H_ATTN):
                kv = h // GROUP
                p, den = attn[blk, h]
                vall = vbuf[r0:r0 + 2 * ATTN_BLOCK, kv * HEAD_DIM:(kv + 1) * HEAD_DIM]
                heads.append(jnp.dot(p, vall, preferred_element_type=F32) / den)
            mix_ref[r0:r0 + ATTN_BLOCK, D_CONV + D_HGRN:D_MODEL] = jnp.concatenate(heads, axis=1).astype(BF16)
        return run

    tick = _ticker([stage(blk) for blk in range(tile // ATTN_BLOCK)
                    for stage in (attn_scores, attn_softmax, attn_values)])

    abuf[CONV_PAD:CONV_PAD + tile, :] = (proj_ref[:, OFF_AVAL:OFF_AVAL + D_CONV]
                                         * jax.nn.sigmoid(proj_ref[:, OFF_AGATE:OFF_AGATE + D_CONV]))
    first_row = CONV_PAD - (CONV_WIDTH - 1)
    acc = jnp.broadcast_to(convb_ref[layer:layer + 1, :], (tile, D_CONV))
    for r in range(SUBLANES):
        z = None
        for off in range(r, first_row + CONV_WIDTH, SUBLANES):
            j = off - first_row
            if j < 0:
                continue
            term = convw_ref[layer, j:j + 1, :] * abuf[off - r:off - r + tile + SUBLANES, :]
            z = term if z is None else z + term
        acc = acc + (z[0:tile] if r == 0 else pltpu.roll(z, tile + SUBLANES - r, 0)[0:tile])
    mix_ref[:, 0:D_CONV] = _conv_ln_swish(acc, lng_ref[layer:layer + 1, :], lnb_ref[layer:layer + 1, :]).astype(BF16)
    tick()

    lb = _layer_lb(hlb_ref[...], layer)
    hng = hng_ref[layer:layer + 1, :]
    span = HGRN_SPAN * HGRN_CHUNK
    ri = lax.broadcasted_iota(jnp.int32, (span, span), 0)
    ci = lax.broadcasted_iota(jnp.int32, (span, span), 1)
    tri = jnp.where((ri >= ci) & (ri // HGRN_CHUNK == ci // HGRN_CHUNK), 1.0, 0.0).astype(BF16)
    for i in range(tile // span):
        _hgrn_span(proj_ref, mix_ref, st_ref, i * span, lb, hng, tri, tick)
    tick.flush()

    @pl.when(t == last)
    def _():
        convo_ref[...] = abuf[CONV_PAD + tile - (CONV_WIDTH - 1):CONV_PAD + tile, :]
        for h in range(H_HGRN):
            so_ref[h] = st_ref[h].T
        ko_ref[...] = proj_ref[tile - WINDOW:tile, OFF_KA:OFF_KA + D_KV]
        vo_ref[...] = proj_ref[tile - WINDOW:tile, OFF_VA:OFF_VA + D_KV]

    abuf[0:CONV_PAD, :] = abuf[tile:tile + CONV_PAD, :]
    kbuf[0:ATTN_BLOCK, :] = kbuf[tile:tile + ATTN_BLOCK, :]
    vbuf[0:ATTN_BLOCK, :] = vbuf[tile:tile + ATTN_BLOCK, :]

    @pl.when((pl.program_id(0) == pl.num_programs(0) - 1) & (t == last))
    def _():
        h_s = _modulated_norm(sx_ref[...], ng_ref[layer:layer + 1, :], ssc_ref[...], ssh_ref[...])
        sproj_ref[...] = jnp.dot(h_s, wbf_ref[...], preferred_element_type=F32)


def _carry_specs(carried):
    return [pl.BlockSpec(memory_space=pl.ANY)] * len(carried)


def _without_carry(kernel_fn, n_in, n_carried, *refs):
    return kernel_fn(*refs[:n_in], *refs[n_in + n_carried:])


def _carried(kernel_fn, n_in, n_carried):
    return functools.partial(_without_carry, kernel_fn, n_in, n_carried)


def _prompt_mixers(x, mod, xs, mod_all, n_sample, norm_g, w_in, sinks, conv_w, conv_b, ln_g, ln_b, hgrn_lb, hng,
                   bias_p, layer, carried):
    B, T = x.shape[:2]
    tile = MIX_TILE
    depth = hgrn_lb.shape[0]
    const2 = lambda b, t: (0, 0)
    inputs = (sinks, x, mod, mod, xs, mod_all, mod_all, norm_g, w_in, conv_w, conv_b, ln_g, ln_b, hgrn_lb, hng, bias_p)
    return pl.pallas_call(
        _carried(functools.partial(_prompt_mix_kernel, layer=layer, tile=tile), len(inputs), len(carried)),
        out_shape=(jax.ShapeDtypeStruct((B, T, D_MODEL), BF16),
                   jax.ShapeDtypeStruct((xs.shape[0], IN_WIDTH), F32),
                   jax.ShapeDtypeStruct((depth, B, CONV_WIDTH - 1, D_CONV), F32),
                   jax.ShapeDtypeStruct((depth, B, H_HGRN, DK_HGRN, DV_HGRN), F32),
                   jax.ShapeDtypeStruct((depth, B, WINDOW, D_KV), F32),
                   jax.ShapeDtypeStruct((depth, B, WINDOW, D_KV), F32)),
        grid=(B, T // tile),
        in_specs=[pl.BlockSpec(memory_space=pltpu.SMEM),
                  pl.BlockSpec((None, tile, D_MODEL), lambda b, t: (b, t, 0)),
                  pl.BlockSpec((None, 1, D_MODEL), lambda b, t: (b, 0, 0)),
                  pl.BlockSpec((None, 1, D_MODEL), lambda b, t: (b, 0, 1)),
                  pl.BlockSpec(xs.shape, const2, pipeline_mode=pl.Buffered(1))]
                 + _sample_mod_specs(n_sample, (0, 1), lambda b, t, chunk: (layer, 0, chunk)) + [
                  pl.BlockSpec(norm_g.shape, const2),
                  pl.BlockSpec((None, D_MODEL, IN_WIDTH), lambda b, t: (layer, 0, 0), pipeline_mode=pl.Buffered(1)),
                  pl.BlockSpec(conv_w.shape, lambda b, t: (0, 0, 0)),
                  pl.BlockSpec(conv_b.shape, const2),
                  pl.BlockSpec(ln_g.shape, const2),
                  pl.BlockSpec(ln_b.shape, const2),
                  pl.BlockSpec(hgrn_lb.shape, const2),
                  pl.BlockSpec(hng.shape, const2),
                  pl.BlockSpec((H_ATTN, ATTN_BLOCK, 2 * ATTN_BLOCK), lambda b, t: (0, 0, 0))] + _carry_specs(carried),
        out_specs=(pl.BlockSpec((None, tile, D_MODEL), lambda b, t: (b, t, 0)),
                   pl.BlockSpec((xs.shape[0], IN_WIDTH), const2),
                   pl.BlockSpec((None, None, CONV_WIDTH - 1, D_CONV), lambda b, t: (layer, b, 0, 0)),
                   pl.BlockSpec((None, None, H_HGRN, DK_HGRN, DV_HGRN), lambda b, t: (layer, b, 0, 0, 0)),
                   pl.BlockSpec((None, None, WINDOW, D_KV), lambda b, t: (layer, b, 0, 0)),
                   pl.BlockSpec((None, None, WINDOW, D_KV), lambda b, t: (layer, b, 0, 0))),
        input_output_aliases={len(inputs) + i: 2 + i for i in range(len(carried))},
        scratch_shapes=[pltpu.VMEM((tile, IN_WIDTH), F32),
                        pltpu.VMEM((D_MODEL, IN_WIDTH), BF16),
                        pltpu.VMEM((CONV_PAD + tile + SUBLANES, D_CONV), F32),
                        pltpu.VMEM((ATTN_BLOCK + tile, D_KV), BF16),
                        pltpu.VMEM((ATTN_BLOCK + tile, D_KV), BF16),
                        pltpu.VMEM((H_HGRN, DV_HGRN, DK_HGRN), F32)],
        compiler_params=pltpu.CompilerParams(dimension_semantics=("arbitrary", "arbitrary"),
                                             vmem_limit_bytes=VMEM_LIMIT),
        name="prompt_mixers",
    )(*inputs, *carried)


def _sample_mix_kernel(sinks_ref, proj_ref, cconv_ref, state_ref, ck_ref, cv_ref, convw_ref, convb_ref, lng_ref,
                       lnb_ref, hlb_ref, hng_ref, bias_ref,
                       mix_ref, convo_ref, so_ref, ko_ref, vo_ref, kpad_ref, vpad_ref, *,
                       layer, block, seq, w_buf):
    hist = CONV_WIDTH - 1

    @pl.when(pl.program_id(0) == 0)
    def _():
        for ref in (kpad_ref, vpad_ref):
            ref[:, 0:w_buf - seq, :] = jnp.zeros((block, w_buf - seq, D_KV), F32)

    lb = _layer_lb(hlb_ref[...], layer)
    hng = hng_ref[layer:layer + 1, :]
    scale = HEAD_DIM ** -0.5
    elems = range(block)
    row8 = lax.broadcasted_iota(jnp.int32, (SUBLANES, DV_HGRN), 0)
    ones_rows = jnp.where((row8 >= seq) & (row8 < seq + 3), 1.0, 0.0)
    zrow = jnp.zeros((1, DK_HGRN), BF16)
    prow = lax.broadcasted_iota(jnp.int32, (seq, seq), 0)
    pcol = lax.broadcasted_iota(jnp.int32, (seq, seq), 1)
    grow = lax.broadcasted_iota(jnp.int32, (GROUP * seq, 1), 0)

    proj = [proj_ref[e * seq:(e + 1) * seq, :] for e in elems]

    glu = [p[:, OFF_AVAL:OFF_AVAL + D_CONV] * jax.nn.sigmoid(p[:, OFF_AGATE:OFF_AGATE + D_CONV]) for p in proj]
    full = [cconv_ref[i] for i in range(hist)]
    full += [jnp.concatenate([glu[e][t:t + 1] for e in elems], axis=0) for t in range(seq)]
    for i in range(hist):
        convo_ref[i] = full[i + seq]
    conv_out = []
    for t in range(seq):
        acc = jnp.broadcast_to(convb_ref[layer:layer + 1, :], (block, D_CONV))
        for j in range(CONV_WIDTH):
            acc = acc + convw_ref[layer, j:j + 1, :] * full[t + j]
        conv_out.append(_conv_ln_swish(acc, lng_ref[layer:layer + 1, :], lnb_ref[layer:layer + 1, :]))
    out_a = [jnp.concatenate([conv_out[t][e:e + 1] for t in range(seq)], axis=0) for e in elems]

    units = [(e, h) for e in elems for h in range(H_HGRN)]
    ops = {}
    for e in elems:
        p = proj[e]
        g, k = _hgrn_gates(p[:, OFF_F:OFF_F + D_HGRN], lb)
        b = _cumsum_rows_small(g)
        for h in range(H_HGRN):
            cs = slice(h * DK_HGRN, (h + 1) * DK_HGRN)
            q = p[:, OFF_Q + h * DK_HGRN:OFF_Q + (h + 1) * DK_HGRN]
            v = p[:, OFF_I + h * DV_HGRN:OFF_I + (h + 1) * DV_HGRN]
            bb = b[:, cs]
            bl = bb[seq - 1:seq, :]
            kst = (k[:, cs] * jnp.exp(bl - bb)).astype(BF16)
            x = jnp.concatenate([kst.astype(F32), *(part.astype(F32) for part in _split3_bf16(jnp.exp(bl))),
                                 zrow.astype(F32)], axis=0).astype(BF16)
            vpad = jnp.concatenate([v, jnp.zeros((SUBLANES - seq, DV_HGRN), F32)], axis=0)
            ops[e, h] = dict(qp=(q * jnp.exp(bb - bl)).astype(BF16), kst=kst, v=v.astype(BF16),
                             qt=(q * jnp.exp(bb)).astype(BF16), x=x,
                             r=jnp.concatenate([vpad, ones_rows], axis=1).astype(BF16))
    for u in units:
        o = ops[u]
        st = state_ref[u[0], u[1]]
        o["p"] = lax.dot_general(o["qp"], o["kst"], NT_DIMS, preferred_element_type=F32)
        o["inter"] = jnp.dot(o["qt"], st.astype(BF16), preferred_element_type=F32)
        me = lax.dot_general(o["x"], o["r"], TN_DIMS, preferred_element_type=F32)
        so_ref[u[0], u[1]] = me[:, DV_HGRN:] * st + me[:, :DV_HGRN]
    out_b = {}
    for u in units:
        o = ops[u]
        pm = jnp.where(prow >= pcol, o["p"], 0.0).astype(BF16)
        out = o["inter"] + jnp.dot(pm, o["v"], preferred_element_type=F32)
        gate = proj[u[0]][:, OFF_G + u[1] * DV_HGRN:OFF_G + (u[1] + 1) * DV_HGRN]
        out_b[u] = _hgrn_out(out, hng, gate)

    scores = {}
    for e in elems:
        p = proj[e]
        for kv in range(KV_HEADS):
            hs = slice(kv * HEAD_DIM, (kv + 1) * HEAD_DIM)
            q2 = jnp.concatenate([p[:, OFF_QA + h * HEAD_DIM:OFF_QA + (h + 1) * HEAD_DIM]
                                  for h in range(kv * GROUP, (kv + 1) * GROUP)], axis=0)
            q2 = (q2 * scale).astype(BF16)
            bias = bias_ref[kv]
            s_c = jnp.dot(q2, ck_ref[e, hs, :].astype(BF16), preferred_element_type=F32) + bias[:, 0:w_buf]
            s_n = (lax.dot_general(q2, p[:, OFF_KA + kv * HEAD_DIM:OFF_KA + (kv + 1) * HEAD_DIM].astype(BF16), NT_DIMS,
                                   preferred_element_type=F32) + bias[:, w_buf:w_buf + seq])
            scores[e, kv] = (s_c, s_n)
    out_c = {}
    for e in elems:
        p = proj[e]
        for kv in range(KV_HEADS):
            hs = slice(kv * HEAD_DIM, (kv + 1) * HEAD_DIM)
            sink = jnp.zeros((GROUP * seq, 1), F32)
            for gi in range(GROUP):
                sink = jnp.where(grow >= gi * seq, sinks_ref[layer, kv * GROUP + gi], sink)
            s_c, s_n = scores[e, kv]
            m = jnp.maximum(jnp.maximum(jnp.max(s_c, axis=-1, keepdims=True), jnp.max(s_n, axis=-1, keepdims=True)),
                            sink)
            p_c = jnp.exp(s_c - m)
            p_n = jnp.exp(s_n - m)
            den = jnp.sum(p_c, axis=-1, keepdims=True) + jnp.sum(p_n, axis=-1, keepdims=True) + jnp.exp(sink - m)
            vnew = p[:, OFF_VA + kv * HEAD_DIM:OFF_VA + (kv + 1) * HEAD_DIM].astype(BF16)
            o2 = (lax.dot_general(p_c.astype(BF16), cv_ref[e, hs, :].astype(BF16), NT_DIMS,
                                  preferred_element_type=F32)
                  + jnp.dot(p_n.astype(BF16), vnew, preferred_element_type=F32)) / den
            for gi in range(GROUP):
                out_c[e, kv * GROUP + gi] = o2[gi * seq:(gi + 1) * seq]
    lane = lax.broadcasted_iota(jnp.int32, (D_KV, w_buf), 1)
    for e in elems:
        p = proj[e]
        for pad, cache, new, out in ((kpad_ref, ck_ref, p[:, OFF_KA:OFF_KA + D_KV], ko_ref),
                                     (vpad_ref, cv_ref, p[:, OFF_VA:OFF_VA + D_KV], vo_ref)):
            pad[e, w_buf - seq:w_buf, :] = new
            out[e] = jnp.where(lane >= w_buf - seq, pad[e].T, pltpu.roll(cache[e], w_buf - seq, 1))

    for e in elems:
        parts = [out_a[e]] + [out_b[e, h] for h in range(H_HGRN)] + [out_c[e, h] for h in range(H_ATTN)]
        mix_ref[e * seq:(e + 1) * seq, :] = jnp.concatenate(parts, axis=1)


def _sample_mixers(proj2, sinks, cache_conv, state, cache_k, cache_v, conv_w, conv_b, ln_g, ln_b, hgrn_lb, hng,
                   bias_s, layer, carried):
    B = state.shape[1]
    seq = proj2.shape[0] // B
    w_buf = cache_k.shape[3]
    block = SAMPLE_BLOCK
    depth = hgrn_lb.shape[0]
    hist = CONV_WIDTH - 1
    const2 = lambda i: (0, 0)
    cache_specs = [pl.BlockSpec((None, hist, block, D_CONV), lambda i: (layer, 0, i, 0)),
                   pl.BlockSpec((None, block, H_HGRN, DK_HGRN, DV_HGRN), lambda i: (layer, i, 0, 0, 0)),
                   pl.BlockSpec((None, block, D_KV, w_buf), lambda i: (layer, i, 0, 0)),
                   pl.BlockSpec((None, block, D_KV, w_buf), lambda i: (layer, i, 0, 0))]
    inputs = (sinks, proj2, cache_conv, state, cache_k, cache_v, conv_w, conv_b, ln_g, ln_b, hgrn_lb, hng, bias_s)
    return pl.pallas_call(
        _carried(functools.partial(_sample_mix_kernel, layer=layer, block=block, seq=seq, w_buf=w_buf),
                 len(inputs), len(carried)),
        out_shape=(jax.ShapeDtypeStruct((B * seq, D_MODEL), F32),
                   jax.ShapeDtypeStruct((depth, hist, B, D_CONV), F32),
                   jax.ShapeDtypeStruct((depth, B, H_HGRN, DK_HGRN, DV_HGRN), F32),
                   jax.ShapeDtypeStruct((depth, B, D_KV, w_buf), F32),
                   jax.ShapeDtypeStruct((depth, B, D_KV, w_buf), F32)),
        grid=(B // block,),
        in_specs=[pl.BlockSpec(memory_space=pltpu.SMEM),
                  pl.BlockSpec((block * seq, IN_WIDTH), lambda i: (i, 0))] + cache_specs + [
                  pl.BlockSpec(conv_w.shape, lambda i: (0, 0, 0)),
                  pl.BlockSpec(conv_b.shape, const2),
                  pl.BlockSpec(ln_g.shape, const2),
                  pl.BlockSpec(ln_b.shape, const2),
                  pl.BlockSpec(hgrn_lb.shape, const2),
                  pl.BlockSpec(hng.shape, const2),
                  pl.BlockSpec((KV_HEADS, GROUP * seq, 2 * ATTN_BLOCK), lambda i: (0, 0, 0))] + _carry_specs(carried),
        out_specs=tuple([pl.BlockSpec((block * seq, D_MODEL), lambda i: (i, 0))] + cache_specs),
        input_output_aliases={len(inputs) + i: 1 + i for i in range(len(carried))},
        scratch_shapes=[pltpu.VMEM((block, w_buf, D_KV), F32),
                        pltpu.VMEM((block, w_buf, D_KV), F32)],
        compiler_params=pltpu.CompilerParams(dimension_semantics=("arbitrary",), vmem_limit_bytes=VMEM_LIMIT),
        name="sample_mixers",
    )(*inputs, *carried)


def kernel(x_prompt, x_sample, cache_conv, state_hgrn, cache_swa_k, cache_swa_v, c_prompt, c_sample, rel_bias, w_ada, b_ada, norm_mix_g, w_in, conv_w, conv_b, conv_ln_g, conv_ln_b, hgrn_lb, hgrn_norm_g, attn_sinks, w_out, norm_mlp_g, w_up, w_down, final_g):
    Bp, Tp = x_prompt.shape[:2]
    Bs, Ts = x_sample.shape[:2]
    depth = w_in.shape[0]
    w_buf = cache_swa_k.shape[2]
    assert Tp % MIX_TILE == 0 and (Bp * Tp) % TOK_TILE == 0 and Tp % TOK_TILE == 0 and Bs % SAMPLE_BLOCK == 0
    assert w_buf == WINDOW and GROUP * Ts == SUBLANES

    bias_p, bias_s = _bias_tables(rel_bias, Ts, w_buf)
    mod = _modulation(jnp.concatenate([c_sample, c_prompt], axis=0), w_ada, b_ada)
    hlb = hgrn_lb.astype(F32)
    cc = jnp.swapaxes(cache_conv, 1, 2)
    ck = jnp.swapaxes(cache_swa_k.reshape(depth, Bs, w_buf, D_KV), 2, 3)
    cv = jnp.swapaxes(cache_swa_v.reshape(depth, Bs, w_buf, D_KV), 2, 3)

    xp = x_prompt.reshape(Bp * Tp, D_MODEL)
    xs = x_sample.reshape(Bs * Ts, D_MODEL)
    caches_p = ()
    caches_s = ()
    for l in range(depth):
        final = l == depth - 1
        mod_p = mod[l, Bs:].reshape(Bp, 1, N_MOD * D_MODEL)
        mix_p, proj_s, *caches_p = _prompt_mixers(xp.reshape(Bp, Tp, D_MODEL), mod_p, xs, mod, Bs, norm_mix_g, w_in,
                                                  attn_sinks, conv_w, conv_b, conv_ln_g, conv_ln_b, hlb, hgrn_norm_g,
                                                  bias_p, l, caches_p)
        mix_s, *caches_s = _sample_mixers(proj_s, attn_sinks, cc, state_hgrn, ck, cv, conv_w, conv_b, conv_ln_g,
                                          conv_ln_b, hlb, hgrn_norm_g, bias_s, l, caches_s)
        xp, xs = _out_mlp(mix_p.reshape(Bp * Tp, D_MODEL), xp, mod_p, mix_s, xs, mod, Bs, norm_mlp_g, w_out,
                          w_up, w_down, final_g, l, Tp // TOK_TILE, final)
    cp, sp, kp, vp = caches_p
    cs, ss, ksn, vsn = caches_s
    cs, ksn, vsn = jnp.swapaxes(cs, 1, 2), jnp.swapaxes(ksn, 2, 3), jnp.swapaxes(vsn, 2, 3)
    return (xp.reshape(Bp, Tp, D_MODEL), xs.reshape(Bs, Ts, D_MODEL), cp, cs, sp, ss,
            kp.reshape(depth, Bp, WINDOW, KV_HEADS, HEAD_DIM), ksn.reshape(depth, Bs, w_buf, KV_HEADS, HEAD_DIM),
            vp.reshape(depth, Bp, WINDOW, KV_HEADS, HEAD_DIM), vsn.reshape(depth, Bs, w_buf, KV_HEADS, HEAD_DIM))
```

```python
import functools
import math

import jax
import jax.numpy as jnp
from jax import lax
from jax.experimental import pallas as pl
from jax.experimental.pallas import tpu as pltpu

F32 = jnp.float32
BF16 = jnp.bfloat16

D_MODEL = 1024
D_CONV = 256
CONV_WIDTH = 31
H_HGRN = 4
DK_HGRN = 128
DV_HGRN = 128
D_HGRN = 512
HEAD_DIM = 64
H_ATTN = 4
KV_HEADS = 2
GROUP = H_ATTN // KV_HEADS
D_ATTN = H_ATTN * HEAD_DIM
D_KV = KV_HEADS * HEAD_DIM
WINDOW = 128
ATTN_BLOCK = 128
NUM_BUCKETS = 32
MAX_DISTANCE = 128
D_FF = 4 * D_MODEL
N_MOD = 6
EPS = 1e-6

OFF_AVAL = 0
OFF_AGATE = OFF_AVAL + D_CONV
OFF_Q = OFF_AGATE + D_CONV
OFF_F = OFF_Q + H_HGRN * DK_HGRN
OFF_I = OFF_F + H_HGRN * DK_HGRN
OFF_G = OFF_I + D_HGRN
OFF_QA = OFF_G + D_HGRN
OFF_KA = OFF_QA + D_ATTN
OFF_VA = OFF_KA + D_KV
IN_WIDTH = OFF_VA + D_KV

HGRN_CHUNK = 64
HGRN_KEYBLOCK = 32
HGRN_SPAN = 4
SUBLANES = 8
CONV_PAD = 32
MIX_TILE = 512
TOK_TILE = 512
SAMPLE_BLOCK = 16
FF_CHUNK = 1024
CAST_STEPS = 8
MOD_COLS = 2048
VMEM_LIMIT = 56 * 1024 * 1024

NT_DIMS = (((1,), (1,)), ((), ()))
TN_DIMS = (((0,), (0,)), ((), ()))


def _silu(x):
    return x * jax.nn.sigmoid(x)


def _rms_rows(x):
    return x * lax.rsqrt(jnp.mean(x * x, axis=-1, keepdims=True) + EPS)


def _layer_lb(hlb, layer):
    m = jnp.max(hlb, axis=0, keepdims=True)
    e = jnp.exp(hlb - m)
    p = e / jnp.sum(e, axis=0, keepdims=True)
    lb = jnp.zeros_like(m)
    for i in range(1, layer + 1):
        lb = lb + p[i:i + 1, :]
    return lb


def _split3_bf16(x):
    hi = x.astype(BF16)
    r = x - hi.astype(F32)
    mid = r.astype(BF16)
    return hi, mid, (r - mid.astype(F32)).astype(BF16)


def _select_rows_mxu(sel, x):
    return sum(jnp.dot(sel, part, preferred_element_type=F32) for part in _split3_bf16(x))


def _cumsum_rows_small(g):
    row = lax.broadcasted_iota(jnp.int32, g.shape, 0)
    b = jnp.zeros_like(g)
    for u in range(g.shape[0]):
        b = b + jnp.where(row >= u, g[u:u + 1, :], 0.0)
    return b


def _hgrn_span(proj_ref, mix_ref, st_ref, row0, lb, hng, tri, tick):
    L, KB = HGRN_CHUNK, HGRN_KEYBLOCK
    span = HGRN_SPAN * L
    g, k = _hgrn_gates(proj_ref[pl.ds(row0, span), OFF_F:OFF_F + D_HGRN], lb)
    b = _select_rows_mxu(tri, g)
    units = [(c, h) for c in range(HGRN_SPAN) for h in range(H_HGRN)]

    ops = {}
    for c, h in units:
        rows = pl.ds(row0 + c * L, L)
        cs = slice(h * DK_HGRN, (h + 1) * DK_HGRN)
        q = proj_ref[rows, OFF_Q + h * DK_HGRN:OFF_Q + (h + 1) * DK_HGRN]
        v = proj_ref[rows, OFF_I + h * DV_HGRN:OFF_I + (h + 1) * DV_HGRN].astype(BF16)
        kk = k[c * L:(c + 1) * L, cs]
        bb = b[c * L:(c + 1) * L, cs]
        qp, kp = [], []
        for lo in range(0, L, KB):
            r = bb[lo + KB // 2 - 1:lo + KB // 2, :]
            kp.append((kk[lo:lo + KB] * jnp.exp(r - bb[lo:lo + KB])).astype(BF16))
            qp.append((q[lo:] * jnp.exp(bb[lo:] - r)).astype(BF16))
        bl = bb[L - 1:L, :]
        ops[c, h] = dict(qp=qp, kp=kp, v=v, qt=(q * jnp.exp(bb)).astype(BF16),
                         kst=(kk * jnp.exp(bl - bb)).astype(BF16), e=jnp.exp(bl))
    tick()

    for u in units:
        o = ops[u]
        o["p"] = [lax.dot_general(qp, kp, NT_DIMS, preferred_element_type=F32) for qp, kp in zip(o["qp"], o["kp"])]
        o["m"] = lax.dot_general(o["v"], o["kst"], TN_DIMS, preferred_element_type=F32)
    tick()

    for u in units:
        pm = []
        for p in ops[u]["p"]:
            row = lax.broadcasted_iota(jnp.int32, p.shape, 0)
            col = lax.broadcasted_iota(jnp.int32, p.shape, 1)
            pm.append(jnp.where(row >= col, p, 0.0).astype(BF16))
        ops[u]["p"] = pm
    tick()

    for u in units:
        o = ops[u]
        blocks = [None] * (L // KB)
        for j, p in enumerate(o["p"]):
            cj = jnp.dot(p, o["v"][j * KB:(j + 1) * KB], preferred_element_type=F32)
            for i in range(j, L // KB):
                piece = cj[(i - j) * KB:(i - j + 1) * KB]
                blocks[i] = piece if blocks[i] is None else blocks[i] + piece
        o["o"] = jnp.concatenate(blocks, axis=0)
    tick()

    for h in range(H_HGRN):
        st = st_ref[h]
        for c in range(HGRN_SPAN):
            o = ops[c, h]
            out = o["o"] + lax.dot_general(o["qt"], st.astype(BF16), NT_DIMS, preferred_element_type=F32)
            st = o["e"] * st + o["m"]
            rows = pl.ds(row0 + c * L, L)
            gate = proj_ref[rows, OFF_G + h * DV_HGRN:OFF_G + (h + 1) * DV_HGRN]
            mix_ref[rows, D_CONV + h * DV_HGRN:D_CONV + (h + 1) * DV_HGRN] = _hgrn_out(out, hng, gate).astype(BF16)
        st_ref[h] = st


def _sink_softmax(s, sink):
    m = jnp.maximum(jnp.max(s, axis=-1, keepdims=True), sink)
    p = jnp.exp(s - m)
    return p, jnp.sum(p, axis=-1, keepdims=True) + jnp.exp(sink - m)


def _bias_kernel(tab_ref, bp_ref, bs_ref, op_ref, os_ref, *, seq):
    bk = bp_ref[...]
    for h in range(H_ATTN):
        acc = jnp.full(bk.shape, -jnp.inf, F32)
        for bkt in range(NUM_BUCKETS):
            acc = jnp.where(bk == bkt, tab_ref[bkt, h], acc)
        op_ref[h] = acc
    bk = bs_ref[...]
    row = lax.broadcasted_iota(jnp.int32, bk.shape, 0)
    for kv in range(KV_HEADS):
        acc = jnp.full(bk.shape, -jnp.inf, F32)
        for bkt in range(NUM_BUCKETS):
            val = jnp.full(bk.shape, tab_ref[bkt, kv * GROUP], F32)
            for gi in range(1, GROUP):
                val = jnp.where(row >= gi * seq, tab_ref[bkt, kv * GROUP + gi], val)
            acc = jnp.where(bk == bkt, val, acc)
        os_ref[kv] = acc


def _t5_bucket(rel):
    n = jnp.maximum(rel, 0)
    max_exact = NUM_BUCKETS // 2
    nf = jnp.maximum(n, max_exact).astype(F32)
    large = max_exact + (jnp.log(nf / max_exact) / math.log(MAX_DISTANCE / max_exact)
                         * (NUM_BUCKETS - max_exact)).astype(jnp.int32)
    large = jnp.minimum(large, NUM_BUCKETS - 1)
    return jnp.where(n < max_exact, n, large)


def _bias_tables(rel_bias, dec_seq, w_buf):
    qi = jnp.arange(ATTN_BLOCK, dtype=jnp.int32)[:, None]
    kc = jnp.arange(2 * ATTN_BLOCK, dtype=jnp.int32)[None, :]
    rel_p = qi + ATTN_BLOCK - kc
    bucket_p = jnp.where((rel_p >= 0) & (rel_p <= WINDOW), _t5_bucket(rel_p), -1)
    ts = (jnp.arange(GROUP * dec_seq, dtype=jnp.int32) % dec_seq)[:, None]
    js = jnp.arange(2 * ATTN_BLOCK, dtype=jnp.int32)[None, :]
    rel_s = w_buf + ts - js
    ok_s = (rel_s >= 0) & (rel_s <= WINDOW) & (js < w_buf + dec_seq)
    bucket_s = jnp.where(ok_s, _t5_bucket(rel_s), -1)
    return pl.pallas_call(
        functools.partial(_bias_kernel, seq=dec_seq),
        out_shape=(jax.ShapeDtypeStruct((H_ATTN, ATTN_BLOCK, 2 * ATTN_BLOCK), F32),
                   jax.ShapeDtypeStruct((KV_HEADS, GROUP * dec_seq, 2 * ATTN_BLOCK), F32)),
        in_specs=[pl.BlockSpec(memory_space=pltpu.SMEM),
                  pl.BlockSpec(memory_space=pltpu.VMEM),
                  pl.BlockSpec(memory_space=pltpu.VMEM)],
        out_specs=(pl.BlockSpec(memory_space=pltpu.VMEM), pl.BlockSpec(memory_space=pltpu.VMEM)),
        name="rel_bias_tables",
    )(rel_bias.astype(F32), bucket_p, bucket_s)


def _mod_kernel(ca_ref, cb_ref, w_ref, b_ref, o_ref):
    w = w_ref[...].astype(BF16)
    na = ca_ref.shape[0]
    for c_ref, rows in ((ca_ref, slice(0, na)), (cb_ref, slice(na, na + cb_ref.shape[0]))):
        o_ref[rows, :] = jnp.dot(_silu(c_ref[...]).astype(BF16), w, preferred_element_type=F32) + b_ref[...]


def _modulation(c_a, c_b, w_ada, b_ada):
    depth = w_ada.shape[0]
    n = c_a.shape[0] + c_b.shape[0]
    return pl.pallas_call(
        _mod_kernel,
        out_shape=jax.ShapeDtypeStruct((depth, n, N_MOD * D_MODEL), F32),
        grid=(depth, N_MOD * D_MODEL // MOD_COLS),
        in_specs=[pl.BlockSpec(c_a.shape, lambda l, j: (0, 0)),
                  pl.BlockSpec(c_b.shape, lambda l, j: (0, 0)),
                  pl.BlockSpec((None, D_MODEL, MOD_COLS), lambda l, j: (l, 0, j)),
                  pl.BlockSpec((None, 1, MOD_COLS), lambda l, j: (l, 0, j))],
        out_specs=pl.BlockSpec((None, n, MOD_COLS), lambda l, j: (l, 0, j)),
        compiler_params=pltpu.CompilerParams(dimension_semantics=("arbitrary", "arbitrary"),
                                             vmem_limit_bytes=VMEM_LIMIT),
        name="adaln_modulation",
    )(c_a, c_b, w_ada, b_ada.reshape(depth, 1, N_MOD * D_MODEL))


def _mod_rows(m, n_tokens):
    if m.shape[0] == 1:
        return m
    reps = n_tokens // m.shape[0]
    tok = lax.broadcasted_iota(jnp.int32, (n_tokens, m.shape[0]), 0)
    bat = lax.broadcasted_iota(jnp.int32, (n_tokens, m.shape[0]), 1)
    sel = jnp.where((tok >= bat * reps) & (tok < (bat + 1) * reps), 1.0, 0.0).astype(BF16)
    return _select_rows_mxu(sel, m)


def _modulated_norm(x, g, sc, sh):
    n = x.shape[0]
    return (_rms_rows(x) * (g * (1.0 + _mod_rows(sc, n))) + _mod_rows(sh, n)).astype(BF16)


def _sample_mod_specs(n_rows, chunks, index_map):
    return [pl.BlockSpec((None, n_rows, D_MODEL), functools.partial(index_map, chunk=c),
                         pipeline_mode=pl.Buffered(1)) for c in chunks]


def _mlp_tile(mix, x, g1, sh, sc, g2, ng, wout_ref, wup_ref, wdn_ref, fg, final):
    n = x.shape[0]
    x1 = x + _mod_rows(g1, n) * jnp.dot(mix.astype(BF16), wout_ref[...], preferred_element_type=F32)
    h = _modulated_norm(x1, ng, sc, sh)
    acc = None
    for c in range(D_FF // FF_CHUNK):
        u = jnp.dot(h, wup_ref[:, c * FF_CHUNK:(c + 1) * FF_CHUNK], preferred_element_type=F32)
        u = jnp.square(jnp.maximum(u, 0.0)).astype(BF16)
        d = jnp.dot(u, wdn_ref[c * FF_CHUNK:(c + 1) * FF_CHUNK, :], preferred_element_type=F32)
        acc = d if acc is None else acc + d
    x2 = x1 + _mod_rows(g2, n) * acc
    return _rms_rows(x2) * fg if final else x2


def _mlp_kernel(mix_ref, x_ref, g1_ref, sh_ref, sc_ref, g2_ref, smix_ref, sx_ref, sg1_ref, ssh_ref, ssc_ref, sg2_ref,
                ng_ref, wout_ref, wup_ref, wdn_ref, fg_ref, o_ref, so_ref, wout_s, wup_s, wdn_s, *, layer, final):
    s = pl.program_id(0)

    @pl.when(s < CAST_STEPS)
    def _():
        for src, dst in ((wout_ref, wout_s), (wup_ref, wup_s), (wdn_ref, wdn_s)):
            rows = src.shape[0]
            dst[pl.ds(pl.multiple_of(s * rows, rows), rows), :] = src[...].astype(BF16)

    weights = (ng_ref[layer:layer + 1, :], wout_s, wup_s, wdn_s, fg_ref[...], final)

    @pl.when(s >= CAST_STEPS)
    def _():
        o_ref[...] = _mlp_tile(mix_ref[...], x_ref[...], g1_ref[...], sh_ref[...], sc_ref[...], g2_ref[...], *weights)

    @pl.when(s == pl.num_programs(0) - 1)
    def _():
        so_ref[...] = _mlp_tile(smix_ref[...], sx_ref[...], sg1_ref[...], ssh_ref[...], ssc_ref[...], sg2_ref[...],
                                *weights)


def _out_mlp(mix2, x2, mod, mix_s, xs, mod_all, n_sample, norm_g, w_out, w_up, w_down, final_g, layer,
             tiles_per_batch, final):
    n = x2.shape[0]
    tile = TOK_TILE
    const = lambda i: (0, 0)
    tok = lambda i: (jnp.maximum(i - CAST_STEPS, 0), 0)
    chunk_of_layer = lambda i: (layer, jnp.minimum(i, CAST_STEPS - 1), 0)
    mod_p = lambda chunk: pl.BlockSpec((None, 1, D_MODEL),
                                       lambda i: (jnp.maximum(i - CAST_STEPS, 0) // tiles_per_batch, 0, chunk))
    whole = lambda a: pl.BlockSpec(a.shape, const, pipeline_mode=pl.Buffered(1))
    return pl.pallas_call(
        functools.partial(_mlp_kernel, layer=layer, final=final),
        out_shape=(jax.ShapeDtypeStruct((n, D_MODEL), F32), jax.ShapeDtypeStruct(xs.shape, F32)),
        grid=(CAST_STEPS + n // tile,),
        in_specs=[pl.BlockSpec((tile, D_MODEL), tok),
                  pl.BlockSpec((tile, D_MODEL), tok),
                  mod_p(2), mod_p(3), mod_p(4), mod_p(5),
                  whole(mix_s), whole(xs)]
                 + _sample_mod_specs(n_sample, (2, 3, 4, 5), lambda i, chunk: (layer, 0, chunk)) + [
                  pl.BlockSpec(norm_g.shape, const),
                  pl.BlockSpec((None, D_MODEL // CAST_STEPS, D_MODEL), chunk_of_layer),
                  pl.BlockSpec((None, D_MODEL // CAST_STEPS, D_FF), chunk_of_layer),
                  pl.BlockSpec((None, D_FF // CAST_STEPS, D_MODEL), chunk_of_layer),
                  pl.BlockSpec((1, D_MODEL), const)],
        out_specs=(pl.BlockSpec((tile, D_MODEL), tok), pl.BlockSpec(xs.shape, const, pipeline_mode=pl.Buffered(1))),
        scratch_shapes=[pltpu.VMEM((D_MODEL, D_MODEL), BF16),
                        pltpu.VMEM((D_MODEL, D_FF), BF16),
                        pltpu.VMEM((D_FF, D_MODEL), BF16)],
        compiler_params=pltpu.CompilerParams(dimension_semantics=("arbitrary",), vmem_limit_bytes=VMEM_LIMIT),
        name="out_projection_mlp",
    )(mix2, x2, mod, mod, mod, mod, mix_s, xs, mod_all, mod_all, mod_all, mod_all, norm_g, w_out,
      w_up, w_down, final_g.reshape(1, D_MODEL))


def _conv_ln_swish(acc, lng, lnb):
    mu = jnp.mean(acc, axis=-1, keepdims=True)
    xc = acc - mu
    y = xc * lax.rsqrt(jnp.mean(xc * xc, axis=-1, keepdims=True) + EPS) * lng + lnb
    return _silu(y)


def _hgrn_gates(fh, lb):
    f = lb + (1.0 - lb) * jax.nn.sigmoid(fh)
    return jnp.log(f), 1.0 - f


def _hgrn_out(o, hng, gate):
    return _rms_rows(o) * hng * _silu(gate)


def _ticker(pieces):
    it = iter(pieces)

    def tick():
        piece = next(it, None)
        if piece is not None:
            piece()

    def flush():
        for piece in it:
            piece()

    tick.flush = flush
    return tick


def _prompt_mix_kernel(sinks_ref, x_ref, sh_ref, sc_ref, sx_ref, ssh_ref, ssc_ref, ng_ref, win_ref, convw_ref,
                       convb_ref, lng_ref, lnb_ref, hlb_ref, hng_ref, bias_ref,
                       mix_ref, sproj_ref, convo_ref, so_ref, ko_ref, vo_ref,
                       proj_ref, wbf_ref, abuf, kbuf, vbuf, st_ref, *, layer, tile):
    t = pl.program_id(1)
    last = pl.num_programs(1) - 1

    @pl.when((pl.program_id(0) == 0) & (t == 0))
    def _():
        wbf_ref[...] = win_ref[...].astype(BF16)

    @pl.when(t == 0)
    def _():
        abuf[0:CONV_PAD, :] = jnp.zeros((CONV_PAD, D_CONV), F32)
        abuf[CONV_PAD + tile:CONV_PAD + tile + SUBLANES, :] = jnp.zeros((SUBLANES, D_CONV), F32)
        kbuf[0:ATTN_BLOCK, :] = jnp.zeros((ATTN_BLOCK, D_KV), BF16)
        vbuf[0:ATTN_BLOCK, :] = jnp.zeros((ATTN_BLOCK, D_KV), BF16)
        st_ref[...] = jnp.zeros(st_ref.shape, F32)

    h_in = _modulated_norm(x_ref[...], ng_ref[layer:layer + 1, :], sc_ref[...], sh_ref[...])
    for lo, hi in ((OFF_AVAL, OFF_Q), (OFF_F, OFF_I), (OFF_Q, OFF_F), (OFF_I, OFF_G), (OFF_G, OFF_QA),
                   (OFF_QA, IN_WIDTH)):
        proj_ref[:, lo:hi] = jnp.dot(h_in, wbf_ref[:, lo:hi], preferred_element_type=F32)

    kbuf[ATTN_BLOCK:ATTN_BLOCK + tile, :] = proj_ref[:, OFF_KA:OFF_KA + D_KV].astype(BF16)
    vbuf[ATTN_BLOCK:ATTN_BLOCK + tile, :] = proj_ref[:, OFF_VA:OFF_VA + D_KV].astype(BF16)
    scale = HEAD_DIM ** -0.5
    attn = {}

    def attn_scores(blk):
        def run():
            r0 = blk * ATTN_BLOCK
            for h in range(H_ATTN):
                kv = h // GROUP
                q = (proj_ref[r0:r0 + ATTN_BLOCK, OFF_QA + h * HEAD_DIM:OFF_QA + (h + 1) * HEAD_DIM]
                     * scale).astype(BF16)
                kall = kbuf[r0:r0 + 2 * ATTN_BLOCK, kv * HEAD_DIM:(kv + 1) * HEAD_DIM]
                attn[blk, h] = lax.dot_general(q, kall, NT_DIMS, preferred_element_type=F32)
        return run

    def attn_softmax(blk):
        def run():
            for h in range(H_ATTN):
                s = attn[blk, h] + bias_ref[h]
                if blk == 0:
                    col = lax.broadcasted_iota(jnp.int32, s.shape, 1)
                    s = jnp.where(col + (t * tile - ATTN_BLOCK) >= 0, s, -jnp.inf)
                p, den = _sink_softmax(s, sinks_ref[layer, h])
                attn[blk, h] = (p.astype(BF16), den)
        return run

    def attn_values(blk):
        def run():
            r0 = blk * ATTN_BLOCK
            heads = []
            for h in range(H_ATTN):
                kv = h // GROUP
                p, den = attn[blk, h]
                vall = vbuf[r0:r0 + 2 * ATTN_BLOCK, kv * HEAD_DIM:(kv + 1) * HEAD_DIM]
                heads.append(jnp.dot(p, vall, preferred_element_type=F32) / den)
            mix_ref[r0:r0 + ATTN_BLOCK, D_CONV + D_HGRN:D_MODEL] = jnp.concatenate(heads, axis=1).astype(BF16)
        return run

    tick = _ticker([stage(blk) for blk in range(tile // ATTN_BLOCK)
                    for stage in (attn_scores, attn_softmax, attn_values)])

    abuf[CONV_PAD:CONV_PAD + tile, :] = (proj_ref[:, OFF_AVAL:OFF_AVAL + D_CONV]
                                         * jax.nn.sigmoid(proj_ref[:, OFF_AGATE:OFF_AGATE + D_CONV]))
    first_row = CONV_PAD - (CONV_WIDTH - 1)
    acc = jnp.broadcast_to(convb_ref[layer:layer + 1, :], (tile, D_CONV))
    for r in range(SUBLANES):
        z = None
        for off in range(r, first_row + CONV_WIDTH, SUBLANES):
            j = off - first_row
            if j < 0:
                continue
            term = convw_ref[layer, j:j + 1, :] * abuf[off - r:off - r + tile + SUBLANES, :]
            z = term if z is None else z + term
        acc = acc + (z[0:tile] if r == 0 else pltpu.roll(z, tile + SUBLANES - r, 0)[0:tile])
    mix_ref[:, 0:D_CONV] = _conv_ln_swish(acc, lng_ref[layer:layer + 1, :], lnb_ref[layer:layer + 1, :]).astype(BF16)
    tick()

    lb = _layer_lb(hlb_ref[...], layer)
    hng = hng_ref[layer:layer + 1, :]
    span = HGRN_SPAN * HGRN_CHUNK
    ri = lax.broadcasted_iota(jnp.int32, (span, span), 0)
    ci = lax.broadcasted_iota(jnp.int32, (span, span), 1)
    tri = jnp.where((ri >= ci) & (ri // HGRN_CHUNK == ci // HGRN_CHUNK), 1.0, 0.0).astype(BF16)
    for i in range(tile // span):
        _hgrn_span(proj_ref, mix_ref, st_ref, i * span, lb, hng, tri, tick)
    tick.flush()

    @pl.when(t == last)
    def _():
        convo_ref[...] = abuf[CONV_PAD + tile - (CONV_WIDTH - 1):CONV_PAD + tile, :]
        for h in range(H_HGRN):
            so_ref[h] = st_ref[h].T
        ko_ref[...] = proj_ref[tile - WINDOW:tile, OFF_KA:OFF_KA + D_KV].T
        vo_ref[...] = proj_ref[tile - WINDOW:tile, OFF_VA:OFF_VA + D_KV].T

    abuf[0:CONV_PAD, :] = abuf[tile:tile + CONV_PAD, :]
    kbuf[0:ATTN_BLOCK, :] = kbuf[tile:tile + ATTN_BLOCK, :]
    vbuf[0:ATTN_BLOCK, :] = vbuf[tile:tile + ATTN_BLOCK, :]

    @pl.when((pl.program_id(0) == pl.num_programs(0) - 1) & (t == last))
    def _():
        h_s = _modulated_norm(sx_ref[...], ng_ref[layer:layer + 1, :], ssc_ref[...], ssh_ref[...])
        sproj_ref[...] = jnp.dot(h_s, wbf_ref[...], preferred_element_type=F32)


def _carry_specs(carried):
    return [pl.BlockSpec(memory_space=pl.ANY)] * len(carried)


def _without_carry(kernel_fn, n_in, n_carried, *refs):
    return kernel_fn(*refs[:n_in], *refs[n_in + n_carried:])


def _carried(kernel_fn, n_in, n_carried):
    return functools.partial(_without_carry, kernel_fn, n_in, n_carried)


def _prompt_mixers(x, mod, xs, mod_all, n_sample, norm_g, w_in, sinks, conv_w, conv_b, ln_g, ln_b, hgrn_lb, hng,
                   bias_p, layer, carried):
    B, T = x.shape[:2]
    tile = MIX_TILE
    depth = hgrn_lb.shape[0]
    const2 = lambda b, t: (0, 0)
    inputs = (sinks, x, mod, mod, xs, mod_all, mod_all, norm_g, w_in, conv_w, conv_b, ln_g, ln_b, hgrn_lb, hng, bias_p)
    return pl.pallas_call(
        _carried(functools.partial(_prompt_mix_kernel, layer=layer, tile=tile), len(inputs), len(carried)),
        out_shape=(jax.ShapeDtypeStruct((B, T, D_MODEL), BF16),
                   jax.ShapeDtypeStruct((xs.shape[0], IN_WIDTH), F32),
                   jax.ShapeDtypeStruct((depth, B, CONV_WIDTH - 1, D_CONV), F32),
                   jax.ShapeDtypeStruct((depth, B, H_HGRN, DK_HGRN, DV_HGRN), F32),
                   jax.ShapeDtypeStruct((depth, B, D_KV, WINDOW), F32),
                   jax.ShapeDtypeStruct((depth, B, D_KV, WINDOW), F32)),
        grid=(B, T // tile),
        in_specs=[pl.BlockSpec(memory_space=pltpu.SMEM),
                  pl.BlockSpec((None, tile, D_MODEL), lambda b, t: (b, t, 0)),
                  pl.BlockSpec((None, 1, D_MODEL), lambda b, t: (b, 0, 0)),
                  pl.BlockSpec((None, 1, D_MODEL), lambda b, t: (b, 0, 1)),
                  pl.BlockSpec(xs.shape, const2, pipeline_mode=pl.Buffered(1))]
                 + _sample_mod_specs(n_sample, (0, 1), lambda b, t, chunk: (layer, 0, chunk)) + [
                  pl.BlockSpec(norm_g.shape, const2),
                  pl.BlockSpec((None, D_MODEL, IN_WIDTH), lambda b, t: (layer, 0, 0), pipeline_mode=pl.Buffered(1)),
                  pl.BlockSpec(conv_w.shape, lambda b, t: (0, 0, 0)),
                  pl.BlockSpec(conv_b.shape, const2),
                  pl.BlockSpec(ln_g.shape, const2),
                  pl.BlockSpec(ln_b.shape, const2),
                  pl.BlockSpec(hgrn_lb.shape, const2),
                  pl.BlockSpec(hng.shape, const2),
                  pl.BlockSpec((H_ATTN, ATTN_BLOCK, 2 * ATTN_BLOCK), lambda b, t: (0, 0, 0))] + _carry_specs(carried),
        out_specs=(pl.BlockSpec((None, tile, D_MODEL), lambda b, t: (b, t, 0)),
                   pl.BlockSpec((xs.shape[0], IN_WIDTH), const2),
                   pl.BlockSpec((None, None, CONV_WIDTH - 1, D_CONV), lambda b, t: (layer, b, 0, 0)),
                   pl.BlockSpec((None, None, H_HGRN, DK_HGRN, DV_HGRN), lambda b, t: (layer, b, 0, 0, 0)),
                   pl.BlockSpec((None, None, D_KV, WINDOW), lambda b, t: (layer, b, 0, 0)),
                   pl.BlockSpec((None, None, D_KV, WINDOW), lambda b, t: (layer, b, 0, 0))),
        input_output_aliases={len(inputs) + i: 2 + i for i in range(len(carried))},
        scratch_shapes=[pltpu.VMEM((tile, IN_WIDTH), F32),
                        pltpu.VMEM((D_MODEL, IN_WIDTH), BF16),
                        pltpu.VMEM((CONV_PAD + tile + SUBLANES, D_CONV), F32),
                        pltpu.VMEM((ATTN_BLOCK + tile, D_KV), BF16),
                        pltpu.VMEM((ATTN_BLOCK + tile, D_KV), BF16),
                        pltpu.VMEM((H_HGRN, DV_HGRN, DK_HGRN), F32)],
        compiler_params=pltpu.CompilerParams(dimension_semantics=("arbitrary", "arbitrary"),
                                             vmem_limit_bytes=VMEM_LIMIT),
        name="prompt_mixers",
    )(*inputs, *carried)


def _sample_mix_kernel(sinks_ref, proj_ref, cconv_ref, state_ref, ck_ref, cv_ref, convw_ref, convb_ref, lng_ref,
                       lnb_ref, hlb_ref, hng_ref, bias_ref,
                       mix_ref, convo_ref, so_ref, ko_ref, vo_ref, kpad_ref, vpad_ref, *,
                       layer, block, seq, w_buf):
    hist = CONV_WIDTH - 1

    @pl.when(pl.program_id(0) == 0)
    def _():
        for ref in (kpad_ref, vpad_ref):
            ref[:, 0:w_buf - seq, :] = jnp.zeros((block, w_buf - seq, D_KV), F32)

    lb = _layer_lb(hlb_ref[...], layer)
    hng = hng_ref[layer:layer + 1, :]
    scale = HEAD_DIM ** -0.5
    elems = range(block)
    row8 = lax.broadcasted_iota(jnp.int32, (SUBLANES, DV_HGRN), 0)
    ones_rows = jnp.where((row8 >= seq) & (row8 < seq + 3), 1.0, 0.0)
    zrow = jnp.zeros((1, DK_HGRN), BF16)
    prow = lax.broadcasted_iota(jnp.int32, (seq, seq), 0)
    pcol = lax.broadcasted_iota(jnp.int32, (seq, seq), 1)
    grow = lax.broadcasted_iota(jnp.int32, (GROUP * seq, 1), 0)

    proj = [proj_ref[e * seq:(e + 1) * seq, :] for e in elems]

    glu = [p[:, OFF_AVAL:OFF_AVAL + D_CONV] * jax.nn.sigmoid(p[:, OFF_AGATE:OFF_AGATE + D_CONV]) for p in proj]
    full = [cconv_ref[i] for i in range(hist)]
    full += [jnp.concatenate([glu[e][t:t + 1] for e in elems], axis=0) for t in range(seq)]
    for i in range(hist):
        convo_ref[i] = full[i + seq]
    conv_out = []
    for t in range(seq):
        acc = jnp.broadcast_to(convb_ref[layer:layer + 1, :], (block, D_CONV))
        for j in range(CONV_WIDTH):
            acc = acc + convw_ref[layer, j:j + 1, :] * full[t + j]
        conv_out.append(_conv_ln_swish(acc, lng_ref[layer:layer + 1, :], lnb_ref[layer:layer + 1, :]))
    out_a = [jnp.concatenate([conv_out[t][e:e + 1] for t in range(seq)], axis=0) for e in elems]

    units = [(e, h) for e in elems for h in range(H_HGRN)]
    ops = {}
    for e in elems:
        p = proj[e]
        g, k = _hgrn_gates(p[:, OFF_F:OFF_F + D_HGRN], lb)
        b = _cumsum_rows_small(g)
        for h in range(H_HGRN):
            cs = slice(h * DK_HGRN, (h + 1) * DK_HGRN)
            q = p[:, OFF_Q + h * DK_HGRN:OFF_Q + (h + 1) * DK_HGRN]
            v = p[:, OFF_I + h * DV_HGRN:OFF_I + (h + 1) * DV_HGRN]
            bb = b[:, cs]
            bl = bb[seq - 1:seq, :]
            kst = (k[:, cs] * jnp.exp(bl - bb)).astype(BF16)
            x = jnp.concatenate([kst.astype(F32), *(part.astype(F32) for part in _split3_bf16(jnp.exp(bl))),
                                 zrow.astype(F32)], axis=0).astype(BF16)
            vpad = jnp.concatenate([v, jnp.zeros((SUBLANES - seq, DV_HGRN), F32)], axis=0)
            ops[e, h] = dict(qp=(q * jnp.exp(bb - bl)).astype(BF16), kst=kst, v=v.astype(BF16),
                             qt=(q * jnp.exp(bb)).astype(BF16), x=x,
                             r=jnp.concatenate([vpad, ones_rows], axis=1).astype(BF16))
    for u in units:
        o = ops[u]
        st = state_ref[u[0], u[1]]
        o["p"] = lax.dot_general(o["qp"], o["kst"], NT_DIMS, preferred_element_type=F32)
        o["inter"] = jnp.dot(o["qt"], st.astype(BF16), preferred_element_type=F32)
        me = lax.dot_general(o["x"], o["r"], TN_DIMS, preferred_element_type=F32)
        so_ref[u[0], u[1]] = me[:, DV_HGRN:] * st + me[:, :DV_HGRN]
    out_b = {}
    for u in units:
        o = ops[u]
        pm = jnp.where(prow >= pcol, o["p"], 0.0).astype(BF16)
        out = o["inter"] + jnp.dot(pm, o["v"], preferred_element_type=F32)
        gate = proj[u[0]][:, OFF_G + u[1] * DV_HGRN:OFF_G + (u[1] + 1) * DV_HGRN]
        out_b[u] = _hgrn_out(out, hng, gate)

    scores = {}
    for e in elems:
        p = proj[e]
        for kv in range(KV_HEADS):
            hs = slice(kv * HEAD_DIM, (kv + 1) * HEAD_DIM)
            q2 = jnp.concatenate([p[:, OFF_QA + h * HEAD_DIM:OFF_QA + (h + 1) * HEAD_DIM]
                                  for h in range(kv * GROUP, (kv + 1) * GROUP)], axis=0)
            q2 = (q2 * scale).astype(BF16)
            bias = bias_ref[kv]
            s_c = jnp.dot(q2, ck_ref[e, hs, :].astype(BF16), preferred_element_type=F32) + bias[:, 0:w_buf]
            s_n = (lax.dot_general(q2, p[:, OFF_KA + kv * HEAD_DIM:OFF_KA + (kv + 1) * HEAD_DIM].astype(BF16), NT_DIMS,
                                   preferred_element_type=F32) + bias[:, w_buf:w_buf + seq])
            scores[e, kv] = (s_c, s_n)
    out_c = {}
    for e in elems:
        p = proj[e]
        for kv in range(KV_HEADS):
            hs = slice(kv * HEAD_DIM, (kv + 1) * HEAD_DIM)
            sink = jnp.zeros((GROUP * seq, 1), F32)
            for gi in range(GROUP):
                sink = jnp.where(grow >= gi * seq, sinks_ref[layer, kv * GROUP + gi], sink)
            s_c, s_n = scores[e, kv]
            m = jnp.maximum(jnp.maximum(jnp.max(s_c, axis=-1, keepdims=True), jnp.max(s_n, axis=-1, keepdims=True)),
                            sink)
            p_c = jnp.exp(s_c - m)
            p_n = jnp.exp(s_n - m)
            den = jnp.sum(p_c, axis=-1, keepdims=True) + jnp.sum(p_n, axis=-1, keepdims=True) + jnp.exp(sink - m)
            vnew = p[:, OFF_VA + kv * HEAD_DIM:OFF_VA + (kv + 1) * HEAD_DIM].astype(BF16)
            o2 = (lax.dot_general(p_c.astype(BF16), cv_ref[e, hs, :].astype(BF16), NT_DIMS,
                                  preferred_element_type=F32)
                  + jnp.dot(p_n.astype(BF16), vnew, preferred_element_type=F32)) / den
            for gi in range(GROUP):
                out_c[e, kv * GROUP + gi] = o2[gi * seq:(gi + 1) * seq]
    lane = lax.broadcasted_iota(jnp.int32, (D_KV, w_buf), 1)
    for e in elems:
        p = proj[e]
        for pad, cache, new, out in ((kpad_ref, ck_ref, p[:, OFF_KA:OFF_KA + D_KV], ko_ref),
                                     (vpad_ref, cv_ref, p[:, OFF_VA:OFF_VA + D_KV], vo_ref)):
            pad[e, w_buf - seq:w_buf, :] = new
            out[e] = jnp.where(lane >= w_buf - seq, pad[e].T, pltpu.roll(cache[e], w_buf - seq, 1))

    for e in elems:
        parts = [out_a[e]] + [out_b[e, h] for h in range(H_HGRN)] + [out_c[e, h] for h in range(H_ATTN)]
        mix_ref[e * seq:(e + 1) * seq, :] = jnp.concatenate(parts, axis=1)


def _sample_mixers(proj2, sinks, cache_conv, state, cache_k, cache_v, conv_w, conv_b, ln_g, ln_b, hgrn_lb, hng,
                   bias_s, layer, carried):
    B = state.shape[1]
    seq = proj2.shape[0] // B
    w_buf = cache_k.shape[3]
    block = SAMPLE_BLOCK
    depth = hgrn_lb.shape[0]
    hist = CONV_WIDTH - 1
    const2 = lambda i: (0, 0)
    cache_specs = [pl.BlockSpec((None, hist, block, D_CONV), lambda i: (layer, 0, i, 0)),
                   pl.BlockSpec((None, block, H_HGRN, DK_HGRN, DV_HGRN), lambda i: (layer, i, 0, 0, 0)),
                   pl.BlockSpec((None, block, D_KV, w_buf), lambda i: (layer, i, 0, 0)),
                   pl.BlockSpec((None, block, D_KV, w_buf), lambda i: (layer, i, 0, 0))]
    inputs = (sinks, proj2, cache_conv, state, cache_k, cache_v, conv_w, conv_b, ln_g, ln_b, hgrn_lb, hng, bias_s)
    return pl.pallas_call(
        _carried(functools.partial(_sample_mix_kernel, layer=layer, block=block, seq=seq, w_buf=w_buf),
                 len(inputs), len(carried)),
        out_shape=(jax.ShapeDtypeStruct((B * seq, D_MODEL), F32),
                   jax.ShapeDtypeStruct((depth, hist, B, D_CONV), F32),
                   jax.ShapeDtypeStruct((depth, B, H_HGRN, DK_HGRN, DV_HGRN), F32),
                   jax.ShapeDtypeStruct((depth, B, D_KV, w_buf), F32),
                   jax.ShapeDtypeStruct((depth, B, D_KV, w_buf), F32)),
        grid=(B // block,),
        in_specs=[pl.BlockSpec(memory_space=pltpu.SMEM),
                  pl.BlockSpec((block * seq, IN_WIDTH), lambda i: (i, 0))] + cache_specs + [
                  pl.BlockSpec(conv_w.shape, lambda i: (0, 0, 0)),
                  pl.BlockSpec(conv_b.shape, const2),
                  pl.BlockSpec(ln_g.shape, const2),
                  pl.BlockSpec(ln_b.shape, const2),
                  pl.BlockSpec(hgrn_lb.shape, const2),
                  pl.BlockSpec(hng.shape, const2),
                  pl.BlockSpec((KV_HEADS, GROUP * seq, 2 * ATTN_BLOCK), lambda i: (0, 0, 0))] + _carry_specs(carried),
        out_specs=tuple([pl.BlockSpec((block * seq, D_MODEL), lambda i: (i, 0))] + cache_specs),
        input_output_aliases={len(inputs) + i: 1 + i for i in range(len(carried))},
        scratch_shapes=[pltpu.VMEM((block, w_buf, D_KV), F32),
                        pltpu.VMEM((block, w_buf, D_KV), F32)],
        compiler_params=pltpu.CompilerParams(dimension_semantics=("arbitrary",), vmem_limit_bytes=VMEM_LIMIT),
        name="sample_mixers",
    )(*inputs, *carried)


def kernel(x_prompt, x_sample, cache_conv, state_hgrn, cache_swa_k, cache_swa_v, c_prompt, c_sample, rel_bias, w_ada, b_ada, norm_mix_g, w_in, conv_w, conv_b, conv_ln_g, conv_ln_b, hgrn_lb, hgrn_norm_g, attn_sinks, w_out, norm_mlp_g, w_up, w_down, final_g):
    Bp, Tp = x_prompt.shape[:2]
    Bs, Ts = x_sample.shape[:2]
    depth = w_in.shape[0]
    w_buf = cache_swa_k.shape[2]
    assert Tp % MIX_TILE == 0 and (Bp * Tp) % TOK_TILE == 0 and Tp % TOK_TILE == 0 and Bs % SAMPLE_BLOCK == 0
    assert w_buf == WINDOW and GROUP * Ts == SUBLANES

    bias_p, bias_s = _bias_tables(rel_bias, Ts, w_buf)
    mod = _modulation(c_sample, c_prompt, w_ada, b_ada)
    hlb = hgrn_lb.astype(F32)
    cc = jnp.swapaxes(cache_conv, 1, 2)
    ck = jnp.swapaxes(cache_swa_k.reshape(depth, Bs, w_buf, D_KV), 2, 3)
    cv = jnp.swapaxes(cache_swa_v.reshape(depth, Bs, w_buf, D_KV), 2, 3)

    xp = x_prompt.reshape(Bp * Tp, D_MODEL)
    xs = x_sample.reshape(Bs * Ts, D_MODEL)
    caches_p = ()
    caches_s = ()
    for l in range(depth):
        final = l == depth - 1
        mod_p = mod[l, Bs:].reshape(Bp, 1, N_MOD * D_MODEL)
        mix_p, proj_s, *caches_p = _prompt_mixers(xp.reshape(Bp, Tp, D_MODEL), mod_p, xs, mod, Bs, norm_mix_g, w_in,
                                                  attn_sinks, conv_w, conv_b, conv_ln_g, conv_ln_b, hlb, hgrn_norm_g,
                                                  bias_p, l, caches_p)
        mix_s, *caches_s = _sample_mixers(proj_s, attn_sinks, cc, state_hgrn, ck, cv, conv_w, conv_b, conv_ln_g,
                                          conv_ln_b, hlb, hgrn_norm_g, bias_s, l, caches_s)
        xp, xs = _out_mlp(mix_p.reshape(Bp * Tp, D_MODEL), xp, mod_p, mix_s, xs, mod, Bs, norm_mlp_g, w_out,
                          w_up, w_down, final_g, l, Tp // TOK_TILE, final)
    cp, sp, kp, vp = caches_p
    kp, vp = jnp.swapaxes(kp, 2, 3), jnp.swapaxes(vp, 2, 3)
    cs, ss, ksn, vsn = caches_s
    cs, ksn, vsn = jnp.swapaxes(cs, 1, 2), jnp.swapaxes(ksn, 2, 3), jnp.swapaxes(vsn, 2, 3)
    return (xp.reshape(Bp, Tp, D_MODEL), xs.reshape(Bs, Ts, D_MODEL), cp, cs, sp, ss,
            kp.reshape(depth, Bp, WINDOW, KV_HEADS, HEAD_DIM), ksn.reshape(depth, Bs, w_buf, KV_HEADS, HEAD_DIM),
            vp.reshape(depth, Bp, WINDOW, KV_HEADS, HEAD_DIM), vsn.reshape(depth, Bs, w_buf, KV_HEADS, HEAD_DIM))
```

```python
import functools
import math

import jax
import jax.numpy as jnp
from jax import lax
from jax.experimental import pallas as pl
from jax.experimental.pallas import tpu as pltpu

F32 = jnp.float32
BF16 = jnp.bfloat16

D_MODEL = 1024
D_CONV = 256
CONV_WIDTH = 31
H_HGRN = 4
DK_HGRN = 128
DV_HGRN = 128
D_HGRN = 512
HEAD_DIM = 64
H_ATTN = 4
KV_HEADS = 2
GROUP = H_ATTN // KV_HEADS
D_ATTN = H_ATTN * HEAD_DIM
D_KV = KV_HEADS * HEAD_DIM
WINDOW = 128
ATTN_BLOCK = 128
NUM_BUCKETS = 32
MAX_DISTANCE = 128
D_FF = 4 * D_MODEL
N_MOD = 6
EPS = 1e-6

OFF_AVAL = 0
OFF_AGATE = OFF_AVAL + D_CONV
OFF_Q = OFF_AGATE + D_CONV
OFF_F = OFF_Q + H_HGRN * DK_HGRN
OFF_I = OFF_F + H_HGRN * DK_HGRN
OFF_G = OFF_I + D_HGRN
OFF_QA = OFF_G + D_HGRN
OFF_KA = OFF_QA + D_ATTN
OFF_VA = OFF_KA + D_KV
IN_WIDTH = OFF_VA + D_KV

HGRN_CHUNK = 64
HGRN_KEYBLOCK = 32
HGRN_SPAN = 4
SUBLANES = 8
CONV_PAD = 32
MIX_TILE = 512
TOK_TILE = 512
SAMPLE_BLOCK = 16
FF_CHUNK = 2048
CAST_STEPS = 8
MOD_COLS = 2048
VMEM_LIMIT = 56 * 1024 * 1024

NT_DIMS = (((1,), (1,)), ((), ()))
TN_DIMS = (((0,), (0,)), ((), ()))


def _silu(x):
    return x * jax.nn.sigmoid(x)


def _rms_rows(x):
    return x * lax.rsqrt(jnp.mean(x * x, axis=-1, keepdims=True) + EPS)


def _layer_lb(hlb, layer):
    m = jnp.max(hlb, axis=0, keepdims=True)
    e = jnp.exp(hlb - m)
    p = e / jnp.sum(e, axis=0, keepdims=True)
    lb = jnp.zeros_like(m)
    for i in range(1, layer + 1):
        lb = lb + p[i:i + 1, :]
    return lb


def _split3_bf16(x):
    hi = x.astype(BF16)
    r = x - hi.astype(F32)
    mid = r.astype(BF16)
    return hi, mid, (r - mid.astype(F32)).astype(BF16)


def _select_rows_mxu(sel, x):
    return sum(jnp.dot(sel, part, preferred_element_type=F32) for part in _split3_bf16(x))


def _cumsum_rows_small(g):
    row = lax.broadcasted_iota(jnp.int32, g.shape, 0)
    b = jnp.zeros_like(g)
    for u in range(g.shape[0]):
        b = b + jnp.where(row >= u, g[u:u + 1, :], 0.0)
    return b


def _hgrn_span(proj_ref, mix_ref, st_ref, row0, lb, hng, tri, tick):
    L, KB = HGRN_CHUNK, HGRN_KEYBLOCK
    span = HGRN_SPAN * L
    g, k = _hgrn_gates(proj_ref[pl.ds(row0, span), OFF_F:OFF_F + D_HGRN], lb)
    b = _select_rows_mxu(tri, g)
    units = [(c, h) for c in range(HGRN_SPAN) for h in range(H_HGRN)]

    ops = {}
    for c, h in units:
        rows = pl.ds(row0 + c * L, L)
        cs = slice(h * DK_HGRN, (h + 1) * DK_HGRN)
        q = proj_ref[rows, OFF_Q + h * DK_HGRN:OFF_Q + (h + 1) * DK_HGRN]
        v = proj_ref[rows, OFF_I + h * DV_HGRN:OFF_I + (h + 1) * DV_HGRN].astype(BF16)
        kk = k[c * L:(c + 1) * L, cs]
        bb = b[c * L:(c + 1) * L, cs]
        qp, kp = [], []
        for lo in range(0, L, KB):
            r = bb[lo + KB // 2 - 1:lo + KB // 2, :]
            kp.append((kk[lo:lo + KB] * jnp.exp(r - bb[lo:lo + KB])).astype(BF16))
            qp.append((q[lo:] * jnp.exp(bb[lo:] - r)).astype(BF16))
        bl = bb[L - 1:L, :]
        ops[c, h] = dict(qp=qp, kp=kp, v=v, qt=(q * jnp.exp(bb)).astype(BF16),
                         kst=(kk * jnp.exp(bl - bb)).astype(BF16), e=jnp.exp(bl))
    tick()

    for u in units:
        o = ops[u]
        o["p"] = [lax.dot_general(qp, kp, NT_DIMS, preferred_element_type=F32) for qp, kp in zip(o["qp"], o["kp"])]
        o["m"] = lax.dot_general(o["v"], o["kst"], TN_DIMS, preferred_element_type=F32)
    tick()

    for u in units:
        pm = []
        for p in ops[u]["p"]:
            row = lax.broadcasted_iota(jnp.int32, p.shape, 0)
            col = lax.broadcasted_iota(jnp.int32, p.shape, 1)
            pm.append(jnp.where(row >= col, p, 0.0).astype(BF16))
        ops[u]["p"] = pm
    tick()

    for u in units:
        o = ops[u]
        blocks = [None] * (L // KB)
        for j, p in enumerate(o["p"]):
            cj = jnp.dot(p, o["v"][j * KB:(j + 1) * KB], preferred_element_type=F32)
            for i in range(j, L // KB):
                piece = cj[(i - j) * KB:(i - j + 1) * KB]
                blocks[i] = piece if blocks[i] is None else blocks[i] + piece
        o["o"] = jnp.concatenate(blocks, axis=0)
    tick()

    for h in range(H_HGRN):
        st = st_ref[h]
        for c in range(HGRN_SPAN):
            o = ops[c, h]
            out = o["o"] + lax.dot_general(o["qt"], st.astype(BF16), NT_DIMS, preferred_element_type=F32)
            st = o["e"] * st + o["m"]
            rows = pl.ds(row0 + c * L, L)
            gate = proj_ref[rows, OFF_G + h * DV_HGRN:OFF_G + (h + 1) * DV_HGRN]
            mix_ref[rows, D_CONV + h * DV_HGRN:D_CONV + (h + 1) * DV_HGRN] = _hgrn_out(out, hng, gate).astype(BF16)
        st_ref[h] = st


def _sink_softmax(s, sink):
    m = jnp.maximum(jnp.max(s, axis=-1, keepdims=True), sink)
    p = jnp.exp(s - m)
    return p, jnp.sum(p, axis=-1, keepdims=True) + jnp.exp(sink - m)


def _bias_kernel(tab_ref, bp_ref, bs_ref, op_ref, os_ref, *, seq):
    bk = bp_ref[...]
    for h in range(H_ATTN):
        acc = jnp.full(bk.shape, -jnp.inf, F32)
        for bkt in range(NUM_BUCKETS):
            acc = jnp.where(bk == bkt, tab_ref[bkt, h], acc)
        op_ref[h] = acc
    bk = bs_ref[...]
    row = lax.broadcasted_iota(jnp.int32, bk.shape, 0)
    for kv in range(KV_HEADS):
        acc = jnp.full(bk.shape, -jnp.inf, F32)
        for bkt in range(NUM_BUCKETS):
            val = jnp.full(bk.shape, tab_ref[bkt, kv * GROUP], F32)
            for gi in range(1, GROUP):
                val = jnp.where(row >= gi * seq, tab_ref[bkt, kv * GROUP + gi], val)
            acc = jnp.where(bk == bkt, val, acc)
        os_ref[kv] = acc


def _t5_bucket(rel):
    n = jnp.maximum(rel, 0)
    max_exact = NUM_BUCKETS // 2
    nf = jnp.maximum(n, max_exact).astype(F32)
    large = max_exact + (jnp.log(nf / max_exact) / math.log(MAX_DISTANCE / max_exact)
                         * (NUM_BUCKETS - max_exact)).astype(jnp.int32)
    large = jnp.minimum(large, NUM_BUCKETS - 1)
    return jnp.where(n < max_exact, n, large)


def _bias_tables(rel_bias, dec_seq, w_buf):
    qi = jnp.arange(ATTN_BLOCK, dtype=jnp.int32)[:, None]
    kc = jnp.arange(2 * ATTN_BLOCK, dtype=jnp.int32)[None, :]
    rel_p = qi + ATTN_BLOCK - kc
    bucket_p = jnp.where((rel_p >= 0) & (rel_p <= WINDOW), _t5_bucket(rel_p), -1)
    ts = (jnp.arange(GROUP * dec_seq, dtype=jnp.int32) % dec_seq)[:, None]
    js = jnp.arange(2 * ATTN_BLOCK, dtype=jnp.int32)[None, :]
    rel_s = w_buf + ts - js
    ok_s = (rel_s >= 0) & (rel_s <= WINDOW) & (js < w_buf + dec_seq)
    bucket_s = jnp.where(ok_s, _t5_bucket(rel_s), -1)
    return pl.pallas_call(
        functools.partial(_bias_kernel, seq=dec_seq),
        out_shape=(jax.ShapeDtypeStruct((H_ATTN, ATTN_BLOCK, 2 * ATTN_BLOCK), F32),
                   jax.ShapeDtypeStruct((KV_HEADS, GROUP * dec_seq, 2 * ATTN_BLOCK), F32)),
        in_specs=[pl.BlockSpec(memory_space=pltpu.SMEM),
                  pl.BlockSpec(memory_space=pltpu.VMEM),
                  pl.BlockSpec(memory_space=pltpu.VMEM)],
        out_specs=(pl.BlockSpec(memory_space=pltpu.VMEM), pl.BlockSpec(memory_space=pltpu.VMEM)),
        name="rel_bias_tables",
    )(rel_bias.astype(F32), bucket_p, bucket_s)


def _mod_kernel(ca_ref, cb_ref, w_ref, b_ref, o_ref):
    w = w_ref[...].astype(BF16)
    na = ca_ref.shape[0]
    for c_ref, rows in ((ca_ref, slice(0, na)), (cb_ref, slice(na, na + cb_ref.shape[0]))):
        o_ref[rows, :] = jnp.dot(_silu(c_ref[...]).astype(BF16), w, preferred_element_type=F32) + b_ref[...]


def _modulation(c_a, c_b, w_ada, b_ada):
    depth = w_ada.shape[0]
    n = c_a.shape[0] + c_b.shape[0]
    return pl.pallas_call(
        _mod_kernel,
        out_shape=jax.ShapeDtypeStruct((depth, n, N_MOD * D_MODEL), F32),
        grid=(depth, N_MOD * D_MODEL // MOD_COLS),
        in_specs=[pl.BlockSpec(c_a.shape, lambda l, j: (0, 0)),
                  pl.BlockSpec(c_b.shape, lambda l, j: (0, 0)),
                  pl.BlockSpec((None, D_MODEL, MOD_COLS), lambda l, j: (l, 0, j)),
                  pl.BlockSpec((None, 1, MOD_COLS), lambda l, j: (l, 0, j))],
        out_specs=pl.BlockSpec((None, n, MOD_COLS), lambda l, j: (l, 0, j)),
        compiler_params=pltpu.CompilerParams(dimension_semantics=("arbitrary", "arbitrary"),
                                             vmem_limit_bytes=VMEM_LIMIT),
        name="adaln_modulation",
    )(c_a, c_b, w_ada, b_ada.reshape(depth, 1, N_MOD * D_MODEL))


def _mod_rows(m, n_tokens):
    if m.shape[0] == 1:
        return m
    reps = n_tokens // m.shape[0]
    tok = lax.broadcasted_iota(jnp.int32, (n_tokens, m.shape[0]), 0)
    bat = lax.broadcasted_iota(jnp.int32, (n_tokens, m.shape[0]), 1)
    sel = jnp.where((tok >= bat * reps) & (tok < (bat + 1) * reps), 1.0, 0.0).astype(BF16)
    return _select_rows_mxu(sel, m)


def _modulated_norm(x, g, sc, sh):
    n = x.shape[0]
    return (_rms_rows(x) * (g * (1.0 + _mod_rows(sc, n))) + _mod_rows(sh, n)).astype(BF16)


def _sample_mod_specs(n_rows, chunks, index_map):
    return [pl.BlockSpec((None, n_rows, D_MODEL), functools.partial(index_map, chunk=c),
                         pipeline_mode=pl.Buffered(1)) for c in chunks]


def _mlp_tile(mix, x, g1, sh, sc, g2, ng, wout_ref, wup_ref, wdn_ref, fg, final):
    n = x.shape[0]
    x1 = x + _mod_rows(g1, n) * jnp.dot(mix.astype(BF16), wout_ref[...], preferred_element_type=F32)
    h = _modulated_norm(x1, ng, sc, sh)
    acc = None
    for c in range(D_FF // FF_CHUNK):
        u = jnp.dot(h, wup_ref[:, c * FF_CHUNK:(c + 1) * FF_CHUNK], preferred_element_type=F32)
        u = jnp.square(jnp.maximum(u, 0.0)).astype(BF16)
        d = jnp.dot(u, wdn_ref[c * FF_CHUNK:(c + 1) * FF_CHUNK, :], preferred_element_type=F32)
        acc = d if acc is None else acc + d
    x2 = x1 + _mod_rows(g2, n) * acc
    return _rms_rows(x2) * fg if final else x2


def _mlp_kernel(mix_ref, x_ref, g1_ref, sh_ref, sc_ref, g2_ref, smix_ref, sx_ref, sg1_ref, ssh_ref, ssc_ref, sg2_ref,
                ng_ref, wout_ref, wup_ref, wdn_ref, fg_ref, o_ref, so_ref, wout_s, wup_s, wdn_s, *, layer, final):
    s = pl.program_id(0)

    @pl.when(s < CAST_STEPS)
    def _():
        for src, dst in ((wout_ref, wout_s), (wup_ref, wup_s), (wdn_ref, wdn_s)):
            rows = src.shape[0]
            dst[pl.ds(pl.multiple_of(s * rows, rows), rows), :] = src[...].astype(BF16)

    weights = (ng_ref[layer:layer + 1, :], wout_s, wup_s, wdn_s, fg_ref[...], final)

    @pl.when(s >= CAST_STEPS)
    def _():
        o_ref[...] = _mlp_tile(mix_ref[...], x_ref[...], g1_ref[...], sh_ref[...], sc_ref[...], g2_ref[...], *weights)

    @pl.when(s == pl.num_programs(0) - 1)
    def _():
        so_ref[...] = _mlp_tile(smix_ref[...], sx_ref[...], sg1_ref[...], ssh_ref[...], ssc_ref[...], sg2_ref[...],
                                *weights)


def _out_mlp(mix2, x2, mod, mix_s, xs, mod_all, n_sample, norm_g, w_out, w_up, w_down, final_g, layer,
             tiles_per_batch, final):
    n = x2.shape[0]
    tile = TOK_TILE
    const = lambda i: (0, 0)
    tok = lambda i: (jnp.maximum(i - CAST_STEPS, 0), 0)
    chunk_of_layer = lambda i: (layer, jnp.minimum(i, CAST_STEPS - 1), 0)
    mod_p = lambda chunk: pl.BlockSpec((None, 1, D_MODEL),
                                       lambda i: (jnp.maximum(i - CAST_STEPS, 0) // tiles_per_batch, 0, chunk))
    whole = lambda a: pl.BlockSpec(a.shape, const, pipeline_mode=pl.Buffered(1))
    return pl.pallas_call(
        functools.partial(_mlp_kernel, layer=layer, final=final),
        out_shape=(jax.ShapeDtypeStruct((n, D_MODEL), F32), jax.ShapeDtypeStruct(xs.shape, F32)),
        grid=(CAST_STEPS + n // tile,),
        in_specs=[pl.BlockSpec((tile, D_MODEL), tok),
                  pl.BlockSpec((tile, D_MODEL), tok),
                  mod_p(2), mod_p(3), mod_p(4), mod_p(5),
                  whole(mix_s), whole(xs)]
                 + _sample_mod_specs(n_sample, (2, 3, 4, 5), lambda i, chunk: (layer, 0, chunk)) + [
                  pl.BlockSpec(norm_g.shape, const),
                  pl.BlockSpec((None, D_MODEL // CAST_STEPS, D_MODEL), chunk_of_layer),
                  pl.BlockSpec((None, D_MODEL // CAST_STEPS, D_FF), chunk_of_layer),
                  pl.BlockSpec((None, D_FF // CAST_STEPS, D_MODEL), chunk_of_layer),
                  pl.BlockSpec((1, D_MODEL), const)],
        out_specs=(pl.BlockSpec((tile, D_MODEL), tok), pl.BlockSpec(xs.shape, const, pipeline_mode=pl.Buffered(1))),
        scratch_shapes=[pltpu.VMEM((D_MODEL, D_MODEL), BF16),
                        pltpu.VMEM((D_MODEL, D_FF), BF16),
                        pltpu.VMEM((D_FF, D_MODEL), BF16)],
        compiler_params=pltpu.CompilerParams(dimension_semantics=("arbitrary",), vmem_limit_bytes=VMEM_LIMIT),
        name="out_projection_mlp",
    )(mix2, x2, mod, mod, mod, mod, mix_s, xs, mod_all, mod_all, mod_all, mod_all, norm_g, w_out,
      w_up, w_down, final_g.reshape(1, D_MODEL))


def _conv_ln_swish(acc, lng, lnb):
    mu = jnp.mean(acc, axis=-1, keepdims=True)
    xc = acc - mu
    y = xc * lax.rsqrt(jnp.mean(xc * xc, axis=-1, keepdims=True) + EPS) * lng + lnb
    return _silu(y)


def _hgrn_gates(fh, lb):
    f = lb + (1.0 - lb) * jax.nn.sigmoid(fh)
    return jnp.log(f), 1.0 - f


def _hgrn_out(o, hng, gate):
    return _rms_rows(o) * hng * _silu(gate)


def _ticker(pieces):
    it = iter(pieces)

    def tick():
        piece = next(it, None)
        if piece is not None:
            piece()

    def flush():
        for piece in it:
            piece()

    tick.flush = flush
    return tick


def _prompt_mix_kernel(sinks_ref, x_ref, sh_ref, sc_ref, sx_ref, ssh_ref, ssc_ref, ng_ref, win_ref, convw_ref,
                       convb_ref, lng_ref, lnb_ref, hlb_ref, hng_ref, bias_ref,
                       mix_ref, sproj_ref, convo_ref, so_ref, ko_ref, vo_ref,
                       proj_ref, wbf_ref, abuf, kbuf, vbuf, st_ref, *, layer, tile):
    t = pl.program_id(1)
    last = pl.num_programs(1) - 1

    @pl.when((pl.program_id(0) == 0) & (t == 0))
    def _():
        wbf_ref[...] = win_ref[...].astype(BF16)

    @pl.when(t == 0)
    def _():
        abuf[0:CONV_PAD, :] = jnp.zeros((CONV_PAD, D_CONV), F32)
        abuf[CONV_PAD + tile:CONV_PAD + tile + SUBLANES, :] = jnp.zeros((SUBLANES, D_CONV), F32)
        kbuf[0:ATTN_BLOCK, :] = jnp.zeros((ATTN_BLOCK, D_KV), BF16)
        vbuf[0:ATTN_BLOCK, :] = jnp.zeros((ATTN_BLOCK, D_KV), BF16)
        st_ref[...] = jnp.zeros(st_ref.shape, F32)

    h_in = _modulated_norm(x_ref[...], ng_ref[layer:layer + 1, :], sc_ref[...], sh_ref[...])
    for lo, hi in ((OFF_AVAL, OFF_Q), (OFF_F, OFF_I), (OFF_Q, OFF_F), (OFF_I, OFF_G), (OFF_G, OFF_QA),
                   (OFF_QA, IN_WIDTH)):
        proj_ref[:, lo:hi] = jnp.dot(h_in, wbf_ref[:, lo:hi], preferred_element_type=F32)

    kbuf[ATTN_BLOCK:ATTN_BLOCK + tile, :] = proj_ref[:, OFF_KA:OFF_KA + D_KV].astype(BF16)
    vbuf[ATTN_BLOCK:ATTN_BLOCK + tile, :] = proj_ref[:, OFF_VA:OFF_VA + D_KV].astype(BF16)
    scale = HEAD_DIM ** -0.5
    attn = {}

    def attn_scores(blk):
        def run():
            r0 = blk * ATTN_BLOCK
            for h in range(H_ATTN):
                kv = h // GROUP
                q = (proj_ref[r0:r0 + ATTN_BLOCK, OFF_QA + h * HEAD_DIM:OFF_QA + (h + 1) * HEAD_DIM]
                     * scale).astype(BF16)
                kall = kbuf[r0:r0 + 2 * ATTN_BLOCK, kv * HEAD_DIM:(kv + 1) * HEAD_DIM]
                attn[blk, h] = lax.dot_general(q, kall, NT_DIMS, preferred_element_type=F32)
        return run

    def attn_softmax(blk):
        def run():
            for h in range(H_ATTN):
                s = attn[blk, h] + bias_ref[h]
                if blk == 0:
                    col = lax.broadcasted_iota(jnp.int32, s.shape, 1)
                    s = jnp.where(col + (t * tile - ATTN_BLOCK) >= 0, s, -jnp.inf)
                p, den = _sink_softmax(s, sinks_ref[layer, h])
                attn[blk, h] = (p.astype(BF16), den)
        return run

    def attn_values(blk):
        def run():
            r0 = blk * ATTN_BLOCK
            heads = []
            for h in range(H_ATTN):
                kv = h // GROUP
                p, den = attn[blk, h]
                vall = vbuf[r0:r0 + 2 * ATTN_BLOCK, kv * HEAD_DIM:(kv + 1) * HEAD_DIM]
                heads.append(jnp.dot(p, vall, preferred_element_type=F32) / den)
            mix_ref[r0:r0 + ATTN_BLOCK, D_CONV + D_HGRN:D_MODEL] = jnp.concatenate(heads, axis=1).astype(BF16)
        return run

    tick = _ticker([stage(blk) for blk in range(tile // ATTN_BLOCK)
                    for stage in (attn_scores, attn_softmax, attn_values)])

    abuf[CONV_PAD:CONV_PAD + tile, :] = (proj_ref[:, OFF_AVAL:OFF_AVAL + D_CONV]
                                         * jax.nn.sigmoid(proj_ref[:, OFF_AGATE:OFF_AGATE + D_CONV]))
    first_row = CONV_PAD - (CONV_WIDTH - 1)
    acc = jnp.broadcast_to(convb_ref[layer:layer + 1, :], (tile, D_CONV))
    for r in range(SUBLANES):
        z = None
        for off in range(r, first_row + CONV_WIDTH, SUBLANES):
            j = off - first_row
            if j < 0:
                continue
            term = convw_ref[layer, j:j + 1, :] * abuf[off - r:off - r + tile + SUBLANES, :]
            z = term if z is None else z + term
        acc = acc + (z[0:tile] if r == 0 else pltpu.roll(z, tile + SUBLANES - r, 0)[0:tile])
    mix_ref[:, 0:D_CONV] = _conv_ln_swish(acc, lng_ref[layer:layer + 1, :], lnb_ref[layer:layer + 1, :]).astype(BF16)
    tick()

    lb = _layer_lb(hlb_ref[...], layer)
    hng = hng_ref[layer:layer + 1, :]
    span = HGRN_SPAN * HGRN_CHUNK
    ri = lax.broadcasted_iota(jnp.int32, (span, span), 0)
    ci = lax.broadcasted_iota(jnp.int32, (span, span), 1)
    tri = jnp.where((ri >= ci) & (ri // HGRN_CHUNK == ci // HGRN_CHUNK), 1.0, 0.0).astype(BF16)
    for i in range(tile // span):
        _hgrn_span(proj_ref, mix_ref, st_ref, i * span, lb, hng, tri, tick)
    tick.flush()

    @pl.when(t == last)
    def _():
        convo_ref[...] = abuf[CONV_PAD + tile - (CONV_WIDTH - 1):CONV_PAD + tile, :]
        for h in range(H_HGRN):
            so_ref[h] = st_ref[h].T
        ko_ref[...] = proj_ref[tile - WINDOW:tile, OFF_KA:OFF_KA + D_KV].T
        vo_ref[...] = proj_ref[tile - WINDOW:tile, OFF_VA:OFF_VA + D_KV].T

    abuf[0:CONV_PAD, :] = abuf[tile:tile + CONV_PAD, :]
    kbuf[0:ATTN_BLOCK, :] = kbuf[tile:tile + ATTN_BLOCK, :]
    vbuf[0:ATTN_BLOCK, :] = vbuf[tile:tile + ATTN_BLOCK, :]

    @pl.when((pl.program_id(0) == pl.num_programs(0) - 1) & (t == last))
    def _():
        h_s = _modulated_norm(sx_ref[...], ng_ref[layer:layer + 1, :], ssc_ref[...], ssh_ref[...])
        sproj_ref[...] = jnp.dot(h_s, wbf_ref[...], preferred_element_type=F32)


def _carry_specs(carried):
    return [pl.BlockSpec(memory_space=pl.ANY)] * len(carried)


def _without_carry(kernel_fn, n_in, n_carried, *refs):
    return kernel_fn(*refs[:n_in], *refs[n_in + n_carried:])


def _carried(kernel_fn, n_in, n_carried):
    return functools.partial(_without_carry, kernel_fn, n_in, n_carried)


def _prompt_mixers(x, mod, xs, mod_all, n_sample, norm_g, w_in, sinks, conv_w, conv_b, ln_g, ln_b, hgrn_lb, hng,
                   bias_p, layer, carried):
    B, T = x.shape[:2]
    tile = MIX_TILE
    depth = hgrn_lb.shape[0]
    const2 = lambda b, t: (0, 0)
    inputs = (sinks, x, mod, mod, xs, mod_all, mod_all, norm_g, w_in, conv_w, conv_b, ln_g, ln_b, hgrn_lb, hng, bias_p)
    return pl.pallas_call(
        _carried(functools.partial(_prompt_mix_kernel, layer=layer, tile=tile), len(inputs), len(carried)),
        out_shape=(jax.ShapeDtypeStruct((B, T, D_MODEL), BF16),
                   jax.ShapeDtypeStruct((xs.shape[0], IN_WIDTH), F32),
                   jax.ShapeDtypeStruct((depth, B, CONV_WIDTH - 1, D_CONV), F32),
                   jax.ShapeDtypeStruct((depth, B, H_HGRN, DK_HGRN, DV_HGRN), F32),
                   jax.ShapeDtypeStruct((depth, B, D_KV, WINDOW), F32),
                   jax.ShapeDtypeStruct((depth, B, D_KV, WINDOW), F32)),
        grid=(B, T // tile),
        in_specs=[pl.BlockSpec(memory_space=pltpu.SMEM),
                  pl.BlockSpec((None, tile, D_MODEL), lambda b, t: (b, t, 0)),
                  pl.BlockSpec((None, 1, D_MODEL), lambda b, t: (b, 0, 0)),
                  pl.BlockSpec((None, 1, D_MODEL), lambda b, t: (b, 0, 1)),
                  pl.BlockSpec(xs.shape, const2, pipeline_mode=pl.Buffered(1))]
                 + _sample_mod_specs(n_sample, (0, 1), lambda b, t, chunk: (layer, 0, chunk)) + [
                  pl.BlockSpec(norm_g.shape, const2),
                  pl.BlockSpec((None, D_MODEL, IN_WIDTH), lambda b, t: (layer, 0, 0), pipeline_mode=pl.Buffered(1)),
                  pl.BlockSpec(conv_w.shape, lambda b, t: (0, 0, 0)),
                  pl.BlockSpec(conv_b.shape, const2),
                  pl.BlockSpec(ln_g.shape, const2),
                  pl.BlockSpec(ln_b.shape, const2),
                  pl.BlockSpec(hgrn_lb.shape, const2),
                  pl.BlockSpec(hng.shape, const2),
                  pl.BlockSpec((H_ATTN, ATTN_BLOCK, 2 * ATTN_BLOCK), lambda b, t: (0, 0, 0))] + _carry_specs(carried),
        out_specs=(pl.BlockSpec((None, tile, D_MODEL), lambda b, t: (b, t, 0)),
                   pl.BlockSpec((xs.shape[0], IN_WIDTH), const2),
                   pl.BlockSpec((None, None, CONV_WIDTH - 1, D_CONV), lambda b, t: (layer, b, 0, 0)),
                   pl.BlockSpec((None, None, H_HGRN, DK_HGRN, DV_HGRN), lambda b, t: (layer, b, 0, 0, 0)),
                   pl.BlockSpec((None, None, D_KV, WINDOW), lambda b, t: (layer, b, 0, 0)),
                   pl.BlockSpec((None, None, D_KV, WINDOW), lambda b, t: (layer, b, 0, 0))),
        input_output_aliases={len(inputs) + i: 2 + i for i in range(len(carried))},
        scratch_shapes=[pltpu.VMEM((tile, IN_WIDTH), F32),
                        pltpu.VMEM((D_MODEL, IN_WIDTH), BF16),
                        pltpu.VMEM((CONV_PAD + tile + SUBLANES, D_CONV), F32),
                        pltpu.VMEM((ATTN_BLOCK + tile, D_KV), BF16),
                        pltpu.VMEM((ATTN_BLOCK + tile, D_KV), BF16),
                        pltpu.VMEM((H_HGRN, DV_HGRN, DK_HGRN), F32)],
        compiler_params=pltpu.CompilerParams(dimension_semantics=("arbitrary", "arbitrary"),
                                             vmem_limit_bytes=VMEM_LIMIT),
        name="prompt_mixers",
    )(*inputs, *carried)


def _sample_mix_kernel(sinks_ref, proj_ref, cconv_ref, state_ref, ck_ref, cv_ref, convw_ref, convb_ref, lng_ref,
                       lnb_ref, hlb_ref, hng_ref, bias_ref,
                       mix_ref, convo_ref, so_ref, ko_ref, vo_ref, kpad_ref, vpad_ref, *,
                       layer, block, seq, w_buf):
    hist = CONV_WIDTH - 1

    @pl.when(pl.program_id(0) == 0)
    def _():
        for ref in (kpad_ref, vpad_ref):
            ref[:, 0:w_buf - seq, :] = jnp.zeros((block, w_buf - seq, D_KV), F32)

    lb = _layer_lb(hlb_ref[...], layer)
    hng = hng_ref[layer:layer + 1, :]
    scale = HEAD_DIM ** -0.5
    elems = range(block)
    row8 = lax.broadcasted_iota(jnp.int32, (SUBLANES, DV_HGRN), 0)
    ones_rows = jnp.where((row8 >= seq) & (row8 < seq + 3), 1.0, 0.0)
    zrow = jnp.zeros((1, DK_HGRN), BF16)
    prow = lax.broadcasted_iota(jnp.int32, (seq, seq), 0)
    pcol = lax.broadcasted_iota(jnp.int32, (seq, seq), 1)
    grow = lax.broadcasted_iota(jnp.int32, (GROUP * seq, 1), 0)

    proj = [proj_ref[e * seq:(e + 1) * seq, :] for e in elems]

    glu = [p[:, OFF_AVAL:OFF_AVAL + D_CONV] * jax.nn.sigmoid(p[:, OFF_AGATE:OFF_AGATE + D_CONV]) for p in proj]
    full = [cconv_ref[i] for i in range(hist)]
    full += [jnp.concatenate([glu[e][t:t + 1] for e in elems], axis=0) for t in range(seq)]
    for i in range(hist):
        convo_ref[i] = full[i + seq]
    conv_out = []
    for t in range(seq):
        acc = jnp.broadcast_to(convb_ref[layer:layer + 1, :], (block, D_CONV))
        for j in range(CONV_WIDTH):
            acc = acc + convw_ref[layer, j:j + 1, :] * full[t + j]
        conv_out.append(_conv_ln_swish(acc, lng_ref[layer:layer + 1, :], lnb_ref[layer:layer + 1, :]))
    out_a = [jnp.concatenate([conv_out[t][e:e + 1] for t in range(seq)], axis=0) for e in elems]

    units = [(e, h) for e in elems for h in range(H_HGRN)]
    ops = {}
    for e in elems:
        p = proj[e]
        g, k = _hgrn_gates(p[:, OFF_F:OFF_F + D_HGRN], lb)
        b = _cumsum_rows_small(g)
        for h in range(H_HGRN):
            cs = slice(h * DK_HGRN, (h + 1) * DK_HGRN)
            q = p[:, OFF_Q + h * DK_HGRN:OFF_Q + (h + 1) * DK_HGRN]
            v = p[:, OFF_I + h * DV_HGRN:OFF_I + (h + 1) * DV_HGRN]
            bb = b[:, cs]
            bl = bb[seq - 1:seq, :]
            kst = (k[:, cs] * jnp.exp(bl - bb)).astype(BF16)
            x = jnp.concatenate([kst.astype(F32), *(part.astype(F32) for part in _split3_bf16(jnp.exp(bl))),
                                 zrow.astype(F32)], axis=0).astype(BF16)
            vpad = jnp.concatenate([v, jnp.zeros((SUBLANES - seq, DV_HGRN), F32)], axis=0)
            ops[e, h] = dict(qp=(q * jnp.exp(bb - bl)).astype(BF16), kst=kst, v=v.astype(BF16),
                             qt=(q * jnp.exp(bb)).astype(BF16), x=x,
                             r=jnp.concatenate([vpad, ones_rows], axis=1).astype(BF16))
    for u in units:
        o = ops[u]
        st = state_ref[u[0], u[1]]
        o["p"] = lax.dot_general(o["qp"], o["kst"], NT_DIMS, preferred_element_type=F32)
        o["inter"] = jnp.dot(o["qt"], st.astype(BF16), preferred_element_type=F32)
        me = lax.dot_general(o["x"], o["r"], TN_DIMS, preferred_element_type=F32)
        so_ref[u[0], u[1]] = me[:, DV_HGRN:] * st + me[:, :DV_HGRN]
    out_b = {}
    for u in units:
        o = ops[u]
        pm = jnp.where(prow >= pcol, o["p"], 0.0).astype(BF16)
        out = o["inter"] + jnp.dot(pm, o["v"], preferred_element_type=F32)
        gate = proj[u[0]][:, OFF_G + u[1] * DV_HGRN:OFF_G + (u[1] + 1) * DV_HGRN]
        out_b[u] = _hgrn_out(out, hng, gate)

    scores = {}
    for e in elems:
        p = proj[e]
        for kv in range(KV_HEADS):
            hs = slice(kv * HEAD_DIM, (kv + 1) * HEAD_DIM)
            q2 = jnp.concatenate([p[:, OFF_QA + h * HEAD_DIM:OFF_QA + (h + 1) * HEAD_DIM]
                                  for h in range(kv * GROUP, (kv + 1) * GROUP)], axis=0)
            q2 = (q2 * scale).astype(BF16)
            bias = bias_ref[kv]
            s_c = jnp.dot(q2, ck_ref[e, hs, :].astype(BF16), preferred_element_type=F32) + bias[:, 0:w_buf]
            s_n = (lax.dot_general(q2, p[:, OFF_KA + kv * HEAD_DIM:OFF_KA + (kv + 1) * HEAD_DIM].astype(BF16), NT_DIMS,
                                   preferred_element_type=F32) + bias[:, w_buf:w_buf + seq])
            scores[e, kv] = (s_c, s_n)
    out_c = {}
    for e in elems:
        p = proj[e]
        for kv in range(KV_HEADS):
            hs = slice(kv * HEAD_DIM, (kv + 1) * HEAD_DIM)
            sink = jnp.zeros((GROUP * seq, 1), F32)
            for gi in range(GROUP):
                sink = jnp.where(grow >= gi * seq, sinks_ref[layer, kv * GROUP + gi], sink)
            s_c, s_n = scores[e, kv]
            m = jnp.maximum(jnp.maximum(jnp.max(s_c, axis=-1, keepdims=True), jnp.max(s_n, axis=-1, keepdims=True)),
                            sink)
            p_c = jnp.exp(s_c - m)
            p_n = jnp.exp(s_n - m)
            den = jnp.sum(p_c, axis=-1, keepdims=True) + jnp.sum(p_n, axis=-1, keepdims=True) + jnp.exp(sink - m)
            vnew = p[:, OFF_VA + kv * HEAD_DIM:OFF_VA + (kv + 1) * HEAD_DIM].astype(BF16)
            o2 = (lax.dot_general(p_c.astype(BF16), cv_ref[e, hs, :].astype(BF16), NT_DIMS,
                                  preferred_element_type=F32)
                  + jnp.dot(p_n.astype(BF16), vnew, preferred_element_type=F32)) / den
            for gi in range(GROUP):
                out_c[e, kv * GROUP + gi] = o2[gi * seq:(gi + 1) * seq]
    lane = lax.broadcasted_iota(jnp.int32, (D_KV, w_buf), 1)
    for e in elems:
        p = proj[e]
        for pad, cache, new, out in ((kpad_ref, ck_ref, p[:, OFF_KA:OFF_KA + D_KV], ko_ref),
                                     (vpad_ref, cv_ref, p[:, OFF_VA:OFF_VA + D_KV], vo_ref)):
            pad[e, w_buf - seq:w_buf, :] = new
            out[e] = jnp.where(lane >= w_buf - seq, pad[e].T, pltpu.roll(cache[e], w_buf - seq, 1))

    for e in elems:
        parts = [out_a[e]] + [out_b[e, h] for h in range(H_HGRN)] + [out_c[e, h] for h in range(H_ATTN)]
        mix_ref[e * seq:(e + 1) * seq, :] = jnp.concatenate(parts, axis=1)


def _sample_mixers(proj2, sinks, cache_conv, state, cache_k, cache_v, conv_w, conv_b, ln_g, ln_b, hgrn_lb, hng,
                   bias_s, layer, carried):
    B = state.shape[1]
    seq = proj2.shape[0] // B
    w_buf = cache_k.shape[3]
    block = SAMPLE_BLOCK
    depth = hgrn_lb.shape[0]
    hist = CONV_WIDTH - 1
    const2 = lambda i: (0, 0)
    cache_specs = [pl.BlockSpec((None, hist, block, D_CONV), lambda i: (layer, 0, i, 0)),
                   pl.BlockSpec((None, block, H_HGRN, DK_HGRN, DV_HGRN), lambda i: (layer, i, 0, 0, 0)),
                   pl.BlockSpec((None, block, D_KV, w_buf), lambda i: (layer, i, 0, 0)),
                   pl.BlockSpec((None, block, D_KV, w_buf), lambda i: (layer, i, 0, 0))]
    inputs = (sinks, proj2, cache_conv, state, cache_k, cache_v, conv_w, conv_b, ln_g, ln_b, hgrn_lb, hng, bias_s)
    return pl.pallas_call(
        _carried(functools.partial(_sample_mix_kernel, layer=layer, block=block, seq=seq, w_buf=w_buf),
                 len(inputs), len(carried)),
        out_shape=(jax.ShapeDtypeStruct((B * seq, D_MODEL), F32),
                   jax.ShapeDtypeStruct((depth, hist, B, D_CONV), F32),
                   jax.ShapeDtypeStruct((depth, B, H_HGRN, DK_HGRN, DV_HGRN), F32),
                   jax.ShapeDtypeStruct((depth, B, D_KV, w_buf), F32),
                   jax.ShapeDtypeStruct((depth, B, D_KV, w_buf), F32)),
        grid=(B // block,),
        in_specs=[pl.BlockSpec(memory_space=pltpu.SMEM),
                  pl.BlockSpec((block * seq, IN_WIDTH), lambda i: (i, 0))] + cache_specs + [
                  pl.BlockSpec(conv_w.shape, lambda i: (0, 0, 0)),
                  pl.BlockSpec(conv_b.shape, const2),
                  pl.BlockSpec(ln_g.shape, const2),
                  pl.BlockSpec(ln_b.shape, const2),
                  pl.BlockSpec(hgrn_lb.shape, const2),
                  pl.BlockSpec(hng.shape, const2),
                  pl.BlockSpec((KV_HEADS, GROUP * seq, 2 * ATTN_BLOCK), lambda i: (0, 0, 0))] + _carry_specs(carried),
        out_specs=tuple([pl.BlockSpec((block * seq, D_MODEL), lambda i: (i, 0))] + cache_specs),
        input_output_aliases={len(inputs) + i: 1 + i for i in range(len(carried))},
        scratch_shapes=[pltpu.VMEM((block, w_buf, D_KV), F32),
                        pltpu.VMEM((block, w_buf, D_KV), F32)],
        compiler_params=pltpu.CompilerParams(dimension_semantics=("arbitrary",), vmem_limit_bytes=VMEM_LIMIT),
        name="sample_mixers",
    )(*inputs, *carried)


def kernel(x_prompt, x_sample, cache_conv, state_hgrn, cache_swa_k, cache_swa_v, c_prompt, c_sample, rel_bias, w_ada, b_ada, norm_mix_g, w_in, conv_w, conv_b, conv_ln_g, conv_ln_b, hgrn_lb, hgrn_norm_g, attn_sinks, w_out, norm_mlp_g, w_up, w_down, final_g):
    Bp, Tp = x_prompt.shape[:2]
    Bs, Ts = x_sample.shape[:2]
    depth = w_in.shape[0]
    w_buf = cache_swa_k.shape[2]
    assert Tp % MIX_TILE == 0 and (Bp * Tp) % TOK_TILE == 0 and Tp % TOK_TILE == 0 and Bs % SAMPLE_BLOCK == 0
    assert w_buf == WINDOW and GROUP * Ts == SUBLANES

    bias_p, bias_s = _bias_tables(rel_bias, Ts, w_buf)
    mod = _modulation(c_sample, c_prompt, w_ada, b_ada)
    hlb = hgrn_lb.astype(F32)
    cc = jnp.swapaxes(cache_conv, 1, 2)
    ck = jnp.swapaxes(cache_swa_k.reshape(depth, Bs, w_buf, D_KV), 2, 3)
    cv = jnp.swapaxes(cache_swa_v.reshape(depth, Bs, w_buf, D_KV), 2, 3)

    xp = x_prompt.reshape(Bp * Tp, D_MODEL)
    xs = x_sample.reshape(Bs * Ts, D_MODEL)
    caches_p = ()
    caches_s = ()
    for l in range(depth):
        final = l == depth - 1
        mod_p = mod[l, Bs:].reshape(Bp, 1, N_MOD * D_MODEL)
        mix_p, proj_s, *caches_p = _prompt_mixers(xp.reshape(Bp, Tp, D_MODEL), mod_p, xs, mod, Bs, norm_mix_g, w_in,
                                                  attn_sinks, conv_w, conv_b, conv_ln_g, conv_ln_b, hlb, hgrn_norm_g,
                                                  bias_p, l, caches_p)
        mix_s, *caches_s = _sample_mixers(proj_s, attn_sinks, cc, state_hgrn, ck, cv, conv_w, conv_b, conv_ln_g,
                                          conv_ln_b, hlb, hgrn_norm_g, bias_s, l, caches_s)
        xp, xs = _out_mlp(mix_p.reshape(Bp * Tp, D_MODEL), xp, mod_p, mix_s, xs, mod, Bs, norm_mlp_g, w_out,
                          w_up, w_down, final_g, l, Tp // TOK_TILE, final)
    cp, sp, kp, vp = caches_p
    kp, vp = jnp.swapaxes(kp, 2, 3), jnp.swapaxes(vp, 2, 3)
    cs, ss, ksn, vsn = caches_s
    cs, ksn, vsn = jnp.swapaxes(cs, 1, 2), jnp.swapaxes(ksn, 2, 3), jnp.swapaxes(vsn, 2, 3)
    return (xp.reshape(Bp, Tp, D_MODEL), xs.reshape(Bs, Ts, D_MODEL), cp, cs, sp, ss,
            kp.reshape(depth, Bp, WINDOW, KV_HEADS, HEAD_DIM), ksn.reshape(depth, Bs, w_buf, KV_HEADS, HEAD_DIM),
            vp.reshape(depth, Bp, WINDOW, KV_HEADS, HEAD_DIM), vsn.reshape(depth, Bs, w_buf, KV_HEADS, HEAD_DIM))
```

```python
import functools
import math

import jax
import jax.numpy as jnp
from jax import lax
from jax.experimental import pallas as pl
from jax.experimental.pallas import tpu as pltpu

F32 = jnp.float32
BF16 = jnp.bfloat16

D_MODEL = 1024
D_CONV = 256
CONV_WIDTH = 31
H_HGRN = 4
DK_HGRN = 128
DV_HGRN = 128
D_HGRN = 512
HEAD_DIM = 64
H_ATTN = 4
KV_HEADS = 2
GROUP = H_ATTN // KV_HEADS
D_ATTN = H_ATTN * HEAD_DIM
D_KV = KV_HEADS * HEAD_DIM
WINDOW = 128
ATTN_BLOCK = 128
NUM_BUCKETS = 32
MAX_DISTANCE = 128
D_FF = 4 * D_MODEL
N_MOD = 6
EPS = 1e-6

OFF_AVAL = 0
OFF_AGATE = OFF_AVAL + D_CONV
OFF_Q = OFF_AGATE + D_CONV
OFF_F = OFF_Q + H_HGRN * DK_HGRN
OFF_I = OFF_F + H_HGRN * DK_HGRN
OFF_G = OFF_I + D_HGRN
OFF_QA = OFF_G + D_HGRN
OFF_KA = OFF_QA + D_ATTN
OFF_VA = OFF_KA + D_KV
IN_WIDTH = OFF_VA + D_KV

HGRN_CHUNK = 64
HGRN_KEYBLOCK = 32
HGRN_SPAN = 4
SUBLANES = 8
CONV_PAD = 32
MIX_TILE = 512
TOK_TILE = 512
SAMPLE_BLOCK = 16
FF_CHUNK = 2048
CAST_STEPS = 8
MOD_COLS = 2048
VMEM_LIMIT = 56 * 1024 * 1024

NT_DIMS = (((1,), (1,)), ((), ()))
TN_DIMS = (((0,), (0,)), ((), ()))


def _silu(x):
    return x * jax.nn.sigmoid(x)


def _rms_rows(x):
    return x * lax.rsqrt(jnp.mean(x * x, axis=-1, keepdims=True) + EPS)


def _layer_lb(hlb, layer):
    m = jnp.max(hlb, axis=0, keepdims=True)
    e = jnp.exp(hlb - m)
    p = e / jnp.sum(e, axis=0, keepdims=True)
    lb = jnp.zeros_like(m)
    for i in range(1, layer + 1):
        lb = lb + p[i:i + 1, :]
    return lb


def _split3_bf16(x):
    hi = x.astype(BF16)
    r = x - hi.astype(F32)
    mid = r.astype(BF16)
    return hi, mid, (r - mid.astype(F32)).astype(BF16)


def _select_rows_mxu(sel, x):
    return sum(jnp.dot(sel, part, preferred_element_type=F32) for part in _split3_bf16(x))


def _cumsum_rows_small(g):
    row = lax.broadcasted_iota(jnp.int32, g.shape, 0)
    b = jnp.zeros_like(g)
    for u in range(g.shape[0]):
        b = b + jnp.where(row >= u, g[u:u + 1, :], 0.0)
    return b


def _hgrn_span(proj_ref, mix_ref, st_ref, row0, lb, hng, tri, tick):
    L, KB = HGRN_CHUNK, HGRN_KEYBLOCK
    span = HGRN_SPAN * L
    g, k = _hgrn_gates(proj_ref[pl.ds(row0, span), OFF_F:OFF_F + D_HGRN], lb)
    b = _select_rows_mxu(tri, g)
    units = [(c, h) for c in range(HGRN_SPAN) for h in range(H_HGRN)]

    ops = {}
    for c, h in units:
        rows = pl.ds(row0 + c * L, L)
        cs = slice(h * DK_HGRN, (h + 1) * DK_HGRN)
        q = proj_ref[rows, OFF_Q + h * DK_HGRN:OFF_Q + (h + 1) * DK_HGRN]
        v = proj_ref[rows, OFF_I + h * DV_HGRN:OFF_I + (h + 1) * DV_HGRN].astype(BF16)
        kk = k[c * L:(c + 1) * L, cs]
        bb = b[c * L:(c + 1) * L, cs]
        qp, kp = [], []
        for lo in range(0, L, KB):
            r = bb[lo + KB // 2 - 1:lo + KB // 2, :]
            kp.append((kk[lo:lo + KB] * jnp.exp(r - bb[lo:lo + KB])).astype(BF16))
            qp.append((q[lo:] * jnp.exp(bb[lo:] - r)).astype(BF16))
        bl = bb[L - 1:L, :]
        ops[c, h] = dict(qp=qp, kp=kp, v=v, qt=(q * jnp.exp(bb)).astype(BF16),
                         kst=(kk * jnp.exp(bl - bb)).astype(BF16), e=jnp.exp(bl))
    tick()

    for u in units:
        o = ops[u]
        o["p"] = [lax.dot_general(qp, kp, NT_DIMS, preferred_element_type=F32) for qp, kp in zip(o["qp"], o["kp"])]
        o["m"] = lax.dot_general(o["v"], o["kst"], TN_DIMS, preferred_element_type=F32)
    tick()

    for u in units:
        pm = []
        for p in ops[u]["p"]:
            row = lax.broadcasted_iota(jnp.int32, p.shape, 0)
            col = lax.broadcasted_iota(jnp.int32, p.shape, 1)
            pm.append(jnp.where(row >= col, p, 0.0).astype(BF16))
        ops[u]["p"] = pm
    tick()

    for u in units:
        o = ops[u]
        blocks = [None] * (L // KB)
        for j, p in enumerate(o["p"]):
            cj = jnp.dot(p, o["v"][j * KB:(j + 1) * KB], preferred_element_type=F32)
            for i in range(j, L // KB):
                piece = cj[(i - j) * KB:(i - j + 1) * KB]
                blocks[i] = piece if blocks[i] is None else blocks[i] + piece
        o["o"] = jnp.concatenate(blocks, axis=0)
    tick()

    for h in range(H_HGRN):
        st = st_ref[h]
        for c in range(HGRN_SPAN):
            o = ops[c, h]
            out = o["o"] + lax.dot_general(o["qt"], st.astype(BF16), NT_DIMS, preferred_element_type=F32)
            st = o["e"] * st + o["m"]
            rows = pl.ds(row0 + c * L, L)
            gate = proj_ref[rows, OFF_G + h * DV_HGRN:OFF_G + (h + 1) * DV_HGRN]
            mix_ref[rows, D_CONV + h * DV_HGRN:D_CONV + (h + 1) * DV_HGRN] = _hgrn_out(out, hng, gate).astype(BF16)
        st_ref[h] = st


def _sink_softmax(s, sink):
    m = jnp.maximum(jnp.max(s, axis=-1, keepdims=True), sink)
    p = jnp.exp(s - m)
    return p, jnp.sum(p, axis=-1, keepdims=True) + jnp.exp(sink - m)


def _bias_kernel(tab_ref, bp_ref, bs_ref, op_ref, os_ref, *, seq):
    bk = bp_ref[...]
    for h in range(H_ATTN):
        acc = jnp.full(bk.shape, -jnp.inf, F32)
        for bkt in range(NUM_BUCKETS):
            acc = jnp.where(bk == bkt, tab_ref[bkt, h], acc)
        op_ref[h] = acc
    bk = bs_ref[...]
    row = lax.broadcasted_iota(jnp.int32, bk.shape, 0)
    for kv in range(KV_HEADS):
        acc = jnp.full(bk.shape, -jnp.inf, F32)
        for bkt in range(NUM_BUCKETS):
            val = jnp.full(bk.shape, tab_ref[bkt, kv * GROUP], F32)
            for gi in range(1, GROUP):
                val = jnp.where(row >= gi * seq, tab_ref[bkt, kv * GROUP + gi], val)
            acc = jnp.where(bk == bkt, val, acc)
        os_ref[kv] = acc


def _t5_bucket(rel):
    n = jnp.maximum(rel, 0)
    max_exact = NUM_BUCKETS // 2
    nf = jnp.maximum(n, max_exact).astype(F32)
    large = max_exact + (jnp.log(nf / max_exact) / math.log(MAX_DISTANCE / max_exact)
                         * (NUM_BUCKETS - max_exact)).astype(jnp.int32)
    large = jnp.minimum(large, NUM_BUCKETS - 1)
    return jnp.where(n < max_exact, n, large)


def _bias_tables(rel_bias, dec_seq, w_buf):
    qi = jnp.arange(ATTN_BLOCK, dtype=jnp.int32)[:, None]
    kc = jnp.arange(2 * ATTN_BLOCK, dtype=jnp.int32)[None, :]
    rel_p = qi + ATTN_BLOCK - kc
    bucket_p = jnp.where((rel_p >= 0) & (rel_p <= WINDOW), _t5_bucket(rel_p), -1)
    ts = (jnp.arange(GROUP * dec_seq, dtype=jnp.int32) % dec_seq)[:, None]
    js = jnp.arange(2 * ATTN_BLOCK, dtype=jnp.int32)[None, :]
    rel_s = w_buf + ts - js
    ok_s = (rel_s >= 0) & (rel_s <= WINDOW) & (js < w_buf + dec_seq)
    bucket_s = jnp.where(ok_s, _t5_bucket(rel_s), -1)
    return pl.pallas_call(
        functools.partial(_bias_kernel, seq=dec_seq),
        out_shape=(jax.ShapeDtypeStruct((H_ATTN, ATTN_BLOCK, 2 * ATTN_BLOCK), F32),
                   jax.ShapeDtypeStruct((KV_HEADS, GROUP * dec_seq, 2 * ATTN_BLOCK), F32)),
        in_specs=[pl.BlockSpec(memory_space=pltpu.SMEM),
                  pl.BlockSpec(memory_space=pltpu.VMEM),
                  pl.BlockSpec(memory_space=pltpu.VMEM)],
        out_specs=(pl.BlockSpec(memory_space=pltpu.VMEM), pl.BlockSpec(memory_space=pltpu.VMEM)),
        name="rel_bias_tables",
    )(rel_bias.astype(F32), bucket_p, bucket_s)


def _mod_kernel(ca_ref, cb_ref, w_ref, b_ref, o_ref):
    w = w_ref[...].astype(BF16)
    na = ca_ref.shape[0]
    for c_ref, rows in ((ca_ref, slice(0, na)), (cb_ref, slice(na, na + cb_ref.shape[0]))):
        o_ref[rows, :] = jnp.dot(_silu(c_ref[...]).astype(BF16), w, preferred_element_type=F32) + b_ref[...]


def _modulation(c_a, c_b, w_ada, b_ada):
    depth = w_ada.shape[0]
    n = c_a.shape[0] + c_b.shape[0]
    return pl.pallas_call(
        _mod_kernel,
        out_shape=jax.ShapeDtypeStruct((depth, n, N_MOD * D_MODEL), F32),
        grid=(depth, N_MOD * D_MODEL // MOD_COLS),
        in_specs=[pl.BlockSpec(c_a.shape, lambda l, j: (0, 0)),
                  pl.BlockSpec(c_b.shape, lambda l, j: (0, 0)),
                  pl.BlockSpec((None, D_MODEL, MOD_COLS), lambda l, j: (l, 0, j)),
                  pl.BlockSpec((None, 1, MOD_COLS), lambda l, j: (l, 0, j))],
        out_specs=pl.BlockSpec((None, n, MOD_COLS), lambda l, j: (l, 0, j)),
        compiler_params=pltpu.CompilerParams(dimension_semantics=("arbitrary", "arbitrary"),
                                             vmem_limit_bytes=VMEM_LIMIT),
        name="adaln_modulation",
    )(c_a, c_b, w_ada, b_ada.reshape(depth, 1, N_MOD * D_MODEL))


def _mod_rows(m, n_tokens):
    if m.shape[0] == 1:
        return m
    reps = n_tokens // m.shape[0]
    return jnp.broadcast_to(m[:, None, :], (m.shape[0], reps, m.shape[1])).reshape(n_tokens, m.shape[1])


def _modulated_norm(x, g, sc, sh):
    n = x.shape[0]
    return (_rms_rows(x) * (g * (1.0 + _mod_rows(sc, n))) + _mod_rows(sh, n)).astype(BF16)


def _sample_mod_specs(n_rows, chunks, index_map):
    return [pl.BlockSpec((None, n_rows, D_MODEL), functools.partial(index_map, chunk=c),
                         pipeline_mode=pl.Buffered(1)) for c in chunks]


def _mlp_tile(mix, x, g1, sh, sc, g2, ng, wout_ref, wup_ref, wdn_ref, fg, final):
    n = x.shape[0]
    x1 = x + _mod_rows(g1, n) * jnp.dot(mix.astype(BF16), wout_ref[...], preferred_element_type=F32)
    h = _modulated_norm(x1, ng, sc, sh)
    acc = None
    for c in range(D_FF // FF_CHUNK):
        u = jnp.dot(h, wup_ref[:, c * FF_CHUNK:(c + 1) * FF_CHUNK], preferred_element_type=F32)
        u = jnp.square(jnp.maximum(u, 0.0)).astype(BF16)
        d = jnp.dot(u, wdn_ref[c * FF_CHUNK:(c + 1) * FF_CHUNK, :], preferred_element_type=F32)
        acc = d if acc is None else acc + d
    x2 = x1 + _mod_rows(g2, n) * acc
    return _rms_rows(x2) * fg if final else x2


def _mlp_kernel(mix_ref, x_ref, g1_ref, sh_ref, sc_ref, g2_ref, smix_ref, sx_ref, sg1_ref, ssh_ref, ssc_ref, sg2_ref,
                ng_ref, wout_ref, wup_ref, wdn_ref, fg_ref, o_ref, so_ref, wout_s, wup_s, wdn_s, *, layer, final):
    s = pl.program_id(0)

    @pl.when(s < CAST_STEPS)
    def _():
        for src, dst in ((wout_ref, wout_s), (wup_ref, wup_s), (wdn_ref, wdn_s)):
            rows = src.shape[0]
            dst[pl.ds(pl.multiple_of(s * rows, rows), rows), :] = src[...].astype(BF16)

    weights = (ng_ref[layer:layer + 1, :], wout_s, wup_s, wdn_s, fg_ref[...], final)

    @pl.when(s >= CAST_STEPS)
    def _():
        o_ref[...] = _mlp_tile(mix_ref[...], x_ref[...], g1_ref[...], sh_ref[...], sc_ref[...], g2_ref[...], *weights)

    @pl.when(s == pl.num_programs(0) - 1)
    def _():
        so_ref[...] = _mlp_tile(smix_ref[...], sx_ref[...], sg1_ref[...], ssh_ref[...], ssc_ref[...], sg2_ref[...],
                                *weights)


def _out_mlp(mix2, x2, mod, mix_s, xs, mod_all, n_sample, norm_g, w_out, w_up, w_down, final_g, layer,
             tiles_per_batch, final):
    n = x2.shape[0]
    tile = TOK_TILE
    const = lambda i: (0, 0)
    tok = lambda i: (jnp.maximum(i - CAST_STEPS, 0), 0)
    chunk_of_layer = lambda i: (layer, jnp.minimum(i, CAST_STEPS - 1), 0)
    mod_p = lambda chunk: pl.BlockSpec((None, 1, D_MODEL),
                                       lambda i: (jnp.maximum(i - CAST_STEPS, 0) // tiles_per_batch, 0, chunk))
    whole = lambda a: pl.BlockSpec(a.shape, const, pipeline_mode=pl.Buffered(1))
    return pl.pallas_call(
        functools.partial(_mlp_kernel, layer=layer, final=final),
        out_shape=(jax.ShapeDtypeStruct((n, D_MODEL), F32), jax.ShapeDtypeStruct(xs.shape, F32)),
        grid=(CAST_STEPS + n // tile,),
        in_specs=[pl.BlockSpec((tile, D_MODEL), tok),
                  pl.BlockSpec((tile, D_MODEL), tok),
                  mod_p(2), mod_p(3), mod_p(4), mod_p(5),
                  whole(mix_s), whole(xs)]
                 + _sample_mod_specs(n_sample, (2, 3, 4, 5), lambda i, chunk: (layer, 0, chunk)) + [
                  pl.BlockSpec(norm_g.shape, const),
                  pl.BlockSpec((None, D_MODEL // CAST_STEPS, D_MODEL), chunk_of_layer),
                  pl.BlockSpec((None, D_MODEL // CAST_STEPS, D_FF), chunk_of_layer),
                  pl.BlockSpec((None, D_FF // CAST_STEPS, D_MODEL), chunk_of_layer),
                  pl.BlockSpec((1, D_MODEL), const)],
        out_specs=(pl.BlockSpec((tile, D_MODEL), tok), pl.BlockSpec(xs.shape, const, pipeline_mode=pl.Buffered(1))),
        scratch_shapes=[pltpu.VMEM((D_MODEL, D_MODEL), BF16),
                        pltpu.VMEM((D_MODEL, D_FF), BF16),
                        pltpu.VMEM((D_FF, D_MODEL), BF16)],
        compiler_params=pltpu.CompilerParams(dimension_semantics=("arbitrary",), vmem_limit_bytes=VMEM_LIMIT),
        name="out_projection_mlp",
    )(mix2, x2, mod, mod, mod, mod, mix_s, xs, mod_all, mod_all, mod_all, mod_all, norm_g, w_out,
      w_up, w_down, final_g.reshape(1, D_MODEL))


def _conv_ln_swish(acc, lng, lnb):
    mu = jnp.mean(acc, axis=-1, keepdims=True)
    xc = acc - mu
    y = xc * lax.rsqrt(jnp.mean(xc * xc, axis=-1, keepdims=True) + EPS) * lng + lnb
    return _silu(y)


def _hgrn_gates(fh, lb):
    f = lb + (1.0 - lb) * jax.nn.sigmoid(fh)
    return jnp.log(f), 1.0 - f


def _hgrn_out(o, hng, gate):
    return _rms_rows(o) * hng * _silu(gate)


def _ticker(pieces):
    it = iter(pieces)

    def tick():
        piece = next(it, None)
        if piece is not None:
            piece()

    def flush():
        for piece in it:
            piece()

    tick.flush = flush
    return tick


def _prompt_mix_kernel(sinks_ref, x_ref, sh_ref, sc_ref, sx_ref, ssh_ref, ssc_ref, ng_ref, win_ref, convw_ref,
                       convb_ref, lng_ref, lnb_ref, hlb_ref, hng_ref, bias_ref,
                       mix_ref, sproj_ref, convo_ref, so_ref, ko_ref, vo_ref,
                       proj_ref, wbf_ref, abuf, kbuf, vbuf, st_ref, *, layer, tile):
    t = pl.program_id(1)
    last = pl.num_programs(1) - 1

    @pl.when((pl.program_id(0) == 0) & (t == 0))
    def _():
        wbf_ref[...] = win_ref[...].astype(BF16)

    @pl.when(t == 0)
    def _():
        abuf[0:CONV_PAD, :] = jnp.zeros((CONV_PAD, D_CONV), F32)
        abuf[CONV_PAD + tile:CONV_PAD + tile + SUBLANES, :] = jnp.zeros((SUBLANES, D_CONV), F32)
        kbuf[0:ATTN_BLOCK, :] = jnp.zeros((ATTN_BLOCK, D_KV), BF16)
        vbuf[0:ATTN_BLOCK, :] = jnp.zeros((ATTN_BLOCK, D_KV), BF16)
        st_ref[...] = jnp.zeros(st_ref.shape, F32)

    h_in = _modulated_norm(x_ref[...], ng_ref[layer:layer + 1, :], sc_ref[...], sh_ref[...])
    for lo, hi in ((OFF_AVAL, OFF_Q), (OFF_F, OFF_I), (OFF_Q, OFF_F), (OFF_I, OFF_G), (OFF_G, OFF_QA),
                   (OFF_QA, IN_WIDTH)):
        proj_ref[:, lo:hi] = jnp.dot(h_in, wbf_ref[:, lo:hi], preferred_element_type=F32)

    kbuf[ATTN_BLOCK:ATTN_BLOCK + tile, :] = proj_ref[:, OFF_KA:OFF_KA + D_KV].astype(BF16)
    vbuf[ATTN_BLOCK:ATTN_BLOCK + tile, :] = proj_ref[:, OFF_VA:OFF_VA + D_KV].astype(BF16)
    scale = HEAD_DIM ** -0.5
    attn = {}

    def attn_scores(blk):
        def run():
            r0 = blk * ATTN_BLOCK
            for h in range(H_ATTN):
                kv = h // GROUP
                q = (proj_ref[r0:r0 + ATTN_BLOCK, OFF_QA + h * HEAD_DIM:OFF_QA + (h + 1) * HEAD_DIM]
                     * scale).astype(BF16)
                kall = kbuf[r0:r0 + 2 * ATTN_BLOCK, kv * HEAD_DIM:(kv + 1) * HEAD_DIM]
                attn[blk, h] = lax.dot_general(q, kall, NT_DIMS, preferred_element_type=F32)
        return run

    def attn_softmax(blk):
        def run():
            for h in range(H_ATTN):
                s = attn[blk, h] + bias_ref[h]
                if blk == 0:
                    col = lax.broadcasted_iota(jnp.int32, s.shape, 1)
                    s = jnp.where(col + (t * tile - ATTN_BLOCK) >= 0, s, -jnp.inf)
                p, den = _sink_softmax(s, sinks_ref[layer, h])
                attn[blk, h] = (p.astype(BF16), den)
        return run

    def attn_values(blk):
        def run():
            r0 = blk * ATTN_BLOCK
            heads = []
            for h in range(H_ATTN):
                kv = h // GROUP
                p, den = attn[blk, h]
                vall = vbuf[r0:r0 + 2 * ATTN_BLOCK, kv * HEAD_DIM:(kv + 1) * HEAD_DIM]
                heads.append(jnp.dot(p, vall, preferred_element_type=F32) / den)
            mix_ref[r0:r0 + ATTN_BLOCK, D_CONV + D_HGRN:D_MODEL] = jnp.concatenate(heads, axis=1).astype(BF16)
        return run

    tick = _ticker([stage(blk) for blk in range(tile // ATTN_BLOCK)
                    for stage in (attn_scores, attn_softmax, attn_values)])

    abuf[CONV_PAD:CONV_PAD + tile, :] = (proj_ref[:, OFF_AVAL:OFF_AVAL + D_CONV]
                                         * jax.nn.sigmoid(proj_ref[:, OFF_AGATE:OFF_AGATE + D_CONV]))
    first_row = CONV_PAD - (CONV_WIDTH - 1)
    acc = jnp.broadcast_to(convb_ref[layer:layer + 1, :], (tile, D_CONV))
    for r in range(SUBLANES):
        z = None
        for off in range(r, first_row + CONV_WIDTH, SUBLANES):
            j = off - first_row
            if j < 0:
                continue
            term = convw_ref[layer, j:j + 1, :] * abuf[off - r:off - r + tile + SUBLANES, :]
            z = term if z is None else z + term
        acc = acc + (z[0:tile] if r == 0 else pltpu.roll(z, tile + SUBLANES - r, 0)[0:tile])
    mix_ref[:, 0:D_CONV] = _conv_ln_swish(acc, lng_ref[layer:layer + 1, :], lnb_ref[layer:layer + 1, :]).astype(BF16)
    tick()

    lb = _layer_lb(hlb_ref[...], layer)
    hng = hng_ref[layer:layer + 1, :]
    span = HGRN_SPAN * HGRN_CHUNK
    ri = lax.broadcasted_iota(jnp.int32, (span, span), 0)
    ci = lax.broadcasted_iota(jnp.int32, (span, span), 1)
    tri = jnp.where((ri >= ci) & (ri // HGRN_CHUNK == ci // HGRN_CHUNK), 1.0, 0.0).astype(BF16)
    for i in range(tile // span):
        _hgrn_span(proj_ref, mix_ref, st_ref, i * span, lb, hng, tri, tick)
    tick.flush()

    @pl.when(t == last)
    def _():
        convo_ref[...] = abuf[CONV_PAD + tile - (CONV_WIDTH - 1):CONV_PAD + tile, :]
        for h in range(H_HGRN):
            so_ref[h] = st_ref[h].T
        ko_ref[...] = proj_ref[tile - WINDOW:tile, OFF_KA:OFF_KA + D_KV].T
        vo_ref[...] = proj_ref[tile - WINDOW:tile, OFF_VA:OFF_VA + D_KV].T

    abuf[0:CONV_PAD, :] = abuf[tile:tile + CONV_PAD, :]
    kbuf[0:ATTN_BLOCK, :] = kbuf[tile:tile + ATTN_BLOCK, :]
    vbuf[0:ATTN_BLOCK, :] = vbuf[tile:tile + ATTN_BLOCK, :]

    @pl.when((pl.program_id(0) == pl.num_programs(0) - 1) & (t == last))
    def _():
        h_s = _modulated_norm(sx_ref[...], ng_ref[layer:layer + 1, :], ssc_ref[...], ssh_ref[...])
        sproj_ref[...] = jnp.dot(h_s, wbf_ref[...], preferred_element_type=F32)


def _carry_specs(carried):
    return [pl.BlockSpec(memory_space=pl.ANY)] * len(carried)


def _without_carry(kernel_fn, n_in, n_carried, *refs):
    return kernel_fn(*refs[:n_in], *refs[n_in + n_carried:])


def _carried(kernel_fn, n_in, n_carried):
    return functools.partial(_without_carry, kernel_fn, n_in, n_carried)


def _prompt_mixers(x, mod, xs, mod_all, n_sample, norm_g, w_in, sinks, conv_w, conv_b, ln_g, ln_b, hgrn_lb, hng,
                   bias_p, layer, carried):
    B, T = x.shape[:2]
    tile = MIX_TILE
    depth = hgrn_lb.shape[0]
    const2 = lambda b, t: (0, 0)
    inputs = (sinks, x, mod, mod, xs, mod_all, mod_all, norm_g, w_in, conv_w, conv_b, ln_g, ln_b, hgrn_lb, hng, bias_p)
    return pl.pallas_call(
        _carried(functools.partial(_prompt_mix_kernel, layer=layer, tile=tile), len(inputs), len(carried)),
        out_shape=(jax.ShapeDtypeStruct((B, T, D_MODEL), BF16),
                   jax.ShapeDtypeStruct((xs.shape[0], IN_WIDTH), F32),
                   jax.ShapeDtypeStruct((depth, B, CONV_WIDTH - 1, D_CONV), F32),
                   jax.ShapeDtypeStruct((depth, B, H_HGRN, DK_HGRN, DV_HGRN), F32),
                   jax.ShapeDtypeStruct((depth, B, D_KV, WINDOW), F32),
                   jax.ShapeDtypeStruct((depth, B, D_KV, WINDOW), F32)),
        grid=(B, T // tile),
        in_specs=[pl.BlockSpec(memory_space=pltpu.SMEM),
                  pl.BlockSpec((None, tile, D_MODEL), lambda b, t: (b, t, 0)),
                  pl.BlockSpec((None, 1, D_MODEL), lambda b, t: (b, 0, 0)),
                  pl.BlockSpec((None, 1, D_MODEL), lambda b, t: (b, 0, 1)),
                  pl.BlockSpec(xs.shape, const2, pipeline_mode=pl.Buffered(1))]
                 + _sample_mod_specs(n_sample, (0, 1), lambda b, t, chunk: (layer, 0, chunk)) + [
                  pl.BlockSpec(norm_g.shape, const2),
                  pl.BlockSpec((None, D_MODEL, IN_WIDTH), lambda b, t: (layer, 0, 0), pipeline_mode=pl.Buffered(1)),
                  pl.BlockSpec(conv_w.shape, lambda b, t: (0, 0, 0)),
                  pl.BlockSpec(conv_b.shape, const2),
                  pl.BlockSpec(ln_g.shape, const2),
                  pl.BlockSpec(ln_b.shape, const2),
                  pl.BlockSpec(hgrn_lb.shape, const2),
                  pl.BlockSpec(hng.shape, const2),
                  pl.BlockSpec((H_ATTN, ATTN_BLOCK, 2 * ATTN_BLOCK), lambda b, t: (0, 0, 0))] + _carry_specs(carried),
        out_specs=(pl.BlockSpec((None, tile, D_MODEL), lambda b, t: (b, t, 0)),
                   pl.BlockSpec((xs.shape[0], IN_WIDTH), const2),
                   pl.BlockSpec((None, None, CONV_WIDTH - 1, D_CONV), lambda b, t: (layer, b, 0, 0)),
                   pl.BlockSpec((None, None, H_HGRN, DK_HGRN, DV_HGRN), lambda b, t: (layer, b, 0, 0, 0)),
                   pl.BlockSpec((None, None, D_KV, WINDOW), lambda b, t: (layer, b, 0, 0)),
                   pl.BlockSpec((None, None, D_KV, WINDOW), lambda b, t: (layer, b, 0, 0))),
        input_output_aliases={len(inputs) + i: 2 + i for i in range(len(carried))},
        scratch_shapes=[pltpu.VMEM((tile, IN_WIDTH), F32),
                        pltpu.VMEM((D_MODEL, IN_WIDTH), BF16),
                        pltpu.VMEM((CONV_PAD + tile + SUBLANES, D_CONV), F32),
                        pltpu.VMEM((ATTN_BLOCK + tile, D_KV), BF16),
                        pltpu.VMEM((ATTN_BLOCK + tile, D_KV), BF16),
                        pltpu.VMEM((H_HGRN, DV_HGRN, DK_HGRN), F32)],
        compiler_params=pltpu.CompilerParams(dimension_semantics=("arbitrary", "arbitrary"),
                                             vmem_limit_bytes=VMEM_LIMIT),
        name="prompt_mixers",
    )(*inputs, *carried)


def _sample_mix_kernel(sinks_ref, proj_ref, cconv_ref, state_ref, ck_ref, cv_ref, convw_ref, convb_ref, lng_ref,
                       lnb_ref, hlb_ref, hng_ref, bias_ref,
                       mix_ref, convo_ref, so_ref, ko_ref, vo_ref, kpad_ref, vpad_ref, *,
                       layer, block, seq, w_buf):
    hist = CONV_WIDTH - 1

    @pl.when(pl.program_id(0) == 0)
    def _():
        for ref in (kpad_ref, vpad_ref):
            ref[:, 0:w_buf - seq, :] = jnp.zeros((block, w_buf - seq, D_KV), F32)

    lb = _layer_lb(hlb_ref[...], layer)
    hng = hng_ref[layer:layer + 1, :]
    scale = HEAD_DIM ** -0.5
    elems = range(block)
    row8 = lax.broadcasted_iota(jnp.int32, (SUBLANES, DV_HGRN), 0)
    ones_rows = jnp.where((row8 >= seq) & (row8 < seq + 3), 1.0, 0.0)
    zrow = jnp.zeros((1, DK_HGRN), BF16)
    prow = lax.broadcasted_iota(jnp.int32, (seq, seq), 0)
    pcol = lax.broadcasted_iota(jnp.int32, (seq, seq), 1)
    grow = lax.broadcasted_iota(jnp.int32, (GROUP * seq, 1), 0)

    proj = [proj_ref[e * seq:(e + 1) * seq, :] for e in elems]

    glu = [p[:, OFF_AVAL:OFF_AVAL + D_CONV] * jax.nn.sigmoid(p[:, OFF_AGATE:OFF_AGATE + D_CONV]) for p in proj]
    full = [cconv_ref[i] for i in range(hist)]
    full += [jnp.concatenate([glu[e][t:t + 1] for e in elems], axis=0) for t in range(seq)]
    for i in range(hist):
        convo_ref[i] = full[i + seq]
    conv_out = []
    for t in range(seq):
        acc = jnp.broadcast_to(convb_ref[layer:layer + 1, :], (block, D_CONV))
        for j in range(CONV_WIDTH):
            acc = acc + convw_ref[layer, j:j + 1, :] * full[t + j]
        conv_out.append(_conv_ln_swish(acc, lng_ref[layer:layer + 1, :], lnb_ref[layer:layer + 1, :]))
    out_a = [jnp.concatenate([conv_out[t][e:e + 1] for t in range(seq)], axis=0) for e in elems]

    units = [(e, h) for e in elems for h in range(H_HGRN)]
    ops = {}
    for e in elems:
        p = proj[e]
        g, k = _hgrn_gates(p[:, OFF_F:OFF_F + D_HGRN], lb)
        b = _cumsum_rows_small(g)
        for h in range(H_HGRN):
            cs = slice(h * DK_HGRN, (h + 1) * DK_HGRN)
            q = p[:, OFF_Q + h * DK_HGRN:OFF_Q + (h + 1) * DK_HGRN]
            v = p[:, OFF_I + h * DV_HGRN:OFF_I + (h + 1) * DV_HGRN]
            bb = b[:, cs]
            bl = bb[seq - 1:seq, :]
            kst = (k[:, cs] * jnp.exp(bl - bb)).astype(BF16)
            x = jnp.concatenate([kst.astype(F32), *(part.astype(F32) for part in _split3_bf16(jnp.exp(bl))),
                                 zrow.astype(F32)], axis=0).astype(BF16)
            vpad = jnp.concatenate([v, jnp.zeros((SUBLANES - seq, DV_HGRN), F32)], axis=0)
            ops[e, h] = dict(qp=(q * jnp.exp(bb - bl)).astype(BF16), kst=kst, v=v.astype(BF16),
                             qt=(q * jnp.exp(bb)).astype(BF16), x=x,
                             r=jnp.concatenate([vpad, ones_rows], axis=1).astype(BF16))
    for u in units:
        o = ops[u]
        st = state_ref[u[0], u[1]]
        o["p"] = lax.dot_general(o["qp"], o["kst"], NT_DIMS, preferred_element_type=F32)
        o["inter"] = jnp.dot(o["qt"], st.astype(BF16), preferred_element_type=F32)
        me = lax.dot_general(o["x"], o["r"], TN_DIMS, preferred_element_type=F32)
        so_ref[u[0], u[1]] = me[:, DV_HGRN:] * st + me[:, :DV_HGRN]
    out_b = {}
    for u in units:
        o = ops[u]
        pm = jnp.where(prow >= pcol, o["p"], 0.0).astype(BF16)
        out = o["inter"] + jnp.dot(pm, o["v"], preferred_element_type=F32)
        gate = proj[u[0]][:, OFF_G + u[1] * DV_HGRN:OFF_G + (u[1] + 1) * DV_HGRN]
        out_b[u] = _hgrn_out(out, hng, gate)

    scores = {}
    for e in elems:
        p = proj[e]
        for kv in range(KV_HEADS):
            hs = slice(kv * HEAD_DIM, (kv + 1) * HEAD_DIM)
            q2 = jnp.concatenate([p[:, OFF_QA + h * HEAD_DIM:OFF_QA + (h + 1) * HEAD_DIM]
                                  for h in range(kv * GROUP, (kv + 1) * GROUP)], axis=0)
            q2 = (q2 * scale).astype(BF16)
            bias = bias_ref[kv]
            s_c = jnp.dot(q2, ck_ref[e, hs, :].astype(BF16), preferred_element_type=F32) + bias[:, 0:w_buf]
            s_n = (lax.dot_general(q2, p[:, OFF_KA + kv * HEAD_DIM:OFF_KA + (kv + 1) * HEAD_DIM].astype(BF16), NT_DIMS,
                                   preferred_element_type=F32) + bias[:, w_buf:w_buf + seq])
            scores[e, kv] = (s_c, s_n)
    out_c = {}
    for e in elems:
        p = proj[e]
        for kv in range(KV_HEADS):
            hs = slice(kv * HEAD_DIM, (kv + 1) * HEAD_DIM)
            sink = jnp.zeros((GROUP * seq, 1), F32)
            for gi in range(GROUP):
                sink = jnp.where(grow >= gi * seq, sinks_ref[layer, kv * GROUP + gi], sink)
            s_c, s_n = scores[e, kv]
            m = jnp.maximum(jnp.maximum(jnp.max(s_c, axis=-1, keepdims=True), jnp.max(s_n, axis=-1, keepdims=True)),
                            sink)
            p_c = jnp.exp(s_c - m)
            p_n = jnp.exp(s_n - m)
            den = jnp.sum(p_c, axis=-1, keepdims=True) + jnp.sum(p_n, axis=-1, keepdims=True) + jnp.exp(sink - m)
            vnew = p[:, OFF_VA + kv * HEAD_DIM:OFF_VA + (kv + 1) * HEAD_DIM].astype(BF16)
            o2 = (lax.dot_general(p_c.astype(BF16), cv_ref[e, hs, :].astype(BF16), NT_DIMS,
                                  preferred_element_type=F32)
                  + jnp.dot(p_n.astype(BF16), vnew, preferred_element_type=F32)) / den
            for gi in range(GROUP):
                out_c[e, kv * GROUP + gi] = o2[gi * seq:(gi + 1) * seq]
    lane = lax.broadcasted_iota(jnp.int32, (D_KV, w_buf), 1)
    for e in elems:
        p = proj[e]
        for pad, cache, new, out in ((kpad_ref, ck_ref, p[:, OFF_KA:OFF_KA + D_KV], ko_ref),
                                     (vpad_ref, cv_ref, p[:, OFF_VA:OFF_VA + D_KV], vo_ref)):
            pad[e, w_buf - seq:w_buf, :] = new
            out[e] = jnp.where(lane >= w_buf - seq, pad[e].T, pltpu.roll(cache[e], w_buf - seq, 1))

    for e in elems:
        parts = [out_a[e]] + [out_b[e, h] for h in range(H_HGRN)] + [out_c[e, h] for h in range(H_ATTN)]
        mix_ref[e * seq:(e + 1) * seq, :] = jnp.concatenate(parts, axis=1)


def _sample_mixers(proj2, sinks, cache_conv, state, cache_k, cache_v, conv_w, conv_b, ln_g, ln_b, hgrn_lb, hng,
                   bias_s, layer, carried):
    B = state.shape[1]
    seq = proj2.shape[0] // B
    w_buf = cache_k.shape[3]
    block = SAMPLE_BLOCK
    depth = hgrn_lb.shape[0]
    hist = CONV_WIDTH - 1
    const2 = lambda i: (0, 0)
    cache_specs = [pl.BlockSpec((None, hist, block, D_CONV), lambda i: (layer, 0, i, 0)),
                   pl.BlockSpec((None, block, H_HGRN, DK_HGRN, DV_HGRN), lambda i: (layer, i, 0, 0, 0)),
                   pl.BlockSpec((None, block, D_KV, w_buf), lambda i: (layer, i, 0, 0)),
                   pl.BlockSpec((None, block, D_KV, w_buf), lambda i: (layer, i, 0, 0))]
    inputs = (sinks, proj2, cache_conv, state, cache_k, cache_v, conv_w, conv_b, ln_g, ln_b, hgrn_lb, hng, bias_s)
    return pl.pallas_call(
        _carried(functools.partial(_sample_mix_kernel, layer=layer, block=block, seq=seq, w_buf=w_buf),
                 len(inputs), len(carried)),
        out_shape=(jax.ShapeDtypeStruct((B * seq, D_MODEL), F32),
                   jax.ShapeDtypeStruct((depth, hist, B, D_CONV), F32),
                   jax.ShapeDtypeStruct((depth, B, H_HGRN, DK_HGRN, DV_HGRN), F32),
                   jax.ShapeDtypeStruct((depth, B, D_KV, w_buf), F32),
                   jax.ShapeDtypeStruct((depth, B, D_KV, w_buf), F32)),
        grid=(B // block,),
        in_specs=[pl.BlockSpec(memory_space=pltpu.SMEM),
                  pl.BlockSpec((block * seq, IN_WIDTH), lambda i: (i, 0))] + cache_specs + [
                  pl.BlockSpec(conv_w.shape, lambda i: (0, 0, 0)),
                  pl.BlockSpec(conv_b.shape, const2),
                  pl.BlockSpec(ln_g.shape, const2),
                  pl.BlockSpec(ln_b.shape, const2),
                  pl.BlockSpec(hgrn_lb.shape, const2),
                  pl.BlockSpec(hng.shape, const2),
                  pl.BlockSpec((KV_HEADS, GROUP * seq, 2 * ATTN_BLOCK), lambda i: (0, 0, 0))] + _carry_specs(carried),
        out_specs=tuple([pl.BlockSpec((block * seq, D_MODEL), lambda i: (i, 0))] + cache_specs),
        input_output_aliases={len(inputs) + i: 1 + i for i in range(len(carried))},
        scratch_shapes=[pltpu.VMEM((block, w_buf, D_KV), F32),
                        pltpu.VMEM((block, w_buf, D_KV), F32)],
        compiler_params=pltpu.CompilerParams(dimension_semantics=("arbitrary",), vmem_limit_bytes=VMEM_LIMIT),
        name="sample_mixers",
    )(*inputs, *carried)


def kernel(x_prompt, x_sample, cache_conv, state_hgrn, cache_swa_k, cache_swa_v, c_prompt, c_sample, rel_bias, w_ada, b_ada, norm_mix_g, w_in, conv_w, conv_b, conv_ln_g, conv_ln_b, hgrn_lb, hgrn_norm_g, attn_sinks, w_out, norm_mlp_g, w_up, w_down, final_g):
    Bp, Tp = x_prompt.shape[:2]
    Bs, Ts = x_sample.shape[:2]
    depth = w_in.shape[0]
    w_buf = cache_swa_k.shape[2]
    assert Tp % MIX_TILE == 0 and (Bp * Tp) % TOK_TILE == 0 and Tp % TOK_TILE == 0 and Bs % SAMPLE_BLOCK == 0
    assert w_buf == WINDOW and GROUP * Ts == SUBLANES

    bias_p, bias_s = _bias_tables(rel_bias, Ts, w_buf)
    mod = _modulation(c_sample, c_prompt, w_ada, b_ada)
    hlb = hgrn_lb.astype(F32)
    cc = jnp.swapaxes(cache_conv, 1, 2)
    ck = jnp.swapaxes(cache_swa_k.reshape(depth, Bs, w_buf, D_KV), 2, 3)
    cv = jnp.swapaxes(cache_swa_v.reshape(depth, Bs, w_buf, D_KV), 2, 3)

    xp = x_prompt.reshape(Bp * Tp, D_MODEL)
    xs = x_sample.reshape(Bs * Ts, D_MODEL)
    caches_p = ()
    caches_s = ()
    for l in range(depth):
        final = l == depth - 1
        mod_p = mod[l, Bs:].reshape(Bp, 1, N_MOD * D_MODEL)
        mix_p, proj_s, *caches_p = _prompt_mixers(xp.reshape(Bp, Tp, D_MODEL), mod_p, xs, mod, Bs, norm_mix_g, w_in,
                                                  attn_sinks, conv_w, conv_b, conv_ln_g, conv_ln_b, hlb, hgrn_norm_g,
                                                  bias_p, l, caches_p)
        mix_s, *caches_s = _sample_mixers(proj_s, attn_sinks, cc, state_hgrn, ck, cv, conv_w, conv_b, conv_ln_g,
                                          conv_ln_b, hlb, hgrn_norm_g, bias_s, l, caches_s)
        xp, xs = _out_mlp(mix_p.reshape(Bp * Tp, D_MODEL), xp, mod_p, mix_s, xs, mod, Bs, norm_mlp_g, w_out,
                          w_up, w_down, final_g, l, Tp // TOK_TILE, final)
    cp, sp, kp, vp = caches_p
    kp, vp = jnp.swapaxes(kp, 2, 3), jnp.swapaxes(vp, 2, 3)
    cs, ss, ksn, vsn = caches_s
    cs, ksn, vsn = jnp.swapaxes(cs, 1, 2), jnp.swapaxes(ksn, 2, 3), jnp.swapaxes(vsn, 2, 3)
    return (xp.reshape(Bp, Tp, D_MODEL), xs.reshape(Bs, Ts, D_MODEL), cp, cs, sp, ss,
            kp.reshape(depth, Bp, WINDOW, KV_HEADS, HEAD_DIM), ksn.reshape(depth, Bs, w_buf, KV_HEADS, HEAD_DIM),
            vp.reshape(depth, Bp, WINDOW, KV_HEADS, HEAD_DIM), vsn.reshape(depth, Bs, w_buf, KV_HEADS, HEAD_DIM))
```

```python
import functools
import math

import jax
import jax.numpy as jnp
from jax import lax
from jax.experimental import pallas as pl
from jax.experimental.pallas import tpu as pltpu

F32 = jnp.float32
BF16 = jnp.bfloat16

D_MODEL = 1024
D_CONV = 256
CONV_WIDTH = 31
H_HGRN = 4
DK_HGRN = 128
DV_HGRN = 128
D_HGRN = 512
HEAD_DIM = 64
H_ATTN = 4
KV_HEADS = 2
GROUP = H_ATTN // KV_HEADS
D_ATTN = H_ATTN * HEAD_DIM
D_KV = KV_HEADS * HEAD_DIM
WINDOW = 128
ATTN_BLOCK = 128
NUM_BUCKETS = 32
MAX_DISTANCE = 128
D_FF = 4 * D_MODEL
N_MOD = 6
EPS = 1e-6

OFF_AVAL = 0
OFF_AGATE = OFF_AVAL + D_CONV
OFF_Q = OFF_AGATE + D_CONV
OFF_F = OFF_Q + H_HGRN * DK_HGRN
OFF_I = OFF_F + H_HGRN * DK_HGRN
OFF_G = OFF_I + D_HGRN
OFF_QA = OFF_G + D_HGRN
OFF_KA = OFF_QA + D_ATTN
OFF_VA = OFF_KA + D_KV
IN_WIDTH = OFF_VA + D_KV

HGRN_CHUNK = 64
HGRN_KEYBLOCK = 32
HGRN_SPAN = 4
SUBLANES = 8
CONV_PAD = 32
MIX_TILE = 512
TOK_TILE = 512
SAMPLE_BLOCK = 16
FF_CHUNK = 2048
CAST_STEPS = 8
MOD_COLS = 2048
VMEM_LIMIT = 56 * 1024 * 1024

NT_DIMS = (((1,), (1,)), ((), ()))
TN_DIMS = (((0,), (0,)), ((), ()))


def _silu(x):
    return x * jax.nn.sigmoid(x)


def _rms_rows(x):
    return x * lax.rsqrt(jnp.mean(x * x, axis=-1, keepdims=True) + EPS)


def _layer_lb(hlb, layer):
    m = jnp.max(hlb, axis=0, keepdims=True)
    e = jnp.exp(hlb - m)
    p = e / jnp.sum(e, axis=0, keepdims=True)
    lb = jnp.zeros_like(m)
    for i in range(1, layer + 1):
        lb = lb + p[i:i + 1, :]
    return lb


def _split3_bf16(x):
    hi = x.astype(BF16)
    r = x - hi.astype(F32)
    mid = r.astype(BF16)
    return hi, mid, (r - mid.astype(F32)).astype(BF16)


def _select_rows_mxu(sel, x):
    return sum(jnp.dot(sel, part, preferred_element_type=F32) for part in _split3_bf16(x))


def _cumsum_rows_small(g):
    row = lax.broadcasted_iota(jnp.int32, g.shape, 0)
    b = jnp.zeros_like(g)
    for u in range(g.shape[0]):
        b = b + jnp.where(row >= u, g[u:u + 1, :], 0.0)
    return b


def _hgrn_span(proj_ref, mix_ref, st_ref, row0, lb, hng, tri, tick):
    L, KB = HGRN_CHUNK, HGRN_KEYBLOCK
    span = HGRN_SPAN * L
    g, k = _hgrn_gates(proj_ref[pl.ds(row0, span), OFF_F:OFF_F + D_HGRN], lb)
    b = _select_rows_mxu(tri, g)
    units = [(c, h) for c in range(HGRN_SPAN) for h in range(H_HGRN)]

    ops = {}
    for c, h in units:
        rows = pl.ds(row0 + c * L, L)
        cs = slice(h * DK_HGRN, (h + 1) * DK_HGRN)
        q = proj_ref[rows, OFF_Q + h * DK_HGRN:OFF_Q + (h + 1) * DK_HGRN]
        v = proj_ref[rows, OFF_I + h * DV_HGRN:OFF_I + (h + 1) * DV_HGRN].astype(BF16)
        kk = k[c * L:(c + 1) * L, cs]
        bb = b[c * L:(c + 1) * L, cs]
        qp, kp = [], []
        for lo in range(0, L, KB):
            r = bb[lo + KB // 2 - 1:lo + KB // 2, :]
            kp.append((kk[lo:lo + KB] * jnp.exp(r - bb[lo:lo + KB])).astype(BF16))
            qp.append((q[lo:] * jnp.exp(bb[lo:] - r)).astype(BF16))
        bl = bb[L - 1:L, :]
        ops[c, h] = dict(qp=qp, kp=kp, v=v, qt=(q * jnp.exp(bb)).astype(BF16),
                         kst=(kk * jnp.exp(bl - bb)).astype(BF16), e=jnp.exp(bl))
    tick()

    for u in units:
        o = ops[u]
        o["p"] = [lax.dot_general(qp, kp, NT_DIMS, preferred_element_type=F32) for qp, kp in zip(o["qp"], o["kp"])]
        o["m"] = lax.dot_general(o["v"], o["kst"], TN_DIMS, preferred_element_type=F32)
    tick()

    for u in units:
        pm = []
        for p in ops[u]["p"]:
            row = lax.broadcasted_iota(jnp.int32, p.shape, 0)
            col = lax.broadcasted_iota(jnp.int32, p.shape, 1)
            pm.append(jnp.where(row >= col, p, 0.0).astype(BF16))
        ops[u]["p"] = pm
    tick()

    for u in units:
        o = ops[u]
        blocks = [None] * (L // KB)
        for j, p in enumerate(o["p"]):
            cj = jnp.dot(p, o["v"][j * KB:(j + 1) * KB], preferred_element_type=F32)
            for i in range(j, L // KB):
                piece = cj[(i - j) * KB:(i - j + 1) * KB]
                blocks[i] = piece if blocks[i] is None else blocks[i] + piece
        o["o"] = jnp.concatenate(blocks, axis=0)
    tick()

    for h in range(H_HGRN):
        st = st_ref[h]
        for c in range(HGRN_SPAN):
            o = ops[c, h]
            out = o["o"] + lax.dot_general(o["qt"], st.astype(BF16), NT_DIMS, preferred_element_type=F32)
            st = o["e"] * st + o["m"]
            rows = pl.ds(row0 + c * L, L)
            gate = proj_ref[rows, OFF_G + h * DV_HGRN:OFF_G + (h + 1) * DV_HGRN]
            mix_ref[rows, D_CONV + h * DV_HGRN:D_CONV + (h + 1) * DV_HGRN] = _hgrn_out(out, hng, gate).astype(BF16)
        st_ref[h] = st


def _sink_softmax(s, sink):
    m = jnp.maximum(jnp.max(s, axis=-1, keepdims=True), sink)
    p = jnp.exp(s - m)
    return p, jnp.sum(p, axis=-1, keepdims=True) + jnp.exp(sink - m)


def _bias_kernel(tab_ref, bp_ref, bs_ref, op_ref, os_ref, *, seq):
    bk = bp_ref[...]
    for h in range(H_ATTN):
        acc = jnp.full(bk.shape, -jnp.inf, F32)
        for bkt in range(NUM_BUCKETS):
            acc = jnp.where(bk == bkt, tab_ref[bkt, h], acc)
        op_ref[h] = acc
    bk = bs_ref[...]
    row = lax.broadcasted_iota(jnp.int32, bk.shape, 0)
    for kv in range(KV_HEADS):
        acc = jnp.full(bk.shape, -jnp.inf, F32)
        for bkt in range(NUM_BUCKETS):
            val = jnp.full(bk.shape, tab_ref[bkt, kv * GROUP], F32)
            for gi in range(1, GROUP):
                val = jnp.where(row >= gi * seq, tab_ref[bkt, kv * GROUP + gi], val)
            acc = jnp.where(bk == bkt, val, acc)
        os_ref[kv] = acc


def _t5_bucket(rel):
    n = jnp.maximum(rel, 0)
    max_exact = NUM_BUCKETS // 2
    nf = jnp.maximum(n, max_exact).astype(F32)
    large = max_exact + (jnp.log(nf / max_exact) / math.log(MAX_DISTANCE / max_exact)
                         * (NUM_BUCKETS - max_exact)).astype(jnp.int32)
    large = jnp.minimum(large, NUM_BUCKETS - 1)
    return jnp.where(n < max_exact, n, large)


def _bias_tables(rel_bias, dec_seq, w_buf):
    qi = jnp.arange(ATTN_BLOCK, dtype=jnp.int32)[:, None]
    kc = jnp.arange(2 * ATTN_BLOCK, dtype=jnp.int32)[None, :]
    rel_p = qi + ATTN_BLOCK - kc
    bucket_p = jnp.where((rel_p >= 0) & (rel_p <= WINDOW), _t5_bucket(rel_p), -1)
    ts = (jnp.arange(GROUP * dec_seq, dtype=jnp.int32) % dec_seq)[:, None]
    js = jnp.arange(2 * ATTN_BLOCK, dtype=jnp.int32)[None, :]
    rel_s = w_buf + ts - js
    ok_s = (rel_s >= 0) & (rel_s <= WINDOW) & (js < w_buf + dec_seq)
    bucket_s = jnp.where(ok_s, _t5_bucket(rel_s), -1)
    return pl.pallas_call(
        functools.partial(_bias_kernel, seq=dec_seq),
        out_shape=(jax.ShapeDtypeStruct((H_ATTN, ATTN_BLOCK, 2 * ATTN_BLOCK), F32),
                   jax.ShapeDtypeStruct((KV_HEADS, GROUP * dec_seq, 2 * ATTN_BLOCK), F32)),
        in_specs=[pl.BlockSpec(memory_space=pltpu.SMEM),
                  pl.BlockSpec(memory_space=pltpu.VMEM),
                  pl.BlockSpec(memory_space=pltpu.VMEM)],
        out_specs=(pl.BlockSpec(memory_space=pltpu.VMEM), pl.BlockSpec(memory_space=pltpu.VMEM)),
        name="rel_bias_tables",
    )(rel_bias.astype(F32), bucket_p, bucket_s)


def _mod_kernel(ca_ref, cb_ref, w_ref, b_ref, o_ref):
    w = w_ref[...].astype(BF16)
    na = ca_ref.shape[0]
    for c_ref, rows in ((ca_ref, slice(0, na)), (cb_ref, slice(na, na + cb_ref.shape[0]))):
        o_ref[rows, :] = jnp.dot(_silu(c_ref[...]).astype(BF16), w, preferred_element_type=F32) + b_ref[...]


def _modulation(c_a, c_b, w_ada, b_ada):
    depth = w_ada.shape[0]
    n = c_a.shape[0] + c_b.shape[0]
    return pl.pallas_call(
        _mod_kernel,
        out_shape=jax.ShapeDtypeStruct((depth, n, N_MOD * D_MODEL), F32),
        grid=(depth, N_MOD * D_MODEL // MOD_COLS),
        in_specs=[pl.BlockSpec(c_a.shape, lambda l, j: (0, 0)),
                  pl.BlockSpec(c_b.shape, lambda l, j: (0, 0)),
                  pl.BlockSpec((None, D_MODEL, MOD_COLS), lambda l, j: (l, 0, j)),
                  pl.BlockSpec((None, 1, MOD_COLS), lambda l, j: (l, 0, j))],
        out_specs=pl.BlockSpec((None, n, MOD_COLS), lambda l, j: (l, 0, j)),
        compiler_params=pltpu.CompilerParams(dimension_semantics=("arbitrary", "arbitrary"),
                                             vmem_limit_bytes=VMEM_LIMIT),
        name="adaln_modulation",
    )(c_a, c_b, w_ada, b_ada.reshape(depth, 1, N_MOD * D_MODEL))


def _mod_rows(m, n_tokens):
    if m.shape[0] == 1:
        return m
    reps = n_tokens // m.shape[0]
    return jnp.broadcast_to(m[:, None, :], (m.shape[0], reps, m.shape[1])).reshape(n_tokens, m.shape[1])


def _modulated_norm(x, g, sc, sh):
    n = x.shape[0]
    return (_rms_rows(x) * (g * (1.0 + _mod_rows(sc, n))) + _mod_rows(sh, n)).astype(BF16)


def _sample_mod_specs(n_rows, chunks, index_map):
    return [pl.BlockSpec((None, n_rows, D_MODEL), functools.partial(index_map, chunk=c),
                         pipeline_mode=pl.Buffered(1)) for c in chunks]


def _mlp_tile(mix, x, g1, sh, sc, g2, ng, wout_ref, wup_ref, wdn_ref, fg, final):
    n = x.shape[0]
    x1 = x + _mod_rows(g1, n) * jnp.dot(mix.astype(BF16), wout_ref[...], preferred_element_type=F32)
    h = _modulated_norm(x1, ng, sc, sh)
    acc = None
    for c in range(D_FF // FF_CHUNK):
        u = jnp.dot(h, wup_ref[:, c * FF_CHUNK:(c + 1) * FF_CHUNK], preferred_element_type=F32)
        u = jnp.square(jnp.maximum(u, 0.0)).astype(BF16)
        d = jnp.dot(u, wdn_ref[c * FF_CHUNK:(c + 1) * FF_CHUNK, :], preferred_element_type=F32)
        acc = d if acc is None else acc + d
    x2 = x1 + _mod_rows(g2, n) * acc
    return _rms_rows(x2) * fg if final else x2


def _mlp_kernel(mix_ref, x_ref, g1_ref, sh_ref, sc_ref, g2_ref, smix_ref, sx_ref, sg1_ref, ssh_ref, ssc_ref, sg2_ref,
                ng_ref, wout_ref, wup_ref, wdn_ref, fg_ref, o_ref, so_ref, wout_s, wup_s, wdn_s, *, layer, final):
    s = pl.program_id(0)

    @pl.when(s < CAST_STEPS)
    def _():
        for src, dst in ((wout_ref, wout_s), (wup_ref, wup_s), (wdn_ref, wdn_s)):
            rows = src.shape[0]
            dst[pl.ds(pl.multiple_of(s * rows, rows), rows), :] = src[...].astype(BF16)

    weights = (ng_ref[layer:layer + 1, :], wout_s, wup_s, wdn_s, fg_ref[...], final)

    @pl.when(s >= CAST_STEPS)
    def _():
        o_ref[...] = _mlp_tile(mix_ref[...], x_ref[...], g1_ref[...], sh_ref[...], sc_ref[...], g2_ref[...], *weights)

    @pl.when(s == pl.num_programs(0) - 1)
    def _():
        sx = sx_ref[...].reshape(-1, D_MODEL)
        so_ref[...] = _mlp_tile(smix_ref[...], sx, sg1_ref[...], ssh_ref[...], ssc_ref[...], sg2_ref[...],
                                *weights).reshape(so_ref.shape)


def _out_mlp(mix2, x2, mod, mix_s, xs, xs_out_shape, mod_all, n_sample, norm_g, w_out, w_up, w_down, final_g, layer,
             tiles_per_batch, final):
    n = x2.shape[0]
    tile = TOK_TILE
    const = lambda i: (0, 0)
    tok = lambda i: (jnp.maximum(i - CAST_STEPS, 0), 0)
    chunk_of_layer = lambda i: (layer, jnp.minimum(i, CAST_STEPS - 1), 0)
    mod_p = lambda chunk: pl.BlockSpec((None, 1, D_MODEL),
                                       lambda i: (jnp.maximum(i - CAST_STEPS, 0) // tiles_per_batch, 0, chunk))
    whole = lambda shape: pl.BlockSpec(shape, lambda i: (0,) * len(shape), pipeline_mode=pl.Buffered(1))
    return pl.pallas_call(
        functools.partial(_mlp_kernel, layer=layer, final=final),
        out_shape=(jax.ShapeDtypeStruct((n, D_MODEL), F32), jax.ShapeDtypeStruct(xs_out_shape, F32)),
        grid=(CAST_STEPS + n // tile,),
        in_specs=[pl.BlockSpec((tile, D_MODEL), tok),
                  pl.BlockSpec((tile, D_MODEL), tok),
                  mod_p(2), mod_p(3), mod_p(4), mod_p(5),
                  whole(mix_s.shape), whole(xs.shape)]
                 + _sample_mod_specs(n_sample, (2, 3, 4, 5), lambda i, chunk: (layer, 0, chunk)) + [
                  pl.BlockSpec(norm_g.shape, const),
                  pl.BlockSpec((None, D_MODEL // CAST_STEPS, D_MODEL), chunk_of_layer),
                  pl.BlockSpec((None, D_MODEL // CAST_STEPS, D_FF), chunk_of_layer),
                  pl.BlockSpec((None, D_FF // CAST_STEPS, D_MODEL), chunk_of_layer),
                  pl.BlockSpec((1, D_MODEL), const)],
        out_specs=(pl.BlockSpec((tile, D_MODEL), tok), whole(xs_out_shape)),
        scratch_shapes=[pltpu.VMEM((D_MODEL, D_MODEL), BF16),
                        pltpu.VMEM((D_MODEL, D_FF), BF16),
                        pltpu.VMEM((D_FF, D_MODEL), BF16)],
        compiler_params=pltpu.CompilerParams(dimension_semantics=("arbitrary",), vmem_limit_bytes=VMEM_LIMIT),
        name="out_projection_mlp",
    )(mix2, x2, mod, mod, mod, mod, mix_s, xs, mod_all, mod_all, mod_all, mod_all, norm_g, w_out,
      w_up, w_down, final_g.reshape(1, D_MODEL))


def _conv_ln_swish(acc, lng, lnb):
    mu = jnp.mean(acc, axis=-1, keepdims=True)
    xc = acc - mu
    y = xc * lax.rsqrt(jnp.mean(xc * xc, axis=-1, keepdims=True) + EPS) * lng + lnb
    return _silu(y)


def _hgrn_gates(fh, lb):
    f = lb + (1.0 - lb) * jax.nn.sigmoid(fh)
    return jnp.log(f), 1.0 - f


def _hgrn_out(o, hng, gate):
    return _rms_rows(o) * hng * _silu(gate)


def _ticker(pieces):
    it = iter(pieces)

    def tick():
        piece = next(it, None)
        if piece is not None:
            piece()

    def flush():
        for piece in it:
            piece()

    tick.flush = flush
    return tick


def _prompt_mix_kernel(sinks_ref, x_ref, sh_ref, sc_ref, sx_ref, ssh_ref, ssc_ref, ng_ref, win_ref, convw_ref,
                       convb_ref, lng_ref, lnb_ref, hlb_ref, hng_ref, bias_ref,
                       mix_ref, sproj_ref, convo_ref, so_ref, ko_ref, vo_ref,
                       proj_ref, wbf_ref, abuf, kbuf, vbuf, st_ref, *, layer, tile):
    t = pl.program_id(1)
    last = pl.num_programs(1) - 1

    @pl.when((pl.program_id(0) == 0) & (t == 0))
    def _():
        wbf_ref[...] = win_ref[...].astype(BF16)

    @pl.when(t == 0)
    def _():
        abuf[0:CONV_PAD, :] = jnp.zeros((CONV_PAD, D_CONV), F32)
        abuf[CONV_PAD + tile:CONV_PAD + tile + SUBLANES, :] = jnp.zeros((SUBLANES, D_CONV), F32)
        kbuf[0:ATTN_BLOCK, :] = jnp.zeros((ATTN_BLOCK, D_KV), BF16)
        vbuf[0:ATTN_BLOCK, :] = jnp.zeros((ATTN_BLOCK, D_KV), BF16)
        st_ref[...] = jnp.zeros(st_ref.shape, F32)

    h_in = _modulated_norm(x_ref[...], ng_ref[layer:layer + 1, :], sc_ref[...], sh_ref[...])
    for lo, hi in ((OFF_AVAL, OFF_Q), (OFF_F, OFF_I), (OFF_Q, OFF_F), (OFF_I, OFF_G), (OFF_G, OFF_QA),
                   (OFF_QA, IN_WIDTH)):
        proj_ref[:, lo:hi] = jnp.dot(h_in, wbf_ref[:, lo:hi], preferred_element_type=F32)

    kbuf[ATTN_BLOCK:ATTN_BLOCK + tile, :] = proj_ref[:, OFF_KA:OFF_KA + D_KV].astype(BF16)
    vbuf[ATTN_BLOCK:ATTN_BLOCK + tile, :] = proj_ref[:, OFF_VA:OFF_VA + D_KV].astype(BF16)
    scale = HEAD_DIM ** -0.5
    attn = {}

    def attn_scores(blk):
        def run():
            r0 = blk * ATTN_BLOCK
            for h in range(H_ATTN):
                kv = h // GROUP
                q = (proj_ref[r0:r0 + ATTN_BLOCK, OFF_QA + h * HEAD_DIM:OFF_QA + (h + 1) * HEAD_DIM]
                     * scale).astype(BF16)
                kall = kbuf[r0:r0 + 2 * ATTN_BLOCK, kv * HEAD_DIM:(kv + 1) * HEAD_DIM]
                attn[blk, h] = lax.dot_general(q, kall, NT_DIMS, preferred_element_type=F32)
        return run

    def attn_softmax(blk):
        def run():
            for h in range(H_ATTN):
                s = attn[blk, h] + bias_ref[h]
                if blk == 0:
                    col = lax.broadcasted_iota(jnp.int32, s.shape, 1)
                    s = jnp.where(col + (t * tile - ATTN_BLOCK) >= 0, s, -jnp.inf)
                p, den = _sink_softmax(s, sinks_ref[layer, h])
                attn[blk, h] = (p.astype(BF16), den)
        return run

    def attn_values(blk):
        def run():
            r0 = blk * ATTN_BLOCK
            heads = []
            for h in range(H_ATTN):
                kv = h // GROUP
                p, den = attn[blk, h]
                vall = vbuf[r0:r0 + 2 * ATTN_BLOCK, kv * HEAD_DIM:(kv + 1) * HEAD_DIM]
                heads.append(jnp.dot(p, vall, preferred_element_type=F32) / den)
            mix_ref[r0:r0 + ATTN_BLOCK, D_CONV + D_HGRN:D_MODEL] = jnp.concatenate(heads, axis=1).astype(BF16)
        return run

    tick = _ticker([stage(blk) for blk in range(tile // ATTN_BLOCK)
                    for stage in (attn_scores, attn_softmax, attn_values)])

    abuf[CONV_PAD:CONV_PAD + tile, :] = (proj_ref[:, OFF_AVAL:OFF_AVAL + D_CONV]
                                         * jax.nn.sigmoid(proj_ref[:, OFF_AGATE:OFF_AGATE + D_CONV]))
    first_row = CONV_PAD - (CONV_WIDTH - 1)
    acc = jnp.broadcast_to(convb_ref[layer:layer + 1, :], (tile, D_CONV))
    for r in range(SUBLANES):
        z = None
        for off in range(r, first_row + CONV_WIDTH, SUBLANES):
            j = off - first_row
            if j < 0:
                continue
            term = convw_ref[layer, j:j + 1, :] * abuf[off - r:off - r + tile + SUBLANES, :]
            z = term if z is None else z + term
        acc = acc + (z[0:tile] if r == 0 else pltpu.roll(z, tile + SUBLANES - r, 0)[0:tile])
    mix_ref[:, 0:D_CONV] = _conv_ln_swish(acc, lng_ref[layer:layer + 1, :], lnb_ref[layer:layer + 1, :]).astype(BF16)
    tick()

    lb = _layer_lb(hlb_ref[...], layer)
    hng = hng_ref[layer:layer + 1, :]
    span = HGRN_SPAN * HGRN_CHUNK
    ri = lax.broadcasted_iota(jnp.int32, (span, span), 0)
    ci = lax.broadcasted_iota(jnp.int32, (span, span), 1)
    tri = jnp.where((ri >= ci) & (ri // HGRN_CHUNK == ci // HGRN_CHUNK), 1.0, 0.0).astype(BF16)
    for i in range(tile // span):
        _hgrn_span(proj_ref, mix_ref, st_ref, i * span, lb, hng, tri, tick)
    tick.flush()

    @pl.when(t == last)
    def _():
        convo_ref[...] = abuf[CONV_PAD + tile - (CONV_WIDTH - 1):CONV_PAD + tile, :]
        for h in range(H_HGRN):
            so_ref[h] = st_ref[h].T
        ko_ref[...] = proj_ref[tile - WINDOW:tile, OFF_KA:OFF_KA + D_KV].T
        vo_ref[...] = proj_ref[tile - WINDOW:tile, OFF_VA:OFF_VA + D_KV].T

    abuf[0:CONV_PAD, :] = abuf[tile:tile + CONV_PAD, :]
    kbuf[0:ATTN_BLOCK, :] = kbuf[tile:tile + ATTN_BLOCK, :]
    vbuf[0:ATTN_BLOCK, :] = vbuf[tile:tile + ATTN_BLOCK, :]

    @pl.when((pl.program_id(0) == pl.num_programs(0) - 1) & (t == last))
    def _():
        h_s = _modulated_norm(sx_ref[...].reshape(-1, D_MODEL), ng_ref[layer:layer + 1, :], ssc_ref[...], ssh_ref[...])
        sproj_ref[...] = jnp.dot(h_s, wbf_ref[...], preferred_element_type=F32)


def _carry_specs(carried):
    return [pl.BlockSpec(memory_space=pl.ANY)] * len(carried)


def _without_carry(kernel_fn, n_in, n_carried, *refs):
    return kernel_fn(*refs[:n_in], *refs[n_in + n_carried:])


def _carried(kernel_fn, n_in, n_carried):
    return functools.partial(_without_carry, kernel_fn, n_in, n_carried)


def _prompt_mixers(x, mod, xs, mod_all, n_sample, norm_g, w_in, sinks, conv_w, conv_b, ln_g, ln_b, hgrn_lb, hng,
                   bias_p, layer, carried):
    B, T = x.shape[:2]
    tile = MIX_TILE
    depth = hgrn_lb.shape[0]
    n_tok_s = math.prod(xs.shape[:-1])
    const2 = lambda b, t: (0, 0)
    inputs = (sinks, x, mod, mod, xs, mod_all, mod_all, norm_g, w_in, conv_w, conv_b, ln_g, ln_b, hgrn_lb, hng, bias_p)
    return pl.pallas_call(
        _carried(functools.partial(_prompt_mix_kernel, layer=layer, tile=tile), len(inputs), len(carried)),
        out_shape=(jax.ShapeDtypeStruct((B, T, D_MODEL), BF16),
                   jax.ShapeDtypeStruct((n_tok_s, IN_WIDTH), F32),
                   jax.ShapeDtypeStruct((depth, B, CONV_WIDTH - 1, D_CONV), F32),
                   jax.ShapeDtypeStruct((depth, B, H_HGRN, DK_HGRN, DV_HGRN), F32),
                   jax.ShapeDtypeStruct((depth, B, D_KV, WINDOW), F32),
                   jax.ShapeDtypeStruct((depth, B, D_KV, WINDOW), F32)),
        grid=(B, T // tile),
        in_specs=[pl.BlockSpec(memory_space=pltpu.SMEM),
                  pl.BlockSpec((None, tile, D_MODEL), lambda b, t: (b, t, 0)),
                  pl.BlockSpec((None, 1, D_MODEL), lambda b, t: (b, 0, 0)),
                  pl.BlockSpec((None, 1, D_MODEL), lambda b, t: (b, 0, 1)),
                  pl.BlockSpec(xs.shape, lambda b, t: (0,) * xs.ndim, pipeline_mode=pl.Buffered(1))]
                 + _sample_mod_specs(n_sample, (0, 1), lambda b, t, chunk: (layer, 0, chunk)) + [
                  pl.BlockSpec(norm_g.shape, const2),
                  pl.BlockSpec((None, D_MODEL, IN_WIDTH), lambda b, t: (layer, 0, 0), pipeline_mode=pl.Buffered(1)),
                  pl.BlockSpec(conv_w.shape, lambda b, t: (0, 0, 0)),
                  pl.BlockSpec(conv_b.shape, const2),
                  pl.BlockSpec(ln_g.shape, const2),
                  pl.BlockSpec(ln_b.shape, const2),
                  pl.BlockSpec(hgrn_lb.shape, const2),
                  pl.BlockSpec(hng.shape, const2),
                  pl.BlockSpec((H_ATTN, ATTN_BLOCK, 2 * ATTN_BLOCK), lambda b, t: (0, 0, 0))] + _carry_specs(carried),
        out_specs=(pl.BlockSpec((None, tile, D_MODEL), lambda b, t: (b, t, 0)),
                   pl.BlockSpec((n_tok_s, IN_WIDTH), const2),
                   pl.BlockSpec((None, None, CONV_WIDTH - 1, D_CONV), lambda b, t: (layer, b, 0, 0)),
                   pl.BlockSpec((None, None, H_HGRN, DK_HGRN, DV_HGRN), lambda b, t: (layer, b, 0, 0, 0)),
                   pl.BlockSpec((None, None, D_KV, WINDOW), lambda b, t: (layer, b, 0, 0)),
                   pl.BlockSpec((None, None, D_KV, WINDOW), lambda b, t: (layer, b, 0, 0))),
        input_output_aliases={len(inputs) + i: 2 + i for i in range(len(carried))},
        scratch_shapes=[pltpu.VMEM((tile, IN_WIDTH), F32),
                        pltpu.VMEM((D_MODEL, IN_WIDTH), BF16),
                        pltpu.VMEM((CONV_PAD + tile + SUBLANES, D_CONV), F32),
                        pltpu.VMEM((ATTN_BLOCK + tile, D_KV), BF16),
                        pltpu.VMEM((ATTN_BLOCK + tile, D_KV), BF16),
                        pltpu.VMEM((H_HGRN, DV_HGRN, DK_HGRN), F32)],
        compiler_params=pltpu.CompilerParams(dimension_semantics=("arbitrary", "arbitrary"),
                                             vmem_limit_bytes=VMEM_LIMIT),
        name="prompt_mixers",
    )(*inputs, *carried)


def _sample_mix_kernel(sinks_ref, proj_ref, cconv_ref, state_ref, ck_ref, cv_ref, convw_ref, convb_ref, lng_ref,
                       lnb_ref, hlb_ref, hng_ref, bias_ref,
                       mix_ref, convo_ref, so_ref, ko_ref, vo_ref, kpad_ref, vpad_ref, *,
                       layer, block, seq, w_buf):
    hist = CONV_WIDTH - 1

    @pl.when(pl.program_id(0) == 0)
    def _():
        for ref in (kpad_ref, vpad_ref):
            ref[:, 0:w_buf - seq, :] = jnp.zeros((block, w_buf - seq, D_KV), F32)

    lb = _layer_lb(hlb_ref[...], layer)
    hng = hng_ref[layer:layer + 1, :]
    scale = HEAD_DIM ** -0.5
    elems = range(block)
    row8 = lax.broadcasted_iota(jnp.int32, (SUBLANES, DV_HGRN), 0)
    ones_rows = jnp.where((row8 >= seq) & (row8 < seq + 3), 1.0, 0.0)
    zrow = jnp.zeros((1, DK_HGRN), BF16)
    prow = lax.broadcasted_iota(jnp.int32, (seq, seq), 0)
    pcol = lax.broadcasted_iota(jnp.int32, (seq, seq), 1)
    grow = lax.broadcasted_iota(jnp.int32, (GROUP * seq, 1), 0)

    proj = [proj_ref[e * seq:(e + 1) * seq, :] for e in elems]

    glu = [p[:, OFF_AVAL:OFF_AVAL + D_CONV] * jax.nn.sigmoid(p[:, OFF_AGATE:OFF_AGATE + D_CONV]) for p in proj]
    full = [cconv_ref[i] for i in range(hist)]
    full += [jnp.concatenate([glu[e][t:t + 1] for e in elems], axis=0) for t in range(seq)]
    for i in range(hist):
        convo_ref[i] = full[i + seq]
    conv_out = []
    for t in range(seq):
        acc = jnp.broadcast_to(convb_ref[layer:layer + 1, :], (block, D_CONV))
        for j in range(CONV_WIDTH):
            acc = acc + convw_ref[layer, j:j + 1, :] * full[t + j]
        conv_out.append(_conv_ln_swish(acc, lng_ref[layer:layer + 1, :], lnb_ref[layer:layer + 1, :]))
    out_a = [jnp.concatenate([conv_out[t][e:e + 1] for t in range(seq)], axis=0) for e in elems]

    units = [(e, h) for e in elems for h in range(H_HGRN)]
    ops = {}
    for e in elems:
        p = proj[e]
        g, k = _hgrn_gates(p[:, OFF_F:OFF_F + D_HGRN], lb)
        b = _cumsum_rows_small(g)
        for h in range(H_HGRN):
            cs = slice(h * DK_HGRN, (h + 1) * DK_HGRN)
            q = p[:, OFF_Q + h * DK_HGRN:OFF_Q + (h + 1) * DK_HGRN]
            v = p[:, OFF_I + h * DV_HGRN:OFF_I + (h + 1) * DV_HGRN]
            bb = b[:, cs]
            bl = bb[seq - 1:seq, :]
            kst = (k[:, cs] * jnp.exp(bl - bb)).astype(BF16)
            x = jnp.concatenate([kst.astype(F32), *(part.astype(F32) for part in _split3_bf16(jnp.exp(bl))),
                                 zrow.astype(F32)], axis=0).astype(BF16)
            vpad = jnp.concatenate([v, jnp.zeros((SUBLANES - seq, DV_HGRN), F32)], axis=0)
            ops[e, h] = dict(qp=(q * jnp.exp(bb - bl)).astype(BF16), kst=kst, v=v.astype(BF16),
                             qt=(q * jnp.exp(bb)).astype(BF16), x=x,
                             r=jnp.concatenate([vpad, ones_rows], axis=1).astype(BF16))
    for u in units:
        o = ops[u]
        st = state_ref[u[0], u[1]]
        o["p"] = lax.dot_general(o["qp"], o["kst"], NT_DIMS, preferred_element_type=F32)
        o["inter"] = jnp.dot(o["qt"], st.astype(BF16), preferred_element_type=F32)
        me = lax.dot_general(o["x"], o["r"], TN_DIMS, preferred_element_type=F32)
        so_ref[u[0], u[1]] = me[:, DV_HGRN:] * st + me[:, :DV_HGRN]
    out_b = {}
    for u in units:
        o = ops[u]
        pm = jnp.where(prow >= pcol, o["p"], 0.0).astype(BF16)
        out = o["inter"] + jnp.dot(pm, o["v"], preferred_element_type=F32)
        gate = proj[u[0]][:, OFF_G + u[1] * DV_HGRN:OFF_G + (u[1] + 1) * DV_HGRN]
        out_b[u] = _hgrn_out(out, hng, gate)

    scores = {}
    for e in elems:
        p = proj[e]
        for kv in range(KV_HEADS):
            hs = slice(kv * HEAD_DIM, (kv + 1) * HEAD_DIM)
            q2 = jnp.concatenate([p[:, OFF_QA + h * HEAD_DIM:OFF_QA + (h + 1) * HEAD_DIM]
                                  for h in range(kv * GROUP, (kv + 1) * GROUP)], axis=0)
            q2 = (q2 * scale).astype(BF16)
            bias = bias_ref[kv]
            s_c = jnp.dot(q2, ck_ref[e, hs, :].astype(BF16), preferred_element_type=F32) + bias[:, 0:w_buf]
            s_n = (lax.dot_general(q2, p[:, OFF_KA + kv * HEAD_DIM:OFF_KA + (kv + 1) * HEAD_DIM].astype(BF16), NT_DIMS,
                                   preferred_element_type=F32) + bias[:, w_buf:w_buf + seq])
            scores[e, kv] = (s_c, s_n)
    out_c = {}
    for e in elems:
        p = proj[e]
        for kv in range(KV_HEADS):
            hs = slice(kv * HEAD_DIM, (kv + 1) * HEAD_DIM)
            sink = jnp.zeros((GROUP * seq, 1), F32)
            for gi in range(GROUP):
                sink = jnp.where(grow >= gi * seq, sinks_ref[layer, kv * GROUP + gi], sink)
            s_c, s_n = scores[e, kv]
            m = jnp.maximum(jnp.maximum(jnp.max(s_c, axis=-1, keepdims=True), jnp.max(s_n, axis=-1, keepdims=True)),
                            sink)
            p_c = jnp.exp(s_c - m)
            p_n = jnp.exp(s_n - m)
            den = jnp.sum(p_c, axis=-1, keepdims=True) + jnp.sum(p_n, axis=-1, keepdims=True) + jnp.exp(sink - m)
            vnew = p[:, OFF_VA + kv * HEAD_DIM:OFF_VA + (kv + 1) * HEAD_DIM].astype(BF16)
            o2 = (lax.dot_general(p_c.astype(BF16), cv_ref[e, hs, :].astype(BF16), NT_DIMS,
                                  preferred_element_type=F32)
                  + jnp.dot(p_n.astype(BF16), vnew, preferred_element_type=F32)) / den
            for gi in range(GROUP):
                out_c[e, kv * GROUP + gi] = o2[gi * seq:(gi + 1) * seq]
    lane = lax.broadcasted_iota(jnp.int32, (D_KV, w_buf), 1)
    for e in elems:
        p = proj[e]
        for pad, cache, new, out in ((kpad_ref, ck_ref, p[:, OFF_KA:OFF_KA + D_KV], ko_ref),
                                     (vpad_ref, cv_ref, p[:, OFF_VA:OFF_VA + D_KV], vo_ref)):
            pad[e, w_buf - seq:w_buf, :] = new
            out[e] = jnp.where(lane >= w_buf - seq, pad[e].T, pltpu.roll(cache[e], w_buf - seq, 1))

    for e in elems:
        parts = [out_a[e]] + [out_b[e, h] for h in range(H_HGRN)] + [out_c[e, h] for h in range(H_ATTN)]
        mix_ref[e * seq:(e + 1) * seq, :] = jnp.concatenate(parts, axis=1)


def _sample_mixers(proj2, sinks, cache_conv, state, cache_k, cache_v, conv_w, conv_b, ln_g, ln_b, hgrn_lb, hng,
                   bias_s, layer, carried):
    B = state.shape[1]
    seq = proj2.shape[0] // B
    w_buf = cache_k.shape[3]
    block = SAMPLE_BLOCK
    depth = hgrn_lb.shape[0]
    hist = CONV_WIDTH - 1
    const2 = lambda i: (0, 0)
    cache_specs = [pl.BlockSpec((None, hist, block, D_CONV), lambda i: (layer, 0, i, 0)),
                   pl.BlockSpec((None, block, H_HGRN, DK_HGRN, DV_HGRN), lambda i: (layer, i, 0, 0, 0)),
                   pl.BlockSpec((None, block, D_KV, w_buf), lambda i: (layer, i, 0, 0)),
                   pl.BlockSpec((None, block, D_KV, w_buf), lambda i: (layer, i, 0, 0))]
    inputs = (sinks, proj2, cache_conv, state, cache_k, cache_v, conv_w, conv_b, ln_g, ln_b, hgrn_lb, hng, bias_s)
    return pl.pallas_call(
        _carried(functools.partial(_sample_mix_kernel, layer=layer, block=block, seq=seq, w_buf=w_buf),
                 len(inputs), len(carried)),
        out_shape=(jax.ShapeDtypeStruct((B * seq, D_MODEL), F32),
                   jax.ShapeDtypeStruct((depth, hist, B, D_CONV), F32),
                   jax.ShapeDtypeStruct((depth, B, H_HGRN, DK_HGRN, DV_HGRN), F32),
                   jax.ShapeDtypeStruct((depth, B, D_KV, w_buf), F32),
                   jax.ShapeDtypeStruct((depth, B, D_KV, w_buf), F32)),
        grid=(B // block,),
        in_specs=[pl.BlockSpec(memory_space=pltpu.SMEM),
                  pl.BlockSpec((block * seq, IN_WIDTH), lambda i: (i, 0))] + cache_specs + [
                  pl.BlockSpec(conv_w.shape, lambda i: (0, 0, 0)),
                  pl.BlockSpec(conv_b.shape, const2),
                  pl.BlockSpec(ln_g.shape, const2),
                  pl.BlockSpec(ln_b.shape, const2),
                  pl.BlockSpec(hgrn_lb.shape, const2),
                  pl.BlockSpec(hng.shape, const2),
                  pl.BlockSpec((KV_HEADS, GROUP * seq, 2 * ATTN_BLOCK), lambda i: (0, 0, 0))] + _carry_specs(carried),
        out_specs=tuple([pl.BlockSpec((block * seq, D_MODEL), lambda i: (i, 0))] + cache_specs),
        input_output_aliases={len(inputs) + i: 1 + i for i in range(len(carried))},
        scratch_shapes=[pltpu.VMEM((block, w_buf, D_KV), F32),
                        pltpu.VMEM((block, w_buf, D_KV), F32)],
        compiler_params=pltpu.CompilerParams(dimension_semantics=("arbitrary",), vmem_limit_bytes=VMEM_LIMIT),
        name="sample_mixers",
    )(*inputs, *carried)


def kernel(x_prompt, x_sample, cache_conv, state_hgrn, cache_swa_k, cache_swa_v, c_prompt, c_sample, rel_bias, w_ada, b_ada, norm_mix_g, w_in, conv_w, conv_b, conv_ln_g, conv_ln_b, hgrn_lb, hgrn_norm_g, attn_sinks, w_out, norm_mlp_g, w_up, w_down, final_g):
    Bp, Tp = x_prompt.shape[:2]
    Bs, Ts = x_sample.shape[:2]
    depth = w_in.shape[0]
    w_buf = cache_swa_k.shape[2]
    assert Tp % MIX_TILE == 0 and (Bp * Tp) % TOK_TILE == 0 and Tp % TOK_TILE == 0 and Bs % SAMPLE_BLOCK == 0
    assert w_buf == WINDOW and GROUP * Ts == SUBLANES

    bias_p, bias_s = _bias_tables(rel_bias, Ts, w_buf)
    mod = _modulation(c_sample, c_prompt, w_ada, b_ada)
    hlb = hgrn_lb.astype(F32)
    cc = jnp.swapaxes(cache_conv, 1, 2)
    ck = jnp.swapaxes(cache_swa_k.reshape(depth, Bs, w_buf, D_KV), 2, 3)
    cv = jnp.swapaxes(cache_swa_v.reshape(depth, Bs, w_buf, D_KV), 2, 3)

    xp = x_prompt.reshape(Bp * Tp, D_MODEL)
    xs = x_sample
    caches_p = ()
    caches_s = ()
    for l in range(depth):
        final = l == depth - 1
        mod_p = mod[l, Bs:].reshape(Bp, 1, N_MOD * D_MODEL)
        mix_p, proj_s, *caches_p = _prompt_mixers(xp.reshape(Bp, Tp, D_MODEL), mod_p, xs, mod, Bs, norm_mix_g, w_in,
                                                  attn_sinks, conv_w, conv_b, conv_ln_g, conv_ln_b, hlb, hgrn_norm_g,
                                                  bias_p, l, caches_p)
        mix_s, *caches_s = _sample_mixers(proj_s, attn_sinks, cc, state_hgrn, ck, cv, conv_w, conv_b, conv_ln_g,
                                          conv_ln_b, hlb, hgrn_norm_g, bias_s, l, caches_s)
        xs_out_shape = x_sample.shape if final else (Bs * Ts, D_MODEL)
        xp, xs = _out_mlp(mix_p.reshape(Bp * Tp, D_MODEL), xp, mod_p, mix_s, xs, xs_out_shape, mod, Bs, norm_mlp_g, w_out,
                          w_up, w_down, final_g, l, Tp // TOK_TILE, final)
    cp, sp, kp, vp = caches_p
    kp, vp = jnp.swapaxes(kp, 2, 3), jnp.swapaxes(vp, 2, 3)
    cs, ss, ksn, vsn = caches_s
    cs, ksn, vsn = jnp.swapaxes(cs, 1, 2), jnp.swapaxes(ksn, 2, 3), jnp.swapaxes(vsn, 2, 3)
    return (xp.reshape(Bp, Tp, D_MODEL), xs, cp, cs, sp, ss,
            kp.reshape(depth, Bp, WINDOW, KV_HEADS, HEAD_DIM), ksn.reshape(depth, Bs, w_buf, KV_HEADS, HEAD_DIM),
            vp.reshape(depth, Bp, WINDOW, KV_HEADS, HEAD_DIM), vsn.reshape(depth, Bs, w_buf, KV_HEADS, HEAD_DIM))
```

```python
import functools
import math

import jax
import jax.numpy as jnp
from jax import lax
from jax.experimental import pallas as pl
from jax.experimental.pallas import tpu as pltpu

F32 = jnp.float32
BF16 = jnp.bfloat16

D_MODEL = 1024
D_CONV = 256
CONV_WIDTH = 31
H_HGRN = 4
DK_HGRN = 128
DV_HGRN = 128
D_HGRN = 512
HEAD_DIM = 64
H_ATTN = 4
KV_HEADS = 2
GROUP = H_ATTN // KV_HEADS
D_ATTN = H_ATTN * HEAD_DIM
D_KV = KV_HEADS * HEAD_DIM
WINDOW = 128
ATTN_BLOCK = 128
NUM_BUCKETS = 32
MAX_DISTANCE = 128
D_FF = 4 * D_MODEL
N_MOD = 6
EPS = 1e-6

OFF_AVAL = 0
OFF_AGATE = OFF_AVAL + D_CONV
OFF_Q = OFF_AGATE + D_CONV
OFF_F = OFF_Q + H_HGRN * DK_HGRN
OFF_I = OFF_F + H_HGRN * DK_HGRN
OFF_G = OFF_I + D_HGRN
OFF_QA = OFF_G + D_HGRN
OFF_KA = OFF_QA + D_ATTN
OFF_VA = OFF_KA + D_KV
IN_WIDTH = OFF_VA + D_KV

HGRN_CHUNK = 64
HGRN_KEYBLOCK = 32
HGRN_SPAN = 4
SUBLANES = 8
CONV_PAD = 32
MIX_TILE = 512
TOK_TILE = 512
SAMPLE_BLOCK = 16
FF_CHUNK = 2048
CAST_STEPS = 8
MOD_COLS = 2048
VMEM_LIMIT = 56 * 1024 * 1024

NT_DIMS = (((1,), (1,)), ((), ()))
TN_DIMS = (((0,), (0,)), ((), ()))


def _silu(x):
    return x * jax.nn.sigmoid(x)


def _rms_rows(x):
    return x * lax.rsqrt(jnp.mean(x * x, axis=-1, keepdims=True) + EPS)


def _layer_lb(hlb, layer):
    m = jnp.max(hlb, axis=0, keepdims=True)
    e = jnp.exp(hlb - m)
    p = e / jnp.sum(e, axis=0, keepdims=True)
    lb = jnp.zeros_like(m)
    for i in range(1, layer + 1):
        lb = lb + p[i:i + 1, :]
    return lb


def _split3_bf16(x):
    hi = x.astype(BF16)
    r = x - hi.astype(F32)
    mid = r.astype(BF16)
    return hi, mid, (r - mid.astype(F32)).astype(BF16)


def _select_rows_mxu(sel, x):
    return sum(jnp.dot(sel, part, preferred_element_type=F32) for part in _split3_bf16(x))


def _cumsum_rows_small(g):
    row = lax.broadcasted_iota(jnp.int32, g.shape, 0)
    b = jnp.zeros_like(g)
    for u in range(g.shape[0]):
        b = b + jnp.where(row >= u, g[u:u + 1, :], 0.0)
    return b


def _hgrn_span(proj_ref, mix_ref, st_ref, row0, lb, hng, tri, tick):
    L, KB = HGRN_CHUNK, HGRN_KEYBLOCK
    span = HGRN_SPAN * L
    g, k = _hgrn_gates(proj_ref[pl.ds(row0, span), OFF_F:OFF_F + D_HGRN], lb)
    b = _select_rows_mxu(tri, g)
    units = [(c, h) for c in range(HGRN_SPAN) for h in range(H_HGRN)]

    ops = {}
    for c, h in units:
        rows = pl.ds(row0 + c * L, L)
        cs = slice(h * DK_HGRN, (h + 1) * DK_HGRN)
        q = proj_ref[rows, OFF_Q + h * DK_HGRN:OFF_Q + (h + 1) * DK_HGRN]
        v = proj_ref[rows, OFF_I + h * DV_HGRN:OFF_I + (h + 1) * DV_HGRN].astype(BF16)
        kk = k[c * L:(c + 1) * L, cs]
        bb = b[c * L:(c + 1) * L, cs]
        refs = [bb[lo + KB // 2 - 1:lo + KB // 2, :] for lo in range(0, L, KB)]
        bl = bb[L - 1:L, :]
        kpf = [kk[j * KB:(j + 1) * KB] * jnp.exp(r - bb[j * KB:(j + 1) * KB]) for j, r in enumerate(refs)]
        own = [q[j * KB:(j + 1) * KB] * jnp.exp(bb[j * KB:(j + 1) * KB] - r) for j, r in enumerate(refs)]
        qpf = [jnp.concatenate([own[j]] + [own[i] * jnp.exp(refs[i] - refs[j]) for i in range(j + 1, len(refs))], axis=0)
               for j in range(len(refs))]
        kst = jnp.concatenate([kpf[j] * jnp.exp(bl - r) for j, r in enumerate(refs)], axis=0)
        ops[c, h] = dict(qp=[x.astype(BF16) for x in qpf], kp=[x.astype(BF16) for x in kpf], v=v,
                         qt=(qpf[0] * jnp.exp(refs[0])).astype(BF16), kst=kst.astype(BF16), e=jnp.exp(bl))
    tick()

    for u in units:
        o = ops[u]
        o["p"] = [lax.dot_general(qp, kp, NT_DIMS, preferred_element_type=F32) for qp, kp in zip(o["qp"], o["kp"])]
        o["m"] = lax.dot_general(o["v"], o["kst"], TN_DIMS, preferred_element_type=F32)
    tick()

    for u in units:
        pm = []
        for p in ops[u]["p"]:
            row = lax.broadcasted_iota(jnp.int32, p.shape, 0)
            col = lax.broadcasted_iota(jnp.int32, p.shape, 1)
            pm.append(jnp.where(row >= col, p, 0.0).astype(BF16))
        ops[u]["p"] = pm
    tick()

    for u in units:
        o = ops[u]
        blocks = [None] * (L // KB)
        for j, p in enumerate(o["p"]):
            cj = jnp.dot(p, o["v"][j * KB:(j + 1) * KB], preferred_element_type=F32)
            for i in range(j, L // KB):
                piece = cj[(i - j) * KB:(i - j + 1) * KB]
                blocks[i] = piece if blocks[i] is None else blocks[i] + piece
        o["o"] = jnp.concatenate(blocks, axis=0)
    tick()

    for h in range(H_HGRN):
        st = st_ref[h]
        for c in range(HGRN_SPAN):
            o = ops[c, h]
            out = o["o"] + lax.dot_general(o["qt"], st.astype(BF16), NT_DIMS, preferred_element_type=F32)
            st = o["e"] * st + o["m"]
            rows = pl.ds(row0 + c * L, L)
            gate = proj_ref[rows, OFF_G + h * DV_HGRN:OFF_G + (h + 1) * DV_HGRN]
            mix_ref[rows, D_CONV + h * DV_HGRN:D_CONV + (h + 1) * DV_HGRN] = _hgrn_out(out, hng, gate).astype(BF16)
        st_ref[h] = st


def _sink_softmax(s, sink):
    m = jnp.maximum(jnp.max(s, axis=-1, keepdims=True), sink)
    p = jnp.exp(s - m)
    return p, jnp.sum(p, axis=-1, keepdims=True) + jnp.exp(sink - m)


def _bias_kernel(tab_ref, bp_ref, bs_ref, op_ref, os_ref, *, seq):
    bk = bp_ref[...]
    for h in range(H_ATTN):
        acc = jnp.full(bk.shape, -jnp.inf, F32)
        for bkt in range(NUM_BUCKETS):
            acc = jnp.where(bk == bkt, tab_ref[bkt, h], acc)
        op_ref[h] = acc
    bk = bs_ref[...]
    row = lax.broadcasted_iota(jnp.int32, bk.shape, 0)
    for kv in range(KV_HEADS):
        acc = jnp.full(bk.shape, -jnp.inf, F32)
        for bkt in range(NUM_BUCKETS):
            val = jnp.full(bk.shape, tab_ref[bkt, kv * GROUP], F32)
            for gi in range(1, GROUP):
                val = jnp.where(row >= gi * seq, tab_ref[bkt, kv * GROUP + gi], val)
            acc = jnp.where(bk == bkt, val, acc)
        os_ref[kv] = acc


def _t5_bucket(rel):
    n = jnp.maximum(rel, 0)
    max_exact = NUM_BUCKETS // 2
    nf = jnp.maximum(n, max_exact).astype(F32)
    large = max_exact + (jnp.log(nf / max_exact) / math.log(MAX_DISTANCE / max_exact)
                         * (NUM_BUCKETS - max_exact)).astype(jnp.int32)
    large = jnp.minimum(large, NUM_BUCKETS - 1)
    return jnp.where(n < max_exact, n, large)


def _bias_tables(rel_bias, dec_seq, w_buf):
    qi = jnp.arange(ATTN_BLOCK, dtype=jnp.int32)[:, None]
    kc = jnp.arange(2 * ATTN_BLOCK, dtype=jnp.int32)[None, :]
    rel_p = qi + ATTN_BLOCK - kc
    bucket_p = jnp.where((rel_p >= 0) & (rel_p <= WINDOW), _t5_bucket(rel_p), -1)
    ts = (jnp.arange(GROUP * dec_seq, dtype=jnp.int32) % dec_seq)[:, None]
    js = jnp.arange(2 * ATTN_BLOCK, dtype=jnp.int32)[None, :]
    rel_s = w_buf + ts - js
    ok_s = (rel_s >= 0) & (rel_s <= WINDOW) & (js < w_buf + dec_seq)
    bucket_s = jnp.where(ok_s, _t5_bucket(rel_s), -1)
    return pl.pallas_call(
        functools.partial(_bias_kernel, seq=dec_seq),
        out_shape=(jax.ShapeDtypeStruct((H_ATTN, ATTN_BLOCK, 2 * ATTN_BLOCK), F32),
                   jax.ShapeDtypeStruct((KV_HEADS, GROUP * dec_seq, 2 * ATTN_BLOCK), F32)),
        in_specs=[pl.BlockSpec(memory_space=pltpu.SMEM),
                  pl.BlockSpec(memory_space=pltpu.VMEM),
                  pl.BlockSpec(memory_space=pltpu.VMEM)],
        out_specs=(pl.BlockSpec(memory_space=pltpu.VMEM), pl.BlockSpec(memory_space=pltpu.VMEM)),
        name="rel_bias_tables",
    )(rel_bias.astype(F32), bucket_p, bucket_s)


def _mod_kernel(ca_ref, cb_ref, w_ref, b_ref, o_ref):
    w = w_ref[...].astype(BF16)
    na = ca_ref.shape[0]
    for c_ref, rows in ((ca_ref, slice(0, na)), (cb_ref, slice(na, na + cb_ref.shape[0]))):
        o_ref[rows, :] = jnp.dot(_silu(c_ref[...]).astype(BF16), w, preferred_element_type=F32) + b_ref[...]


def _modulation(c_a, c_b, w_ada, b_ada):
    depth = w_ada.shape[0]
    n = c_a.shape[0] + c_b.shape[0]
    return pl.pallas_call(
        _mod_kernel,
        out_shape=jax.ShapeDtypeStruct((depth, n, N_MOD * D_MODEL), F32),
        grid=(depth, N_MOD * D_MODEL // MOD_COLS),
        in_specs=[pl.BlockSpec(c_a.shape, lambda l, j: (0, 0)),
                  pl.BlockSpec(c_b.shape, lambda l, j: (0, 0)),
                  pl.BlockSpec((None, D_MODEL, MOD_COLS), lambda l, j: (l, 0, j)),
                  pl.BlockSpec((None, 1, MOD_COLS), lambda l, j: (l, 0, j))],
        out_specs=pl.BlockSpec((None, n, MOD_COLS), lambda l, j: (l, 0, j)),
        compiler_params=pltpu.CompilerParams(dimension_semantics=("arbitrary", "arbitrary"),
                                             vmem_limit_bytes=VMEM_LIMIT),
        name="adaln_modulation",
    )(c_a, c_b, w_ada, b_ada.reshape(depth, 1, N_MOD * D_MODEL))


def _mod_rows(m, n_tokens):
    if m.shape[0] == 1:
        return m
    reps = n_tokens // m.shape[0]
    return jnp.broadcast_to(m[:, None, :], (m.shape[0], reps, m.shape[1])).reshape(n_tokens, m.shape[1])


def _modulated_norm(x, g, sc, sh):
    n = x.shape[0]
    return (_rms_rows(x) * (g * (1.0 + _mod_rows(sc, n))) + _mod_rows(sh, n)).astype(BF16)


def _sample_mod_specs(n_rows, chunks, index_map):
    return [pl.BlockSpec((None, n_rows, D_MODEL), functools.partial(index_map, chunk=c),
                         pipeline_mode=pl.Buffered(1)) for c in chunks]


def _mlp_tile(mix, x, g1, sh, sc, g2, ng, wout_ref, wup_ref, wdn_ref, fg, final):
    n = x.shape[0]
    x1 = x + _mod_rows(g1, n) * jnp.dot(mix.astype(BF16), wout_ref[...], preferred_element_type=F32)
    h = _modulated_norm(x1, ng, sc, sh)
    acc = None
    for c in range(D_FF // FF_CHUNK):
        u = jnp.dot(h, wup_ref[:, c * FF_CHUNK:(c + 1) * FF_CHUNK], preferred_element_type=F32)
        u = jnp.square(jnp.maximum(u, 0.0)).astype(BF16)
        d = jnp.dot(u, wdn_ref[c * FF_CHUNK:(c + 1) * FF_CHUNK, :], preferred_element_type=F32)
        acc = d if acc is None else acc + d
    x2 = x1 + _mod_rows(g2, n) * acc
    return _rms_rows(x2) * fg if final else x2


def _mlp_kernel(mix_ref, x_ref, g1_ref, sh_ref, sc_ref, g2_ref, smix_ref, sx_ref, sg1_ref, ssh_ref, ssc_ref, sg2_ref,
                ng_ref, wout_ref, wup_ref, wdn_ref, fg_ref, o_ref, so_ref, wout_s, wup_s, wdn_s, *, layer, final):
    s = pl.program_id(0)

    @pl.when(s < CAST_STEPS)
    def _():
        for src, dst in ((wout_ref, wout_s), (wup_ref, wup_s), (wdn_ref, wdn_s)):
            rows = src.shape[0]
            dst[pl.ds(pl.multiple_of(s * rows, rows), rows), :] = src[...].astype(BF16)

    weights = (ng_ref[layer:layer + 1, :], wout_s, wup_s, wdn_s, fg_ref[...], final)

    @pl.when(s >= CAST_STEPS)
    def _():
        o_ref[...] = _mlp_tile(mix_ref[...], x_ref[...], g1_ref[...], sh_ref[...], sc_ref[...], g2_ref[...], *weights)

    @pl.when(s == pl.num_programs(0) - 1)
    def _():
        sx = sx_ref[...].reshape(-1, D_MODEL)
        so_ref[...] = _mlp_tile(smix_ref[...], sx, sg1_ref[...], ssh_ref[...], ssc_ref[...], sg2_ref[...],
                                *weights).reshape(so_ref.shape)


def _out_mlp(mix2, x2, mod, mix_s, xs, xs_out_shape, mod_all, n_sample, norm_g, w_out, w_up, w_down, final_g, layer,
             tiles_per_batch, final):
    n = x2.shape[0]
    tile = TOK_TILE
    const = lambda i: (0, 0)
    tok = lambda i: (jnp.maximum(i - CAST_STEPS, 0), 0)
    chunk_of_layer = lambda i: (layer, jnp.minimum(i, CAST_STEPS - 1), 0)
    mod_p = lambda chunk: pl.BlockSpec((None, 1, D_MODEL),
                                       lambda i: (jnp.maximum(i - CAST_STEPS, 0) // tiles_per_batch, 0, chunk))
    whole = lambda shape: pl.BlockSpec(shape, lambda i: (0,) * len(shape), pipeline_mode=pl.Buffered(1))
    return pl.pallas_call(
        functools.partial(_mlp_kernel, layer=layer, final=final),
        out_shape=(jax.ShapeDtypeStruct((n, D_MODEL), F32), jax.ShapeDtypeStruct(xs_out_shape, F32)),
        grid=(CAST_STEPS + n // tile,),
        in_specs=[pl.BlockSpec((tile, D_MODEL), tok),
                  pl.BlockSpec((tile, D_MODEL), tok),
                  mod_p(2), mod_p(3), mod_p(4), mod_p(5),
                  whole(mix_s.shape), whole(xs.shape)]
                 + _sample_mod_specs(n_sample, (2, 3, 4, 5), lambda i, chunk: (layer, 0, chunk)) + [
                  pl.BlockSpec(norm_g.shape, const),
                  pl.BlockSpec((None, D_MODEL // CAST_STEPS, D_MODEL), chunk_of_layer),
                  pl.BlockSpec((None, D_MODEL // CAST_STEPS, D_FF), chunk_of_layer),
                  pl.BlockSpec((None, D_FF // CAST_STEPS, D_MODEL), chunk_of_layer),
                  pl.BlockSpec((1, D_MODEL), const)],
        out_specs=(pl.BlockSpec((tile, D_MODEL), tok), whole(xs_out_shape)),
        scratch_shapes=[pltpu.VMEM((D_MODEL, D_MODEL), BF16),
                        pltpu.VMEM((D_MODEL, D_FF), BF16),
                        pltpu.VMEM((D_FF, D_MODEL), BF16)],
        compiler_params=pltpu.CompilerParams(dimension_semantics=("arbitrary",), vmem_limit_bytes=VMEM_LIMIT),
        name="out_projection_mlp",
    )(mix2, x2, mod, mod, mod, mod, mix_s, xs, mod_all, mod_all, mod_all, mod_all, norm_g, w_out,
      w_up, w_down, final_g.reshape(1, D_MODEL))


def _conv_ln_swish(acc, lng, lnb):
    mu = jnp.mean(acc, axis=-1, keepdims=True)
    xc = acc - mu
    y = xc * lax.rsqrt(jnp.mean(xc * xc, axis=-1, keepdims=True) + EPS) * lng + lnb
    return _silu(y)


def _hgrn_gates(fh, lb):
    f = lb + (1.0 - lb) * jax.nn.sigmoid(fh)
    return jnp.log(f), 1.0 - f


def _hgrn_out(o, hng, gate):
    return _rms_rows(o) * hng * _silu(gate)


def _ticker(pieces):
    it = iter(pieces)

    def tick():
        piece = next(it, None)
        if piece is not None:
            piece()

    def flush():
        for piece in it:
            piece()

    tick.flush = flush
    return tick


def _prompt_mix_kernel(sinks_ref, x_ref, sh_ref, sc_ref, sx_ref, ssh_ref, ssc_ref, ng_ref, win_ref, convw_ref,
                       convb_ref, lng_ref, lnb_ref, hlb_ref, hng_ref, bias_ref,
                       mix_ref, sproj_ref, convo_ref, so_ref, ko_ref, vo_ref,
                       proj_ref, wbf_ref, abuf, kbuf, vbuf, st_ref, *, layer, tile):
    t = pl.program_id(1)
    last = pl.num_programs(1) - 1

    @pl.when((pl.program_id(0) == 0) & (t == 0))
    def _():
        wbf_ref[...] = win_ref[...].astype(BF16)

    @pl.when(t == 0)
    def _():
        abuf[0:CONV_PAD, :] = jnp.zeros((CONV_PAD, D_CONV), F32)
        abuf[CONV_PAD + tile:CONV_PAD + tile + SUBLANES, :] = jnp.zeros((SUBLANES, D_CONV), F32)
        kbuf[0:ATTN_BLOCK, :] = jnp.zeros((ATTN_BLOCK, D_KV), BF16)
        vbuf[0:ATTN_BLOCK, :] = jnp.zeros((ATTN_BLOCK, D_KV), BF16)
        st_ref[...] = jnp.zeros(st_ref.shape, F32)

    h_in = _modulated_norm(x_ref[...], ng_ref[layer:layer + 1, :], sc_ref[...], sh_ref[...])
    for lo, hi in ((OFF_AVAL, OFF_Q), (OFF_F, OFF_I), (OFF_Q, OFF_F), (OFF_I, OFF_G), (OFF_G, OFF_QA),
                   (OFF_QA, IN_WIDTH)):
        proj_ref[:, lo:hi] = jnp.dot(h_in, wbf_ref[:, lo:hi], preferred_element_type=F32)

    kbuf[ATTN_BLOCK:ATTN_BLOCK + tile, :] = proj_ref[:, OFF_KA:OFF_KA + D_KV].astype(BF16)
    vbuf[ATTN_BLOCK:ATTN_BLOCK + tile, :] = proj_ref[:, OFF_VA:OFF_VA + D_KV].astype(BF16)
    scale = HEAD_DIM ** -0.5
    attn = {}

    def attn_scores(blk):
        def run():
            r0 = blk * ATTN_BLOCK
            for h in range(H_ATTN):
                kv = h // GROUP
                q = (proj_ref[r0:r0 + ATTN_BLOCK, OFF_QA + h * HEAD_DIM:OFF_QA + (h + 1) * HEAD_DIM]
                     * scale).astype(BF16)
                kall = kbuf[r0:r0 + 2 * ATTN_BLOCK, kv * HEAD_DIM:(kv + 1) * HEAD_DIM]
                attn[blk, h] = lax.dot_general(q, kall, NT_DIMS, preferred_element_type=F32)
        return run

    def attn_softmax(blk):
        def run():
            for h in range(H_ATTN):
                s = attn[blk, h] + bias_ref[h]
                if blk == 0:
                    col = lax.broadcasted_iota(jnp.int32, s.shape, 1)
                    s = jnp.where(col + (t * tile - ATTN_BLOCK) >= 0, s, -jnp.inf)
                p, den = _sink_softmax(s, sinks_ref[layer, h])
                attn[blk, h] = (p.astype(BF16), den)
        return run

    def attn_values(blk):
        def run():
            r0 = blk * ATTN_BLOCK
            heads = []
            for h in range(H_ATTN):
                kv = h // GROUP
                p, den = attn[blk, h]
                vall = vbuf[r0:r0 + 2 * ATTN_BLOCK, kv * HEAD_DIM:(kv + 1) * HEAD_DIM]
                heads.append(jnp.dot(p, vall, preferred_element_type=F32) / den)
            mix_ref[r0:r0 + ATTN_BLOCK, D_CONV + D_HGRN:D_MODEL] = jnp.concatenate(heads, axis=1).astype(BF16)
        return run

    tick = _ticker([stage(blk) for blk in range(tile // ATTN_BLOCK)
                    for stage in (attn_scores, attn_softmax, attn_values)])

    abuf[CONV_PAD:CONV_PAD + tile, :] = (proj_ref[:, OFF_AVAL:OFF_AVAL + D_CONV]
                                         * jax.nn.sigmoid(proj_ref[:, OFF_AGATE:OFF_AGATE + D_CONV]))
    first_row = CONV_PAD - (CONV_WIDTH - 1)
    acc = jnp.broadcast_to(convb_ref[layer:layer + 1, :], (tile, D_CONV))
    for r in range(SUBLANES):
        z = None
        for off in range(r, first_row + CONV_WIDTH, SUBLANES):
            j = off - first_row
            if j < 0:
                continue
            term = convw_ref[layer, j:j + 1, :] * abuf[off - r:off - r + tile + SUBLANES, :]
            z = term if z is None else z + term
        acc = acc + (z[0:tile] if r == 0 else pltpu.roll(z, tile + SUBLANES - r, 0)[0:tile])
    mix_ref[:, 0:D_CONV] = _conv_ln_swish(acc, lng_ref[layer:layer + 1, :], lnb_ref[layer:layer + 1, :]).astype(BF16)
    tick()

    lb = _layer_lb(hlb_ref[...], layer)
    hng = hng_ref[layer:layer + 1, :]
    span = HGRN_SPAN * HGRN_CHUNK
    ri = lax.broadcasted_iota(jnp.int32, (span, span), 0)
    ci = lax.broadcasted_iota(jnp.int32, (span, span), 1)
    tri = jnp.where((ri >= ci) & (ri // HGRN_CHUNK == ci // HGRN_CHUNK), 1.0, 0.0).astype(BF16)
    for i in range(tile // span):
        _hgrn_span(proj_ref, mix_ref, st_ref, i * span, lb, hng, tri, tick)
    tick.flush()

    @pl.when(t == last)
    def _():
        convo_ref[...] = abuf[CONV_PAD + tile - (CONV_WIDTH - 1):CONV_PAD + tile, :]
        for h in range(H_HGRN):
            so_ref[h] = st_ref[h].T
        ko_ref[...] = proj_ref[tile - WINDOW:tile, OFF_KA:OFF_KA + D_KV].T
        vo_ref[...] = proj_ref[tile - WINDOW:tile, OFF_VA:OFF_VA + D_KV].T

    abuf[0:CONV_PAD, :] = abuf[tile:tile + CONV_PAD, :]
    kbuf[0:ATTN_BLOCK, :] = kbuf[tile:tile + ATTN_BLOCK, :]
    vbuf[0:ATTN_BLOCK, :] = vbuf[tile:tile + ATTN_BLOCK, :]

    @pl.when((pl.program_id(0) == pl.num_programs(0) - 1) & (t == last))
    def _():
        h_s = _modulated_norm(sx_ref[...].reshape(-1, D_MODEL), ng_ref[layer:layer + 1, :], ssc_ref[...], ssh_ref[...])
        sproj_ref[...] = jnp.dot(h_s, wbf_ref[...], preferred_element_type=F32)


def _carry_specs(carried):
    return [pl.BlockSpec(memory_space=pl.ANY)] * len(carried)


def _without_carry(kernel_fn, n_in, n_carried, *refs):
    return kernel_fn(*refs[:n_in], *refs[n_in + n_carried:])


def _carried(kernel_fn, n_in, n_carried):
    return functools.partial(_without_carry, kernel_fn, n_in, n_carried)


def _prompt_mixers(x, mod, xs, mod_all, n_sample, norm_g, w_in, sinks, conv_w, conv_b, ln_g, ln_b, hgrn_lb, hng,
                   bias_p, layer, carried):
    B, T = x.shape[:2]
    tile = MIX_TILE
    depth = hgrn_lb.shape[0]
    n_tok_s = math.prod(xs.shape[:-1])
    const2 = lambda b, t: (0, 0)
    inputs = (sinks, x, mod, mod, xs, mod_all, mod_all, norm_g, w_in, conv_w, conv_b, ln_g, ln_b, hgrn_lb, hng, bias_p)
    return pl.pallas_call(
        _carried(functools.partial(_prompt_mix_kernel, layer=layer, tile=tile), len(inputs), len(carried)),
        out_shape=(jax.ShapeDtypeStruct((B, T, D_MODEL), BF16),
                   jax.ShapeDtypeStruct((n_tok_s, IN_WIDTH), F32),
                   jax.ShapeDtypeStruct((depth, B, CONV_WIDTH - 1, D_CONV), F32),
                   jax.ShapeDtypeStruct((depth, B, H_HGRN, DK_HGRN, DV_HGRN), F32),
                   jax.ShapeDtypeStruct((depth, B, D_KV, WINDOW), F32),
                   jax.ShapeDtypeStruct((depth, B, D_KV, WINDOW), F32)),
        grid=(B, T // tile),
        in_specs=[pl.BlockSpec(memory_space=pltpu.SMEM),
                  pl.BlockSpec((None, tile, D_MODEL), lambda b, t: (b, t, 0)),
                  pl.BlockSpec((None, 1, D_MODEL), lambda b, t: (b, 0, 0)),
                  pl.BlockSpec((None, 1, D_MODEL), lambda b, t: (b, 0, 1)),
                  pl.BlockSpec(xs.shape, lambda b, t: (0,) * xs.ndim, pipeline_mode=pl.Buffered(1))]
                 + _sample_mod_specs(n_sample, (0, 1), lambda b, t, chunk: (layer, 0, chunk)) + [
                  pl.BlockSpec(norm_g.shape, const2),
                  pl.BlockSpec((None, D_MODEL, IN_WIDTH), lambda b, t: (layer, 0, 0), pipeline_mode=pl.Buffered(1)),
                  pl.BlockSpec(conv_w.shape, lambda b, t: (0, 0, 0)),
                  pl.BlockSpec(conv_b.shape, const2),
                  pl.BlockSpec(ln_g.shape, const2),
                  pl.BlockSpec(ln_b.shape, const2),
                  pl.BlockSpec(hgrn_lb.shape, const2),
                  pl.BlockSpec(hng.shape, const2),
                  pl.BlockSpec((H_ATTN, ATTN_BLOCK, 2 * ATTN_BLOCK), lambda b, t: (0, 0, 0))] + _carry_specs(carried),
        out_specs=(pl.BlockSpec((None, tile, D_MODEL), lambda b, t: (b, t, 0)),
                   pl.BlockSpec((n_tok_s, IN_WIDTH), const2),
                   pl.BlockSpec((None, None, CONV_WIDTH - 1, D_CONV), lambda b, t: (layer, b, 0, 0)),
                   pl.BlockSpec((None, None, H_HGRN, DK_HGRN, DV_HGRN), lambda b, t: (layer, b, 0, 0, 0)),
                   pl.BlockSpec((None, None, D_KV, WINDOW), lambda b, t: (layer, b, 0, 0)),
                   pl.BlockSpec((None, None, D_KV, WINDOW), lambda b, t: (layer, b, 0, 0))),
        input_output_aliases={len(inputs) + i: 2 + i for i in range(len(carried))},
        scratch_shapes=[pltpu.VMEM((tile, IN_WIDTH), F32),
                        pltpu.VMEM((D_MODEL, IN_WIDTH), BF16),
                        pltpu.VMEM((CONV_PAD + tile + SUBLANES, D_CONV), F32),
                        pltpu.VMEM((ATTN_BLOCK + tile, D_KV), BF16),
                        pltpu.VMEM((ATTN_BLOCK + tile, D_KV), BF16),
                        pltpu.VMEM((H_HGRN, DV_HGRN, DK_HGRN), F32)],
        compiler_params=pltpu.CompilerParams(dimension_semantics=("arbitrary", "arbitrary"),
                                             vmem_limit_bytes=VMEM_LIMIT),
        name="prompt_mixers",
    )(*inputs, *carried)


def _sample_mix_kernel(sinks_ref, proj_ref, cconv_ref, state_ref, ck_ref, cv_ref, convw_ref, convb_ref, lng_ref,
                       lnb_ref, hlb_ref, hng_ref, bias_ref,
                       mix_ref, convo_ref, so_ref, ko_ref, vo_ref, kpad_ref, vpad_ref, *,
                       layer, block, seq, w_buf):
    hist = CONV_WIDTH - 1

    @pl.when(pl.program_id(0) == 0)
    def _():
        for ref in (kpad_ref, vpad_ref):
            ref[:, 0:w_buf - seq, :] = jnp.zeros((block, w_buf - seq, D_KV), F32)

    lb = _layer_lb(hlb_ref[...], layer)
    hng = hng_ref[layer:layer + 1, :]
    scale = HEAD_DIM ** -0.5
    elems = range(block)
    row8 = lax.broadcasted_iota(jnp.int32, (SUBLANES, DV_HGRN), 0)
    ones_rows = jnp.where((row8 >= seq) & (row8 < seq + 3), 1.0, 0.0)
    zrow = jnp.zeros((1, DK_HGRN), BF16)
    prow = lax.broadcasted_iota(jnp.int32, (seq, seq), 0)
    pcol = lax.broadcasted_iota(jnp.int32, (seq, seq), 1)
    grow = lax.broadcasted_iota(jnp.int32, (GROUP * seq, 1), 0)

    proj = [proj_ref[e * seq:(e + 1) * seq, :] for e in elems]

    glu = [p[:, OFF_AVAL:OFF_AVAL + D_CONV] * jax.nn.sigmoid(p[:, OFF_AGATE:OFF_AGATE + D_CONV]) for p in proj]
    full = [cconv_ref[i] for i in range(hist)]
    full += [jnp.concatenate([glu[e][t:t + 1] for e in elems], axis=0) for t in range(seq)]
    for i in range(hist):
        convo_ref[i] = full[i + seq]
    conv_out = []
    for t in range(seq):
        acc = jnp.broadcast_to(convb_ref[layer:layer + 1, :], (block, D_CONV))
        for j in range(CONV_WIDTH):
            acc = acc + convw_ref[layer, j:j + 1, :] * full[t + j]
        conv_out.append(_conv_ln_swish(acc, lng_ref[layer:layer + 1, :], lnb_ref[layer:layer + 1, :]))
    out_a = [jnp.concatenate([conv_out[t][e:e + 1] for t in range(seq)], axis=0) for e in elems]

    units = [(e, h) for e in elems for h in range(H_HGRN)]
    ops = {}
    for e in elems:
        p = proj[e]
        g, k = _hgrn_gates(p[:, OFF_F:OFF_F + D_HGRN], lb)
        b = _cumsum_rows_small(g)
        for h in range(H_HGRN):
            cs = slice(h * DK_HGRN, (h + 1) * DK_HGRN)
            q = p[:, OFF_Q + h * DK_HGRN:OFF_Q + (h + 1) * DK_HGRN]
            v = p[:, OFF_I + h * DV_HGRN:OFF_I + (h + 1) * DV_HGRN]
            bb = b[:, cs]
            bl = bb[seq - 1:seq, :]
            kst = (k[:, cs] * jnp.exp(bl - bb)).astype(BF16)
            x = jnp.concatenate([kst.astype(F32), *(part.astype(F32) for part in _split3_bf16(jnp.exp(bl))),
                                 zrow.astype(F32)], axis=0).astype(BF16)
            vpad = jnp.concatenate([v, jnp.zeros((SUBLANES - seq, DV_HGRN), F32)], axis=0)
            ops[e, h] = dict(qp=(q * jnp.exp(bb - bl)).astype(BF16), kst=kst, v=v.astype(BF16),
                             qt=(q * jnp.exp(bb)).astype(BF16), x=x,
                             r=jnp.concatenate([vpad, ones_rows], axis=1).astype(BF16))
    for u in units:
        o = ops[u]
        st = state_ref[u[0], u[1]]
        o["p"] = lax.dot_general(o["qp"], o["kst"], NT_DIMS, preferred_element_type=F32)
        o["inter"] = jnp.dot(o["qt"], st.astype(BF16), preferred_element_type=F32)
        me = lax.dot_general(o["x"], o["r"], TN_DIMS, preferred_element_type=F32)
        so_ref[u[0], u[1]] = me[:, DV_HGRN:] * st + me[:, :DV_HGRN]
    out_b = {}
    for u in units:
        o = ops[u]
        pm = jnp.where(prow >= pcol, o["p"], 0.0).astype(BF16)
        out = o["inter"] + jnp.dot(pm, o["v"], preferred_element_type=F32)
        gate = proj[u[0]][:, OFF_G + u[1] * DV_HGRN:OFF_G + (u[1] + 1) * DV_HGRN]
        out_b[u] = _hgrn_out(out, hng, gate)

    scores = {}
    for e in elems:
        p = proj[e]
        for kv in range(KV_HEADS):
            hs = slice(kv * HEAD_DIM, (kv + 1) * HEAD_DIM)
            q2 = jnp.concatenate([p[:, OFF_QA + h * HEAD_DIM:OFF_QA + (h + 1) * HEAD_DIM]
                                  for h in range(kv * GROUP, (kv + 1) * GROUP)], axis=0)
            q2 = (q2 * scale).astype(BF16)
            bias = bias_ref[kv]
            s_c = jnp.dot(q2, ck_ref[e, hs, :].astype(BF16), preferred_element_type=F32) + bias[:, 0:w_buf]
            s_n = (lax.dot_general(q2, p[:, OFF_KA + kv * HEAD_DIM:OFF_KA + (kv + 1) * HEAD_DIM].astype(BF16), NT_DIMS,
                                   preferred_element_type=F32) + bias[:, w_buf:w_buf + seq])
            scores[e, kv] = (s_c, s_n)
    out_c = {}
    for e in elems:
        p = proj[e]
        for kv in range(KV_HEADS):
            hs = slice(kv * HEAD_DIM, (kv + 1) * HEAD_DIM)
            sink = jnp.zeros((GROUP * seq, 1), F32)
            for gi in range(GROUP):
                sink = jnp.where(grow >= gi * seq, sinks_ref[layer, kv * GROUP + gi], sink)
            s_c, s_n = scores[e, kv]
            m = jnp.maximum(jnp.maximum(jnp.max(s_c, axis=-1, keepdims=True), jnp.max(s_n, axis=-1, keepdims=True)),
                            sink)
            p_c = jnp.exp(s_c - m)
            p_n = jnp.exp(s_n - m)
            den = jnp.sum(p_c, axis=-1, keepdims=True) + jnp.sum(p_n, axis=-1, keepdims=True) + jnp.exp(sink - m)
            vnew = p[:, OFF_VA + kv * HEAD_DIM:OFF_VA + (kv + 1) * HEAD_DIM].astype(BF16)
            o2 = (lax.dot_general(p_c.astype(BF16), cv_ref[e, hs, :].astype(BF16), NT_DIMS,
                                  preferred_element_type=F32)
                  + jnp.dot(p_n.astype(BF16), vnew, preferred_element_type=F32)) / den
            for gi in range(GROUP):
                out_c[e, kv * GROUP + gi] = o2[gi * seq:(gi + 1) * seq]
    lane = lax.broadcasted_iota(jnp.int32, (D_KV, w_buf), 1)
    for e in elems:
        p = proj[e]
        for pad, cache, new, out in ((kpad_ref, ck_ref, p[:, OFF_KA:OFF_KA + D_KV], ko_ref),
                                     (vpad_ref, cv_ref, p[:, OFF_VA:OFF_VA + D_KV], vo_ref)):
            pad[e, w_buf - seq:w_buf, :] = new
            out[e] = jnp.where(lane >= w_buf - seq, pad[e].T, pltpu.roll(cache[e], w_buf - seq, 1))

    for e in elems:
        parts = [out_a[e]] + [out_b[e, h] for h in range(H_HGRN)] + [out_c[e, h] for h in range(H_ATTN)]
        mix_ref[e * seq:(e + 1) * seq, :] = jnp.concatenate(parts, axis=1)


def _sample_mixers(proj2, sinks, cache_conv, state, cache_k, cache_v, conv_w, conv_b, ln_g, ln_b, hgrn_lb, hng,
                   bias_s, layer, carried):
    B = state.shape[1]
    seq = proj2.shape[0] // B
    w_buf = cache_k.shape[3]
    block = SAMPLE_BLOCK
    depth = hgrn_lb.shape[0]
    hist = CONV_WIDTH - 1
    const2 = lambda i: (0, 0)
    cache_specs = [pl.BlockSpec((None, hist, block, D_CONV), lambda i: (layer, 0, i, 0)),
                   pl.BlockSpec((None, block, H_HGRN, DK_HGRN, DV_HGRN), lambda i: (layer, i, 0, 0, 0)),
                   pl.BlockSpec((None, block, D_KV, w_buf), lambda i: (layer, i, 0, 0)),
                   pl.BlockSpec((None, block, D_KV, w_buf), lambda i: (layer, i, 0, 0))]
    inputs = (sinks, proj2, cache_conv, state, cache_k, cache_v, conv_w, conv_b, ln_g, ln_b, hgrn_lb, hng, bias_s)
    return pl.pallas_call(
        _carried(functools.partial(_sample_mix_kernel, layer=layer, block=block, seq=seq, w_buf=w_buf),
                 len(inputs), len(carried)),
        out_shape=(jax.ShapeDtypeStruct((B * seq, D_MODEL), F32),
                   jax.ShapeDtypeStruct((depth, hist, B, D_CONV), F32),
                   jax.ShapeDtypeStruct((depth, B, H_HGRN, DK_HGRN, DV_HGRN), F32),
                   jax.ShapeDtypeStruct((depth, B, D_KV, w_buf), F32),
                   jax.ShapeDtypeStruct((depth, B, D_KV, w_buf), F32)),
        grid=(B // block,),
        in_specs=[pl.BlockSpec(memory_space=pltpu.SMEM),
                  pl.BlockSpec((block * seq, IN_WIDTH), lambda i: (i, 0))] + cache_specs + [
                  pl.BlockSpec(conv_w.shape, lambda i: (0, 0, 0)),
                  pl.BlockSpec(conv_b.shape, const2),
                  pl.BlockSpec(ln_g.shape, const2),
                  pl.BlockSpec(ln_b.shape, const2),
                  pl.BlockSpec(hgrn_lb.shape, const2),
                  pl.BlockSpec(hng.shape, const2),
                  pl.BlockSpec((KV_HEADS, GROUP * seq, 2 * ATTN_BLOCK), lambda i: (0, 0, 0))] + _carry_specs(carried),
        out_specs=tuple([pl.BlockSpec((block * seq, D_MODEL), lambda i: (i, 0))] + cache_specs),
        input_output_aliases={len(inputs) + i: 1 + i for i in range(len(carried))},
        scratch_shapes=[pltpu.VMEM((block, w_buf, D_KV), F32),
                        pltpu.VMEM((block, w_buf, D_KV), F32)],
        compiler_params=pltpu.CompilerParams(dimension_semantics=("arbitrary",), vmem_limit_bytes=VMEM_LIMIT),
        name="sample_mixers",
    )(*inputs, *carried)


def kernel(x_prompt, x_sample, cache_conv, state_hgrn, cache_swa_k, cache_swa_v, c_prompt, c_sample, rel_bias, w_ada, b_ada, norm_mix_g, w_in, conv_w, conv_b, conv_ln_g, conv_ln_b, hgrn_lb, hgrn_norm_g, attn_sinks, w_out, norm_mlp_g, w_up, w_down, final_g):
    Bp, Tp = x_prompt.shape[:2]
    Bs, Ts = x_sample.shape[:2]
    depth = w_in.shape[0]
    w_buf = cache_swa_k.shape[2]
    assert Tp % MIX_TILE == 0 and (Bp * Tp) % TOK_TILE == 0 and Tp % TOK_TILE == 0 and Bs % SAMPLE_BLOCK == 0
    assert w_buf == WINDOW and GROUP * Ts == SUBLANES

    bias_p, bias_s = _bias_tables(rel_bias, Ts, w_buf)
    mod = _modulation(c_sample, c_prompt, w_ada, b_ada)
    hlb = hgrn_lb.astype(F32)
    cc = jnp.swapaxes(cache_conv, 1, 2)
    ck = jnp.swapaxes(cache_swa_k.reshape(depth, Bs, w_buf, D_KV), 2, 3)
    cv = jnp.swapaxes(cache_swa_v.reshape(depth, Bs, w_buf, D_KV), 2, 3)

    xp = x_prompt.reshape(Bp * Tp, D_MODEL)
    xs = x_sample
    caches_p = ()
    caches_s = ()
    for l in range(depth):
        final = l == depth - 1
        mod_p = mod[l, Bs:].reshape(Bp, 1, N_MOD * D_MODEL)
        mix_p, proj_s, *caches_p = _prompt_mixers(xp.reshape(Bp, Tp, D_MODEL), mod_p, xs, mod, Bs, norm_mix_g, w_in,
                                                  attn_sinks, conv_w, conv_b, conv_ln_g, conv_ln_b, hlb, hgrn_norm_g,
                                                  bias_p, l, caches_p)
        mix_s, *caches_s = _sample_mixers(proj_s, attn_sinks, cc, state_hgrn, ck, cv, conv_w, conv_b, conv_ln_g,
                                          conv_ln_b, hlb, hgrn_norm_g, bias_s, l, caches_s)
        xs_out_shape = x_sample.shape if final else (Bs * Ts, D_MODEL)
        xp, xs = _out_mlp(mix_p.reshape(Bp * Tp, D_MODEL), xp, mod_p, mix_s, xs, xs_out_shape, mod, Bs, norm_mlp_g, w_out,
                          w_up, w_down, final_g, l, Tp // TOK_TILE, final)
    cp, sp, kp, vp = caches_p
    kp, vp = jnp.swapaxes(kp, 2, 3), jnp.swapaxes(vp, 2, 3)
    cs, ss, ksn, vsn = caches_s
    cs, ksn, vsn = jnp.swapaxes(cs, 1, 2), jnp.swapaxes(ksn, 2, 3), jnp.swapaxes(vsn, 2, 3)
    return (xp.reshape(Bp, Tp, D_MODEL), xs, cp, cs, sp, ss,
            kp.reshape(depth, Bp, WINDOW, KV_HEADS, HEAD_DIM), ksn.reshape(depth, Bs, w_buf, KV_HEADS, HEAD_DIM),
            vp.reshape(depth, Bp, WINDOW, KV_HEADS, HEAD_DIM), vsn.reshape(depth, Bs, w_buf, KV_HEADS, HEAD_DIM))
```

```python
import functools
import math

import jax
import jax.numpy as jnp
from jax import lax
from jax.experimental import pallas as pl
from jax.experimental.pallas import tpu as pltpu

F32 = jnp.float32
BF16 = jnp.bfloat16

D_MODEL = 1024
D_CONV = 256
CONV_WIDTH = 31
H_HGRN = 4
DK_HGRN = 128
DV_HGRN = 128
D_HGRN = 512
HEAD_DIM = 64
H_ATTN = 4
KV_HEADS = 2
GROUP = H_ATTN // KV_HEADS
D_ATTN = H_ATTN * HEAD_DIM
D_KV = KV_HEADS * HEAD_DIM
WINDOW = 128
ATTN_BLOCK = 128
NUM_BUCKETS = 32
MAX_DISTANCE = 128
D_FF = 4 * D_MODEL
N_MOD = 6
EPS = 1e-6

OFF_AVAL = 0
OFF_AGATE = OFF_AVAL + D_CONV
OFF_Q = OFF_AGATE + D_CONV
OFF_F = OFF_Q + H_HGRN * DK_HGRN
OFF_I = OFF_F + H_HGRN * DK_HGRN
OFF_G = OFF_I + D_HGRN
OFF_QA = OFF_G + D_HGRN
OFF_KA = OFF_QA + D_ATTN
OFF_VA = OFF_KA + D_KV
IN_WIDTH = OFF_VA + D_KV

HGRN_CHUNK = 64
HGRN_KEYBLOCK = 32
HGRN_SPAN = 4
SUBLANES = 8
CONV_PAD = 32
MIX_TILE = 512
TOK_TILE = 512
SAMPLE_BLOCK = 16
FF_CHUNK = 2048
CAST_STEPS = 8
MOD_COLS = 2048
VMEM_LIMIT = 56 * 1024 * 1024

NT_DIMS = (((1,), (1,)), ((), ()))
TN_DIMS = (((0,), (0,)), ((), ()))


def _silu(x):
    return x * jax.nn.sigmoid(x)


def _rms_rows(x):
    return x * lax.rsqrt(jnp.mean(x * x, axis=-1, keepdims=True) + EPS)


def _layer_lb(hlb, layer):
    m = jnp.max(hlb, axis=0, keepdims=True)
    e = jnp.exp(hlb - m)
    p = e / jnp.sum(e, axis=0, keepdims=True)
    lb = jnp.zeros_like(m)
    for i in range(1, layer + 1):
        lb = lb + p[i:i + 1, :]
    return lb


def _split3_bf16(x):
    hi = x.astype(BF16)
    r = x - hi.astype(F32)
    mid = r.astype(BF16)
    return hi, mid, (r - mid.astype(F32)).astype(BF16)


def _select_rows_mxu(sel, x):
    return sum(jnp.dot(sel, part, preferred_element_type=F32) for part in _split3_bf16(x))


def _cumsum_rows_small(g):
    row = lax.broadcasted_iota(jnp.int32, g.shape, 0)
    b = jnp.zeros_like(g)
    for u in range(g.shape[0]):
        b = b + jnp.where(row >= u, g[u:u + 1, :], 0.0)
    return b


def _hgrn_span(proj_ref, mix_ref, st_ref, row0, lb, hng, tri, tick):
    L, KB = HGRN_CHUNK, HGRN_KEYBLOCK
    span = HGRN_SPAN * L
    g, k = _hgrn_gates(proj_ref[pl.ds(row0, span), OFF_F:OFF_F + D_HGRN], lb)
    b = _select_rows_mxu(tri, g)
    units = [(c, h) for c in range(HGRN_SPAN) for h in range(H_HGRN)]

    ops = {}
    for c, h in units:
        rows = pl.ds(row0 + c * L, L)
        cs = slice(h * DK_HGRN, (h + 1) * DK_HGRN)
        q = proj_ref[rows, OFF_Q + h * DK_HGRN:OFF_Q + (h + 1) * DK_HGRN]
        v = proj_ref[rows, OFF_I + h * DV_HGRN:OFF_I + (h + 1) * DV_HGRN].astype(BF16)
        kk = k[c * L:(c + 1) * L, cs]
        bb = b[c * L:(c + 1) * L, cs]
        refs = [bb[lo + KB // 2 - 1:lo + KB // 2, :] for lo in range(0, L, KB)]
        bl = bb[L - 1:L, :]
        kpf = [kk[j * KB:(j + 1) * KB] * jnp.exp(r - bb[j * KB:(j + 1) * KB]) for j, r in enumerate(refs)]
        own = [q[j * KB:(j + 1) * KB] * jnp.exp(bb[j * KB:(j + 1) * KB] - r) for j, r in enumerate(refs)]
        qpf = [jnp.concatenate([own[j]] + [own[i] * jnp.exp(refs[i] - refs[j]) for i in range(j + 1, len(refs))], axis=0)
               for j in range(len(refs))]
        kst = jnp.concatenate([kpf[j] * jnp.exp(bl - r) for j, r in enumerate(refs)], axis=0)
        ops[c, h] = dict(qp=[x.astype(BF16) for x in qpf], kp=[x.astype(BF16) for x in kpf], v=v,
                         qt=(qpf[0] * jnp.exp(refs[0])).astype(BF16), kst=kst.astype(BF16), e=jnp.exp(bl))
    tick()

    for u in units:
        o = ops[u]
        o["p"] = [lax.dot_general(qp, kp, NT_DIMS, preferred_element_type=F32) for qp, kp in zip(o["qp"], o["kp"])]
        o["m"] = lax.dot_general(o["v"], o["kst"], TN_DIMS, preferred_element_type=F32)
    tick()

    for u in units:
        pm = []
        for p in ops[u]["p"]:
            row = lax.broadcasted_iota(jnp.int32, p.shape, 0)
            col = lax.broadcasted_iota(jnp.int32, p.shape, 1)
            pm.append(jnp.where(row >= col, p, 0.0).astype(BF16))
        ops[u]["p"] = pm
    tick()

    for u in units:
        o = ops[u]
        blocks = [None] * (L // KB)
        for j, p in enumerate(o["p"]):
            cj = jnp.dot(p, o["v"][j * KB:(j + 1) * KB], preferred_element_type=F32)
            for i in range(j, L // KB):
                piece = cj[(i - j) * KB:(i - j + 1) * KB]
                blocks[i] = piece if blocks[i] is None else blocks[i] + piece
        o["o"] = jnp.concatenate(blocks, axis=0)
    tick()

    for h in range(H_HGRN):
        st = st_ref[h]
        for c in range(HGRN_SPAN):
            o = ops[c, h]
            out = o["o"] + lax.dot_general(o["qt"], st.astype(BF16), NT_DIMS, preferred_element_type=F32)
            st = o["e"] * st + o["m"]
            rows = pl.ds(row0 + c * L, L)
            gate = proj_ref[rows, OFF_G + h * DV_HGRN:OFF_G + (h + 1) * DV_HGRN]
            mix_ref[rows, D_CONV + h * DV_HGRN:D_CONV + (h + 1) * DV_HGRN] = _hgrn_out(out, hng, gate).astype(BF16)
        st_ref[h] = st


def _sink_softmax(s, sink):
    m = jnp.maximum(jnp.max(s, axis=-1, keepdims=True), sink)
    p = jnp.exp(s - m)
    return p, jnp.sum(p, axis=-1, keepdims=True) + jnp.exp(sink - m)


def _bias_kernel(tab_ref, bp_ref, bs_ref, op_ref, os_ref, *, seq):
    bk = bp_ref[...]
    for h in range(H_ATTN):
        acc = jnp.full(bk.shape, -jnp.inf, F32)
        for bkt in range(NUM_BUCKETS):
            acc = jnp.where(bk == bkt, tab_ref[bkt, h], acc)
        op_ref[h] = acc
    bk = bs_ref[...]
    row = lax.broadcasted_iota(jnp.int32, bk.shape, 0)
    for kv in range(KV_HEADS):
        acc = jnp.full(bk.shape, -jnp.inf, F32)
        for bkt in range(NUM_BUCKETS):
            val = jnp.full(bk.shape, tab_ref[bkt, kv * GROUP], F32)
            for gi in range(1, GROUP):
                val = jnp.where(row >= gi * seq, tab_ref[bkt, kv * GROUP + gi], val)
            acc = jnp.where(bk == bkt, val, acc)
        os_ref[kv] = acc


def _t5_bucket(rel):
    n = jnp.maximum(rel, 0)
    max_exact = NUM_BUCKETS // 2
    nf = jnp.maximum(n, max_exact).astype(F32)
    large = max_exact + (jnp.log(nf / max_exact) / math.log(MAX_DISTANCE / max_exact)
                         * (NUM_BUCKETS - max_exact)).astype(jnp.int32)
    large = jnp.minimum(large, NUM_BUCKETS - 1)
    return jnp.where(n < max_exact, n, large)


def _bias_tables(rel_bias, dec_seq, w_buf):
    qi = jnp.arange(ATTN_BLOCK, dtype=jnp.int32)[:, None]
    kc = jnp.arange(2 * ATTN_BLOCK, dtype=jnp.int32)[None, :]
    rel_p = qi + ATTN_BLOCK - kc
    bucket_p = jnp.where((rel_p >= 0) & (rel_p <= WINDOW), _t5_bucket(rel_p), -1)
    ts = (jnp.arange(GROUP * dec_seq, dtype=jnp.int32) % dec_seq)[:, None]
    js = jnp.arange(2 * ATTN_BLOCK, dtype=jnp.int32)[None, :]
    rel_s = w_buf + ts - js
    ok_s = (rel_s >= 0) & (rel_s <= WINDOW) & (js < w_buf + dec_seq)
    bucket_s = jnp.where(ok_s, _t5_bucket(rel_s), -1)
    return pl.pallas_call(
        functools.partial(_bias_kernel, seq=dec_seq),
        out_shape=(jax.ShapeDtypeStruct((H_ATTN, ATTN_BLOCK, 2 * ATTN_BLOCK), F32),
                   jax.ShapeDtypeStruct((KV_HEADS, GROUP * dec_seq, 2 * ATTN_BLOCK), F32)),
        in_specs=[pl.BlockSpec(memory_space=pltpu.SMEM),
                  pl.BlockSpec(memory_space=pltpu.VMEM),
                  pl.BlockSpec(memory_space=pltpu.VMEM)],
        out_specs=(pl.BlockSpec(memory_space=pltpu.VMEM), pl.BlockSpec(memory_space=pltpu.VMEM)),
        name="rel_bias_tables",
    )(rel_bias.astype(F32), bucket_p, bucket_s)


def _mod_kernel(ca_ref, cb_ref, w_ref, b_ref, o_ref):
    w = w_ref[...].astype(BF16)
    na = ca_ref.shape[0]
    for c_ref, rows in ((ca_ref, slice(0, na)), (cb_ref, slice(na, na + cb_ref.shape[0]))):
        o_ref[rows, :] = jnp.dot(_silu(c_ref[...]).astype(BF16), w, preferred_element_type=F32) + b_ref[...]


def _modulation(c_a, c_b, w_ada, b_ada):
    depth = w_ada.shape[0]
    n = c_a.shape[0] + c_b.shape[0]
    return pl.pallas_call(
        _mod_kernel,
        out_shape=jax.ShapeDtypeStruct((depth, n, N_MOD * D_MODEL), F32),
        grid=(depth, N_MOD * D_MODEL // MOD_COLS),
        in_specs=[pl.BlockSpec(c_a.shape, lambda l, j: (0, 0)),
                  pl.BlockSpec(c_b.shape, lambda l, j: (0, 0)),
                  pl.BlockSpec((None, D_MODEL, MOD_COLS), lambda l, j: (l, 0, j)),
                  pl.BlockSpec((None, 1, MOD_COLS), lambda l, j: (l, 0, j))],
        out_specs=pl.BlockSpec((None, n, MOD_COLS), lambda l, j: (l, 0, j)),
        compiler_params=pltpu.CompilerParams(dimension_semantics=("arbitrary", "arbitrary"),
                                             vmem_limit_bytes=VMEM_LIMIT),
        name="adaln_modulation",
    )(c_a, c_b, w_ada, b_ada.reshape(depth, 1, N_MOD * D_MODEL))


def _mod_rows(m, n_tokens):
    if m.shape[0] == 1:
        return m
    reps = n_tokens // m.shape[0]
    return jnp.broadcast_to(m[:, None, :], (m.shape[0], reps, m.shape[1])).reshape(n_tokens, m.shape[1])


def _modulated_norm(x, g, sc, sh):
    n = x.shape[0]
    return (_rms_rows(x) * (g * (1.0 + _mod_rows(sc, n))) + _mod_rows(sh, n)).astype(BF16)


def _sample_mod_specs(n_rows, chunks, index_map):
    return [pl.BlockSpec((None, n_rows, D_MODEL), functools.partial(index_map, chunk=c),
                         pipeline_mode=pl.Buffered(1)) for c in chunks]


def _mlp_tile(mix, x, g1, sh, sc, g2, ng, wout_ref, wup_ref, wdn_ref, fg, final):
    n = x.shape[0]
    x1 = x + _mod_rows(g1, n) * jnp.dot(mix.astype(BF16), wout_ref[...], preferred_element_type=F32)
    h = _modulated_norm(x1, ng, sc, sh)
    acc = None
    for c in range(D_FF // FF_CHUNK):
        u = jnp.dot(h, wup_ref[:, c * FF_CHUNK:(c + 1) * FF_CHUNK], preferred_element_type=F32)
        u = jnp.square(jnp.maximum(u, 0.0)).astype(BF16)
        d = jnp.dot(u, wdn_ref[c * FF_CHUNK:(c + 1) * FF_CHUNK, :], preferred_element_type=F32)
        acc = d if acc is None else acc + d
    x2 = x1 + _mod_rows(g2, n) * acc
    return _rms_rows(x2) * fg if final else x2


def _mlp_kernel(mix_ref, x_ref, g1_ref, sh_ref, sc_ref, g2_ref, smix_ref, sx_ref, sg1_ref, ssh_ref, ssc_ref, sg2_ref,
                ng_ref, wout_ref, wup_ref, wdn_ref, fg_ref, o_ref, so_ref, wout_s, wup_s, wdn_s, *, layer, final):
    s = pl.program_id(0)

    @pl.when(s < CAST_STEPS)
    def _():
        for src, dst in ((wout_ref, wout_s), (wup_ref, wup_s), (wdn_ref, wdn_s)):
            rows = src.shape[0]
            dst[pl.ds(pl.multiple_of(s * rows, rows), rows), :] = src[...].astype(BF16)

    weights = (ng_ref[layer:layer + 1, :], wout_s, wup_s, wdn_s, fg_ref[...], final)

    @pl.when(s >= CAST_STEPS)
    def _():
        o_ref[...] = _mlp_tile(mix_ref[...], x_ref[...], g1_ref[...], sh_ref[...], sc_ref[...], g2_ref[...], *weights)

    @pl.when(s == pl.num_programs(0) - 1)
    def _():
        sx = sx_ref[...].reshape(-1, D_MODEL)
        so_ref[...] = _mlp_tile(smix_ref[...], sx, sg1_ref[...], ssh_ref[...], ssc_ref[...], sg2_ref[...],
                                *weights).reshape(so_ref.shape)


def _out_mlp(mix2, x2, mod, mix_s, xs, xs_out_shape, mod_all, n_sample, norm_g, w_out, w_up, w_down, final_g, layer,
             tiles_per_batch, final):
    n = x2.shape[0]
    tile = TOK_TILE
    const = lambda i: (0, 0)
    tok = lambda i: (jnp.maximum(i - CAST_STEPS, 0), 0)
    chunk_of_layer = lambda i: (layer, jnp.minimum(i, CAST_STEPS - 1), 0)
    mod_p = lambda chunk: pl.BlockSpec((None, 1, D_MODEL),
                                       lambda i: (jnp.maximum(i - CAST_STEPS, 0) // tiles_per_batch, 0, chunk))
    whole = lambda shape: pl.BlockSpec(shape, lambda i: (0,) * len(shape), pipeline_mode=pl.Buffered(1))
    return pl.pallas_call(
        functools.partial(_mlp_kernel, layer=layer, final=final),
        out_shape=(jax.ShapeDtypeStruct((n, D_MODEL), F32), jax.ShapeDtypeStruct(xs_out_shape, F32)),
        grid=(CAST_STEPS + n // tile,),
        in_specs=[pl.BlockSpec((tile, D_MODEL), tok),
                  pl.BlockSpec((tile, D_MODEL), tok),
                  mod_p(2), mod_p(3), mod_p(4), mod_p(5),
                  whole(mix_s.shape), whole(xs.shape)]
                 + _sample_mod_specs(n_sample, (2, 3, 4, 5), lambda i, chunk: (layer, 0, chunk)) + [
                  pl.BlockSpec(norm_g.shape, const),
                  pl.BlockSpec((None, D_MODEL // CAST_STEPS, D_MODEL), chunk_of_layer),
                  pl.BlockSpec((None, D_MODEL // CAST_STEPS, D_FF), chunk_of_layer),
                  pl.BlockSpec((None, D_FF // CAST_STEPS, D_MODEL), chunk_of_layer),
                  pl.BlockSpec((1, D_MODEL), const)],
        out_specs=(pl.BlockSpec((tile, D_MODEL), tok), whole(xs_out_shape)),
        scratch_shapes=[pltpu.VMEM((D_MODEL, D_MODEL), BF16),
                        pltpu.VMEM((D_MODEL, D_FF), BF16),
                        pltpu.VMEM((D_FF, D_MODEL), BF16)],
        compiler_params=pltpu.CompilerParams(dimension_semantics=("arbitrary",), vmem_limit_bytes=VMEM_LIMIT),
        name="out_projection_mlp",
    )(mix2, x2, mod, mod, mod, mod, mix_s, xs, mod_all, mod_all, mod_all, mod_all, norm_g, w_out,
      w_up, w_down, final_g.reshape(1, D_MODEL))


def _conv_ln_swish(acc, lng, lnb):
    mu = jnp.mean(acc, axis=-1, keepdims=True)
    xc = acc - mu
    y = xc * lax.rsqrt(jnp.mean(xc * xc, axis=-1, keepdims=True) + EPS) * lng + lnb
    return _silu(y)


def _hgrn_gates(fh, lb):
    f = lb + (1.0 - lb) * jax.nn.sigmoid(fh)
    return jnp.log(f), 1.0 - f


def _hgrn_out(o, hng, gate):
    return _rms_rows(o) * hng * _silu(gate)


def _ticker(pieces):
    it = iter(pieces)

    def tick():
        piece = next(it, None)
        if piece is not None:
            piece()

    def flush():
        for piece in it:
            piece()

    tick.flush = flush
    return tick


def _prompt_mix_kernel(sinks_ref, x_ref, sh_ref, sc_ref, sx_ref, ssh_ref, ssc_ref, ng_ref, win_ref, convw_ref,
                       convb_ref, lng_ref, lnb_ref, hlb_ref, hng_ref, bias_ref,
                       mix_ref, sproj_ref, convo_ref, so_ref, ko_ref, vo_ref,
                       proj_ref, wbf_ref, abuf, kbuf, vbuf, st_ref, *, layer, tile):
    t = pl.program_id(1)
    last = pl.num_programs(1) - 1

    @pl.when((pl.program_id(0) == 0) & (t == 0))
    def _():
        wbf_ref[...] = win_ref[...].astype(BF16)

    @pl.when(t == 0)
    def _():
        abuf[0:CONV_PAD, :] = jnp.zeros((CONV_PAD, D_CONV), F32)
        abuf[CONV_PAD + tile:CONV_PAD + tile + SUBLANES, :] = jnp.zeros((SUBLANES, D_CONV), F32)
        kbuf[0:ATTN_BLOCK, :] = jnp.zeros((ATTN_BLOCK, D_KV), BF16)
        vbuf[0:ATTN_BLOCK, :] = jnp.zeros((ATTN_BLOCK, D_KV), BF16)
        st_ref[...] = jnp.zeros(st_ref.shape, F32)

    h_in = _modulated_norm(x_ref[...], ng_ref[layer:layer + 1, :], sc_ref[...], sh_ref[...])
    proj_ref[...] = jnp.dot(h_in, wbf_ref[...], preferred_element_type=F32)

    kbuf[ATTN_BLOCK:ATTN_BLOCK + tile, :] = proj_ref[:, OFF_KA:OFF_KA + D_KV].astype(BF16)
    vbuf[ATTN_BLOCK:ATTN_BLOCK + tile, :] = proj_ref[:, OFF_VA:OFF_VA + D_KV].astype(BF16)
    scale = HEAD_DIM ** -0.5
    attn = {}

    def attn_scores(blk):
        def run():
            r0 = blk * ATTN_BLOCK
            for h in range(H_ATTN):
                kv = h // GROUP
                q = (proj_ref[r0:r0 + ATTN_BLOCK, OFF_QA + h * HEAD_DIM:OFF_QA + (h + 1) * HEAD_DIM]
                     * scale).astype(BF16)
                kall = kbuf[r0:r0 + 2 * ATTN_BLOCK, kv * HEAD_DIM:(kv + 1) * HEAD_DIM]
                attn[blk, h] = lax.dot_general(q, kall, NT_DIMS, preferred_element_type=F32)
        return run

    def attn_softmax(blk):
        def run():
            for h in range(H_ATTN):
                s = attn[blk, h] + bias_ref[h]
                if blk == 0:
                    col = lax.broadcasted_iota(jnp.int32, s.shape, 1)
                    s = jnp.where(col + (t * tile - ATTN_BLOCK) >= 0, s, -jnp.inf)
                p, den = _sink_softmax(s, sinks_ref[layer, h])
                attn[blk, h] = (p.astype(BF16), den)
        return run

    def attn_values(blk):
        def run():
            r0 = blk * ATTN_BLOCK
            heads = []
            for h in range(H_ATTN):
                kv = h // GROUP
                p, den = attn[blk, h]
                vall = vbuf[r0:r0 + 2 * ATTN_BLOCK, kv * HEAD_DIM:(kv + 1) * HEAD_DIM]
                heads.append(jnp.dot(p, vall, preferred_element_type=F32) / den)
            mix_ref[r0:r0 + ATTN_BLOCK, D_CONV + D_HGRN:D_MODEL] = jnp.concatenate(heads, axis=1).astype(BF16)
        return run

    tick = _ticker([stage(blk) for blk in range(tile // ATTN_BLOCK)
                    for stage in (attn_scores, attn_softmax, attn_values)])

    abuf[CONV_PAD:CONV_PAD + tile, :] = (proj_ref[:, OFF_AVAL:OFF_AVAL + D_CONV]
                                         * jax.nn.sigmoid(proj_ref[:, OFF_AGATE:OFF_AGATE + D_CONV]))
    first_row = CONV_PAD - (CONV_WIDTH - 1)
    acc = jnp.broadcast_to(convb_ref[layer:layer + 1, :], (tile, D_CONV))
    for r in range(SUBLANES):
        z = None
        for off in range(r, first_row + CONV_WIDTH, SUBLANES):
            j = off - first_row
            if j < 0:
                continue
            term = convw_ref[layer, j:j + 1, :] * abuf[off - r:off - r + tile + SUBLANES, :]
            z = term if z is None else z + term
        acc = acc + (z[0:tile] if r == 0 else pltpu.roll(z, tile + SUBLANES - r, 0)[0:tile])
    mix_ref[:, 0:D_CONV] = _conv_ln_swish(acc, lng_ref[layer:layer + 1, :], lnb_ref[layer:layer + 1, :]).astype(BF16)
    tick()

    lb = _layer_lb(hlb_ref[...], layer)
    hng = hng_ref[layer:layer + 1, :]
    span = HGRN_SPAN * HGRN_CHUNK
    ri = lax.broadcasted_iota(jnp.int32, (span, span), 0)
    ci = lax.broadcasted_iota(jnp.int32, (span, span), 1)
    tri = jnp.where((ri >= ci) & (ri // HGRN_CHUNK == ci // HGRN_CHUNK), 1.0, 0.0).astype(BF16)
    for i in range(tile // span):
        _hgrn_span(proj_ref, mix_ref, st_ref, i * span, lb, hng, tri, tick)
    tick.flush()

    @pl.when(t == last)
    def _():
        convo_ref[...] = abuf[CONV_PAD + tile - (CONV_WIDTH - 1):CONV_PAD + tile, :]
        for h in range(H_HGRN):
            so_ref[h] = st_ref[h].T
        ko_ref[...] = proj_ref[tile - WINDOW:tile, OFF_KA:OFF_KA + D_KV].T
        vo_ref[...] = proj_ref[tile - WINDOW:tile, OFF_VA:OFF_VA + D_KV].T

    abuf[0:CONV_PAD, :] = abuf[tile:tile + CONV_PAD, :]
    kbuf[0:ATTN_BLOCK, :] = kbuf[tile:tile + ATTN_BLOCK, :]
    vbuf[0:ATTN_BLOCK, :] = vbuf[tile:tile + ATTN_BLOCK, :]

    @pl.when((pl.program_id(0) == pl.num_programs(0) - 1) & (t == last))
    def _():
        h_s = _modulated_norm(sx_ref[...].reshape(-1, D_MODEL), ng_ref[layer:layer + 1, :], ssc_ref[...], ssh_ref[...])
        sproj_ref[...] = jnp.dot(h_s, wbf_ref[...], preferred_element_type=F32)


def _carry_specs(carried):
    return [pl.BlockSpec(memory_space=pl.ANY)] * len(carried)


def _without_carry(kernel_fn, n_in, n_carried, *refs):
    return kernel_fn(*refs[:n_in], *refs[n_in + n_carried:])


def _carried(kernel_fn, n_in, n_carried):
    return functools.partial(_without_carry, kernel_fn, n_in, n_carried)


def _prompt_mixers(x, mod, xs, mod_all, n_sample, norm_g, w_in, sinks, conv_w, conv_b, ln_g, ln_b, hgrn_lb, hng,
                   bias_p, layer, carried):
    B, T = x.shape[:2]
    tile = MIX_TILE
    depth = hgrn_lb.shape[0]
    n_tok_s = math.prod(xs.shape[:-1])
    const2 = lambda b, t: (0, 0)
    inputs = (sinks, x, mod, mod, xs, mod_all, mod_all, norm_g, w_in, conv_w, conv_b, ln_g, ln_b, hgrn_lb, hng, bias_p)
    return pl.pallas_call(
        _carried(functools.partial(_prompt_mix_kernel, layer=layer, tile=tile), len(inputs), len(carried)),
        out_shape=(jax.ShapeDtypeStruct((B, T, D_MODEL), BF16),
                   jax.ShapeDtypeStruct((n_tok_s, IN_WIDTH), F32),
                   jax.ShapeDtypeStruct((depth, B, CONV_WIDTH - 1, D_CONV), F32),
                   jax.ShapeDtypeStruct((depth, B, H_HGRN, DK_HGRN, DV_HGRN), F32),
                   jax.ShapeDtypeStruct((depth, B, D_KV, WINDOW), F32),
                   jax.ShapeDtypeStruct((depth, B, D_KV, WINDOW), F32)),
        grid=(B, T // tile),
        in_specs=[pl.BlockSpec(memory_space=pltpu.SMEM),
                  pl.BlockSpec((None, tile, D_MODEL), lambda b, t: (b, t, 0)),
                  pl.BlockSpec((None, 1, D_MODEL), lambda b, t: (b, 0, 0)),
                  pl.BlockSpec((None, 1, D_MODEL), lambda b, t: (b, 0, 1)),
                  pl.BlockSpec(xs.shape, lambda b, t: (0,) * xs.ndim, pipeline_mode=pl.Buffered(1))]
                 + _sample_mod_specs(n_sample, (0, 1), lambda b, t, chunk: (layer, 0, chunk)) + [
                  pl.BlockSpec(norm_g.shape, const2),
                  pl.BlockSpec((None, D_MODEL, IN_WIDTH), lambda b, t: (layer, 0, 0), pipeline_mode=pl.Buffered(1)),
                  pl.BlockSpec(conv_w.shape, lambda b, t: (0, 0, 0)),
                  pl.BlockSpec(conv_b.shape, const2),
                  pl.BlockSpec(ln_g.shape, const2),
                  pl.BlockSpec(ln_b.shape, const2),
                  pl.BlockSpec(hgrn_lb.shape, const2),
                  pl.BlockSpec(hng.shape, const2),
                  pl.BlockSpec((H_ATTN, ATTN_BLOCK, 2 * ATTN_BLOCK), lambda b, t: (0, 0, 0))] + _carry_specs(carried),
        out_specs=(pl.BlockSpec((None, tile, D_MODEL), lambda b, t: (b, t, 0)),
                   pl.BlockSpec((n_tok_s, IN_WIDTH), const2),
                   pl.BlockSpec((None, None, CONV_WIDTH - 1, D_CONV), lambda b, t: (layer, b, 0, 0)),
                   pl.BlockSpec((None, None, H_HGRN, DK_HGRN, DV_HGRN), lambda b, t: (layer, b, 0, 0, 0)),
                   pl.BlockSpec((None, None, D_KV, WINDOW), lambda b, t: (layer, b, 0, 0)),
                   pl.BlockSpec((None, None, D_KV, WINDOW), lambda b, t: (layer, b, 0, 0))),
        input_output_aliases={len(inputs) + i: 2 + i for i in range(len(carried))},
        scratch_shapes=[pltpu.VMEM((tile, IN_WIDTH), F32),
                        pltpu.VMEM((D_MODEL, IN_WIDTH), BF16),
                        pltpu.VMEM((CONV_PAD + tile + SUBLANES, D_CONV), F32),
                        pltpu.VMEM((ATTN_BLOCK + tile, D_KV), BF16),
                        pltpu.VMEM((ATTN_BLOCK + tile, D_KV), BF16),
                        pltpu.VMEM((H_HGRN, DV_HGRN, DK_HGRN), F32)],
        compiler_params=pltpu.CompilerParams(dimension_semantics=("arbitrary", "arbitrary"),
                                             vmem_limit_bytes=VMEM_LIMIT),
        name="prompt_mixers",
    )(*inputs, *carried)


def _sample_mix_kernel(sinks_ref, proj_ref, cconv_ref, state_ref, ck_ref, cv_ref, convw_ref, convb_ref, lng_ref,
                       lnb_ref, hlb_ref, hng_ref, bias_ref,
                       mix_ref, convo_ref, so_ref, ko_ref, vo_ref, kpad_ref, vpad_ref, *,
                       layer, block, seq, w_buf):
    hist = CONV_WIDTH - 1

    @pl.when(pl.program_id(0) == 0)
    def _():
        for ref in (kpad_ref, vpad_ref):
            ref[:, 0:w_buf - seq, :] = jnp.zeros((block, w_buf - seq, D_KV), F32)

    lb = _layer_lb(hlb_ref[...], layer)
    hng = hng_ref[layer:layer + 1, :]
    scale = HEAD_DIM ** -0.5
    elems = range(block)
    row8 = lax.broadcasted_iota(jnp.int32, (SUBLANES, DV_HGRN), 0)
    ones_rows = jnp.where((row8 >= seq) & (row8 < seq + 3), 1.0, 0.0)
    zrow = jnp.zeros((1, DK_HGRN), BF16)
    prow = lax.broadcasted_iota(jnp.int32, (seq, seq), 0)
    pcol = lax.broadcasted_iota(jnp.int32, (seq, seq), 1)
    grow = lax.broadcasted_iota(jnp.int32, (GROUP * seq, 1), 0)

    proj = [proj_ref[e * seq:(e + 1) * seq, :] for e in elems]

    glu = [p[:, OFF_AVAL:OFF_AVAL + D_CONV] * jax.nn.sigmoid(p[:, OFF_AGATE:OFF_AGATE + D_CONV]) for p in proj]
    full = [cconv_ref[i] for i in range(hist)]
    full += [jnp.concatenate([glu[e][t:t + 1] for e in elems], axis=0) for t in range(seq)]
    for i in range(hist):
        convo_ref[i] = full[i + seq]
    conv_out = []
    for t in range(seq):
        acc = jnp.broadcast_to(convb_ref[layer:layer + 1, :], (block, D_CONV))
        for j in range(CONV_WIDTH):
            acc = acc + convw_ref[layer, j:j + 1, :] * full[t + j]
        conv_out.append(_conv_ln_swish(acc, lng_ref[layer:layer + 1, :], lnb_ref[layer:layer + 1, :]))
    out_a = [jnp.concatenate([conv_out[t][e:e + 1] for t in range(seq)], axis=0) for e in elems]

    units = [(e, h) for e in elems for h in range(H_HGRN)]
    ops = {}
    for e in elems:
        p = proj[e]
        g, k = _hgrn_gates(p[:, OFF_F:OFF_F + D_HGRN], lb)
        b = _cumsum_rows_small(g)
        for h in range(H_HGRN):
            cs = slice(h * DK_HGRN, (h + 1) * DK_HGRN)
            q = p[:, OFF_Q + h * DK_HGRN:OFF_Q + (h + 1) * DK_HGRN]
            v = p[:, OFF_I + h * DV_HGRN:OFF_I + (h + 1) * DV_HGRN]
            bb = b[:, cs]
            bl = bb[seq - 1:seq, :]
            kst = (k[:, cs] * jnp.exp(bl - bb)).astype(BF16)
            x = jnp.concatenate([kst.astype(F32), *(part.astype(F32) for part in _split3_bf16(jnp.exp(bl))),
                                 zrow.astype(F32)], axis=0).astype(BF16)
            vpad = jnp.concatenate([v, jnp.zeros((SUBLANES - seq, DV_HGRN), F32)], axis=0)
            ops[e, h] = dict(qp=(q * jnp.exp(bb - bl)).astype(BF16), kst=kst, v=v.astype(BF16),
                             qt=(q * jnp.exp(bb)).astype(BF16), x=x,
                             r=jnp.concatenate([vpad, ones_rows], axis=1).astype(BF16))
    for u in units:
        o = ops[u]
        st = state_ref[u[0], u[1]]
        o["p"] = lax.dot_general(o["qp"], o["kst"], NT_DIMS, preferred_element_type=F32)
        o["inter"] = jnp.dot(o["qt"], st.astype(BF16), preferred_element_type=F32)
        me = lax.dot_general(o["x"], o["r"], TN_DIMS, preferred_element_type=F32)
        so_ref[u[0], u[1]] = me[:, DV_HGRN:] * st + me[:, :DV_HGRN]
    out_b = {}
    for u in units:
        o = ops[u]
        pm = jnp.where(prow >= pcol, o["p"], 0.0).astype(BF16)
        out = o["inter"] + jnp.dot(pm, o["v"], preferred_element_type=F32)
        gate = proj[u[0]][:, OFF_G + u[1] * DV_HGRN:OFF_G + (u[1] + 1) * DV_HGRN]
        out_b[u] = _hgrn_out(out, hng, gate)

    scores = {}
    for e in elems:
        p = proj[e]
        for kv in range(KV_HEADS):
            hs = slice(kv * HEAD_DIM, (kv + 1) * HEAD_DIM)
            q2 = jnp.concatenate([p[:, OFF_QA + h * HEAD_DIM:OFF_QA + (h + 1) * HEAD_DIM]
                                  for h in range(kv * GROUP, (kv + 1) * GROUP)], axis=0)
            q2 = (q2 * scale).astype(BF16)
            bias = bias_ref[kv]
            s_c = jnp.dot(q2, ck_ref[e, hs, :].astype(BF16), preferred_element_type=F32) + bias[:, 0:w_buf]
            s_n = (lax.dot_general(q2, p[:, OFF_KA + kv * HEAD_DIM:OFF_KA + (kv + 1) * HEAD_DIM].astype(BF16), NT_DIMS,
                                   preferred_element_type=F32) + bias[:, w_buf:w_buf + seq])
            scores[e, kv] = (s_c, s_n)
    out_c = {}
    for e in elems:
        p = proj[e]
        for kv in range(KV_HEADS):
            hs = slice(kv * HEAD_DIM, (kv + 1) * HEAD_DIM)
            sink = jnp.zeros((GROUP * seq, 1), F32)
            for gi in range(GROUP):
                sink = jnp.where(grow >= gi * seq, sinks_ref[layer, kv * GROUP + gi], sink)
            s_c, s_n = scores[e, kv]
            m = jnp.maximum(jnp.maximum(jnp.max(s_c, axis=-1, keepdims=True), jnp.max(s_n, axis=-1, keepdims=True)),
                            sink)
            p_c = jnp.exp(s_c - m)
            p_n = jnp.exp(s_n - m)
            den = jnp.sum(p_c, axis=-1, keepdims=True) + jnp.sum(p_n, axis=-1, keepdims=True) + jnp.exp(sink - m)
            vnew = p[:, OFF_VA + kv * HEAD_DIM:OFF_VA + (kv + 1) * HEAD_DIM].astype(BF16)
            o2 = (lax.dot_general(p_c.astype(BF16), cv_ref[e, hs, :].astype(BF16), NT_DIMS,
                                  preferred_element_type=F32)
                  + jnp.dot(p_n.astype(BF16), vnew, preferred_element_type=F32)) / den
            for gi in range(GROUP):
                out_c[e, kv * GROUP + gi] = o2[gi * seq:(gi + 1) * seq]
    lane = lax.broadcasted_iota(jnp.int32, (D_KV, w_buf), 1)
    for e in elems:
        p = proj[e]
        for pad, cache, new, out in ((kpad_ref, ck_ref, p[:, OFF_KA:OFF_KA + D_KV], ko_ref),
                                     (vpad_ref, cv_ref, p[:, OFF_VA:OFF_VA + D_KV], vo_ref)):
            pad[e, w_buf - seq:w_buf, :] = new
            out[e] = jnp.where(lane >= w_buf - seq, pad[e].T, pltpu.roll(cache[e], w_buf - seq, 1))

    for e in elems:
        parts = [out_a[e]] + [out_b[e, h] for h in range(H_HGRN)] + [out_c[e, h] for h in range(H_ATTN)]
        mix_ref[e * seq:(e + 1) * seq, :] = jnp.concatenate(parts, axis=1)


def _sample_mixers(proj2, sinks, cache_conv, state, cache_k, cache_v, conv_w, conv_b, ln_g, ln_b, hgrn_lb, hng,
                   bias_s, layer, carried):
    B = state.shape[1]
    seq = proj2.shape[0] // B
    w_buf = cache_k.shape[3]
    block = SAMPLE_BLOCK
    depth = hgrn_lb.shape[0]
    hist = CONV_WIDTH - 1
    const2 = lambda i: (0, 0)
    cache_specs = [pl.BlockSpec((None, hist, block, D_CONV), lambda i: (layer, 0, i, 0)),
                   pl.BlockSpec((None, block, H_HGRN, DK_HGRN, DV_HGRN), lambda i: (layer, i, 0, 0, 0)),
                   pl.BlockSpec((None, block, D_KV, w_buf), lambda i: (layer, i, 0, 0)),
                   pl.BlockSpec((None, block, D_KV, w_buf), lambda i: (layer, i, 0, 0))]
    inputs = (sinks, proj2, cache_conv, state, cache_k, cache_v, conv_w, conv_b, ln_g, ln_b, hgrn_lb, hng, bias_s)
    return pl.pallas_call(
        _carried(functools.partial(_sample_mix_kernel, layer=layer, block=block, seq=seq, w_buf=w_buf),
                 len(inputs), len(carried)),
        out_shape=(jax.ShapeDtypeStruct((B * seq, D_MODEL), F32),
                   jax.ShapeDtypeStruct((depth, hist, B, D_CONV), F32),
                   jax.ShapeDtypeStruct((depth, B, H_HGRN, DK_HGRN, DV_HGRN), F32),
                   jax.ShapeDtypeStruct((depth, B, D_KV, w_buf), F32),
                   jax.ShapeDtypeStruct((depth, B, D_KV, w_buf), F32)),
        grid=(B // block,),
        in_specs=[pl.BlockSpec(memory_space=pltpu.SMEM),
                  pl.BlockSpec((block * seq, IN_WIDTH), lambda i: (i, 0))] + cache_specs + [
                  pl.BlockSpec(conv_w.shape, lambda i: (0, 0, 0)),
                  pl.BlockSpec(conv_b.shape, const2),
                  pl.BlockSpec(ln_g.shape, const2),
                  pl.BlockSpec(ln_b.shape, const2),
                  pl.BlockSpec(hgrn_lb.shape, const2),
                  pl.BlockSpec(hng.shape, const2),
                  pl.BlockSpec((KV_HEADS, GROUP * seq, 2 * ATTN_BLOCK), lambda i: (0, 0, 0))] + _carry_specs(carried),
        out_specs=tuple([pl.BlockSpec((block * seq, D_MODEL), lambda i: (i, 0))] + cache_specs),
        input_output_aliases={len(inputs) + i: 1 + i for i in range(len(carried))},
        scratch_shapes=[pltpu.VMEM((block, w_buf, D_KV), F32),
                        pltpu.VMEM((block, w_buf, D_KV), F32)],
        compiler_params=pltpu.CompilerParams(dimension_semantics=("arbitrary",), vmem_limit_bytes=VMEM_LIMIT),
        name="sample_mixers",
    )(*inputs, *carried)


def kernel(x_prompt, x_sample, cache_conv, state_hgrn, cache_swa_k, cache_swa_v, c_prompt, c_sample, rel_bias, w_ada, b_ada, norm_mix_g, w_in, conv_w, conv_b, conv_ln_g, conv_ln_b, hgrn_lb, hgrn_norm_g, attn_sinks, w_out, norm_mlp_g, w_up, w_down, final_g):
    Bp, Tp = x_prompt.shape[:2]
    Bs, Ts = x_sample.shape[:2]
    depth = w_in.shape[0]
    w_buf = cache_swa_k.shape[2]
    assert Tp % MIX_TILE == 0 and (Bp * Tp) % TOK_TILE == 0 and Tp % TOK_TILE == 0 and Bs % SAMPLE_BLOCK == 0
    assert w_buf == WINDOW and GROUP * Ts == SUBLANES

    bias_p, bias_s = _bias_tables(rel_bias, Ts, w_buf)
    mod = _modulation(c_sample, c_prompt, w_ada, b_ada)
    hlb = hgrn_lb.astype(F32)
    cc = jnp.swapaxes(cache_conv, 1, 2)
    ck = jnp.swapaxes(cache_swa_k.reshape(depth, Bs, w_buf, D_KV), 2, 3)
    cv = jnp.swapaxes(cache_swa_v.reshape(depth, Bs, w_buf, D_KV), 2, 3)

    xp = x_prompt.reshape(Bp * Tp, D_MODEL)
    xs = x_sample
    caches_p = ()
    caches_s = ()
    for l in range(depth):
        final = l == depth - 1
        mod_p = mod[l, Bs:].reshape(Bp, 1, N_MOD * D_MODEL)
        mix_p, proj_s, *caches_p = _prompt_mixers(xp.reshape(Bp, Tp, D_MODEL), mod_p, xs, mod, Bs, norm_mix_g, w_in,
                                                  attn_sinks, conv_w, conv_b, conv_ln_g, conv_ln_b, hlb, hgrn_norm_g,
                                                  bias_p, l, caches_p)
        mix_s, *caches_s = _sample_mixers(proj_s, attn_sinks, cc, state_hgrn, ck, cv, conv_w, conv_b, conv_ln_g,
                                          conv_ln_b, hlb, hgrn_norm_g, bias_s, l, caches_s)
        xs_out_shape = x_sample.shape if final else (Bs * Ts, D_MODEL)
        xp, xs = _out_mlp(mix_p.reshape(Bp * Tp, D_MODEL), xp, mod_p, mix_s, xs, xs_out_shape, mod, Bs, norm_mlp_g, w_out,
                          w_up, w_down, final_g, l, Tp // TOK_TILE, final)
    cp, sp, kp, vp = caches_p
    kp, vp = jnp.swapaxes(kp, 2, 3), jnp.swapaxes(vp, 2, 3)
    cs, ss, ksn, vsn = caches_s
    cs, ksn, vsn = jnp.swapaxes(cs, 1, 2), jnp.swapaxes(ksn, 2, 3), jnp.swapaxes(vsn, 2, 3)
    return (xp.reshape(Bp, Tp, D_MODEL), xs, cp, cs, sp, ss,
            kp.reshape(depth, Bp, WINDOW, KV_HEADS, HEAD_DIM), ksn.reshape(depth, Bs, w_buf, KV_HEADS, HEAD_DIM),
            vp.reshape(depth, Bp, WINDOW, KV_HEADS, HEAD_DIM), vsn.reshape(depth, Bs, w_buf, KV_HEADS, HEAD_DIM))
```

```python
import functools
import math

import jax
import jax.numpy as jnp
from jax import lax
from jax.experimental import pallas as pl
from jax.experimental.pallas import tpu as pltpu

F32 = jnp.float32
BF16 = jnp.bfloat16

D_MODEL = 1024
D_CONV = 256
CONV_WIDTH = 31
H_HGRN = 4
DK_HGRN = 128
DV_HGRN = 128
D_HGRN = 512
HEAD_DIM = 64
H_ATTN = 4
KV_HEADS = 2
GROUP = H_ATTN // KV_HEADS
D_ATTN = H_ATTN * HEAD_DIM
D_KV = KV_HEADS * HEAD_DIM
WINDOW = 128
ATTN_BLOCK = 128
NUM_BUCKETS = 32
MAX_DISTANCE = 128
D_FF = 4 * D_MODEL
N_MOD = 6
EPS = 1e-6

OFF_AVAL = 0
OFF_AGATE = OFF_AVAL + D_CONV
OFF_Q = OFF_AGATE + D_CONV
OFF_F = OFF_Q + H_HGRN * DK_HGRN
OFF_I = OFF_F + H_HGRN * DK_HGRN
OFF_G = OFF_I + D_HGRN
OFF_QA = OFF_G + D_HGRN
OFF_KA = OFF_QA + D_ATTN
OFF_VA = OFF_KA + D_KV
IN_WIDTH = OFF_VA + D_KV

HGRN_CHUNK = 64
HGRN_KEYBLOCK = 32
HGRN_SPAN = 4
SUBLANES = 8
CONV_PAD = 32
MIX_TILE = 512
TOK_TILE = 512
SAMPLE_BLOCK = 16
FF_CHUNK = 2048
CAST_STEPS = 8
MOD_COLS = 2048
VMEM_LIMIT = 56 * 1024 * 1024

NT_DIMS = (((1,), (1,)), ((), ()))
TN_DIMS = (((0,), (0,)), ((), ()))


def _silu(x):
    return x * jax.nn.sigmoid(x)


def _rms_rows(x):
    return x * lax.rsqrt(jnp.mean(x * x, axis=-1, keepdims=True) + EPS)


def _layer_lb(hlb, layer):
    m = jnp.max(hlb, axis=0, keepdims=True)
    e = jnp.exp(hlb - m)
    p = e / jnp.sum(e, axis=0, keepdims=True)
    lb = jnp.zeros_like(m)
    for i in range(1, layer + 1):
        lb = lb + p[i:i + 1, :]
    return lb


def _split3_bf16(x):
    hi = x.astype(BF16)
    r = x - hi.astype(F32)
    mid = r.astype(BF16)
    return hi, mid, (r - mid.astype(F32)).astype(BF16)


def _select_rows_mxu(sel, x):
    return sum(jnp.dot(sel, part, preferred_element_type=F32) for part in _split3_bf16(x))


def _cumsum_rows_small(g):
    row = lax.broadcasted_iota(jnp.int32, g.shape, 0)
    b = jnp.zeros_like(g)
    for u in range(g.shape[0]):
        b = b + jnp.where(row >= u, g[u:u + 1, :], 0.0)
    return b


def _hgrn_span(proj_ref, mix_ref, st_ref, row0, lb, hng, tri, tick):
    L, KB = HGRN_CHUNK, HGRN_KEYBLOCK
    span = HGRN_SPAN * L
    g, k = _hgrn_gates(proj_ref[pl.ds(row0, span), OFF_F:OFF_F + D_HGRN], lb)
    b = _select_rows_mxu(tri, g)
    units = [(c, h) for c in range(HGRN_SPAN) for h in range(H_HGRN)]

    ops = {}
    for c, h in units:
        rows = pl.ds(row0 + c * L, L)
        cs = slice(h * DK_HGRN, (h + 1) * DK_HGRN)
        q = proj_ref[rows, OFF_Q + h * DK_HGRN:OFF_Q + (h + 1) * DK_HGRN]
        v = proj_ref[rows, OFF_I + h * DV_HGRN:OFF_I + (h + 1) * DV_HGRN].astype(BF16)
        kk = k[c * L:(c + 1) * L, cs]
        bb = b[c * L:(c + 1) * L, cs]
        refs = [bb[lo + KB // 2 - 1:lo + KB // 2, :] for lo in range(0, L, KB)]
        bl = bb[L - 1:L, :]
        kpf = [kk[j * KB:(j + 1) * KB] * jnp.exp(r - bb[j * KB:(j + 1) * KB]) for j, r in enumerate(refs)]
        own = [q[j * KB:(j + 1) * KB] * jnp.exp(bb[j * KB:(j + 1) * KB] - r) for j, r in enumerate(refs)]
        qpf = [jnp.concatenate([own[j]] + [own[i] * jnp.exp(refs[i] - refs[j]) for i in range(j + 1, len(refs))], axis=0)
               for j in range(len(refs))]
        kst = jnp.concatenate([kpf[j] * jnp.exp(bl - r) for j, r in enumerate(refs)], axis=0)
        ops[c, h] = dict(qp=[x.astype(BF16) for x in qpf], kp=[x.astype(BF16) for x in kpf], v=v,
                         qt=(qpf[0] * jnp.exp(refs[0])).astype(BF16), kst=kst.astype(BF16), e=jnp.exp(bl))
    tick()

    for u in units:
        o = ops[u]
        o["p"] = [lax.dot_general(qp, kp, NT_DIMS, preferred_element_type=F32) for qp, kp in zip(o["qp"], o["kp"])]
        o["m"] = lax.dot_general(o["v"], o["kst"], TN_DIMS, preferred_element_type=F32)
    tick()

    for u in units:
        pm = []
        for p in ops[u]["p"]:
            row = lax.broadcasted_iota(jnp.int32, p.shape, 0)
            col = lax.broadcasted_iota(jnp.int32, p.shape, 1)
            pm.append(jnp.where(row >= col, p, 0.0).astype(BF16))
        ops[u]["p"] = pm
    tick()

    for u in units:
        o = ops[u]
        blocks = [None] * (L // KB)
        for j, p in enumerate(o["p"]):
            cj = jnp.dot(p, o["v"][j * KB:(j + 1) * KB], preferred_element_type=F32)
            for i in range(j, L // KB):
                piece = cj[(i - j) * KB:(i - j + 1) * KB]
                blocks[i] = piece if blocks[i] is None else blocks[i] + piece
        o["o"] = jnp.concatenate(blocks, axis=0)
    tick()

    for h in range(H_HGRN):
        st = st_ref[h]
        for c in range(HGRN_SPAN):
            o = ops[c, h]
            out = o["o"] + lax.dot_general(o["qt"], st.astype(BF16), NT_DIMS, preferred_element_type=F32)
            st = o["e"] * st + o["m"]
            rows = pl.ds(row0 + c * L, L)
            gate = proj_ref[rows, OFF_G + h * DV_HGRN:OFF_G + (h + 1) * DV_HGRN]
            mix_ref[rows, D_CONV + h * DV_HGRN:D_CONV + (h + 1) * DV_HGRN] = _hgrn_out(out, hng, gate).astype(BF16)
        st_ref[h] = st


def _sink_softmax(s, sink):
    m = jnp.maximum(jnp.max(s, axis=-1, keepdims=True), sink)
    p = jnp.exp(s - m)
    return p, jnp.sum(p, axis=-1, keepdims=True) + jnp.exp(sink - m)


def _bias_kernel(tab_ref, bp_ref, bs_ref, op_ref, os_ref, *, seq):
    bk = bp_ref[...]
    for h in range(H_ATTN):
        acc = jnp.full(bk.shape, -jnp.inf, F32)
        for bkt in range(NUM_BUCKETS):
            acc = jnp.where(bk == bkt, tab_ref[bkt, h], acc)
        op_ref[h] = acc
    bk = bs_ref[...]
    row = lax.broadcasted_iota(jnp.int32, bk.shape, 0)
    for kv in range(KV_HEADS):
        acc = jnp.full(bk.shape, -jnp.inf, F32)
        for bkt in range(NUM_BUCKETS):
            val = jnp.full(bk.shape, tab_ref[bkt, kv * GROUP], F32)
            for gi in range(1, GROUP):
                val = jnp.where(row >= gi * seq, tab_ref[bkt, kv * GROUP + gi], val)
            acc = jnp.where(bk == bkt, val, acc)
        os_ref[kv] = acc


def _t5_bucket(rel):
    n = jnp.maximum(rel, 0)
    max_exact = NUM_BUCKETS // 2
    nf = jnp.maximum(n, max_exact).astype(F32)
    large = max_exact + (jnp.log(nf / max_exact) / math.log(MAX_DISTANCE / max_exact)
                         * (NUM_BUCKETS - max_exact)).astype(jnp.int32)
    large = jnp.minimum(large, NUM_BUCKETS - 1)
    return jnp.where(n < max_exact, n, large)


def _bias_tables(rel_bias, dec_seq, w_buf):
    qi = jnp.arange(ATTN_BLOCK, dtype=jnp.int32)[:, None]
    kc = jnp.arange(2 * ATTN_BLOCK, dtype=jnp.int32)[None, :]
    rel_p = qi + ATTN_BLOCK - kc
    bucket_p = jnp.where((rel_p >= 0) & (rel_p <= WINDOW), _t5_bucket(rel_p), -1)
    ts = (jnp.arange(GROUP * dec_seq, dtype=jnp.int32) % dec_seq)[:, None]
    js = jnp.arange(2 * ATTN_BLOCK, dtype=jnp.int32)[None, :]
    rel_s = w_buf + ts - js
    ok_s = (rel_s >= 0) & (rel_s <= WINDOW) & (js < w_buf + dec_seq)
    bucket_s = jnp.where(ok_s, _t5_bucket(rel_s), -1)
    return pl.pallas_call(
        functools.partial(_bias_kernel, seq=dec_seq),
        out_shape=(jax.ShapeDtypeStruct((H_ATTN, ATTN_BLOCK, 2 * ATTN_BLOCK), F32),
                   jax.ShapeDtypeStruct((KV_HEADS, GROUP * dec_seq, 2 * ATTN_BLOCK), F32)),
        in_specs=[pl.BlockSpec(memory_space=pltpu.SMEM),
                  pl.BlockSpec(memory_space=pltpu.VMEM),
                  pl.BlockSpec(memory_space=pltpu.VMEM)],
        out_specs=(pl.BlockSpec(memory_space=pltpu.VMEM), pl.BlockSpec(memory_space=pltpu.VMEM)),
        name="rel_bias_tables",
    )(rel_bias.astype(F32), bucket_p, bucket_s)


def _mod_kernel(ca_ref, cb_ref, w_ref, b_ref, o_ref):
    w = w_ref[...].astype(BF16)
    na = ca_ref.shape[0]
    for c_ref, rows in ((ca_ref, slice(0, na)), (cb_ref, slice(na, na + cb_ref.shape[0]))):
        o_ref[rows, :] = jnp.dot(_silu(c_ref[...]).astype(BF16), w, preferred_element_type=F32) + b_ref[...]


def _modulation(c_a, c_b, w_ada, b_ada):
    depth = w_ada.shape[0]
    n = c_a.shape[0] + c_b.shape[0]
    return pl.pallas_call(
        _mod_kernel,
        out_shape=jax.ShapeDtypeStruct((depth, n, N_MOD * D_MODEL), F32),
        grid=(depth, N_MOD * D_MODEL // MOD_COLS),
        in_specs=[pl.BlockSpec(c_a.shape, lambda l, j: (0, 0)),
                  pl.BlockSpec(c_b.shape, lambda l, j: (0, 0)),
                  pl.BlockSpec((None, D_MODEL, MOD_COLS), lambda l, j: (l, 0, j)),
                  pl.BlockSpec((None, 1, MOD_COLS), lambda l, j: (l, 0, j))],
        out_specs=pl.BlockSpec((None, n, MOD_COLS), lambda l, j: (l, 0, j)),
        compiler_params=pltpu.CompilerParams(dimension_semantics=("arbitrary", "arbitrary"),
                                             vmem_limit_bytes=VMEM_LIMIT),
        name="adaln_modulation",
    )(c_a, c_b, w_ada, b_ada.reshape(depth, 1, N_MOD * D_MODEL))


def _mod_rows(m, n_tokens):
    if m.shape[0] == 1:
        return m
    reps = n_tokens // m.shape[0]
    return jnp.broadcast_to(m[:, None, :], (m.shape[0], reps, m.shape[1])).reshape(n_tokens, m.shape[1])


def _modulated_norm(x, g, sc, sh):
    n = x.shape[0]
    return (_rms_rows(x) * (g * (1.0 + _mod_rows(sc, n))) + _mod_rows(sh, n)).astype(BF16)


def _sample_mod_specs(n_rows, chunks, index_map):
    return [pl.BlockSpec((None, n_rows, D_MODEL), functools.partial(index_map, chunk=c),
                         pipeline_mode=pl.Buffered(1)) for c in chunks]


def _mlp_tile(mix, x, g1, sh, sc, g2, ng, wout_ref, wup_ref, wdn_ref, fg, final):
    n = x.shape[0]
    x1 = x + _mod_rows(g1, n) * jnp.dot(mix.astype(BF16), wout_ref[...], preferred_element_type=F32)
    h = _modulated_norm(x1, ng, sc, sh)
    acc = None
    for c in range(D_FF // FF_CHUNK):
        u = jnp.dot(h, wup_ref[:, c * FF_CHUNK:(c + 1) * FF_CHUNK], preferred_element_type=F32)
        u = jnp.square(jnp.maximum(u, 0.0)).astype(BF16)
        d = jnp.dot(u, wdn_ref[c * FF_CHUNK:(c + 1) * FF_CHUNK, :], preferred_element_type=F32)
        acc = d if acc is None else acc + d
    x2 = x1 + _mod_rows(g2, n) * acc
    return _rms_rows(x2) * fg if final else x2


def _mlp_kernel(mix_ref, x_ref, g1_ref, sh_ref, sc_ref, g2_ref, smix_ref, sx_ref, sg1_ref, ssh_ref, ssc_ref, sg2_ref,
                ng_ref, wout_ref, wup_ref, wdn_ref, fg_ref, o_ref, so_ref, wout_s, wup_s, wdn_s, *, layer, final):
    s = pl.program_id(0)

    @pl.when(s < CAST_STEPS)
    def _():
        for src, dst in ((wout_ref, wout_s), (wup_ref, wup_s), (wdn_ref, wdn_s)):
            rows = src.shape[0]
            dst[pl.ds(pl.multiple_of(s * rows, rows), rows), :] = src[...].astype(BF16)

    weights = (ng_ref[layer:layer + 1, :], wout_s, wup_s, wdn_s, fg_ref[...], final)

    @pl.when(s >= CAST_STEPS)
    def _():
        o_ref[...] = _mlp_tile(mix_ref[...], x_ref[...], g1_ref[...], sh_ref[...], sc_ref[...], g2_ref[...], *weights)

    @pl.when(s == pl.num_programs(0) - 1)
    def _():
        sx = sx_ref[...].reshape(-1, D_MODEL)
        so_ref[...] = _mlp_tile(smix_ref[...], sx, sg1_ref[...], ssh_ref[...], ssc_ref[...], sg2_ref[...],
                                *weights).reshape(so_ref.shape)


def _out_mlp(mix2, x2, mod, mix_s, xs, xs_out_shape, mod_all, n_sample, norm_g, w_out, w_up, w_down, final_g, layer,
             tiles_per_batch, final):
    n = x2.shape[0]
    tile = TOK_TILE
    const = lambda i: (0, 0)
    tok = lambda i: (jnp.maximum(i - CAST_STEPS, 0), 0)
    chunk_of_layer = lambda i: (layer, jnp.minimum(i, CAST_STEPS - 1), 0)
    mod_p = lambda chunk: pl.BlockSpec((None, 1, D_MODEL),
                                       lambda i: (jnp.maximum(i - CAST_STEPS, 0) // tiles_per_batch, 0, chunk))
    whole = lambda shape: pl.BlockSpec(shape, lambda i: (0,) * len(shape), pipeline_mode=pl.Buffered(1))
    return pl.pallas_call(
        functools.partial(_mlp_kernel, layer=layer, final=final),
        out_shape=(jax.ShapeDtypeStruct((n, D_MODEL), F32), jax.ShapeDtypeStruct(xs_out_shape, F32)),
        grid=(CAST_STEPS + n // tile,),
        in_specs=[pl.BlockSpec((tile, D_MODEL), tok),
                  pl.BlockSpec((tile, D_MODEL), tok),
                  mod_p(2), mod_p(3), mod_p(4), mod_p(5),
                  whole(mix_s.shape), whole(xs.shape)]
                 + _sample_mod_specs(n_sample, (2, 3, 4, 5), lambda i, chunk: (layer, 0, chunk)) + [
                  pl.BlockSpec(norm_g.shape, const),
                  pl.BlockSpec((None, D_MODEL // CAST_STEPS, D_MODEL), chunk_of_layer),
                  pl.BlockSpec((None, D_MODEL // CAST_STEPS, D_FF), chunk_of_layer),
                  pl.BlockSpec((None, D_FF // CAST_STEPS, D_MODEL), chunk_of_layer),
                  pl.BlockSpec((1, D_MODEL), const)],
        out_specs=(pl.BlockSpec((tile, D_MODEL), tok), whole(xs_out_shape)),
        scratch_shapes=[pltpu.VMEM((D_MODEL, D_MODEL), BF16),
                        pltpu.VMEM((D_MODEL, D_FF), BF16),
                        pltpu.VMEM((D_FF, D_MODEL), BF16)],
        compiler_params=pltpu.CompilerParams(dimension_semantics=("arbitrary",), vmem_limit_bytes=VMEM_LIMIT),
        name="out_projection_mlp",
    )(mix2, x2, mod, mod, mod, mod, mix_s, xs, mod_all, mod_all, mod_all, mod_all, norm_g, w_out,
      w_up, w_down, final_g.reshape(1, D_MODEL))


def _conv_ln_swish(acc, lng, lnb):
    mu = jnp.mean(acc, axis=-1, keepdims=True)
    xc = acc - mu
    y = xc * lax.rsqrt(jnp.mean(xc * xc, axis=-1, keepdims=True) + EPS) * lng + lnb
    return _silu(y)


def _hgrn_gates(fh, lb):
    f = lb + (1.0 - lb) * jax.nn.sigmoid(fh)
    return jnp.log(f), 1.0 - f


def _hgrn_out(o, hng, gate):
    return _rms_rows(o) * hng * _silu(gate)


def _ticker(pieces):
    it = iter(pieces)

    def tick():
        piece = next(it, None)
        if piece is not None:
            piece()

    def flush():
        for piece in it:
            piece()

    tick.flush = flush
    return tick


def _prompt_mix_kernel(sinks_ref, x_ref, sh_ref, sc_ref, sx_ref, ssh_ref, ssc_ref, ng_ref, win_ref, convw_ref,
                       convb_ref, lng_ref, lnb_ref, hlb_ref, hng_ref, bias_ref,
                       mix_ref, sproj_ref, convo_ref, so_ref, ko_ref, vo_ref,
                       proj_ref, wbf_ref, abuf, kbuf, vbuf, st_ref, *, layer, tile):
    t = pl.program_id(1)
    last = pl.num_programs(1) - 1

    @pl.when((pl.program_id(0) == 0) & (t == 0))
    def _():
        wbf_ref[...] = win_ref[...].astype(BF16)

    @pl.when(t == 0)
    def _():
        abuf[0:CONV_PAD, :] = jnp.zeros((CONV_PAD, D_CONV), F32)
        abuf[CONV_PAD + tile:CONV_PAD + tile + SUBLANES, :] = jnp.zeros((SUBLANES, D_CONV), F32)
        kbuf[0:ATTN_BLOCK, :] = jnp.zeros((ATTN_BLOCK, D_KV), BF16)
        vbuf[0:ATTN_BLOCK, :] = jnp.zeros((ATTN_BLOCK, D_KV), BF16)
        st_ref[...] = jnp.zeros(st_ref.shape, F32)

    h_in = _modulated_norm(x_ref[...], ng_ref[layer:layer + 1, :], sc_ref[...], sh_ref[...])
    for lo, hi in ((OFF_AVAL, OFF_Q), (OFF_F, OFF_I), (OFF_Q, OFF_F), (OFF_I, OFF_G), (OFF_G, OFF_QA),
                   (OFF_QA, IN_WIDTH)):
        proj_ref[:, lo:hi] = jnp.dot(h_in, wbf_ref[:, lo:hi], preferred_element_type=F32)

    kbuf[ATTN_BLOCK:ATTN_BLOCK + tile, :] = proj_ref[:, OFF_KA:OFF_KA + D_KV].astype(BF16)
    vbuf[ATTN_BLOCK:ATTN_BLOCK + tile, :] = proj_ref[:, OFF_VA:OFF_VA + D_KV].astype(BF16)
    scale = HEAD_DIM ** -0.5
    attn = {}

    def attn_scores(blk):
        def run():
            r0 = blk * ATTN_BLOCK
            for h in range(H_ATTN):
                kv = h // GROUP
                q = (proj_ref[r0:r0 + ATTN_BLOCK, OFF_QA + h * HEAD_DIM:OFF_QA + (h + 1) * HEAD_DIM]
                     * scale).astype(BF16)
                kall = kbuf[r0:r0 + 2 * ATTN_BLOCK, kv * HEAD_DIM:(kv + 1) * HEAD_DIM]
                attn[blk, h] = lax.dot_general(q, kall, NT_DIMS, preferred_element_type=F32)
        return run

    def attn_softmax(blk):
        def run():
            for h in range(H_ATTN):
                s = attn[blk, h] + bias_ref[h]
                if blk == 0:
                    col = lax.broadcasted_iota(jnp.int32, s.shape, 1)
                    s = jnp.where(col + (t * tile - ATTN_BLOCK) >= 0, s, -jnp.inf)
                p, den = _sink_softmax(s, sinks_ref[layer, h])
                attn[blk, h] = (p.astype(BF16), den)
        return run

    def attn_values(blk):
        def run():
            r0 = blk * ATTN_BLOCK
            heads, dens = [], []
            for h in range(H_ATTN):
                kv = h // GROUP
                p, den = attn[blk, h]
                vall = vbuf[r0:r0 + 2 * ATTN_BLOCK, kv * HEAD_DIM:(kv + 1) * HEAD_DIM]
                heads.append(jnp.dot(p, vall, preferred_element_type=F32))
                dens.append(jnp.broadcast_to(den, (ATTN_BLOCK, HEAD_DIM)))
            out = jnp.concatenate(heads, axis=1) / jnp.concatenate(dens, axis=1)
            mix_ref[r0:r0 + ATTN_BLOCK, D_CONV + D_HGRN:D_MODEL] = out.astype(BF16)
        return run

    tick = _ticker([stage(blk) for blk in range(tile // ATTN_BLOCK)
                    for stage in (attn_scores, attn_softmax, attn_values)])

    abuf[CONV_PAD:CONV_PAD + tile, :] = (proj_ref[:, OFF_AVAL:OFF_AVAL + D_CONV]
                                         * jax.nn.sigmoid(proj_ref[:, OFF_AGATE:OFF_AGATE + D_CONV]))
    first_row = CONV_PAD - (CONV_WIDTH - 1)
    acc = jnp.broadcast_to(convb_ref[layer:layer + 1, :], (tile, D_CONV))
    for r in range(SUBLANES):
        z = None
        for off in range(r, first_row + CONV_WIDTH, SUBLANES):
            j = off - first_row
            if j < 0:
                continue
            term = convw_ref[layer, j:j + 1, :] * abuf[off - r:off - r + tile + SUBLANES, :]
            z = term if z is None else z + term
        acc = acc + (z[0:tile] if r == 0 else pltpu.roll(z, tile + SUBLANES - r, 0)[0:tile])
    mix_ref[:, 0:D_CONV] = _conv_ln_swish(acc, lng_ref[layer:layer + 1, :], lnb_ref[layer:layer + 1, :]).astype(BF16)
    tick()

    lb = _layer_lb(hlb_ref[...], layer)
    hng = hng_ref[layer:layer + 1, :]
    span = HGRN_SPAN * HGRN_CHUNK
    ri = lax.broadcasted_iota(jnp.int32, (span, span), 0)
    ci = lax.broadcasted_iota(jnp.int32, (span, span), 1)
    tri = jnp.where((ri >= ci) & (ri // HGRN_CHUNK == ci // HGRN_CHUNK), 1.0, 0.0).astype(BF16)
    for i in range(tile // span):
        _hgrn_span(proj_ref, mix_ref, st_ref, i * span, lb, hng, tri, tick)
    tick.flush()

    @pl.when(t == last)
    def _():
        convo_ref[...] = abuf[CONV_PAD + tile - (CONV_WIDTH - 1):CONV_PAD + tile, :]
        for h in range(H_HGRN):
            so_ref[h] = st_ref[h].T
        ko_ref[...] = proj_ref[tile - WINDOW:tile, OFF_KA:OFF_KA + D_KV].T
        vo_ref[...] = proj_ref[tile - WINDOW:tile, OFF_VA:OFF_VA + D_KV].T

    abuf[0:CONV_PAD, :] = abuf[tile:tile + CONV_PAD, :]
    kbuf[0:ATTN_BLOCK, :] = kbuf[tile:tile + ATTN_BLOCK, :]
    vbuf[0:ATTN_BLOCK, :] = vbuf[tile:tile + ATTN_BLOCK, :]

    @pl.when((pl.program_id(0) == pl.num_programs(0) - 1) & (t == last))
    def _():
        h_s = _modulated_norm(sx_ref[...].reshape(-1, D_MODEL), ng_ref[layer:layer + 1, :], ssc_ref[...], ssh_ref[...])
        sproj_ref[...] = jnp.dot(h_s, wbf_ref[...], preferred_element_type=F32)


def _carry_specs(carried):
    return [pl.BlockSpec(memory_space=pl.ANY)] * len(carried)


def _without_carry(kernel_fn, n_in, n_carried, *refs):
    return kernel_fn(*refs[:n_in], *refs[n_in + n_carried:])


def _carried(kernel_fn, n_in, n_carried):
    return functools.partial(_without_carry, kernel_fn, n_in, n_carried)


def _prompt_mixers(x, mod, xs, mod_all, n_sample, norm_g, w_in, sinks, conv_w, conv_b, ln_g, ln_b, hgrn_lb, hng,
                   bias_p, layer, carried):
    B, T = x.shape[:2]
    tile = MIX_TILE
    depth = hgrn_lb.shape[0]
    n_tok_s = math.prod(xs.shape[:-1])
    const2 = lambda b, t: (0, 0)
    inputs = (sinks, x, mod, mod, xs, mod_all, mod_all, norm_g, w_in, conv_w, conv_b, ln_g, ln_b, hgrn_lb, hng, bias_p)
    return pl.pallas_call(
        _carried(functools.partial(_prompt_mix_kernel, layer=layer, tile=tile), len(inputs), len(carried)),
        out_shape=(jax.ShapeDtypeStruct((B, T, D_MODEL), BF16),
                   jax.ShapeDtypeStruct((n_tok_s, IN_WIDTH), F32),
                   jax.ShapeDtypeStruct((depth, B, CONV_WIDTH - 1, D_CONV), F32),
                   jax.ShapeDtypeStruct((depth, B, H_HGRN, DK_HGRN, DV_HGRN), F32),
                   jax.ShapeDtypeStruct((depth, B, D_KV, WINDOW), F32),
                   jax.ShapeDtypeStruct((depth, B, D_KV, WINDOW), F32)),
        grid=(B, T // tile),
        in_specs=[pl.BlockSpec(memory_space=pltpu.SMEM),
                  pl.BlockSpec((None, tile, D_MODEL), lambda b, t: (b, t, 0)),
                  pl.BlockSpec((None, 1, D_MODEL), lambda b, t: (b, 0, 0)),
                  pl.BlockSpec((None, 1, D_MODEL), lambda b, t: (b, 0, 1)),
                  pl.BlockSpec(xs.shape, lambda b, t: (0,) * xs.ndim, pipeline_mode=pl.Buffered(1))]
                 + _sample_mod_specs(n_sample, (0, 1), lambda b, t, chunk: (layer, 0, chunk)) + [
                  pl.BlockSpec(norm_g.shape, const2),
                  pl.BlockSpec((None, D_MODEL, IN_WIDTH), lambda b, t: (layer, 0, 0), pipeline_mode=pl.Buffered(1)),
                  pl.BlockSpec(conv_w.shape, lambda b, t: (0, 0, 0)),
                  pl.BlockSpec(conv_b.shape, const2),
                  pl.BlockSpec(ln_g.shape, const2),
                  pl.BlockSpec(ln_b.shape, const2),
                  pl.BlockSpec(hgrn_lb.shape, const2),
                  pl.BlockSpec(hng.shape, const2),
                  pl.BlockSpec((H_ATTN, ATTN_BLOCK, 2 * ATTN_BLOCK), lambda b, t: (0, 0, 0))] + _carry_specs(carried),
        out_specs=(pl.BlockSpec((None, tile, D_MODEL), lambda b, t: (b, t, 0)),
                   pl.BlockSpec((n_tok_s, IN_WIDTH), const2),
                   pl.BlockSpec((None, None, CONV_WIDTH - 1, D_CONV), lambda b, t: (layer, b, 0, 0)),
                   pl.BlockSpec((None, None, H_HGRN, DK_HGRN, DV_HGRN), lambda b, t: (layer, b, 0, 0, 0)),
                   pl.BlockSpec((None, None, D_KV, WINDOW), lambda b, t: (layer, b, 0, 0)),
                   pl.BlockSpec((None, None, D_KV, WINDOW), lambda b, t: (layer, b, 0, 0))),
        input_output_aliases={len(inputs) + i: 2 + i for i in range(len(carried))},
        scratch_shapes=[pltpu.VMEM((tile, IN_WIDTH), F32),
                        pltpu.VMEM((D_MODEL, IN_WIDTH), BF16),
                        pltpu.VMEM((CONV_PAD + tile + SUBLANES, D_CONV), F32),
                        pltpu.VMEM((ATTN_BLOCK + tile, D_KV), BF16),
                        pltpu.VMEM((ATTN_BLOCK + tile, D_KV), BF16),
                        pltpu.VMEM((H_HGRN, DV_HGRN, DK_HGRN), F32)],
        compiler_params=pltpu.CompilerParams(dimension_semantics=("arbitrary", "arbitrary"),
                                             vmem_limit_bytes=VMEM_LIMIT),
        name="prompt_mixers",
    )(*inputs, *carried)


def _sample_mix_kernel(sinks_ref, proj_ref, cconv_ref, state_ref, ck_ref, cv_ref, convw_ref, convb_ref, lng_ref,
                       lnb_ref, hlb_ref, hng_ref, bias_ref,
                       mix_ref, convo_ref, so_ref, ko_ref, vo_ref, kpad_ref, vpad_ref, *,
                       layer, block, seq, w_buf):
    hist = CONV_WIDTH - 1

    @pl.when(pl.program_id(0) == 0)
    def _():
        for ref in (kpad_ref, vpad_ref):
            ref[:, 0:w_buf - seq, :] = jnp.zeros((block, w_buf - seq, D_KV), F32)

    lb = _layer_lb(hlb_ref[...], layer)
    hng = hng_ref[layer:layer + 1, :]
    scale = HEAD_DIM ** -0.5
    elems = range(block)
    row8 = lax.broadcasted_iota(jnp.int32, (SUBLANES, DV_HGRN), 0)
    ones_rows = jnp.where((row8 >= seq) & (row8 < seq + 3), 1.0, 0.0)
    zrow = jnp.zeros((1, DK_HGRN), BF16)
    prow = lax.broadcasted_iota(jnp.int32, (seq, seq), 0)
    pcol = lax.broadcasted_iota(jnp.int32, (seq, seq), 1)
    grow = lax.broadcasted_iota(jnp.int32, (GROUP * seq, 1), 0)

    proj = [proj_ref[e * seq:(e + 1) * seq, :] for e in elems]

    glu = [p[:, OFF_AVAL:OFF_AVAL + D_CONV] * jax.nn.sigmoid(p[:, OFF_AGATE:OFF_AGATE + D_CONV]) for p in proj]
    full = [cconv_ref[i] for i in range(hist)]
    full += [jnp.concatenate([glu[e][t:t + 1] for e in elems], axis=0) for t in range(seq)]
    for i in range(hist):
        convo_ref[i] = full[i + seq]
    conv_out = []
    for t in range(seq):
        acc = jnp.broadcast_to(convb_ref[layer:layer + 1, :], (block, D_CONV))
        for j in range(CONV_WIDTH):
            acc = acc + convw_ref[layer, j:j + 1, :] * full[t + j]
        conv_out.append(_conv_ln_swish(acc, lng_ref[layer:layer + 1, :], lnb_ref[layer:layer + 1, :]))
    out_a = [jnp.concatenate([conv_out[t][e:e + 1] for t in range(seq)], axis=0) for e in elems]

    units = [(e, h) for e in elems for h in range(H_HGRN)]
    ops = {}
    for e in elems:
        p = proj[e]
        g, k = _hgrn_gates(p[:, OFF_F:OFF_F + D_HGRN], lb)
        b = _cumsum_rows_small(g)
        for h in range(H_HGRN):
            cs = slice(h * DK_HGRN, (h + 1) * DK_HGRN)
            q = p[:, OFF_Q + h * DK_HGRN:OFF_Q + (h + 1) * DK_HGRN]
            v = p[:, OFF_I + h * DV_HGRN:OFF_I + (h + 1) * DV_HGRN]
            bb = b[:, cs]
            bl = bb[seq - 1:seq, :]
            kst = (k[:, cs] * jnp.exp(bl - bb)).astype(BF16)
            x = jnp.concatenate([kst.astype(F32), *(part.astype(F32) for part in _split3_bf16(jnp.exp(bl))),
                                 zrow.astype(F32)], axis=0).astype(BF16)
            vpad = jnp.concatenate([v, jnp.zeros((SUBLANES - seq, DV_HGRN), F32)], axis=0)
            ops[e, h] = dict(qp=(q * jnp.exp(bb - bl)).astype(BF16), kst=kst, v=v.astype(BF16),
                             qt=(q * jnp.exp(bb)).astype(BF16), x=x,
                             r=jnp.concatenate([vpad, ones_rows], axis=1).astype(BF16))
    for u in units:
        o = ops[u]
        st = state_ref[u[0], u[1]]
        o["p"] = lax.dot_general(o["qp"], o["kst"], NT_DIMS, preferred_element_type=F32)
        o["inter"] = jnp.dot(o["qt"], st.astype(BF16), preferred_element_type=F32)
        me = lax.dot_general(o["x"], o["r"], TN_DIMS, preferred_element_type=F32)
        so_ref[u[0], u[1]] = me[:, DV_HGRN:] * st + me[:, :DV_HGRN]
    out_b = {}
    for u in units:
        o = ops[u]
        pm = jnp.where(prow >= pcol, o["p"], 0.0).astype(BF16)
        out = o["inter"] + jnp.dot(pm, o["v"], preferred_element_type=F32)
        gate = proj[u[0]][:, OFF_G + u[1] * DV_HGRN:OFF_G + (u[1] + 1) * DV_HGRN]
        out_b[u] = _hgrn_out(out, hng, gate)

    scores = {}
    for e in elems:
        p = proj[e]
        for kv in range(KV_HEADS):
            hs = slice(kv * HEAD_DIM, (kv + 1) * HEAD_DIM)
            q2 = jnp.concatenate([p[:, OFF_QA + h * HEAD_DIM:OFF_QA + (h + 1) * HEAD_DIM]
                                  for h in range(kv * GROUP, (kv + 1) * GROUP)], axis=0)
            q2 = (q2 * scale).astype(BF16)
            bias = bias_ref[kv]
            s_c = jnp.dot(q2, ck_ref[e, hs, :].astype(BF16), preferred_element_type=F32) + bias[:, 0:w_buf]
            s_n = (lax.dot_general(q2, p[:, OFF_KA + kv * HEAD_DIM:OFF_KA + (kv + 1) * HEAD_DIM].astype(BF16), NT_DIMS,
                                   preferred_element_type=F32) + bias[:, w_buf:w_buf + seq])
            scores[e, kv] = (s_c, s_n)
    out_c = {}
    for e in elems:
        p = proj[e]
        for kv in range(KV_HEADS):
            hs = slice(kv * HEAD_DIM, (kv + 1) * HEAD_DIM)
            sink = jnp.zeros((GROUP * seq, 1), F32)
            for gi in range(GROUP):
                sink = jnp.where(grow >= gi * seq, sinks_ref[layer, kv * GROUP + gi], sink)
            s_c, s_n = scores[e, kv]
            m = jnp.maximum(jnp.maximum(jnp.max(s_c, axis=-1, keepdims=True), jnp.max(s_n, axis=-1, keepdims=True)),
                            sink)
            p_c = jnp.exp(s_c - m)
            p_n = jnp.exp(s_n - m)
            den = jnp.sum(p_c, axis=-1, keepdims=True) + jnp.sum(p_n, axis=-1, keepdims=True) + jnp.exp(sink - m)
            vnew = p[:, OFF_VA + kv * HEAD_DIM:OFF_VA + (kv + 1) * HEAD_DIM].astype(BF16)
            o2 = (lax.dot_general(p_c.astype(BF16), cv_ref[e, hs, :].astype(BF16), NT_DIMS,
                                  preferred_element_type=F32)
                  + jnp.dot(p_n.astype(BF16), vnew, preferred_element_type=F32)) / den
            for gi in range(GROUP):
                out_c[e, kv * GROUP + gi] = o2[gi * seq:(gi + 1) * seq]
    lane = lax.broadcasted_iota(jnp.int32, (D_KV, w_buf), 1)
    for e in elems:
        p = proj[e]
        for pad, cache, new, out in ((kpad_ref, ck_ref, p[:, OFF_KA:OFF_KA + D_KV], ko_ref),
                                     (vpad_ref, cv_ref, p[:, OFF_VA:OFF_VA + D_KV], vo_ref)):
            pad[e, w_buf - seq:w_buf, :] = new
            out[e] = jnp.where(lane >= w_buf - seq, pad[e].T, pltpu.roll(cache[e], w_buf - seq, 1))

    for e in elems:
        parts = [out_a[e]] + [out_b[e, h] for h in range(H_HGRN)] + [out_c[e, h] for h in range(H_ATTN)]
        mix_ref[e * seq:(e + 1) * seq, :] = jnp.concatenate(parts, axis=1)


def _sample_mixers(proj2, sinks, cache_conv, state, cache_k, cache_v, conv_w, conv_b, ln_g, ln_b, hgrn_lb, hng,
                   bias_s, layer, carried):
    B = state.shape[1]
    seq = proj2.shape[0] // B
    w_buf = cache_k.shape[3]
    block = SAMPLE_BLOCK
    depth = hgrn_lb.shape[0]
    hist = CONV_WIDTH - 1
    const2 = lambda i: (0, 0)
    cache_specs = [pl.BlockSpec((None, hist, block, D_CONV), lambda i: (layer, 0, i, 0)),
                   pl.BlockSpec((None, block, H_HGRN, DK_HGRN, DV_HGRN), lambda i: (layer, i, 0, 0, 0)),
                   pl.BlockSpec((None, block, D_KV, w_buf), lambda i: (layer, i, 0, 0)),
                   pl.BlockSpec((None, block, D_KV, w_buf), lambda i: (layer, i, 0, 0))]
    inputs = (sinks, proj2, cache_conv, state, cache_k, cache_v, conv_w, conv_b, ln_g, ln_b, hgrn_lb, hng, bias_s)
    return pl.pallas_call(
        _carried(functools.partial(_sample_mix_kernel, layer=layer, block=block, seq=seq, w_buf=w_buf),
                 len(inputs), len(carried)),
        out_shape=(jax.ShapeDtypeStruct((B * seq, D_MODEL), F32),
                   jax.ShapeDtypeStruct((depth, hist, B, D_CONV), F32),
                   jax.ShapeDtypeStruct((depth, B, H_HGRN, DK_HGRN, DV_HGRN), F32),
                   jax.ShapeDtypeStruct((depth, B, D_KV, w_buf), F32),
                   jax.ShapeDtypeStruct((depth, B, D_KV, w_buf), F32)),
        grid=(B // block,),
        in_specs=[pl.BlockSpec(memory_space=pltpu.SMEM),
                  pl.BlockSpec((block * seq, IN_WIDTH), lambda i: (i, 0))] + cache_specs + [
                  pl.BlockSpec(conv_w.shape, lambda i: (0, 0, 0)),
                  pl.BlockSpec(conv_b.shape, const2),
                  pl.BlockSpec(ln_g.shape, const2),
                  pl.BlockSpec(ln_b.shape, const2),
                  pl.BlockSpec(hgrn_lb.shape, const2),
                  pl.BlockSpec(hng.shape, const2),
                  pl.BlockSpec((KV_HEADS, GROUP * seq, 2 * ATTN_BLOCK), lambda i: (0, 0, 0))] + _carry_specs(carried),
        out_specs=tuple([pl.BlockSpec((block * seq, D_MODEL), lambda i: (i, 0))] + cache_specs),
        input_output_aliases={len(inputs) + i: 1 + i for i in range(len(carried))},
        scratch_shapes=[pltpu.VMEM((block, w_buf, D_KV), F32),
                        pltpu.VMEM((block, w_buf, D_KV), F32)],
        compiler_params=pltpu.CompilerParams(dimension_semantics=("arbitrary",), vmem_limit_bytes=VMEM_LIMIT),
        name="sample_mixers",
    )(*inputs, *carried)


def kernel(x_prompt, x_sample, cache_conv, state_hgrn, cache_swa_k, cache_swa_v, c_prompt, c_sample, rel_bias, w_ada, b_ada, norm_mix_g, w_in, conv_w, conv_b, conv_ln_g, conv_ln_b, hgrn_lb, hgrn_norm_g, attn_sinks, w_out, norm_mlp_g, w_up, w_down, final_g):
    Bp, Tp = x_prompt.shape[:2]
    Bs, Ts = x_sample.shape[:2]
    depth = w_in.shape[0]
    w_buf = cache_swa_k.shape[2]
    assert Tp % MIX_TILE == 0 and (Bp * Tp) % TOK_TILE == 0 and Tp % TOK_TILE == 0 and Bs % SAMPLE_BLOCK == 0
    assert w_buf == WINDOW and GROUP * Ts == SUBLANES

    bias_p, bias_s = _bias_tables(rel_bias, Ts, w_buf)
    mod = _modulation(c_sample, c_prompt, w_ada, b_ada)
    hlb = hgrn_lb.astype(F32)
    cc = jnp.swapaxes(cache_conv, 1, 2)
    ck = jnp.swapaxes(cache_swa_k.reshape(depth, Bs, w_buf, D_KV), 2, 3)
    cv = jnp.swapaxes(cache_swa_v.reshape(depth, Bs, w_buf, D_KV), 2, 3)

    xp = x_prompt.reshape(Bp * Tp, D_MODEL)
    xs = x_sample
    caches_p = ()
    caches_s = ()
    for l in range(depth):
        final = l == depth - 1
        mod_p = mod[l, Bs:].reshape(Bp, 1, N_MOD * D_MODEL)
        mix_p, proj_s, *caches_p = _prompt_mixers(xp.reshape(Bp, Tp, D_MODEL), mod_p, xs, mod, Bs, norm_mix_g, w_in,
                                                  attn_sinks, conv_w, conv_b, conv_ln_g, conv_ln_b, hlb, hgrn_norm_g,
                                                  bias_p, l, caches_p)
        mix_s, *caches_s = _sample_mixers(proj_s, attn_sinks, cc, state_hgrn, ck, cv, conv_w, conv_b, conv_ln_g,
                                          conv_ln_b, hlb, hgrn_norm_g, bias_s, l, caches_s)
        xs_out_shape = x_sample.shape if final else (Bs * Ts, D_MODEL)
        xp, xs = _out_mlp(mix_p.reshape(Bp * Tp, D_MODEL), xp, mod_p, mix_s, xs, xs_out_shape, mod, Bs, norm_mlp_g, w_out,
                          w_up, w_down, final_g, l, Tp // TOK_TILE, final)
    cp, sp, kp, vp = caches_p
    kp, vp = jnp.swapaxes(kp, 2, 3), jnp.swapaxes(vp, 2, 3)
    cs, ss, ksn, vsn = caches_s
    cs, ksn, vsn = jnp.swapaxes(cs, 1, 2), jnp.swapaxes(ksn, 2, 3), jnp.swapaxes(vsn, 2, 3)
    return (xp.reshape(Bp, Tp, D_MODEL), xs, cp, cs, sp, ss,
            kp.reshape(depth, Bp, WINDOW, KV_HEADS, HEAD_DIM), ksn.reshape(depth, Bs, w_buf, KV_HEADS, HEAD_DIM),
            vp.reshape(depth, Bp, WINDOW, KV_HEADS, HEAD_DIM), vsn.reshape(depth, Bs, w_buf, KV_HEADS, HEAD_DIM))
```
